```python
import jax
import jax.numpy as jnp
from jax import lax
import numpy as np


D_MODEL = 1024
BATCH = 8
SEQ = 4096
DEPTH = 1

GRID_W = 64
CTX_LEN = 256
MIX_W = D_MODEL
HEAD_DIM = 64
ATT_HEADS = 8
ATT_KV_HEADS = 2
ATT_GROUP = ATT_HEADS // ATT_KV_HEADS
ATT_W = ATT_HEADS * HEAD_DIM
KV_W = ATT_KV_HEADS * HEAD_DIM
Q_BLOCK = 128
ROPE_THETA = 10000.0
ROPE_HALF = HEAD_DIM // 2
ROPE_PAIRS_AXIS = HEAD_DIM // 4
GM_GROUPS = 8
GM_W = MIX_W - ATT_W
GM_HD = GM_W // GM_GROUPS
CHUNK = 128
IN_W = ATT_W + 2 * KV_W + 2 * GM_W
N_EXPERTS = 64
TOP_K = 6
EXPERT_H = 256
SHARED_H = 256
ROUTED_SCALE = 2.5
EPS = 1e-6
N_MOD = 6

kernel_name = "hybrid_attn_gmlp_moe_dit_layer"


def rms_norm(x, g):
    xf = x.astype(jnp.float32)
    y = xf * lax.rsqrt(jnp.mean(xf * xf, axis=-1, keepdims=True) + EPS)
    return (y * g.astype(jnp.float32)).astype(x.dtype)


def modulate(x, shift, scale):
    return x * (1 + scale) + shift


def ada_params(cvec, w_ada, b_ada):
    mod = jax.nn.silu(cvec) @ w_ada + b_ada
    return jnp.split(mod, N_MOD, axis=-1)


def axial_rope_tables(length):
    rows = length // GRID_W
    r = jnp.repeat(jnp.arange(rows, dtype=jnp.float32), GRID_W)
    col = jnp.tile(jnp.arange(GRID_W, dtype=jnp.float32), rows)
    inv = ROPE_THETA ** (-jnp.arange(ROPE_PAIRS_AXIS, dtype=jnp.float32) / ROPE_PAIRS_AXIS)
    ang = jnp.concatenate([r[:, None] * inv, col[:, None] * inv], axis=-1)
    return jnp.cos(ang), jnp.sin(ang)


def apply_rope(x, cos, sin):
    length = x.shape[1]
    shp = (1, length) + (1,) * (x.ndim - 3) + (ROPE_HALF,)
    cos = cos.reshape(shp).astype(x.dtype)
    sin = sin.reshape(shp).astype(x.dtype)
    x1, x2 = x[..., :ROPE_HALF], x[..., ROPE_HALF:]
    return jnp.concatenate([x1 * cos - x2 * sin, x1 * sin + x2 * cos], axis=-1)


def split_in_proj(p):
    return jnp.split(p, [ATT_W, ATT_W + KV_W, ATT_W + 2 * KV_W, ATT_W + 2 * KV_W + GM_W], axis=-1)


def latent_attention(q, k, v, k_ctx, v_ctx):
    b, length = q.shape[:2]
    n_blocks = length // Q_BLOCK
    k_all = jnp.concatenate([k_ctx, k], axis=1)
    v_all = jnp.concatenate([v_ctx, v], axis=1)
    scale = HEAD_DIM ** -0.5
    q_blocks = jnp.moveaxis(q.reshape(b, n_blocks, Q_BLOCK, ATT_KV_HEADS, ATT_GROUP, HEAD_DIM), 1, 0)

    def one_block(qb):
        s = jnp.einsum('bqhgd,bkhd->bhgqk', qb, k_all, preferred_element_type=jnp.float32) * scale
        p = jax.nn.softmax(s, axis=-1).astype(v_all.dtype)
        return jnp.einsum('bhgqk,bkhd->bqhgd', p, v_all)

    out = lax.map(one_block, q_blocks)
    return jnp.moveaxis(out, 0, 1).reshape(b, length, ATT_W)


def context_attention(q, k, v):
    b, length = q.shape[:2]
    s = jnp.einsum('bqhgd,bkhd->bhgqk', q, k, preferred_element_type=jnp.float32) * (HEAD_DIM ** -0.5)
    p = jax.nn.softmax(s, axis=-1).astype(v.dtype)
    return jnp.einsum('bhgqk,bkhd->bqhgd', p, v).reshape(b, length, ATT_W)


def chunk_gmlp(u, gt, gm_norm, w_spatial, b_spatial):
    b, length, _ = u.shape
    u = jax.nn.gelu(u)
    gt = rms_norm(jax.nn.gelu(gt).reshape(b, length // CHUNK, CHUNK, GM_GROUPS, GM_HD), gm_norm)
    mixed = jnp.einsum('hpq,bnqhd->bnphd', w_spatial, gt) + b_spatial.T[None, None, :, :, None]
    return u * mixed.reshape(b, length, GM_W)


def merge_groups(att, gm, out_norm_attn, out_norm_gm, w_out):
    y = jnp.concatenate([rms_norm(att, out_norm_attn), rms_norm(gm, out_norm_gm)], axis=-1)
    return y @ w_out


def swiglu(t, w_gu, w_down):
    a, g = jnp.split(t @ w_gu, 2, axis=-1)
    return (jax.nn.silu(a) * g) @ w_down


def moe_ffn(h, w_router, router_bias, w_exp_gu, w_exp_down, w_sh_gu, w_sh_down):
    shape = h.shape
    t = h.reshape(-1, D_MODEL)
    scores = jax.nn.sigmoid((t @ w_router).astype(jnp.float32))
    _, idx = lax.top_k(scores + router_bias.astype(jnp.float32), TOP_K)
    sel = jnp.take_along_axis(scores, idx, axis=-1)
    gates = sel / jnp.sum(sel, axis=-1, keepdims=True) * ROUTED_SCALE
    flat_e = idx.reshape(-1)
    order = jnp.argsort(flat_e)
    tok = order // TOP_K
    group_sizes = jnp.bincount(flat_e, length=N_EXPERTS).astype(jnp.int32)
    xs = t[tok]
    a, g = jnp.split(lax.ragged_dot(xs, w_exp_gu, group_sizes), 2, axis=-1)
    ys = lax.ragged_dot(jax.nn.silu(a) * g, w_exp_down, group_sizes)
    ys = (ys * gates.reshape(-1)[order][:, None]).astype(t.dtype)
    routed = jnp.zeros_like(t).at[tok].add(ys)
    return (routed + swiglu(t, w_sh_gu, w_sh_down)).reshape(shape)


def setup_inputs(seed: int = 0) -> dict:
    key = jax.random.key(seed)
    ks = jax.random.split(key, 24)
    f32 = jnp.float32

    def nrm(k, shape, scale):
        return jax.random.normal(k, shape, f32) * scale

    def gain(k, shape):
        return 1.0 + 0.02 * jax.random.normal(k, shape, f32)

    return {
        "x": nrm(ks[0], (BATCH, SEQ, D_MODEL), 1.0),
        "c": nrm(ks[1], (BATCH, D_MODEL), 1.0),
        "ctx": nrm(ks[2], (BATCH, CTX_LEN, D_MODEL), 1.0),
        "c_ctx": nrm(ks[3], (D_MODEL,), 1.0),
        "w_ada": nrm(ks[4], (DEPTH, D_MODEL, N_MOD * D_MODEL), 0.5 * D_MODEL ** -0.5),
        "b_ada": nrm(ks[5], (DEPTH, N_MOD * D_MODEL), 0.02),
        "norm_mix": gain(ks[6], (DEPTH, D_MODEL)),
        "w_in": nrm(ks[7], (DEPTH, D_MODEL, IN_W), D_MODEL ** -0.5),
        "q_norm": gain(ks[8], (DEPTH, HEAD_DIM)),
        "k_norm": gain(ks[9], (DEPTH, HEAD_DIM)),
        "gm_norm": gain(ks[10], (DEPTH, GM_HD)),
        "w_spatial": nrm(ks[11], (DEPTH, GM_GROUPS, CHUNK, CHUNK), 0.5 * CHUNK ** -0.5),
        "b_spatial": gain(ks[12], (DEPTH, GM_GROUPS, CHUNK)),
        "out_norm_attn": gain(ks[13], (DEPTH, ATT_W)),
        "out_norm_gm": gain(ks[14], (DEPTH, GM_W)),
        "w_out": nrm(ks[15], (DEPTH, MIX_W, D_MODEL), MIX_W ** -0.5),
        "norm_ffn": gain(ks[16], (DEPTH, D_MODEL)),
        "w_router": nrm(ks[17], (DEPTH, D_MODEL, N_EXPERTS), D_MODEL ** -0.5),
        "router_bias": nrm(ks[18], (DEPTH, N_EXPERTS), 0.01),
        "w_exp_gu": nrm(ks[19], (DEPTH, N_EXPERTS, D_MODEL, 2 * EXPERT_H), D_MODEL ** -0.5),
        "w_exp_down": nrm(ks[20], (DEPTH, N_EXPERTS, EXPERT_H, D_MODEL), EXPERT_H ** -0.5),
        "w_sh_gu": nrm(ks[21], (DEPTH, D_MODEL, 2 * SHARED_H), D_MODEL ** -0.5),
        "w_sh_down": nrm(ks[22], (DEPTH, SHARED_H, D_MODEL), SHARED_H ** -0.5),
    }


def reference(x, c, ctx, c_ctx, w_ada, b_ada, norm_mix, w_in, q_norm, k_norm, gm_norm,
              w_spatial, b_spatial, out_norm_attn, out_norm_gm, w_out, norm_ffn,
              w_router, router_bias, w_exp_gu, w_exp_down, w_sh_gu, w_sh_down):
    b, length, _ = x.shape
    n_ctx = ctx.shape[1]
    cos, sin = axial_rope_tables(length)
    for layer in range(DEPTH):
        last = layer == DEPTH - 1
        sh1, sc1, g1, sh2, sc2, g2 = ada_params(c, w_ada[layer], b_ada[layer])
        csh1, csc1, cg1, csh2, csc2, cg2 = ada_params(c_ctx, w_ada[layer], b_ada[layer])

        hc = modulate(rms_norm(ctx, norm_mix[layer]), csh1, csc1)
        if last:
            k_c, v_c = jnp.split(hc @ w_in[layer][:, ATT_W:ATT_W + 2 * KV_W], 2, axis=-1)
        else:
            q_c, k_c, v_c, u_c, gt_c = split_in_proj(hc @ w_in[layer])
        k_c = rms_norm(k_c.reshape(b, n_ctx, ATT_KV_HEADS, HEAD_DIM), k_norm[layer])
        v_c = v_c.reshape(b, n_ctx, ATT_KV_HEADS, HEAD_DIM)

        h = modulate(rms_norm(x, norm_mix[layer]), sh1[:, None], sc1[:, None])
        q, k, v, u, gt = split_in_proj(h @ w_in[layer])
        q = apply_rope(rms_norm(q.reshape(b, length, ATT_KV_HEADS, ATT_GROUP, HEAD_DIM), q_norm[layer]), cos, sin)
        k = apply_rope(rms_norm(k.reshape(b, length, ATT_KV_HEADS, HEAD_DIM), k_norm[layer]), cos, sin)
        v = v.reshape(b, length, ATT_KV_HEADS, HEAD_DIM)
        att = latent_attention(q, k, v, k_c, v_c)
        gm = chunk_gmlp(u, gt, gm_norm[layer], w_spatial[layer], b_spatial[layer])
        x_new = x + g1[:, None] * merge_groups(att, gm, out_norm_attn[layer], out_norm_gm[layer], w_out[layer])

        h2 = modulate(rms_norm(x_new, norm_ffn[layer]), sh2[:, None], sc2[:, None])
        x_new = x_new + g2[:, None] * moe_ffn(h2, w_router[layer], router_bias[layer], w_exp_gu[layer],
                                              w_exp_down[layer], w_sh_gu[layer], w_sh_down[layer])

        if not last:
            q_c = rms_norm(q_c.reshape(b, n_ctx, ATT_KV_HEADS, ATT_GROUP, HEAD_DIM), q_norm[layer])
            att_c = context_attention(q_c, k_c, v_c)
            gm_c = chunk_gmlp(u_c, gt_c, gm_norm[layer], w_spatial[layer], b_spatial[layer])
            ctx = ctx + cg1 * merge_groups(att_c, gm_c, out_norm_attn[layer], out_norm_gm[layer], w_out[layer])
            h2c = modulate(rms_norm(ctx, norm_ffn[layer]), csh2, csc2)
            ctx = ctx + cg2 * moe_ffn(h2c, w_router[layer], router_bias[layer], w_exp_gu[layer],
                                      w_exp_down[layer], w_sh_gu[layer], w_sh_down[layer])
        x = x_new
    return x
```

```python
import functools

import jax
import jax.numpy as jnp
from jax import lax
from jax.experimental import pallas as pl
from jax.experimental.pallas import tpu as pltpu

F32 = jnp.float32
BF16 = jnp.bfloat16

EPS = 1e-6
GRID_W = 64
ROPE_THETA = 10000.0
ATT_KV_HEADS = 2
TOP_K = 6
ROUTED_SCALE = 2.5
N_MOD = 6
CHUNK = 128

LANES = 128
MXU_EDGE = 256
VMEM_LIMIT_BYTES = 56 * 1024 * 1024


def _cparams(*sem):
    return pltpu.CompilerParams(dimension_semantics=sem, vmem_limit_bytes=VMEM_LIMIT_BYTES)


def _dot(a, b):
    return jnp.dot(a, b, preferred_element_type=F32)


def _dot_nt(a, b):
    return lax.dot_general(a, b, (((1,), (1,)), ((), ())), preferred_element_type=F32)


def _rms(x, g):
    return x * lax.rsqrt(jnp.mean(x * x, axis=-1, keepdims=True) + EPS) * g


def _group_mean_sq(x, bd_ref):
    x2 = (x * x).astype(BF16)
    parts = [_dot(x2[:, i:i + MXU_EDGE], bd_ref[...]) for i in range(0, x.shape[1], MXU_EDGE)]
    return parts[0] if len(parts) == 1 else jnp.concatenate(parts, axis=-1)


def _dup_halves(x, lane):
    xr = pltpu.roll(x, LANES // 2, axis=1)
    lo = lane < LANES // 2
    return jnp.where(lo, x, xr), jnp.where(lo, xr, x)


def _ada_kernel(c_ref, w_ref, b_ref, o_ref):
    s = jax.nn.silu(c_ref[...])
    o_ref[...] = jnp.dot(s, w_ref[...], precision=lax.Precision.HIGHEST,
                         preferred_element_type=F32) + b_ref[...]


def _ada(c_all, w_ada, b_ada):
    rows, d = c_all.shape
    n = w_ada.shape[1]
    bn = 512
    return pl.pallas_call(
        _ada_kernel,
        out_shape=jax.ShapeDtypeStruct((rows, n), F32),
        grid=(n // bn,),
        in_specs=[pl.BlockSpec((rows, d), lambda j: (0, 0)),
                  pl.BlockSpec((d, bn), lambda j: (0, j)),
                  pl.BlockSpec((1, bn), lambda j: (0, j))],
        out_specs=pl.BlockSpec((rows, bn), lambda j: (0, j)),
        compiler_params=_cparams("arbitrary"),
        name="ada",
    )(c_all, w_ada, b_ada)


def _ctx_kv_kernel(x_ref, sh_ref, sc_ref, nmix_ref, w_ref, kg_ref, bd_ref, k_ref, v_ref):
    h = _rms(x_ref[...], nmix_ref[...]) * (1.0 + sc_ref[...]) + sh_ref[...]
    p = _dot(h.astype(BF16), w_ref[...])
    ms = _group_mean_sq(p, bd_ref)
    k = p[:, :LANES] * lax.rsqrt(ms[:, :LANES] + EPS) * kg_ref[...]
    v = p[:, LANES:]
    lane = lax.broadcasted_iota(jnp.int32, k.shape, 1)
    ka, kb = _dup_halves(k, lane)
    va, vb = _dup_halves(v, lane)
    k_ref[...] = jnp.concatenate([ka, kb], axis=-1).astype(BF16)
    v_ref[...] = jnp.concatenate([va, vb], axis=-1).astype(BF16)


def _ctx_kv(ctx, mod4, ctx_row, nmix, w_kv, kg, bd):
    b, n_ctx, d = ctx.shape
    kv_w = 2 * LANES
    return pl.pallas_call(
        _ctx_kv_kernel,
        out_shape=(jax.ShapeDtypeStruct((b, n_ctx, kv_w), BF16),) * 2,
        grid=(b,),
        in_specs=[pl.BlockSpec((None, n_ctx, d), lambda i: (i, 0, 0)),
                  pl.BlockSpec((None, None, 1, d), lambda i: (ctx_row, 0, 0, 0)),
                  pl.BlockSpec((None, None, 1, d), lambda i: (ctx_row, 1, 0, 0)),
                  pl.BlockSpec((1, d), lambda i: (0, 0)),
                  pl.BlockSpec(w_kv.shape, lambda i: (0, 0)),
                  pl.BlockSpec((1, LANES), lambda i: (0, 0)),
                  pl.BlockSpec(bd.shape, lambda i: (0, 0))],
        out_specs=(pl.BlockSpec((None, n_ctx, kv_w), lambda i: (i, 0, 0)),) * 2,
        compiler_params=_cparams("arbitrary"),
        name="ctx_kv",
    )(ctx, mod4, mod4, nmix, w_kv, kg, bd)


def _inproj_kernel(x_ref, sh_ref, sc_ref, nmix_ref, w_ref, cos_ref, sin_ref, qg_ref, kg_ref, gmg_ref,
                   bd_ref, ws_ref, bs_ref, og_ref, q_ref, k_ref, v_ref, gm_ref, *, att_w, gm_w):
    tm = x_ref.shape[0]
    h = _rms(x_ref[...], nmix_ref[...]) * (1.0 + sc_ref[...]) + sh_ref[...]
    p = _dot(h.astype(BF16), w_ref[...])

    cos1, sin1 = cos_ref[...], sin_ref[...]
    lane1 = lax.broadcasted_iota(jnp.int32, (tm, LANES), 1)

    def rope(xn, cosw, sinw, lanew):
        w = xn.shape[1]
        fwd = pltpu.roll(xn, w - 32, axis=1)
        bwd = pltpu.roll(xn, 32, axis=1)
        swapped = jnp.where((lanew & 63) < 32, fwd, bwd)
        return xn * cosw + swapped * sinw

    q = p[:, :att_w]
    qn = q * lax.rsqrt(_group_mean_sq(q, bd_ref) + EPS) * qg_ref[...]
    reps = att_w // LANES
    cosq = jnp.concatenate([cos1] * reps, axis=-1)
    sinq = jnp.concatenate([sin1] * reps, axis=-1)
    laneq = lax.broadcasted_iota(jnp.int32, (tm, att_w), 1)
    q_ref[...] = rope(qn, cosq, sinq, laneq).astype(BF16)

    kv = p[:, att_w:att_w + 2 * LANES]
    ms = _group_mean_sq(kv, bd_ref)
    kn = kv[:, :LANES] * lax.rsqrt(ms[:, :LANES] + EPS) * kg_ref[...]
    k = rope(kn, cos1, sin1, lane1)
    v = kv[:, LANES:]
    ka, kb = _dup_halves(k, lane1)
    va, vb = _dup_halves(v, lane1)
    k_ref[...] = jnp.concatenate([ka, kb], axis=-1).astype(BF16)
    v_ref[...] = jnp.concatenate([va, vb], axis=-1).astype(BF16)

    u0 = att_w + 2 * LANES
    u = jax.nn.gelu(p[:, u0:u0 + gm_w])
    gt = jax.nn.gelu(p[:, u0 + gm_w:u0 + 2 * gm_w])
    gt = (gt * lax.rsqrt(_group_mean_sq(gt, bd_ref) + EPS) * gmg_ref[...]).astype(BF16)
    lo_half = lax.broadcasted_iota(jnp.int32, (CHUNK, LANES), 1) < LANES // 2
    rows = []
    for c in range(tm // CHUNK):
        blocks = []
        for j in range(gm_w // LANES):
            g = gt[c * CHUNK:(c + 1) * CHUNK, j * LANES:(j + 1) * LANES]
            r = _dot(ws_ref[j], g)
            blocks.append(jnp.where(lo_half, r[:CHUNK], r[CHUNK:]))
        rows.append(jnp.concatenate(blocks, axis=-1) + bs_ref[...])
    mixed = jnp.concatenate(rows, axis=0)
    gm_ref[...] = _rms(u * mixed, og_ref[...]).astype(BF16)


def _inproj(x, mod4, nmix, w_in, cos_t, sin_t, qg, kg, gmg, bd, ws, bs, og, *, att_w, gm_w, tm):
    b, length, d = x.shape
    kv_w = 2 * LANES
    n_in = w_in.shape[1]
    const = lambda shape: pl.BlockSpec(shape, lambda i, j: (0,) * len(shape))
    return pl.pallas_call(
        functools.partial(_inproj_kernel, att_w=att_w, gm_w=gm_w),
        out_shape=(jax.ShapeDtypeStruct((b, length, att_w), BF16),
                   jax.ShapeDtypeStruct((b, length, kv_w), BF16),
                   jax.ShapeDtypeStruct((b, length, kv_w), BF16),
                   jax.ShapeDtypeStruct((b, length, gm_w), BF16)),
        grid=(b, length // tm),
        in_specs=[pl.BlockSpec((None, tm, d), lambda i, j: (i, j, 0)),
                  pl.BlockSpec((None, None, 1, d), lambda i, j: (i, 0, 0, 0)),
                  pl.BlockSpec((None, None, 1, d), lambda i, j: (i, 1, 0, 0)),
                  const((1, d)),
                  const((d, n_in)),
                  pl.BlockSpec((tm, LANES), lambda i, j: (j, 0)),
                  pl.BlockSpec((tm, LANES), lambda i, j: (j, 0)),
                  const((1, att_w)), const((1, LANES)), const((1, gm_w)),
                  const(bd.shape), const(ws.shape), const(bs.shape), const((1, gm_w))],
        out_specs=(pl.BlockSpec((None, tm, att_w), lambda i, j: (i, j, 0)),
                   pl.BlockSpec((None, tm, kv_w), lambda i, j: (i, j, 0)),
                   pl.BlockSpec((None, tm, kv_w), lambda i, j: (i, j, 0)),
                   pl.BlockSpec((None, tm, gm_w), lambda i, j: (i, j, 0))),
        compiler_params=_cparams("arbitrary", "arbitrary"),
        name="inproj",
    )(x, mod4, mod4, nmix, w_in, cos_t, sin_t, qg, kg, gmg, bd, ws, bs, og)


def _attn_kernel(q_ref, kc_ref, kl_ref, vc_ref, vl_ref, o_ref):
    q = q_ref[...]
    lane = lax.broadcasted_iota(jnp.int32, q.shape, 1)
    lo = lane < LANES // 2
    zero = jnp.zeros_like(q)
    outs = []
    for qh in (jnp.where(lo, q, zero), jnp.where(lo, zero, q)):
        s_c = _dot_nt(qh, kc_ref[...])
        s_l = _dot_nt(qh, kl_ref[...])
        m = jnp.maximum(jnp.max(s_c, axis=-1, keepdims=True), jnp.max(s_l, axis=-1, keepdims=True))
        e_c = jnp.exp(s_c - m)
        e_l = jnp.exp(s_l - m)
        denom = jnp.sum(e_c, axis=-1, keepdims=True) + jnp.sum(e_l, axis=-1, keepdims=True)
        o = _dot(e_c.astype(BF16), vc_ref[...]) + _dot(e_l.astype(BF16), vl_ref[...])
        outs.append(o / denom)
    o_ref[...] = jnp.where(lo, outs[0], outs[1]).astype(o_ref.dtype)


def _attention(q, k_c, k_l, v_c, v_l, *, tq):
    b, length, att_w = q.shape
    n_ctx = k_c.shape[1]
    pairs = att_w // LANES // ATT_KV_HEADS
    return pl.pallas_call(
        _attn_kernel,
        out_shape=jax.ShapeDtypeStruct((b, length, att_w), BF16),
        grid=(b, ATT_KV_HEADS, length // tq, pairs),
        in_specs=[pl.BlockSpec((None, tq, LANES), lambda i, h, j, p: (i, j, h * pairs + p)),
                  pl.BlockSpec((None, n_ctx, LANES), lambda i, h, j, p: (i, 0, h)),
                  pl.BlockSpec((None, length, LANES), lambda i, h, j, p: (i, 0, h)),
                  pl.BlockSpec((None, n_ctx, LANES), lambda i, h, j, p: (i, 0, h)),
                  pl.BlockSpec((None, length, LANES), lambda i, h, j, p: (i, 0, h))],
        out_specs=pl.BlockSpec((None, tq, LANES), lambda i, h, j, p: (i, j, h * pairs + p)),
        compiler_params=_cparams("arbitrary", "arbitrary", "arbitrary", "arbitrary"),
        name="attn",
    )(q, k_c, k_l, v_c, v_l)


def _post_kernel(att_ref, gm_ref, x_ref, g1_ref, sh2_ref, sc2_ref, g2_ref, oa_ref, wo_ref, nffn_ref,
                 wrh_ref, wrl_ref, rb_ref, wsg_ref, wsd_ref,
                 base_ref, h2_ref, idx_ref, gate_ref, rank_ref, cnt_ref, carry_ref, *, n_exp):
    i = pl.program_id(0)
    tm = x_ref.shape[0]

    @pl.when(i == 0)
    def _():
        carry_ref[...] = jnp.zeros_like(carry_ref)

    att_n = _rms(att_ref[...].astype(F32), oa_ref[...]).astype(BF16)
    y = _dot(jnp.concatenate([att_n, gm_ref[...]], axis=-1), wo_ref[...])
    x_new = x_ref[...] + g1_ref[...] * y
    h2 = _rms(x_new, nffn_ref[...]) * (1.0 + sc2_ref[...]) + sh2_ref[...]
    h2_ref[...] = h2

    h_hi = h2.astype(BF16)
    h_lo = (h2 - h_hi.astype(F32)).astype(BF16)
    logits = _dot(h_hi, wrh_ref[...]) + (_dot(h_lo, wrh_ref[...]) + _dot(h_hi, wrl_ref[...]))
    scores = jax.nn.sigmoid(logits)
    lane = lax.broadcasted_iota(jnp.int32, scores.shape, 1).astype(F32)
    neg = jnp.float32(-jnp.inf)
    sel = jnp.where(lane < n_exp, scores + rb_ref[...], neg)

    onehot = jnp.zeros(scores.shape, F32)
    idxs, svals = [], []
    for _k in range(TOP_K):
        m = jnp.max(sel, axis=-1, keepdims=True)
        idx = jnp.min(jnp.where(sel == m, lane, float(LANES)), axis=-1, keepdims=True)
        hit = lane == idx
        idxs.append(idx)
        svals.append(jnp.sum(jnp.where(hit, scores, 0.0), axis=-1, keepdims=True))
        onehot = onehot + jnp.where(hit, 1.0, 0.0)
        sel = jnp.where(hit, neg, sel)
    ssum = functools.reduce(lambda a, b: a + b, svals)

    r_io = lax.broadcasted_iota(jnp.int32, (tm, tm), 0)
    c_io = lax.broadcasted_iota(jnp.int32, (tm, tm), 1)
    tri = jnp.where(c_io < r_io, 1.0, 0.0).astype(BF16)
    before = _dot(tri, onehot.astype(BF16)) + carry_ref[...]
    carry_ref[...] = carry_ref[...] + jnp.sum(onehot, axis=0, keepdims=True)
    cnt_ref[...] = carry_ref[...]

    idx_o = jnp.zeros(scores.shape, jnp.int32)
    rank_o = jnp.zeros(scores.shape, jnp.int32)
    gate_o = jnp.zeros(scores.shape, F32)
    for k in range(TOP_K):
        hit = lane == idxs[k]
        rank = jnp.sum(jnp.where(hit, before, 0.0), axis=-1, keepdims=True)
        idx_o = jnp.where(lane == k, idxs[k].astype(jnp.int32), idx_o)
        rank_o = jnp.where(lane == k, rank.astype(jnp.int32), rank_o)
        gate_o = jnp.where(lane == k, svals[k] / ssum * ROUTED_SCALE, gate_o)
    idx_ref[...] = idx_o
    rank_ref[...] = rank_o
    gate_ref[...] = gate_o

    gu = _dot(h_hi, wsg_ref[...])
    hs = gu.shape[1] // 2
    act = (jax.nn.silu(gu[:, :hs]) * gu[:, hs:]).astype(BF16)
    base_ref[...] = x_new + g2_ref[...] * _dot(act, wsd_ref[...])


def _post(att, gmn, x2, mod4, oa, w_out, nffn, wr_hi, wr_lo, rb, w_sg, w_sd, *, tm, tokens_per_batch, n_exp):
    t, d = x2.shape
    att_w, gm_w = att.shape[1], gmn.shape[1]
    tpb = tokens_per_batch // tm
    const = lambda shape: pl.BlockSpec(shape, lambda i: (0,) * len(shape))
    mrow = lambda j: pl.BlockSpec((None, None, 1, d), lambda i: (i // tpb, j, 0, 0))
    wide = lambda dt: jax.ShapeDtypeStruct((t, LANES), dt)
    return pl.pallas_call(
        functools.partial(_post_kernel, n_exp=n_exp),
        out_shape=(jax.ShapeDtypeStruct((t, d), F32), jax.ShapeDtypeStruct((t, d), F32),
                   wide(jnp.int32), wide(F32), wide(jnp.int32),
                   jax.ShapeDtypeStruct((1, LANES), F32)),
        grid=(t // tm,),
        in_specs=[pl.BlockSpec((tm, att_w), lambda i: (i, 0)),
                  pl.BlockSpec((tm, gm_w), lambda i: (i, 0)),
                  pl.BlockSpec((tm, d), lambda i: (i, 0)),
                  mrow(2), mrow(3), mrow(4), mrow(5),
                  const((1, att_w)), const(w_out.shape), const((1, d)),
                  const(wr_hi.shape), const(wr_lo.shape), const((1, LANES)),
                  const(w_sg.shape), const(w_sd.shape)],
        out_specs=(pl.BlockSpec((tm, d), lambda i: (i, 0)), pl.BlockSpec((tm, d), lambda i: (i, 0)),
                   pl.BlockSpec((tm, LANES), lambda i: (i, 0)), pl.BlockSpec((tm, LANES), lambda i: (i, 0)),
                   pl.BlockSpec((tm, LANES), lambda i: (i, 0)), pl.BlockSpec((1, LANES), lambda i: (0, 0))),
        scratch_shapes=[pltpu.VMEM((1, LANES), F32)],
        compiler_params=_cparams("arbitrary"),
        name="post",
    )(att, gmn, x2, mod4, mod4, mod4, mod4, oa, w_out, nffn, wr_hi, wr_lo, rb, w_sg, w_sd)


def _experts_kernel(te_ref, nv_ref, tok_ref, dst_ref, h2_hbm, wgu_ref, wdn_ref, r_hbm, xbuf, ybuf, sems):
    i = pl.program_id(0)
    tr = xbuf.shape[0]
    n_valid = nv_ref[i]

    def gather_row(r):
        return pltpu.make_async_copy(h2_hbm.at[pl.ds(tok_ref[0, 0, r], 1)], xbuf.at[pl.ds(r, 1)], sems.at[0])

    def scatter_row(r):
        return pltpu.make_async_copy(ybuf.at[pl.ds(r, 1)], r_hbm.at[pl.ds(dst_ref[0, 0, r], 1)], sems.at[1])

    @pl.when(n_valid > 0)
    def _():
        def g_start(r, c):
            gather_row(r).start()
            return c

        def g_wait(r, c):
            gather_row(r).wait()
            return c

        lax.fori_loop(0, tr, g_start, 0)
        lax.fori_loop(0, tr, g_wait, 0)
        gu = _dot(xbuf[...].astype(BF16), wgu_ref[...])
        hh = gu.shape[1] // 2
        act = (jax.nn.silu(gu[:, :hh]) * gu[:, hh:]).astype(BF16)
        ybuf[...] = _dot(act, wdn_ref[...])

        def s_start(r, c):
            scatter_row(r).start()
            return c

        def s_wait(r, c):
            scatter_row(r).wait()
            return c

        lax.fori_loop(0, n_valid, s_start, 0)
        lax.fori_loop(0, n_valid, s_wait, 0)


def _experts(tile_expert, tile_valid, tok_sorted, dst_sorted, h2, w_gu, w_dn, *, tr, n_rows_out):
    nt = tile_expert.shape[0]
    n_exp, d, h2w = w_gu.shape
    hh = w_dn.shape[1]
    smem_blk = pl.BlockSpec((1, 1, tr), lambda i, te, nv: (i, 0, 0), memory_space=pltpu.SMEM)
    wmap = lambda i, te, nv: (te[i], 0, 0)
    return pl.pallas_call(
        _experts_kernel,
        out_shape=jax.ShapeDtypeStruct((n_rows_out, d), F32),
        grid_spec=pltpu.PrefetchScalarGridSpec(
            num_scalar_prefetch=2,
            grid=(nt,),
            in_specs=[smem_blk, smem_blk,
                      pl.BlockSpec(memory_space=pl.ANY),
                      pl.BlockSpec((None, d, h2w), wmap),
                      pl.BlockSpec((None, hh, d), wmap)],
            out_specs=pl.BlockSpec(memory_space=pl.ANY),
            scratch_shapes=[pltpu.VMEM((tr, d), F32), pltpu.VMEM((tr, d), F32),
                            pltpu.SemaphoreType.DMA((2,))]),
        compiler_params=_cparams("arbitrary"),
        name="experts",
    )(tile_expert, tile_valid, tok_sorted.reshape(nt, 1, tr), dst_sorted.reshape(nt, 1, tr), h2, w_gu, w_dn)


def _combine_kernel(base_ref, g2_ref, gate_ref, *refs):
    r_refs, o_ref = refs[:TOP_K], refs[TOP_K]
    gate = gate_ref[...]
    acc = jnp.zeros(base_ref.shape, F32)
    for k in range(TOP_K):
        acc = acc + gate[:, k:k + 1] * r_refs[k][...]
    o_ref[...] = base_ref[...] + g2_ref[...] * acc


def _combine(base, mod4, gate, r, *, tm, tokens_per_batch):
    t, d = base.shape
    tpb = tokens_per_batch // tm
    nblk = t // tm
    r_spec = lambda k: pl.BlockSpec((tm, d), lambda i: (k * nblk + i, 0))
    return pl.pallas_call(
        _combine_kernel,
        out_shape=jax.ShapeDtypeStruct((t, d), F32),
        grid=(nblk,),
        in_specs=[pl.BlockSpec((tm, d), lambda i: (i, 0)),
                  pl.BlockSpec((None, None, 1, d), lambda i: (i // tpb, 5, 0, 0)),
                  pl.BlockSpec((tm, LANES), lambda i: (i, 0))] + [r_spec(k) for k in range(TOP_K)],
        out_specs=pl.BlockSpec((tm, d), lambda i: (i, 0)),
        compiler_params=_cparams("arbitrary"),
        name="combine",
    )(base, mod4, gate, *([r] * TOP_K))


def _rope_tables(length, head_dim):
    rows = length // GRID_W
    pairs_axis = head_dim // 4
    r = jnp.repeat(jnp.arange(rows, dtype=F32), GRID_W)
    col = jnp.tile(jnp.arange(GRID_W, dtype=F32), rows)
    inv = ROPE_THETA ** (-jnp.arange(pairs_axis, dtype=F32) / pairs_axis)
    ang = jnp.concatenate([r[:, None] * inv, col[:, None] * inv], axis=-1)
    cos, sin = jnp.cos(ang), jnp.sin(ang)
    cos_h = jnp.concatenate([cos, cos], axis=-1)
    sin_h = jnp.concatenate([-sin, sin], axis=-1)
    reps = LANES // head_dim
    return jnp.tile(cos_h, (1, reps)), jnp.tile(sin_h, (1, reps))


def kernel(x, c, ctx, c_ctx, w_ada, b_ada, norm_mix, w_in, q_norm, k_norm, gm_norm, w_spatial, b_spatial,
           out_norm_attn, out_norm_gm, w_out, norm_ffn, w_router, router_bias, w_exp_gu, w_exp_down,
           w_sh_gu, w_sh_down):
    assert w_ada.shape[0] == 1, "single-layer kernel"
    b, length, d = x.shape
    head_dim = q_norm.shape[-1]
    att_w = out_norm_attn.shape[-1]
    gm_w = out_norm_gm.shape[-1]
    gm_groups = w_spatial.shape[1]
    gm_hd = gm_w // gm_groups
    n_exp = w_router.shape[-1]
    t = b * length
    assert head_dim == LANES // 2 and gm_hd == LANES // 2 and w_spatial.shape[-1] == CHUNK
    assert n_exp <= LANES and att_w % MXU_EDGE == 0 and gm_w % MXU_EDGE == 0

    mod_rows = 16
    c_all = jnp.zeros((mod_rows, d), F32).at[:b].set(c).at[b].set(c_ctx)
    mod = _ada(c_all, w_ada[0], b_ada)
    mod4 = mod.reshape(mod_rows, N_MOD, 1, d)

    w_in_b = w_in[0].astype(BF16)
    bd = (jnp.arange(MXU_EDGE)[:, None] // head_dim == jnp.arange(MXU_EDGE)[None, :] // head_dim)
    bd = (bd.astype(F32) / head_dim).astype(BF16)
    qg = jnp.tile(q_norm[0] * head_dim ** -0.5, att_w // head_dim)[None, :]
    kg = jnp.tile(k_norm[0], LANES // head_dim)[None, :]
    gmg = jnp.tile(gm_norm[0], gm_groups)[None, :]
    ws = w_spatial[0].astype(BF16).reshape(gm_groups // 2, 2 * CHUNK, CHUNK)
    bs = jnp.repeat(b_spatial[0].T, gm_hd, axis=1)
    cos_t, sin_t = _rope_tables(length, head_dim)

    k_c, v_c = _ctx_kv(ctx, mod4, b, norm_mix, w_in_b[:, att_w:att_w + 2 * LANES], kg, bd)
    q, k_l, v_l, gmn = _inproj(x, mod4, norm_mix, w_in_b, cos_t, sin_t, qg, kg, gmg, bd, ws, bs,
                               out_norm_gm, att_w=att_w, gm_w=gm_w, tm=min(512, length))
    att = _attention(q, k_c, k_l, v_c, v_l, tq=min(256, length))

    wr = jnp.zeros((d, LANES), F32).at[:, :n_exp].set(w_router[0])
    wr_hi = wr.astype(BF16)
    wr_lo = (wr - wr_hi.astype(F32)).astype(BF16)
    rb = jnp.zeros((1, LANES), F32).at[0, :n_exp].set(router_bias[0])
    tm_post = min(512, length)
    base, h2, idx, gate, rank, cnt = _post(
        att.reshape(t, att_w), gmn.reshape(t, gm_w), x.reshape(t, d), mod4, out_norm_attn, w_out[0].astype(BF16),
        norm_ffn, wr_hi, wr_lo, rb, w_sh_gu[0].astype(BF16), w_sh_down[0].astype(BF16),
        tm=tm_post, tokens_per_batch=length, n_exp=n_exp)

    tr = 256
    n_pairs = t * TOP_K
    nt = n_pairs // tr + n_exp
    counts = cnt[0, :n_exp].astype(jnp.int32)
    padded = (counts + tr - 1) // tr * tr
    ends = jnp.cumsum(padded)
    offs = ends - padded
    tile_start = jnp.arange(nt, dtype=jnp.int32) * tr
    tile_expert = jnp.minimum(jnp.searchsorted(ends, tile_start, side="right"), n_exp - 1).astype(jnp.int32)
    tile_valid = jnp.clip(offs[tile_expert] + counts[tile_expert] - tile_start, 0, tr).astype(jnp.int32)
    pos = (offs[idx[:, :TOP_K]] + rank[:, :TOP_K]).reshape(-1)
    out_row = (jnp.arange(TOP_K, dtype=jnp.int32)[None, :] * t
               + jnp.arange(t, dtype=jnp.int32)[:, None]).reshape(-1)
    dst_sorted = jnp.zeros((nt * tr,), jnp.int32).at[pos].set(out_row, unique_indices=True)
    tok_sorted = dst_sorted % t

    r = _experts(tile_expert, tile_valid, tok_sorted, dst_sorted, h2, w_exp_gu[0].astype(BF16),
                 w_exp_down[0].astype(BF16), tr=tr, n_rows_out=n_pairs)
    out = _combine(base, mod4, gate, r, tm=min(256, length), tokens_per_batch=length)
    return out.reshape(b, length, d)
```

```python
import dataclasses
import functools

import jax
import jax.numpy as jnp
from jax import lax
from jax.experimental import pallas as pl
from jax.experimental.pallas import tpu as pltpu
from jax.experimental.pallas import tpu_sc as plsc

F32 = jnp.float32
BF16 = jnp.bfloat16

EPS = 1e-6
GRID_W = 64
ROPE_THETA = 10000.0
ATT_KV_HEADS = 2
TOP_K = 6
ROUTED_SCALE = 2.5
N_MOD = 6
CHUNK = 128

LANES = 128
MXU_EDGE = 256
VMEM_LIMIT_BYTES = 56 * 1024 * 1024
SUBLANES = 8
CHOICE_ROWS = SUBLANES
SC_WINDOW = 64


def _cparams(*sem):
    return pltpu.CompilerParams(dimension_semantics=sem, vmem_limit_bytes=VMEM_LIMIT_BYTES)


def _dot(a, b):
    return jnp.dot(a, b, preferred_element_type=F32)


def _dot_nt(a, b):
    return lax.dot_general(a, b, (((1,), (1,)), ((), ())), preferred_element_type=F32)


def _rms(x, g):
    return x * lax.rsqrt(jnp.mean(x * x, axis=-1, keepdims=True) + EPS) * g


def _group_mean_sq(x, bd_ref):
    x2 = (x * x).astype(BF16)
    parts = [_dot(x2[:, i:i + MXU_EDGE], bd_ref[...]) for i in range(0, x.shape[1], MXU_EDGE)]
    return parts[0] if len(parts) == 1 else jnp.concatenate(parts, axis=-1)


def _dup_halves(x, lane):
    xr = pltpu.roll(x, LANES // 2, axis=1)
    lo = lane < LANES // 2
    return jnp.where(lo, x, xr), jnp.where(lo, xr, x)


_HI16 = 0xFFFF0000


def _pack_rows(x):
    half = x.shape[1] // 2
    rounded = x.astype(BF16).astype(F32)
    bits = lax.bitcast_convert_type(rounded, jnp.uint32)
    word = (bits[:, :half] >> 16) | (bits[:, half:] & jnp.uint32(_HI16))
    return lax.bitcast_convert_type(word, jnp.int32)


def _unpack_rows(w):
    bits = lax.bitcast_convert_type(w, jnp.uint32)
    lo = lax.bitcast_convert_type(bits << 16, F32)
    hi = lax.bitcast_convert_type(bits & jnp.uint32(_HI16), F32)
    return lo, hi


def _ada_kernel(c_ref, w_ref, b_ref, o_ref):
    s = jax.nn.silu(c_ref[...])
    o_ref[...] = jnp.dot(s, w_ref[...], precision=lax.Precision.HIGHEST,
                         preferred_element_type=F32) + b_ref[...]


def _ada(c_all, w_ada, b_ada):
    rows, d = c_all.shape
    n = w_ada.shape[1]
    bn = 512
    return pl.pallas_call(
        _ada_kernel,
        out_shape=jax.ShapeDtypeStruct((rows, n), F32),
        grid=(n // bn,),
        in_specs=[pl.BlockSpec((rows, d), lambda j: (0, 0)),
                  pl.BlockSpec((d, bn), lambda j: (0, j)),
                  pl.BlockSpec((1, bn), lambda j: (0, j))],
        out_specs=pl.BlockSpec((rows, bn), lambda j: (0, j)),
        compiler_params=_cparams("arbitrary"),
        name="ada",
    )(c_all, w_ada, b_ada)


def _ctx_kv_kernel(x_ref, sh_ref, sc_ref, nmix_ref, w_ref, kg_ref, bd_ref, k_ref, v_ref):
    h = _rms(x_ref[...], nmix_ref[...]) * (1.0 + sc_ref[...]) + sh_ref[...]
    p = _dot(h.astype(BF16), w_ref[...])
    ms = _group_mean_sq(p, bd_ref)
    k = p[:, :LANES] * lax.rsqrt(ms[:, :LANES] + EPS) * kg_ref[...]
    v = p[:, LANES:]
    lane = lax.broadcasted_iota(jnp.int32, k.shape, 1)
    ka, kb = _dup_halves(k, lane)
    va, vb = _dup_halves(v, lane)
    k_ref[...] = jnp.concatenate([ka, kb], axis=-1).astype(BF16)
    v_ref[...] = jnp.concatenate([va, vb], axis=-1).astype(BF16)


def _ctx_kv(ctx, mod4, ctx_row, nmix, w_kv, kg, bd):
    b, n_ctx, d = ctx.shape
    kv_w = 2 * LANES
    return pl.pallas_call(
        _ctx_kv_kernel,
        out_shape=(jax.ShapeDtypeStruct((b, n_ctx, kv_w), BF16),) * 2,
        grid=(b,),
        in_specs=[pl.BlockSpec((None, n_ctx, d), lambda i: (i, 0, 0)),
                  pl.BlockSpec((None, None, 1, d), lambda i: (ctx_row, 0, 0, 0)),
                  pl.BlockSpec((None, None, 1, d), lambda i: (ctx_row, 1, 0, 0)),
                  pl.BlockSpec((1, d), lambda i: (0, 0)),
                  pl.BlockSpec(w_kv.shape, lambda i: (0, 0)),
                  pl.BlockSpec((1, LANES), lambda i: (0, 0)),
                  pl.BlockSpec(bd.shape, lambda i: (0, 0))],
        out_specs=(pl.BlockSpec((None, n_ctx, kv_w), lambda i: (i, 0, 0)),) * 2,
        compiler_params=_cparams("arbitrary"),
        name="ctx_kv",
    )(ctx, mod4, mod4, nmix, w_kv, kg, bd)


def _inproj_kernel(x_ref, sh_ref, sc_ref, nmix_ref, w_ref, cos_ref, sin_ref, qg_ref, kg_ref, gmg_ref,
                   bd_ref, ws_ref, bs_ref, og_ref, q_ref, k_ref, v_ref, gm_ref, *, att_w, gm_w):
    tm = x_ref.shape[0]
    h = _rms(x_ref[...], nmix_ref[...]) * (1.0 + sc_ref[...]) + sh_ref[...]
    p = _dot(h.astype(BF16), w_ref[...])

    cos1, sin1 = cos_ref[...], sin_ref[...]
    lane1 = lax.broadcasted_iota(jnp.int32, (tm, LANES), 1)

    def rope(xn, cosw, sinw, lanew):
        w = xn.shape[1]
        fwd = pltpu.roll(xn, w - 32, axis=1)
        bwd = pltpu.roll(xn, 32, axis=1)
        swapped = jnp.where((lanew & 63) < 32, fwd, bwd)
        return xn * cosw + swapped * sinw

    q = p[:, :att_w]
    qn = q * lax.rsqrt(_group_mean_sq(q, bd_ref) + EPS) * qg_ref[...]
    reps = att_w // LANES
    cosq = jnp.concatenate([cos1] * reps, axis=-1)
    sinq = jnp.concatenate([sin1] * reps, axis=-1)
    laneq = lax.broadcasted_iota(jnp.int32, (tm, att_w), 1)
    q_ref[...] = rope(qn, cosq, sinq, laneq).astype(BF16)

    kv = p[:, att_w:att_w + 2 * LANES]
    ms = _group_mean_sq(kv, bd_ref)
    kn = kv[:, :LANES] * lax.rsqrt(ms[:, :LANES] + EPS) * kg_ref[...]
    k = rope(kn, cos1, sin1, lane1)
    v = kv[:, LANES:]
    ka, kb = _dup_halves(k, lane1)
    va, vb = _dup_halves(v, lane1)
    k_ref[...] = jnp.concatenate([ka, kb], axis=-1).astype(BF16)
    v_ref[...] = jnp.concatenate([va, vb], axis=-1).astype(BF16)

    u0 = att_w + 2 * LANES
    u = jax.nn.gelu(p[:, u0:u0 + gm_w])
    gt = jax.nn.gelu(p[:, u0 + gm_w:u0 + 2 * gm_w])
    gt = (gt * lax.rsqrt(_group_mean_sq(gt, bd_ref) + EPS) * gmg_ref[...]).astype(BF16)
    lo_half = lax.broadcasted_iota(jnp.int32, (CHUNK, LANES), 1) < LANES // 2
    rows = []
    for c in range(tm // CHUNK):
        blocks = []
        for j in range(gm_w // LANES):
            g = gt[c * CHUNK:(c + 1) * CHUNK, j * LANES:(j + 1) * LANES]
            r = _dot(ws_ref[j], g)
            blocks.append(jnp.where(lo_half, r[:CHUNK], r[CHUNK:]))
        rows.append(jnp.concatenate(blocks, axis=-1) + bs_ref[...])
    mixed = jnp.concatenate(rows, axis=0)
    gm_ref[...] = _rms(u * mixed, og_ref[...]).astype(BF16)


def _inproj(x, mod4, nmix, w_in, cos_t, sin_t, qg, kg, gmg, bd, ws, bs, og, *, att_w, gm_w, tm):
    b, length, d = x.shape
    kv_w = 2 * LANES
    n_in = w_in.shape[1]
    const = lambda shape: pl.BlockSpec(shape, lambda i, j: (0,) * len(shape))
    return pl.pallas_call(
        functools.partial(_inproj_kernel, att_w=att_w, gm_w=gm_w),
        out_shape=(jax.ShapeDtypeStruct((b, length, att_w), BF16),
                   jax.ShapeDtypeStruct((b, length, kv_w), BF16),
                   jax.ShapeDtypeStruct((b, length, kv_w), BF16),
                   jax.ShapeDtypeStruct((b, length, gm_w), BF16)),
        grid=(b, length // tm),
        in_specs=[pl.BlockSpec((None, tm, d), lambda i, j: (i, j, 0)),
                  pl.BlockSpec((None, None, 1, d), lambda i, j: (i, 0, 0, 0)),
                  pl.BlockSpec((None, None, 1, d), lambda i, j: (i, 1, 0, 0)),
                  const((1, d)),
                  const((d, n_in)),
                  pl.BlockSpec((tm, LANES), lambda i, j: (j, 0)),
                  pl.BlockSpec((tm, LANES), lambda i, j: (j, 0)),
                  const((1, att_w)), const((1, LANES)), const((1, gm_w)),
                  const(bd.shape), const(ws.shape), const(bs.shape), const((1, gm_w))],
        out_specs=(pl.BlockSpec((None, tm, att_w), lambda i, j: (i, j, 0)),
                   pl.BlockSpec((None, tm, kv_w), lambda i, j: (i, j, 0)),
                   pl.BlockSpec((None, tm, kv_w), lambda i, j: (i, j, 0)),
                   pl.BlockSpec((None, tm, gm_w), lambda i, j: (i, j, 0))),
        compiler_params=_cparams("arbitrary", "arbitrary"),
        name="inproj",
    )(x, mod4, mod4, nmix, w_in, cos_t, sin_t, qg, kg, gmg, bd, ws, bs, og)


def _attn_kernel(q_ref, kc_ref, kl_ref, vc_ref, vl_ref, o_ref):
    q = q_ref[...]
    lane = lax.broadcasted_iota(jnp.int32, q.shape, 1)
    lo = lane < LANES // 2
    zero = jnp.zeros_like(q)
    outs = []
    for qh in (jnp.where(lo, q, zero), jnp.where(lo, zero, q)):
        s_c = _dot_nt(qh, kc_ref[...])
        s_l = _dot_nt(qh, kl_ref[...])
        m = jnp.maximum(jnp.max(s_c, axis=-1, keepdims=True), jnp.max(s_l, axis=-1, keepdims=True))
        e_c = jnp.exp(s_c - m)
        e_l = jnp.exp(s_l - m)
        denom = jnp.sum(e_c, axis=-1, keepdims=True) + jnp.sum(e_l, axis=-1, keepdims=True)
        o = _dot(e_c.astype(BF16), vc_ref[...]) + _dot(e_l.astype(BF16), vl_ref[...])
        outs.append(o / denom)
    o_ref[...] = jnp.where(lo, outs[0], outs[1]).astype(o_ref.dtype)


def _attention(q, k_c, k_l, v_c, v_l, *, tq):
    b, length, att_w = q.shape
    n_ctx = k_c.shape[1]
    pairs = att_w // LANES // ATT_KV_HEADS
    return pl.pallas_call(
        _attn_kernel,
        out_shape=jax.ShapeDtypeStruct((b, length, att_w), BF16),
        grid=(b, ATT_KV_HEADS, length // tq, pairs),
        in_specs=[pl.BlockSpec((None, tq, LANES), lambda i, h, j, p: (i, j, h * pairs + p)),
                  pl.BlockSpec((None, n_ctx, LANES), lambda i, h, j, p: (i, 0, h)),
                  pl.BlockSpec((None, length, LANES), lambda i, h, j, p: (i, 0, h)),
                  pl.BlockSpec((None, n_ctx, LANES), lambda i, h, j, p: (i, 0, h)),
                  pl.BlockSpec((None, length, LANES), lambda i, h, j, p: (i, 0, h))],
        out_specs=pl.BlockSpec((None, tq, LANES), lambda i, h, j, p: (i, j, h * pairs + p)),
        compiler_params=_cparams("arbitrary", "arbitrary", "arbitrary", "arbitrary"),
        name="attn",
    )(q, k_c, k_l, v_c, v_l)


def _post_kernel(att_ref, gm_ref, x_ref, g1_ref, sh2_ref, sc2_ref, g2_ref, oa_ref, wo_ref, nffn_ref,
                 wrh_ref, wrl_ref, rb_ref, wsg_ref, wsd_ref,
                 base_ref, h2p_ref, idx_ref, gate_ref, rank_ref, cnt_ref, carry_ref, *, n_exp):
    i = pl.program_id(0)
    tm = x_ref.shape[0]

    @pl.when(i == 0)
    def _():
        carry_ref[...] = jnp.zeros_like(carry_ref)

    att_n = _rms(att_ref[...].astype(F32), oa_ref[...]).astype(BF16)
    y = _dot(jnp.concatenate([att_n, gm_ref[...]], axis=-1), wo_ref[...])
    x_new = x_ref[...] + g1_ref[...] * y
    h2 = _rms(x_new, nffn_ref[...]) * (1.0 + sc2_ref[...]) + sh2_ref[...]
    h2p_ref[...] = _pack_rows(h2)

    h_hi = h2.astype(BF16)
    h_lo = (h2 - h_hi.astype(F32)).astype(BF16)
    logits = _dot(h_hi, wrh_ref[...]) + (_dot(h_lo, wrh_ref[...]) + _dot(h_hi, wrl_ref[...]))
    scores = jax.nn.sigmoid(logits)
    lane = lax.broadcasted_iota(jnp.int32, scores.shape, 1).astype(F32)
    neg = jnp.float32(-jnp.inf)
    sel = jnp.where(lane < n_exp, scores + rb_ref[...], neg)

    onehot = jnp.zeros(scores.shape, F32)
    idxs, svals = [], []
    for _k in range(TOP_K):
        m = jnp.max(sel, axis=-1, keepdims=True)
        idx = jnp.min(jnp.where(sel == m, lane, float(LANES)), axis=-1, keepdims=True)
        hit = lane == idx
        idxs.append(idx)
        svals.append(jnp.sum(jnp.where(hit, scores, 0.0), axis=-1, keepdims=True))
        onehot = onehot + jnp.where(hit, 1.0, 0.0)
        sel = jnp.where(hit, neg, sel)
    ssum = functools.reduce(lambda a, b: a + b, svals)

    r_io = lax.broadcasted_iota(jnp.int32, (tm, tm), 0)
    c_io = lax.broadcasted_iota(jnp.int32, (tm, tm), 1)
    tri = jnp.where(c_io < r_io, 1.0, 0.0).astype(BF16)
    before = _dot(tri, onehot.astype(BF16)) + carry_ref[...]
    carry_ref[...] = carry_ref[...] + jnp.sum(onehot, axis=0, keepdims=True)
    cnt_ref[...] = carry_ref[...]

    idx_o = jnp.zeros(scores.shape, F32)
    rank_o = jnp.zeros(scores.shape, F32)
    gate_o = jnp.zeros(scores.shape, F32)
    for k in range(TOP_K):
        hit = lane == idxs[k]
        rank = jnp.sum(jnp.where(hit, before, 0.0), axis=-1, keepdims=True)
        idx_o = jnp.where(lane == k, idxs[k], idx_o)
        rank_o = jnp.where(lane == k, rank, rank_o)
        gate_o = jnp.where(lane == k, svals[k] / ssum * ROUTED_SCALE, gate_o)
    rows_out = idx_ref.shape[0]
    idx_ref[...] = idx_o.T[:rows_out].astype(jnp.int32)
    rank_ref[...] = rank_o.T[:rows_out].astype(jnp.int32)
    gate_ref[...] = gate_o

    gu = _dot(h_hi, wsg_ref[...])
    hs = gu.shape[1] // 2
    act = (jax.nn.silu(gu[:, :hs]) * gu[:, hs:]).astype(BF16)
    base_ref[...] = x_new + g2_ref[...] * _dot(act, wsd_ref[...])


def _post(att, gmn, x2, mod4, oa, w_out, nffn, wr_hi, wr_lo, rb, w_sg, w_sd, *, tm, tokens_per_batch, n_exp):
    t, d = x2.shape
    att_w, gm_w = att.shape[1], gmn.shape[1]
    tpb = tokens_per_batch // tm
    const = lambda shape: pl.BlockSpec(shape, lambda i: (0,) * len(shape))
    mrow = lambda j: pl.BlockSpec((None, None, 1, d), lambda i: (i // tpb, j, 0, 0))
    by_choice = jax.ShapeDtypeStruct((CHOICE_ROWS, t), jnp.int32)
    return pl.pallas_call(
        functools.partial(_post_kernel, n_exp=n_exp),
        out_shape=(jax.ShapeDtypeStruct((t, d), F32), jax.ShapeDtypeStruct((t, d // 2), jnp.int32),
                   by_choice, jax.ShapeDtypeStruct((t, LANES), F32), by_choice,
                   jax.ShapeDtypeStruct((1, LANES), F32)),
        grid=(t // tm,),
        in_specs=[pl.BlockSpec((tm, att_w), lambda i: (i, 0)),
                  pl.BlockSpec((tm, gm_w), lambda i: (i, 0)),
                  pl.BlockSpec((tm, d), lambda i: (i, 0)),
                  mrow(2), mrow(3), mrow(4), mrow(5),
                  const((1, att_w)), const(w_out.shape), const((1, d)),
                  const(wr_hi.shape), const(wr_lo.shape), const((1, LANES)),
                  const(w_sg.shape), const(w_sd.shape)],
        out_specs=(pl.BlockSpec((tm, d), lambda i: (i, 0)), pl.BlockSpec((tm, d // 2), lambda i: (i, 0)),
                   pl.BlockSpec((CHOICE_ROWS, tm), lambda i: (0, i)), pl.BlockSpec((tm, LANES), lambda i: (i, 0)),
                   pl.BlockSpec((CHOICE_ROWS, tm), lambda i: (0, i)), pl.BlockSpec((1, LANES), lambda i: (0, 0))),
        scratch_shapes=[pltpu.VMEM((1, LANES), F32)],
        compiler_params=_cparams("arbitrary"),
        name="post",
    )(att, gmn, x2, mod4, mod4, mod4, mod4, oa, w_out, nffn, wr_hi, wr_lo, rb, w_sg, w_sd)


def _sc_mesh_info():
    info = plsc.get_sparse_core_info()
    mesh = plsc.VectorSubcoreMesh(core_axis_name="c", subcore_axis_name="s")
    return mesh, info.num_cores, info.num_subcores, info.num_lanes


def _sc_params():
    cp = pltpu.CompilerParams()
    if "needs_layout_passes" in pltpu.CompilerParams.__dataclass_fields__:
        cp = dataclasses.replace(cp, needs_layout_passes=False)
    return cp


def _dispatch(h2p, idx_t, rank_t, offs, *, n_slots):
    t, dw = h2p.shape
    mesh, n_cores, n_sub, n_lanes = _sc_mesh_info()
    per_worker = t // (n_cores * n_sub)
    w = SC_WINDOW
    assert per_worker % w == 0 and w % n_lanes == 0

    @functools.partial(
        pl.kernel, mesh=mesh,
        out_type=(jax.ShapeDtypeStruct((n_slots, dw), jnp.int32), jax.ShapeDtypeStruct((CHOICE_ROWS, t), jnp.int32)),
        scratch_types=[pltpu.VMEM(offs.shape, jnp.int32)] + [pltpu.VMEM((TOP_K, w), jnp.int32)] * 3
        + [pltpu.VMEM((w, dw), jnp.int32), pltpu.SemaphoreType.DMA],
        compiler_params=_sc_params(),
    )
    def run(h_hbm, idx_hbm, rank_hbm, offs_hbm, xs_hbm, pos_hbm, offs_v, idx_v, rank_v, pos_v, rows_v, sem):
        base = (lax.axis_index("s") * n_cores + lax.axis_index("c")) * per_worker
        pltpu.sync_copy(offs_hbm, offs_v)

        @pl.loop(0, per_worker // w)
        def _(i):
            t0 = base + i * w
            pltpu.sync_copy(h_hbm.at[pl.ds(t0, w)], rows_v)
            for k in range(TOP_K):
                pltpu.sync_copy(idx_hbm.at[k, pl.ds(t0, w)], idx_v.at[k])
                pltpu.sync_copy(rank_hbm.at[k, pl.ds(t0, w)], rank_v.at[k])
            for k in range(TOP_K):
                for j in range(0, w, n_lanes):
                    group_start = plsc.load_gather(offs_v, [idx_v[k, pl.ds(j, n_lanes)]])
                    pos_v[k, pl.ds(j, n_lanes)] = group_start + rank_v[k, pl.ds(j, n_lanes)]
            copies = [pltpu.async_copy(rows_v, xs_hbm.at[pos_v.at[k]], sem) for k in range(TOP_K)]
            for k in range(TOP_K):
                pltpu.sync_copy(pos_v.at[k], pos_hbm.at[k, pl.ds(t0, w)])
            for cp in copies:
                cp.wait()

    return run(h2p, idx_t, rank_t, offs)


def _collect(ys, pos_t):
    t = pos_t.shape[1]
    dw = ys.shape[1]
    mesh, n_cores, n_sub, _ = _sc_mesh_info()
    per_worker = t // (n_cores * n_sub)
    w = SC_WINDOW
    assert per_worker % w == 0

    @functools.partial(
        pl.kernel, mesh=mesh,
        out_type=jax.ShapeDtypeStruct((TOP_K * t, dw), jnp.int32),
        scratch_types=[pltpu.VMEM((TOP_K, w), jnp.int32), pltpu.VMEM((w, dw), jnp.int32), pltpu.SemaphoreType.DMA],
        compiler_params=_sc_params(),
    )
    def run(ys_hbm, pos_hbm, r_hbm, pos_v, rows_v, sem):
        base = (lax.axis_index("s") * n_cores + lax.axis_index("c")) * per_worker

        @pl.loop(0, per_worker // w)
        def _(i):
            t0 = base + i * w
            for k in range(TOP_K):
                pltpu.sync_copy(pos_hbm.at[k, pl.ds(t0, w)], pos_v.at[k])
            for k in range(TOP_K):
                pltpu.async_copy(ys_hbm.at[pos_v.at[k]], rows_v, sem).wait()
                pltpu.sync_copy(rows_v, r_hbm.at[pl.ds(k * t + t0, w)])

    return run(ys, pos_t)


def _experts_kernel(te_ref, nv_ref, xs_ref, wgu_ref, wdn_ref, ys_ref):
    @pl.when(nv_ref[pl.program_id(0)] > 0)
    def _():
        half = xs_ref.shape[1]
        lo, hi = _unpack_rows(xs_ref[...])
        gu = _dot(lo.astype(BF16), wgu_ref[:half, :]) + _dot(hi.astype(BF16), wgu_ref[half:, :])
        hh = gu.shape[1] // 2
        act = (jax.nn.silu(gu[:, :hh]) * gu[:, hh:]).astype(BF16)
        ys_ref[...] = _pack_rows(_dot(act, wdn_ref[...]))


def _experts(tile_expert, tile_valid, xs, w_gu, w_dn, *, tr):
    nt = tile_expert.shape[0]
    _, d, h2w = w_gu.shape
    hh = w_dn.shape[1]
    wmap = lambda i, te, nv: (te[i], 0, 0)
    return pl.pallas_call(
        _experts_kernel,
        out_shape=jax.ShapeDtypeStruct(xs.shape, jnp.int32),
        grid_spec=pltpu.PrefetchScalarGridSpec(
            num_scalar_prefetch=2,
            grid=(nt,),
            in_specs=[pl.BlockSpec((tr, d // 2), lambda i, te, nv: (i, 0)),
                      pl.BlockSpec((None, d, h2w), wmap),
                      pl.BlockSpec((None, hh, d), wmap)],
            out_specs=pl.BlockSpec((tr, d // 2), lambda i, te, nv: (i, 0))),
        compiler_params=_cparams("arbitrary"),
        name="experts",
    )(tile_expert, tile_valid, xs, w_gu, w_dn)


def _combine_kernel(base_ref, g2_ref, gate_ref, *refs):
    r_refs, o_ref = refs[:TOP_K], refs[TOP_K]
    gate = gate_ref[...]
    half = base_ref.shape[1] // 2
    acc_lo = jnp.zeros((base_ref.shape[0], half), F32)
    acc_hi = jnp.zeros((base_ref.shape[0], half), F32)
    for k in range(TOP_K):
        lo, hi = _unpack_rows(r_refs[k][...])
        g = gate[:, k:k + 1]
        acc_lo = acc_lo + g * lo
        acc_hi = acc_hi + g * hi
    g2 = g2_ref[...]
    o_ref[:, :half] = base_ref[:, :half] + g2[:, :half] * acc_lo
    o_ref[:, half:] = base_ref[:, half:] + g2[:, half:] * acc_hi


def _combine(base, mod4, gate, r, *, tm, tokens_per_batch):
    t, d = base.shape
    tpb = tokens_per_batch // tm
    nblk = t // tm
    r_spec = lambda k: pl.BlockSpec((tm, d // 2), lambda i: (k * nblk + i, 0))
    return pl.pallas_call(
        _combine_kernel,
        out_shape=jax.ShapeDtypeStruct((t, d), F32),
        grid=(nblk,),
        in_specs=[pl.BlockSpec((tm, d), lambda i: (i, 0)),
                  pl.BlockSpec((None, None, 1, d), lambda i: (i // tpb, 5, 0, 0)),
                  pl.BlockSpec((tm, LANES), lambda i: (i, 0))] + [r_spec(k) for k in range(TOP_K)],
        out_specs=pl.BlockSpec((tm, d), lambda i: (i, 0)),
        compiler_params=_cparams("arbitrary"),
        name="combine",
    )(base, mod4, gate, *([r] * TOP_K))


def _rope_tables(length, head_dim):
    rows = length // GRID_W
    pairs_axis = head_dim // 4
    r = jnp.repeat(jnp.arange(rows, dtype=F32), GRID_W)
    col = jnp.tile(jnp.arange(GRID_W, dtype=F32), rows)
    inv = ROPE_THETA ** (-jnp.arange(pairs_axis, dtype=F32) / pairs_axis)
    ang = jnp.concatenate([r[:, None] * inv, col[:, None] * inv], axis=-1)
    cos, sin = jnp.cos(ang), jnp.sin(ang)
    cos_h = jnp.concatenate([cos, cos], axis=-1)
    sin_h = jnp.concatenate([-sin, sin], axis=-1)
    reps = LANES // head_dim
    return jnp.tile(cos_h, (1, reps)), jnp.tile(sin_h, (1, reps))


def kernel(x, c, ctx, c_ctx, w_ada, b_ada, norm_mix, w_in, q_norm, k_norm, gm_norm, w_spatial, b_spatial,
           out_norm_attn, out_norm_gm, w_out, norm_ffn, w_router, router_bias, w_exp_gu, w_exp_down,
           w_sh_gu, w_sh_down):
    assert w_ada.shape[0] == 1, "single-layer kernel"
    b, length, d = x.shape
    head_dim = q_norm.shape[-1]
    att_w = out_norm_attn.shape[-1]
    gm_w = out_norm_gm.shape[-1]
    gm_groups = w_spatial.shape[1]
    gm_hd = gm_w // gm_groups
    n_exp = w_router.shape[-1]
    t = b * length
    assert head_dim == LANES // 2 and gm_hd == LANES // 2 and w_spatial.shape[-1] == CHUNK
    assert n_exp <= LANES and att_w % MXU_EDGE == 0 and gm_w % MXU_EDGE == 0

    mod_rows = 16
    c_all = jnp.zeros((mod_rows, d), F32).at[:b].set(c).at[b].set(c_ctx)
    mod = _ada(c_all, w_ada[0], b_ada)
    mod4 = mod.reshape(mod_rows, N_MOD, 1, d)

    w_in_b = w_in[0].astype(BF16)
    bd = (jnp.arange(MXU_EDGE)[:, None] // head_dim == jnp.arange(MXU_EDGE)[None, :] // head_dim)
    bd = (bd.astype(F32) / head_dim).astype(BF16)
    qg = jnp.tile(q_norm[0] * head_dim ** -0.5, att_w // head_dim)[None, :]
    kg = jnp.tile(k_norm[0], LANES // head_dim)[None, :]
    gmg = jnp.tile(gm_norm[0], gm_groups)[None, :]
    ws = w_spatial[0].astype(BF16).reshape(gm_groups // 2, 2 * CHUNK, CHUNK)
    bs = jnp.repeat(b_spatial[0].T, gm_hd, axis=1)
    cos_t, sin_t = _rope_tables(length, head_dim)

    k_c, v_c = _ctx_kv(ctx, mod4, b, norm_mix, w_in_b[:, att_w:att_w + 2 * LANES], kg, bd)
    q, k_l, v_l, gmn = _inproj(x, mod4, norm_mix, w_in_b, cos_t, sin_t, qg, kg, gmg, bd, ws, bs,
                               out_norm_gm, att_w=att_w, gm_w=gm_w, tm=min(512, length))
    att = _attention(q, k_c, k_l, v_c, v_l, tq=min(256, length))

    wr = jnp.zeros((d, LANES), F32).at[:, :n_exp].set(w_router[0])
    wr_hi = wr.astype(BF16)
    wr_lo = (wr - wr_hi.astype(F32)).astype(BF16)
    rb = jnp.zeros((1, LANES), F32).at[0, :n_exp].set(router_bias[0])
    tm_post = min(512, length)
    base, h2p, idx_t, gate, rank_t, cnt = _post(
        att.reshape(t, att_w), gmn.reshape(t, gm_w), x.reshape(t, d), mod4, out_norm_attn, w_out[0].astype(BF16),
        norm_ffn, wr_hi, wr_lo, rb, w_sh_gu[0].astype(BF16), w_sh_down[0].astype(BF16),
        tm=tm_post, tokens_per_batch=length, n_exp=n_exp)

    tr = 256
    nt = t * TOP_K // tr + n_exp
    counts = cnt[0, :n_exp].astype(jnp.int32)
    padded = (counts + tr - 1) // tr * tr
    ends = jnp.cumsum(padded)
    offs = ends - padded
    tile_start = jnp.arange(nt, dtype=jnp.int32) * tr
    tile_expert = jnp.minimum(jnp.sum(tile_start[:, None] >= ends[None, :], axis=1), n_exp - 1).astype(jnp.int32)
    hot = tile_expert[:, None] == jnp.arange(n_exp, dtype=jnp.int32)[None, :]
    group_end = jnp.sum(jnp.where(hot, (offs + counts)[None, :], 0), axis=1)
    tile_valid = jnp.clip(group_end - tile_start, 0, tr).astype(jnp.int32)
    offs_pad = jnp.zeros((LANES,), jnp.int32).at[:n_exp].set(offs)

    xs, pos_t = _dispatch(h2p, idx_t, rank_t, offs_pad, n_slots=nt * tr)
    ys = _experts(tile_expert, tile_valid, xs, w_exp_gu[0].astype(BF16), w_exp_down[0].astype(BF16), tr=tr)
    r = _collect(ys, pos_t)
    out = _combine(base, mod4, gate, r, tm=min(256, length), tokens_per_batch=length)
    return out.reshape(b, length, d)
```

```python
import dataclasses
import functools

import jax
import jax.numpy as jnp
from jax import lax
from jax.experimental import pallas as pl
from jax.experimental.pallas import tpu as pltpu
from jax.experimental.pallas import tpu_sc as plsc

F32 = jnp.float32
BF16 = jnp.bfloat16

EPS = 1e-6
GRID_W = 64
ROPE_THETA = 10000.0
ATT_KV_HEADS = 2
TOP_K = 6
ROUTED_SCALE = 2.5
N_MOD = 6
CHUNK = 128
LOG2_E = 1.4426950408889634

LANES = 128
MXU_EDGE = 256
VMEM_LIMIT_BYTES = 56 * 1024 * 1024
SUBLANES = 8
CHOICE_ROWS = SUBLANES
SC_WINDOW = 64


def _cparams(*sem):
    return pltpu.CompilerParams(dimension_semantics=sem, vmem_limit_bytes=VMEM_LIMIT_BYTES)


def _dot(a, b):
    return jnp.dot(a, b, preferred_element_type=F32)


def _dot_nt(a, b):
    return lax.dot_general(a, b, (((1,), (1,)), ((), ())), preferred_element_type=F32)


def _rms(x, g):
    return x * lax.rsqrt(jnp.mean(x * x, axis=-1, keepdims=True) + EPS) * g


def _group_mean_sq(x, bd_ref):
    x2 = (x * x).astype(BF16)
    parts = [_dot(x2[:, i:i + MXU_EDGE], bd_ref[...]) for i in range(0, x.shape[1], MXU_EDGE)]
    return parts[0] if len(parts) == 1 else jnp.concatenate(parts, axis=-1)


def _dup_halves(x, lane):
    xr = pltpu.roll(x, LANES // 2, axis=1)
    lo = lane < LANES // 2
    return jnp.where(lo, x, xr), jnp.where(lo, xr, x)


def _with_ones(x, lane):
    xr = pltpu.roll(x, LANES // 2, axis=1)
    lo = lane < LANES // 2
    return jnp.concatenate([jnp.where(lo, x, 1.0), jnp.where(lo, xr, 1.0)], axis=-1)


_HI16 = 0xFFFF0000


def _pack_rows(x):
    half = x.shape[1] // 2
    rounded = x.astype(BF16).astype(F32)
    bits = lax.bitcast_convert_type(rounded, jnp.uint32)
    word = (bits[:, :half] >> 16) | (bits[:, half:] & jnp.uint32(_HI16))
    return lax.bitcast_convert_type(word, jnp.int32)


def _unpack_rows(w):
    bits = lax.bitcast_convert_type(w, jnp.uint32)
    lo = lax.bitcast_convert_type(bits << 16, F32)
    hi = lax.bitcast_convert_type(bits & jnp.uint32(_HI16), F32)
    return lo, hi


def _ada_kernel(c_ref, w_ref, b_ref, o_ref):
    s = jax.nn.silu(c_ref[...])
    o_ref[...] = jnp.dot(s, w_ref[...], precision=lax.Precision.HIGHEST,
                         preferred_element_type=F32) + b_ref[...]


def _ada(c_all, w_ada, b_ada):
    rows, d = c_all.shape
    n = w_ada.shape[1]
    bn = 512
    return pl.pallas_call(
        _ada_kernel,
        out_shape=jax.ShapeDtypeStruct((rows, n), F32),
        grid=(n // bn,),
        in_specs=[pl.BlockSpec((rows, d), lambda j: (0, 0)),
                  pl.BlockSpec((d, bn), lambda j: (0, j)),
                  pl.BlockSpec((1, bn), lambda j: (0, j))],
        out_specs=pl.BlockSpec((rows, bn), lambda j: (0, j)),
        compiler_params=_cparams("arbitrary"),
        name="ada",
    )(c_all, w_ada, b_ada)


def _ctx_kv_kernel(x_ref, sh_ref, sc_ref, nmix_ref, w_ref, kg_ref, bd_ref, k_ref, v_ref):
    h = _rms(x_ref[...], nmix_ref[...]) * (1.0 + sc_ref[...]) + sh_ref[...]
    p = _dot(h.astype(BF16), w_ref[...])
    ms = _group_mean_sq(p, bd_ref)
    k = p[:, :LANES] * lax.rsqrt(ms[:, :LANES] + EPS) * kg_ref[...]
    v = p[:, LANES:]
    lane = lax.broadcasted_iota(jnp.int32, k.shape, 1)
    ka, kb = _dup_halves(k, lane)
    k_ref[...] = jnp.concatenate([ka, kb], axis=-1).astype(BF16)
    v_ref[...] = _with_ones(v, lane).astype(BF16)


def _ctx_kv(ctx, mod4, ctx_row, nmix, w_kv, kg, bd):
    b, n_ctx, d = ctx.shape
    kv_w = 2 * LANES
    return pl.pallas_call(
        _ctx_kv_kernel,
        out_shape=(jax.ShapeDtypeStruct((b, n_ctx, kv_w), BF16),) * 2,
        grid=(b,),
        in_specs=[pl.BlockSpec((None, n_ctx, d), lambda i: (i, 0, 0)),
                  pl.BlockSpec((None, None, 1, d), lambda i: (ctx_row, 0, 0, 0)),
                  pl.BlockSpec((None, None, 1, d), lambda i: (ctx_row, 1, 0, 0)),
                  pl.BlockSpec((1, d), lambda i: (0, 0)),
                  pl.BlockSpec(w_kv.shape, lambda i: (0, 0)),
                  pl.BlockSpec((1, LANES), lambda i: (0, 0)),
                  pl.BlockSpec(bd.shape, lambda i: (0, 0))],
        out_specs=(pl.BlockSpec((None, n_ctx, kv_w), lambda i: (i, 0, 0)),) * 2,
        compiler_params=_cparams("arbitrary"),
        name="ctx_kv",
    )(ctx, mod4, mod4, nmix, w_kv, kg, bd)


def _inproj_kernel(x_ref, sh_ref, sc_ref, nmix_ref, w_ref, cos_ref, sin_ref, qg_ref, kg_ref, gmg_ref,
                   bd_ref, ws_ref, bs_ref, og_ref, q_ref, k_ref, v_ref, gm_ref, *, att_w, gm_w):
    tm = x_ref.shape[0]
    h = _rms(x_ref[...], nmix_ref[...]) * (1.0 + sc_ref[...]) + sh_ref[...]
    p = _dot(h.astype(BF16), w_ref[...])

    cos1, sin1 = cos_ref[...], sin_ref[...]
    lane1 = lax.broadcasted_iota(jnp.int32, (tm, LANES), 1)

    def rope(xn, cosw, sinw, lanew):
        w = xn.shape[1]
        fwd = pltpu.roll(xn, w - 32, axis=1)
        bwd = pltpu.roll(xn, 32, axis=1)
        swapped = jnp.where((lanew & 63) < 32, fwd, bwd)
        return xn * cosw + swapped * sinw

    q = p[:, :att_w]
    qn = q * lax.rsqrt(_group_mean_sq(q, bd_ref) + EPS) * qg_ref[...]
    reps = att_w // LANES
    cosq = jnp.concatenate([cos1] * reps, axis=-1)
    sinq = jnp.concatenate([sin1] * reps, axis=-1)
    laneq = lax.broadcasted_iota(jnp.int32, (tm, att_w), 1)
    q_ref[...] = rope(qn, cosq, sinq, laneq).astype(BF16)

    kv = p[:, att_w:att_w + 2 * LANES]
    ms = _group_mean_sq(kv, bd_ref)
    kn = kv[:, :LANES] * lax.rsqrt(ms[:, :LANES] + EPS) * kg_ref[...]
    k = rope(kn, cos1, sin1, lane1)
    v = kv[:, LANES:]
    ka, kb = _dup_halves(k, lane1)
    k_ref[...] = jnp.concatenate([ka, kb], axis=-1).astype(BF16)
    v_ref[...] = _with_ones(v, lane1).astype(BF16)

    u0 = att_w + 2 * LANES
    u = jax.nn.gelu(p[:, u0:u0 + gm_w])
    gt = jax.nn.gelu(p[:, u0 + gm_w:u0 + 2 * gm_w])
    gt = (gt * lax.rsqrt(_group_mean_sq(gt, bd_ref) + EPS) * gmg_ref[...]).astype(BF16)
    lo_half = lax.broadcasted_iota(jnp.int32, (CHUNK, LANES), 1) < LANES // 2
    rows = []
    for c in range(tm // CHUNK):
        blocks = []
        for j in range(gm_w // LANES):
            g = gt[c * CHUNK:(c + 1) * CHUNK, j * LANES:(j + 1) * LANES]
            r = _dot(ws_ref[j], g)
            blocks.append(jnp.where(lo_half, r[:CHUNK], r[CHUNK:]))
        rows.append(jnp.concatenate(blocks, axis=-1) + bs_ref[...])
    mixed = jnp.concatenate(rows, axis=0)
    gm_ref[...] = _rms(u * mixed, og_ref[...]).astype(BF16)


def _inproj(x, mod4, nmix, w_in, cos_t, sin_t, qg, kg, gmg, bd, ws, bs, og, *, att_w, gm_w, tm):
    b, length, d = x.shape
    kv_w = 2 * LANES
    n_in = w_in.shape[1]
    const = lambda shape: pl.BlockSpec(shape, lambda i, j: (0,) * len(shape))
    return pl.pallas_call(
        functools.partial(_inproj_kernel, att_w=att_w, gm_w=gm_w),
        out_shape=(jax.ShapeDtypeStruct((b, length, att_w), BF16),
                   jax.ShapeDtypeStruct((b, length, kv_w), BF16),
                   jax.ShapeDtypeStruct((b, length, kv_w), BF16),
                   jax.ShapeDtypeStruct((b, length, gm_w), BF16)),
        grid=(b, length // tm),
        in_specs=[pl.BlockSpec((None, tm, d), lambda i, j: (i, j, 0)),
                  pl.BlockSpec((None, None, 1, d), lambda i, j: (i, 0, 0, 0)),
                  pl.BlockSpec((None, None, 1, d), lambda i, j: (i, 1, 0, 0)),
                  const((1, d)),
                  const((d, n_in)),
                  pl.BlockSpec((tm, LANES), lambda i, j: (j, 0)),
                  pl.BlockSpec((tm, LANES), lambda i, j: (j, 0)),
                  const((1, att_w)), const((1, LANES)), const((1, gm_w)),
                  const(bd.shape), const(ws.shape), const(bs.shape), const((1, gm_w))],
        out_specs=(pl.BlockSpec((None, tm, att_w), lambda i, j: (i, j, 0)),
                   pl.BlockSpec((None, tm, kv_w), lambda i, j: (i, j, 0)),
                   pl.BlockSpec((None, tm, kv_w), lambda i, j: (i, j, 0)),
                   pl.BlockSpec((None, tm, gm_w), lambda i, j: (i, j, 0))),
        compiler_params=_cparams("arbitrary", "arbitrary"),
        name="inproj",
    )(x, mod4, mod4, nmix, w_in, cos_t, sin_t, qg, kg, gmg, bd, ws, bs, og)


def _attn_kernel(q_ref, kc_ref, kl_ref, vc_ref, vl_ref, o_ref, s_buf, p_buf, *, tq):
    n_ctx = kc_ref.shape[0]
    heads = q_ref.shape[1] // (LANES // 2)
    nj = q_ref.shape[0] // tq
    half = LANES // 2
    lo = lax.broadcasted_iota(jnp.int32, (tq, LANES), 1) < half

    s_buf[...] = jnp.zeros_like(s_buf)
    p_buf[...] = jnp.ones_like(p_buf)

    def scores(j, g):
        q = q_ref[pl.ds(j * tq, tq), (g // 2) * LANES:(g // 2 + 1) * LANES]
        zero = jnp.zeros_like(q)
        qh = jnp.where(lo, q, zero) if g % 2 == 0 else jnp.where(lo, zero, q)
        s_buf[g % 2, :, :n_ctx] = _dot_nt(qh, kc_ref[...])
        s_buf[g % 2, :, n_ctx:] = _dot_nt(qh, kl_ref[...])

    def probs(g):
        s = s_buf[g % 2]
        p_buf[g % 2] = jnp.exp2(s - jnp.max(s, axis=-1, keepdims=True)).astype(BF16)

    def output(j, g):
        p = p_buf[g % 2]
        t = _dot(p[:, :n_ctx], vc_ref[...]) + _dot(p[:, n_ctx:], vl_ref[...])
        r = pltpu.roll(t, half, axis=1)
        rows = pl.ds(j * tq, tq)
        if g % 2 == 0:
            o_ref[rows, g * half:(g + 1) * half] = (t / r)[:, :half].astype(o_ref.dtype)
        else:
            o_ref[rows, g * half:(g + 1) * half] = (r / t)[:, half:].astype(o_ref.dtype)

    def trip(j, carry):
        j_prev = jnp.maximum(j - 1, 0)
        for g in range(heads):
            scores(j, g)
            probs((g - 1) % heads)
            output(j if g >= 2 else j_prev, (g - 2) % heads)
        return carry

    lax.fori_loop(0, nj, trip, 0)
    probs(heads - 1)
    output(nj - 1, heads - 2)
    output(nj - 1, heads - 1)


def _attention(q, k_c, k_l, v_c, v_l, *, tq):
    b, length, att_w = q.shape
    n_ctx = k_c.shape[1]
    group_w = att_w // ATT_KV_HEADS
    assert (group_w // (LANES // 2)) % 2 == 0 and length % tq == 0
    kv_blk = lambda n: pl.BlockSpec((None, n, LANES), lambda i, h: (i, 0, h))
    return pl.pallas_call(
        functools.partial(_attn_kernel, tq=tq),
        out_shape=jax.ShapeDtypeStruct((b, length, att_w), BF16),
        grid=(b, ATT_KV_HEADS),
        in_specs=[pl.BlockSpec((None, length, group_w), lambda i, h: (i, 0, h)),
                  kv_blk(n_ctx), kv_blk(length), kv_blk(n_ctx), kv_blk(length)],
        out_specs=pl.BlockSpec((None, length, group_w), lambda i, h: (i, 0, h)),
        scratch_shapes=[pltpu.VMEM((2, tq, n_ctx + length), F32), pltpu.VMEM((2, tq, n_ctx + length), BF16)],
        compiler_params=_cparams("arbitrary", "arbitrary"),
        name="attn",
    )(q, k_c, k_l, v_c, v_l)


def _post_kernel(att_ref, gm_ref, x_ref, g1_ref, sh2_ref, sc2_ref, g2_ref, oa_ref, wo_ref, nffn_ref,
                 wrh_ref, wrl_ref, rb_ref, wsg_ref, wsd_ref,
                 base_ref, h2p_ref, idx_ref, gate_ref, rank_ref, cnt_ref, carry_ref, *, n_exp):
    i = pl.program_id(0)
    tm = x_ref.shape[0]

    @pl.when(i == 0)
    def _():
        carry_ref[...] = jnp.zeros_like(carry_ref)

    att_n = _rms(att_ref[...].astype(F32), oa_ref[...]).astype(BF16)
    y = _dot(jnp.concatenate([att_n, gm_ref[...]], axis=-1), wo_ref[...])
    x_new = x_ref[...] + g1_ref[...] * y
    h2 = _rms(x_new, nffn_ref[...]) * (1.0 + sc2_ref[...]) + sh2_ref[...]
    h2p_ref[...] = _pack_rows(h2)

    h_hi = h2.astype(BF16)
    h_lo = (h2 - h_hi.astype(F32)).astype(BF16)
    logits = _dot(h_hi, wrh_ref[...]) + (_dot(h_lo, wrh_ref[...]) + _dot(h_hi, wrl_ref[...]))
    scores = jax.nn.sigmoid(logits)
    lane = lax.broadcasted_iota(jnp.int32, scores.shape, 1).astype(F32)
    neg = jnp.float32(-jnp.inf)
    sel = jnp.where(lane < n_exp, scores + rb_ref[...], neg)

    onehot = jnp.zeros(scores.shape, F32)
    idxs, svals = [], []
    for _k in range(TOP_K):
        m = jnp.max(sel, axis=-1, keepdims=True)
        idx = jnp.min(jnp.where(sel == m, lane, float(LANES)), axis=-1, keepdims=True)
        hit = lane == idx
        idxs.append(idx)
        svals.append(jnp.sum(jnp.where(hit, scores, 0.0), axis=-1, keepdims=True))
        onehot = onehot + jnp.where(hit, 1.0, 0.0)
        sel = jnp.where(hit, neg, sel)
    ssum = functools.reduce(lambda a, b: a + b, svals)

    r_io = lax.broadcasted_iota(jnp.int32, (tm, tm), 0)
    c_io = lax.broadcasted_iota(jnp.int32, (tm, tm), 1)
    tri = jnp.where(c_io < r_io, 1.0, 0.0).astype(BF16)
    before = _dot(tri, onehot.astype(BF16)) + carry_ref[...]
    carry_ref[...] = carry_ref[...] + jnp.sum(onehot, axis=0, keepdims=True)
    cnt_ref[...] = carry_ref[...]

    idx_o = jnp.zeros(scores.shape, F32)
    rank_o = jnp.zeros(scores.shape, F32)
    gate_o = jnp.zeros(scores.shape, F32)
    for k in range(TOP_K):
        hit = lane == idxs[k]
        rank = jnp.sum(jnp.where(hit, before, 0.0), axis=-1, keepdims=True)
        idx_o = jnp.where(lane == k, idxs[k], idx_o)
        rank_o = jnp.where(lane == k, rank, rank_o)
        gate_o = jnp.where(lane == k, svals[k] / ssum * ROUTED_SCALE, gate_o)
    rows_out = idx_ref.shape[0]
    idx_ref[...] = idx_o.T[:rows_out].astype(jnp.int32)
    rank_ref[...] = rank_o.T[:rows_out].astype(jnp.int32)
    gate_ref[...] = gate_o

    gu = _dot(h_hi, wsg_ref[...])
    hs = gu.shape[1] // 2
    act = (jax.nn.silu(gu[:, :hs]) * gu[:, hs:]).astype(BF16)
    base_ref[...] = x_new + g2_ref[...] * _dot(act, wsd_ref[...])


def _post(att, gmn, x2, mod4, oa, w_out, nffn, wr_hi, wr_lo, rb, w_sg, w_sd, *, tm, tokens_per_batch, n_exp):
    t, d = x2.shape
    att_w, gm_w = att.shape[1], gmn.shape[1]
    tpb = tokens_per_batch // tm
    const = lambda shape: pl.BlockSpec(shape, lambda i: (0,) * len(shape))
    mrow = lambda j: pl.BlockSpec((None, None, 1, d), lambda i: (i // tpb, j, 0, 0))
    by_choice = jax.ShapeDtypeStruct((CHOICE_ROWS, t), jnp.int32)
    return pl.pallas_call(
        functools.partial(_post_kernel, n_exp=n_exp),
        out_shape=(jax.ShapeDtypeStruct((t, d), F32), jax.ShapeDtypeStruct((t, d // 2), jnp.int32),
                   by_choice, jax.ShapeDtypeStruct((t, LANES), F32), by_choice,
                   jax.ShapeDtypeStruct((1, LANES), F32)),
        grid=(t // tm,),
        in_specs=[pl.BlockSpec((tm, att_w), lambda i: (i, 0)),
                  pl.BlockSpec((tm, gm_w), lambda i: (i, 0)),
                  pl.BlockSpec((tm, d), lambda i: (i, 0)),
                  mrow(2), mrow(3), mrow(4), mrow(5),
                  const((1, att_w)), const(w_out.shape), const((1, d)),
                  const(wr_hi.shape), const(wr_lo.shape), const((1, LANES)),
                  const(w_sg.shape), const(w_sd.shape)],
        out_specs=(pl.BlockSpec((tm, d), lambda i: (i, 0)), pl.BlockSpec((tm, d // 2), lambda i: (i, 0)),
                   pl.BlockSpec((CHOICE_ROWS, tm), lambda i: (0, i)), pl.BlockSpec((tm, LANES), lambda i: (i, 0)),
                   pl.BlockSpec((CHOICE_ROWS, tm), lambda i: (0, i)), pl.BlockSpec((1, LANES), lambda i: (0, 0))),
        scratch_shapes=[pltpu.VMEM((1, LANES), F32)],
        compiler_params=_cparams("arbitrary"),
        name="post",
    )(att, gmn, x2, mod4, mod4, mod4, mod4, oa, w_out, nffn, wr_hi, wr_lo, rb, w_sg, w_sd)


def _sc_mesh_info():
    info = plsc.get_sparse_core_info()
    mesh = plsc.VectorSubcoreMesh(core_axis_name="c", subcore_axis_name="s")
    return mesh, info.num_cores, info.num_subcores, info.num_lanes


def _sc_params():
    cp = pltpu.CompilerParams()
    if "needs_layout_passes" in pltpu.CompilerParams.__dataclass_fields__:
        cp = dataclasses.replace(cp, needs_layout_passes=False)
    return cp


def _dispatch(h2p, idx_t, rank_t, offs, *, n_slots):
    t, dw = h2p.shape
    mesh, n_cores, n_sub, n_lanes = _sc_mesh_info()
    per_worker = t // (n_cores * n_sub)
    w = SC_WINDOW
    assert per_worker % w == 0 and w % n_lanes == 0

    @functools.partial(
        pl.kernel, mesh=mesh,
        out_type=(jax.ShapeDtypeStruct((n_slots, dw), jnp.int32), jax.ShapeDtypeStruct((CHOICE_ROWS, t), jnp.int32)),
        scratch_types=[pltpu.VMEM(offs.shape, jnp.int32)] + [pltpu.VMEM((TOP_K, w), jnp.int32)] * 3
        + [pltpu.VMEM((w, dw), jnp.int32), pltpu.SemaphoreType.DMA],
        compiler_params=_sc_params(),
    )
    def run(h_hbm, idx_hbm, rank_hbm, offs_hbm, xs_hbm, pos_hbm, offs_v, idx_v, rank_v, pos_v, rows_v, sem):
        base = (lax.axis_index("s") * n_cores + lax.axis_index("c")) * per_worker
        pltpu.sync_copy(offs_hbm, offs_v)

        @pl.loop(0, per_worker // w)
        def _(i):
            t0 = base + i * w
            pltpu.sync_copy(h_hbm.at[pl.ds(t0, w)], rows_v)
            for k in range(TOP_K):
                pltpu.sync_copy(idx_hbm.at[k, pl.ds(t0, w)], idx_v.at[k])
                pltpu.sync_copy(rank_hbm.at[k, pl.ds(t0, w)], rank_v.at[k])
            for k in range(TOP_K):
                for j in range(0, w, n_lanes):
                    group_start = plsc.load_gather(offs_v, [idx_v[k, pl.ds(j, n_lanes)]])
                    pos_v[k, pl.ds(j, n_lanes)] = group_start + rank_v[k, pl.ds(j, n_lanes)]
            copies = [pltpu.async_copy(rows_v, xs_hbm.at[pos_v.at[k]], sem) for k in range(TOP_K)]
            for k in range(TOP_K):
                pltpu.sync_copy(pos_v.at[k], pos_hbm.at[k, pl.ds(t0, w)])
            for cp in copies:
                cp.wait()

    return run(h2p, idx_t, rank_t, offs)


def _collect(ys, pos_t):
    t = pos_t.shape[1]
    dw = ys.shape[1]
    mesh, n_cores, n_sub, _ = _sc_mesh_info()
    per_worker = t // (n_cores * n_sub)
    w = SC_WINDOW
    assert per_worker % w == 0

    @functools.partial(
        pl.kernel, mesh=mesh,
        out_type=jax.ShapeDtypeStruct((TOP_K * t, dw), jnp.int32),
        scratch_types=[pltpu.VMEM((TOP_K, w), jnp.int32), pltpu.VMEM((w, dw), jnp.int32), pltpu.SemaphoreType.DMA],
        compiler_params=_sc_params(),
    )
    def run(ys_hbm, pos_hbm, r_hbm, pos_v, rows_v, sem):
        base = (lax.axis_index("s") * n_cores + lax.axis_index("c")) * per_worker

        @pl.loop(0, per_worker // w)
        def _(i):
            t0 = base + i * w
            for k in range(TOP_K):
                pltpu.sync_copy(pos_hbm.at[k, pl.ds(t0, w)], pos_v.at[k])
            for k in range(TOP_K):
                pltpu.async_copy(ys_hbm.at[pos_v.at[k]], rows_v, sem).wait()
                pltpu.sync_copy(rows_v, r_hbm.at[pl.ds(k * t + t0, w)])

    return run(ys, pos_t)


def _experts_kernel(te_ref, nv_ref, xs_ref, wgu_ref, wdn_ref, ys_ref):
    @pl.when(nv_ref[pl.program_id(0)] > 0)
    def _():
        half = xs_ref.shape[1]
        lo, hi = _unpack_rows(xs_ref[...])
        gu = _dot(lo.astype(BF16), wgu_ref[:half, :]) + _dot(hi.astype(BF16), wgu_ref[half:, :])
        hh = gu.shape[1] // 2
        act = (jax.nn.silu(gu[:, :hh]) * gu[:, hh:]).astype(BF16)
        ys_ref[...] = _pack_rows(_dot(act, wdn_ref[...]))


def _experts(tile_expert, tile_valid, xs, w_gu, w_dn, *, tr):
    nt = tile_expert.shape[0]
    _, d, h2w = w_gu.shape
    hh = w_dn.shape[1]
    wmap = lambda i, te, nv: (te[i], 0, 0)
    return pl.pallas_call(
        _experts_kernel,
        out_shape=jax.ShapeDtypeStruct(xs.shape, jnp.int32),
        grid_spec=pltpu.PrefetchScalarGridSpec(
            num_scalar_prefetch=2,
            grid=(nt,),
            in_specs=[pl.BlockSpec((tr, d // 2), lambda i, te, nv: (i, 0)),
                      pl.BlockSpec((None, d, h2w), wmap),
                      pl.BlockSpec((None, hh, d), wmap)],
            out_specs=pl.BlockSpec((tr, d // 2), lambda i, te, nv: (i, 0))),
        compiler_params=_cparams("arbitrary"),
        name="experts",
    )(tile_expert, tile_valid, xs, w_gu, w_dn)


def _combine_kernel(base_ref, g2_ref, gate_ref, *refs):
    r_refs, o_ref = refs[:TOP_K], refs[TOP_K]
    gate = gate_ref[...]
    half = base_ref.shape[1] // 2
    acc_lo = jnp.zeros((base_ref.shape[0], half), F32)
    acc_hi = jnp.zeros((base_ref.shape[0], half), F32)
    for k in range(TOP_K):
        lo, hi = _unpack_rows(r_refs[k][...])
        g = gate[:, k:k + 1]
        acc_lo = acc_lo + g * lo
        acc_hi = acc_hi + g * hi
    g2 = g2_ref[...]
    o_ref[:, :half] = base_ref[:, :half] + g2[:, :half] * acc_lo
    o_ref[:, half:] = base_ref[:, half:] + g2[:, half:] * acc_hi


def _combine(base, mod4, gate, r, *, tm, tokens_per_batch):
    t, d = base.shape
    tpb = tokens_per_batch // tm
    nblk = t // tm
    r_spec = lambda k: pl.BlockSpec((tm, d // 2), lambda i: (k * nblk + i, 0))
    return pl.pallas_call(
        _combine_kernel,
        out_shape=jax.ShapeDtypeStruct((t, d), F32),
        grid=(nblk,),
        in_specs=[pl.BlockSpec((tm, d), lambda i: (i, 0)),
                  pl.BlockSpec((None, None, 1, d), lambda i: (i // tpb, 5, 0, 0)),
                  pl.BlockSpec((tm, LANES), lambda i: (i, 0))] + [r_spec(k) for k in range(TOP_K)],
        out_specs=pl.BlockSpec((tm, d), lambda i: (i, 0)),
        compiler_params=_cparams("arbitrary"),
        name="combine",
    )(base, mod4, gate, *([r] * TOP_K))


def _rope_tables(length, head_dim):
    rows = length // GRID_W
    pairs_axis = head_dim // 4
    r = jnp.repeat(jnp.arange(rows, dtype=F32), GRID_W)
    col = jnp.tile(jnp.arange(GRID_W, dtype=F32), rows)
    inv = ROPE_THETA ** (-jnp.arange(pairs_axis, dtype=F32) / pairs_axis)
    ang = jnp.concatenate([r[:, None] * inv, col[:, None] * inv], axis=-1)
    cos, sin = jnp.cos(ang), jnp.sin(ang)
    cos_h = jnp.concatenate([cos, cos], axis=-1)
    sin_h = jnp.concatenate([-sin, sin], axis=-1)
    reps = LANES // head_dim
    return jnp.tile(cos_h, (1, reps)), jnp.tile(sin_h, (1, reps))


def kernel(x, c, ctx, c_ctx, w_ada, b_ada, norm_mix, w_in, q_norm, k_norm, gm_norm, w_spatial, b_spatial,
           out_norm_attn, out_norm_gm, w_out, norm_ffn, w_router, router_bias, w_exp_gu, w_exp_down,
           w_sh_gu, w_sh_down):
    assert w_ada.shape[0] == 1, "single-layer kernel"
    b, length, d = x.shape
    head_dim = q_norm.shape[-1]
    att_w = out_norm_attn.shape[-1]
    gm_w = out_norm_gm.shape[-1]
    gm_groups = w_spatial.shape[1]
    gm_hd = gm_w // gm_groups
    n_exp = w_router.shape[-1]
    t = b * length
    assert head_dim == LANES // 2 and gm_hd == LANES // 2 and w_spatial.shape[-1] == CHUNK
    assert n_exp <= LANES and att_w % MXU_EDGE == 0 and gm_w % MXU_EDGE == 0

    mod_rows = 16
    c_all = jnp.zeros((mod_rows, d), F32).at[:b].set(c).at[b].set(c_ctx)
    mod = _ada(c_all, w_ada[0], b_ada)
    mod4 = mod.reshape(mod_rows, N_MOD, 1, d)

    w_in_b = w_in[0].astype(BF16)
    bd = (jnp.arange(MXU_EDGE)[:, None] // head_dim == jnp.arange(MXU_EDGE)[None, :] // head_dim)
    bd = (bd.astype(F32) / head_dim).astype(BF16)
    qg = jnp.tile(q_norm[0] * (head_dim ** -0.5 * LOG2_E), att_w // head_dim)[None, :]
    kg = jnp.tile(k_norm[0], LANES // head_dim)[None, :]
    gmg = jnp.tile(gm_norm[0], gm_groups)[None, :]
    ws = w_spatial[0].astype(BF16).reshape(gm_groups // 2, 2 * CHUNK, CHUNK)
    bs = jnp.repeat(b_spatial[0].T, gm_hd, axis=1)
    cos_t, sin_t = _rope_tables(length, head_dim)

    k_c, v_c = _ctx_kv(ctx, mod4, b, norm_mix, w_in_b[:, att_w:att_w + 2 * LANES], kg, bd)
    q, k_l, v_l, gmn = _inproj(x, mod4, norm_mix, w_in_b, cos_t, sin_t, qg, kg, gmg, bd, ws, bs,
                               out_norm_gm, att_w=att_w, gm_w=gm_w, tm=min(512, length))
    att = _attention(q, k_c, k_l, v_c, v_l, tq=min(256, length))

    wr = jnp.zeros((d, LANES), F32).at[:, :n_exp].set(w_router[0])
    wr_hi = wr.astype(BF16)
    wr_lo = (wr - wr_hi.astype(F32)).astype(BF16)
    rb = jnp.zeros((1, LANES), F32).at[0, :n_exp].set(router_bias[0])
    tm_post = min(512, length)
    base, h2p, idx_t, gate, rank_t, cnt = _post(
        att.reshape(t, att_w), gmn.reshape(t, gm_w), x.reshape(t, d), mod4, out_norm_attn, w_out[0].astype(BF16),
        norm_ffn, wr_hi, wr_lo, rb, w_sh_gu[0].astype(BF16), w_sh_down[0].astype(BF16),
        tm=tm_post, tokens_per_batch=length, n_exp=n_exp)

    tr = 512
    nt = t * TOP_K // tr + n_exp
    counts = cnt[0, :n_exp].astype(jnp.int32)
    padded = (counts + tr - 1) // tr * tr
    ends = jnp.cumsum(padded)
    offs = ends - padded
    tile_start = jnp.arange(nt, dtype=jnp.int32) * tr
    tile_expert = jnp.minimum(jnp.sum(tile_start[:, None] >= ends[None, :], axis=1), n_exp - 1).astype(jnp.int32)
    hot = tile_expert[:, None] == jnp.arange(n_exp, dtype=jnp.int32)[None, :]
    group_end = jnp.sum(jnp.where(hot, (offs + counts)[None, :], 0), axis=1)
    tile_valid = jnp.clip(group_end - tile_start, 0, tr).astype(jnp.int32)
    offs_pad = jnp.zeros((LANES,), jnp.int32).at[:n_exp].set(offs)

    xs, pos_t = _dispatch(h2p, idx_t, rank_t, offs_pad, n_slots=nt * tr)
    ys = _experts(tile_expert, tile_valid, xs, w_exp_gu[0].astype(BF16), w_exp_down[0].astype(BF16), tr=tr)
    r = _collect(ys, pos_t)
    out = _combine(base, mod4, gate, r, tm=min(256, length), tokens_per_batch=length)
    return out.reshape(b, length, d)
```

```python
import dataclasses
import functools

import jax
import jax.numpy as jnp
from jax import lax
from jax.experimental import pallas as pl
from jax.experimental.pallas import tpu as pltpu
from jax.experimental.pallas import tpu_sc as plsc

F32 = jnp.float32
BF16 = jnp.bfloat16

EPS = 1e-6
GRID_W = 64
ROPE_THETA = 10000.0
ATT_KV_HEADS = 2
TOP_K = 6
ROUTED_SCALE = 2.5
N_MOD = 6
CHUNK = 128
LOG2_E = 1.4426950408889634

LANES = 128
MXU_EDGE = 256
VMEM_LIMIT_BYTES = 56 * 1024 * 1024
SUBLANES = 8
CHOICE_ROWS = SUBLANES
SC_WINDOW = 64


def _cparams(*sem):
    return pltpu.CompilerParams(dimension_semantics=sem, vmem_limit_bytes=VMEM_LIMIT_BYTES)


def _dot(a, b):
    return jnp.dot(a, b, preferred_element_type=F32)


def _dot_nt(a, b):
    return lax.dot_general(a, b, (((1,), (1,)), ((), ())), preferred_element_type=F32)


def _rms(x, g):
    return x * lax.rsqrt(jnp.mean(x * x, axis=-1, keepdims=True) + EPS) * g


def _group_mean_sq(x, bd_ref):
    x2 = (x * x).astype(BF16)
    parts = [_dot(x2[:, i:i + MXU_EDGE], bd_ref[...]) for i in range(0, x.shape[1], MXU_EDGE)]
    return parts[0] if len(parts) == 1 else jnp.concatenate(parts, axis=-1)


def _dup_halves(x, lane):
    xr = pltpu.roll(x, LANES // 2, axis=1)
    lo = lane < LANES // 2
    return jnp.where(lo, x, xr), jnp.where(lo, xr, x)


def _with_ones(x, lane):
    xr = pltpu.roll(x, LANES // 2, axis=1)
    lo = lane < LANES // 2
    return jnp.concatenate([jnp.where(lo, x, 1.0), jnp.where(lo, xr, 1.0)], axis=-1)


_HI16 = 0xFFFF0000


def _pack_rows(x):
    half = x.shape[1] // 2
    rounded = x.astype(BF16).astype(F32)
    bits = lax.bitcast_convert_type(rounded, jnp.uint32)
    word = (bits[:, :half] >> 16) | (bits[:, half:] & jnp.uint32(_HI16))
    return lax.bitcast_convert_type(word, jnp.int32)


def _unpack_rows(w):
    bits = lax.bitcast_convert_type(w, jnp.uint32)
    lo = lax.bitcast_convert_type(bits << 16, F32)
    hi = lax.bitcast_convert_type(bits & jnp.uint32(_HI16), F32)
    return lo, hi


def _ada_kernel(c_ref, w_ref, b_ref, o_ref):
    s = jax.nn.silu(c_ref[...])
    o_ref[...] = jnp.dot(s, w_ref[...], precision=lax.Precision.HIGHEST,
                         preferred_element_type=F32) + b_ref[...]


def _ada(c_all, w_ada, b_ada):
    rows, d = c_all.shape
    n = w_ada.shape[1]
    bn = 512
    return pl.pallas_call(
        _ada_kernel,
        out_shape=jax.ShapeDtypeStruct((rows, n), F32),
        grid=(n // bn,),
        in_specs=[pl.BlockSpec((rows, d), lambda j: (0, 0)),
                  pl.BlockSpec((d, bn), lambda j: (0, j)),
                  pl.BlockSpec((1, bn), lambda j: (0, j))],
        out_specs=pl.BlockSpec((rows, bn), lambda j: (0, j)),
        compiler_params=_cparams("arbitrary"),
        name="ada",
    )(c_all, w_ada, b_ada)


def _ctx_kv_kernel(x_ref, sh_ref, sc_ref, nmix_ref, w_ref, kg_ref, bd_ref, k_ref, v_ref):
    h = _rms(x_ref[...], nmix_ref[...]) * (1.0 + sc_ref[...]) + sh_ref[...]
    p = _dot(h.astype(BF16), w_ref[...])
    ms = _group_mean_sq(p, bd_ref)
    k = p[:, :LANES] * lax.rsqrt(ms[:, :LANES] + EPS) * kg_ref[...]
    v = p[:, LANES:]
    lane = lax.broadcasted_iota(jnp.int32, k.shape, 1)
    ka, kb = _dup_halves(k, lane)
    k_ref[...] = jnp.concatenate([ka, kb], axis=-1).astype(BF16)
    v_ref[...] = _with_ones(v, lane).astype(BF16)


def _ctx_kv(ctx, mod4, ctx_row, nmix, w_kv, kg, bd):
    b, n_ctx, d = ctx.shape
    kv_w = 2 * LANES
    return pl.pallas_call(
        _ctx_kv_kernel,
        out_shape=(jax.ShapeDtypeStruct((b, n_ctx, kv_w), BF16),) * 2,
        grid=(b,),
        in_specs=[pl.BlockSpec((None, n_ctx, d), lambda i: (i, 0, 0)),
                  pl.BlockSpec((None, None, 1, d), lambda i: (ctx_row, 0, 0, 0)),
                  pl.BlockSpec((None, None, 1, d), lambda i: (ctx_row, 1, 0, 0)),
                  pl.BlockSpec((1, d), lambda i: (0, 0)),
                  pl.BlockSpec(w_kv.shape, lambda i: (0, 0)),
                  pl.BlockSpec((1, LANES), lambda i: (0, 0)),
                  pl.BlockSpec(bd.shape, lambda i: (0, 0))],
        out_specs=(pl.BlockSpec((None, n_ctx, kv_w), lambda i: (i, 0, 0)),) * 2,
        compiler_params=_cparams("arbitrary"),
        name="ctx_kv",
    )(ctx, mod4, mod4, nmix, w_kv, kg, bd)


def _inproj_kernel(x_ref, sh_ref, sc_ref, nmix_ref, w_ref, cos_ref, sin_ref, qg_ref, kg_ref, gmg_ref,
                   bd_ref, ws_ref, bs_ref, og_ref, q_ref, k_ref, v_ref, gm_ref, *, att_w, gm_w):
    tm = x_ref.shape[0]
    h = _rms(x_ref[...], nmix_ref[...]) * (1.0 + sc_ref[...]) + sh_ref[...]
    p = _dot(h.astype(BF16), w_ref[...])

    cos1, sin1 = cos_ref[...], sin_ref[...]
    lane1 = lax.broadcasted_iota(jnp.int32, (tm, LANES), 1)

    def rope(xn, cosw, sinw, lanew):
        w = xn.shape[1]
        fwd = pltpu.roll(xn, w - 32, axis=1)
        bwd = pltpu.roll(xn, 32, axis=1)
        swapped = jnp.where((lanew & 63) < 32, fwd, bwd)
        return xn * cosw + swapped * sinw

    q = p[:, :att_w]
    qn = q * lax.rsqrt(_group_mean_sq(q, bd_ref) + EPS) * qg_ref[...]
    reps = att_w // LANES
    cosq = jnp.concatenate([cos1] * reps, axis=-1)
    sinq = jnp.concatenate([sin1] * reps, axis=-1)
    laneq = lax.broadcasted_iota(jnp.int32, (tm, att_w), 1)
    q_ref[...] = rope(qn, cosq, sinq, laneq).astype(BF16)

    kv = p[:, att_w:att_w + 2 * LANES]
    ms = _group_mean_sq(kv, bd_ref)
    kn = kv[:, :LANES] * lax.rsqrt(ms[:, :LANES] + EPS) * kg_ref[...]
    k = rope(kn, cos1, sin1, lane1)
    v = kv[:, LANES:]
    ka, kb = _dup_halves(k, lane1)
    k_ref[...] = jnp.concatenate([ka, kb], axis=-1).astype(BF16)
    v_ref[...] = _with_ones(v, lane1).astype(BF16)

    u0 = att_w + 2 * LANES
    u = jax.nn.gelu(p[:, u0:u0 + gm_w])
    gt = jax.nn.gelu(p[:, u0 + gm_w:u0 + 2 * gm_w])
    gt = (gt * lax.rsqrt(_group_mean_sq(gt, bd_ref) + EPS) * gmg_ref[...]).astype(BF16)
    lo_half = lax.broadcasted_iota(jnp.int32, (CHUNK, LANES), 1) < LANES // 2
    rows = []
    for c in range(tm // CHUNK):
        blocks = []
        for j in range(gm_w // LANES):
            g = gt[c * CHUNK:(c + 1) * CHUNK, j * LANES:(j + 1) * LANES]
            r = _dot(ws_ref[j], g)
            blocks.append(jnp.where(lo_half, r[:CHUNK], r[CHUNK:]))
        rows.append(jnp.concatenate(blocks, axis=-1) + bs_ref[...])
    mixed = jnp.concatenate(rows, axis=0)
    gm_ref[...] = _rms(u * mixed, og_ref[...]).astype(BF16)


def _inproj(x, mod4, nmix, w_in, cos_t, sin_t, qg, kg, gmg, bd, ws, bs, og, *, att_w, gm_w, tm):
    b, length, d = x.shape
    kv_w = 2 * LANES
    n_in = w_in.shape[1]
    const = lambda shape: pl.BlockSpec(shape, lambda i, j: (0,) * len(shape))
    return pl.pallas_call(
        functools.partial(_inproj_kernel, att_w=att_w, gm_w=gm_w),
        out_shape=(jax.ShapeDtypeStruct((b, length, att_w), BF16),
                   jax.ShapeDtypeStruct((b, length, kv_w), BF16),
                   jax.ShapeDtypeStruct((b, length, kv_w), BF16),
                   jax.ShapeDtypeStruct((b, length, gm_w), BF16)),
        grid=(b, length // tm),
        in_specs=[pl.BlockSpec((None, tm, d), lambda i, j: (i, j, 0)),
                  pl.BlockSpec((None, None, 1, d), lambda i, j: (i, 0, 0, 0)),
                  pl.BlockSpec((None, None, 1, d), lambda i, j: (i, 1, 0, 0)),
                  const((1, d)),
                  const((d, n_in)),
                  pl.BlockSpec((tm, LANES), lambda i, j: (j, 0)),
                  pl.BlockSpec((tm, LANES), lambda i, j: (j, 0)),
                  const((1, att_w)), const((1, LANES)), const((1, gm_w)),
                  const(bd.shape), const(ws.shape), const(bs.shape), const((1, gm_w))],
        out_specs=(pl.BlockSpec((None, tm, att_w), lambda i, j: (i, j, 0)),
                   pl.BlockSpec((None, tm, kv_w), lambda i, j: (i, j, 0)),
                   pl.BlockSpec((None, tm, kv_w), lambda i, j: (i, j, 0)),
                   pl.BlockSpec((None, tm, gm_w), lambda i, j: (i, j, 0))),
        compiler_params=_cparams("arbitrary", "arbitrary"),
        name="inproj",
    )(x, mod4, mod4, nmix, w_in, cos_t, sin_t, qg, kg, gmg, bd, ws, bs, og)


def _attn_kernel(q_ref, kc_ref, kl_ref, vc_ref, vl_ref, o_ref, s_buf, p_buf, *, tq):
    n_ctx = kc_ref.shape[0]
    heads = q_ref.shape[1] // (LANES // 2)
    nj = q_ref.shape[0] // tq
    half = LANES // 2
    lo = lax.broadcasted_iota(jnp.int32, (tq, LANES), 1) < half

    s_buf[...] = jnp.zeros_like(s_buf)
    p_buf[...] = jnp.ones_like(p_buf)

    def scores(j, g):
        q = q_ref[pl.ds(j * tq, tq), (g // 2) * LANES:(g // 2 + 1) * LANES]
        zero = jnp.zeros_like(q)
        qh = jnp.where(lo, q, zero) if g % 2 == 0 else jnp.where(lo, zero, q)
        s_buf[g % 2, :, :n_ctx] = _dot_nt(qh, kc_ref[...])
        s_buf[g % 2, :, n_ctx:] = _dot_nt(qh, kl_ref[...])

    def probs(g):
        s = s_buf[g % 2]
        p_buf[g % 2] = jnp.exp2(s - jnp.max(s, axis=-1, keepdims=True)).astype(BF16)

    def output(j, g):
        p = p_buf[g % 2]
        t = _dot(p[:, :n_ctx], vc_ref[...]) + _dot(p[:, n_ctx:], vl_ref[...])
        r = pltpu.roll(t, half, axis=1)
        rows = pl.ds(j * tq, tq)
        if g % 2 == 0:
            o_ref[rows, g * half:(g + 1) * half] = (t / r)[:, :half].astype(o_ref.dtype)
        else:
            o_ref[rows, g * half:(g + 1) * half] = (r / t)[:, half:].astype(o_ref.dtype)

    def trip(j, carry):
        j_prev = jnp.maximum(j - 1, 0)
        for g in range(heads):
            scores(j, g)
            probs((g - 1) % heads)
            output(j if g >= 2 else j_prev, (g - 2) % heads)
        return carry

    lax.fori_loop(0, nj, trip, 0)
    probs(heads - 1)
    output(nj - 1, heads - 2)
    output(nj - 1, heads - 1)


def _attention(q, k_c, k_l, v_c, v_l, *, tq):
    b, length, att_w = q.shape
    n_ctx = k_c.shape[1]
    group_w = att_w // ATT_KV_HEADS
    assert (group_w // (LANES // 2)) % 2 == 0 and length % tq == 0
    kv_blk = lambda n: pl.BlockSpec((None, n, LANES), lambda i, h: (i, 0, h))
    return pl.pallas_call(
        functools.partial(_attn_kernel, tq=tq),
        out_shape=jax.ShapeDtypeStruct((b, length, att_w), BF16),
        grid=(b, ATT_KV_HEADS),
        in_specs=[pl.BlockSpec((None, length, group_w), lambda i, h: (i, 0, h)),
                  kv_blk(n_ctx), kv_blk(length), kv_blk(n_ctx), kv_blk(length)],
        out_specs=pl.BlockSpec((None, length, group_w), lambda i, h: (i, 0, h)),
        scratch_shapes=[pltpu.VMEM((2, tq, n_ctx + length), F32), pltpu.VMEM((2, tq, n_ctx + length), BF16)],
        compiler_params=_cparams("arbitrary", "arbitrary"),
        name="attn",
    )(q, k_c, k_l, v_c, v_l)


def _post_kernel(att_ref, gm_ref, x_ref, g1_ref, sh2_ref, sc2_ref, g2_ref, oa_ref, wo_ref, nffn_ref,
                 wrt_ref, rb_ref, wsg_ref, wsd_ref,
                 base_ref, h2p_ref, idx_ref, gate_ref, rank_ref, cnt_ref, carry_ref):
    i = pl.program_id(0)
    tm = x_ref.shape[0]
    n_exp = wrt_ref.shape[0]

    @pl.when(i == 0)
    def _():
        carry_ref[...] = jnp.zeros_like(carry_ref)

    att_n = _rms(att_ref[...].astype(F32), oa_ref[...]).astype(BF16)
    y = _dot(jnp.concatenate([att_n, gm_ref[...]], axis=-1), wo_ref[...])
    x_new = x_ref[...] + g1_ref[...] * y
    h2 = _rms(x_new, nffn_ref[...]) * (1.0 + sc2_ref[...]) + sh2_ref[...]
    h2p_ref[...] = _pack_rows(h2)

    h_hi = h2.astype(BF16)
    scores_t = jax.nn.sigmoid(_dot_nt(wrt_ref[...], h_hi))
    reps = tm // LANES
    sel = scores_t + jnp.concatenate([rb_ref[...]] * reps, axis=-1)
    expert = lax.broadcasted_iota(jnp.int32, scores_t.shape, 0).astype(F32)
    neg = jnp.float32(-jnp.inf)

    onehot = jnp.zeros(scores_t.shape, F32)
    hits, idxs, svals = [], [], []
    for _k in range(TOP_K):
        m = jnp.max(sel, axis=0, keepdims=True)
        idx = jnp.min(jnp.where(sel == m, expert, float(n_exp)), axis=0, keepdims=True)
        hit = expert == idx
        hits.append(hit)
        idxs.append(idx)
        svals.append(jnp.sum(jnp.where(hit, scores_t, 0.0), axis=0, keepdims=True))
        onehot = onehot + jnp.where(hit, 1.0, 0.0)
        sel = jnp.where(hit, neg, sel)
    ssum = functools.reduce(lambda a, b: a + b, svals)

    r_io = lax.broadcasted_iota(jnp.int32, (tm, tm), 0)
    c_io = lax.broadcasted_iota(jnp.int32, (tm, tm), 1)
    tri = jnp.where(r_io < c_io, 1.0, 0.0).astype(BF16)
    carry = carry_ref[...]
    before = _dot(onehot.astype(BF16), tri) + jnp.concatenate([carry] * reps, axis=-1)
    carry = carry + jnp.sum(onehot, axis=1, keepdims=True)
    carry_ref[...] = carry
    cnt_ref[...] = carry

    row = lax.broadcasted_iota(jnp.int32, (idx_ref.shape[0], tm), 0)
    idx_o = jnp.zeros(row.shape, F32)
    rank_o = jnp.zeros(row.shape, F32)
    gate_o = jnp.zeros(row.shape, F32)
    for k in range(TOP_K):
        rank = jnp.sum(jnp.where(hits[k], before, 0.0), axis=0, keepdims=True)
        idx_o = jnp.where(row == k, idxs[k], idx_o)
        rank_o = jnp.where(row == k, rank, rank_o)
        gate_o = jnp.where(row == k, svals[k] / ssum * ROUTED_SCALE, gate_o)
    idx_ref[...] = idx_o.astype(jnp.int32)
    rank_ref[...] = rank_o.astype(jnp.int32)
    gate_pad = jnp.concatenate([gate_o, jnp.zeros((LANES - gate_o.shape[0], tm), F32)], axis=0)
    gate_ref[...] = gate_pad.T

    gu = _dot(h_hi, wsg_ref[...])
    hs = gu.shape[1] // 2
    act = (jax.nn.silu(gu[:, :hs]) * gu[:, hs:]).astype(BF16)
    base_ref[...] = x_new + g2_ref[...] * _dot(act, wsd_ref[...])


def _post(att, gmn, x2, mod4, oa, w_out, nffn, wr_t, rb, w_sg, w_sd, *, tm, tokens_per_batch):
    t, d = x2.shape
    n_exp = wr_t.shape[0]
    att_w, gm_w = att.shape[1], gmn.shape[1]
    tpb = tokens_per_batch // tm
    const = lambda shape: pl.BlockSpec(shape, lambda i: (0,) * len(shape))
    mrow = lambda j: pl.BlockSpec((None, None, 1, d), lambda i: (i // tpb, j, 0, 0))
    by_choice = jax.ShapeDtypeStruct((CHOICE_ROWS, t), jnp.int32)
    return pl.pallas_call(
        _post_kernel,
        out_shape=(jax.ShapeDtypeStruct((t, d), F32), jax.ShapeDtypeStruct((t, d // 2), jnp.int32),
                   by_choice, jax.ShapeDtypeStruct((t, LANES), F32), by_choice,
                   jax.ShapeDtypeStruct((n_exp, LANES), F32)),
        grid=(t // tm,),
        in_specs=[pl.BlockSpec((tm, att_w), lambda i: (i, 0)),
                  pl.BlockSpec((tm, gm_w), lambda i: (i, 0)),
                  pl.BlockSpec((tm, d), lambda i: (i, 0)),
                  mrow(2), mrow(3), mrow(4), mrow(5),
                  const((1, att_w)), const(w_out.shape), const((1, d)),
                  const(wr_t.shape), const(rb.shape),
                  const(w_sg.shape), const(w_sd.shape)],
        out_specs=(pl.BlockSpec((tm, d), lambda i: (i, 0)), pl.BlockSpec((tm, d // 2), lambda i: (i, 0)),
                   pl.BlockSpec((CHOICE_ROWS, tm), lambda i: (0, i)), pl.BlockSpec((tm, LANES), lambda i: (i, 0)),
                   pl.BlockSpec((CHOICE_ROWS, tm), lambda i: (0, i)), pl.BlockSpec((n_exp, LANES), lambda i: (0, 0))),
        scratch_shapes=[pltpu.VMEM((n_exp, LANES), F32)],
        compiler_params=_cparams("arbitrary"),
        name="post",
    )(att, gmn, x2, mod4, mod4, mod4, mod4, oa, w_out, nffn, wr_t, rb, w_sg, w_sd)


def _sc_mesh_info():
    info = plsc.get_sparse_core_info()
    mesh = plsc.VectorSubcoreMesh(core_axis_name="c", subcore_axis_name="s")
    return mesh, info.num_cores, info.num_subcores, info.num_lanes


def _sc_params():
    cp = pltpu.CompilerParams()
    if "needs_layout_passes" in pltpu.CompilerParams.__dataclass_fields__:
        cp = dataclasses.replace(cp, needs_layout_passes=False)
    return cp


def _dispatch(h2p, idx_t, rank_t, offs, *, n_slots):
    t, dw = h2p.shape
    mesh, n_cores, n_sub, n_lanes = _sc_mesh_info()
    per_worker = t // (n_cores * n_sub)
    w = SC_WINDOW
    assert per_worker % w == 0 and w % n_lanes == 0

    @functools.partial(
        pl.kernel, mesh=mesh,
        out_type=(jax.ShapeDtypeStruct((n_slots, dw), jnp.int32), jax.ShapeDtypeStruct((CHOICE_ROWS, t), jnp.int32)),
        scratch_types=[pltpu.VMEM(offs.shape, jnp.int32)] + [pltpu.VMEM((TOP_K, w), jnp.int32)] * 3
        + [pltpu.VMEM((w, dw), jnp.int32), pltpu.SemaphoreType.DMA],
        compiler_params=_sc_params(),
    )
    def run(h_hbm, idx_hbm, rank_hbm, offs_hbm, xs_hbm, pos_hbm, offs_v, idx_v, rank_v, pos_v, rows_v, sem):
        base = (lax.axis_index("s") * n_cores + lax.axis_index("c")) * per_worker
        pltpu.sync_copy(offs_hbm, offs_v)

        @pl.loop(0, per_worker // w)
        def _(i):
            t0 = base + i * w
            pltpu.sync_copy(h_hbm.at[pl.ds(t0, w)], rows_v)
            for k in range(TOP_K):
                pltpu.sync_copy(idx_hbm.at[k, pl.ds(t0, w)], idx_v.at[k])
                pltpu.sync_copy(rank_hbm.at[k, pl.ds(t0, w)], rank_v.at[k])
            for k in range(TOP_K):
                for j in range(0, w, n_lanes):
                    group_start = plsc.load_gather(offs_v, [idx_v[k, pl.ds(j, n_lanes)]])
                    pos_v[k, pl.ds(j, n_lanes)] = group_start + rank_v[k, pl.ds(j, n_lanes)]
            copies = [pltpu.async_copy(rows_v, xs_hbm.at[pos_v.at[k]], sem) for k in range(TOP_K)]
            for k in range(TOP_K):
                pltpu.sync_copy(pos_v.at[k], pos_hbm.at[k, pl.ds(t0, w)])
            for cp in copies:
                cp.wait()

    return run(h2p, idx_t, rank_t, offs)


def _collect(ys, pos_t):
    t = pos_t.shape[1]
    dw = ys.shape[1]
    mesh, n_cores, n_sub, _ = _sc_mesh_info()
    per_worker = t // (n_cores * n_sub)
    w = SC_WINDOW
    assert per_worker % w == 0

    @functools.partial(
        pl.kernel, mesh=mesh,
        out_type=jax.ShapeDtypeStruct((TOP_K * t, dw), jnp.int32),
        scratch_types=[pltpu.VMEM((TOP_K, w), jnp.int32), pltpu.VMEM((w, dw), jnp.int32), pltpu.SemaphoreType.DMA],
        compiler_params=_sc_params(),
    )
    def run(ys_hbm, pos_hbm, r_hbm, pos_v, rows_v, sem):
        base = (lax.axis_index("s") * n_cores + lax.axis_index("c")) * per_worker

        @pl.loop(0, per_worker // w)
        def _(i):
            t0 = base + i * w
            for k in range(TOP_K):
                pltpu.sync_copy(pos_hbm.at[k, pl.ds(t0, w)], pos_v.at[k])
            for k in range(TOP_K):
                pltpu.async_copy(ys_hbm.at[pos_v.at[k]], rows_v, sem).wait()
                pltpu.sync_copy(rows_v, r_hbm.at[pl.ds(k * t + t0, w)])

    return run(ys, pos_t)


def _experts_kernel(te_ref, nu_ref, xs_ref, wgu_ref, wdn_ref, ys_ref, wgu_b, wdn_b):
    i = pl.program_id(0)

    @pl.when(i < nu_ref[0])
    def _():
        @pl.when(jnp.logical_or(i == 0, te_ref[i] != te_ref[jnp.maximum(i - 1, 0)]))
        def _():
            wgu_b[...] = wgu_ref[...].astype(BF16)
            wdn_b[...] = wdn_ref[...].astype(BF16)

        half = xs_ref.shape[1]
        lo, hi = _unpack_rows(xs_ref[...])
        gu = _dot(lo.astype(BF16), wgu_b[:half, :]) + _dot(hi.astype(BF16), wgu_b[half:, :])
        hh = gu.shape[1] // 2
        act = (jax.nn.silu(gu[:, :hh]) * gu[:, hh:]).astype(BF16)
        ys_ref[...] = _pack_rows(_dot(act, wdn_b[...]))


def _experts(tile_expert, n_used, xs, w_gu, w_dn, *, tr):
    nt = tile_expert.shape[0]
    _, d, h2w = w_gu.shape
    hh = w_dn.shape[1]
    wmap = lambda i, te, nu: (te[jnp.minimum(i, nu[0] - 1)], 0, 0)
    return pl.pallas_call(
        _experts_kernel,
        out_shape=jax.ShapeDtypeStruct(xs.shape, jnp.int32),
        grid_spec=pltpu.PrefetchScalarGridSpec(
            num_scalar_prefetch=2,
            grid=(nt,),
            in_specs=[pl.BlockSpec((tr, d // 2), lambda i, te, nu: (jnp.minimum(i, nu[0] - 1), 0)),
                      pl.BlockSpec((None, d, h2w), wmap),
                      pl.BlockSpec((None, hh, d), wmap)],
            out_specs=pl.BlockSpec((tr, d // 2), lambda i, te, nu: (jnp.where(i < nu[0], i, nt - 1), 0)),
            scratch_shapes=[pltpu.VMEM((d, h2w), BF16), pltpu.VMEM((hh, d), BF16)]),
        compiler_params=_cparams("arbitrary"),
        name="experts",
    )(tile_expert, n_used, xs, w_gu, w_dn)


def _combine_kernel(base_ref, g2_ref, gate_ref, *refs):
    r_refs, o_ref = refs[:TOP_K], refs[TOP_K]
    gate = gate_ref[...]
    half = base_ref.shape[1] // 2
    acc_lo = jnp.zeros((base_ref.shape[0], half), F32)
    acc_hi = jnp.zeros((base_ref.shape[0], half), F32)
    for k in range(TOP_K):
        lo, hi = _unpack_rows(r_refs[k][...])
        g = gate[:, k:k + 1]
        acc_lo = acc_lo + g * lo
        acc_hi = acc_hi + g * hi
    g2 = g2_ref[...]
    o_ref[:, :half] = base_ref[:, :half] + g2[:, :half] * acc_lo
    o_ref[:, half:] = base_ref[:, half:] + g2[:, half:] * acc_hi


def _combine(base, mod4, gate, r, *, tm, tokens_per_batch):
    t, d = base.shape
    tpb = tokens_per_batch // tm
    nblk = t // tm
    r_spec = lambda k: pl.BlockSpec((tm, d // 2), lambda i: (k * nblk + i, 0))
    return pl.pallas_call(
        _combine_kernel,
        out_shape=jax.ShapeDtypeStruct((t, d), F32),
        grid=(nblk,),
        in_specs=[pl.BlockSpec((tm, d), lambda i: (i, 0)),
                  pl.BlockSpec((None, None, 1, d), lambda i: (i // tpb, 5, 0, 0)),
                  pl.BlockSpec((tm, LANES), lambda i: (i, 0))] + [r_spec(k) for k in range(TOP_K)],
        out_specs=pl.BlockSpec((tm, d), lambda i: (i, 0)),
        compiler_params=_cparams("arbitrary"),
        name="combine",
    )(base, mod4, gate, *([r] * TOP_K))


def _rope_tables(length, head_dim):
    rows = length // GRID_W
    pairs_axis = head_dim // 4
    r = jnp.repeat(jnp.arange(rows, dtype=F32), GRID_W)
    col = jnp.tile(jnp.arange(GRID_W, dtype=F32), rows)
    inv = ROPE_THETA ** (-jnp.arange(pairs_axis, dtype=F32) / pairs_axis)
    ang = jnp.concatenate([r[:, None] * inv, col[:, None] * inv], axis=-1)
    cos, sin = jnp.cos(ang), jnp.sin(ang)
    cos_h = jnp.concatenate([cos, cos], axis=-1)
    sin_h = jnp.concatenate([-sin, sin], axis=-1)
    reps = LANES // head_dim
    return jnp.tile(cos_h, (1, reps)), jnp.tile(sin_h, (1, reps))


def kernel(x, c, ctx, c_ctx, w_ada, b_ada, norm_mix, w_in, q_norm, k_norm, gm_norm, w_spatial, b_spatial,
           out_norm_attn, out_norm_gm, w_out, norm_ffn, w_router, router_bias, w_exp_gu, w_exp_down,
           w_sh_gu, w_sh_down):
    assert w_ada.shape[0] == 1, "single-layer kernel"
    b, length, d = x.shape
    head_dim = q_norm.shape[-1]
    att_w = out_norm_attn.shape[-1]
    gm_w = out_norm_gm.shape[-1]
    gm_groups = w_spatial.shape[1]
    gm_hd = gm_w // gm_groups
    n_exp = w_router.shape[-1]
    t = b * length
    assert head_dim == LANES // 2 and gm_hd == LANES // 2 and w_spatial.shape[-1] == CHUNK
    assert n_exp <= LANES and att_w % MXU_EDGE == 0 and gm_w % MXU_EDGE == 0

    mod_rows = 16
    c_all = jnp.zeros((mod_rows, d), F32).at[:b].set(c).at[b].set(c_ctx)
    mod = _ada(c_all, w_ada[0], b_ada)
    mod4 = mod.reshape(mod_rows, N_MOD, 1, d)

    w_in_b = w_in[0].astype(BF16)
    bd = (jnp.arange(MXU_EDGE)[:, None] // head_dim == jnp.arange(MXU_EDGE)[None, :] // head_dim)
    bd = (bd.astype(F32) / head_dim).astype(BF16)
    qg = jnp.tile(q_norm[0] * (head_dim ** -0.5 * LOG2_E), att_w // head_dim)[None, :]
    kg = jnp.tile(k_norm[0], LANES // head_dim)[None, :]
    gmg = jnp.tile(gm_norm[0], gm_groups)[None, :]
    ws = w_spatial[0].astype(BF16).reshape(gm_groups // 2, 2 * CHUNK, CHUNK)
    bs = jnp.repeat(b_spatial[0].T, gm_hd, axis=1)
    cos_t, sin_t = _rope_tables(length, head_dim)

    k_c, v_c = _ctx_kv(ctx, mod4, b, norm_mix, w_in_b[:, att_w:att_w + 2 * LANES], kg, bd)
    q, k_l, v_l, gmn = _inproj(x, mod4, norm_mix, w_in_b, cos_t, sin_t, qg, kg, gmg, bd, ws, bs,
                               out_norm_gm, att_w=att_w, gm_w=gm_w, tm=min(512, length))
    att = _attention(q, k_c, k_l, v_c, v_l, tq=min(256, length))

    wr_t = w_router[0].T.astype(BF16)
    rb = jnp.tile(router_bias[0][:, None], (1, LANES))
    tm_post = min(512, length)
    base, h2p, idx_t, gate, rank_t, cnt = _post(
        att.reshape(t, att_w), gmn.reshape(t, gm_w), x.reshape(t, d), mod4, out_norm_attn, w_out[0].astype(BF16),
        norm_ffn, wr_t, rb, w_sh_gu[0].astype(BF16), w_sh_down[0].astype(BF16),
        tm=tm_post, tokens_per_batch=length)

    tr = 512
    nt = t * TOP_K // tr + n_exp
    counts = cnt[:, 0].astype(jnp.int32)
    padded = (counts + tr - 1) // tr * tr
    ends = jnp.cumsum(padded)
    offs = ends - padded
    tile_start = jnp.arange(nt, dtype=jnp.int32) * tr
    tile_expert = jnp.minimum(jnp.sum(tile_start[:, None] >= ends[None, :], axis=1), n_exp - 1).astype(jnp.int32)
    n_used = (ends[-1:] // tr).astype(jnp.int32)
    offs_pad = jnp.zeros((LANES,), jnp.int32).at[:n_exp].set(offs)

    xs, pos_t = _dispatch(h2p, idx_t, rank_t, offs_pad, n_slots=nt * tr)
    ys = _experts(tile_expert, n_used, xs, w_exp_gu[0], w_exp_down[0], tr=tr)
    r = _collect(ys, pos_t)
    out = _combine(base, mod4, gate, r, tm=min(256, length), tokens_per_batch=length)
    return out.reshape(b, length, d)
```

```python
import dataclasses
import functools

import jax
import jax.numpy as jnp
from jax import lax
from jax.experimental import pallas as pl
from jax.experimental.pallas import tpu as pltpu
from jax.experimental.pallas import tpu_sc as plsc

F32 = jnp.float32
BF16 = jnp.bfloat16

EPS = 1e-6
GRID_W = 64
ROPE_THETA = 10000.0
ATT_KV_HEADS = 2
TOP_K = 6
ROUTED_SCALE = 2.5
N_MOD = 6
CHUNK = 128
LOG2_E = 1.4426950408889634

LANES = 128
MXU_EDGE = 256
VMEM_LIMIT_BYTES = 56 * 1024 * 1024
SUBLANES = 8
CHOICE_ROWS = SUBLANES
SC_WINDOW = 64
MOE_PARTS = 2


def _cparams(*sem):
    return pltpu.CompilerParams(dimension_semantics=sem, vmem_limit_bytes=VMEM_LIMIT_BYTES)


def _dot(a, b):
    return jnp.dot(a, b, preferred_element_type=F32)


def _dot_nt(a, b):
    return lax.dot_general(a, b, (((1,), (1,)), ((), ())), preferred_element_type=F32)


def _rms(x, g):
    return x * lax.rsqrt(jnp.mean(x * x, axis=-1, keepdims=True) + EPS) * g


def _group_mean_sq(x, bd_ref):
    x2 = (x * x).astype(BF16)
    parts = [_dot(x2[:, i:i + MXU_EDGE], bd_ref[...]) for i in range(0, x.shape[1], MXU_EDGE)]
    return parts[0] if len(parts) == 1 else jnp.concatenate(parts, axis=-1)


def _dup_halves(x, lane):
    xr = pltpu.roll(x, LANES // 2, axis=1)
    lo = lane < LANES // 2
    return jnp.where(lo, x, xr), jnp.where(lo, xr, x)


def _with_ones(x, lane):
    xr = pltpu.roll(x, LANES // 2, axis=1)
    lo = lane < LANES // 2
    return jnp.concatenate([jnp.where(lo, x, 1.0), jnp.where(lo, xr, 1.0)], axis=-1)


_HI16 = 0xFFFF0000


def _pack_rows(x):
    half = x.shape[1] // 2
    rounded = x.astype(BF16).astype(F32)
    bits = lax.bitcast_convert_type(rounded, jnp.uint32)
    word = (bits[:, :half] >> 16) | (bits[:, half:] & jnp.uint32(_HI16))
    return lax.bitcast_convert_type(word, jnp.int32)


def _unpack_rows(w):
    bits = lax.bitcast_convert_type(w, jnp.uint32)
    lo = lax.bitcast_convert_type(bits << 16, F32)
    hi = lax.bitcast_convert_type(bits & jnp.uint32(_HI16), F32)
    return lo, hi


def _ada_kernel(c_ref, w_ref, b_ref, o_ref):
    s = jax.nn.silu(c_ref[...])
    o_ref[...] = jnp.dot(s, w_ref[...], precision=lax.Precision.HIGHEST,
                         preferred_element_type=F32) + b_ref[...]


def _ada(c_all, w_ada, b_ada):
    rows, d = c_all.shape
    n = w_ada.shape[1]
    bn = 512
    return pl.pallas_call(
        _ada_kernel,
        out_shape=jax.ShapeDtypeStruct((rows, n), F32),
        grid=(n // bn,),
        in_specs=[pl.BlockSpec((rows, d), lambda j: (0, 0)),
                  pl.BlockSpec((d, bn), lambda j: (0, j)),
                  pl.BlockSpec((1, bn), lambda j: (0, j))],
        out_specs=pl.BlockSpec((rows, bn), lambda j: (0, j)),
        compiler_params=_cparams("arbitrary"),
        name="ada",
    )(c_all, w_ada, b_ada)


def _ctx_kv_kernel(x_ref, sh_ref, sc_ref, nmix_ref, w_ref, kg_ref, bd_ref, k_ref, v_ref):
    h = _rms(x_ref[...], nmix_ref[...]) * (1.0 + sc_ref[...]) + sh_ref[...]
    p = _dot(h.astype(BF16), w_ref[...])
    ms = _group_mean_sq(p, bd_ref)
    k = p[:, :LANES] * lax.rsqrt(ms[:, :LANES] + EPS) * kg_ref[...]
    v = p[:, LANES:]
    lane = lax.broadcasted_iota(jnp.int32, k.shape, 1)
    ka, kb = _dup_halves(k, lane)
    k_ref[...] = jnp.concatenate([ka, kb], axis=-1).astype(BF16)
    v_ref[...] = _with_ones(v, lane).astype(BF16)


def _ctx_kv(ctx, mod4, ctx_row, nmix, w_kv, kg, bd):
    b, n_ctx, d = ctx.shape
    kv_w = 2 * LANES
    return pl.pallas_call(
        _ctx_kv_kernel,
        out_shape=(jax.ShapeDtypeStruct((b, n_ctx, kv_w), BF16),) * 2,
        grid=(b,),
        in_specs=[pl.BlockSpec((None, n_ctx, d), lambda i: (i, 0, 0)),
                  pl.BlockSpec((None, None, 1, d), lambda i: (ctx_row, 0, 0, 0)),
                  pl.BlockSpec((None, None, 1, d), lambda i: (ctx_row, 1, 0, 0)),
                  pl.BlockSpec((1, d), lambda i: (0, 0)),
                  pl.BlockSpec(w_kv.shape, lambda i: (0, 0)),
                  pl.BlockSpec((1, LANES), lambda i: (0, 0)),
                  pl.BlockSpec(bd.shape, lambda i: (0, 0))],
        out_specs=(pl.BlockSpec((None, n_ctx, kv_w), lambda i: (i, 0, 0)),) * 2,
        compiler_params=_cparams("arbitrary"),
        name="ctx_kv",
    )(ctx, mod4, mod4, nmix, w_kv, kg, bd)


def _inproj_kernel(x_ref, sh_ref, sc_ref, nmix_ref, w_ref, cos_ref, sin_ref, qg_ref, kg_ref, gmg_ref,
                   bd_ref, ws_ref, bs_ref, og_ref, q_ref, k_ref, v_ref, gm_ref, *, att_w, gm_w):
    tm = x_ref.shape[0]
    h = _rms(x_ref[...], nmix_ref[...]) * (1.0 + sc_ref[...]) + sh_ref[...]
    p = _dot(h.astype(BF16), w_ref[...])

    cos1, sin1 = cos_ref[...], sin_ref[...]
    lane1 = lax.broadcasted_iota(jnp.int32, (tm, LANES), 1)

    def rope(xn, cosw, sinw, lanew):
        w = xn.shape[1]
        fwd = pltpu.roll(xn, w - 32, axis=1)
        bwd = pltpu.roll(xn, 32, axis=1)
        swapped = jnp.where((lanew & 63) < 32, fwd, bwd)
        return xn * cosw + swapped * sinw

    q = p[:, :att_w]
    qn = q * lax.rsqrt(_group_mean_sq(q, bd_ref) + EPS) * qg_ref[...]
    reps = att_w // LANES
    cosq = jnp.concatenate([cos1] * reps, axis=-1)
    sinq = jnp.concatenate([sin1] * reps, axis=-1)
    laneq = lax.broadcasted_iota(jnp.int32, (tm, att_w), 1)
    q_ref[...] = rope(qn, cosq, sinq, laneq).astype(BF16)

    kv = p[:, att_w:att_w + 2 * LANES]
    ms = _group_mean_sq(kv, bd_ref)
    kn = kv[:, :LANES] * lax.rsqrt(ms[:, :LANES] + EPS) * kg_ref[...]
    k = rope(kn, cos1, sin1, lane1)
    v = kv[:, LANES:]
    ka, kb = _dup_halves(k, lane1)
    k_ref[...] = jnp.concatenate([ka, kb], axis=-1).astype(BF16)
    v_ref[...] = _with_ones(v, lane1).astype(BF16)

    u0 = att_w + 2 * LANES
    u = jax.nn.gelu(p[:, u0:u0 + gm_w])
    gt = jax.nn.gelu(p[:, u0 + gm_w:u0 + 2 * gm_w])
    gt = (gt * lax.rsqrt(_group_mean_sq(gt, bd_ref) + EPS) * gmg_ref[...]).astype(BF16)
    lo_half = lax.broadcasted_iota(jnp.int32, (CHUNK, LANES), 1) < LANES // 2
    rows = []
    for c in range(tm // CHUNK):
        blocks = []
        for j in range(gm_w // LANES):
            g = gt[c * CHUNK:(c + 1) * CHUNK, j * LANES:(j + 1) * LANES]
            r = _dot(ws_ref[j], g)
            blocks.append(jnp.where(lo_half, r[:CHUNK], r[CHUNK:]))
        rows.append(jnp.concatenate(blocks, axis=-1) + bs_ref[...])
    mixed = jnp.concatenate(rows, axis=0)
    gm_ref[...] = _rms(u * mixed, og_ref[...]).astype(BF16)


def _inproj(x, mod4, nmix, w_in, cos_t, sin_t, qg, kg, gmg, bd, ws, bs, og, *, att_w, gm_w, tm):
    b, length, d = x.shape
    kv_w = 2 * LANES
    n_in = w_in.shape[1]
    const = lambda shape: pl.BlockSpec(shape, lambda i, j: (0,) * len(shape))
    return pl.pallas_call(
        functools.partial(_inproj_kernel, att_w=att_w, gm_w=gm_w),
        out_shape=(jax.ShapeDtypeStruct((b, length, att_w), BF16),
                   jax.ShapeDtypeStruct((b, length, kv_w), BF16),
                   jax.ShapeDtypeStruct((b, length, kv_w), BF16),
                   jax.ShapeDtypeStruct((b, length, gm_w), BF16)),
        grid=(b, length // tm),
        in_specs=[pl.BlockSpec((None, tm, d), lambda i, j: (i, j, 0)),
                  pl.BlockSpec((None, None, 1, d), lambda i, j: (i, 0, 0, 0)),
                  pl.BlockSpec((None, None, 1, d), lambda i, j: (i, 1, 0, 0)),
                  const((1, d)),
                  const((d, n_in)),
                  pl.BlockSpec((tm, LANES), lambda i, j: (j, 0)),
                  pl.BlockSpec((tm, LANES), lambda i, j: (j, 0)),
                  const((1, att_w)), const((1, LANES)), const((1, gm_w)),
                  const(bd.shape), const(ws.shape), const(bs.shape), const((1, gm_w))],
        out_specs=(pl.BlockSpec((None, tm, att_w), lambda i, j: (i, j, 0)),
                   pl.BlockSpec((None, tm, kv_w), lambda i, j: (i, j, 0)),
                   pl.BlockSpec((None, tm, kv_w), lambda i, j: (i, j, 0)),
                   pl.BlockSpec((None, tm, gm_w), lambda i, j: (i, j, 0))),
        compiler_params=_cparams("arbitrary", "arbitrary"),
        name="inproj",
    )(x, mod4, mod4, nmix, w_in, cos_t, sin_t, qg, kg, gmg, bd, ws, bs, og)


def _attn_kernel(q_ref, kc_ref, kl_ref, vc_ref, vl_ref, o_ref, s_buf, p_buf, *, tq):
    n_ctx = kc_ref.shape[0]
    heads = q_ref.shape[1] // (LANES // 2)
    nj = q_ref.shape[0] // tq
    half = LANES // 2
    lo = lax.broadcasted_iota(jnp.int32, (tq, LANES), 1) < half

    s_buf[...] = jnp.zeros_like(s_buf)
    p_buf[...] = jnp.ones_like(p_buf)

    def scores(j, g):
        q = q_ref[pl.ds(j * tq, tq), (g // 2) * LANES:(g // 2 + 1) * LANES]
        zero = jnp.zeros_like(q)
        qh = jnp.where(lo, q, zero) if g % 2 == 0 else jnp.where(lo, zero, q)
        s_buf[g % 2, :, :n_ctx] = _dot_nt(qh, kc_ref[...])
        s_buf[g % 2, :, n_ctx:] = _dot_nt(qh, kl_ref[...])

    def probs(g):
        s = s_buf[g % 2]
        p_buf[g % 2] = jnp.exp2(s - jnp.max(s, axis=-1, keepdims=True)).astype(BF16)

    def output(j, g):
        p = p_buf[g % 2]
        t = _dot(p[:, :n_ctx], vc_ref[...]) + _dot(p[:, n_ctx:], vl_ref[...])
        r = pltpu.roll(t, half, axis=1)
        rows = pl.ds(j * tq, tq)
        if g % 2 == 0:
            o_ref[rows, g * half:(g + 1) * half] = (t / r)[:, :half].astype(o_ref.dtype)
        else:
            o_ref[rows, g * half:(g + 1) * half] = (r / t)[:, half:].astype(o_ref.dtype)

    def trip(j, carry):
        j_prev = jnp.maximum(j - 1, 0)
        for g in range(heads):
            scores(j, g)
            probs((g - 1) % heads)
            output(j if g >= 2 else j_prev, (g - 2) % heads)
        return carry

    lax.fori_loop(0, nj, trip, 0)
    probs(heads - 1)
    output(nj - 1, heads - 2)
    output(nj - 1, heads - 1)


def _attention(q, k_c, k_l, v_c, v_l, *, tq, b0, nb):
    _, length, att_w = q.shape
    n_ctx = k_c.shape[1]
    group_w = att_w // ATT_KV_HEADS
    assert (group_w // (LANES // 2)) % 2 == 0 and length % tq == 0
    kv_blk = lambda n: pl.BlockSpec((None, n, LANES), lambda i, h: (i + b0, 0, h))
    return pl.pallas_call(
        functools.partial(_attn_kernel, tq=tq),
        out_shape=jax.ShapeDtypeStruct((nb, length, att_w), BF16),
        grid=(nb, ATT_KV_HEADS),
        in_specs=[pl.BlockSpec((None, length, group_w), lambda i, h: (i + b0, 0, h)),
                  kv_blk(n_ctx), kv_blk(length), kv_blk(n_ctx), kv_blk(length)],
        out_specs=pl.BlockSpec((None, length, group_w), lambda i, h: (i, 0, h)),
        scratch_shapes=[pltpu.VMEM((2, tq, n_ctx + length), F32), pltpu.VMEM((2, tq, n_ctx + length), BF16)],
        compiler_params=_cparams("arbitrary", "arbitrary"),
        name="attn",
    )(q, k_c, k_l, v_c, v_l)


def _post_kernel(att_ref, gm_ref, x_ref, g1_ref, sh2_ref, sc2_ref, g2_ref, oa_ref, wo_ref, nffn_ref,
                 wrt_ref, rb_ref, wsg_ref, wsd_ref,
                 base_ref, h2p_ref, idx_ref, gate_ref, rank_ref, cnt_ref, carry_ref):
    i = pl.program_id(0)
    tm = x_ref.shape[0]
    n_exp = wrt_ref.shape[0]

    @pl.when(i == 0)
    def _():
        carry_ref[...] = jnp.zeros_like(carry_ref)

    att_n = _rms(att_ref[...].astype(F32), oa_ref[...]).astype(BF16)
    y = _dot(jnp.concatenate([att_n, gm_ref[...]], axis=-1), wo_ref[...])
    x_new = x_ref[...] + g1_ref[...] * y
    h2 = _rms(x_new, nffn_ref[...]) * (1.0 + sc2_ref[...]) + sh2_ref[...]
    h2p_ref[...] = _pack_rows(h2)

    h_hi = h2.astype(BF16)
    scores_t = jax.nn.sigmoid(_dot_nt(wrt_ref[...], h_hi))
    reps = tm // LANES
    sel = scores_t + jnp.concatenate([rb_ref[...]] * reps, axis=-1)
    expert = lax.broadcasted_iota(jnp.int32, scores_t.shape, 0).astype(F32)
    neg = jnp.float32(-jnp.inf)

    onehot = jnp.zeros(scores_t.shape, F32)
    hits, idxs, svals = [], [], []
    for _k in range(TOP_K):
        m = jnp.max(sel, axis=0, keepdims=True)
        idx = jnp.min(jnp.where(sel == m, expert, float(n_exp)), axis=0, keepdims=True)
        hit = expert == idx
        hits.append(hit)
        idxs.append(idx)
        svals.append(jnp.sum(jnp.where(hit, scores_t, 0.0), axis=0, keepdims=True))
        onehot = onehot + jnp.where(hit, 1.0, 0.0)
        sel = jnp.where(hit, neg, sel)
    ssum = functools.reduce(lambda a, b: a + b, svals)

    r_io = lax.broadcasted_iota(jnp.int32, (tm, tm), 0)
    c_io = lax.broadcasted_iota(jnp.int32, (tm, tm), 1)
    tri = jnp.where(r_io < c_io, 1.0, 0.0).astype(BF16)
    carry = carry_ref[...]
    before = _dot(onehot.astype(BF16), tri) + jnp.concatenate([carry] * reps, axis=-1)
    carry = carry + jnp.sum(onehot, axis=1, keepdims=True)
    carry_ref[...] = carry
    cnt_ref[...] = carry

    row = lax.broadcasted_iota(jnp.int32, (idx_ref.shape[0], tm), 0)
    idx_o = jnp.zeros(row.shape, F32)
    rank_o = jnp.zeros(row.shape, F32)
    gate_o = jnp.zeros(row.shape, F32)
    for k in range(TOP_K):
        rank = jnp.sum(jnp.where(hits[k], before, 0.0), axis=0, keepdims=True)
        idx_o = jnp.where(row == k, idxs[k], idx_o)
        rank_o = jnp.where(row == k, rank, rank_o)
        gate_o = jnp.where(row == k, svals[k] / ssum * ROUTED_SCALE, gate_o)
    idx_ref[...] = idx_o.astype(jnp.int32)
    rank_ref[...] = rank_o.astype(jnp.int32)
    gate_pad = jnp.concatenate([gate_o, jnp.zeros((LANES - gate_o.shape[0], tm), F32)], axis=0)
    gate_ref[...] = gate_pad.T

    gu = _dot(h_hi, wsg_ref[...])
    hs = gu.shape[1] // 2
    act = (jax.nn.silu(gu[:, :hs]) * gu[:, hs:]).astype(BF16)
    base_ref[...] = x_new + g2_ref[...] * _dot(act, wsd_ref[...])


def _post(att, gmn, x2, mod4, oa, w_out, nffn, wr_t, rb, w_sg, w_sd, *, tm, tokens_per_batch, tok0):
    t, att_w = att.shape
    d = x2.shape[1]
    n_exp = wr_t.shape[0]
    gm_w = gmn.shape[1]
    tpb = tokens_per_batch // tm
    blk0 = tok0 // tm
    const = lambda shape: pl.BlockSpec(shape, lambda i: (0,) * len(shape))
    mrow = lambda j: pl.BlockSpec((None, None, 1, d), lambda i: ((i + blk0) // tpb, j, 0, 0))
    by_choice = jax.ShapeDtypeStruct((CHOICE_ROWS, t), jnp.int32)
    return pl.pallas_call(
        _post_kernel,
        out_shape=(jax.ShapeDtypeStruct((t, d), F32), jax.ShapeDtypeStruct((t, d // 2), jnp.int32),
                   by_choice, jax.ShapeDtypeStruct((t, LANES), F32), by_choice,
                   jax.ShapeDtypeStruct((n_exp, LANES), F32)),
        grid=(t // tm,),
        in_specs=[pl.BlockSpec((tm, att_w), lambda i: (i, 0)),
                  pl.BlockSpec((tm, gm_w), lambda i: (i + blk0, 0)),
                  pl.BlockSpec((tm, d), lambda i: (i + blk0, 0)),
                  mrow(2), mrow(3), mrow(4), mrow(5),
                  const((1, att_w)), const(w_out.shape), const((1, d)),
                  const(wr_t.shape), const(rb.shape),
                  const(w_sg.shape), const(w_sd.shape)],
        out_specs=(pl.BlockSpec((tm, d), lambda i: (i, 0)), pl.BlockSpec((tm, d // 2), lambda i: (i, 0)),
                   pl.BlockSpec((CHOICE_ROWS, tm), lambda i: (0, i)), pl.BlockSpec((tm, LANES), lambda i: (i, 0)),
                   pl.BlockSpec((CHOICE_ROWS, tm), lambda i: (0, i)), pl.BlockSpec((n_exp, LANES), lambda i: (0, 0))),
        scratch_shapes=[pltpu.VMEM((n_exp, LANES), F32)],
        compiler_params=_cparams("arbitrary"),
        name="post",
    )(att, gmn, x2, mod4, mod4, mod4, mod4, oa, w_out, nffn, wr_t, rb, w_sg, w_sd)


def _sc_mesh_info():
    info = plsc.get_sparse_core_info()
    mesh = plsc.VectorSubcoreMesh(core_axis_name="c", subcore_axis_name="s")
    return mesh, info.num_cores, info.num_subcores, info.num_lanes


def _sc_params():
    cp = pltpu.CompilerParams()
    if "needs_layout_passes" in pltpu.CompilerParams.__dataclass_fields__:
        cp = dataclasses.replace(cp, needs_layout_passes=False)
    return cp


def _dispatch(h2p, idx_t, rank_t, offs, *, n_slots):
    t, dw = h2p.shape
    mesh, n_cores, n_sub, n_lanes = _sc_mesh_info()
    per_worker = t // (n_cores * n_sub)
    w = SC_WINDOW
    assert per_worker % w == 0 and w % n_lanes == 0

    @functools.partial(
        pl.kernel, mesh=mesh,
        out_type=(jax.ShapeDtypeStruct((n_slots, dw), jnp.int32), jax.ShapeDtypeStruct((CHOICE_ROWS, t), jnp.int32)),
        scratch_types=[pltpu.VMEM(offs.shape, jnp.int32)] + [pltpu.VMEM((TOP_K, w), jnp.int32)] * 3
        + [pltpu.VMEM((w, dw), jnp.int32), pltpu.SemaphoreType.DMA],
        compiler_params=_sc_params(),
    )
    def run(h_hbm, idx_hbm, rank_hbm, offs_hbm, xs_hbm, pos_hbm, offs_v, idx_v, rank_v, pos_v, rows_v, sem):
        base = (lax.axis_index("s") * n_cores + lax.axis_index("c")) * per_worker
        pltpu.sync_copy(offs_hbm, offs_v)

        @pl.loop(0, per_worker // w)
        def _(i):
            t0 = base + i * w
            pltpu.sync_copy(h_hbm.at[pl.ds(t0, w)], rows_v)
            for k in range(TOP_K):
                pltpu.sync_copy(idx_hbm.at[k, pl.ds(t0, w)], idx_v.at[k])
                pltpu.sync_copy(rank_hbm.at[k, pl.ds(t0, w)], rank_v.at[k])
            for k in range(TOP_K):
                for j in range(0, w, n_lanes):
                    group_start = plsc.load_gather(offs_v, [idx_v[k, pl.ds(j, n_lanes)]])
                    pos_v[k, pl.ds(j, n_lanes)] = group_start + rank_v[k, pl.ds(j, n_lanes)]
            copies = [pltpu.async_copy(rows_v, xs_hbm.at[pos_v.at[k]], sem) for k in range(TOP_K)]
            for k in range(TOP_K):
                pltpu.sync_copy(pos_v.at[k], pos_hbm.at[k, pl.ds(t0, w)])
            for cp in copies:
                cp.wait()

    return run(h2p, idx_t, rank_t, offs)


def _collect(ys, pos_t):
    t = pos_t.shape[1]
    dw = ys.shape[1]
    mesh, n_cores, n_sub, _ = _sc_mesh_info()
    per_worker = t // (n_cores * n_sub)
    w = SC_WINDOW
    assert per_worker % w == 0

    @functools.partial(
        pl.kernel, mesh=mesh,
        out_type=jax.ShapeDtypeStruct((TOP_K * t, dw), jnp.int32),
        scratch_types=[pltpu.VMEM((TOP_K, w), jnp.int32), pltpu.VMEM((w, dw), jnp.int32), pltpu.SemaphoreType.DMA],
        compiler_params=_sc_params(),
    )
    def run(ys_hbm, pos_hbm, r_hbm, pos_v, rows_v, sem):
        base = (lax.axis_index("s") * n_cores + lax.axis_index("c")) * per_worker

        @pl.loop(0, per_worker // w)
        def _(i):
            t0 = base + i * w
            for k in range(TOP_K):
                pltpu.sync_copy(pos_hbm.at[k, pl.ds(t0, w)], pos_v.at[k])
            for k in range(TOP_K):
                pltpu.async_copy(ys_hbm.at[pos_v.at[k]], rows_v, sem).wait()
                pltpu.sync_copy(rows_v, r_hbm.at[pl.ds(k * t + t0, w)])

    return run(ys, pos_t)


def _experts_kernel(te_ref, nu_ref, xs_ref, wgu_ref, wdn_ref, ys_ref, wgu_b, wdn_b):
    i = pl.program_id(0)

    @pl.when(i < nu_ref[0])
    def _():
        @pl.when(jnp.logical_or(i == 0, te_ref[i] != te_ref[jnp.maximum(i - 1, 0)]))
        def _():
            wgu_b[...] = wgu_ref[...].astype(BF16)
            wdn_b[...] = wdn_ref[...].astype(BF16)

        half = xs_ref.shape[1]
        lo, hi = _unpack_rows(xs_ref[...])
        gu = _dot(lo.astype(BF16), wgu_b[:half, :]) + _dot(hi.astype(BF16), wgu_b[half:, :])
        hh = gu.shape[1] // 2
        act = (jax.nn.silu(gu[:, :hh]) * gu[:, hh:]).astype(BF16)
        ys_ref[...] = _pack_rows(_dot(act, wdn_b[...]))


def _experts(tile_expert, n_used, xs, w_gu, w_dn, *, tr):
    nt = tile_expert.shape[0]
    _, d, h2w = w_gu.shape
    hh = w_dn.shape[1]
    wmap = lambda i, te, nu: (te[jnp.minimum(i, nu[0] - 1)], 0, 0)
    return pl.pallas_call(
        _experts_kernel,
        out_shape=jax.ShapeDtypeStruct(xs.shape, jnp.int32),
        grid_spec=pltpu.PrefetchScalarGridSpec(
            num_scalar_prefetch=2,
            grid=(nt,),
            in_specs=[pl.BlockSpec((tr, d // 2), lambda i, te, nu: (jnp.minimum(i, nu[0] - 1), 0)),
                      pl.BlockSpec((None, d, h2w), wmap),
                      pl.BlockSpec((None, hh, d), wmap)],
            out_specs=pl.BlockSpec((tr, d // 2), lambda i, te, nu: (jnp.where(i < nu[0], i, nt - 1), 0)),
            scratch_shapes=[pltpu.VMEM((d, h2w), BF16), pltpu.VMEM((hh, d), BF16)]),
        compiler_params=_cparams("arbitrary"),
        name="experts",
    )(tile_expert, n_used, xs, w_gu, w_dn)


def _combine_kernel(base_ref, g2_ref, gate_ref, *refs):
    r_refs, o_ref = refs[:TOP_K], refs[TOP_K]
    gate = gate_ref[...]
    half = base_ref.shape[1] // 2
    acc_lo = jnp.zeros((base_ref.shape[0], half), F32)
    acc_hi = jnp.zeros((base_ref.shape[0], half), F32)
    for k in range(TOP_K):
        lo, hi = _unpack_rows(r_refs[k][...])
        g = gate[:, k:k + 1]
        acc_lo = acc_lo + g * lo
        acc_hi = acc_hi + g * hi
    g2 = g2_ref[...]
    o_ref[:, :half] = base_ref[:, :half] + g2[:, :half] * acc_lo
    o_ref[:, half:] = base_ref[:, half:] + g2[:, half:] * acc_hi


def _combine(base, mod4, gate, r, out_so_far, *, tm, tokens_per_batch, tok0, total_tokens):
    t, d = base.shape
    tpb = tokens_per_batch // tm
    nblk = t // tm
    blk0 = tok0 // tm
    r_spec = lambda k: pl.BlockSpec((tm, d // 2), lambda i: (k * nblk + i, 0))
    in_specs = [pl.BlockSpec((tm, d), lambda i: (i, 0)),
                pl.BlockSpec((None, None, 1, d), lambda i: ((i + blk0) // tpb, 5, 0, 0)),
                pl.BlockSpec((tm, LANES), lambda i: (i, 0))] + [r_spec(k) for k in range(TOP_K)]
    operands = [base, mod4, gate] + [r] * TOP_K
    aliases = {}
    body = _combine_kernel
    if out_so_far is not None:
        in_specs.append(pl.BlockSpec(memory_space=pl.ANY))
        aliases = {len(operands): 0}
        operands.append(out_so_far)
        body = lambda *refs: _combine_kernel(*refs[:3 + TOP_K], refs[-1])
    return pl.pallas_call(
        body,
        out_shape=jax.ShapeDtypeStruct((total_tokens, d), F32),
        grid=(nblk,),
        in_specs=in_specs,
        out_specs=pl.BlockSpec((tm, d), lambda i: (i + blk0, 0)),
        input_output_aliases=aliases,
        compiler_params=_cparams("arbitrary"),
        name="combine",
    )(*operands)


def _rope_tables(length, head_dim):
    rows = length // GRID_W
    pairs_axis = head_dim // 4
    r = jnp.repeat(jnp.arange(rows, dtype=F32), GRID_W)
    col = jnp.tile(jnp.arange(GRID_W, dtype=F32), rows)
    inv = ROPE_THETA ** (-jnp.arange(pairs_axis, dtype=F32) / pairs_axis)
    ang = jnp.concatenate([r[:, None] * inv, col[:, None] * inv], axis=-1)
    cos, sin = jnp.cos(ang), jnp.sin(ang)
    cos_h = jnp.concatenate([cos, cos], axis=-1)
    sin_h = jnp.concatenate([-sin, sin], axis=-1)
    reps = LANES // head_dim
    return jnp.tile(cos_h, (1, reps)), jnp.tile(sin_h, (1, reps))


def kernel(x, c, ctx, c_ctx, w_ada, b_ada, norm_mix, w_in, q_norm, k_norm, gm_norm, w_spatial, b_spatial,
           out_norm_attn, out_norm_gm, w_out, norm_ffn, w_router, router_bias, w_exp_gu, w_exp_down,
           w_sh_gu, w_sh_down):
    assert w_ada.shape[0] == 1, "single-layer kernel"
    b, length, d = x.shape
    head_dim = q_norm.shape[-1]
    att_w = out_norm_attn.shape[-1]
    gm_w = out_norm_gm.shape[-1]
    gm_groups = w_spatial.shape[1]
    gm_hd = gm_w // gm_groups
    n_exp = w_router.shape[-1]
    t = b * length
    assert head_dim == LANES // 2 and gm_hd == LANES // 2 and w_spatial.shape[-1] == CHUNK
    assert n_exp <= LANES and att_w % MXU_EDGE == 0 and gm_w % MXU_EDGE == 0

    mod_rows = 16
    c_all = jnp.zeros((mod_rows, d), F32).at[:b].set(c).at[b].set(c_ctx)
    mod = _ada(c_all, w_ada[0], b_ada)
    mod4 = mod.reshape(mod_rows, N_MOD, 1, d)

    w_in_b = w_in[0].astype(BF16)
    bd = (jnp.arange(MXU_EDGE)[:, None] // head_dim == jnp.arange(MXU_EDGE)[None, :] // head_dim)
    bd = (bd.astype(F32) / head_dim).astype(BF16)
    qg = jnp.tile(q_norm[0] * (head_dim ** -0.5 * LOG2_E), att_w // head_dim)[None, :]
    kg = jnp.tile(k_norm[0], LANES // head_dim)[None, :]
    gmg = jnp.tile(gm_norm[0], gm_groups)[None, :]
    ws = w_spatial[0].astype(BF16).reshape(gm_groups // 2, 2 * CHUNK, CHUNK)
    bs = jnp.repeat(b_spatial[0].T, gm_hd, axis=1)
    cos_t, sin_t = _rope_tables(length, head_dim)

    k_c, v_c = _ctx_kv(ctx, mod4, b, norm_mix, w_in_b[:, att_w:att_w + 2 * LANES], kg, bd)
    q, k_l, v_l, gmn = _inproj(x, mod4, norm_mix, w_in_b, cos_t, sin_t, qg, kg, gmg, bd, ws, bs,
                               out_norm_gm, att_w=att_w, gm_w=gm_w, tm=min(512, length))
    wr_t = w_router[0].T.astype(BF16)
    rb = jnp.tile(router_bias[0][:, None], (1, LANES))
    w_out_b, w_sg_b, w_sd_b = w_out[0].astype(BF16), w_sh_gu[0].astype(BF16), w_sh_down[0].astype(BF16)
    gmn2, x2 = gmn.reshape(t, gm_w), x.reshape(t, d)
    tm_post = min(512, length)
    tr = 512

    n_parts = MOE_PARTS if b % MOE_PARTS == 0 else 1
    nb = b // n_parts
    tp = nb * length
    nt = tp * TOP_K // tr + n_exp

    def route(part):
        att = _attention(q, k_c, k_l, v_c, v_l, tq=min(256, length), b0=part * nb, nb=nb)
        base, h2p, idx_t, gate, rank_t, cnt = _post(
            att.reshape(tp, att_w), gmn2, x2, mod4, out_norm_attn, w_out_b, norm_ffn, wr_t, rb, w_sg_b, w_sd_b,
            tm=tm_post, tokens_per_batch=length, tok0=part * tp)
        counts = cnt[:, 0].astype(jnp.int32)
        padded = (counts + tr - 1) // tr * tr
        ends = jnp.cumsum(padded)
        offs = ends - padded
        tile_start = jnp.arange(nt, dtype=jnp.int32) * tr
        tile_expert = jnp.minimum(jnp.sum(tile_start[:, None] >= ends[None, :], axis=1), n_exp - 1).astype(jnp.int32)
        n_used = (ends[-1:] // tr).astype(jnp.int32)
        offs_pad = jnp.zeros((LANES,), jnp.int32).at[:n_exp].set(offs)
        xs, pos_t = _dispatch(h2p, idx_t, rank_t, offs_pad, n_slots=nt * tr)
        return base, gate, xs, pos_t, tile_expert, n_used

    routed = [route(part) for part in range(n_parts)]
    collected = []
    for base, gate, xs, pos_t, tile_expert, n_used in routed:
        ys = _experts(tile_expert, n_used, xs, w_exp_gu[0], w_exp_down[0], tr=tr)
        collected.append((base, gate, _collect(ys, pos_t)))
    out = None
    for part, (base, gate, r) in enumerate(collected):
        out = _combine(base, mod4, gate, r, out, tm=min(256, length), tokens_per_batch=length,
                       tok0=part * tp, total_tokens=t)
    return out.reshape(b, length, d)
```

```python
import dataclasses
import functools

import jax
import jax.numpy as jnp
from jax import lax
from jax.experimental import pallas as pl
from jax.experimental.pallas import tpu as pltpu
from jax.experimental.pallas import tpu_sc as plsc

F32 = jnp.float32
BF16 = jnp.bfloat16

EPS = 1e-6
GRID_W = 64
ROPE_THETA = 10000.0
ATT_KV_HEADS = 2
TOP_K = 6
ROUTED_SCALE = 2.5
N_MOD = 6
CHUNK = 128
LOG2_E = 1.4426950408889634

LANES = 128
MXU_EDGE = 256
VMEM_LIMIT_BYTES = 56 * 1024 * 1024
SUBLANES = 8
CHOICE_ROWS = SUBLANES
SC_WINDOW = 64
MOE_PARTS = 2


def _cparams(*sem):
    return pltpu.CompilerParams(dimension_semantics=sem, vmem_limit_bytes=VMEM_LIMIT_BYTES)


def _dot(a, b):
    return jnp.dot(a, b, preferred_element_type=F32)


def _dot_nt(a, b):
    return lax.dot_general(a, b, (((1,), (1,)), ((), ())), preferred_element_type=F32)


def _rms(x, g):
    return x * lax.rsqrt(jnp.mean(x * x, axis=-1, keepdims=True) + EPS) * g


def _group_mean_sq(x, bd_ref):
    x2 = (x * x).astype(BF16)
    parts = [_dot(x2[:, i:i + MXU_EDGE], bd_ref[...]) for i in range(0, x.shape[1], MXU_EDGE)]
    return parts[0] if len(parts) == 1 else jnp.concatenate(parts, axis=-1)


def _dup_halves(x, lane):
    xr = pltpu.roll(x, LANES // 2, axis=1)
    lo = lane < LANES // 2
    return jnp.where(lo, x, xr), jnp.where(lo, xr, x)


def _with_ones(x, lane):
    xr = pltpu.roll(x, LANES // 2, axis=1)
    lo = lane < LANES // 2
    return jnp.concatenate([jnp.where(lo, x, 1.0), jnp.where(lo, xr, 1.0)], axis=-1)


_HI16 = 0xFFFF0000


def _pack_rows(x):
    half = x.shape[1] // 2
    rounded = x.astype(BF16).astype(F32)
    bits = lax.bitcast_convert_type(rounded, jnp.uint32)
    word = (bits[:, :half] >> 16) | (bits[:, half:] & jnp.uint32(_HI16))
    return lax.bitcast_convert_type(word, jnp.int32)


def _unpack_rows(w):
    bits = lax.bitcast_convert_type(w, jnp.uint32)
    lo = lax.bitcast_convert_type(bits << 16, F32)
    hi = lax.bitcast_convert_type(bits & jnp.uint32(_HI16), F32)
    return lo, hi


def _ada_kernel(c_ref, w_ref, b_ref, o_ref):
    s = jax.nn.silu(c_ref[...])
    o_ref[...] = jnp.dot(s, w_ref[...], precision=lax.Precision.HIGHEST,
                         preferred_element_type=F32) + b_ref[...]


def _ada(c_all, w_ada, b_ada):
    rows, d = c_all.shape
    n = w_ada.shape[1]
    bn = 512
    return pl.pallas_call(
        _ada_kernel,
        out_shape=jax.ShapeDtypeStruct((rows, n), F32),
        grid=(n // bn,),
        in_specs=[pl.BlockSpec((rows, d), lambda j: (0, 0)),
                  pl.BlockSpec((d, bn), lambda j: (0, j)),
                  pl.BlockSpec((1, bn), lambda j: (0, j))],
        out_specs=pl.BlockSpec((rows, bn), lambda j: (0, j)),
        compiler_params=_cparams("arbitrary"),
        name="ada",
    )(c_all, w_ada, b_ada)


def _ctx_kv_kernel(x_ref, sh_ref, sc_ref, nmix_ref, w_ref, kg_ref, bd_ref, k_ref, v_ref):
    h = _rms(x_ref[...], nmix_ref[...]) * (1.0 + sc_ref[...]) + sh_ref[...]
    p = _dot(h.astype(BF16), w_ref[...])
    ms = _group_mean_sq(p, bd_ref)
    k = p[:, :LANES] * lax.rsqrt(ms[:, :LANES] + EPS) * kg_ref[...]
    v = p[:, LANES:]
    lane = lax.broadcasted_iota(jnp.int32, k.shape, 1)
    ka, kb = _dup_halves(k, lane)
    k_ref[...] = jnp.concatenate([ka, kb], axis=-1).astype(BF16)
    v_ref[...] = _with_ones(v, lane).astype(BF16)


def _ctx_kv(ctx, mod4, ctx_row, nmix, w_kv, kg, bd):
    b, n_ctx, d = ctx.shape
    kv_w = 2 * LANES
    return pl.pallas_call(
        _ctx_kv_kernel,
        out_shape=(jax.ShapeDtypeStruct((b, n_ctx, kv_w), BF16),) * 2,
        grid=(b,),
        in_specs=[pl.BlockSpec((None, n_ctx, d), lambda i: (i, 0, 0)),
                  pl.BlockSpec((None, None, 1, d), lambda i: (ctx_row, 0, 0, 0)),
                  pl.BlockSpec((None, None, 1, d), lambda i: (ctx_row, 1, 0, 0)),
                  pl.BlockSpec((1, d), lambda i: (0, 0)),
                  pl.BlockSpec(w_kv.shape, lambda i: (0, 0)),
                  pl.BlockSpec((1, LANES), lambda i: (0, 0)),
                  pl.BlockSpec(bd.shape, lambda i: (0, 0))],
        out_specs=(pl.BlockSpec((None, n_ctx, kv_w), lambda i: (i, 0, 0)),) * 2,
        compiler_params=_cparams("arbitrary"),
        name="ctx_kv",
    )(ctx, mod4, mod4, nmix, w_kv, kg, bd)


def _inproj_kernel(x_ref, sh_ref, sc_ref, nmix_ref, w_ref, cos_ref, sin_ref, qg_ref, kg_ref, gmg_ref,
                   bd_ref, ws_ref, bs_ref, og_ref, q_ref, k_ref, v_ref, gm_ref, *, att_w, gm_w):
    tm = x_ref.shape[0]
    h = _rms(x_ref[...], nmix_ref[...]) * (1.0 + sc_ref[...]) + sh_ref[...]
    p = _dot(h.astype(BF16), w_ref[...])

    cos1, sin1 = cos_ref[...], sin_ref[...]
    lane1 = lax.broadcasted_iota(jnp.int32, (tm, LANES), 1)

    def rope(xn, cosw, sinw, lanew):
        w = xn.shape[1]
        fwd = pltpu.roll(xn, w - 32, axis=1)
        bwd = pltpu.roll(xn, 32, axis=1)
        swapped = jnp.where((lanew & 63) < 32, fwd, bwd)
        return xn * cosw + swapped * sinw

    q = p[:, :att_w]
    qn = q * lax.rsqrt(_group_mean_sq(q, bd_ref) + EPS) * qg_ref[...]
    reps = att_w // LANES
    cosq = jnp.concatenate([cos1] * reps, axis=-1)
    sinq = jnp.concatenate([sin1] * reps, axis=-1)
    laneq = lax.broadcasted_iota(jnp.int32, (tm, att_w), 1)
    q_ref[...] = rope(qn, cosq, sinq, laneq).astype(BF16)

    kv = p[:, att_w:att_w + 2 * LANES]
    ms = _group_mean_sq(kv, bd_ref)
    kn = kv[:, :LANES] * lax.rsqrt(ms[:, :LANES] + EPS) * kg_ref[...]
    k = rope(kn, cos1, sin1, lane1)
    v = kv[:, LANES:]
    ka, kb = _dup_halves(k, lane1)
    k_ref[...] = jnp.concatenate([ka, kb], axis=-1).astype(BF16)
    v_ref[...] = _with_ones(v, lane1).astype(BF16)

    u0 = att_w + 2 * LANES
    u = jax.nn.gelu(p[:, u0:u0 + gm_w])
    gt = jax.nn.gelu(p[:, u0 + gm_w:u0 + 2 * gm_w])
    gt = (gt * lax.rsqrt(_group_mean_sq(gt, bd_ref) + EPS) * gmg_ref[...]).astype(BF16)
    lo_half = lax.broadcasted_iota(jnp.int32, (CHUNK, LANES), 1) < LANES // 2
    rows = []
    for c in range(tm // CHUNK):
        blocks = []
        for j in range(gm_w // LANES):
            g = gt[c * CHUNK:(c + 1) * CHUNK, j * LANES:(j + 1) * LANES]
            r = _dot(ws_ref[j], g)
            blocks.append(jnp.where(lo_half, r[:CHUNK], r[CHUNK:]))
        rows.append(jnp.concatenate(blocks, axis=-1) + bs_ref[...])
    mixed = jnp.concatenate(rows, axis=0)
    gm_ref[...] = _rms(u * mixed, og_ref[...]).astype(BF16)


def _inproj(x, mod4, nmix, w_in, cos_t, sin_t, qg, kg, gmg, bd, ws, bs, og, *, att_w, gm_w, tm):
    b, length, d = x.shape
    kv_w = 2 * LANES
    n_in = w_in.shape[1]
    const = lambda shape: pl.BlockSpec(shape, lambda i, j: (0,) * len(shape))
    return pl.pallas_call(
        functools.partial(_inproj_kernel, att_w=att_w, gm_w=gm_w),
        out_shape=(jax.ShapeDtypeStruct((b, length, att_w), BF16),
                   jax.ShapeDtypeStruct((b, length, kv_w), BF16),
                   jax.ShapeDtypeStruct((b, length, kv_w), BF16),
                   jax.ShapeDtypeStruct((b, length, gm_w), BF16)),
        grid=(b, length // tm),
        in_specs=[pl.BlockSpec((None, tm, d), lambda i, j: (i, j, 0)),
                  pl.BlockSpec((None, None, 1, d), lambda i, j: (i, 0, 0, 0)),
                  pl.BlockSpec((None, None, 1, d), lambda i, j: (i, 1, 0, 0)),
                  const((1, d)),
                  const((d, n_in)),
                  pl.BlockSpec((tm, LANES), lambda i, j: (j, 0)),
                  pl.BlockSpec((tm, LANES), lambda i, j: (j, 0)),
                  const((1, att_w)), const((1, LANES)), const((1, gm_w)),
                  const(bd.shape), const(ws.shape), const(bs.shape), const((1, gm_w))],
        out_specs=(pl.BlockSpec((None, tm, att_w), lambda i, j: (i, j, 0)),
                   pl.BlockSpec((None, tm, kv_w), lambda i, j: (i, j, 0)),
                   pl.BlockSpec((None, tm, kv_w), lambda i, j: (i, j, 0)),
                   pl.BlockSpec((None, tm, gm_w), lambda i, j: (i, j, 0))),
        compiler_params=_cparams("arbitrary", "arbitrary"),
        name="inproj",
    )(x, mod4, mod4, nmix, w_in, cos_t, sin_t, qg, kg, gmg, bd, ws, bs, og)


def _attn_kernel(q_ref, kc_ref, kl_ref, vc_ref, vl_ref, o_ref, s_buf, p_buf, *, tq):
    n_ctx = kc_ref.shape[0]
    heads = q_ref.shape[1] // (LANES // 2)
    nj = q_ref.shape[0] // tq
    half = LANES // 2
    lo = lax.broadcasted_iota(jnp.int32, (tq, LANES), 1) < half

    s_buf[...] = jnp.zeros_like(s_buf)
    p_buf[...] = jnp.ones_like(p_buf)

    def scores(j, g):
        q = q_ref[pl.ds(j * tq, tq), (g // 2) * LANES:(g // 2 + 1) * LANES]
        zero = jnp.zeros_like(q)
        qh = jnp.where(lo, q, zero) if g % 2 == 0 else jnp.where(lo, zero, q)
        s_buf[g % 2, :, :n_ctx] = _dot_nt(qh, kc_ref[...])
        s_buf[g % 2, :, n_ctx:] = _dot_nt(qh, kl_ref[...])

    def probs(g):
        s = s_buf[g % 2]
        p_buf[g % 2] = jnp.exp2(s - jnp.max(s, axis=-1, keepdims=True)).astype(BF16)

    def output(j, g):
        p = p_buf[g % 2]
        w = _dot(p[:, :n_ctx], vc_ref[...]) + _dot(p[:, n_ctx:], vl_ref[...])
        t = jnp.where(pl.program_id(1) == 0, w[:, :LANES], w[:, LANES:])
        r = pltpu.roll(t, half, axis=1)
        rows = pl.ds(j * tq, tq)
        if g % 2 == 0:
            o_ref[rows, g * half:(g + 1) * half] = (t / r)[:, :half].astype(o_ref.dtype)
        else:
            o_ref[rows, g * half:(g + 1) * half] = (r / t)[:, half:].astype(o_ref.dtype)

    def trip(j, carry):
        j_prev = jnp.maximum(j - 1, 0)
        for g in range(heads):
            scores(j, g)
            probs((g - 1) % heads)
            output(j if g >= 2 else j_prev, (g - 2) % heads)
        return carry

    lax.fori_loop(0, nj, trip, 0)
    probs(heads - 1)
    output(nj - 1, heads - 2)
    output(nj - 1, heads - 1)


def _attention(q, k_c, k_l, v_c, v_l, att_so_far, *, tq, b0, nb, slot0, slots):
    _, length, att_w = q.shape
    n_ctx = k_c.shape[1]
    group_w = att_w // ATT_KV_HEADS
    assert (group_w // (LANES // 2)) % 2 == 0 and length % tq == 0
    k_blk = lambda n: pl.BlockSpec((None, n, LANES), lambda i, h: (i + b0, 0, h))
    v_blk = lambda n: pl.BlockSpec((None, n, ATT_KV_HEADS * LANES), lambda i, h: (i + b0, 0, 0))
    in_specs = [pl.BlockSpec((None, length, group_w), lambda i, h: (i + b0, 0, h)),
                k_blk(n_ctx), k_blk(length), v_blk(n_ctx), v_blk(length)]
    operands = [q, k_c, k_l, v_c, v_l]
    aliases = {}
    body = functools.partial(_attn_kernel, tq=tq)
    if att_so_far is not None:
        in_specs.append(pl.BlockSpec(memory_space=pl.ANY))
        aliases = {len(operands): 0}
        operands.append(att_so_far)
        body = lambda *refs: _attn_kernel(*refs[:5], *refs[6:], tq=tq)
    return pl.pallas_call(
        body,
        out_shape=jax.ShapeDtypeStruct((slots, length, att_w), BF16),
        grid=(nb, ATT_KV_HEADS),
        in_specs=in_specs,
        out_specs=pl.BlockSpec((None, length, group_w), lambda i, h: (i + slot0, 0, h)),
        scratch_shapes=[pltpu.VMEM((2, tq, n_ctx + length), F32), pltpu.VMEM((2, tq, n_ctx + length), BF16)],
        input_output_aliases=aliases,
        compiler_params=_cparams("arbitrary", "arbitrary"),
        name="attn",
    )(*operands)


def _post_kernel(att_ref, gm_ref, x_ref, g1_ref, sh2_ref, sc2_ref, g2_ref, oa_ref, wo_ref, nffn_ref,
                 wrt_ref, rb_ref, wsg_ref, wsd_ref,
                 base_ref, h2p_ref, idx_ref, gate_ref, rank_ref, cnt_ref, carry_ref):
    i = pl.program_id(0)
    tm = x_ref.shape[0]
    n_exp = wrt_ref.shape[0]

    @pl.when(i == 0)
    def _():
        carry_ref[...] = jnp.zeros_like(carry_ref)

    att_n = _rms(att_ref[...].astype(F32), oa_ref[...]).astype(BF16)
    y = _dot(jnp.concatenate([att_n, gm_ref[...]], axis=-1), wo_ref[...])
    x_new = x_ref[...] + g1_ref[...] * y
    h2 = _rms(x_new, nffn_ref[...]) * (1.0 + sc2_ref[...]) + sh2_ref[...]
    h2p_ref[...] = _pack_rows(h2)

    h_hi = h2.astype(BF16)
    scores_t = jax.nn.sigmoid(_dot_nt(wrt_ref[...], h_hi))
    reps = tm // LANES
    sel = scores_t + jnp.concatenate([rb_ref[...]] * reps, axis=-1)
    expert = lax.broadcasted_iota(jnp.int32, scores_t.shape, 0).astype(F32)
    neg = jnp.float32(-jnp.inf)

    onehot = jnp.zeros(scores_t.shape, F32)
    hits, idxs, svals = [], [], []
    for _k in range(TOP_K):
        m = jnp.max(sel, axis=0, keepdims=True)
        idx = jnp.min(jnp.where(sel == m, expert, float(n_exp)), axis=0, keepdims=True)
        hit = expert == idx
        hits.append(hit)
        idxs.append(idx)
        svals.append(jnp.sum(jnp.where(hit, scores_t, 0.0), axis=0, keepdims=True))
        onehot = onehot + jnp.where(hit, 1.0, 0.0)
        sel = jnp.where(hit, neg, sel)
    ssum = functools.reduce(lambda a, b: a + b, svals)

    r_io = lax.broadcasted_iota(jnp.int32, (tm, tm), 0)
    c_io = lax.broadcasted_iota(jnp.int32, (tm, tm), 1)
    tri = jnp.where(r_io < c_io, 1.0, 0.0).astype(BF16)
    carry = carry_ref[...]
    before = _dot(onehot.astype(BF16), tri) + jnp.concatenate([carry] * reps, axis=-1)
    carry = carry + jnp.sum(onehot, axis=1, keepdims=True)
    carry_ref[...] = carry
    cnt_ref[...] = carry

    row = lax.broadcasted_iota(jnp.int32, (idx_ref.shape[0], tm), 0)
    idx_o = jnp.zeros(row.shape, F32)
    rank_o = jnp.zeros(row.shape, F32)
    gate_o = jnp.zeros(row.shape, F32)
    for k in range(TOP_K):
        rank = jnp.sum(jnp.where(hits[k], before, 0.0), axis=0, keepdims=True)
        idx_o = jnp.where(row == k, idxs[k], idx_o)
        rank_o = jnp.where(row == k, rank, rank_o)
        gate_o = jnp.where(row == k, svals[k] / ssum * ROUTED_SCALE, gate_o)
    idx_ref[...] = idx_o.astype(jnp.int32)
    rank_ref[...] = rank_o.astype(jnp.int32)
    gate_pad = jnp.concatenate([gate_o, jnp.zeros((LANES - gate_o.shape[0], tm), F32)], axis=0)
    gate_ref[...] = gate_pad.T

    gu = _dot(h_hi, wsg_ref[...])
    hs = gu.shape[1] // 2
    act = (jax.nn.silu(gu[:, :hs]) * gu[:, hs:]).astype(BF16)
    base_ref[...] = x_new + g2_ref[...] * _dot(act, wsd_ref[...])


def _post(att, gmn, x2, mod4, oa, w_out, nffn, wr_t, rb, w_sg, w_sd, *, tm, tokens_per_batch, tok0):
    t, att_w = att.shape
    d = x2.shape[1]
    n_exp = wr_t.shape[0]
    gm_w = gmn.shape[1]
    tpb = tokens_per_batch // tm
    blk0 = tok0 // tm
    const = lambda shape: pl.BlockSpec(shape, lambda i: (0,) * len(shape))
    mrow = lambda j: pl.BlockSpec((None, None, 1, d), lambda i: ((i + blk0) // tpb, j, 0, 0))
    by_choice = jax.ShapeDtypeStruct((CHOICE_ROWS, t), jnp.int32)
    return pl.pallas_call(
        _post_kernel,
        out_shape=(jax.ShapeDtypeStruct((t, d), F32), jax.ShapeDtypeStruct((t, d // 2), jnp.int32),
                   by_choice, jax.ShapeDtypeStruct((t, LANES), F32), by_choice,
                   jax.ShapeDtypeStruct((n_exp, LANES), F32)),
        grid=(t // tm,),
        in_specs=[pl.BlockSpec((tm, att_w), lambda i: (i, 0)),
                  pl.BlockSpec((tm, gm_w), lambda i: (i + blk0, 0)),
                  pl.BlockSpec((tm, d), lambda i: (i + blk0, 0)),
                  mrow(2), mrow(3), mrow(4), mrow(5),
                  const((1, att_w)), const(w_out.shape), const((1, d)),
                  const(wr_t.shape), const(rb.shape),
                  const(w_sg.shape), const(w_sd.shape)],
        out_specs=(pl.BlockSpec((tm, d), lambda i: (i, 0)), pl.BlockSpec((tm, d // 2), lambda i: (i, 0)),
                   pl.BlockSpec((CHOICE_ROWS, tm), lambda i: (0, i)), pl.BlockSpec((tm, LANES), lambda i: (i, 0)),
                   pl.BlockSpec((CHOICE_ROWS, tm), lambda i: (0, i)), pl.BlockSpec((n_exp, LANES), lambda i: (0, 0))),
        scratch_shapes=[pltpu.VMEM((n_exp, LANES), F32)],
        compiler_params=_cparams("arbitrary"),
        name="post",
    )(att, gmn, x2, mod4, mod4, mod4, mod4, oa, w_out, nffn, wr_t, rb, w_sg, w_sd)


def _sc_mesh_info():
    info = plsc.get_sparse_core_info()
    mesh = plsc.VectorSubcoreMesh(core_axis_name="c", subcore_axis_name="s")
    return mesh, info.num_cores, info.num_subcores, info.num_lanes


def _sc_params():
    cp = pltpu.CompilerParams()
    if "needs_layout_passes" in pltpu.CompilerParams.__dataclass_fields__:
        cp = dataclasses.replace(cp, needs_layout_passes=False)
    return cp


def _dispatch(h2p, idx_t, rank_t, offs, *, n_slots):
    t, dw = h2p.shape
    mesh, n_cores, n_sub, n_lanes = _sc_mesh_info()
    per_worker = t // (n_cores * n_sub)
    w = SC_WINDOW
    assert per_worker % w == 0 and w % n_lanes == 0

    @functools.partial(
        pl.kernel, mesh=mesh,
        out_type=(jax.ShapeDtypeStruct((n_slots, dw), jnp.int32), jax.ShapeDtypeStruct((CHOICE_ROWS, t), jnp.int32)),
        scratch_types=[pltpu.VMEM(offs.shape, jnp.int32)] + [pltpu.VMEM((TOP_K, w), jnp.int32)] * 3
        + [pltpu.VMEM((w, dw), jnp.int32), pltpu.SemaphoreType.DMA],
        compiler_params=_sc_params(),
    )
    def run(h_hbm, idx_hbm, rank_hbm, offs_hbm, xs_hbm, pos_hbm, offs_v, idx_v, rank_v, pos_v, rows_v, sem):
        base = (lax.axis_index("s") * n_cores + lax.axis_index("c")) * per_worker
        pltpu.sync_copy(offs_hbm, offs_v)

        @pl.loop(0, per_worker // w)
        def _(i):
            t0 = base + i * w
            pltpu.sync_copy(h_hbm.at[pl.ds(t0, w)], rows_v)
            for k in range(TOP_K):
                pltpu.sync_copy(idx_hbm.at[k, pl.ds(t0, w)], idx_v.at[k])
                pltpu.sync_copy(rank_hbm.at[k, pl.ds(t0, w)], rank_v.at[k])
            for k in range(TOP_K):
                for j in range(0, w, n_lanes):
                    group_start = plsc.load_gather(offs_v, [idx_v[k, pl.ds(j, n_lanes)]])
                    pos_v[k, pl.ds(j, n_lanes)] = group_start + rank_v[k, pl.ds(j, n_lanes)]
            copies = [pltpu.async_copy(rows_v, xs_hbm.at[pos_v.at[k]], sem) for k in range(TOP_K)]
            for k in range(TOP_K):
                pltpu.sync_copy(pos_v.at[k], pos_hbm.at[k, pl.ds(t0, w)])
            for cp in copies:
                cp.wait()

    return run(h2p, idx_t, rank_t, offs)


def _collect(ys, pos_t):
    t = pos_t.shape[1]
    dw = ys.shape[1]
    mesh, n_cores, n_sub, _ = _sc_mesh_info()
    per_worker = t // (n_cores * n_sub)
    w = SC_WINDOW
    assert per_worker % w == 0

    @functools.partial(
        pl.kernel, mesh=mesh,
        out_type=jax.ShapeDtypeStruct((TOP_K * t, dw), jnp.int32),
        scratch_types=[pltpu.VMEM((TOP_K, w), jnp.int32), pltpu.VMEM((w, dw), jnp.int32), pltpu.SemaphoreType.DMA],
        compiler_params=_sc_params(),
    )
    def run(ys_hbm, pos_hbm, r_hbm, pos_v, rows_v, sem):
        base = (lax.axis_index("s") * n_cores + lax.axis_index("c")) * per_worker

        @pl.loop(0, per_worker // w)
        def _(i):
            t0 = base + i * w
            for k in range(TOP_K):
                pltpu.sync_copy(pos_hbm.at[k, pl.ds(t0, w)], pos_v.at[k])
            for k in range(TOP_K):
                pltpu.async_copy(ys_hbm.at[pos_v.at[k]], rows_v, sem).wait()
                pltpu.sync_copy(rows_v, r_hbm.at[pl.ds(k * t + t0, w)])

    return run(ys, pos_t)


def _experts_kernel(te_ref, nu_ref, xs_ref, wgu_ref, wdn_ref, ys_ref, wgu_b, wdn_b):
    i = pl.program_id(0)

    @pl.when(i < nu_ref[0])
    def _():
        @pl.when(jnp.logical_or(i == 0, te_ref[i] != te_ref[jnp.maximum(i - 1, 0)]))
        def _():
            wgu_b[...] = wgu_ref[...].astype(BF16)
            wdn_b[...] = wdn_ref[...].astype(BF16)

        half = xs_ref.shape[1]
        lo, hi = _unpack_rows(xs_ref[...])
        gu = _dot(lo.astype(BF16), wgu_b[:half, :]) + _dot(hi.astype(BF16), wgu_b[half:, :])
        hh = gu.shape[1] // 2
        act = (jax.nn.silu(gu[:, :hh]) * gu[:, hh:]).astype(BF16)
        ys_ref[...] = _pack_rows(_dot(act, wdn_b[...]))


def _experts(tile_expert, n_used, xs, w_gu, w_dn, *, tr):
    nt = tile_expert.shape[0]
    _, d, h2w = w_gu.shape
    hh = w_dn.shape[1]
    wmap = lambda i, te, nu: (te[jnp.minimum(i, nu[0] - 1)], 0, 0)
    return pl.pallas_call(
        _experts_kernel,
        out_shape=jax.ShapeDtypeStruct(xs.shape, jnp.int32),
        grid_spec=pltpu.PrefetchScalarGridSpec(
            num_scalar_prefetch=2,
            grid=(nt,),
            in_specs=[pl.BlockSpec((tr, d // 2), lambda i, te, nu: (jnp.minimum(i, nu[0] - 1), 0)),
                      pl.BlockSpec((None, d, h2w), wmap),
                      pl.BlockSpec((None, hh, d), wmap)],
            out_specs=pl.BlockSpec((tr, d // 2), lambda i, te, nu: (jnp.where(i < nu[0], i, nt - 1), 0)),
            scratch_shapes=[pltpu.VMEM((d, h2w), BF16), pltpu.VMEM((hh, d), BF16)]),
        compiler_params=_cparams("arbitrary"),
        name="experts",
    )(tile_expert, n_used, xs, w_gu, w_dn)


def _combine_kernel(base_ref, g2_ref, gate_ref, *refs):
    r_refs, o_ref = refs[:TOP_K], refs[TOP_K]
    gate = gate_ref[...]
    half = base_ref.shape[1] // 2
    acc_lo = jnp.zeros((base_ref.shape[0], half), F32)
    acc_hi = jnp.zeros((base_ref.shape[0], half), F32)
    for k in range(TOP_K):
        lo, hi = _unpack_rows(r_refs[k][...])
        g = gate[:, k:k + 1]
        acc_lo = acc_lo + g * lo
        acc_hi = acc_hi + g * hi
    g2 = g2_ref[...]
    o_ref[:, :half] = base_ref[:, :half] + g2[:, :half] * acc_lo
    o_ref[:, half:] = base_ref[:, half:] + g2[:, half:] * acc_hi


def _combine(base, mod4, gate, r, out_so_far, *, tm, tokens_per_batch, tok0, total_tokens):
    t, d = base.shape
    tpb = tokens_per_batch // tm
    nblk = t // tm
    blk0 = tok0 // tm
    r_spec = lambda k: pl.BlockSpec((tm, d // 2), lambda i: (k * nblk + i, 0))
    in_specs = [pl.BlockSpec((tm, d), lambda i: (i, 0)),
                pl.BlockSpec((None, None, 1, d), lambda i: ((i + blk0) // tpb, 5, 0, 0)),
                pl.BlockSpec((tm, LANES), lambda i: (i, 0))] + [r_spec(k) for k in range(TOP_K)]
    operands = [base, mod4, gate] + [r] * TOP_K
    aliases = {}
    body = _combine_kernel
    if out_so_far is not None:
        in_specs.append(pl.BlockSpec(memory_space=pl.ANY))
        aliases = {len(operands): 0}
        operands.append(out_so_far)
        body = lambda *refs: _combine_kernel(*refs[:3 + TOP_K], refs[-1])
    return pl.pallas_call(
        body,
        out_shape=jax.ShapeDtypeStruct((total_tokens, d), F32),
        grid=(nblk,),
        in_specs=in_specs,
        out_specs=pl.BlockSpec((tm, d), lambda i: (i + blk0, 0)),
        input_output_aliases=aliases,
        compiler_params=_cparams("arbitrary"),
        name="combine",
    )(*operands)


def _rope_tables(length, head_dim):
    rows = length // GRID_W
    pairs_axis = head_dim // 4
    r = jnp.repeat(jnp.arange(rows, dtype=F32), GRID_W)
    col = jnp.tile(jnp.arange(GRID_W, dtype=F32), rows)
    inv = ROPE_THETA ** (-jnp.arange(pairs_axis, dtype=F32) / pairs_axis)
    ang = jnp.concatenate([r[:, None] * inv, col[:, None] * inv], axis=-1)
    cos, sin = jnp.cos(ang), jnp.sin(ang)
    cos_h = jnp.concatenate([cos, cos], axis=-1)
    sin_h = jnp.concatenate([-sin, sin], axis=-1)
    reps = LANES // head_dim
    return jnp.tile(cos_h, (1, reps)), jnp.tile(sin_h, (1, reps))


def kernel(x, c, ctx, c_ctx, w_ada, b_ada, norm_mix, w_in, q_norm, k_norm, gm_norm, w_spatial, b_spatial,
           out_norm_attn, out_norm_gm, w_out, norm_ffn, w_router, router_bias, w_exp_gu, w_exp_down,
           w_sh_gu, w_sh_down):
    assert w_ada.shape[0] == 1, "single-layer kernel"
    b, length, d = x.shape
    head_dim = q_norm.shape[-1]
    att_w = out_norm_attn.shape[-1]
    gm_w = out_norm_gm.shape[-1]
    gm_groups = w_spatial.shape[1]
    gm_hd = gm_w // gm_groups
    n_exp = w_router.shape[-1]
    t = b * length
    assert head_dim == LANES // 2 and gm_hd == LANES // 2 and w_spatial.shape[-1] == CHUNK
    assert n_exp <= LANES and att_w % MXU_EDGE == 0 and gm_w % MXU_EDGE == 0

    mod_rows = 16
    c_all = jnp.zeros((mod_rows, d), F32).at[:b].set(c).at[b].set(c_ctx)
    mod = _ada(c_all, w_ada[0], b_ada)
    mod4 = mod.reshape(mod_rows, N_MOD, 1, d)

    w_in_b = w_in[0].astype(BF16)
    bd = (jnp.arange(MXU_EDGE)[:, None] // head_dim == jnp.arange(MXU_EDGE)[None, :] // head_dim)
    bd = (bd.astype(F32) / head_dim).astype(BF16)
    qg = jnp.tile(q_norm[0] * (head_dim ** -0.5 * LOG2_E), att_w // head_dim)[None, :]
    kg = jnp.tile(k_norm[0], LANES // head_dim)[None, :]
    gmg = jnp.tile(gm_norm[0], gm_groups)[None, :]
    ws = w_spatial[0].astype(BF16).reshape(gm_groups // 2, 2 * CHUNK, CHUNK)
    bs = jnp.repeat(b_spatial[0].T, gm_hd, axis=1)
    cos_t, sin_t = _rope_tables(length, head_dim)

    k_c, v_c = _ctx_kv(ctx, mod4, b, norm_mix, w_in_b[:, att_w:att_w + 2 * LANES], kg, bd)
    q, k_l, v_l, gmn = _inproj(x, mod4, norm_mix, w_in_b, cos_t, sin_t, qg, kg, gmg, bd, ws, bs,
                               out_norm_gm, att_w=att_w, gm_w=gm_w, tm=min(512, length))
    wr_t = w_router[0].T.astype(BF16)
    rb = jnp.tile(router_bias[0][:, None], (1, LANES))
    w_out_b, w_sg_b, w_sd_b = w_out[0].astype(BF16), w_sh_gu[0].astype(BF16), w_sh_down[0].astype(BF16)
    gmn2, x2 = gmn.reshape(t, gm_w), x.reshape(t, d)
    tm_post = min(512, length)
    tr = 512

    n_parts = MOE_PARTS if b % MOE_PARTS == 0 else 1
    nb = b // n_parts
    tp = nb * length
    nt = tp * TOP_K // tr + n_exp

    state = [dict() for _ in range(n_parts)]
    halves = (nb // 2, nb - nb // 2) if nb >= 2 else (nb,)

    def attend(part, half):
        st = state[part]
        slot0 = sum(halves[:half])
        st["att"] = _attention(q, k_c, k_l, v_c, v_l, st.get("att"), tq=min(256, length),
                               b0=part * nb + slot0, nb=halves[half], slot0=slot0, slots=nb)

    def route(part):
        st = state[part]
        base, h2p, idx_t, gate, rank_t, cnt = _post(
            st["att"].reshape(tp, att_w), gmn2, x2, mod4, out_norm_attn, w_out_b, norm_ffn, wr_t, rb, w_sg_b, w_sd_b,
            tm=tm_post, tokens_per_batch=length, tok0=part * tp)
        counts = cnt[:, 0].astype(jnp.int32)
        padded = (counts + tr - 1) // tr * tr
        ends = jnp.cumsum(padded)
        offs = ends - padded
        tile_start = jnp.arange(nt, dtype=jnp.int32) * tr
        tile_expert = jnp.minimum(jnp.sum(tile_start[:, None] >= ends[None, :], axis=1), n_exp - 1).astype(jnp.int32)
        n_used = (ends[-1:] // tr).astype(jnp.int32)
        offs_pad = jnp.zeros((LANES,), jnp.int32).at[:n_exp].set(offs)
        xs, pos_t = _dispatch(h2p, idx_t, rank_t, offs_pad, n_slots=nt * tr)
        st.update(base=base, gate=gate, xs=xs, pos_t=pos_t, tile_expert=tile_expert, n_used=n_used)

    def run_experts(part):
        st = state[part]
        ys = _experts(st["tile_expert"], st["n_used"], st["xs"], w_exp_gu[0], w_exp_down[0], tr=tr)
        st["r"] = _collect(ys, st["pos_t"])

    result = [None]

    def finish(part):
        st = state[part]
        result[0] = _combine(st["base"], mod4, st["gate"], st["r"], result[0], tm=min(256, length),
                             tokens_per_batch=length, tok0=part * tp, total_tokens=t)

    stages = [(0, lambda p: attend(p, 0))]
    if len(halves) == 2:
        stages.append((1, lambda p: attend(p, 1)))
    stages += [(2, route), (4, run_experts), (7, finish)]
    plan = sorted((when + 3 * part, part, n, fn) for part in range(n_parts) for n, (when, fn) in enumerate(stages))
    for _, part, _, fn in plan:
        fn(part)
    return result[0].reshape(b, length, d)
```

```python
import dataclasses
import functools

import jax
import jax.numpy as jnp
from jax import lax
from jax.experimental import pallas as pl
from jax.experimental.pallas import tpu as pltpu
from jax.experimental.pallas import tpu_sc as plsc

F32 = jnp.float32
BF16 = jnp.bfloat16

EPS = 1e-6
GRID_W = 64
ROPE_THETA = 10000.0
ATT_KV_HEADS = 2
TOP_K = 6
ROUTED_SCALE = 2.5
N_MOD = 6
CHUNK = 128
LOG2_E = 1.4426950408889634

LANES = 128
MXU_EDGE = 256
VMEM_LIMIT_BYTES = 56 * 1024 * 1024
SUBLANES = 8
CHOICE_ROWS = SUBLANES
SC_WINDOW = 64
MOE_PARTS = 2


def _cparams(*sem):
    return pltpu.CompilerParams(dimension_semantics=sem, vmem_limit_bytes=VMEM_LIMIT_BYTES)


def _dot(a, b):
    return jnp.dot(a, b, preferred_element_type=F32)


def _dot_nt(a, b):
    return lax.dot_general(a, b, (((1,), (1,)), ((), ())), preferred_element_type=F32)


def _rms(x, g):
    return x * lax.rsqrt(jnp.mean(x * x, axis=-1, keepdims=True) + EPS) * g


def _group_mean_sq(x, bd_ref):
    x2 = (x * x).astype(BF16)
    parts = [_dot(x2[:, i:i + MXU_EDGE], bd_ref[...]) for i in range(0, x.shape[1], MXU_EDGE)]
    return parts[0] if len(parts) == 1 else jnp.concatenate(parts, axis=-1)


def _dup_halves(x, lane):
    xr = pltpu.roll(x, LANES // 2, axis=1)
    lo = lane < LANES // 2
    return jnp.where(lo, x, xr), jnp.where(lo, xr, x)


def _with_ones(x, lane):
    xr = pltpu.roll(x, LANES // 2, axis=1)
    lo = lane < LANES // 2
    return jnp.concatenate([jnp.where(lo, x, 1.0), jnp.where(lo, xr, 1.0)], axis=-1)


_HI16 = 0xFFFF0000


def _pack_rows(x):
    half = x.shape[1] // 2
    rounded = x.astype(BF16).astype(F32)
    bits = lax.bitcast_convert_type(rounded, jnp.uint32)
    word = (bits[:, :half] >> 16) | (bits[:, half:] & jnp.uint32(_HI16))
    return lax.bitcast_convert_type(word, jnp.int32)


def _unpack_rows(w):
    bits = lax.bitcast_convert_type(w, jnp.uint32)
    lo = lax.bitcast_convert_type(bits << 16, F32)
    hi = lax.bitcast_convert_type(bits & jnp.uint32(_HI16), F32)
    return lo, hi


def _ada_kernel(c_ref, w_ref, b_ref, o_ref):
    s = jax.nn.silu(c_ref[...])
    o_ref[...] = jnp.dot(s, w_ref[...], precision=lax.Precision.HIGHEST,
                         preferred_element_type=F32) + b_ref[...]


def _ada(c_all, w_ada, b_ada):
    rows, d = c_all.shape
    n = w_ada.shape[1]
    bn = 512
    return pl.pallas_call(
        _ada_kernel,
        out_shape=jax.ShapeDtypeStruct((rows, n), F32),
        grid=(n // bn,),
        in_specs=[pl.BlockSpec((rows, d), lambda j: (0, 0)),
                  pl.BlockSpec((d, bn), lambda j: (0, j)),
                  pl.BlockSpec((1, bn), lambda j: (0, j))],
        out_specs=pl.BlockSpec((rows, bn), lambda j: (0, j)),
        compiler_params=_cparams("arbitrary"),
        name="ada",
    )(c_all, w_ada, b_ada)


def _ctx_kv_kernel(x_ref, sh_ref, sc_ref, nmix_ref, w_ref, kg_ref, bd_ref, k_ref, v_ref):
    h = _rms(x_ref[...], nmix_ref[...]) * (1.0 + sc_ref[...]) + sh_ref[...]
    p = _dot(h.astype(BF16), w_ref[...])
    ms = _group_mean_sq(p, bd_ref)
    k = p[:, :LANES] * lax.rsqrt(ms[:, :LANES] + EPS) * kg_ref[...]
    v = p[:, LANES:]
    lane = lax.broadcasted_iota(jnp.int32, k.shape, 1)
    ka, kb = _dup_halves(k, lane)
    k_ref[...] = jnp.concatenate([ka, kb], axis=-1).astype(BF16)
    v_ref[...] = _with_ones(v, lane).astype(BF16)


def _ctx_kv(ctx, mod4, ctx_row, nmix, w_kv, kg, bd):
    b, n_ctx, d = ctx.shape
    kv_w = 2 * LANES
    return pl.pallas_call(
        _ctx_kv_kernel,
        out_shape=(jax.ShapeDtypeStruct((b, n_ctx, kv_w), BF16),) * 2,
        grid=(b,),
        in_specs=[pl.BlockSpec((None, n_ctx, d), lambda i: (i, 0, 0)),
                  pl.BlockSpec((None, None, 1, d), lambda i: (ctx_row, 0, 0, 0)),
                  pl.BlockSpec((None, None, 1, d), lambda i: (ctx_row, 1, 0, 0)),
                  pl.BlockSpec((1, d), lambda i: (0, 0)),
                  pl.BlockSpec(w_kv.shape, lambda i: (0, 0)),
                  pl.BlockSpec((1, LANES), lambda i: (0, 0)),
                  pl.BlockSpec(bd.shape, lambda i: (0, 0))],
        out_specs=(pl.BlockSpec((None, n_ctx, kv_w), lambda i: (i, 0, 0)),) * 2,
        compiler_params=_cparams("arbitrary"),
        name="ctx_kv",
    )(ctx, mod4, mod4, nmix, w_kv, kg, bd)


def _inproj_kernel(x_ref, sh_ref, sc_ref, nmix_ref, w_ref, cos_ref, sin_ref, qg_ref, kg_ref, gmg_ref,
                   bd_ref, ws_ref, bs_ref, og_ref, q_ref, k_ref, v_ref, gm_ref, *, att_w, gm_w):
    tm = x_ref.shape[0]
    h = _rms(x_ref[...], nmix_ref[...]) * (1.0 + sc_ref[...]) + sh_ref[...]
    p = _dot(h.astype(BF16), w_ref[...])

    cos1, sin1 = cos_ref[...], sin_ref[...]
    lane1 = lax.broadcasted_iota(jnp.int32, (tm, LANES), 1)

    def rope(xn, cosw, sinw, lanew):
        w = xn.shape[1]
        fwd = pltpu.roll(xn, w - 32, axis=1)
        bwd = pltpu.roll(xn, 32, axis=1)
        swapped = jnp.where((lanew & 63) < 32, fwd, bwd)
        return xn * cosw + swapped * sinw

    q = p[:, :att_w]
    qn = q * lax.rsqrt(_group_mean_sq(q, bd_ref) + EPS) * qg_ref[...]
    reps = att_w // LANES
    cosq = jnp.concatenate([cos1] * reps, axis=-1)
    sinq = jnp.concatenate([sin1] * reps, axis=-1)
    laneq = lax.broadcasted_iota(jnp.int32, (tm, att_w), 1)
    q_ref[...] = rope(qn, cosq, sinq, laneq).astype(BF16)

    kv = p[:, att_w:att_w + 2 * LANES]
    ms = _group_mean_sq(kv, bd_ref)
    kn = kv[:, :LANES] * lax.rsqrt(ms[:, :LANES] + EPS) * kg_ref[...]
    k = rope(kn, cos1, sin1, lane1)
    v = kv[:, LANES:]
    ka, kb = _dup_halves(k, lane1)
    k_ref[...] = jnp.concatenate([ka, kb], axis=-1).astype(BF16)
    v_ref[...] = _with_ones(v, lane1).astype(BF16)

    u0 = att_w + 2 * LANES
    u = jax.nn.gelu(p[:, u0:u0 + gm_w])
    gt = jax.nn.gelu(p[:, u0 + gm_w:u0 + 2 * gm_w])
    gt = (gt * lax.rsqrt(_group_mean_sq(gt, bd_ref) + EPS) * gmg_ref[...]).astype(BF16)
    lo_half = lax.broadcasted_iota(jnp.int32, (CHUNK, LANES), 1) < LANES // 2
    rows = []
    for c in range(tm // CHUNK):
        blocks = []
        for j in range(gm_w // LANES):
            g = gt[c * CHUNK:(c + 1) * CHUNK, j * LANES:(j + 1) * LANES]
            r = _dot(ws_ref[j], g)
            blocks.append(jnp.where(lo_half, r[:CHUNK], r[CHUNK:]))
        rows.append(jnp.concatenate(blocks, axis=-1) + bs_ref[...])
    mixed = jnp.concatenate(rows, axis=0)
    gm_ref[...] = _rms(u * mixed, og_ref[...]).astype(BF16)


def _inproj(x, mod4, nmix, w_in, cos_t, sin_t, qg, kg, gmg, bd, ws, bs, og, *, att_w, gm_w, tm):
    b, length, d = x.shape
    kv_w = 2 * LANES
    n_in = w_in.shape[1]
    const = lambda shape: pl.BlockSpec(shape, lambda i, j: (0,) * len(shape))
    return pl.pallas_call(
        functools.partial(_inproj_kernel, att_w=att_w, gm_w=gm_w),
        out_shape=(jax.ShapeDtypeStruct((b, length, att_w), BF16),
                   jax.ShapeDtypeStruct((b, length, kv_w), BF16),
                   jax.ShapeDtypeStruct((b, length, kv_w), BF16),
                   jax.ShapeDtypeStruct((b, length, gm_w), BF16)),
        grid=(b, length // tm),
        in_specs=[pl.BlockSpec((None, tm, d), lambda i, j: (i, j, 0)),
                  pl.BlockSpec((None, None, 1, d), lambda i, j: (i, 0, 0, 0)),
                  pl.BlockSpec((None, None, 1, d), lambda i, j: (i, 1, 0, 0)),
                  const((1, d)),
                  const((d, n_in)),
                  pl.BlockSpec((tm, LANES), lambda i, j: (j, 0)),
                  pl.BlockSpec((tm, LANES), lambda i, j: (j, 0)),
                  const((1, att_w)), const((1, LANES)), const((1, gm_w)),
                  const(bd.shape), const(ws.shape), const(bs.shape), const((1, gm_w))],
        out_specs=(pl.BlockSpec((None, tm, att_w), lambda i, j: (i, j, 0)),
                   pl.BlockSpec((None, tm, kv_w), lambda i, j: (i, j, 0)),
                   pl.BlockSpec((None, tm, kv_w), lambda i, j: (i, j, 0)),
                   pl.BlockSpec((None, tm, gm_w), lambda i, j: (i, j, 0))),
        compiler_params=_cparams("arbitrary", "arbitrary"),
        name="inproj",
    )(x, mod4, mod4, nmix, w_in, cos_t, sin_t, qg, kg, gmg, bd, ws, bs, og)


def _attn_kernel(q_ref, kc_ref, kl_ref, vc_ref, vl_ref, o_ref, s_buf, p_buf, *, tq):
    n_ctx = kc_ref.shape[0]
    heads = q_ref.shape[1] // (LANES // 2)
    nj = q_ref.shape[0] // tq
    half = LANES // 2
    lo = lax.broadcasted_iota(jnp.int32, (tq, LANES), 1) < half

    s_buf[...] = jnp.zeros_like(s_buf)
    p_buf[...] = jnp.ones_like(p_buf)

    def scores(j, g):
        q = q_ref[pl.ds(j * tq, tq), (g // 2) * LANES:(g // 2 + 1) * LANES]
        zero = jnp.zeros_like(q)
        qh = jnp.where(lo, q, zero) if g % 2 == 0 else jnp.where(lo, zero, q)
        s_buf[g % 2, :, :n_ctx] = _dot_nt(qh, kc_ref[...])
        s_buf[g % 2, :, n_ctx:] = _dot_nt(qh, kl_ref[...])

    def probs(g):
        s = s_buf[g % 2]
        p_buf[g % 2] = jnp.exp2(s - jnp.max(s, axis=-1, keepdims=True)).astype(BF16)

    def output(j, g):
        p = p_buf[g % 2]
        w = _dot(p[:, :n_ctx], vc_ref[...]) + _dot(p[:, n_ctx:], vl_ref[...])
        t = jnp.where(pl.program_id(1) == 0, w[:, :LANES], w[:, LANES:])
        r = pltpu.roll(t, half, axis=1)
        rows = pl.ds(j * tq, tq)
        if g % 2 == 0:
            o_ref[rows, g * half:(g + 1) * half] = (t / r)[:, :half].astype(o_ref.dtype)
        else:
            o_ref[rows, g * half:(g + 1) * half] = (r / t)[:, half:].astype(o_ref.dtype)

    def trip(j, carry):
        j_prev = jnp.maximum(j - 1, 0)
        for g in range(heads):
            scores(j, g)
            probs((g - 1) % heads)
            output(j if g >= 2 else j_prev, (g - 2) % heads)
        return carry

    lax.fori_loop(0, nj, trip, 0)
    probs(heads - 1)
    output(nj - 1, heads - 2)
    output(nj - 1, heads - 1)


def _unread_operand(in_specs, operands, array):
    in_specs.append(pl.BlockSpec(memory_space=pl.ANY))
    operands.append(array)
    return len(operands) - 1


def _attention(q, k_c, k_l, v_c, v_l, att_so_far, after, *, tq, b0, nb, slot0, slots):
    _, length, att_w = q.shape
    n_ctx = k_c.shape[1]
    group_w = att_w // ATT_KV_HEADS
    assert (group_w // (LANES // 2)) % 2 == 0 and length % tq == 0
    k_blk = lambda n: pl.BlockSpec((None, n, LANES), lambda i, h: (i + b0, 0, h))
    v_blk = lambda n: pl.BlockSpec((None, n, ATT_KV_HEADS * LANES), lambda i, h: (i + b0, 0, 0))
    in_specs = [pl.BlockSpec((None, length, group_w), lambda i, h: (i + b0, 0, h)),
                k_blk(n_ctx), k_blk(length), v_blk(n_ctx), v_blk(length)]
    operands = [q, k_c, k_l, v_c, v_l]
    n_read = len(operands)
    aliases = {}
    if att_so_far is not None:
        aliases = {_unread_operand(in_specs, operands, att_so_far): 0}
    if after is not None:
        _unread_operand(in_specs, operands, after)
    n_in = len(operands)
    return pl.pallas_call(
        lambda *refs: _attn_kernel(*refs[:n_read], *refs[n_in:], tq=tq),
        out_shape=jax.ShapeDtypeStruct((slots, length, att_w), BF16),
        grid=(nb, ATT_KV_HEADS),
        in_specs=in_specs,
        out_specs=pl.BlockSpec((None, length, group_w), lambda i, h: (i + slot0, 0, h)),
        scratch_shapes=[pltpu.VMEM((2, tq, n_ctx + length), F32), pltpu.VMEM((2, tq, n_ctx + length), BF16)],
        input_output_aliases=aliases,
        compiler_params=_cparams("arbitrary", "arbitrary"),
        name="attn",
    )(*operands)


def _post_kernel(att_ref, gm_ref, x_ref, g1_ref, sh2_ref, sc2_ref, g2_ref, oa_ref, wo_ref, nffn_ref,
                 wrt_ref, rb_ref, wsg_ref, wsd_ref,
                 base_ref, h2p_ref, idx_ref, gate_ref, rank_ref, cnt_ref, carry_ref):
    i = pl.program_id(0)
    tm = x_ref.shape[0]
    n_exp = wrt_ref.shape[0]

    @pl.when(i == 0)
    def _():
        carry_ref[...] = jnp.zeros_like(carry_ref)

    att_n = _rms(att_ref[...].astype(F32), oa_ref[...]).astype(BF16)
    y = _dot(jnp.concatenate([att_n, gm_ref[...]], axis=-1), wo_ref[...])
    x_new = x_ref[...] + g1_ref[...] * y
    h2 = _rms(x_new, nffn_ref[...]) * (1.0 + sc2_ref[...]) + sh2_ref[...]
    h2p_ref[...] = _pack_rows(h2)

    h_hi = h2.astype(BF16)
    scores_t = jax.nn.sigmoid(_dot_nt(wrt_ref[...], h_hi))
    reps = tm // LANES
    sel = scores_t + jnp.concatenate([rb_ref[...]] * reps, axis=-1)
    expert = lax.broadcasted_iota(jnp.int32, scores_t.shape, 0).astype(F32)
    neg = jnp.float32(-jnp.inf)

    onehot = jnp.zeros(scores_t.shape, F32)
    hits, idxs, svals = [], [], []
    for _k in range(TOP_K):
        m = jnp.max(sel, axis=0, keepdims=True)
        idx = jnp.min(jnp.where(sel == m, expert, float(n_exp)), axis=0, keepdims=True)
        hit = expert == idx
        hits.append(hit)
        idxs.append(idx)
        svals.append(jnp.sum(jnp.where(hit, scores_t, 0.0), axis=0, keepdims=True))
        onehot = onehot + jnp.where(hit, 1.0, 0.0)
        sel = jnp.where(hit, neg, sel)
    ssum = functools.reduce(lambda a, b: a + b, svals)

    r_io = lax.broadcasted_iota(jnp.int32, (tm, tm), 0)
    c_io = lax.broadcasted_iota(jnp.int32, (tm, tm), 1)
    tri = jnp.where(r_io < c_io, 1.0, 0.0).astype(BF16)
    carry = carry_ref[...]
    before = _dot(onehot.astype(BF16), tri) + jnp.concatenate([carry] * reps, axis=-1)
    carry = carry + jnp.sum(onehot, axis=1, keepdims=True)
    carry_ref[...] = carry
    cnt_ref[...] = carry

    row = lax.broadcasted_iota(jnp.int32, (idx_ref.shape[0], tm), 0)
    idx_o = jnp.zeros(row.shape, F32)
    rank_o = jnp.zeros(row.shape, F32)
    gate_o = jnp.zeros(row.shape, F32)
    for k in range(TOP_K):
        rank = jnp.sum(jnp.where(hits[k], before, 0.0), axis=0, keepdims=True)
        idx_o = jnp.where(row == k, idxs[k], idx_o)
        rank_o = jnp.where(row == k, rank, rank_o)
        gate_o = jnp.where(row == k, svals[k] / ssum * ROUTED_SCALE, gate_o)
    idx_ref[...] = idx_o.astype(jnp.int32)
    rank_ref[...] = rank_o.astype(jnp.int32)
    gate_pad = jnp.concatenate([gate_o, jnp.zeros((LANES - gate_o.shape[0], tm), F32)], axis=0)
    gate_ref[...] = gate_pad.T

    gu = _dot(h_hi, wsg_ref[...])
    hs = gu.shape[1] // 2
    act = (jax.nn.silu(gu[:, :hs]) * gu[:, hs:]).astype(BF16)
    base_ref[...] = x_new + g2_ref[...] * _dot(act, wsd_ref[...])


def _post(att, gmn, x2, mod4, oa, w_out, nffn, wr_t, rb, w_sg, w_sd, *, tm, tokens_per_batch, tok0):
    t, att_w = att.shape
    d = x2.shape[1]
    n_exp = wr_t.shape[0]
    gm_w = gmn.shape[1]
    tpb = tokens_per_batch // tm
    blk0 = tok0 // tm
    const = lambda shape: pl.BlockSpec(shape, lambda i: (0,) * len(shape))
    mrow = lambda j: pl.BlockSpec((None, None, 1, d), lambda i: ((i + blk0) // tpb, j, 0, 0))
    by_choice = jax.ShapeDtypeStruct((CHOICE_ROWS, t), jnp.int32)
    return pl.pallas_call(
        _post_kernel,
        out_shape=(jax.ShapeDtypeStruct((t, d), F32), jax.ShapeDtypeStruct((t, d // 2), jnp.int32),
                   by_choice, jax.ShapeDtypeStruct((t, LANES), F32), by_choice,
                   jax.ShapeDtypeStruct((n_exp, LANES), F32)),
        grid=(t // tm,),
        in_specs=[pl.BlockSpec((tm, att_w), lambda i: (i, 0)),
                  pl.BlockSpec((tm, gm_w), lambda i: (i + blk0, 0)),
                  pl.BlockSpec((tm, d), lambda i: (i + blk0, 0)),
                  mrow(2), mrow(3), mrow(4), mrow(5),
                  const((1, att_w)), const(w_out.shape), const((1, d)),
                  const(wr_t.shape), const(rb.shape),
                  const(w_sg.shape), const(w_sd.shape)],
        out_specs=(pl.BlockSpec((tm, d), lambda i: (i, 0)), pl.BlockSpec((tm, d // 2), lambda i: (i, 0)),
                   pl.BlockSpec((CHOICE_ROWS, tm), lambda i: (0, i)), pl.BlockSpec((tm, LANES), lambda i: (i, 0)),
                   pl.BlockSpec((CHOICE_ROWS, tm), lambda i: (0, i)), pl.BlockSpec((n_exp, LANES), lambda i: (0, 0))),
        scratch_shapes=[pltpu.VMEM((n_exp, LANES), F32)],
        compiler_params=_cparams("arbitrary"),
        name="post",
    )(att, gmn, x2, mod4, mod4, mod4, mod4, oa, w_out, nffn, wr_t, rb, w_sg, w_sd)


def _sc_mesh_info():
    info = plsc.get_sparse_core_info()
    mesh = plsc.VectorSubcoreMesh(core_axis_name="c", subcore_axis_name="s")
    return mesh, info.num_cores, info.num_subcores, info.num_lanes


def _sc_params():
    cp = pltpu.CompilerParams()
    if "needs_layout_passes" in pltpu.CompilerParams.__dataclass_fields__:
        cp = dataclasses.replace(cp, needs_layout_passes=False)
    return cp


def _dispatch(h2p, idx_t, rank_t, offs, *, n_slots):
    t, dw = h2p.shape
    mesh, n_cores, n_sub, n_lanes = _sc_mesh_info()
    per_worker = t // (n_cores * n_sub)
    w = SC_WINDOW
    assert per_worker % w == 0 and w % n_lanes == 0

    @functools.partial(
        pl.kernel, mesh=mesh,
        out_type=(jax.ShapeDtypeStruct((n_slots, dw), jnp.int32), jax.ShapeDtypeStruct((CHOICE_ROWS, t), jnp.int32)),
        scratch_types=[pltpu.VMEM(offs.shape, jnp.int32)] + [pltpu.VMEM((TOP_K, w), jnp.int32)] * 3
        + [pltpu.VMEM((w, dw), jnp.int32), pltpu.SemaphoreType.DMA],
        compiler_params=_sc_params(),
    )
    def run(h_hbm, idx_hbm, rank_hbm, offs_hbm, xs_hbm, pos_hbm, offs_v, idx_v, rank_v, pos_v, rows_v, sem):
        base = (lax.axis_index("s") * n_cores + lax.axis_index("c")) * per_worker
        pltpu.sync_copy(offs_hbm, offs_v)

        @pl.loop(0, per_worker // w)
        def _(i):
            t0 = base + i * w
            pltpu.sync_copy(h_hbm.at[pl.ds(t0, w)], rows_v)
            for k in range(TOP_K):
                pltpu.sync_copy(idx_hbm.at[k, pl.ds(t0, w)], idx_v.at[k])
                pltpu.sync_copy(rank_hbm.at[k, pl.ds(t0, w)], rank_v.at[k])
            for k in range(TOP_K):
                for j in range(0, w, n_lanes):
                    group_start = plsc.load_gather(offs_v, [idx_v[k, pl.ds(j, n_lanes)]])
                    pos_v[k, pl.ds(j, n_lanes)] = group_start + rank_v[k, pl.ds(j, n_lanes)]
            copies = [pltpu.async_copy(rows_v, xs_hbm.at[pos_v.at[k]], sem) for k in range(TOP_K)]
            for k in range(TOP_K):
                pltpu.sync_copy(pos_v.at[k], pos_hbm.at[k, pl.ds(t0, w)])
            for cp in copies:
                cp.wait()

    return run(h2p, idx_t, rank_t, offs)


def _collect(ys, pos_t):
    t = pos_t.shape[1]
    dw = ys.shape[1]
    mesh, n_cores, n_sub, _ = _sc_mesh_info()
    per_worker = t // (n_cores * n_sub)
    w = SC_WINDOW
    assert per_worker % w == 0

    @functools.partial(
        pl.kernel, mesh=mesh,
        out_type=jax.ShapeDtypeStruct((TOP_K * t, dw), jnp.int32),
        scratch_types=[pltpu.VMEM((TOP_K, w), jnp.int32), pltpu.VMEM((w, dw), jnp.int32), pltpu.SemaphoreType.DMA],
        compiler_params=_sc_params(),
    )
    def run(ys_hbm, pos_hbm, r_hbm, pos_v, rows_v, sem):
        base = (lax.axis_index("s") * n_cores + lax.axis_index("c")) * per_worker

        @pl.loop(0, per_worker // w)
        def _(i):
            t0 = base + i * w
            for k in range(TOP_K):
                pltpu.sync_copy(pos_hbm.at[k, pl.ds(t0, w)], pos_v.at[k])
            for k in range(TOP_K):
                pltpu.async_copy(ys_hbm.at[pos_v.at[k]], rows_v, sem).wait()
                pltpu.sync_copy(rows_v, r_hbm.at[pl.ds(k * t + t0, w)])

    return run(ys, pos_t)


def _experts_kernel(te_ref, nu_ref, xs_ref, wgu_ref, wdn_ref, ys_ref, wgu_b, wdn_b):
    i = pl.program_id(0)

    @pl.when(i < nu_ref[0])
    def _():
        @pl.when(jnp.logical_or(i == 0, te_ref[i] != te_ref[jnp.maximum(i - 1, 0)]))
        def _():
            wgu_b[...] = wgu_ref[...].astype(BF16)
            wdn_b[...] = wdn_ref[...].astype(BF16)

        half = xs_ref.shape[1]
        lo, hi = _unpack_rows(xs_ref[...])
        gu = _dot(lo.astype(BF16), wgu_b[:half, :]) + _dot(hi.astype(BF16), wgu_b[half:, :])
        hh = gu.shape[1] // 2
        act = (jax.nn.silu(gu[:, :hh]) * gu[:, hh:]).astype(BF16)
        ys_ref[...] = _pack_rows(_dot(act, wdn_b[...]))


def _experts(tile_expert, n_used, xs, w_gu, w_dn, after, *, tr):
    nt = tile_expert.shape[0]
    _, d, h2w = w_gu.shape
    hh = w_dn.shape[1]
    wmap = lambda i, te, nu: (te[jnp.minimum(i, nu[0] - 1)], 0, 0)
    in_specs = [pl.BlockSpec((tr, d // 2), lambda i, te, nu: (jnp.minimum(i, nu[0] - 1), 0)),
                pl.BlockSpec((None, d, h2w), wmap),
                pl.BlockSpec((None, hh, d), wmap)]
    operands = [xs, w_gu, w_dn]
    n_read = 2 + len(operands)
    if after is not None:
        _unread_operand(in_specs, operands, after)
    n_in = 2 + len(operands)
    return pl.pallas_call(
        lambda *refs: _experts_kernel(*refs[:n_read], *refs[n_in:]),
        out_shape=jax.ShapeDtypeStruct(xs.shape, jnp.int32),
        grid_spec=pltpu.PrefetchScalarGridSpec(
            num_scalar_prefetch=2,
            grid=(nt,),
            in_specs=in_specs,
            out_specs=pl.BlockSpec((tr, d // 2), lambda i, te, nu: (jnp.where(i < nu[0], i, nt - 1), 0)),
            scratch_shapes=[pltpu.VMEM((d, h2w), BF16), pltpu.VMEM((hh, d), BF16)]),
        compiler_params=_cparams("arbitrary"),
        name="experts",
    )(tile_expert, n_used, *operands)


def _combine_kernel(base_ref, g2_ref, gate_ref, *refs):
    r_refs, o_ref = refs[:TOP_K], refs[TOP_K]
    gate = gate_ref[...]
    half = base_ref.shape[1] // 2
    acc_lo = jnp.zeros((base_ref.shape[0], half), F32)
    acc_hi = jnp.zeros((base_ref.shape[0], half), F32)
    for k in range(TOP_K):
        lo, hi = _unpack_rows(r_refs[k][...])
        g = gate[:, k:k + 1]
        acc_lo = acc_lo + g * lo
        acc_hi = acc_hi + g * hi
    g2 = g2_ref[...]
    o_ref[:, :half] = base_ref[:, :half] + g2[:, :half] * acc_lo
    o_ref[:, half:] = base_ref[:, half:] + g2[:, half:] * acc_hi


def _combine(base, mod4, gate, r, out_so_far, after, *, tm, tokens_per_batch, tok0, total_tokens):
    t, d = base.shape
    tpb = tokens_per_batch // tm
    nblk = t // tm
    blk0 = tok0 // tm
    r_spec = lambda k: pl.BlockSpec((tm, d // 2), lambda i: (k * nblk + i, 0))
    in_specs = [pl.BlockSpec((tm, d), lambda i: (i, 0)),
                pl.BlockSpec((None, None, 1, d), lambda i: ((i + blk0) // tpb, 5, 0, 0)),
                pl.BlockSpec((tm, LANES), lambda i: (i, 0))] + [r_spec(k) for k in range(TOP_K)]
    operands = [base, mod4, gate] + [r] * TOP_K
    n_read = len(operands)
    aliases = {}
    if out_so_far is not None:
        aliases = {_unread_operand(in_specs, operands, out_so_far): 0}
    if after is not None:
        _unread_operand(in_specs, operands, after)
    n_in = len(operands)
    return pl.pallas_call(
        lambda *refs: _combine_kernel(*refs[:n_read], *refs[n_in:]),
        out_shape=jax.ShapeDtypeStruct((total_tokens, d), F32),
        grid=(nblk,),
        in_specs=in_specs,
        out_specs=pl.BlockSpec((tm, d), lambda i: (i + blk0, 0)),
        input_output_aliases=aliases,
        compiler_params=_cparams("arbitrary"),
        name="combine",
    )(*operands)


def _rope_tables(length, head_dim):
    rows = length // GRID_W
    pairs_axis = head_dim // 4
    r = jnp.repeat(jnp.arange(rows, dtype=F32), GRID_W)
    col = jnp.tile(jnp.arange(GRID_W, dtype=F32), rows)
    inv = ROPE_THETA ** (-jnp.arange(pairs_axis, dtype=F32) / pairs_axis)
    ang = jnp.concatenate([r[:, None] * inv, col[:, None] * inv], axis=-1)
    cos, sin = jnp.cos(ang), jnp.sin(ang)
    cos_h = jnp.concatenate([cos, cos], axis=-1)
    sin_h = jnp.concatenate([-sin, sin], axis=-1)
    reps = LANES // head_dim
    return jnp.tile(cos_h, (1, reps)), jnp.tile(sin_h, (1, reps))


def kernel(x, c, ctx, c_ctx, w_ada, b_ada, norm_mix, w_in, q_norm, k_norm, gm_norm, w_spatial, b_spatial,
           out_norm_attn, out_norm_gm, w_out, norm_ffn, w_router, router_bias, w_exp_gu, w_exp_down,
           w_sh_gu, w_sh_down):
    assert w_ada.shape[0] == 1, "single-layer kernel"
    b, length, d = x.shape
    head_dim = q_norm.shape[-1]
    att_w = out_norm_attn.shape[-1]
    gm_w = out_norm_gm.shape[-1]
    gm_groups = w_spatial.shape[1]
    gm_hd = gm_w // gm_groups
    n_exp = w_router.shape[-1]
    t = b * length
    assert head_dim == LANES // 2 and gm_hd == LANES // 2 and w_spatial.shape[-1] == CHUNK
    assert n_exp <= LANES and att_w % MXU_EDGE == 0 and gm_w % MXU_EDGE == 0

    mod_rows = 16
    c_all = jnp.zeros((mod_rows, d), F32).at[:b].set(c).at[b].set(c_ctx)
    mod = _ada(c_all, w_ada[0], b_ada)
    mod4 = mod.reshape(mod_rows, N_MOD, 1, d)

    w_in_b = w_in[0].astype(BF16)
    bd = (jnp.arange(MXU_EDGE)[:, None] // head_dim == jnp.arange(MXU_EDGE)[None, :] // head_dim)
    bd = (bd.astype(F32) / head_dim).astype(BF16)
    qg = jnp.tile(q_norm[0] * (head_dim ** -0.5 * LOG2_E), att_w // head_dim)[None, :]
    kg = jnp.tile(k_norm[0], LANES // head_dim)[None, :]
    gmg = jnp.tile(gm_norm[0], gm_groups)[None, :]
    ws = w_spatial[0].astype(BF16).reshape(gm_groups // 2, 2 * CHUNK, CHUNK)
    bs = jnp.repeat(b_spatial[0].T, gm_hd, axis=1)
    cos_t, sin_t = _rope_tables(length, head_dim)

    k_c, v_c = _ctx_kv(ctx, mod4, b, norm_mix, w_in_b[:, att_w:att_w + 2 * LANES], kg, bd)
    q, k_l, v_l, gmn = _inproj(x, mod4, norm_mix, w_in_b, cos_t, sin_t, qg, kg, gmg, bd, ws, bs,
                               out_norm_gm, att_w=att_w, gm_w=gm_w, tm=min(512, length))
    wr_t = w_router[0].T.astype(BF16)
    rb = jnp.tile(router_bias[0][:, None], (1, LANES))
    w_out_b, w_sg_b, w_sd_b = w_out[0].astype(BF16), w_sh_gu[0].astype(BF16), w_sh_down[0].astype(BF16)
    gmn2, x2 = gmn.reshape(t, gm_w), x.reshape(t, d)
    tm_post = min(512, length)
    tr = 512

    n_parts = MOE_PARTS if b % MOE_PARTS == 0 else 1
    nb = b // n_parts
    tp = nb * length
    nt = tp * TOP_K // tr + n_exp

    state = [dict() for _ in range(n_parts)]
    halves = (nb // 2, nb - nb // 2) if nb >= 2 else (nb,)

    last_tc = [None]

    def attend(part, half):
        st = state[part]
        slot0 = sum(halves[:half])
        after = None if last_tc[0] is st.get("att") else last_tc[0]
        st["att"] = _attention(q, k_c, k_l, v_c, v_l, st.get("att"), after, tq=min(256, length),
                               b0=part * nb + slot0, nb=halves[half], slot0=slot0, slots=nb)
        last_tc[0] = st["att"]

    def route(part):
        st = state[part]
        base, h2p, idx_t, gate, rank_t, cnt = _post(
            st["att"].reshape(tp, att_w), gmn2, x2, mod4, out_norm_attn, w_out_b, norm_ffn, wr_t, rb, w_sg_b, w_sd_b,
            tm=tm_post, tokens_per_batch=length, tok0=part * tp)
        counts = cnt[:, 0].astype(jnp.int32)
        padded = (counts + tr - 1) // tr * tr
        ends = jnp.cumsum(padded)
        offs = ends - padded
        tile_start = jnp.arange(nt, dtype=jnp.int32) * tr
        tile_expert = jnp.minimum(jnp.sum(tile_start[:, None] >= ends[None, :], axis=1), n_exp - 1).astype(jnp.int32)
        n_used = (ends[-1:] // tr).astype(jnp.int32)
        offs_pad = jnp.zeros((LANES,), jnp.int32).at[:n_exp].set(offs)
        xs, pos_t = _dispatch(h2p, idx_t, rank_t, offs_pad, n_slots=nt * tr)
        st.update(base=base, gate=gate, xs=xs, pos_t=pos_t, tile_expert=tile_expert, n_used=n_used)
        last_tc[0] = base

    def run_experts(part):
        st = state[part]
        ys = _experts(st["tile_expert"], st["n_used"], st["xs"], w_exp_gu[0], w_exp_down[0], last_tc[0], tr=tr)
        st["r"] = _collect(ys, st["pos_t"])
        last_tc[0] = ys

    result = [None]

    def finish(part):
        st = state[part]
        after = None if last_tc[0] is result[0] or last_tc[0] is st["base"] else last_tc[0]
        result[0] = _combine(st["base"], mod4, st["gate"], st["r"], result[0], after, tm=min(256, length),
                             tokens_per_batch=length, tok0=part * tp, total_tokens=t)
        last_tc[0] = result[0]

    stages = [(0, lambda p: attend(p, 0))]
    if len(halves) == 2:
        stages.append((1, lambda p: attend(p, 1)))
    stages += [(2, route), (4, run_experts), (7, finish)]
    plan = sorted((when + 3 * part, part, n, fn) for part in range(n_parts) for n, (when, fn) in enumerate(stages))
    for _, part, _, fn in plan:
        fn(part)
    return result[0].reshape(b, length, d)
```

```python
import dataclasses
import functools

import jax
import jax.numpy as jnp
from jax import lax
from jax.experimental import pallas as pl
from jax.experimental.pallas import tpu as pltpu
from jax.experimental.pallas import tpu_sc as plsc

F32 = jnp.float32
BF16 = jnp.bfloat16

EPS = 1e-6
GRID_W = 64
ROPE_THETA = 10000.0
ATT_KV_HEADS = 2
TOP_K = 6
ROUTED_SCALE = 2.5
N_MOD = 6
CHUNK = 128
LOG2_E = 1.4426950408889634

LANES = 128
MXU_EDGE = 256
VMEM_LIMIT_BYTES = 56 * 1024 * 1024
SUBLANES = 8
CHOICE_ROWS = SUBLANES
SC_WINDOW = 64
MOE_PARTS = 2


def _cparams(*sem):
    return pltpu.CompilerParams(dimension_semantics=sem, vmem_limit_bytes=VMEM_LIMIT_BYTES)


def _dot(a, b):
    return jnp.dot(a, b, preferred_element_type=F32)


def _dot_nt(a, b):
    return lax.dot_general(a, b, (((1,), (1,)), ((), ())), preferred_element_type=F32)


def _rms(x, g):
    return x * lax.rsqrt(jnp.mean(x * x, axis=-1, keepdims=True) + EPS) * g


def _group_mean_sq(x, bd_ref):
    x2 = (x * x).astype(BF16)
    parts = [_dot(x2[:, i:i + MXU_EDGE], bd_ref[...]) for i in range(0, x.shape[1], MXU_EDGE)]
    return parts[0] if len(parts) == 1 else jnp.concatenate(parts, axis=-1)


def _dup_halves(x, lane):
    xr = pltpu.roll(x, LANES // 2, axis=1)
    lo = lane < LANES // 2
    return jnp.where(lo, x, xr), jnp.where(lo, xr, x)


def _with_ones(x, lane):
    xr = pltpu.roll(x, LANES // 2, axis=1)
    lo = lane < LANES // 2
    return jnp.concatenate([jnp.where(lo, x, 1.0), jnp.where(lo, xr, 1.0)], axis=-1)


_HI16 = 0xFFFF0000


def _pack_rows(x):
    half = x.shape[1] // 2
    rounded = x.astype(BF16).astype(F32)
    bits = lax.bitcast_convert_type(rounded, jnp.uint32)
    word = (bits[:, :half] >> 16) | (bits[:, half:] & jnp.uint32(_HI16))
    return lax.bitcast_convert_type(word, jnp.int32)


def _unpack_rows(w):
    bits = lax.bitcast_convert_type(w, jnp.uint32)
    lo = lax.bitcast_convert_type(bits << 16, F32)
    hi = lax.bitcast_convert_type(bits & jnp.uint32(_HI16), F32)
    return lo, hi


def _ada_kernel(c_ref, w_ref, b_ref, o_ref):
    s = jax.nn.silu(c_ref[...])
    o_ref[...] = jnp.dot(s, w_ref[...], precision=lax.Precision.HIGHEST,
                         preferred_element_type=F32) + b_ref[...]


def _ada(c_all, w_ada, b_ada):
    rows, d = c_all.shape
    n = w_ada.shape[1]
    bn = 512
    return pl.pallas_call(
        _ada_kernel,
        out_shape=jax.ShapeDtypeStruct((rows, n), F32),
        grid=(n // bn,),
        in_specs=[pl.BlockSpec((rows, d), lambda j: (0, 0)),
                  pl.BlockSpec((d, bn), lambda j: (0, j)),
                  pl.BlockSpec((1, bn), lambda j: (0, j))],
        out_specs=pl.BlockSpec((rows, bn), lambda j: (0, j)),
        compiler_params=_cparams("arbitrary"),
        name="ada",
    )(c_all, w_ada, b_ada)


def _ctx_kv_kernel(x_ref, sh_ref, sc_ref, nmix_ref, w_ref, kg_ref, bd_ref, k_ref, v_ref):
    h = _rms(x_ref[...], nmix_ref[...]) * (1.0 + sc_ref[...]) + sh_ref[...]
    p = _dot(h.astype(BF16), w_ref[...])
    ms = _group_mean_sq(p, bd_ref)
    k = p[:, :LANES] * lax.rsqrt(ms[:, :LANES] + EPS) * kg_ref[...]
    v = p[:, LANES:]
    lane = lax.broadcasted_iota(jnp.int32, k.shape, 1)
    ka, kb = _dup_halves(k, lane)
    k_ref[...] = jnp.concatenate([ka, kb], axis=-1).astype(BF16)
    v_ref[...] = _with_ones(v, lane).astype(BF16)


def _ctx_kv(ctx, mod4, ctx_row, nmix, w_kv, kg, bd):
    b, n_ctx, d = ctx.shape
    kv_w = 2 * LANES
    return pl.pallas_call(
        _ctx_kv_kernel,
        out_shape=(jax.ShapeDtypeStruct((b, n_ctx, kv_w), BF16),) * 2,
        grid=(b,),
        in_specs=[pl.BlockSpec((None, n_ctx, d), lambda i: (i, 0, 0)),
                  pl.BlockSpec((None, None, 1, d), lambda i: (ctx_row, 0, 0, 0)),
                  pl.BlockSpec((None, None, 1, d), lambda i: (ctx_row, 1, 0, 0)),
                  pl.BlockSpec((1, d), lambda i: (0, 0)),
                  pl.BlockSpec(w_kv.shape, lambda i: (0, 0)),
                  pl.BlockSpec((1, LANES), lambda i: (0, 0)),
                  pl.BlockSpec(bd.shape, lambda i: (0, 0))],
        out_specs=(pl.BlockSpec((None, n_ctx, kv_w), lambda i: (i, 0, 0)),) * 2,
        compiler_params=_cparams("arbitrary"),
        name="ctx_kv",
    )(ctx, mod4, mod4, nmix, w_kv, kg, bd)


def _inproj_kernel(x_ref, sh_ref, sc_ref, nmix_ref, w_ref, cos_ref, sin_ref, qg_ref, kg_ref, gmg_ref,
                   bd_ref, ws_ref, bs_ref, og_ref, q_ref, k_ref, v_ref, gm_ref, *, att_w, gm_w):
    tm = x_ref.shape[0]
    h = _rms(x_ref[...], nmix_ref[...]) * (1.0 + sc_ref[...]) + sh_ref[...]
    p = _dot(h.astype(BF16), w_ref[...])

    cos1, sin1 = cos_ref[...], sin_ref[...]
    lane1 = lax.broadcasted_iota(jnp.int32, (tm, LANES), 1)

    def rope(xn, cosw, sinw, lanew):
        w = xn.shape[1]
        fwd = pltpu.roll(xn, w - 32, axis=1)
        bwd = pltpu.roll(xn, 32, axis=1)
        swapped = jnp.where((lanew & 63) < 32, fwd, bwd)
        return xn * cosw + swapped * sinw

    q = p[:, :att_w]
    qn = q * lax.rsqrt(_group_mean_sq(q, bd_ref) + EPS) * qg_ref[...]
    reps = att_w // LANES
    cosq = jnp.concatenate([cos1] * reps, axis=-1)
    sinq = jnp.concatenate([sin1] * reps, axis=-1)
    laneq = lax.broadcasted_iota(jnp.int32, (tm, att_w), 1)
    q_ref[...] = rope(qn, cosq, sinq, laneq).astype(BF16)

    kv = p[:, att_w:att_w + 2 * LANES]
    ms = _group_mean_sq(kv, bd_ref)
    kn = kv[:, :LANES] * lax.rsqrt(ms[:, :LANES] + EPS) * kg_ref[...]
    k = rope(kn, cos1, sin1, lane1)
    v = kv[:, LANES:]
    ka, kb = _dup_halves(k, lane1)
    k_ref[...] = jnp.concatenate([ka, kb], axis=-1).astype(BF16)
    v_ref[...] = _with_ones(v, lane1).astype(BF16)

    u0 = att_w + 2 * LANES
    u = jax.nn.gelu(p[:, u0:u0 + gm_w])
    gt = jax.nn.gelu(p[:, u0 + gm_w:u0 + 2 * gm_w])
    gt = (gt * lax.rsqrt(_group_mean_sq(gt, bd_ref) + EPS) * gmg_ref[...]).astype(BF16)
    lo_half = lax.broadcasted_iota(jnp.int32, (CHUNK, LANES), 1) < LANES // 2
    rows = []
    for c in range(tm // CHUNK):
        blocks = []
        for j in range(gm_w // LANES):
            g = gt[c * CHUNK:(c + 1) * CHUNK, j * LANES:(j + 1) * LANES]
            r = _dot(ws_ref[j], g)
            blocks.append(jnp.where(lo_half, r[:CHUNK], r[CHUNK:]))
        rows.append(jnp.concatenate(blocks, axis=-1) + bs_ref[...])
    mixed = jnp.concatenate(rows, axis=0)
    gm_ref[...] = _rms(u * mixed, og_ref[...]).astype(BF16)


def _inproj(x, mod4, nmix, w_in, cos_t, sin_t, qg, kg, gmg, bd, ws, bs, og, *, att_w, gm_w, tm):
    b, length, d = x.shape
    kv_w = 2 * LANES
    n_in = w_in.shape[1]
    const = lambda shape: pl.BlockSpec(shape, lambda i, j: (0,) * len(shape))
    return pl.pallas_call(
        functools.partial(_inproj_kernel, att_w=att_w, gm_w=gm_w),
        out_shape=(jax.ShapeDtypeStruct((b, length, att_w), BF16),
                   jax.ShapeDtypeStruct((b, length, kv_w), BF16),
                   jax.ShapeDtypeStruct((b, length, kv_w), BF16),
                   jax.ShapeDtypeStruct((b, length, gm_w), BF16)),
        grid=(b, length // tm),
        in_specs=[pl.BlockSpec((None, tm, d), lambda i, j: (i, j, 0)),
                  pl.BlockSpec((None, None, 1, d), lambda i, j: (i, 0, 0, 0)),
                  pl.BlockSpec((None, None, 1, d), lambda i, j: (i, 1, 0, 0)),
                  const((1, d)),
                  const((d, n_in)),
                  pl.BlockSpec((tm, LANES), lambda i, j: (j, 0)),
                  pl.BlockSpec((tm, LANES), lambda i, j: (j, 0)),
                  const((1, att_w)), const((1, LANES)), const((1, gm_w)),
                  const(bd.shape), const(ws.shape), const(bs.shape), const((1, gm_w))],
        out_specs=(pl.BlockSpec((None, tm, att_w), lambda i, j: (i, j, 0)),
                   pl.BlockSpec((None, tm, kv_w), lambda i, j: (i, j, 0)),
                   pl.BlockSpec((None, tm, kv_w), lambda i, j: (i, j, 0)),
                   pl.BlockSpec((None, tm, gm_w), lambda i, j: (i, j, 0))),
        compiler_params=_cparams("arbitrary", "arbitrary"),
        name="inproj",
    )(x, mod4, mod4, nmix, w_in, cos_t, sin_t, qg, kg, gmg, bd, ws, bs, og)


def _attn_kernel(q_ref, kc_ref, kl_ref, vc_ref, vl_ref, o_ref, s_buf, p_buf, *, tq):
    n_ctx = kc_ref.shape[0]
    heads = q_ref.shape[1] // (LANES // 2)
    nj = q_ref.shape[0] // tq
    half = LANES // 2
    lo = lax.broadcasted_iota(jnp.int32, (tq, LANES), 1) < half

    s_buf[...] = jnp.zeros_like(s_buf)
    p_buf[...] = jnp.ones_like(p_buf)

    def scores(j, g):
        q = q_ref[pl.ds(j * tq, tq), (g // 2) * LANES:(g // 2 + 1) * LANES]
        zero = jnp.zeros_like(q)
        qh = jnp.where(lo, q, zero) if g % 2 == 0 else jnp.where(lo, zero, q)
        s_buf[g % 2, :, :n_ctx] = _dot_nt(qh, kc_ref[...])
        s_buf[g % 2, :, n_ctx:] = _dot_nt(qh, kl_ref[...])

    def probs(g):
        s = s_buf[g % 2]
        p_buf[g % 2] = jnp.exp2(s - jnp.max(s, axis=-1, keepdims=True)).astype(BF16)

    def output(j, g):
        p = p_buf[g % 2]
        w = _dot(p[:, :n_ctx], vc_ref[...]) + _dot(p[:, n_ctx:], vl_ref[...])
        t = jnp.where(pl.program_id(1) == 0, w[:, :LANES], w[:, LANES:])
        r = pltpu.roll(t, half, axis=1)
        rows = pl.ds(j * tq, tq)
        if g % 2 == 0:
            o_ref[rows, g * half:(g + 1) * half] = (t / r)[:, :half].astype(o_ref.dtype)
        else:
            o_ref[rows, g * half:(g + 1) * half] = (r / t)[:, half:].astype(o_ref.dtype)

    def trip(j, carry):
        j_prev = jnp.maximum(j - 1, 0)
        for g in range(heads):
            scores(j, g)
            probs((g - 1) % heads)
            output(j if g >= 2 else j_prev, (g - 2) % heads)
        return carry

    lax.fori_loop(0, nj, trip, 0)
    probs(heads - 1)
    output(nj - 1, heads - 2)
    output(nj - 1, heads - 1)


def _unread_operand(in_specs, operands, array):
    in_specs.append(pl.BlockSpec(memory_space=pl.ANY))
    operands.append(array)
    return len(operands) - 1


def _attention(q, k_c, k_l, v_c, v_l, att_so_far, after, *, tq, b0, nb, slot0, slots):
    _, length, att_w = q.shape
    n_ctx = k_c.shape[1]
    group_w = att_w // ATT_KV_HEADS
    assert (group_w // (LANES // 2)) % 2 == 0 and length % tq == 0
    k_blk = lambda n: pl.BlockSpec((None, n, LANES), lambda i, h: (i + b0, 0, h))
    v_blk = lambda n: pl.BlockSpec((None, n, ATT_KV_HEADS * LANES), lambda i, h: (i + b0, 0, 0))
    in_specs = [pl.BlockSpec((None, length, group_w), lambda i, h: (i + b0, 0, h)),
                k_blk(n_ctx), k_blk(length), v_blk(n_ctx), v_blk(length)]
    operands = [q, k_c, k_l, v_c, v_l]
    n_read = len(operands)
    aliases = {}
    if att_so_far is not None:
        aliases = {_unread_operand(in_specs, operands, att_so_far): 0}
    if after is not None:
        _unread_operand(in_specs, operands, after)
    n_in = len(operands)
    return pl.pallas_call(
        lambda *refs: _attn_kernel(*refs[:n_read], *refs[n_in:], tq=tq),
        out_shape=jax.ShapeDtypeStruct((slots, length, att_w), BF16),
        grid=(nb, ATT_KV_HEADS),
        in_specs=in_specs,
        out_specs=pl.BlockSpec((None, length, group_w), lambda i, h: (i + slot0, 0, h)),
        scratch_shapes=[pltpu.VMEM((2, tq, n_ctx + length), F32), pltpu.VMEM((2, tq, n_ctx + length), BF16)],
        input_output_aliases=aliases,
        compiler_params=_cparams("arbitrary", "arbitrary"),
        name="attn",
    )(*operands)


def _post_kernel(att_ref, gm_ref, x_ref, g1_ref, sh2_ref, sc2_ref, g2_ref, oa_ref, wo_ref, nffn_ref,
                 wrt_ref, rb_ref, wsg_ref, wsd_ref,
                 base_ref, h2p_ref, idx_ref, gate_ref, rank_ref, cnt_ref, carry_ref):
    i = pl.program_id(0)
    tm = x_ref.shape[0]
    n_exp = wrt_ref.shape[0]

    @pl.when(i == 0)
    def _():
        carry_ref[...] = jnp.zeros_like(carry_ref)

    att_n = _rms(att_ref[...].astype(F32), oa_ref[...]).astype(BF16)
    y = _dot(jnp.concatenate([att_n, gm_ref[...]], axis=-1), wo_ref[...])
    x_new = x_ref[...] + g1_ref[...] * y
    h2 = _rms(x_new, nffn_ref[...]) * (1.0 + sc2_ref[...]) + sh2_ref[...]
    h2p_ref[...] = _pack_rows(h2)

    h_hi = h2.astype(BF16)
    scores_t = jax.nn.sigmoid(_dot_nt(wrt_ref[...], h_hi))
    reps = tm // LANES
    sel = scores_t + jnp.concatenate([rb_ref[...]] * reps, axis=-1)
    expert = lax.broadcasted_iota(jnp.int32, scores_t.shape, 0).astype(F32)
    neg = jnp.float32(-jnp.inf)

    onehot = jnp.zeros(scores_t.shape, F32)
    hits, idxs, svals = [], [], []
    for _k in range(TOP_K):
        m = jnp.max(sel, axis=0, keepdims=True)
        idx = jnp.min(jnp.where(sel == m, expert, float(n_exp)), axis=0, keepdims=True)
        hit = expert == idx
        hits.append(hit)
        idxs.append(idx)
        svals.append(jnp.sum(jnp.where(hit, scores_t, 0.0), axis=0, keepdims=True))
        onehot = onehot + jnp.where(hit, 1.0, 0.0)
        sel = jnp.where(hit, neg, sel)
    ssum = functools.reduce(lambda a, b: a + b, svals)

    r_io = lax.broadcasted_iota(jnp.int32, (tm, tm), 0)
    c_io = lax.broadcasted_iota(jnp.int32, (tm, tm), 1)
    tri = jnp.where(r_io < c_io, 1.0, 0.0).astype(BF16)
    carry = carry_ref[...]
    before = _dot(onehot.astype(BF16), tri) + jnp.concatenate([carry] * reps, axis=-1)
    carry = carry + jnp.sum(onehot, axis=1, keepdims=True)
    carry_ref[...] = carry
    cnt_ref[...] = carry

    row = lax.broadcasted_iota(jnp.int32, (idx_ref.shape[0], tm), 0)
    idx_o = jnp.zeros(row.shape, F32)
    rank_o = jnp.zeros(row.shape, F32)
    gate_o = jnp.zeros(row.shape, F32)
    for k in range(TOP_K):
        rank = jnp.sum(jnp.where(hits[k], before, 0.0), axis=0, keepdims=True)
        idx_o = jnp.where(row == k, idxs[k], idx_o)
        rank_o = jnp.where(row == k, rank, rank_o)
        gate_o = jnp.where(row == k, svals[k] / ssum * ROUTED_SCALE, gate_o)
    idx_ref[...] = idx_o.astype(jnp.int32)
    rank_ref[...] = rank_o.astype(jnp.int32)
    gate_pad = jnp.concatenate([gate_o, jnp.zeros((LANES - gate_o.shape[0], tm), F32)], axis=0)
    gate_ref[...] = gate_pad.T

    gu = _dot(h_hi, wsg_ref[...])
    hs = gu.shape[1] // 2
    act = (jax.nn.silu(gu[:, :hs]) * gu[:, hs:]).astype(BF16)
    base_ref[...] = x_new + g2_ref[...] * _dot(act, wsd_ref[...])


def _post(att, gmn, x2, mod4, oa, w_out, nffn, wr_t, rb, w_sg, w_sd, *, tm, tokens_per_batch, tok0):
    t, att_w = att.shape
    d = x2.shape[1]
    n_exp = wr_t.shape[0]
    gm_w = gmn.shape[1]
    tpb = tokens_per_batch // tm
    blk0 = tok0 // tm
    const = lambda shape: pl.BlockSpec(shape, lambda i: (0,) * len(shape))
    mrow = lambda j: pl.BlockSpec((None, None, 1, d), lambda i: ((i + blk0) // tpb, j, 0, 0))
    by_choice = jax.ShapeDtypeStruct((CHOICE_ROWS, t), jnp.int32)
    return pl.pallas_call(
        _post_kernel,
        out_shape=(jax.ShapeDtypeStruct((t, d), F32), jax.ShapeDtypeStruct((t, d // 2), jnp.int32),
                   by_choice, jax.ShapeDtypeStruct((t, LANES), F32), by_choice,
                   jax.ShapeDtypeStruct((n_exp, LANES), F32)),
        grid=(t // tm,),
        in_specs=[pl.BlockSpec((tm, att_w), lambda i: (i, 0)),
                  pl.BlockSpec((tm, gm_w), lambda i: (i + blk0, 0)),
                  pl.BlockSpec((tm, d), lambda i: (i + blk0, 0)),
                  mrow(2), mrow(3), mrow(4), mrow(5),
                  const((1, att_w)), const(w_out.shape), const((1, d)),
                  const(wr_t.shape), const(rb.shape),
                  const(w_sg.shape), const(w_sd.shape)],
        out_specs=(pl.BlockSpec((tm, d), lambda i: (i, 0)), pl.BlockSpec((tm, d // 2), lambda i: (i, 0)),
                   pl.BlockSpec((CHOICE_ROWS, tm), lambda i: (0, i)), pl.BlockSpec((tm, LANES), lambda i: (i, 0)),
                   pl.BlockSpec((CHOICE_ROWS, tm), lambda i: (0, i)), pl.BlockSpec((n_exp, LANES), lambda i: (0, 0))),
        scratch_shapes=[pltpu.VMEM((n_exp, LANES), F32)],
        compiler_params=_cparams("arbitrary"),
        name="post",
    )(att, gmn, x2, mod4, mod4, mod4, mod4, oa, w_out, nffn, wr_t, rb, w_sg, w_sd)


def _sc_mesh_info():
    info = plsc.get_sparse_core_info()
    mesh = plsc.VectorSubcoreMesh(core_axis_name="c", subcore_axis_name="s")
    return mesh, info.num_cores, info.num_subcores, info.num_lanes


def _sc_params():
    cp = pltpu.CompilerParams()
    if "needs_layout_passes" in pltpu.CompilerParams.__dataclass_fields__:
        cp = dataclasses.replace(cp, needs_layout_passes=False)
    return cp


def _dispatch(h2p, idx_t, rank_t, offs, *, n_slots):
    t, dw = h2p.shape
    mesh, n_cores, n_sub, n_lanes = _sc_mesh_info()
    per_worker = t // (n_cores * n_sub)
    w = SC_WINDOW
    assert per_worker % w == 0 and w % n_lanes == 0

    @functools.partial(
        pl.kernel, mesh=mesh,
        out_type=(jax.ShapeDtypeStruct((n_slots, dw), jnp.int32), jax.ShapeDtypeStruct((CHOICE_ROWS, t), jnp.int32)),
        scratch_types=[pltpu.VMEM(offs.shape, jnp.int32)] + [pltpu.VMEM((TOP_K, w), jnp.int32)] * 3
        + [pltpu.VMEM((w, dw), jnp.int32), pltpu.SemaphoreType.DMA],
        compiler_params=_sc_params(),
    )
    def run(h_hbm, idx_hbm, rank_hbm, offs_hbm, xs_hbm, pos_hbm, offs_v, idx_v, rank_v, pos_v, rows_v, sem):
        base = (lax.axis_index("s") * n_cores + lax.axis_index("c")) * per_worker
        pltpu.sync_copy(offs_hbm, offs_v)

        @pl.loop(0, per_worker // w)
        def _(i):
            t0 = base + i * w
            pltpu.sync_copy(h_hbm.at[pl.ds(t0, w)], rows_v)
            for k in range(TOP_K):
                pltpu.sync_copy(idx_hbm.at[k, pl.ds(t0, w)], idx_v.at[k])
                pltpu.sync_copy(rank_hbm.at[k, pl.ds(t0, w)], rank_v.at[k])
            for k in range(TOP_K):
                for j in range(0, w, n_lanes):
                    group_start = plsc.load_gather(offs_v, [idx_v[k, pl.ds(j, n_lanes)]])
                    pos_v[k, pl.ds(j, n_lanes)] = group_start + rank_v[k, pl.ds(j, n_lanes)]
            copies = [pltpu.async_copy(rows_v, xs_hbm.at[pos_v.at[k]], sem) for k in range(TOP_K)]
            for k in range(TOP_K):
                pltpu.sync_copy(pos_v.at[k], pos_hbm.at[k, pl.ds(t0, w)])
            for cp in copies:
                cp.wait()

    return run(h2p, idx_t, rank_t, offs)


def _collect(ys, pos_t):
    t = pos_t.shape[1]
    dw = ys.shape[1]
    mesh, n_cores, n_sub, _ = _sc_mesh_info()
    per_worker = t // (n_cores * n_sub)
    w = SC_WINDOW
    assert per_worker % w == 0

    @functools.partial(
        pl.kernel, mesh=mesh,
        out_type=jax.ShapeDtypeStruct((TOP_K * t, dw), jnp.int32),
        scratch_types=[pltpu.VMEM((TOP_K, w), jnp.int32), pltpu.VMEM((w, dw), jnp.int32), pltpu.SemaphoreType.DMA],
        compiler_params=_sc_params(),
    )
    def run(ys_hbm, pos_hbm, r_hbm, pos_v, rows_v, sem):
        base = (lax.axis_index("s") * n_cores + lax.axis_index("c")) * per_worker

        @pl.loop(0, per_worker // w)
        def _(i):
            t0 = base + i * w
            for k in range(TOP_K):
                pltpu.sync_copy(pos_hbm.at[k, pl.ds(t0, w)], pos_v.at[k])
            for k in range(TOP_K):
                pltpu.async_copy(ys_hbm.at[pos_v.at[k]], rows_v, sem).wait()
                pltpu.sync_copy(rows_v, r_hbm.at[pl.ds(k * t + t0, w)])

    return run(ys, pos_t)


def _experts_kernel(t0_ref, t1_ref, xs_hbm, wgu_ref, wdn_ref, ys_hbm, xbuf, ybuf, wgu_b, wdn_b, sem_in, sem_out):
    e = pl.program_id(0)
    n_exp = pl.num_programs(0)
    tr = xbuf.shape[1]
    n_tiles = t1_ref[n_exp - 1]

    def fetch(t):
        slot = t & 1
        return pltpu.make_async_copy(xs_hbm.at[pl.ds(t * tr, tr)], xbuf.at[slot], sem_in.at[slot])

    def flush(t):
        slot = t & 1
        return pltpu.make_async_copy(ybuf.at[slot], ys_hbm.at[pl.ds(t * tr, tr)], sem_out.at[slot])

    @pl.when(jnp.logical_and(e == 0, n_tiles > 0))
    def _():
        fetch(0).start()

    wgu_b[...] = wgu_ref[...].astype(BF16)
    wdn_b[...] = wdn_ref[...].astype(BF16)
    half = xbuf.shape[2]

    def tile(t, carry):
        slot = t & 1
        fetch(t).wait()

        @pl.when(t + 1 < n_tiles)
        def _():
            fetch(t + 1).start()

        lo, hi = _unpack_rows(xbuf[slot])
        gu = _dot(lo.astype(BF16), wgu_b[:half, :]) + _dot(hi.astype(BF16), wgu_b[half:, :])
        hh = gu.shape[1] // 2
        act = (jax.nn.silu(gu[:, :hh]) * gu[:, hh:]).astype(BF16)
        y = _pack_rows(_dot(act, wdn_b[...]))

        @pl.when(t >= 2)
        def _():
            flush(t - 2).wait()

        ybuf[slot] = y
        flush(t).start()
        return carry

    lax.fori_loop(t0_ref[e], t1_ref[e], tile, 0)

    @pl.when(e == n_exp - 1)
    def _():
        @pl.when(n_tiles >= 2)
        def _():
            flush(n_tiles - 2).wait()

        @pl.when(n_tiles >= 1)
        def _():
            flush(n_tiles - 1).wait()


def _experts(tile_lo, tile_hi, xs, w_gu, w_dn, after, *, tr):
    n_exp, d, h2w = w_gu.shape
    hh = w_dn.shape[1]
    in_specs = [pl.BlockSpec(memory_space=pl.ANY),
                pl.BlockSpec((None, d, h2w), lambda e, lo, hi: (e, 0, 0)),
                pl.BlockSpec((None, hh, d), lambda e, lo, hi: (e, 0, 0))]
    operands = [xs, w_gu, w_dn]
    n_read = 2 + len(operands)
    if after is not None:
        _unread_operand(in_specs, operands, after)
    n_in = 2 + len(operands)
    return pl.pallas_call(
        lambda *refs: _experts_kernel(*refs[:n_read], *refs[n_in:]),
        out_shape=jax.ShapeDtypeStruct(xs.shape, jnp.int32),
        grid_spec=pltpu.PrefetchScalarGridSpec(
            num_scalar_prefetch=2,
            grid=(n_exp,),
            in_specs=in_specs,
            out_specs=pl.BlockSpec(memory_space=pl.ANY),
            scratch_shapes=[pltpu.VMEM((2, tr, d // 2), jnp.int32), pltpu.VMEM((2, tr, d // 2), jnp.int32),
                            pltpu.VMEM((d, h2w), BF16), pltpu.VMEM((hh, d), BF16),
                            pltpu.SemaphoreType.DMA((2,)), pltpu.SemaphoreType.DMA((2,))]),
        compiler_params=_cparams("arbitrary"),
        name="experts",
    )(tile_lo, tile_hi, *operands)


def _combine_kernel(base_ref, g2_ref, gate_ref, *refs):
    r_refs, o_ref = refs[:TOP_K], refs[TOP_K]
    gate = gate_ref[...]
    half = base_ref.shape[1] // 2
    acc_lo = jnp.zeros((base_ref.shape[0], half), F32)
    acc_hi = jnp.zeros((base_ref.shape[0], half), F32)
    for k in range(TOP_K):
        lo, hi = _unpack_rows(r_refs[k][...])
        g = gate[:, k:k + 1]
        acc_lo = acc_lo + g * lo
        acc_hi = acc_hi + g * hi
    g2 = g2_ref[...]
    o_ref[:, :half] = base_ref[:, :half] + g2[:, :half] * acc_lo
    o_ref[:, half:] = base_ref[:, half:] + g2[:, half:] * acc_hi


def _combine(base, mod4, gate, r, out_so_far, after, *, tm, tokens_per_batch, tok0, total_tokens):
    t, d = base.shape
    tpb = tokens_per_batch // tm
    nblk = t // tm
    blk0 = tok0 // tm
    r_spec = lambda k: pl.BlockSpec((tm, d // 2), lambda i: (k * nblk + i, 0))
    in_specs = [pl.BlockSpec((tm, d), lambda i: (i, 0)),
                pl.BlockSpec((None, None, 1, d), lambda i: ((i + blk0) // tpb, 5, 0, 0)),
                pl.BlockSpec((tm, LANES), lambda i: (i, 0))] + [r_spec(k) for k in range(TOP_K)]
    operands = [base, mod4, gate] + [r] * TOP_K
    n_read = len(operands)
    aliases = {}
    if out_so_far is not None:
        aliases = {_unread_operand(in_specs, operands, out_so_far): 0}
    if after is not None:
        _unread_operand(in_specs, operands, after)
    n_in = len(operands)
    return pl.pallas_call(
        lambda *refs: _combine_kernel(*refs[:n_read], *refs[n_in:]),
        out_shape=jax.ShapeDtypeStruct((total_tokens, d), F32),
        grid=(nblk,),
        in_specs=in_specs,
        out_specs=pl.BlockSpec((tm, d), lambda i: (i + blk0, 0)),
        input_output_aliases=aliases,
        compiler_params=_cparams("arbitrary"),
        name="combine",
    )(*operands)


def _rope_tables(length, head_dim):
    rows = length // GRID_W
    pairs_axis = head_dim // 4
    r = jnp.repeat(jnp.arange(rows, dtype=F32), GRID_W)
    col = jnp.tile(jnp.arange(GRID_W, dtype=F32), rows)
    inv = ROPE_THETA ** (-jnp.arange(pairs_axis, dtype=F32) / pairs_axis)
    ang = jnp.concatenate([r[:, None] * inv, col[:, None] * inv], axis=-1)
    cos, sin = jnp.cos(ang), jnp.sin(ang)
    cos_h = jnp.concatenate([cos, cos], axis=-1)
    sin_h = jnp.concatenate([-sin, sin], axis=-1)
    reps = LANES // head_dim
    return jnp.tile(cos_h, (1, reps)), jnp.tile(sin_h, (1, reps))


def kernel(x, c, ctx, c_ctx, w_ada, b_ada, norm_mix, w_in, q_norm, k_norm, gm_norm, w_spatial, b_spatial,
           out_norm_attn, out_norm_gm, w_out, norm_ffn, w_router, router_bias, w_exp_gu, w_exp_down,
           w_sh_gu, w_sh_down):
    assert w_ada.shape[0] == 1, "single-layer kernel"
    b, length, d = x.shape
    head_dim = q_norm.shape[-1]
    att_w = out_norm_attn.shape[-1]
    gm_w = out_norm_gm.shape[-1]
    gm_groups = w_spatial.shape[1]
    gm_hd = gm_w // gm_groups
    n_exp = w_router.shape[-1]
    t = b * length
    assert head_dim == LANES // 2 and gm_hd == LANES // 2 and w_spatial.shape[-1] == CHUNK
    assert n_exp <= LANES and att_w % MXU_EDGE == 0 and gm_w % MXU_EDGE == 0

    mod_rows = 16
    c_all = jnp.zeros((mod_rows, d), F32).at[:b].set(c).at[b].set(c_ctx)
    mod = _ada(c_all, w_ada[0], b_ada)
    mod4 = mod.reshape(mod_rows, N_MOD, 1, d)

    w_in_b = w_in[0].astype(BF16)
    bd = (jnp.arange(MXU_EDGE)[:, None] // head_dim == jnp.arange(MXU_EDGE)[None, :] // head_dim)
    bd = (bd.astype(F32) / head_dim).astype(BF16)
    qg = jnp.tile(q_norm[0] * (head_dim ** -0.5 * LOG2_E), att_w // head_dim)[None, :]
    kg = jnp.tile(k_norm[0], LANES // head_dim)[None, :]
    gmg = jnp.tile(gm_norm[0], gm_groups)[None, :]
    ws = w_spatial[0].astype(BF16).reshape(gm_groups // 2, 2 * CHUNK, CHUNK)
    bs = jnp.repeat(b_spatial[0].T, gm_hd, axis=1)
    cos_t, sin_t = _rope_tables(length, head_dim)

    k_c, v_c = _ctx_kv(ctx, mod4, b, norm_mix, w_in_b[:, att_w:att_w + 2 * LANES], kg, bd)
    q, k_l, v_l, gmn = _inproj(x, mod4, norm_mix, w_in_b, cos_t, sin_t, qg, kg, gmg, bd, ws, bs,
                               out_norm_gm, att_w=att_w, gm_w=gm_w, tm=min(512, length))
    wr_t = w_router[0].T.astype(BF16)
    rb = jnp.tile(router_bias[0][:, None], (1, LANES))
    w_out_b, w_sg_b, w_sd_b = w_out[0].astype(BF16), w_sh_gu[0].astype(BF16), w_sh_down[0].astype(BF16)
    gmn2, x2 = gmn.reshape(t, gm_w), x.reshape(t, d)
    tm_post = min(512, length)
    tr = 512

    n_parts = MOE_PARTS if b % MOE_PARTS == 0 else 1
    nb = b // n_parts
    tp = nb * length
    nt = tp * TOP_K // tr + n_exp

    state = [dict() for _ in range(n_parts)]
    halves = (nb // 2, nb - nb // 2) if nb >= 2 else (nb,)

    last_tc = [None]

    def attend(part, half):
        st = state[part]
        slot0 = sum(halves[:half])
        after = None if last_tc[0] is st.get("att") else last_tc[0]
        st["att"] = _attention(q, k_c, k_l, v_c, v_l, st.get("att"), after, tq=min(256, length),
                               b0=part * nb + slot0, nb=halves[half], slot0=slot0, slots=nb)
        last_tc[0] = st["att"]

    def route(part):
        st = state[part]
        base, h2p, idx_t, gate, rank_t, cnt = _post(
            st["att"].reshape(tp, att_w), gmn2, x2, mod4, out_norm_attn, w_out_b, norm_ffn, wr_t, rb, w_sg_b, w_sd_b,
            tm=tm_post, tokens_per_batch=length, tok0=part * tp)
        counts = cnt[:, 0].astype(jnp.int32)
        padded = (counts + tr - 1) // tr * tr
        ends = jnp.cumsum(padded)
        offs = ends - padded
        offs_pad = jnp.zeros((LANES,), jnp.int32).at[:n_exp].set(offs)
        xs, pos_t = _dispatch(h2p, idx_t, rank_t, offs_pad, n_slots=nt * tr)
        st.update(base=base, gate=gate, xs=xs, pos_t=pos_t, tile_lo=offs // tr, tile_hi=ends // tr)
        last_tc[0] = base

    def run_experts(part):
        st = state[part]
        ys = _experts(st["tile_lo"], st["tile_hi"], st["xs"], w_exp_gu[0], w_exp_down[0], last_tc[0], tr=tr)
        st["r"] = _collect(ys, st["pos_t"])
        last_tc[0] = ys

    result = [None]

    def finish(part):
        st = state[part]
        after = None if last_tc[0] is result[0] or last_tc[0] is st["base"] else last_tc[0]
        result[0] = _combine(st["base"], mod4, st["gate"], st["r"], result[0], after, tm=min(256, length),
                             tokens_per_batch=length, tok0=part * tp, total_tokens=t)
        last_tc[0] = result[0]

    stages = [(0, lambda p: attend(p, 0))]
    if len(halves) == 2:
        stages.append((1, lambda p: attend(p, 1)))
    stages += [(2, route), (4, run_experts), (8, finish)]
    plan = sorted((when + 3 * part, part, n, fn) for part in range(n_parts) for n, (when, fn) in enumerate(stages))
    for _, part, _, fn in plan:
        fn(part)
    return result[0].reshape(b, length, d)
```

```python
import dataclasses
import functools

import jax
import jax.numpy as jnp
from jax import lax
from jax.experimental import pallas as pl
from jax.experimental.pallas import tpu as pltpu
from jax.experimental.pallas import tpu_sc as plsc

F32 = jnp.float32
BF16 = jnp.bfloat16

EPS = 1e-6
GRID_W = 64
ROPE_THETA = 10000.0
ATT_KV_HEADS = 2
TOP_K = 6
ROUTED_SCALE = 2.5
N_MOD = 6
CHUNK = 128
LOG2_E = 1.4426950408889634

LANES = 128
MXU_EDGE = 256
VMEM_LIMIT_BYTES = 56 * 1024 * 1024
SUBLANES = 8
CHOICE_ROWS = SUBLANES
SC_WINDOW = 64
MOE_PARTS = 2
EXPERT_IN_SLOTS = 4
EXPERT_OUT_SLOTS = 2


def _cparams(*sem):
    return pltpu.CompilerParams(dimension_semantics=sem, vmem_limit_bytes=VMEM_LIMIT_BYTES)


def _dot(a, b):
    return jnp.dot(a, b, preferred_element_type=F32)


def _dot_nt(a, b):
    return lax.dot_general(a, b, (((1,), (1,)), ((), ())), preferred_element_type=F32)


def _rms(x, g):
    return x * lax.rsqrt(jnp.mean(x * x, axis=-1, keepdims=True) + EPS) * g


def _group_mean_sq(x, bd_ref):
    x2 = (x * x).astype(BF16)
    parts = [_dot(x2[:, i:i + MXU_EDGE], bd_ref[...]) for i in range(0, x.shape[1], MXU_EDGE)]
    return parts[0] if len(parts) == 1 else jnp.concatenate(parts, axis=-1)


def _dup_halves(x, lane):
    xr = pltpu.roll(x, LANES // 2, axis=1)
    lo = lane < LANES // 2
    return jnp.where(lo, x, xr), jnp.where(lo, xr, x)


def _with_ones(x, lane):
    xr = pltpu.roll(x, LANES // 2, axis=1)
    lo = lane < LANES // 2
    return jnp.concatenate([jnp.where(lo, x, 1.0), jnp.where(lo, xr, 1.0)], axis=-1)


_HI16 = 0xFFFF0000


def _pack_rows(x):
    half = x.shape[1] // 2
    rounded = x.astype(BF16).astype(F32)
    bits = lax.bitcast_convert_type(rounded, jnp.uint32)
    word = (bits[:, :half] >> 16) | (bits[:, half:] & jnp.uint32(_HI16))
    return lax.bitcast_convert_type(word, jnp.int32)


def _unpack_rows(w):
    bits = lax.bitcast_convert_type(w, jnp.uint32)
    lo = lax.bitcast_convert_type(bits << 16, F32)
    hi = lax.bitcast_convert_type(bits & jnp.uint32(_HI16), F32)
    return lo, hi


def _ada_kernel(c_ref, w_ref, b_ref, o_ref):
    s = jax.nn.silu(c_ref[...])
    o_ref[...] = jnp.dot(s, w_ref[...], precision=lax.Precision.HIGHEST,
                         preferred_element_type=F32) + b_ref[...]


def _ada(c_all, w_ada, b_ada):
    rows, d = c_all.shape
    n = w_ada.shape[1]
    bn = 512
    return pl.pallas_call(
        _ada_kernel,
        out_shape=jax.ShapeDtypeStruct((rows, n), F32),
        grid=(n // bn,),
        in_specs=[pl.BlockSpec((rows, d), lambda j: (0, 0)),
                  pl.BlockSpec((d, bn), lambda j: (0, j)),
                  pl.BlockSpec((1, bn), lambda j: (0, j))],
        out_specs=pl.BlockSpec((rows, bn), lambda j: (0, j)),
        compiler_params=_cparams("arbitrary"),
        name="ada",
    )(c_all, w_ada, b_ada)


def _ctx_kv_kernel(x_ref, sh_ref, sc_ref, nmix_ref, w_ref, kg_ref, bd_ref, k_ref, v_ref):
    h = _rms(x_ref[...], nmix_ref[...]) * (1.0 + sc_ref[...]) + sh_ref[...]
    p = _dot(h.astype(BF16), w_ref[...])
    ms = _group_mean_sq(p, bd_ref)
    k = p[:, :LANES] * lax.rsqrt(ms[:, :LANES] + EPS) * kg_ref[...]
    v = p[:, LANES:]
    lane = lax.broadcasted_iota(jnp.int32, k.shape, 1)
    ka, kb = _dup_halves(k, lane)
    k_ref[...] = jnp.concatenate([ka, kb], axis=-1).astype(BF16)
    v_ref[...] = _with_ones(v, lane).astype(BF16)


def _ctx_kv(ctx, mod4, ctx_row, nmix, w_kv, kg, bd):
    b, n_ctx, d = ctx.shape
    kv_w = 2 * LANES
    return pl.pallas_call(
        _ctx_kv_kernel,
        out_shape=(jax.ShapeDtypeStruct((b, n_ctx, kv_w), BF16),) * 2,
        grid=(b,),
        in_specs=[pl.BlockSpec((None, n_ctx, d), lambda i: (i, 0, 0)),
                  pl.BlockSpec((None, None, 1, d), lambda i: (ctx_row, 0, 0, 0)),
                  pl.BlockSpec((None, None, 1, d), lambda i: (ctx_row, 1, 0, 0)),
                  pl.BlockSpec((1, d), lambda i: (0, 0)),
                  pl.BlockSpec(w_kv.shape, lambda i: (0, 0)),
                  pl.BlockSpec((1, LANES), lambda i: (0, 0)),
                  pl.BlockSpec(bd.shape, lambda i: (0, 0))],
        out_specs=(pl.BlockSpec((None, n_ctx, kv_w), lambda i: (i, 0, 0)),) * 2,
        compiler_params=_cparams("arbitrary"),
        name="ctx_kv",
    )(ctx, mod4, mod4, nmix, w_kv, kg, bd)


def _inproj_kernel(x_ref, sh_ref, sc_ref, nmix_ref, w_ref, cos_ref, sin_ref, qg_ref, kg_ref, gmg_ref,
                   bd_ref, ws_ref, bs_ref, og_ref, q_ref, k_ref, v_ref, gm_ref, *, att_w, gm_w):
    tm = x_ref.shape[0]
    h = _rms(x_ref[...], nmix_ref[...]) * (1.0 + sc_ref[...]) + sh_ref[...]
    p = _dot(h.astype(BF16), w_ref[...])

    cos1, sin1 = cos_ref[...], sin_ref[...]
    lane1 = lax.broadcasted_iota(jnp.int32, (tm, LANES), 1)

    def rope(xn, cosw, sinw, lanew):
        w = xn.shape[1]
        fwd = pltpu.roll(xn, w - 32, axis=1)
        bwd = pltpu.roll(xn, 32, axis=1)
        swapped = jnp.where((lanew & 63) < 32, fwd, bwd)
        return xn * cosw + swapped * sinw

    q = p[:, :att_w]
    qn = q * lax.rsqrt(_group_mean_sq(q, bd_ref) + EPS) * qg_ref[...]
    reps = att_w // LANES
    cosq = jnp.concatenate([cos1] * reps, axis=-1)
    sinq = jnp.concatenate([sin1] * reps, axis=-1)
    laneq = lax.broadcasted_iota(jnp.int32, (tm, att_w), 1)
    q_ref[...] = rope(qn, cosq, sinq, laneq).astype(BF16)

    kv = p[:, att_w:att_w + 2 * LANES]
    ms = _group_mean_sq(kv, bd_ref)
    kn = kv[:, :LANES] * lax.rsqrt(ms[:, :LANES] + EPS) * kg_ref[...]
    k = rope(kn, cos1, sin1, lane1)
    v = kv[:, LANES:]
    ka, kb = _dup_halves(k, lane1)
    k_ref[...] = jnp.concatenate([ka, kb], axis=-1).astype(BF16)
    v_ref[...] = _with_ones(v, lane1).astype(BF16)

    u0 = att_w + 2 * LANES
    u = jax.nn.gelu(p[:, u0:u0 + gm_w])
    gt = jax.nn.gelu(p[:, u0 + gm_w:u0 + 2 * gm_w])
    gt = (gt * lax.rsqrt(_group_mean_sq(gt, bd_ref) + EPS) * gmg_ref[...]).astype(BF16)
    lo_half = lax.broadcasted_iota(jnp.int32, (CHUNK, LANES), 1) < LANES // 2
    rows = []
    for c in range(tm // CHUNK):
        blocks = []
        for j in range(gm_w // LANES):
            g = gt[c * CHUNK:(c + 1) * CHUNK, j * LANES:(j + 1) * LANES]
            r = _dot(ws_ref[j], g)
            blocks.append(jnp.where(lo_half, r[:CHUNK], r[CHUNK:]))
        rows.append(jnp.concatenate(blocks, axis=-1) + bs_ref[...])
    mixed = jnp.concatenate(rows, axis=0)
    gm_ref[...] = _rms(u * mixed, og_ref[...]).astype(BF16)


def _inproj(x, mod4, nmix, w_in, cos_t, sin_t, qg, kg, gmg, bd, ws, bs, og, *, att_w, gm_w, tm):
    b, length, d = x.shape
    kv_w = 2 * LANES
    n_in = w_in.shape[1]
    const = lambda shape: pl.BlockSpec(shape, lambda i, j: (0,) * len(shape))
    return pl.pallas_call(
        functools.partial(_inproj_kernel, att_w=att_w, gm_w=gm_w),
        out_shape=(jax.ShapeDtypeStruct((b, length, att_w), BF16),
                   jax.ShapeDtypeStruct((b, length, kv_w), BF16),
                   jax.ShapeDtypeStruct((b, length, kv_w), BF16),
                   jax.ShapeDtypeStruct((b, length, gm_w), BF16)),
        grid=(b, length // tm),
        in_specs=[pl.BlockSpec((None, tm, d), lambda i, j: (i, j, 0)),
                  pl.BlockSpec((None, None, 1, d), lambda i, j: (i, 0, 0, 0)),
                  pl.BlockSpec((None, None, 1, d), lambda i, j: (i, 1, 0, 0)),
                  const((1, d)),
                  const((d, n_in)),
                  pl.BlockSpec((tm, LANES), lambda i, j: (j, 0)),
                  pl.BlockSpec((tm, LANES), lambda i, j: (j, 0)),
                  const((1, att_w)), const((1, LANES)), const((1, gm_w)),
                  const(bd.shape), const(ws.shape), const(bs.shape), const((1, gm_w))],
        out_specs=(pl.BlockSpec((None, tm, att_w), lambda i, j: (i, j, 0)),
                   pl.BlockSpec((None, tm, kv_w), lambda i, j: (i, j, 0)),
                   pl.BlockSpec((None, tm, kv_w), lambda i, j: (i, j, 0)),
                   pl.BlockSpec((None, tm, gm_w), lambda i, j: (i, j, 0))),
        compiler_params=_cparams("arbitrary", "arbitrary"),
        name="inproj",
    )(x, mod4, mod4, nmix, w_in, cos_t, sin_t, qg, kg, gmg, bd, ws, bs, og)


def _attn_kernel(q_ref, kc_ref, kl_ref, vc_ref, vl_ref, o_ref, s_buf, p_buf, *, tq):
    n_ctx = kc_ref.shape[0]
    heads = q_ref.shape[1] // (LANES // 2)
    nj = q_ref.shape[0] // tq
    half = LANES // 2
    lo = lax.broadcasted_iota(jnp.int32, (tq, LANES), 1) < half

    s_buf[...] = jnp.zeros_like(s_buf)
    p_buf[...] = jnp.ones_like(p_buf)

    def scores(j, g):
        q = q_ref[pl.ds(j * tq, tq), (g // 2) * LANES:(g // 2 + 1) * LANES]
        zero = jnp.zeros_like(q)
        qh = jnp.where(lo, q, zero) if g % 2 == 0 else jnp.where(lo, zero, q)
        s_buf[g % 2, :, :n_ctx] = _dot_nt(qh, kc_ref[...])
        s_buf[g % 2, :, n_ctx:] = _dot_nt(qh, kl_ref[...])

    def probs(g):
        s = s_buf[g % 2]
        p_buf[g % 2] = jnp.exp2(s - jnp.max(s, axis=-1, keepdims=True)).astype(BF16)

    def output(j, g):
        p = p_buf[g % 2]
        w = _dot(p[:, :n_ctx], vc_ref[...]) + _dot(p[:, n_ctx:], vl_ref[...])
        t = jnp.where(pl.program_id(1) == 0, w[:, :LANES], w[:, LANES:])
        r = pltpu.roll(t, half, axis=1)
        rows = pl.ds(j * tq, tq)
        if g % 2 == 0:
            o_ref[rows, g * half:(g + 1) * half] = (t / r)[:, :half].astype(o_ref.dtype)
        else:
            o_ref[rows, g * half:(g + 1) * half] = (r / t)[:, half:].astype(o_ref.dtype)

    def trip(j, carry):
        j_prev = jnp.maximum(j - 1, 0)
        for g in range(heads):
            scores(j, g)
            probs((g - 1) % heads)
            output(j if g >= 2 else j_prev, (g - 2) % heads)
        return carry

    lax.fori_loop(0, nj, trip, 0)
    probs(heads - 1)
    output(nj - 1, heads - 2)
    output(nj - 1, heads - 1)


def _unread_operand(in_specs, operands, array):
    in_specs.append(pl.BlockSpec(memory_space=pl.ANY))
    operands.append(array)
    return len(operands) - 1


def _attention(q, k_c, k_l, v_c, v_l, att_so_far, after, *, tq, b0, nb, slot0, slots):
    _, length, att_w = q.shape
    n_ctx = k_c.shape[1]
    group_w = att_w // ATT_KV_HEADS
    assert (group_w // (LANES // 2)) % 2 == 0 and length % tq == 0
    k_blk = lambda n: pl.BlockSpec((None, n, LANES), lambda i, h: (i + b0, 0, h))
    v_blk = lambda n: pl.BlockSpec((None, n, ATT_KV_HEADS * LANES), lambda i, h: (i + b0, 0, 0))
    in_specs = [pl.BlockSpec((None, length, group_w), lambda i, h: (i + b0, 0, h)),
                k_blk(n_ctx), k_blk(length), v_blk(n_ctx), v_blk(length)]
    operands = [q, k_c, k_l, v_c, v_l]
    n_read = len(operands)
    aliases = {}
    if att_so_far is not None:
        aliases = {_unread_operand(in_specs, operands, att_so_far): 0}
    if after is not None:
        _unread_operand(in_specs, operands, after)
    n_in = len(operands)
    return pl.pallas_call(
        lambda *refs: _attn_kernel(*refs[:n_read], *refs[n_in:], tq=tq),
        out_shape=jax.ShapeDtypeStruct((slots, length, att_w), BF16),
        grid=(nb, ATT_KV_HEADS),
        in_specs=in_specs,
        out_specs=pl.BlockSpec((None, length, group_w), lambda i, h: (i + slot0, 0, h)),
        scratch_shapes=[pltpu.VMEM((2, tq, n_ctx + length), F32), pltpu.VMEM((2, tq, n_ctx + length), BF16)],
        input_output_aliases=aliases,
        compiler_params=_cparams("arbitrary", "arbitrary"),
        name="attn",
    )(*operands)


def _post_kernel(att_ref, gm_ref, x_ref, g1_ref, sh2_ref, sc2_ref, g2_ref, oa_ref, wo_ref, nffn_ref,
                 wrt_ref, rb_ref, wsg_ref, wsd_ref,
                 base_ref, h2p_ref, idx_ref, gate_ref, rank_ref, cnt_ref, carry_ref):
    i = pl.program_id(0)
    tm = x_ref.shape[0]
    n_exp = wrt_ref.shape[0]

    @pl.when(i == 0)
    def _():
        carry_ref[...] = jnp.zeros_like(carry_ref)

    att_n = _rms(att_ref[...].astype(F32), oa_ref[...]).astype(BF16)
    y = _dot(jnp.concatenate([att_n, gm_ref[...]], axis=-1), wo_ref[...])
    x_new = x_ref[...] + g1_ref[...] * y
    h2 = _rms(x_new, nffn_ref[...]) * (1.0 + sc2_ref[...]) + sh2_ref[...]
    h2p_ref[...] = _pack_rows(h2)

    h_hi = h2.astype(BF16)
    scores_t = jax.nn.sigmoid(_dot_nt(wrt_ref[...], h_hi))
    reps = tm // LANES
    sel = scores_t + jnp.concatenate([rb_ref[...]] * reps, axis=-1)
    expert = lax.broadcasted_iota(jnp.int32, scores_t.shape, 0).astype(F32)
    neg = jnp.float32(-jnp.inf)

    onehot = jnp.zeros(scores_t.shape, F32)
    hits, idxs, svals = [], [], []
    for _k in range(TOP_K):
        m = jnp.max(sel, axis=0, keepdims=True)
        idx = jnp.min(jnp.where(sel == m, expert, float(n_exp)), axis=0, keepdims=True)
        hit = expert == idx
        hits.append(hit)
        idxs.append(idx)
        svals.append(jnp.sum(jnp.where(hit, scores_t, 0.0), axis=0, keepdims=True))
        onehot = onehot + jnp.where(hit, 1.0, 0.0)
        sel = jnp.where(hit, neg, sel)
    ssum = functools.reduce(lambda a, b: a + b, svals)

    r_io = lax.broadcasted_iota(jnp.int32, (tm, tm), 0)
    c_io = lax.broadcasted_iota(jnp.int32, (tm, tm), 1)
    tri = jnp.where(r_io < c_io, 1.0, 0.0).astype(BF16)
    carry = carry_ref[...]
    before = _dot(onehot.astype(BF16), tri) + jnp.concatenate([carry] * reps, axis=-1)
    carry = carry + jnp.sum(onehot, axis=1, keepdims=True)
    carry_ref[...] = carry
    cnt_ref[...] = carry

    row = lax.broadcasted_iota(jnp.int32, (idx_ref.shape[0], tm), 0)
    idx_o = jnp.zeros(row.shape, F32)
    rank_o = jnp.zeros(row.shape, F32)
    gate_o = jnp.zeros(row.shape, F32)
    for k in range(TOP_K):
        rank = jnp.sum(jnp.where(hits[k], before, 0.0), axis=0, keepdims=True)
        idx_o = jnp.where(row == k, idxs[k], idx_o)
        rank_o = jnp.where(row == k, rank, rank_o)
        gate_o = jnp.where(row == k, svals[k] / ssum * ROUTED_SCALE, gate_o)
    idx_ref[...] = idx_o.astype(jnp.int32)
    rank_ref[...] = rank_o.astype(jnp.int32)
    gate_pad = jnp.concatenate([gate_o, jnp.zeros((LANES - gate_o.shape[0], tm), F32)], axis=0)
    gate_ref[...] = gate_pad.T

    gu = _dot(h_hi, wsg_ref[...])
    hs = gu.shape[1] // 2
    act = (jax.nn.silu(gu[:, :hs]) * gu[:, hs:]).astype(BF16)
    base_ref[...] = x_new + g2_ref[...] * _dot(act, wsd_ref[...])


def _post(att, gmn, x2, mod4, oa, w_out, nffn, wr_t, rb, w_sg, w_sd, *, tm, tokens_per_batch, tok0):
    t, att_w = att.shape
    d = x2.shape[1]
    n_exp = wr_t.shape[0]
    gm_w = gmn.shape[1]
    tpb = tokens_per_batch // tm
    blk0 = tok0 // tm
    const = lambda shape: pl.BlockSpec(shape, lambda i: (0,) * len(shape))
    mrow = lambda j: pl.BlockSpec((None, None, 1, d), lambda i: ((i + blk0) // tpb, j, 0, 0))
    by_choice = jax.ShapeDtypeStruct((CHOICE_ROWS, t), jnp.int32)
    return pl.pallas_call(
        _post_kernel,
        out_shape=(jax.ShapeDtypeStruct((t, d), F32), jax.ShapeDtypeStruct((t, d // 2), jnp.int32),
                   by_choice, jax.ShapeDtypeStruct((t, LANES), F32), by_choice,
                   jax.ShapeDtypeStruct((n_exp, LANES), F32)),
        grid=(t // tm,),
        in_specs=[pl.BlockSpec((tm, att_w), lambda i: (i, 0)),
                  pl.BlockSpec((tm, gm_w), lambda i: (i + blk0, 0)),
                  pl.BlockSpec((tm, d), lambda i: (i + blk0, 0)),
                  mrow(2), mrow(3), mrow(4), mrow(5),
                  const((1, att_w)), const(w_out.shape), const((1, d)),
                  const(wr_t.shape), const(rb.shape),
                  const(w_sg.shape), const(w_sd.shape)],
        out_specs=(pl.BlockSpec((tm, d), lambda i: (i, 0)), pl.BlockSpec((tm, d // 2), lambda i: (i, 0)),
                   pl.BlockSpec((CHOICE_ROWS, tm), lambda i: (0, i)), pl.BlockSpec((tm, LANES), lambda i: (i, 0)),
                   pl.BlockSpec((CHOICE_ROWS, tm), lambda i: (0, i)), pl.BlockSpec((n_exp, LANES), lambda i: (0, 0))),
        scratch_shapes=[pltpu.VMEM((n_exp, LANES), F32)],
        compiler_params=_cparams("arbitrary"),
        name="post",
    )(att, gmn, x2, mod4, mod4, mod4, mod4, oa, w_out, nffn, wr_t, rb, w_sg, w_sd)


def _sc_mesh_info():
    info = plsc.get_sparse_core_info()
    mesh = plsc.VectorSubcoreMesh(core_axis_name="c", subcore_axis_name="s")
    return mesh, info.num_cores, info.num_subcores, info.num_lanes


def _sc_params():
    cp = pltpu.CompilerParams()
    if "needs_layout_passes" in pltpu.CompilerParams.__dataclass_fields__:
        cp = dataclasses.replace(cp, needs_layout_passes=False)
    return cp


def _dispatch(h2p, idx_t, rank_t, offs, *, n_slots):
    t, dw = h2p.shape
    mesh, n_cores, n_sub, n_lanes = _sc_mesh_info()
    per_worker = t // (n_cores * n_sub)
    w = SC_WINDOW
    assert per_worker % w == 0 and w % n_lanes == 0

    @functools.partial(
        pl.kernel, mesh=mesh,
        out_type=(jax.ShapeDtypeStruct((n_slots, dw), jnp.int32), jax.ShapeDtypeStruct((CHOICE_ROWS, t), jnp.int32)),
        scratch_types=[pltpu.VMEM(offs.shape, jnp.int32)] + [pltpu.VMEM((TOP_K, w), jnp.int32)] * 3
        + [pltpu.VMEM((w, dw), jnp.int32), pltpu.SemaphoreType.DMA],
        compiler_params=_sc_params(),
    )
    def run(h_hbm, idx_hbm, rank_hbm, offs_hbm, xs_hbm, pos_hbm, offs_v, idx_v, rank_v, pos_v, rows_v, sem):
        base = (lax.axis_index("s") * n_cores + lax.axis_index("c")) * per_worker
        pltpu.sync_copy(offs_hbm, offs_v)

        @pl.loop(0, per_worker // w)
        def _(i):
            t0 = base + i * w
            pltpu.sync_copy(h_hbm.at[pl.ds(t0, w)], rows_v)
            for k in range(TOP_K):
                pltpu.sync_copy(idx_hbm.at[k, pl.ds(t0, w)], idx_v.at[k])
                pltpu.sync_copy(rank_hbm.at[k, pl.ds(t0, w)], rank_v.at[k])
            for k in range(TOP_K):
                for j in range(0, w, n_lanes):
                    group_start = plsc.load_gather(offs_v, [idx_v[k, pl.ds(j, n_lanes)]])
                    pos_v[k, pl.ds(j, n_lanes)] = group_start + rank_v[k, pl.ds(j, n_lanes)]
            copies = [pltpu.async_copy(rows_v, xs_hbm.at[pos_v.at[k]], sem) for k in range(TOP_K)]
            for k in range(TOP_K):
                pltpu.sync_copy(pos_v.at[k], pos_hbm.at[k, pl.ds(t0, w)])
            for cp in copies:
                cp.wait()

    return run(h2p, idx_t, rank_t, offs)


def _collect(ys, pos_t):
    t = pos_t.shape[1]
    dw = ys.shape[1]
    mesh, n_cores, n_sub, _ = _sc_mesh_info()
    per_worker = t // (n_cores * n_sub)
    w = SC_WINDOW
    assert per_worker % w == 0

    @functools.partial(
        pl.kernel, mesh=mesh,
        out_type=jax.ShapeDtypeStruct((TOP_K * t, dw), jnp.int32),
        scratch_types=[pltpu.VMEM((TOP_K, w), jnp.int32), pltpu.VMEM((w, dw), jnp.int32), pltpu.SemaphoreType.DMA],
        compiler_params=_sc_params(),
    )
    def run(ys_hbm, pos_hbm, r_hbm, pos_v, rows_v, sem):
        base = (lax.axis_index("s") * n_cores + lax.axis_index("c")) * per_worker

        @pl.loop(0, per_worker // w)
        def _(i):
            t0 = base + i * w
            for k in range(TOP_K):
                pltpu.sync_copy(pos_hbm.at[k, pl.ds(t0, w)], pos_v.at[k])
            for k in range(TOP_K):
                pltpu.async_copy(ys_hbm.at[pos_v.at[k]], rows_v, sem).wait()
                pltpu.sync_copy(rows_v, r_hbm.at[pl.ds(k * t + t0, w)])

    return run(ys, pos_t)


def _experts_kernel(t0_ref, t1_ref, xs_hbm, wgu_ref, wdn_ref, ys_hbm, xbuf, ybuf, wgu_b, wdn_b, sem_in, sem_out):
    e = pl.program_id(0)
    n_exp = pl.num_programs(0)
    n_in, tr, half = xbuf.shape
    n_out = ybuf.shape[0]
    n_tiles = t1_ref[n_exp - 1]

    def fetch(t):
        slot = t & (n_in - 1)
        return pltpu.make_async_copy(xs_hbm.at[pl.ds(t * tr, tr)], xbuf.at[slot], sem_in.at[slot])

    def flush(t):
        slot = t & (n_out - 1)
        return pltpu.make_async_copy(ybuf.at[slot], ys_hbm.at[pl.ds(t * tr, tr)], sem_out.at[slot])

    @pl.when(e == 0)
    def _():
        for t in range(n_in - 1):
            @pl.when(t < n_tiles)
            def _():
                fetch(t).start()

    wgu_b[...] = wgu_ref[...].astype(BF16)
    wdn_b[...] = wdn_ref[...].astype(BF16)

    def tile(t, carry):
        fetch(t).wait()

        @pl.when(t + n_in - 1 < n_tiles)
        def _():
            fetch(t + n_in - 1).start()

        lo, hi = _unpack_rows(xbuf[t & (n_in - 1)])
        gu = _dot(lo.astype(BF16), wgu_b[:half, :]) + _dot(hi.astype(BF16), wgu_b[half:, :])
        hh = gu.shape[1] // 2
        act = (jax.nn.silu(gu[:, :hh]) * gu[:, hh:]).astype(BF16)
        y = _pack_rows(_dot(act, wdn_b[...]))

        @pl.when(t >= n_out)
        def _():
            flush(t - n_out).wait()

        ybuf[t & (n_out - 1)] = y
        flush(t).start()
        return carry

    lax.fori_loop(t0_ref[e], t1_ref[e], tile, 0)

    @pl.when(e == n_exp - 1)
    def _():
        for back in range(n_out, 0, -1):
            @pl.when(n_tiles >= back)
            def _():
                flush(n_tiles - back).wait()


def _experts(tile_lo, tile_hi, xs, w_gu, w_dn, after, *, tr):
    n_exp, d, h2w = w_gu.shape
    hh = w_dn.shape[1]
    in_specs = [pl.BlockSpec(memory_space=pl.ANY),
                pl.BlockSpec((None, d, h2w), lambda e, lo, hi: (e, 0, 0)),
                pl.BlockSpec((None, hh, d), lambda e, lo, hi: (e, 0, 0))]
    operands = [xs, w_gu, w_dn]
    n_read = 2 + len(operands)
    if after is not None:
        _unread_operand(in_specs, operands, after)
    n_in = 2 + len(operands)
    return pl.pallas_call(
        lambda *refs: _experts_kernel(*refs[:n_read], *refs[n_in:]),
        out_shape=jax.ShapeDtypeStruct(xs.shape, jnp.int32),
        grid_spec=pltpu.PrefetchScalarGridSpec(
            num_scalar_prefetch=2,
            grid=(n_exp,),
            in_specs=in_specs,
            out_specs=pl.BlockSpec(memory_space=pl.ANY),
            scratch_shapes=[pltpu.VMEM((EXPERT_IN_SLOTS, tr, d // 2), jnp.int32),
                            pltpu.VMEM((EXPERT_OUT_SLOTS, tr, d // 2), jnp.int32),
                            pltpu.VMEM((d, h2w), BF16), pltpu.VMEM((hh, d), BF16),
                            pltpu.SemaphoreType.DMA((EXPERT_IN_SLOTS,)), pltpu.SemaphoreType.DMA((EXPERT_OUT_SLOTS,))]),
        compiler_params=_cparams("arbitrary"),
        name="experts",
    )(tile_lo, tile_hi, *operands)


def _combine_kernel(base_ref, g2_ref, gate_ref, *refs):
    r_refs, o_ref = refs[:TOP_K], refs[TOP_K]
    gate = gate_ref[...]
    half = base_ref.shape[1] // 2
    acc_lo = jnp.zeros((base_ref.shape[0], half), F32)
    acc_hi = jnp.zeros((base_ref.shape[0], half), F32)
    for k in range(TOP_K):
        lo, hi = _unpack_rows(r_refs[k][...])
        g = gate[:, k:k + 1]
        acc_lo = acc_lo + g * lo
        acc_hi = acc_hi + g * hi
    g2 = g2_ref[...]
    o_ref[:, :half] = base_ref[:, :half] + g2[:, :half] * acc_lo
    o_ref[:, half:] = base_ref[:, half:] + g2[:, half:] * acc_hi


def _combine(base, mod4, gate, r, out_so_far, after, *, tm, tokens_per_batch, tok0, total_tokens):
    t, d = base.shape
    tpb = tokens_per_batch // tm
    nblk = t // tm
    blk0 = tok0 // tm
    r_spec = lambda k: pl.BlockSpec((tm, d // 2), lambda i: (k * nblk + i, 0))
    in_specs = [pl.BlockSpec((tm, d), lambda i: (i, 0)),
                pl.BlockSpec((None, None, 1, d), lambda i: ((i + blk0) // tpb, 5, 0, 0)),
                pl.BlockSpec((tm, LANES), lambda i: (i, 0))] + [r_spec(k) for k in range(TOP_K)]
    operands = [base, mod4, gate] + [r] * TOP_K
    n_read = len(operands)
    aliases = {}
    if out_so_far is not None:
        aliases = {_unread_operand(in_specs, operands, out_so_far): 0}
    if after is not None:
        _unread_operand(in_specs, operands, after)
    n_in = len(operands)
    return pl.pallas_call(
        lambda *refs: _combine_kernel(*refs[:n_read], *refs[n_in:]),
        out_shape=jax.ShapeDtypeStruct((total_tokens, d), F32),
        grid=(nblk,),
        in_specs=in_specs,
        out_specs=pl.BlockSpec((tm, d), lambda i: (i + blk0, 0)),
        input_output_aliases=aliases,
        compiler_params=_cparams("arbitrary"),
        name="combine",
    )(*operands)


def _rope_tables(length, head_dim):
    rows = length // GRID_W
    pairs_axis = head_dim // 4
    r = jnp.repeat(jnp.arange(rows, dtype=F32), GRID_W)
    col = jnp.tile(jnp.arange(GRID_W, dtype=F32), rows)
    inv = ROPE_THETA ** (-jnp.arange(pairs_axis, dtype=F32) / pairs_axis)
    ang = jnp.concatenate([r[:, None] * inv, col[:, None] * inv], axis=-1)
    cos, sin = jnp.cos(ang), jnp.sin(ang)
    cos_h = jnp.concatenate([cos, cos], axis=-1)
    sin_h = jnp.concatenate([-sin, sin], axis=-1)
    reps = LANES // head_dim
    return jnp.tile(cos_h, (1, reps)), jnp.tile(sin_h, (1, reps))


def kernel(x, c, ctx, c_ctx, w_ada, b_ada, norm_mix, w_in, q_norm, k_norm, gm_norm, w_spatial, b_spatial,
           out_norm_attn, out_norm_gm, w_out, norm_ffn, w_router, router_bias, w_exp_gu, w_exp_down,
           w_sh_gu, w_sh_down):
    assert w_ada.shape[0] == 1, "single-layer kernel"
    b, length, d = x.shape
    head_dim = q_norm.shape[-1]
    att_w = out_norm_attn.shape[-1]
    gm_w = out_norm_gm.shape[-1]
    gm_groups = w_spatial.shape[1]
    gm_hd = gm_w // gm_groups
    n_exp = w_router.shape[-1]
    t = b * length
    assert head_dim == LANES // 2 and gm_hd == LANES // 2 and w_spatial.shape[-1] == CHUNK
    assert n_exp <= LANES and att_w % MXU_EDGE == 0 and gm_w % MXU_EDGE == 0

    mod_rows = 16
    c_all = jnp.zeros((mod_rows, d), F32).at[:b].set(c).at[b].set(c_ctx)
    mod = _ada(c_all, w_ada[0], b_ada)
    mod4 = mod.reshape(mod_rows, N_MOD, 1, d)

    w_in_b = w_in[0].astype(BF16)
    bd = (jnp.arange(MXU_EDGE)[:, None] // head_dim == jnp.arange(MXU_EDGE)[None, :] // head_dim)
    bd = (bd.astype(F32) / head_dim).astype(BF16)
    qg = jnp.tile(q_norm[0] * (head_dim ** -0.5 * LOG2_E), att_w // head_dim)[None, :]
    kg = jnp.tile(k_norm[0], LANES // head_dim)[None, :]
    gmg = jnp.tile(gm_norm[0], gm_groups)[None, :]
    ws = w_spatial[0].astype(BF16).reshape(gm_groups // 2, 2 * CHUNK, CHUNK)
    bs = jnp.repeat(b_spatial[0].T, gm_hd, axis=1)
    cos_t, sin_t = _rope_tables(length, head_dim)

    k_c, v_c = _ctx_kv(ctx, mod4, b, norm_mix, w_in_b[:, att_w:att_w + 2 * LANES], kg, bd)
    q, k_l, v_l, gmn = _inproj(x, mod4, norm_mix, w_in_b, cos_t, sin_t, qg, kg, gmg, bd, ws, bs,
                               out_norm_gm, att_w=att_w, gm_w=gm_w, tm=min(512, length))
    wr_t = w_router[0].T.astype(BF16)
    rb = jnp.tile(router_bias[0][:, None], (1, LANES))
    w_out_b, w_sg_b, w_sd_b = w_out[0].astype(BF16), w_sh_gu[0].astype(BF16), w_sh_down[0].astype(BF16)
    gmn2, x2 = gmn.reshape(t, gm_w), x.reshape(t, d)
    tm_post = min(512, length)
    tr = 512

    n_parts = MOE_PARTS if b % MOE_PARTS == 0 else 1
    nb = b // n_parts
    tp = nb * length
    nt = tp * TOP_K // tr + n_exp

    state = [dict() for _ in range(n_parts)]
    halves = (nb // 2, nb - nb // 2) if nb >= 2 else (nb,)

    last_tc = [None]

    def attend(part, half):
        st = state[part]
        slot0 = sum(halves[:half])
        after = None if last_tc[0] is st.get("att") else last_tc[0]
        st["att"] = _attention(q, k_c, k_l, v_c, v_l, st.get("att"), after, tq=min(256, length),
                               b0=part * nb + slot0, nb=halves[half], slot0=slot0, slots=nb)
        last_tc[0] = st["att"]

    def route(part):
        st = state[part]
        base, h2p, idx_t, gate, rank_t, cnt = _post(
            st["att"].reshape(tp, att_w), gmn2, x2, mod4, out_norm_attn, w_out_b, norm_ffn, wr_t, rb, w_sg_b, w_sd_b,
            tm=tm_post, tokens_per_batch=length, tok0=part * tp)
        counts = cnt[:, 0].astype(jnp.int32)
        padded = (counts + tr - 1) // tr * tr
        ends = jnp.cumsum(padded)
        offs = ends - padded
        offs_pad = jnp.zeros((LANES,), jnp.int32).at[:n_exp].set(offs)
        xs, pos_t = _dispatch(h2p, idx_t, rank_t, offs_pad, n_slots=nt * tr)
        st.update(base=base, gate=gate, xs=xs, pos_t=pos_t, tile_lo=offs // tr, tile_hi=ends // tr)
        last_tc[0] = base

    def run_experts(part):
        st = state[part]
        ys = _experts(st["tile_lo"], st["tile_hi"], st["xs"], w_exp_gu[0], w_exp_down[0], last_tc[0], tr=tr)
        st["r"] = _collect(ys, st["pos_t"])
        last_tc[0] = ys

    result = [None]

    def finish(part):
        st = state[part]
        after = None if last_tc[0] is result[0] or last_tc[0] is st["base"] else last_tc[0]
        result[0] = _combine(st["base"], mod4, st["gate"], st["r"], result[0], after, tm=min(256, length),
                             tokens_per_batch=length, tok0=part * tp, total_tokens=t)
        last_tc[0] = result[0]

    stages = [(0, lambda p: attend(p, 0))]
    if len(halves) == 2:
        stages.append((1, lambda p: attend(p, 1)))
    stages += [(2, route), (4, run_experts), (7, finish)]
    plan = sorted((when + 3 * part, part, n, fn) for part in range(n_parts) for n, (when, fn) in enumerate(stages))
    for _, part, _, fn in plan:
        fn(part)
    return result[0].reshape(b, length, d)
```

```python
import dataclasses
import functools

import jax
import jax.numpy as jnp
from jax import lax
from jax.experimental import pallas as pl
from jax.experimental.pallas import tpu as pltpu
from jax.experimental.pallas import tpu_sc as plsc

F32 = jnp.float32
BF16 = jnp.bfloat16

EPS = 1e-6
GRID_W = 64
ROPE_THETA = 10000.0
ATT_KV_HEADS = 2
TOP_K = 6
ROUTED_SCALE = 2.5
N_MOD = 6
CHUNK = 128
LOG2_E = 1.4426950408889634

LANES = 128
MXU_EDGE = 256
VMEM_LIMIT_BYTES = 56 * 1024 * 1024
SUBLANES = 8
CHOICE_ROWS = SUBLANES
SC_WINDOW = 64
LEAD_PART_SHARE = 0.625
EXPERT_IN_SLOTS = 4
EXPERT_OUT_SLOTS = 2


def _cparams(*sem):
    return pltpu.CompilerParams(dimension_semantics=sem, vmem_limit_bytes=VMEM_LIMIT_BYTES)


def _dot(a, b):
    return jnp.dot(a, b, preferred_element_type=F32)


def _dot_nt(a, b):
    return lax.dot_general(a, b, (((1,), (1,)), ((), ())), preferred_element_type=F32)


def _rms(x, g):
    return x * lax.rsqrt(jnp.mean(x * x, axis=-1, keepdims=True) + EPS) * g


def _group_mean_sq(x, bd_ref):
    x2 = (x * x).astype(BF16)
    parts = [_dot(x2[:, i:i + MXU_EDGE], bd_ref[...]) for i in range(0, x.shape[1], MXU_EDGE)]
    return parts[0] if len(parts) == 1 else jnp.concatenate(parts, axis=-1)


def _dup_halves(x, lane):
    xr = pltpu.roll(x, LANES // 2, axis=1)
    lo = lane < LANES // 2
    return jnp.where(lo, x, xr), jnp.where(lo, xr, x)


def _with_ones(x, lane):
    xr = pltpu.roll(x, LANES // 2, axis=1)
    lo = lane < LANES // 2
    return jnp.concatenate([jnp.where(lo, x, 1.0), jnp.where(lo, xr, 1.0)], axis=-1)


_HI16 = 0xFFFF0000


def _pack_rows(x):
    half = x.shape[1] // 2
    rounded = x.astype(BF16).astype(F32)
    bits = lax.bitcast_convert_type(rounded, jnp.uint32)
    word = (bits[:, :half] >> 16) | (bits[:, half:] & jnp.uint32(_HI16))
    return lax.bitcast_convert_type(word, jnp.int32)


def _unpack_rows(w):
    bits = lax.bitcast_convert_type(w, jnp.uint32)
    lo = lax.bitcast_convert_type(bits << 16, F32)
    hi = lax.bitcast_convert_type(bits & jnp.uint32(_HI16), F32)
    return lo, hi


def _ada_kernel(c_ref, w_ref, b_ref, o_ref):
    s = jax.nn.silu(c_ref[...])
    o_ref[...] = jnp.dot(s, w_ref[...], precision=lax.Precision.HIGHEST,
                         preferred_element_type=F32) + b_ref[...]


def _ada(c_all, w_ada, b_ada):
    rows, d = c_all.shape
    n = w_ada.shape[1]
    bn = 512
    return pl.pallas_call(
        _ada_kernel,
        out_shape=jax.ShapeDtypeStruct((rows, n), F32),
        grid=(n // bn,),
        in_specs=[pl.BlockSpec((rows, d), lambda j: (0, 0)),
                  pl.BlockSpec((d, bn), lambda j: (0, j)),
                  pl.BlockSpec((1, bn), lambda j: (0, j))],
        out_specs=pl.BlockSpec((rows, bn), lambda j: (0, j)),
        compiler_params=_cparams("arbitrary"),
        name="ada",
    )(c_all, w_ada, b_ada)


def _ctx_kv_kernel(x_ref, sh_ref, sc_ref, nmix_ref, w_ref, kg_ref, bd_ref, k_ref, v_ref):
    h = _rms(x_ref[...], nmix_ref[...]) * (1.0 + sc_ref[...]) + sh_ref[...]
    p = _dot(h.astype(BF16), w_ref[...])
    ms = _group_mean_sq(p, bd_ref)
    k = p[:, :LANES] * lax.rsqrt(ms[:, :LANES] + EPS) * kg_ref[...]
    v = p[:, LANES:]
    lane = lax.broadcasted_iota(jnp.int32, k.shape, 1)
    ka, kb = _dup_halves(k, lane)
    k_ref[...] = jnp.concatenate([ka, kb], axis=-1).astype(BF16)
    v_ref[...] = _with_ones(v, lane).astype(BF16)


def _ctx_kv(ctx, mod4, ctx_row, nmix, w_kv, kg, bd):
    b, n_ctx, d = ctx.shape
    kv_w = 2 * LANES
    return pl.pallas_call(
        _ctx_kv_kernel,
        out_shape=(jax.ShapeDtypeStruct((b, n_ctx, kv_w), BF16),) * 2,
        grid=(b,),
        in_specs=[pl.BlockSpec((None, n_ctx, d), lambda i: (i, 0, 0)),
                  pl.BlockSpec((None, None, 1, d), lambda i: (ctx_row, 0, 0, 0)),
                  pl.BlockSpec((None, None, 1, d), lambda i: (ctx_row, 1, 0, 0)),
                  pl.BlockSpec((1, d), lambda i: (0, 0)),
                  pl.BlockSpec(w_kv.shape, lambda i: (0, 0)),
                  pl.BlockSpec((1, LANES), lambda i: (0, 0)),
                  pl.BlockSpec(bd.shape, lambda i: (0, 0))],
        out_specs=(pl.BlockSpec((None, n_ctx, kv_w), lambda i: (i, 0, 0)),) * 2,
        compiler_params=_cparams("arbitrary"),
        name="ctx_kv",
    )(ctx, mod4, mod4, nmix, w_kv, kg, bd)


def _inproj_kernel(x_ref, sh_ref, sc_ref, nmix_ref, w_ref, cos_ref, sin_ref, qg_ref, kg_ref, gmg_ref,
                   bd_ref, ws_ref, bs_ref, og_ref, q_ref, k_ref, v_ref, gm_ref, *, att_w, gm_w):
    tm = x_ref.shape[0]
    h = _rms(x_ref[...], nmix_ref[...]) * (1.0 + sc_ref[...]) + sh_ref[...]
    p = _dot(h.astype(BF16), w_ref[...])

    cos1, sin1 = cos_ref[...], sin_ref[...]
    lane1 = lax.broadcasted_iota(jnp.int32, (tm, LANES), 1)

    def rope(xn, cosw, sinw, lanew):
        w = xn.shape[1]
        fwd = pltpu.roll(xn, w - 32, axis=1)
        bwd = pltpu.roll(xn, 32, axis=1)
        swapped = jnp.where((lanew & 63) < 32, fwd, bwd)
        return xn * cosw + swapped * sinw

    q = p[:, :att_w]
    qn = q * lax.rsqrt(_group_mean_sq(q, bd_ref) + EPS) * qg_ref[...]
    reps = att_w // LANES
    cosq = jnp.concatenate([cos1] * reps, axis=-1)
    sinq = jnp.concatenate([sin1] * reps, axis=-1)
    laneq = lax.broadcasted_iota(jnp.int32, (tm, att_w), 1)
    q_ref[...] = rope(qn, cosq, sinq, laneq).astype(BF16)

    kv = p[:, att_w:att_w + 2 * LANES]
    ms = _group_mean_sq(kv, bd_ref)
    kn = kv[:, :LANES] * lax.rsqrt(ms[:, :LANES] + EPS) * kg_ref[...]
    k = rope(kn, cos1, sin1, lane1)
    v = kv[:, LANES:]
    ka, kb = _dup_halves(k, lane1)
    k_ref[...] = jnp.concatenate([ka, kb], axis=-1).astype(BF16)
    v_ref[...] = _with_ones(v, lane1).astype(BF16)

    u0 = att_w + 2 * LANES
    u = jax.nn.gelu(p[:, u0:u0 + gm_w])
    gt = jax.nn.gelu(p[:, u0 + gm_w:u0 + 2 * gm_w])
    gt = (gt * lax.rsqrt(_group_mean_sq(gt, bd_ref) + EPS) * gmg_ref[...]).astype(BF16)
    lo_half = lax.broadcasted_iota(jnp.int32, (CHUNK, LANES), 1) < LANES // 2
    rows = []
    for c in range(tm // CHUNK):
        blocks = []
        for j in range(gm_w // LANES):
            g = gt[c * CHUNK:(c + 1) * CHUNK, j * LANES:(j + 1) * LANES]
            r = _dot(ws_ref[j], g)
            blocks.append(jnp.where(lo_half, r[:CHUNK], r[CHUNK:]))
        rows.append(jnp.concatenate(blocks, axis=-1) + bs_ref[...])
    mixed = jnp.concatenate(rows, axis=0)
    gm_ref[...] = _rms(u * mixed, og_ref[...]).astype(BF16)


def _inproj(x, mod4, nmix, w_in, cos_t, sin_t, qg, kg, gmg, bd, ws, bs, og, *, att_w, gm_w, tm):
    b, length, d = x.shape
    kv_w = 2 * LANES
    n_in = w_in.shape[1]
    const = lambda shape: pl.BlockSpec(shape, lambda i, j: (0,) * len(shape))
    return pl.pallas_call(
        functools.partial(_inproj_kernel, att_w=att_w, gm_w=gm_w),
        out_shape=(jax.ShapeDtypeStruct((b, length, att_w), BF16),
                   jax.ShapeDtypeStruct((b, length, kv_w), BF16),
                   jax.ShapeDtypeStruct((b, length, kv_w), BF16),
                   jax.ShapeDtypeStruct((b, length, gm_w), BF16)),
        grid=(b, length // tm),
        in_specs=[pl.BlockSpec((None, tm, d), lambda i, j: (i, j, 0)),
                  pl.BlockSpec((None, None, 1, d), lambda i, j: (i, 0, 0, 0)),
                  pl.BlockSpec((None, None, 1, d), lambda i, j: (i, 1, 0, 0)),
                  const((1, d)),
                  const((d, n_in)),
                  pl.BlockSpec((tm, LANES), lambda i, j: (j, 0)),
                  pl.BlockSpec((tm, LANES), lambda i, j: (j, 0)),
                  const((1, att_w)), const((1, LANES)), const((1, gm_w)),
                  const(bd.shape), const(ws.shape), const(bs.shape), const((1, gm_w))],
        out_specs=(pl.BlockSpec((None, tm, att_w), lambda i, j: (i, j, 0)),
                   pl.BlockSpec((None, tm, kv_w), lambda i, j: (i, j, 0)),
                   pl.BlockSpec((None, tm, kv_w), lambda i, j: (i, j, 0)),
                   pl.BlockSpec((None, tm, gm_w), lambda i, j: (i, j, 0))),
        compiler_params=_cparams("arbitrary", "arbitrary"),
        name="inproj",
    )(x, mod4, mod4, nmix, w_in, cos_t, sin_t, qg, kg, gmg, bd, ws, bs, og)


def _attn_kernel(q_ref, kc_ref, kl_ref, vc_ref, vl_ref, o_ref, s_buf, p_buf, *, tq):
    n_ctx = kc_ref.shape[0]
    heads = q_ref.shape[1] // (LANES // 2)
    nj = q_ref.shape[0] // tq
    half = LANES // 2
    lo = lax.broadcasted_iota(jnp.int32, (tq, LANES), 1) < half

    s_buf[...] = jnp.zeros_like(s_buf)
    p_buf[...] = jnp.ones_like(p_buf)

    def scores(j, g):
        q = q_ref[pl.ds(j * tq, tq), (g // 2) * LANES:(g // 2 + 1) * LANES]
        zero = jnp.zeros_like(q)
        qh = jnp.where(lo, q, zero) if g % 2 == 0 else jnp.where(lo, zero, q)
        s_buf[g % 2, :, :n_ctx] = _dot_nt(qh, kc_ref[...])
        s_buf[g % 2, :, n_ctx:] = _dot_nt(qh, kl_ref[...])

    def probs(g):
        s = s_buf[g % 2]
        p_buf[g % 2] = jnp.exp2(s - jnp.max(s, axis=-1, keepdims=True)).astype(BF16)

    def output(j, g):
        p = p_buf[g % 2]
        w = _dot(p[:, :n_ctx], vc_ref[...]) + _dot(p[:, n_ctx:], vl_ref[...])
        t = jnp.where(pl.program_id(1) == 0, w[:, :LANES], w[:, LANES:])
        r = pltpu.roll(t, half, axis=1)
        rows = pl.ds(j * tq, tq)
        if g % 2 == 0:
            o_ref[rows, g * half:(g + 1) * half] = (t / r)[:, :half].astype(o_ref.dtype)
        else:
            o_ref[rows, g * half:(g + 1) * half] = (r / t)[:, half:].astype(o_ref.dtype)

    def trip(j, carry):
        j_prev = jnp.maximum(j - 1, 0)
        for g in range(heads):
            scores(j, g)
            probs((g - 1) % heads)
            output(j if g >= 2 else j_prev, (g - 2) % heads)
        return carry

    lax.fori_loop(0, nj, trip, 0)
    probs(heads - 1)
    output(nj - 1, heads - 2)
    output(nj - 1, heads - 1)


def _unread_operand(in_specs, operands, array):
    in_specs.append(pl.BlockSpec(memory_space=pl.ANY))
    operands.append(array)
    return len(operands) - 1


def _attention(q, k_c, k_l, v_c, v_l, att_so_far, after, *, tq, b0, nb, slot0, slots):
    _, length, att_w = q.shape
    n_ctx = k_c.shape[1]
    group_w = att_w // ATT_KV_HEADS
    assert (group_w // (LANES // 2)) % 2 == 0 and length % tq == 0
    k_blk = lambda n: pl.BlockSpec((None, n, LANES), lambda i, h: (i + b0, 0, h))
    v_blk = lambda n: pl.BlockSpec((None, n, ATT_KV_HEADS * LANES), lambda i, h: (i + b0, 0, 0))
    in_specs = [pl.BlockSpec((None, length, group_w), lambda i, h: (i + b0, 0, h)),
                k_blk(n_ctx), k_blk(length), v_blk(n_ctx), v_blk(length)]
    operands = [q, k_c, k_l, v_c, v_l]
    n_read = len(operands)
    aliases = {}
    if att_so_far is not None:
        aliases = {_unread_operand(in_specs, operands, att_so_far): 0}
    if after is not None:
        _unread_operand(in_specs, operands, after)
    n_in = len(operands)
    return pl.pallas_call(
        lambda *refs: _attn_kernel(*refs[:n_read], *refs[n_in:], tq=tq),
        out_shape=jax.ShapeDtypeStruct((slots, length, att_w), BF16),
        grid=(nb, ATT_KV_HEADS),
        in_specs=in_specs,
        out_specs=pl.BlockSpec((None, length, group_w), lambda i, h: (i + slot0, 0, h)),
        scratch_shapes=[pltpu.VMEM((2, tq, n_ctx + length), F32), pltpu.VMEM((2, tq, n_ctx + length), BF16)],
        input_output_aliases=aliases,
        compiler_params=_cparams("arbitrary", "arbitrary"),
        name="attn",
    )(*operands)


def _post_kernel(att_ref, gm_ref, x_ref, g1_ref, sh2_ref, sc2_ref, g2_ref, oa_ref, wo_ref, nffn_ref,
                 wrt_ref, rb_ref, wsg_ref, wsd_ref,
                 base_ref, h2p_ref, idx_ref, gate_ref, rank_ref, cnt_ref, carry_ref):
    i = pl.program_id(0)
    tm = x_ref.shape[0]
    n_exp = wrt_ref.shape[0]

    @pl.when(i == 0)
    def _():
        carry_ref[...] = jnp.zeros_like(carry_ref)

    att_n = _rms(att_ref[...].astype(F32), oa_ref[...]).astype(BF16)
    y = _dot(jnp.concatenate([att_n, gm_ref[...]], axis=-1), wo_ref[...])
    x_new = x_ref[...] + g1_ref[...] * y
    h2 = _rms(x_new, nffn_ref[...]) * (1.0 + sc2_ref[...]) + sh2_ref[...]
    h2p_ref[...] = _pack_rows(h2)

    h_hi = h2.astype(BF16)
    scores_t = jax.nn.sigmoid(_dot_nt(wrt_ref[...], h_hi))
    reps = tm // LANES
    sel = scores_t + jnp.concatenate([rb_ref[...]] * reps, axis=-1)
    expert = lax.broadcasted_iota(jnp.int32, scores_t.shape, 0).astype(F32)
    neg = jnp.float32(-jnp.inf)

    onehot = jnp.zeros(scores_t.shape, F32)
    hits, idxs, svals = [], [], []
    for _k in range(TOP_K):
        m = jnp.max(sel, axis=0, keepdims=True)
        idx = jnp.min(jnp.where(sel == m, expert, float(n_exp)), axis=0, keepdims=True)
        hit = expert == idx
        hits.append(hit)
        idxs.append(idx)
        svals.append(jnp.sum(jnp.where(hit, scores_t, 0.0), axis=0, keepdims=True))
        onehot = onehot + jnp.where(hit, 1.0, 0.0)
        sel = jnp.where(hit, neg, sel)
    ssum = functools.reduce(lambda a, b: a + b, svals)

    r_io = lax.broadcasted_iota(jnp.int32, (tm, tm), 0)
    c_io = lax.broadcasted_iota(jnp.int32, (tm, tm), 1)
    tri = jnp.where(r_io < c_io, 1.0, 0.0).astype(BF16)
    carry = carry_ref[...]
    before = _dot(onehot.astype(BF16), tri) + jnp.concatenate([carry] * reps, axis=-1)
    carry = carry + jnp.sum(onehot, axis=1, keepdims=True)
    carry_ref[...] = carry
    cnt_ref[...] = carry

    row = lax.broadcasted_iota(jnp.int32, (idx_ref.shape[0], tm), 0)
    idx_o = jnp.zeros(row.shape, F32)
    rank_o = jnp.zeros(row.shape, F32)
    gate_o = jnp.zeros(row.shape, F32)
    for k in range(TOP_K):
        rank = jnp.sum(jnp.where(hits[k], before, 0.0), axis=0, keepdims=True)
        idx_o = jnp.where(row == k, idxs[k], idx_o)
        rank_o = jnp.where(row == k, rank, rank_o)
        gate_o = jnp.where(row == k, svals[k] / ssum * ROUTED_SCALE, gate_o)
    idx_ref[...] = idx_o.astype(jnp.int32)
    rank_ref[...] = rank_o.astype(jnp.int32)
    gate_pad = jnp.concatenate([gate_o, jnp.zeros((LANES - gate_o.shape[0], tm), F32)], axis=0)
    gate_ref[...] = gate_pad.T

    gu = _dot(h_hi, wsg_ref[...])
    hs = gu.shape[1] // 2
    act = (jax.nn.silu(gu[:, :hs]) * gu[:, hs:]).astype(BF16)
    base_ref[...] = x_new + g2_ref[...] * _dot(act, wsd_ref[...])


def _post(att, gmn, x2, mod4, oa, w_out, nffn, wr_t, rb, w_sg, w_sd, *, tm, tokens_per_batch, tok0):
    t, att_w = att.shape
    d = x2.shape[1]
    n_exp = wr_t.shape[0]
    gm_w = gmn.shape[1]
    tpb = tokens_per_batch // tm
    blk0 = tok0 // tm
    const = lambda shape: pl.BlockSpec(shape, lambda i: (0,) * len(shape))
    mrow = lambda j: pl.BlockSpec((None, None, 1, d), lambda i: ((i + blk0) // tpb, j, 0, 0))
    by_choice = jax.ShapeDtypeStruct((CHOICE_ROWS, t), jnp.int32)
    return pl.pallas_call(
        _post_kernel,
        out_shape=(jax.ShapeDtypeStruct((t, d), F32), jax.ShapeDtypeStruct((t, d // 2), jnp.int32),
                   by_choice, jax.ShapeDtypeStruct((t, LANES), F32), by_choice,
                   jax.ShapeDtypeStruct((n_exp, LANES), F32)),
        grid=(t // tm,),
        in_specs=[pl.BlockSpec((tm, att_w), lambda i: (i, 0)),
                  pl.BlockSpec((tm, gm_w), lambda i: (i + blk0, 0)),
                  pl.BlockSpec((tm, d), lambda i: (i + blk0, 0)),
                  mrow(2), mrow(3), mrow(4), mrow(5),
                  const((1, att_w)), const(w_out.shape), const((1, d)),
                  const(wr_t.shape), const(rb.shape),
                  const(w_sg.shape), const(w_sd.shape)],
        out_specs=(pl.BlockSpec((tm, d), lambda i: (i, 0)), pl.BlockSpec((tm, d // 2), lambda i: (i, 0)),
                   pl.BlockSpec((CHOICE_ROWS, tm), lambda i: (0, i)), pl.BlockSpec((tm, LANES), lambda i: (i, 0)),
                   pl.BlockSpec((CHOICE_ROWS, tm), lambda i: (0, i)), pl.BlockSpec((n_exp, LANES), lambda i: (0, 0))),
        scratch_shapes=[pltpu.VMEM((n_exp, LANES), F32)],
        compiler_params=_cparams("arbitrary"),
        name="post",
    )(att, gmn, x2, mod4, mod4, mod4, mod4, oa, w_out, nffn, wr_t, rb, w_sg, w_sd)


def _sc_mesh_info():
    info = plsc.get_sparse_core_info()
    mesh = plsc.VectorSubcoreMesh(core_axis_name="c", subcore_axis_name="s")
    return mesh, info.num_cores, info.num_subcores, info.num_lanes


def _sc_params():
    cp = pltpu.CompilerParams()
    if "needs_layout_passes" in pltpu.CompilerParams.__dataclass_fields__:
        cp = dataclasses.replace(cp, needs_layout_passes=False)
    return cp


def _dispatch(h2p, idx_t, rank_t, offs, *, n_slots):
    t, dw = h2p.shape
    mesh, n_cores, n_sub, n_lanes = _sc_mesh_info()
    per_worker = t // (n_cores * n_sub)
    w = SC_WINDOW
    assert per_worker % w == 0 and w % n_lanes == 0

    @functools.partial(
        pl.kernel, mesh=mesh,
        out_type=(jax.ShapeDtypeStruct((n_slots, dw), jnp.int32), jax.ShapeDtypeStruct((CHOICE_ROWS, t), jnp.int32)),
        scratch_types=[pltpu.VMEM(offs.shape, jnp.int32)] + [pltpu.VMEM((TOP_K, w), jnp.int32)] * 3
        + [pltpu.VMEM((w, dw), jnp.int32), pltpu.SemaphoreType.DMA],
        compiler_params=_sc_params(),
    )
    def run(h_hbm, idx_hbm, rank_hbm, offs_hbm, xs_hbm, pos_hbm, offs_v, idx_v, rank_v, pos_v, rows_v, sem):
        base = (lax.axis_index("s") * n_cores + lax.axis_index("c")) * per_worker
        pltpu.sync_copy(offs_hbm, offs_v)

        @pl.loop(0, per_worker // w)
        def _(i):
            t0 = base + i * w
            pltpu.sync_copy(h_hbm.at[pl.ds(t0, w)], rows_v)
            for k in range(TOP_K):
                pltpu.sync_copy(idx_hbm.at[k, pl.ds(t0, w)], idx_v.at[k])
                pltpu.sync_copy(rank_hbm.at[k, pl.ds(t0, w)], rank_v.at[k])
            for k in range(TOP_K):
                for j in range(0, w, n_lanes):
                    group_start = plsc.load_gather(offs_v, [idx_v[k, pl.ds(j, n_lanes)]])
                    pos_v[k, pl.ds(j, n_lanes)] = group_start + rank_v[k, pl.ds(j, n_lanes)]
            copies = [pltpu.async_copy(rows_v, xs_hbm.at[pos_v.at[k]], sem) for k in range(TOP_K)]
            for k in range(TOP_K):
                pltpu.sync_copy(pos_v.at[k], pos_hbm.at[k, pl.ds(t0, w)])
            for cp in copies:
                cp.wait()

    return run(h2p, idx_t, rank_t, offs)


def _collect(ys, pos_t):
    t = pos_t.shape[1]
    dw = ys.shape[1]
    mesh, n_cores, n_sub, _ = _sc_mesh_info()
    per_worker = t // (n_cores * n_sub)
    w = SC_WINDOW
    assert per_worker % w == 0

    @functools.partial(
        pl.kernel, mesh=mesh,
        out_type=jax.ShapeDtypeStruct((TOP_K * t, dw), jnp.int32),
        scratch_types=[pltpu.VMEM((TOP_K, w), jnp.int32), pltpu.VMEM((w, dw), jnp.int32), pltpu.SemaphoreType.DMA],
        compiler_params=_sc_params(),
    )
    def run(ys_hbm, pos_hbm, r_hbm, pos_v, rows_v, sem):
        base = (lax.axis_index("s") * n_cores + lax.axis_index("c")) * per_worker

        @pl.loop(0, per_worker // w)
        def _(i):
            t0 = base + i * w
            for k in range(TOP_K):
                pltpu.sync_copy(pos_hbm.at[k, pl.ds(t0, w)], pos_v.at[k])
            for k in range(TOP_K):
                pltpu.async_copy(ys_hbm.at[pos_v.at[k]], rows_v, sem).wait()
                pltpu.sync_copy(rows_v, r_hbm.at[pl.ds(k * t + t0, w)])

    return run(ys, pos_t)


def _experts_kernel(t0_ref, t1_ref, xs_hbm, wgu_ref, wdn_ref, ys_hbm, xbuf, ybuf, wgu_b, wdn_b, sem_in, sem_out):
    e = pl.program_id(0)
    n_exp = pl.num_programs(0)
    n_in, tr, half = xbuf.shape
    n_out = ybuf.shape[0]
    n_tiles = t1_ref[n_exp - 1]

    def fetch(t):
        slot = t & (n_in - 1)
        return pltpu.make_async_copy(xs_hbm.at[pl.ds(t * tr, tr)], xbuf.at[slot], sem_in.at[slot])

    def flush(t):
        slot = t & (n_out - 1)
        return pltpu.make_async_copy(ybuf.at[slot], ys_hbm.at[pl.ds(t * tr, tr)], sem_out.at[slot])

    @pl.when(e == 0)
    def _():
        for t in range(n_in - 1):
            @pl.when(t < n_tiles)
            def _():
                fetch(t).start()

    wgu_b[...] = wgu_ref[...].astype(BF16)
    wdn_b[...] = wdn_ref[...].astype(BF16)

    def tile(t, carry):
        fetch(t).wait()

        @pl.when(t + n_in - 1 < n_tiles)
        def _():
            fetch(t + n_in - 1).start()

        lo, hi = _unpack_rows(xbuf[t & (n_in - 1)])
        gu = _dot(lo.astype(BF16), wgu_b[:half, :]) + _dot(hi.astype(BF16), wgu_b[half:, :])
        hh = gu.shape[1] // 2
        act = (jax.nn.silu(gu[:, :hh]) * gu[:, hh:]).astype(BF16)
        y = _pack_rows(_dot(act, wdn_b[...]))

        @pl.when(t >= n_out)
        def _():
            flush(t - n_out).wait()

        ybuf[t & (n_out - 1)] = y
        flush(t).start()
        return carry

    lax.fori_loop(t0_ref[e], t1_ref[e], tile, 0)

    @pl.when(e == n_exp - 1)
    def _():
        for back in range(n_out, 0, -1):
            @pl.when(n_tiles >= back)
            def _():
                flush(n_tiles - back).wait()


def _experts(tile_lo, tile_hi, xs, w_gu, w_dn, after, *, tr):
    n_exp, d, h2w = w_gu.shape
    hh = w_dn.shape[1]
    in_specs = [pl.BlockSpec(memory_space=pl.ANY),
                pl.BlockSpec((None, d, h2w), lambda e, lo, hi: (e, 0, 0)),
                pl.BlockSpec((None, hh, d), lambda e, lo, hi: (e, 0, 0))]
    operands = [xs, w_gu, w_dn]
    n_read = 2 + len(operands)
    if after is not None:
        _unread_operand(in_specs, operands, after)
    n_in = 2 + len(operands)
    return pl.pallas_call(
        lambda *refs: _experts_kernel(*refs[:n_read], *refs[n_in:]),
        out_shape=jax.ShapeDtypeStruct(xs.shape, jnp.int32),
        grid_spec=pltpu.PrefetchScalarGridSpec(
            num_scalar_prefetch=2,
            grid=(n_exp,),
            in_specs=in_specs,
            out_specs=pl.BlockSpec(memory_space=pl.ANY),
            scratch_shapes=[pltpu.VMEM((EXPERT_IN_SLOTS, tr, d // 2), jnp.int32),
                            pltpu.VMEM((EXPERT_OUT_SLOTS, tr, d // 2), jnp.int32),
                            pltpu.VMEM((d, h2w), BF16), pltpu.VMEM((hh, d), BF16),
                            pltpu.SemaphoreType.DMA((EXPERT_IN_SLOTS,)), pltpu.SemaphoreType.DMA((EXPERT_OUT_SLOTS,))]),
        compiler_params=_cparams("arbitrary"),
        name="experts",
    )(tile_lo, tile_hi, *operands)


def _combine_kernel(base_ref, g2_ref, gate_ref, *refs):
    r_refs, o_ref = refs[:TOP_K], refs[TOP_K]
    gate = gate_ref[...]
    half = base_ref.shape[1] // 2
    acc_lo = jnp.zeros((base_ref.shape[0], half), F32)
    acc_hi = jnp.zeros((base_ref.shape[0], half), F32)
    for k in range(TOP_K):
        lo, hi = _unpack_rows(r_refs[k][...])
        g = gate[:, k:k + 1]
        acc_lo = acc_lo + g * lo
        acc_hi = acc_hi + g * hi
    g2 = g2_ref[...]
    o_ref[:, :half] = base_ref[:, :half] + g2[:, :half] * acc_lo
    o_ref[:, half:] = base_ref[:, half:] + g2[:, half:] * acc_hi


def _combine(base, mod4, gate, r, out_so_far, after, *, tm, tokens_per_batch, tok0, total_tokens):
    t, d = base.shape
    tpb = tokens_per_batch // tm
    nblk = t // tm
    blk0 = tok0 // tm
    r_spec = lambda k: pl.BlockSpec((tm, d // 2), lambda i: (k * nblk + i, 0))
    in_specs = [pl.BlockSpec((tm, d), lambda i: (i, 0)),
                pl.BlockSpec((None, None, 1, d), lambda i: ((i + blk0) // tpb, 5, 0, 0)),
                pl.BlockSpec((tm, LANES), lambda i: (i, 0))] + [r_spec(k) for k in range(TOP_K)]
    operands = [base, mod4, gate] + [r] * TOP_K
    n_read = len(operands)
    aliases = {}
    if out_so_far is not None:
        aliases = {_unread_operand(in_specs, operands, out_so_far): 0}
    if after is not None:
        _unread_operand(in_specs, operands, after)
    n_in = len(operands)
    return pl.pallas_call(
        lambda *refs: _combine_kernel(*refs[:n_read], *refs[n_in:]),
        out_shape=jax.ShapeDtypeStruct((total_tokens, d), F32),
        grid=(nblk,),
        in_specs=in_specs,
        out_specs=pl.BlockSpec((tm, d), lambda i: (i + blk0, 0)),
        input_output_aliases=aliases,
        compiler_params=_cparams("arbitrary"),
        name="combine",
    )(*operands)


def _rope_tables(length, head_dim):
    rows = length // GRID_W
    pairs_axis = head_dim // 4
    r = jnp.repeat(jnp.arange(rows, dtype=F32), GRID_W)
    col = jnp.tile(jnp.arange(GRID_W, dtype=F32), rows)
    inv = ROPE_THETA ** (-jnp.arange(pairs_axis, dtype=F32) / pairs_axis)
    ang = jnp.concatenate([r[:, None] * inv, col[:, None] * inv], axis=-1)
    cos, sin = jnp.cos(ang), jnp.sin(ang)
    cos_h = jnp.concatenate([cos, cos], axis=-1)
    sin_h = jnp.concatenate([-sin, sin], axis=-1)
    reps = LANES // head_dim
    return jnp.tile(cos_h, (1, reps)), jnp.tile(sin_h, (1, reps))


def _part_batches(b):
    if b < 2:
        return [b]
    lead = max(1, min(b - 1, round(LEAD_PART_SHARE * b)))
    return [lead, b - lead]


def kernel(x, c, ctx, c_ctx, w_ada, b_ada, norm_mix, w_in, q_norm, k_norm, gm_norm, w_spatial, b_spatial,
           out_norm_attn, out_norm_gm, w_out, norm_ffn, w_router, router_bias, w_exp_gu, w_exp_down,
           w_sh_gu, w_sh_down):
    assert w_ada.shape[0] == 1, "single-layer kernel"
    b, length, d = x.shape
    head_dim = q_norm.shape[-1]
    att_w = out_norm_attn.shape[-1]
    gm_w = out_norm_gm.shape[-1]
    gm_groups = w_spatial.shape[1]
    gm_hd = gm_w // gm_groups
    n_exp = w_router.shape[-1]
    t = b * length
    assert head_dim == LANES // 2 and gm_hd == LANES // 2 and w_spatial.shape[-1] == CHUNK
    assert n_exp <= LANES and att_w % MXU_EDGE == 0 and gm_w % MXU_EDGE == 0

    mod_rows = 16
    c_all = jnp.zeros((mod_rows, d), F32).at[:b].set(c).at[b].set(c_ctx)
    mod = _ada(c_all, w_ada[0], b_ada)
    mod4 = mod.reshape(mod_rows, N_MOD, 1, d)

    w_in_b = w_in[0].astype(BF16)
    bd = (jnp.arange(MXU_EDGE)[:, None] // head_dim == jnp.arange(MXU_EDGE)[None, :] // head_dim)
    bd = (bd.astype(F32) / head_dim).astype(BF16)
    qg = jnp.tile(q_norm[0] * (head_dim ** -0.5 * LOG2_E), att_w // head_dim)[None, :]
    kg = jnp.tile(k_norm[0], LANES // head_dim)[None, :]
    gmg = jnp.tile(gm_norm[0], gm_groups)[None, :]
    ws = w_spatial[0].astype(BF16).reshape(gm_groups // 2, 2 * CHUNK, CHUNK)
    bs = jnp.repeat(b_spatial[0].T, gm_hd, axis=1)
    cos_t, sin_t = _rope_tables(length, head_dim)

    k_c, v_c = _ctx_kv(ctx, mod4, b, norm_mix, w_in_b[:, att_w:att_w + 2 * LANES], kg, bd)
    q, k_l, v_l, gmn = _inproj(x, mod4, norm_mix, w_in_b, cos_t, sin_t, qg, kg, gmg, bd, ws, bs,
                               out_norm_gm, att_w=att_w, gm_w=gm_w, tm=min(512, length))
    wr_t = w_router[0].T.astype(BF16)
    rb = jnp.tile(router_bias[0][:, None], (1, LANES))
    w_out_b, w_sg_b, w_sd_b = w_out[0].astype(BF16), w_sh_gu[0].astype(BF16), w_sh_down[0].astype(BF16)
    gmn2, x2 = gmn.reshape(t, gm_w), x.reshape(t, d)
    tm_post = min(512, length)
    tr = 512

    part_nb = _part_batches(b)
    n_parts = len(part_nb)
    part_b0 = [sum(part_nb[:p]) for p in range(n_parts)]
    state = [dict() for _ in range(n_parts)]

    last_tc = [None]

    def halves(part):
        nb = part_nb[part]
        return (nb // 2, nb - nb // 2) if nb >= 2 else (nb,)

    def attend(part, half):
        if half >= len(halves(part)):
            return
        st = state[part]
        slot0 = sum(halves(part)[:half])
        after = None if last_tc[0] is st.get("att") else last_tc[0]
        st["att"] = _attention(q, k_c, k_l, v_c, v_l, st.get("att"), after, tq=min(256, length),
                               b0=part_b0[part] + slot0, nb=halves(part)[half], slot0=slot0, slots=part_nb[part])
        last_tc[0] = st["att"]

    def route(part):
        st = state[part]
        tp = part_nb[part] * length
        nt = tp * TOP_K // tr + n_exp
        base, h2p, idx_t, gate, rank_t, cnt = _post(
            st["att"].reshape(tp, att_w), gmn2, x2, mod4, out_norm_attn, w_out_b, norm_ffn, wr_t, rb, w_sg_b, w_sd_b,
            tm=tm_post, tokens_per_batch=length, tok0=part_b0[part] * length)
        counts = cnt[:, 0].astype(jnp.int32)
        padded = (counts + tr - 1) // tr * tr
        ends = jnp.cumsum(padded)
        offs = ends - padded
        offs_pad = jnp.zeros((LANES,), jnp.int32).at[:n_exp].set(offs)
        xs, pos_t = _dispatch(h2p, idx_t, rank_t, offs_pad, n_slots=nt * tr)
        st.update(base=base, gate=gate, xs=xs, pos_t=pos_t, tile_lo=offs // tr, tile_hi=ends // tr)
        last_tc[0] = base

    def run_experts(part):
        st = state[part]
        ys = _experts(st["tile_lo"], st["tile_hi"], st["xs"], w_exp_gu[0], w_exp_down[0], last_tc[0], tr=tr)
        st["r"] = _collect(ys, st["pos_t"])
        last_tc[0] = ys

    result = [None]

    def finish(part):
        st = state[part]
        after = None if last_tc[0] is result[0] or last_tc[0] is st["base"] else last_tc[0]
        result[0] = _combine(st["base"], mod4, st["gate"], st["r"], result[0], after, tm=min(256, length),
                             tokens_per_batch=length, tok0=part_b0[part] * length, total_tokens=t)
        last_tc[0] = result[0]

    stages = [(0, lambda p: attend(p, 0)), (1, lambda p: attend(p, 1)), (2, route), (4, run_experts), (7, finish)]
    plan = sorted((when + 3 * part, part, n, fn) for part in range(n_parts) for n, (when, fn) in enumerate(stages))
    for _, part, _, fn in plan:
        fn(part)
    return result[0].reshape(b, length, d)
```

```python
import dataclasses
import functools

import jax
import jax.numpy as jnp
from jax import lax
from jax.experimental import pallas as pl
from jax.experimental.pallas import tpu as pltpu
from jax.experimental.pallas import tpu_sc as plsc

F32 = jnp.float32
BF16 = jnp.bfloat16

EPS = 1e-6
GRID_W = 64
ROPE_THETA = 10000.0
ATT_KV_HEADS = 2
TOP_K = 6
ROUTED_SCALE = 2.5
N_MOD = 6
CHUNK = 128
LOG2_E = 1.4426950408889634

LANES = 128
MXU_EDGE = 256
VMEM_LIMIT_BYTES = 56 * 1024 * 1024
SUBLANES = 8
CHOICE_ROWS = SUBLANES
SC_WINDOW = 64
LEAD_PART_SHARE = 0.75
EXPERT_IN_SLOTS = 4
EXPERT_OUT_SLOTS = 2


def _cparams(*sem):
    return pltpu.CompilerParams(dimension_semantics=sem, vmem_limit_bytes=VMEM_LIMIT_BYTES)


def _dot(a, b):
    return jnp.dot(a, b, preferred_element_type=F32)


def _dot_nt(a, b):
    return lax.dot_general(a, b, (((1,), (1,)), ((), ())), preferred_element_type=F32)


def _rms(x, g):
    return x * lax.rsqrt(jnp.mean(x * x, axis=-1, keepdims=True) + EPS) * g


def _group_mean_sq(x, bd_ref):
    x2 = (x * x).astype(BF16)
    parts = [_dot(x2[:, i:i + MXU_EDGE], bd_ref[...]) for i in range(0, x.shape[1], MXU_EDGE)]
    return parts[0] if len(parts) == 1 else jnp.concatenate(parts, axis=-1)


def _dup_halves(x, lane):
    xr = pltpu.roll(x, LANES // 2, axis=1)
    lo = lane < LANES // 2
    return jnp.where(lo, x, xr), jnp.where(lo, xr, x)


def _with_ones(x, lane):
    xr = pltpu.roll(x, LANES // 2, axis=1)
    lo = lane < LANES // 2
    return jnp.concatenate([jnp.where(lo, x, 1.0), jnp.where(lo, xr, 1.0)], axis=-1)


_HI16 = 0xFFFF0000


def _pack_rows(x):
    half = x.shape[1] // 2
    rounded = x.astype(BF16).astype(F32)
    bits = lax.bitcast_convert_type(rounded, jnp.uint32)
    word = (bits[:, :half] >> 16) | (bits[:, half:] & jnp.uint32(_HI16))
    return lax.bitcast_convert_type(word, jnp.int32)


def _unpack_rows(w):
    bits = lax.bitcast_convert_type(w, jnp.uint32)
    lo = lax.bitcast_convert_type(bits << 16, F32)
    hi = lax.bitcast_convert_type(bits & jnp.uint32(_HI16), F32)
    return lo, hi


def _ada_kernel(c_ref, w_ref, b_ref, o_ref):
    s = jax.nn.silu(c_ref[...])
    o_ref[...] = jnp.dot(s, w_ref[...], precision=lax.Precision.HIGHEST,
                         preferred_element_type=F32) + b_ref[...]


def _ada(c_all, w_ada, b_ada):
    rows, d = c_all.shape
    n = w_ada.shape[1]
    bn = 512
    return pl.pallas_call(
        _ada_kernel,
        out_shape=jax.ShapeDtypeStruct((rows, n), F32),
        grid=(n // bn,),
        in_specs=[pl.BlockSpec((rows, d), lambda j: (0, 0)),
                  pl.BlockSpec((d, bn), lambda j: (0, j)),
                  pl.BlockSpec((1, bn), lambda j: (0, j))],
        out_specs=pl.BlockSpec((rows, bn), lambda j: (0, j)),
        compiler_params=_cparams("arbitrary"),
        name="ada",
    )(c_all, w_ada, b_ada)


def _ctx_kv_kernel(x_ref, sh_ref, sc_ref, nmix_ref, w_ref, kg_ref, bd_ref, k_ref, v_ref):
    h = _rms(x_ref[...], nmix_ref[...]) * (1.0 + sc_ref[...]) + sh_ref[...]
    p = _dot(h.astype(BF16), w_ref[...])
    ms = _group_mean_sq(p, bd_ref)
    k = p[:, :LANES] * lax.rsqrt(ms[:, :LANES] + EPS) * kg_ref[...]
    v = p[:, LANES:]
    lane = lax.broadcasted_iota(jnp.int32, k.shape, 1)
    ka, kb = _dup_halves(k, lane)
    k_ref[...] = jnp.concatenate([ka, kb], axis=-1).astype(BF16)
    v_ref[...] = _with_ones(v, lane).astype(BF16)


def _ctx_kv(ctx, mod4, ctx_row, nmix, w_kv, kg, bd):
    b, n_ctx, d = ctx.shape
    kv_w = 2 * LANES
    return pl.pallas_call(
        _ctx_kv_kernel,
        out_shape=(jax.ShapeDtypeStruct((b, n_ctx, kv_w), BF16),) * 2,
        grid=(b,),
        in_specs=[pl.BlockSpec((None, n_ctx, d), lambda i: (i, 0, 0)),
                  pl.BlockSpec((None, None, 1, d), lambda i: (ctx_row, 0, 0, 0)),
                  pl.BlockSpec((None, None, 1, d), lambda i: (ctx_row, 1, 0, 0)),
                  pl.BlockSpec((1, d), lambda i: (0, 0)),
                  pl.BlockSpec(w_kv.shape, lambda i: (0, 0)),
                  pl.BlockSpec((1, LANES), lambda i: (0, 0)),
                  pl.BlockSpec(bd.shape, lambda i: (0, 0))],
        out_specs=(pl.BlockSpec((None, n_ctx, kv_w), lambda i: (i, 0, 0)),) * 2,
        compiler_params=_cparams("arbitrary"),
        name="ctx_kv",
    )(ctx, mod4, mod4, nmix, w_kv, kg, bd)


def _inproj_kernel(x_ref, sh_ref, sc_ref, nmix_ref, w_ref, cos_ref, sin_ref, qg_ref, kg_ref, gmg_ref,
                   bd_ref, ws_ref, bs_ref, og_ref, q_ref, k_ref, v_ref, gm_ref, *, att_w, gm_w):
    tm = x_ref.shape[0]
    h = _rms(x_ref[...], nmix_ref[...]) * (1.0 + sc_ref[...]) + sh_ref[...]
    p = _dot(h.astype(BF16), w_ref[...])

    cos1, sin1 = cos_ref[...], sin_ref[...]
    lane1 = lax.broadcasted_iota(jnp.int32, (tm, LANES), 1)

    def rope(xn, cosw, sinw, lanew):
        w = xn.shape[1]
        fwd = pltpu.roll(xn, w - 32, axis=1)
        bwd = pltpu.roll(xn, 32, axis=1)
        swapped = jnp.where((lanew & 63) < 32, fwd, bwd)
        return xn * cosw + swapped * sinw

    q = p[:, :att_w]
    qn = q * lax.rsqrt(_group_mean_sq(q, bd_ref) + EPS) * qg_ref[...]
    reps = att_w // LANES
    cosq = jnp.concatenate([cos1] * reps, axis=-1)
    sinq = jnp.concatenate([sin1] * reps, axis=-1)
    laneq = lax.broadcasted_iota(jnp.int32, (tm, att_w), 1)
    q_ref[...] = rope(qn, cosq, sinq, laneq).astype(BF16)

    kv = p[:, att_w:att_w + 2 * LANES]
    ms = _group_mean_sq(kv, bd_ref)
    kn = kv[:, :LANES] * lax.rsqrt(ms[:, :LANES] + EPS) * kg_ref[...]
    k = rope(kn, cos1, sin1, lane1)
    v = kv[:, LANES:]
    ka, kb = _dup_halves(k, lane1)
    k_ref[...] = jnp.concatenate([ka, kb], axis=-1).astype(BF16)
    v_ref[...] = _with_ones(v, lane1).astype(BF16)

    u0 = att_w + 2 * LANES
    u = jax.nn.gelu(p[:, u0:u0 + gm_w])
    gt = jax.nn.gelu(p[:, u0 + gm_w:u0 + 2 * gm_w])
    gt = (gt * lax.rsqrt(_group_mean_sq(gt, bd_ref) + EPS) * gmg_ref[...]).astype(BF16)
    lo_half = lax.broadcasted_iota(jnp.int32, (CHUNK, LANES), 1) < LANES // 2
    rows = []
    for c in range(tm // CHUNK):
        blocks = []
        for j in range(gm_w // LANES):
            g = gt[c * CHUNK:(c + 1) * CHUNK, j * LANES:(j + 1) * LANES]
            r = _dot(ws_ref[j], g)
            blocks.append(jnp.where(lo_half, r[:CHUNK], r[CHUNK:]))
        rows.append(jnp.concatenate(blocks, axis=-1) + bs_ref[...])
    mixed = jnp.concatenate(rows, axis=0)
    gm_ref[...] = _rms(u * mixed, og_ref[...]).astype(BF16)


def _inproj(x, mod4, nmix, w_in, cos_t, sin_t, qg, kg, gmg, bd, ws, bs, og, *, att_w, gm_w, tm):
    b, length, d = x.shape
    kv_w = 2 * LANES
    n_in = w_in.shape[1]
    const = lambda shape: pl.BlockSpec(shape, lambda i, j: (0,) * len(shape))
    return pl.pallas_call(
        functools.partial(_inproj_kernel, att_w=att_w, gm_w=gm_w),
        out_shape=(jax.ShapeDtypeStruct((b, length, att_w), BF16),
                   jax.ShapeDtypeStruct((b, length, kv_w), BF16),
                   jax.ShapeDtypeStruct((b, length, kv_w), BF16),
                   jax.ShapeDtypeStruct((b, length, gm_w), BF16)),
        grid=(b, length // tm),
        in_specs=[pl.BlockSpec((None, tm, d), lambda i, j: (i, j, 0)),
                  pl.BlockSpec((None, None, 1, d), lambda i, j: (i, 0, 0, 0)),
                  pl.BlockSpec((None, None, 1, d), lambda i, j: (i, 1, 0, 0)),
                  const((1, d)),
                  const((d, n_in)),
                  pl.BlockSpec((tm, LANES), lambda i, j: (j, 0)),
                  pl.BlockSpec((tm, LANES), lambda i, j: (j, 0)),
                  const((1, att_w)), const((1, LANES)), const((1, gm_w)),
                  const(bd.shape), const(ws.shape), const(bs.shape), const((1, gm_w))],
        out_specs=(pl.BlockSpec((None, tm, att_w), lambda i, j: (i, j, 0)),
                   pl.BlockSpec((None, tm, kv_w), lambda i, j: (i, j, 0)),
                   pl.BlockSpec((None, tm, kv_w), lambda i, j: (i, j, 0)),
                   pl.BlockSpec((None, tm, gm_w), lambda i, j: (i, j, 0))),
        compiler_params=_cparams("arbitrary", "arbitrary"),
        name="inproj",
    )(x, mod4, mod4, nmix, w_in, cos_t, sin_t, qg, kg, gmg, bd, ws, bs, og)


def _attn_kernel(q_ref, kc_ref, kl_ref, vc_ref, vl_ref, o_ref, s_buf, p_buf, *, tq):
    n_ctx = kc_ref.shape[0]
    heads = q_ref.shape[1] // (LANES // 2)
    nj = q_ref.shape[0] // tq
    half = LANES // 2
    lo = lax.broadcasted_iota(jnp.int32, (tq, LANES), 1) < half

    s_buf[...] = jnp.zeros_like(s_buf)
    p_buf[...] = jnp.ones_like(p_buf)

    def scores(j, g):
        q = q_ref[pl.ds(j * tq, tq), (g // 2) * LANES:(g // 2 + 1) * LANES]
        zero = jnp.zeros_like(q)
        qh = jnp.where(lo, q, zero) if g % 2 == 0 else jnp.where(lo, zero, q)
        s_buf[g % 2, :, :n_ctx] = _dot_nt(qh, kc_ref[...])
        s_buf[g % 2, :, n_ctx:] = _dot_nt(qh, kl_ref[...])

    def probs(g):
        s = s_buf[g % 2]
        p_buf[g % 2] = jnp.exp2(s - jnp.max(s, axis=-1, keepdims=True)).astype(BF16)

    def output(j, g):
        p = p_buf[g % 2]
        w = _dot(p[:, :n_ctx], vc_ref[...]) + _dot(p[:, n_ctx:], vl_ref[...])
        t = jnp.where(pl.program_id(1) == 0, w[:, :LANES], w[:, LANES:])
        r = pltpu.roll(t, half, axis=1)
        rows = pl.ds(j * tq, tq)
        if g % 2 == 0:
            o_ref[rows, g * half:(g + 1) * half] = (t / r)[:, :half].astype(o_ref.dtype)
        else:
            o_ref[rows, g * half:(g + 1) * half] = (r / t)[:, half:].astype(o_ref.dtype)

    def trip(j, carry):
        j_prev = jnp.maximum(j - 1, 0)
        for g in range(heads):
            scores(j, g)
            probs((g - 1) % heads)
            output(j if g >= 2 else j_prev, (g - 2) % heads)
        return carry

    lax.fori_loop(0, nj, trip, 0)
    probs(heads - 1)
    output(nj - 1, heads - 2)
    output(nj - 1, heads - 1)


def _unread_operand(in_specs, operands, array):
    in_specs.append(pl.BlockSpec(memory_space=pl.ANY))
    operands.append(array)
    return len(operands) - 1


def _attention(q, k_c, k_l, v_c, v_l, att_so_far, after, *, tq, b0, nb, slot0, slots):
    _, length, att_w = q.shape
    n_ctx = k_c.shape[1]
    group_w = att_w // ATT_KV_HEADS
    assert (group_w // (LANES // 2)) % 2 == 0 and length % tq == 0
    k_blk = lambda n: pl.BlockSpec((None, n, LANES), lambda i, h: (i + b0, 0, h))
    v_blk = lambda n: pl.BlockSpec((None, n, ATT_KV_HEADS * LANES), lambda i, h: (i + b0, 0, 0))
    in_specs = [pl.BlockSpec((None, length, group_w), lambda i, h: (i + b0, 0, h)),
                k_blk(n_ctx), k_blk(length), v_blk(n_ctx), v_blk(length)]
    operands = [q, k_c, k_l, v_c, v_l]
    n_read = len(operands)
    aliases = {}
    if att_so_far is not None:
        aliases = {_unread_operand(in_specs, operands, att_so_far): 0}
    if after is not None:
        _unread_operand(in_specs, operands, after)
    n_in = len(operands)
    return pl.pallas_call(
        lambda *refs: _attn_kernel(*refs[:n_read], *refs[n_in:], tq=tq),
        out_shape=jax.ShapeDtypeStruct((slots, length, att_w), BF16),
        grid=(nb, ATT_KV_HEADS),
        in_specs=in_specs,
        out_specs=pl.BlockSpec((None, length, group_w), lambda i, h: (i + slot0, 0, h)),
        scratch_shapes=[pltpu.VMEM((2, tq, n_ctx + length), F32), pltpu.VMEM((2, tq, n_ctx + length), BF16)],
        input_output_aliases=aliases,
        compiler_params=_cparams("arbitrary", "arbitrary"),
        name="attn",
    )(*operands)


def _post_kernel(att_ref, gm_ref, x_ref, g1_ref, sh2_ref, sc2_ref, g2_ref, oa_ref, wo_ref, nffn_ref,
                 wrt_ref, rb_ref, wsg_ref, wsd_ref,
                 base_ref, h2p_ref, idx_ref, gate_ref, rank_ref, cnt_ref, carry_ref):
    i = pl.program_id(0)
    tm = x_ref.shape[0]
    n_exp = wrt_ref.shape[0]

    @pl.when(i == 0)
    def _():
        carry_ref[...] = jnp.zeros_like(carry_ref)

    att_n = _rms(att_ref[...].astype(F32), oa_ref[...]).astype(BF16)
    y = _dot(jnp.concatenate([att_n, gm_ref[...]], axis=-1), wo_ref[...])
    x_new = x_ref[...] + g1_ref[...] * y
    h2 = _rms(x_new, nffn_ref[...]) * (1.0 + sc2_ref[...]) + sh2_ref[...]
    h2p_ref[...] = _pack_rows(h2)

    h_hi = h2.astype(BF16)
    scores_t = jax.nn.sigmoid(_dot_nt(wrt_ref[...], h_hi))
    reps = tm // LANES
    sel = scores_t + jnp.concatenate([rb_ref[...]] * reps, axis=-1)
    expert = lax.broadcasted_iota(jnp.int32, scores_t.shape, 0).astype(F32)
    neg = jnp.float32(-jnp.inf)

    onehot = jnp.zeros(scores_t.shape, F32)
    hits, idxs, svals = [], [], []
    for _k in range(TOP_K):
        m = jnp.max(sel, axis=0, keepdims=True)
        idx = jnp.min(jnp.where(sel == m, expert, float(n_exp)), axis=0, keepdims=True)
        hit = expert == idx
        hits.append(hit)
        idxs.append(idx)
        svals.append(jnp.sum(jnp.where(hit, scores_t, 0.0), axis=0, keepdims=True))
        onehot = onehot + jnp.where(hit, 1.0, 0.0)
        sel = jnp.where(hit, neg, sel)
    ssum = functools.reduce(lambda a, b: a + b, svals)

    r_io = lax.broadcasted_iota(jnp.int32, (tm, tm), 0)
    c_io = lax.broadcasted_iota(jnp.int32, (tm, tm), 1)
    tri = jnp.where(r_io < c_io, 1.0, 0.0).astype(BF16)
    carry = carry_ref[...]
    before = _dot(onehot.astype(BF16), tri) + jnp.concatenate([carry] * reps, axis=-1)
    carry = carry + jnp.sum(onehot, axis=1, keepdims=True)
    carry_ref[...] = carry
    cnt_ref[...] = carry

    row = lax.broadcasted_iota(jnp.int32, (idx_ref.shape[0], tm), 0)
    idx_o = jnp.zeros(row.shape, F32)
    rank_o = jnp.zeros(row.shape, F32)
    gate_o = jnp.zeros(row.shape, F32)
    for k in range(TOP_K):
        rank = jnp.sum(jnp.where(hits[k], before, 0.0), axis=0, keepdims=True)
        idx_o = jnp.where(row == k, idxs[k], idx_o)
        rank_o = jnp.where(row == k, rank, rank_o)
        gate_o = jnp.where(row == k, svals[k] / ssum * ROUTED_SCALE, gate_o)
    idx_ref[...] = idx_o.astype(jnp.int32)
    rank_ref[...] = rank_o.astype(jnp.int32)
    gate_pad = jnp.concatenate([gate_o, jnp.zeros((LANES - gate_o.shape[0], tm), F32)], axis=0)
    gate_ref[...] = gate_pad.T

    gu = _dot(h_hi, wsg_ref[...])
    hs = gu.shape[1] // 2
    act = (jax.nn.silu(gu[:, :hs]) * gu[:, hs:]).astype(BF16)
    base_ref[...] = x_new + g2_ref[...] * _dot(act, wsd_ref[...])


def _post(att, gmn, x2, mod4, oa, w_out, nffn, wr_t, rb, w_sg, w_sd, *, tm, tokens_per_batch, tok0):
    t, att_w = att.shape
    d = x2.shape[1]
    n_exp = wr_t.shape[0]
    gm_w = gmn.shape[1]
    tpb = tokens_per_batch // tm
    blk0 = tok0 // tm
    const = lambda shape: pl.BlockSpec(shape, lambda i: (0,) * len(shape))
    mrow = lambda j: pl.BlockSpec((None, None, 1, d), lambda i: ((i + blk0) // tpb, j, 0, 0))
    by_choice = jax.ShapeDtypeStruct((CHOICE_ROWS, t), jnp.int32)
    return pl.pallas_call(
        _post_kernel,
        out_shape=(jax.ShapeDtypeStruct((t, d), F32), jax.ShapeDtypeStruct((t, d // 2), jnp.int32),
                   by_choice, jax.ShapeDtypeStruct((t, LANES), F32), by_choice,
                   jax.ShapeDtypeStruct((n_exp, LANES), F32)),
        grid=(t // tm,),
        in_specs=[pl.BlockSpec((tm, att_w), lambda i: (i, 0)),
                  pl.BlockSpec((tm, gm_w), lambda i: (i + blk0, 0)),
                  pl.BlockSpec((tm, d), lambda i: (i + blk0, 0)),
                  mrow(2), mrow(3), mrow(4), mrow(5),
                  const((1, att_w)), const(w_out.shape), const((1, d)),
                  const(wr_t.shape), const(rb.shape),
                  const(w_sg.shape), const(w_sd.shape)],
        out_specs=(pl.BlockSpec((tm, d), lambda i: (i, 0)), pl.BlockSpec((tm, d // 2), lambda i: (i, 0)),
                   pl.BlockSpec((CHOICE_ROWS, tm), lambda i: (0, i)), pl.BlockSpec((tm, LANES), lambda i: (i, 0)),
                   pl.BlockSpec((CHOICE_ROWS, tm), lambda i: (0, i)), pl.BlockSpec((n_exp, LANES), lambda i: (0, 0))),
        scratch_shapes=[pltpu.VMEM((n_exp, LANES), F32)],
        compiler_params=_cparams("arbitrary"),
        name="post",
    )(att, gmn, x2, mod4, mod4, mod4, mod4, oa, w_out, nffn, wr_t, rb, w_sg, w_sd)


def _sc_mesh_info():
    info = plsc.get_sparse_core_info()
    mesh = plsc.VectorSubcoreMesh(core_axis_name="c", subcore_axis_name="s")
    return mesh, info.num_cores, info.num_subcores, info.num_lanes


def _sc_params():
    cp = pltpu.CompilerParams()
    if "needs_layout_passes" in pltpu.CompilerParams.__dataclass_fields__:
        cp = dataclasses.replace(cp, needs_layout_passes=False)
    return cp


def _dispatch(h2p, idx_t, rank_t, offs, *, n_slots):
    t, dw = h2p.shape
    mesh, n_cores, n_sub, n_lanes = _sc_mesh_info()
    per_worker = t // (n_cores * n_sub)
    w = SC_WINDOW
    assert per_worker % w == 0 and w % n_lanes == 0

    @functools.partial(
        pl.kernel, mesh=mesh,
        out_type=(jax.ShapeDtypeStruct((n_slots, dw), jnp.int32), jax.ShapeDtypeStruct((CHOICE_ROWS, t), jnp.int32)),
        scratch_types=[pltpu.VMEM(offs.shape, jnp.int32)] + [pltpu.VMEM((TOP_K, w), jnp.int32)] * 3
        + [pltpu.VMEM((w, dw), jnp.int32), pltpu.SemaphoreType.DMA],
        compiler_params=_sc_params(),
    )
    def run(h_hbm, idx_hbm, rank_hbm, offs_hbm, xs_hbm, pos_hbm, offs_v, idx_v, rank_v, pos_v, rows_v, sem):
        base = (lax.axis_index("s") * n_cores + lax.axis_index("c")) * per_worker
        pltpu.sync_copy(offs_hbm, offs_v)

        @pl.loop(0, per_worker // w)
        def _(i):
            t0 = base + i * w
            pltpu.sync_copy(h_hbm.at[pl.ds(t0, w)], rows_v)
            for k in range(TOP_K):
                pltpu.sync_copy(idx_hbm.at[k, pl.ds(t0, w)], idx_v.at[k])
                pltpu.sync_copy(rank_hbm.at[k, pl.ds(t0, w)], rank_v.at[k])
            for k in range(TOP_K):
                for j in range(0, w, n_lanes):
                    group_start = plsc.load_gather(offs_v, [idx_v[k, pl.ds(j, n_lanes)]])
                    pos_v[k, pl.ds(j, n_lanes)] = group_start + rank_v[k, pl.ds(j, n_lanes)]
            copies = [pltpu.async_copy(rows_v, xs_hbm.at[pos_v.at[k]], sem) for k in range(TOP_K)]
            for k in range(TOP_K):
                pltpu.sync_copy(pos_v.at[k], pos_hbm.at[k, pl.ds(t0, w)])
            for cp in copies:
                cp.wait()

    return run(h2p, idx_t, rank_t, offs)


def _collect(ys, pos_t, *, tok_lo, n_tok):
    dw = ys.shape[1]
    mesh, n_cores, n_sub, _ = _sc_mesh_info()
    per_worker = n_tok // (n_cores * n_sub)
    w = SC_WINDOW
    assert per_worker % w == 0 and TOP_K % 2 == 0

    @functools.partial(
        pl.kernel, mesh=mesh,
        out_type=jax.ShapeDtypeStruct((TOP_K * n_tok, dw), jnp.int32),
        scratch_types=[pltpu.VMEM((TOP_K, w), jnp.int32), pltpu.VMEM((w, dw), jnp.int32),
                       pltpu.VMEM((w, dw), jnp.int32), pltpu.SemaphoreType.DMA, pltpu.SemaphoreType.DMA],
        compiler_params=_sc_params(),
    )
    def run(ys_hbm, pos_hbm, r_hbm, pos_v, rows_a, rows_b, sem_g, sem_w):
        base = (lax.axis_index("s") * n_cores + lax.axis_index("c")) * per_worker
        bufs = (rows_a, rows_b)

        @pl.loop(0, per_worker // w)
        def _(i):
            t0 = base + i * w
            for k in range(TOP_K):
                pltpu.sync_copy(pos_hbm.at[k, pl.ds(tok_lo + t0, w)], pos_v.at[k])
            gathers = [None] * TOP_K
            writes = [None] * TOP_K
            gathers[0] = pltpu.async_copy(ys_hbm.at[pos_v.at[0]], bufs[0], sem_g)
            for k in range(TOP_K):
                gathers[k].wait()
                if k + 1 < TOP_K:
                    if k >= 1:
                        writes[k - 1].wait()
                    gathers[k + 1] = pltpu.async_copy(ys_hbm.at[pos_v.at[k + 1]], bufs[(k + 1) % 2], sem_g)
                writes[k] = pltpu.async_copy(bufs[k % 2], r_hbm.at[pl.ds(k * n_tok + t0, w)], sem_w)
            writes[TOP_K - 2].wait()
            writes[TOP_K - 1].wait()

    return run(ys, pos_t)


def _experts_kernel(t0_ref, t1_ref, xs_hbm, wgu_ref, wdn_ref, ys_hbm, xbuf, ybuf, wgu_b, wdn_b, sem_in, sem_out):
    e = pl.program_id(0)
    n_exp = pl.num_programs(0)
    n_in, tr, half = xbuf.shape
    n_out = ybuf.shape[0]
    n_tiles = t1_ref[n_exp - 1]

    def fetch(t):
        slot = t & (n_in - 1)
        return pltpu.make_async_copy(xs_hbm.at[pl.ds(t * tr, tr)], xbuf.at[slot], sem_in.at[slot])

    def flush(t):
        slot = t & (n_out - 1)
        return pltpu.make_async_copy(ybuf.at[slot], ys_hbm.at[pl.ds(t * tr, tr)], sem_out.at[slot])

    @pl.when(e == 0)
    def _():
        for t in range(n_in - 1):
            @pl.when(t < n_tiles)
            def _():
                fetch(t).start()

    wgu_b[...] = wgu_ref[...].astype(BF16)
    wdn_b[...] = wdn_ref[...].astype(BF16)

    def tile(t, carry):
        fetch(t).wait()

        @pl.when(t + n_in - 1 < n_tiles)
        def _():
            fetch(t + n_in - 1).start()

        lo, hi = _unpack_rows(xbuf[t & (n_in - 1)])
        gu = _dot(lo.astype(BF16), wgu_b[:half, :]) + _dot(hi.astype(BF16), wgu_b[half:, :])
        hh = gu.shape[1] // 2
        act = (jax.nn.silu(gu[:, :hh]) * gu[:, hh:]).astype(BF16)
        y = _pack_rows(_dot(act, wdn_b[...]))

        @pl.when(t >= n_out)
        def _():
            flush(t - n_out).wait()

        ybuf[t & (n_out - 1)] = y
        flush(t).start()
        return carry

    lax.fori_loop(t0_ref[e], t1_ref[e], tile, 0)

    @pl.when(e == n_exp - 1)
    def _():
        for back in range(n_out, 0, -1):
            @pl.when(n_tiles >= back)
            def _():
                flush(n_tiles - back).wait()


def _experts(tile_lo, tile_hi, xs, w_gu, w_dn, after, *, tr):
    n_exp, d, h2w = w_gu.shape
    hh = w_dn.shape[1]
    in_specs = [pl.BlockSpec(memory_space=pl.ANY),
                pl.BlockSpec((None, d, h2w), lambda e, lo, hi: (e, 0, 0)),
                pl.BlockSpec((None, hh, d), lambda e, lo, hi: (e, 0, 0))]
    operands = [xs, w_gu, w_dn]
    n_read = 2 + len(operands)
    if after is not None:
        _unread_operand(in_specs, operands, after)
    n_in = 2 + len(operands)
    return pl.pallas_call(
        lambda *refs: _experts_kernel(*refs[:n_read], *refs[n_in:]),
        out_shape=jax.ShapeDtypeStruct(xs.shape, jnp.int32),
        grid_spec=pltpu.PrefetchScalarGridSpec(
            num_scalar_prefetch=2,
            grid=(n_exp,),
            in_specs=in_specs,
            out_specs=pl.BlockSpec(memory_space=pl.ANY),
            scratch_shapes=[pltpu.VMEM((EXPERT_IN_SLOTS, tr, d // 2), jnp.int32),
                            pltpu.VMEM((EXPERT_OUT_SLOTS, tr, d // 2), jnp.int32),
                            pltpu.VMEM((d, h2w), BF16), pltpu.VMEM((hh, d), BF16),
                            pltpu.SemaphoreType.DMA((EXPERT_IN_SLOTS,)), pltpu.SemaphoreType.DMA((EXPERT_OUT_SLOTS,))]),
        compiler_params=_cparams("arbitrary"),
        name="experts",
    )(tile_lo, tile_hi, *operands)


def _combine_kernel(base_ref, g2_ref, gate_ref, *refs):
    r_refs, o_ref = refs[:TOP_K], refs[TOP_K]
    gate = gate_ref[...]
    half = base_ref.shape[1] // 2
    acc_lo = jnp.zeros((base_ref.shape[0], half), F32)
    acc_hi = jnp.zeros((base_ref.shape[0], half), F32)
    for k in range(TOP_K):
        lo, hi = _unpack_rows(r_refs[k][...])
        g = gate[:, k:k + 1]
        acc_lo = acc_lo + g * lo
        acc_hi = acc_hi + g * hi
    g2 = g2_ref[...]
    o_ref[:, :half] = base_ref[:, :half] + g2[:, :half] * acc_lo
    o_ref[:, half:] = base_ref[:, half:] + g2[:, half:] * acc_hi


def _combine(base, mod4, gate, r, out_so_far, after, *, tm, tokens_per_batch, tok0, part_lo, total_tokens):
    d = base.shape[1]
    t = r.shape[0] // TOP_K
    tpb = tokens_per_batch // tm
    nblk = t // tm
    pblk0 = part_lo // tm
    blk0 = (tok0 + part_lo) // tm
    r_spec = lambda k: pl.BlockSpec((tm, d // 2), lambda i: (k * nblk + i, 0))
    in_specs = [pl.BlockSpec((tm, d), lambda i: (i + pblk0, 0)),
                pl.BlockSpec((None, None, 1, d), lambda i: ((i + blk0) // tpb, 5, 0, 0)),
                pl.BlockSpec((tm, LANES), lambda i: (i + pblk0, 0))] + [r_spec(k) for k in range(TOP_K)]
    operands = [base, mod4, gate] + [r] * TOP_K
    n_read = len(operands)
    aliases = {}
    if out_so_far is not None:
        aliases = {_unread_operand(in_specs, operands, out_so_far): 0}
    if after is not None:
        _unread_operand(in_specs, operands, after)
    n_in = len(operands)
    return pl.pallas_call(
        lambda *refs: _combine_kernel(*refs[:n_read], *refs[n_in:]),
        out_shape=jax.ShapeDtypeStruct((total_tokens, d), F32),
        grid=(nblk,),
        in_specs=in_specs,
        out_specs=pl.BlockSpec((tm, d), lambda i: (i + blk0, 0)),
        input_output_aliases=aliases,
        compiler_params=_cparams("arbitrary"),
        name="combine",
    )(*operands)


def _rope_tables(length, head_dim):
    rows = length // GRID_W
    pairs_axis = head_dim // 4
    r = jnp.repeat(jnp.arange(rows, dtype=F32), GRID_W)
    col = jnp.tile(jnp.arange(GRID_W, dtype=F32), rows)
    inv = ROPE_THETA ** (-jnp.arange(pairs_axis, dtype=F32) / pairs_axis)
    ang = jnp.concatenate([r[:, None] * inv, col[:, None] * inv], axis=-1)
    cos, sin = jnp.cos(ang), jnp.sin(ang)
    cos_h = jnp.concatenate([cos, cos], axis=-1)
    sin_h = jnp.concatenate([-sin, sin], axis=-1)
    reps = LANES // head_dim
    return jnp.tile(cos_h, (1, reps)), jnp.tile(sin_h, (1, reps))


def _part_batches(b):
    if b < 2:
        return [b]
    lead = max(1, min(b - 1, round(LEAD_PART_SHARE * b)))
    return [lead, b - lead]


def kernel(x, c, ctx, c_ctx, w_ada, b_ada, norm_mix, w_in, q_norm, k_norm, gm_norm, w_spatial, b_spatial,
           out_norm_attn, out_norm_gm, w_out, norm_ffn, w_router, router_bias, w_exp_gu, w_exp_down,
           w_sh_gu, w_sh_down):
    assert w_ada.shape[0] == 1, "single-layer kernel"
    b, length, d = x.shape
    head_dim = q_norm.shape[-1]
    att_w = out_norm_attn.shape[-1]
    gm_w = out_norm_gm.shape[-1]
    gm_groups = w_spatial.shape[1]
    gm_hd = gm_w // gm_groups
    n_exp = w_router.shape[-1]
    t = b * length
    assert head_dim == LANES // 2 and gm_hd == LANES // 2 and w_spatial.shape[-1] == CHUNK
    assert n_exp <= LANES and att_w % MXU_EDGE == 0 and gm_w % MXU_EDGE == 0

    mod_rows = 16
    c_all = jnp.zeros((mod_rows, d), F32).at[:b].set(c).at[b].set(c_ctx)
    mod = _ada(c_all, w_ada[0], b_ada)
    mod4 = mod.reshape(mod_rows, N_MOD, 1, d)

    w_in_b = w_in[0].astype(BF16)
    bd = (jnp.arange(MXU_EDGE)[:, None] // head_dim == jnp.arange(MXU_EDGE)[None, :] // head_dim)
    bd = (bd.astype(F32) / head_dim).astype(BF16)
    qg = jnp.tile(q_norm[0] * (head_dim ** -0.5 * LOG2_E), att_w // head_dim)[None, :]
    kg = jnp.tile(k_norm[0], LANES // head_dim)[None, :]
    gmg = jnp.tile(gm_norm[0], gm_groups)[None, :]
    ws = w_spatial[0].astype(BF16).reshape(gm_groups // 2, 2 * CHUNK, CHUNK)
    bs = jnp.repeat(b_spatial[0].T, gm_hd, axis=1)
    cos_t, sin_t = _rope_tables(length, head_dim)

    k_c, v_c = _ctx_kv(ctx, mod4, b, norm_mix, w_in_b[:, att_w:att_w + 2 * LANES], kg, bd)
    q, k_l, v_l, gmn = _inproj(x, mod4, norm_mix, w_in_b, cos_t, sin_t, qg, kg, gmg, bd, ws, bs,
                               out_norm_gm, att_w=att_w, gm_w=gm_w, tm=min(512, length))
    wr_t = w_router[0].T.astype(BF16)
    rb = jnp.tile(router_bias[0][:, None], (1, LANES))
    w_out_b, w_sg_b, w_sd_b = w_out[0].astype(BF16), w_sh_gu[0].astype(BF16), w_sh_down[0].astype(BF16)
    gmn2, x2 = gmn.reshape(t, gm_w), x.reshape(t, d)
    tm_post = min(512, length)
    tr = 512

    part_nb = _part_batches(b)
    n_parts = len(part_nb)
    part_b0 = [sum(part_nb[:p]) for p in range(n_parts)]
    state = [dict() for _ in range(n_parts)]

    last_tc = [None]

    def halves(part):
        nb = part_nb[part]
        return (nb // 2, nb - nb // 2) if nb >= 2 else (nb,)

    def attend(part, half):
        if half >= len(halves(part)):
            return
        st = state[part]
        slot0 = sum(halves(part)[:half])
        after = None if last_tc[0] is st.get("att") else last_tc[0]
        st["att"] = _attention(q, k_c, k_l, v_c, v_l, st.get("att"), after, tq=min(256, length),
                               b0=part_b0[part] + slot0, nb=halves(part)[half], slot0=slot0, slots=part_nb[part])
        last_tc[0] = st["att"]

    def route(part):
        st = state[part]
        tp = part_nb[part] * length
        nt = tp * TOP_K // tr + n_exp
        base, h2p, idx_t, gate, rank_t, cnt = _post(
            st["att"].reshape(tp, att_w), gmn2, x2, mod4, out_norm_attn, w_out_b, norm_ffn, wr_t, rb, w_sg_b, w_sd_b,
            tm=tm_post, tokens_per_batch=length, tok0=part_b0[part] * length)
        counts = cnt[:, 0].astype(jnp.int32)
        padded = (counts + tr - 1) // tr * tr
        ends = jnp.cumsum(padded)
        offs = ends - padded
        offs_pad = jnp.zeros((LANES,), jnp.int32).at[:n_exp].set(offs)
        xs, pos_t = _dispatch(h2p, idx_t, rank_t, offs_pad, n_slots=nt * tr)
        st.update(base=base, gate=gate, xs=xs, pos_t=pos_t, tile_lo=offs // tr, tile_hi=ends // tr)
        last_tc[0] = base

    def run_experts(part):
        st = state[part]
        ys = _experts(st["tile_lo"], st["tile_hi"], st["xs"], w_exp_gu[0], w_exp_down[0], last_tc[0], tr=tr)
        n_chunks = part_nb[part] if part == n_parts - 1 else 1
        chunk = part_nb[part] * length // n_chunks
        st["r"] = [(c * chunk, _collect(ys, st["pos_t"], tok_lo=c * chunk, n_tok=chunk)) for c in range(n_chunks)]
        last_tc[0] = ys

    result = [None]

    def finish(part):
        st = state[part]
        for part_lo, r in st["r"]:
            after = None if last_tc[0] is result[0] or last_tc[0] is st["base"] else last_tc[0]
            result[0] = _combine(st["base"], mod4, st["gate"], r, result[0], after, tm=min(256, length),
                                 tokens_per_batch=length, tok0=part_b0[part] * length, part_lo=part_lo,
                                 total_tokens=t)
            last_tc[0] = result[0]

    stages = [(0, lambda p: attend(p, 0)), (1, lambda p: attend(p, 1)), (2, route), (4, run_experts), (7, finish)]
    plan = sorted((when + 3 * part, part, n, fn) for part in range(n_parts) for n, (when, fn) in enumerate(stages))
    for _, part, _, fn in plan:
        fn(part)
    return result[0].reshape(b, length, d)
```

```python
import dataclasses
import functools

import jax
import jax.numpy as jnp
from jax import lax
from jax.experimental import pallas as pl
from jax.experimental.pallas import tpu as pltpu
from jax.experimental.pallas import tpu_sc as plsc

F32 = jnp.float32
BF16 = jnp.bfloat16

EPS = 1e-6
GRID_W = 64
ROPE_THETA = 10000.0
ATT_KV_HEADS = 2
TOP_K = 6
ROUTED_SCALE = 2.5
N_MOD = 6
CHUNK = 128
LOG2_E = 1.4426950408889634

LANES = 128
MXU_EDGE = 256
VMEM_LIMIT_BYTES = 56 * 1024 * 1024
SUBLANES = 8
CHOICE_ROWS = SUBLANES
SC_WINDOW = 64
LEAD_PART_SHARE = 0.75
EXPERT_IN_SLOTS = 4
EXPERT_OUT_SLOTS = 2


def _cparams(*sem):
    return pltpu.CompilerParams(dimension_semantics=sem, vmem_limit_bytes=VMEM_LIMIT_BYTES)


def _dot(a, b):
    return jnp.dot(a, b, preferred_element_type=F32)


def _dot_nt(a, b):
    return lax.dot_general(a, b, (((1,), (1,)), ((), ())), preferred_element_type=F32)


def _rms(x, g):
    return x * lax.rsqrt(jnp.mean(x * x, axis=-1, keepdims=True) + EPS) * g


def _group_mean_sq(x, bd_ref):
    x2 = (x * x).astype(BF16)
    parts = [_dot(x2[:, i:i + MXU_EDGE], bd_ref[...]) for i in range(0, x.shape[1], MXU_EDGE)]
    return parts[0] if len(parts) == 1 else jnp.concatenate(parts, axis=-1)


def _dup_halves(x, lane):
    xr = pltpu.roll(x, LANES // 2, axis=1)
    lo = lane < LANES // 2
    return jnp.where(lo, x, xr), jnp.where(lo, xr, x)


def _with_ones(x, lane):
    xr = pltpu.roll(x, LANES // 2, axis=1)
    lo = lane < LANES // 2
    return jnp.concatenate([jnp.where(lo, x, 1.0), jnp.where(lo, xr, 1.0)], axis=-1)


_HI16 = 0xFFFF0000


def _pack_rows(x):
    half = x.shape[1] // 2
    rounded = x.astype(BF16).astype(F32)
    bits = lax.bitcast_convert_type(rounded, jnp.uint32)
    word = (bits[:, :half] >> 16) | (bits[:, half:] & jnp.uint32(_HI16))
    return lax.bitcast_convert_type(word, jnp.int32)


def _unpack_rows(w):
    bits = lax.bitcast_convert_type(w, jnp.uint32)
    lo = lax.bitcast_convert_type(bits << 16, F32)
    hi = lax.bitcast_convert_type(bits & jnp.uint32(_HI16), F32)
    return lo, hi


def _ada_kernel(c_ref, w_ref, b_ref, o_ref):
    s = jax.nn.silu(c_ref[...])
    o_ref[...] = jnp.dot(s, w_ref[...], precision=lax.Precision.HIGHEST,
                         preferred_element_type=F32) + b_ref[...]


def _ada(c_all, w_ada, b_ada):
    rows, d = c_all.shape
    n = w_ada.shape[1]
    bn = 512
    return pl.pallas_call(
        _ada_kernel,
        out_shape=jax.ShapeDtypeStruct((rows, n), F32),
        grid=(n // bn,),
        in_specs=[pl.BlockSpec((rows, d), lambda j: (0, 0)),
                  pl.BlockSpec((d, bn), lambda j: (0, j)),
                  pl.BlockSpec((1, bn), lambda j: (0, j))],
        out_specs=pl.BlockSpec((rows, bn), lambda j: (0, j)),
        compiler_params=_cparams("arbitrary"),
        name="ada",
    )(c_all, w_ada, b_ada)


def _ctx_kv_kernel(x_ref, sh_ref, sc_ref, nmix_ref, w_ref, kg_ref, bd_ref, k_ref, v_ref):
    h = _rms(x_ref[...], nmix_ref[...]) * (1.0 + sc_ref[...]) + sh_ref[...]
    p = _dot(h.astype(BF16), w_ref[...])
    ms = _group_mean_sq(p, bd_ref)
    k = p[:, :LANES] * lax.rsqrt(ms[:, :LANES] + EPS) * kg_ref[...]
    v = p[:, LANES:]
    lane = lax.broadcasted_iota(jnp.int32, k.shape, 1)
    ka, kb = _dup_halves(k, lane)
    k_ref[...] = jnp.concatenate([ka, kb], axis=-1).astype(BF16)
    v_ref[...] = _with_ones(v, lane).astype(BF16)


def _ctx_kv(ctx, mod4, ctx_row, nmix, w_kv, kg, bd):
    b, n_ctx, d = ctx.shape
    kv_w = 2 * LANES
    return pl.pallas_call(
        _ctx_kv_kernel,
        out_shape=(jax.ShapeDtypeStruct((b, n_ctx, kv_w), BF16),) * 2,
        grid=(b,),
        in_specs=[pl.BlockSpec((None, n_ctx, d), lambda i: (i, 0, 0)),
                  pl.BlockSpec((None, None, 1, d), lambda i: (ctx_row, 0, 0, 0)),
                  pl.BlockSpec((None, None, 1, d), lambda i: (ctx_row, 1, 0, 0)),
                  pl.BlockSpec((1, d), lambda i: (0, 0)),
                  pl.BlockSpec(w_kv.shape, lambda i: (0, 0)),
                  pl.BlockSpec((1, LANES), lambda i: (0, 0)),
                  pl.BlockSpec(bd.shape, lambda i: (0, 0))],
        out_specs=(pl.BlockSpec((None, n_ctx, kv_w), lambda i: (i, 0, 0)),) * 2,
        compiler_params=_cparams("arbitrary"),
        name="ctx_kv",
    )(ctx, mod4, mod4, nmix, w_kv, kg, bd)


def _inproj_kernel(x_ref, sh_ref, sc_ref, nmix_ref, w_ref, cos_ref, sin_ref, qg_ref, kg_ref, gmg_ref,
                   bd_ref, ws_ref, bs_ref, og_ref, q_ref, k_ref, v_ref, gm_ref, *, att_w, gm_w):
    tm = x_ref.shape[0]
    h = _rms(x_ref[...], nmix_ref[...]) * (1.0 + sc_ref[...]) + sh_ref[...]
    p = _dot(h.astype(BF16), w_ref[...])

    cos1, sin1 = cos_ref[...], sin_ref[...]
    lane1 = lax.broadcasted_iota(jnp.int32, (tm, LANES), 1)

    def rope(xn, cosw, sinw, lanew):
        w = xn.shape[1]
        fwd = pltpu.roll(xn, w - 32, axis=1)
        bwd = pltpu.roll(xn, 32, axis=1)
        swapped = jnp.where((lanew & 63) < 32, fwd, bwd)
        return xn * cosw + swapped * sinw

    q = p[:, :att_w]
    qn = q * lax.rsqrt(_group_mean_sq(q, bd_ref) + EPS) * qg_ref[...]
    reps = att_w // LANES
    cosq = jnp.concatenate([cos1] * reps, axis=-1)
    sinq = jnp.concatenate([sin1] * reps, axis=-1)
    laneq = lax.broadcasted_iota(jnp.int32, (tm, att_w), 1)
    q_ref[...] = rope(qn, cosq, sinq, laneq).astype(BF16)

    kv = p[:, att_w:att_w + 2 * LANES]
    ms = _group_mean_sq(kv, bd_ref)
    kn = kv[:, :LANES] * lax.rsqrt(ms[:, :LANES] + EPS) * kg_ref[...]
    k = rope(kn, cos1, sin1, lane1)
    v = kv[:, LANES:]
    ka, kb = _dup_halves(k, lane1)
    k_ref[...] = jnp.concatenate([ka, kb], axis=-1).astype(BF16)
    v_ref[...] = _with_ones(v, lane1).astype(BF16)

    u0 = att_w + 2 * LANES
    u = jax.nn.gelu(p[:, u0:u0 + gm_w])
    gt = jax.nn.gelu(p[:, u0 + gm_w:u0 + 2 * gm_w])
    gt = (gt * lax.rsqrt(_group_mean_sq(gt, bd_ref) + EPS) * gmg_ref[...]).astype(BF16)
    lo_half = lax.broadcasted_iota(jnp.int32, (CHUNK, LANES), 1) < LANES // 2
    rows = []
    for c in range(tm // CHUNK):
        blocks = []
        for j in range(gm_w // LANES):
            g = gt[c * CHUNK:(c + 1) * CHUNK, j * LANES:(j + 1) * LANES]
            r = _dot(ws_ref[j], g)
            blocks.append(jnp.where(lo_half, r[:CHUNK], r[CHUNK:]))
        rows.append(jnp.concatenate(blocks, axis=-1) + bs_ref[...])
    mixed = jnp.concatenate(rows, axis=0)
    gm_ref[...] = _rms(u * mixed, og_ref[...]).astype(BF16)


def _inproj(x, mod4, nmix, w_in, cos_t, sin_t, qg, kg, gmg, bd, ws, bs, og, *, att_w, gm_w, tm):
    b, length, d = x.shape
    kv_w = 2 * LANES
    n_in = w_in.shape[1]
    const = lambda shape: pl.BlockSpec(shape, lambda i, j: (0,) * len(shape))
    return pl.pallas_call(
        functools.partial(_inproj_kernel, att_w=att_w, gm_w=gm_w),
        out_shape=(jax.ShapeDtypeStruct((b, length, att_w), BF16),
                   jax.ShapeDtypeStruct((b, length, kv_w), BF16),
                   jax.ShapeDtypeStruct((b, length, kv_w), BF16),
                   jax.ShapeDtypeStruct((b, length, gm_w), BF16)),
        grid=(b, length // tm),
        in_specs=[pl.BlockSpec((None, tm, d), lambda i, j: (i, j, 0)),
                  pl.BlockSpec((None, None, 1, d), lambda i, j: (i, 0, 0, 0)),
                  pl.BlockSpec((None, None, 1, d), lambda i, j: (i, 1, 0, 0)),
                  const((1, d)),
                  const((d, n_in)),
                  pl.BlockSpec((tm, LANES), lambda i, j: (j, 0)),
                  pl.BlockSpec((tm, LANES), lambda i, j: (j, 0)),
                  const((1, att_w)), const((1, LANES)), const((1, gm_w)),
                  const(bd.shape), const(ws.shape), const(bs.shape), const((1, gm_w))],
        out_specs=(pl.BlockSpec((None, tm, att_w), lambda i, j: (i, j, 0)),
                   pl.BlockSpec((None, tm, kv_w), lambda i, j: (i, j, 0)),
                   pl.BlockSpec((None, tm, kv_w), lambda i, j: (i, j, 0)),
                   pl.BlockSpec((None, tm, gm_w), lambda i, j: (i, j, 0))),
        compiler_params=_cparams("arbitrary", "arbitrary"),
        name="inproj",
    )(x, mod4, mod4, nmix, w_in, cos_t, sin_t, qg, kg, gmg, bd, ws, bs, og)


def _attn_kernel(q_ref, kc_ref, kl_ref, vc_ref, vl_ref, o_ref, s_buf, p_buf, *, tq):
    n_ctx = kc_ref.shape[0]
    heads = q_ref.shape[1] // (LANES // 2)
    nj = q_ref.shape[0] // tq
    half = LANES // 2
    lo = lax.broadcasted_iota(jnp.int32, (tq, LANES), 1) < half

    @pl.when(jnp.logical_and(pl.program_id(0) == 0, pl.program_id(1) == 0))
    def _():
        s_buf[...] = jnp.zeros_like(s_buf)
        p_buf[...] = jnp.ones_like(p_buf)

    def scores(j, g):
        q = q_ref[pl.ds(j * tq, tq), (g // 2) * LANES:(g // 2 + 1) * LANES]
        zero = jnp.zeros_like(q)
        qh = jnp.where(lo, q, zero) if g % 2 == 0 else jnp.where(lo, zero, q)
        s_buf[g % 2, :, :n_ctx] = _dot_nt(qh, kc_ref[...])
        s_buf[g % 2, :, n_ctx:] = _dot_nt(qh, kl_ref[...])

    def probs(g):
        s = s_buf[g % 2]
        p_buf[g % 2] = jnp.exp2(s - jnp.max(s, axis=-1, keepdims=True)).astype(BF16)

    def output(j, g):
        p = p_buf[g % 2]
        w = _dot(p[:, :n_ctx], vc_ref[...]) + _dot(p[:, n_ctx:], vl_ref[...])
        t = jnp.where(pl.program_id(1) == 0, w[:, :LANES], w[:, LANES:])
        r = pltpu.roll(t, half, axis=1)
        rows = pl.ds(j * tq, tq)
        if g % 2 == 0:
            o_ref[rows, g * half:(g + 1) * half] = (t / r)[:, :half].astype(o_ref.dtype)
        else:
            o_ref[rows, g * half:(g + 1) * half] = (r / t)[:, half:].astype(o_ref.dtype)

    def trip(j, carry):
        j_prev = jnp.maximum(j - 1, 0)
        for g in range(heads):
            scores(j, g)
            probs((g - 1) % heads)
            output(j if g >= 2 else j_prev, (g - 2) % heads)
        return carry

    lax.fori_loop(0, nj, trip, 0)
    probs(heads - 1)
    output(nj - 1, heads - 2)
    output(nj - 1, heads - 1)


def _unread_operand(in_specs, operands, array):
    in_specs.append(pl.BlockSpec(memory_space=pl.ANY))
    operands.append(array)
    return len(operands) - 1


def _attention(q, k_c, k_l, v_c, v_l, att_so_far, after, *, tq, b0, nb, slot0, slots):
    _, length, att_w = q.shape
    n_ctx = k_c.shape[1]
    group_w = att_w // ATT_KV_HEADS
    assert (group_w // (LANES // 2)) % 2 == 0 and length % tq == 0
    k_blk = lambda n: pl.BlockSpec((None, n, LANES), lambda i, h: (i + b0, 0, h))
    v_blk = lambda n: pl.BlockSpec((None, n, ATT_KV_HEADS * LANES), lambda i, h: (i + b0, 0, 0))
    in_specs = [pl.BlockSpec((None, length, group_w), lambda i, h: (i + b0, 0, h)),
                k_blk(n_ctx), k_blk(length), v_blk(n_ctx), v_blk(length)]
    operands = [q, k_c, k_l, v_c, v_l]
    n_read = len(operands)
    aliases = {}
    if att_so_far is not None:
        aliases = {_unread_operand(in_specs, operands, att_so_far): 0}
    if after is not None:
        _unread_operand(in_specs, operands, after)
    n_in = len(operands)
    return pl.pallas_call(
        lambda *refs: _attn_kernel(*refs[:n_read], *refs[n_in:], tq=tq),
        out_shape=jax.ShapeDtypeStruct((slots, length, att_w), BF16),
        grid=(nb, ATT_KV_HEADS),
        in_specs=in_specs,
        out_specs=pl.BlockSpec((None, length, group_w), lambda i, h: (i + slot0, 0, h)),
        scratch_shapes=[pltpu.VMEM((2, tq, n_ctx + length), F32), pltpu.VMEM((2, tq, n_ctx + length), BF16)],
        input_output_aliases=aliases,
        compiler_params=_cparams("arbitrary", "arbitrary"),
        name="attn",
    )(*operands)


def _post_kernel(att_ref, gm_ref, x_ref, g1_ref, sh2_ref, sc2_ref, g2_ref, oa_ref, wo_ref, nffn_ref,
                 wrt_ref, rb_ref, wsg_ref, wsd_ref,
                 base_ref, h2p_ref, idx_ref, gate_ref, rank_ref, cnt_ref, carry_ref):
    i = pl.program_id(0)
    tm = x_ref.shape[0]
    n_exp = wrt_ref.shape[0]

    @pl.when(i == 0)
    def _():
        carry_ref[...] = jnp.zeros_like(carry_ref)

    att_n = _rms(att_ref[...].astype(F32), oa_ref[...]).astype(BF16)
    y = _dot(jnp.concatenate([att_n, gm_ref[...]], axis=-1), wo_ref[...])
    x_new = x_ref[...] + g1_ref[...] * y
    h2 = _rms(x_new, nffn_ref[...]) * (1.0 + sc2_ref[...]) + sh2_ref[...]
    h2p_ref[...] = _pack_rows(h2)

    h_hi = h2.astype(BF16)
    scores_t = jax.nn.sigmoid(_dot_nt(wrt_ref[...], h_hi))
    reps = tm // LANES
    sel = scores_t + jnp.concatenate([rb_ref[...]] * reps, axis=-1)
    expert = lax.broadcasted_iota(jnp.int32, scores_t.shape, 0).astype(F32)
    neg = jnp.float32(-jnp.inf)

    onehot = jnp.zeros(scores_t.shape, F32)
    hits, idxs, svals = [], [], []
    for _k in range(TOP_K):
        m = jnp.max(sel, axis=0, keepdims=True)
        idx = jnp.min(jnp.where(sel == m, expert, float(n_exp)), axis=0, keepdims=True)
        hit = expert == idx
        hits.append(hit)
        idxs.append(idx)
        svals.append(jnp.sum(jnp.where(hit, scores_t, 0.0), axis=0, keepdims=True))
        onehot = onehot + jnp.where(hit, 1.0, 0.0)
        sel = jnp.where(hit, neg, sel)
    ssum = functools.reduce(lambda a, b: a + b, svals)

    r_io = lax.broadcasted_iota(jnp.int32, (tm, tm), 0)
    c_io = lax.broadcasted_iota(jnp.int32, (tm, tm), 1)
    tri = jnp.where(r_io < c_io, 1.0, 0.0).astype(BF16)
    carry = carry_ref[...]
    before = _dot(onehot.astype(BF16), tri) + jnp.concatenate([carry] * reps, axis=-1)
    carry = carry + jnp.sum(onehot, axis=1, keepdims=True)
    carry_ref[...] = carry
    cnt_ref[...] = carry

    row = lax.broadcasted_iota(jnp.int32, (idx_ref.shape[0], tm), 0)
    idx_o = jnp.zeros(row.shape, F32)
    rank_o = jnp.zeros(row.shape, F32)
    gate_o = jnp.zeros(row.shape, F32)
    for k in range(TOP_K):
        rank = jnp.sum(jnp.where(hits[k], before, 0.0), axis=0, keepdims=True)
        idx_o = jnp.where(row == k, idxs[k], idx_o)
        rank_o = jnp.where(row == k, rank, rank_o)
        gate_o = jnp.where(row == k, svals[k] / ssum * ROUTED_SCALE, gate_o)
    idx_ref[...] = idx_o.astype(jnp.int32)
    rank_ref[...] = rank_o.astype(jnp.int32)
    gate_pad = jnp.concatenate([gate_o, jnp.zeros((LANES - gate_o.shape[0], tm), F32)], axis=0)
    gate_ref[...] = gate_pad.T

    gu = _dot(h_hi, wsg_ref[...])
    hs = gu.shape[1] // 2
    act = (jax.nn.silu(gu[:, :hs]) * gu[:, hs:]).astype(BF16)
    base_ref[...] = x_new + g2_ref[...] * _dot(act, wsd_ref[...])


def _post(att, gmn, x2, mod4, oa, w_out, nffn, wr_t, rb, w_sg, w_sd, *, tm, tokens_per_batch, tok0):
    t, att_w = att.shape
    d = x2.shape[1]
    n_exp = wr_t.shape[0]
    gm_w = gmn.shape[1]
    tpb = tokens_per_batch // tm
    blk0 = tok0 // tm
    const = lambda shape: pl.BlockSpec(shape, lambda i: (0,) * len(shape))
    mrow = lambda j: pl.BlockSpec((None, None, 1, d), lambda i: ((i + blk0) // tpb, j, 0, 0))
    by_choice = jax.ShapeDtypeStruct((CHOICE_ROWS, t), jnp.int32)
    return pl.pallas_call(
        _post_kernel,
        out_shape=(jax.ShapeDtypeStruct((t, d), F32), jax.ShapeDtypeStruct((t, d // 2), jnp.int32),
                   by_choice, jax.ShapeDtypeStruct((t, LANES), F32), by_choice,
                   jax.ShapeDtypeStruct((n_exp, LANES), F32)),
        grid=(t // tm,),
        in_specs=[pl.BlockSpec((tm, att_w), lambda i: (i, 0)),
                  pl.BlockSpec((tm, gm_w), lambda i: (i + blk0, 0)),
                  pl.BlockSpec((tm, d), lambda i: (i + blk0, 0)),
                  mrow(2), mrow(3), mrow(4), mrow(5),
                  const((1, att_w)), const(w_out.shape), const((1, d)),
                  const(wr_t.shape), const(rb.shape),
                  const(w_sg.shape), const(w_sd.shape)],
        out_specs=(pl.BlockSpec((tm, d), lambda i: (i, 0)), pl.BlockSpec((tm, d // 2), lambda i: (i, 0)),
                   pl.BlockSpec((CHOICE_ROWS, tm), lambda i: (0, i)), pl.BlockSpec((tm, LANES), lambda i: (i, 0)),
                   pl.BlockSpec((CHOICE_ROWS, tm), lambda i: (0, i)), pl.BlockSpec((n_exp, LANES), lambda i: (0, 0))),
        scratch_shapes=[pltpu.VMEM((n_exp, LANES), F32)],
        compiler_params=_cparams("arbitrary"),
        name="post",
    )(att, gmn, x2, mod4, mod4, mod4, mod4, oa, w_out, nffn, wr_t, rb, w_sg, w_sd)


def _sc_mesh_info():
    info = plsc.get_sparse_core_info()
    mesh = plsc.VectorSubcoreMesh(core_axis_name="c", subcore_axis_name="s")
    return mesh, info.num_cores, info.num_subcores, info.num_lanes


def _sc_params():
    cp = pltpu.CompilerParams()
    if "needs_layout_passes" in pltpu.CompilerParams.__dataclass_fields__:
        cp = dataclasses.replace(cp, needs_layout_passes=False)
    return cp


def _dispatch(h2p, idx_t, rank_t, offs, *, n_slots):
    t, dw = h2p.shape
    mesh, n_cores, n_sub, n_lanes = _sc_mesh_info()
    per_worker = t // (n_cores * n_sub)
    w = SC_WINDOW
    assert per_worker % w == 0 and w % n_lanes == 0

    @functools.partial(
        pl.kernel, mesh=mesh,
        out_type=(jax.ShapeDtypeStruct((n_slots, dw), jnp.int32), jax.ShapeDtypeStruct((CHOICE_ROWS, t), jnp.int32)),
        scratch_types=[pltpu.VMEM(offs.shape, jnp.int32)] + [pltpu.VMEM((TOP_K, w), jnp.int32)] * 3
        + [pltpu.VMEM((w, dw), jnp.int32), pltpu.SemaphoreType.DMA],
        compiler_params=_sc_params(),
    )
    def run(h_hbm, idx_hbm, rank_hbm, offs_hbm, xs_hbm, pos_hbm, offs_v, idx_v, rank_v, pos_v, rows_v, sem):
        base = (lax.axis_index("s") * n_cores + lax.axis_index("c")) * per_worker
        pltpu.sync_copy(offs_hbm, offs_v)

        @pl.loop(0, per_worker // w)
        def _(i):
            t0 = base + i * w
            pltpu.sync_copy(h_hbm.at[pl.ds(t0, w)], rows_v)
            for k in range(TOP_K):
                pltpu.sync_copy(idx_hbm.at[k, pl.ds(t0, w)], idx_v.at[k])
                pltpu.sync_copy(rank_hbm.at[k, pl.ds(t0, w)], rank_v.at[k])
            for k in range(TOP_K):
                for j in range(0, w, n_lanes):
                    group_start = plsc.load_gather(offs_v, [idx_v[k, pl.ds(j, n_lanes)]])
                    pos_v[k, pl.ds(j, n_lanes)] = group_start + rank_v[k, pl.ds(j, n_lanes)]
            copies = [pltpu.async_copy(rows_v, xs_hbm.at[pos_v.at[k]], sem) for k in range(TOP_K)]
            for k in range(TOP_K):
                pltpu.sync_copy(pos_v.at[k], pos_hbm.at[k, pl.ds(t0, w)])
            for cp in copies:
                cp.wait()

    return run(h2p, idx_t, rank_t, offs)


def _collect(ys, pos_t, *, tok_lo, n_tok):
    dw = ys.shape[1]
    mesh, n_cores, n_sub, _ = _sc_mesh_info()
    per_worker = n_tok // (n_cores * n_sub)
    w = SC_WINDOW
    assert per_worker % w == 0 and TOP_K % 2 == 0

    @functools.partial(
        pl.kernel, mesh=mesh,
        out_type=jax.ShapeDtypeStruct((TOP_K * n_tok, dw), jnp.int32),
        scratch_types=[pltpu.VMEM((TOP_K, w), jnp.int32), pltpu.VMEM((w, dw), jnp.int32),
                       pltpu.VMEM((w, dw), jnp.int32), pltpu.SemaphoreType.DMA, pltpu.SemaphoreType.DMA],
        compiler_params=_sc_params(),
    )
    def run(ys_hbm, pos_hbm, r_hbm, pos_v, rows_a, rows_b, sem_g, sem_w):
        base = (lax.axis_index("s") * n_cores + lax.axis_index("c")) * per_worker
        bufs = (rows_a, rows_b)

        @pl.loop(0, per_worker // w)
        def _(i):
            t0 = base + i * w
            for k in range(TOP_K):
                pltpu.sync_copy(pos_hbm.at[k, pl.ds(tok_lo + t0, w)], pos_v.at[k])
            gathers = [None] * TOP_K
            writes = [None] * TOP_K
            gathers[0] = pltpu.async_copy(ys_hbm.at[pos_v.at[0]], bufs[0], sem_g)
            for k in range(TOP_K):
                gathers[k].wait()
                if k + 1 < TOP_K:
                    if k >= 1:
                        writes[k - 1].wait()
                    gathers[k + 1] = pltpu.async_copy(ys_hbm.at[pos_v.at[k + 1]], bufs[(k + 1) % 2], sem_g)
                writes[k] = pltpu.async_copy(bufs[k % 2], r_hbm.at[pl.ds(k * n_tok + t0, w)], sem_w)
            writes[TOP_K - 2].wait()
            writes[TOP_K - 1].wait()

    return run(ys, pos_t)


def _experts_kernel(t0_ref, t1_ref, xs_hbm, wgu_ref, wdn_ref, ys_hbm, xbuf, ybuf, wgu_b, wdn_b, sem_in, sem_out):
    e = pl.program_id(0)
    n_exp = pl.num_programs(0)
    n_in, tr, half = xbuf.shape
    n_out = ybuf.shape[0]
    n_tiles = t1_ref[n_exp - 1]

    def fetch(t):
        slot = t & (n_in - 1)
        return pltpu.make_async_copy(xs_hbm.at[pl.ds(t * tr, tr)], xbuf.at[slot], sem_in.at[slot])

    def flush(t):
        slot = t & (n_out - 1)
        return pltpu.make_async_copy(ybuf.at[slot], ys_hbm.at[pl.ds(t * tr, tr)], sem_out.at[slot])

    @pl.when(e == 0)
    def _():
        for t in range(n_in - 1):
            @pl.when(t < n_tiles)
            def _():
                fetch(t).start()

    wgu_b[...] = wgu_ref[...].astype(BF16)
    wdn_b[...] = wdn_ref[...].astype(BF16)

    def tile(t, carry):
        fetch(t).wait()

        @pl.when(t + n_in - 1 < n_tiles)
        def _():
            fetch(t + n_in - 1).start()

        lo, hi = _unpack_rows(xbuf[t & (n_in - 1)])
        gu = _dot(lo.astype(BF16), wgu_b[:half, :]) + _dot(hi.astype(BF16), wgu_b[half:, :])
        hh = gu.shape[1] // 2
        act = (jax.nn.silu(gu[:, :hh]) * gu[:, hh:]).astype(BF16)
        y = _pack_rows(_dot(act, wdn_b[...]))

        @pl.when(t >= n_out)
        def _():
            flush(t - n_out).wait()

        ybuf[t & (n_out - 1)] = y
        flush(t).start()
        return carry

    lax.fori_loop(t0_ref[e], t1_ref[e], tile, 0)

    @pl.when(e == n_exp - 1)
    def _():
        for back in range(n_out, 0, -1):
            @pl.when(n_tiles >= back)
            def _():
                flush(n_tiles - back).wait()


def _experts(tile_lo, tile_hi, xs, w_gu, w_dn, after, *, tr):
    n_exp, d, h2w = w_gu.shape
    hh = w_dn.shape[1]
    in_specs = [pl.BlockSpec(memory_space=pl.ANY),
                pl.BlockSpec((None, d, h2w), lambda e, lo, hi: (e, 0, 0)),
                pl.BlockSpec((None, hh, d), lambda e, lo, hi: (e, 0, 0))]
    operands = [xs, w_gu, w_dn]
    n_read = 2 + len(operands)
    if after is not None:
        _unread_operand(in_specs, operands, after)
    n_in = 2 + len(operands)
    return pl.pallas_call(
        lambda *refs: _experts_kernel(*refs[:n_read], *refs[n_in:]),
        out_shape=jax.ShapeDtypeStruct(xs.shape, jnp.int32),
        grid_spec=pltpu.PrefetchScalarGridSpec(
            num_scalar_prefetch=2,
            grid=(n_exp,),
            in_specs=in_specs,
            out_specs=pl.BlockSpec(memory_space=pl.ANY),
            scratch_shapes=[pltpu.VMEM((EXPERT_IN_SLOTS, tr, d // 2), jnp.int32),
                            pltpu.VMEM((EXPERT_OUT_SLOTS, tr, d // 2), jnp.int32),
                            pltpu.VMEM((d, h2w), BF16), pltpu.VMEM((hh, d), BF16),
                            pltpu.SemaphoreType.DMA((EXPERT_IN_SLOTS,)), pltpu.SemaphoreType.DMA((EXPERT_OUT_SLOTS,))]),
        compiler_params=_cparams("arbitrary"),
        name="experts",
    )(tile_lo, tile_hi, *operands)


def _combine_kernel(base_ref, g2_ref, gate_ref, *refs):
    r_refs, o_ref = refs[:TOP_K], refs[TOP_K]
    gate = gate_ref[...]
    half = base_ref.shape[1] // 2
    acc_lo = jnp.zeros((base_ref.shape[0], half), F32)
    acc_hi = jnp.zeros((base_ref.shape[0], half), F32)
    for k in range(TOP_K):
        lo, hi = _unpack_rows(r_refs[k][...])
        g = gate[:, k:k + 1]
        acc_lo = acc_lo + g * lo
        acc_hi = acc_hi + g * hi
    g2 = g2_ref[...]
    o_ref[:, :half] = base_ref[:, :half] + g2[:, :half] * acc_lo
    o_ref[:, half:] = base_ref[:, half:] + g2[:, half:] * acc_hi


def _combine(base, mod4, gate, r, out_so_far, after, *, tm, tokens_per_batch, tok0, part_lo, total_tokens):
    d = base.shape[1]
    t = r.shape[0] // TOP_K
    tpb = tokens_per_batch // tm
    nblk = t // tm
    pblk0 = part_lo // tm
    blk0 = (tok0 + part_lo) // tm
    r_spec = lambda k: pl.BlockSpec((tm, d // 2), lambda i: (k * nblk + i, 0))
    in_specs = [pl.BlockSpec((tm, d), lambda i: (i + pblk0, 0)),
                pl.BlockSpec((None, None, 1, d), lambda i: ((i + blk0) // tpb, 5, 0, 0)),
                pl.BlockSpec((tm, LANES), lambda i: (i + pblk0, 0))] + [r_spec(k) for k in range(TOP_K)]
    operands = [base, mod4, gate] + [r] * TOP_K
    n_read = len(operands)
    aliases = {}
    if out_so_far is not None:
        aliases = {_unread_operand(in_specs, operands, out_so_far): 0}
    if after is not None:
        _unread_operand(in_specs, operands, after)
    n_in = len(operands)
    return pl.pallas_call(
        lambda *refs: _combine_kernel(*refs[:n_read], *refs[n_in:]),
        out_shape=jax.ShapeDtypeStruct((total_tokens, d), F32),
        grid=(nblk,),
        in_specs=in_specs,
        out_specs=pl.BlockSpec((tm, d), lambda i: (i + blk0, 0)),
        input_output_aliases=aliases,
        compiler_params=_cparams("arbitrary"),
        name="combine",
    )(*operands)


def _rope_tables(length, head_dim):
    rows = length // GRID_W
    pairs_axis = head_dim // 4
    r = jnp.repeat(jnp.arange(rows, dtype=F32), GRID_W)
    col = jnp.tile(jnp.arange(GRID_W, dtype=F32), rows)
    inv = ROPE_THETA ** (-jnp.arange(pairs_axis, dtype=F32) / pairs_axis)
    ang = jnp.concatenate([r[:, None] * inv, col[:, None] * inv], axis=-1)
    cos, sin = jnp.cos(ang), jnp.sin(ang)
    cos_h = jnp.concatenate([cos, cos], axis=-1)
    sin_h = jnp.concatenate([-sin, sin], axis=-1)
    reps = LANES // head_dim
    return jnp.tile(cos_h, (1, reps)), jnp.tile(sin_h, (1, reps))


def _part_batches(b):
    if b < 2:
        return [b]
    lead = max(1, min(b - 1, round(LEAD_PART_SHARE * b)))
    return [lead, b - lead]


def kernel(x, c, ctx, c_ctx, w_ada, b_ada, norm_mix, w_in, q_norm, k_norm, gm_norm, w_spatial, b_spatial,
           out_norm_attn, out_norm_gm, w_out, norm_ffn, w_router, router_bias, w_exp_gu, w_exp_down,
           w_sh_gu, w_sh_down):
    assert w_ada.shape[0] == 1, "single-layer kernel"
    b, length, d = x.shape
    head_dim = q_norm.shape[-1]
    att_w = out_norm_attn.shape[-1]
    gm_w = out_norm_gm.shape[-1]
    gm_groups = w_spatial.shape[1]
    gm_hd = gm_w // gm_groups
    n_exp = w_router.shape[-1]
    t = b * length
    assert head_dim == LANES // 2 and gm_hd == LANES // 2 and w_spatial.shape[-1] == CHUNK
    assert n_exp <= LANES and att_w % MXU_EDGE == 0 and gm_w % MXU_EDGE == 0

    mod_rows = 16
    c_all = jnp.zeros((mod_rows, d), F32).at[:b].set(c).at[b].set(c_ctx)
    mod = _ada(c_all, w_ada[0], b_ada)
    mod4 = mod.reshape(mod_rows, N_MOD, 1, d)

    w_in_b = w_in[0].astype(BF16)
    bd = (jnp.arange(MXU_EDGE)[:, None] // head_dim == jnp.arange(MXU_EDGE)[None, :] // head_dim)
    bd = (bd.astype(F32) / head_dim).astype(BF16)
    qg = jnp.tile(q_norm[0] * (head_dim ** -0.5 * LOG2_E), att_w // head_dim)[None, :]
    kg = jnp.tile(k_norm[0], LANES // head_dim)[None, :]
    gmg = jnp.tile(gm_norm[0], gm_groups)[None, :]
    ws = w_spatial[0].astype(BF16).reshape(gm_groups // 2, 2 * CHUNK, CHUNK)
    bs = jnp.repeat(b_spatial[0].T, gm_hd, axis=1)
    cos_t, sin_t = _rope_tables(length, head_dim)

    k_c, v_c = _ctx_kv(ctx, mod4, b, norm_mix, w_in_b[:, att_w:att_w + 2 * LANES], kg, bd)
    q, k_l, v_l, gmn = _inproj(x, mod4, norm_mix, w_in_b, cos_t, sin_t, qg, kg, gmg, bd, ws, bs,
                               out_norm_gm, att_w=att_w, gm_w=gm_w, tm=min(1024, length))
    wr_t = w_router[0].T.astype(BF16)
    rb = jnp.tile(router_bias[0][:, None], (1, LANES))
    w_out_b, w_sg_b, w_sd_b = w_out[0].astype(BF16), w_sh_gu[0].astype(BF16), w_sh_down[0].astype(BF16)
    gmn2, x2 = gmn.reshape(t, gm_w), x.reshape(t, d)
    tm_post = min(1024, length)
    tr = 512

    part_nb = _part_batches(b)
    n_parts = len(part_nb)
    part_b0 = [sum(part_nb[:p]) for p in range(n_parts)]
    state = [dict() for _ in range(n_parts)]

    last_tc = [None]

    def halves(part):
        nb = part_nb[part]
        return (nb // 2, nb - nb // 2) if nb >= 2 else (nb,)

    def attend(part, half):
        if half >= len(halves(part)):
            return
        st = state[part]
        slot0 = sum(halves(part)[:half])
        after = None if last_tc[0] is st.get("att") else last_tc[0]
        st["att"] = _attention(q, k_c, k_l, v_c, v_l, st.get("att"), after, tq=min(256, length),
                               b0=part_b0[part] + slot0, nb=halves(part)[half], slot0=slot0, slots=part_nb[part])
        last_tc[0] = st["att"]

    def route(part):
        st = state[part]
        tp = part_nb[part] * length
        nt = tp * TOP_K // tr + n_exp
        base, h2p, idx_t, gate, rank_t, cnt = _post(
            st["att"].reshape(tp, att_w), gmn2, x2, mod4, out_norm_attn, w_out_b, norm_ffn, wr_t, rb, w_sg_b, w_sd_b,
            tm=tm_post, tokens_per_batch=length, tok0=part_b0[part] * length)
        counts = cnt[:, 0].astype(jnp.int32)
        padded = (counts + tr - 1) // tr * tr
        ends = jnp.cumsum(padded)
        offs = ends - padded
        offs_pad = jnp.zeros((LANES,), jnp.int32).at[:n_exp].set(offs)
        xs, pos_t = _dispatch(h2p, idx_t, rank_t, offs_pad, n_slots=nt * tr)
        st.update(base=base, gate=gate, xs=xs, pos_t=pos_t, tile_lo=offs // tr, tile_hi=ends // tr)
        last_tc[0] = base

    def run_experts(part):
        st = state[part]
        ys = _experts(st["tile_lo"], st["tile_hi"], st["xs"], w_exp_gu[0], w_exp_down[0], last_tc[0], tr=tr)
        n_chunks = part_nb[part] if part == n_parts - 1 else 1
        chunk = part_nb[part] * length // n_chunks
        st["r"] = [(c * chunk, _collect(ys, st["pos_t"], tok_lo=c * chunk, n_tok=chunk)) for c in range(n_chunks)]
        last_tc[0] = ys

    result = [None]

    def finish(part):
        st = state[part]
        for part_lo, r in st["r"]:
            after = None if last_tc[0] is result[0] or last_tc[0] is st["base"] else last_tc[0]
            result[0] = _combine(st["base"], mod4, st["gate"], r, result[0], after, tm=min(512, length),
                                 tokens_per_batch=length, tok0=part_b0[part] * length, part_lo=part_lo,
                                 total_tokens=t)
            last_tc[0] = result[0]

    stages = [(0, lambda p: attend(p, 0)), (1, lambda p: attend(p, 1)), (2, route), (4, run_experts), (7, finish)]
    plan = sorted((when + 3 * part, part, n, fn) for part in range(n_parts) for n, (when, fn) in enumerate(stages))
    for _, part, _, fn in plan:
        fn(part)
    return result[0].reshape(b, length, d)
```

```python
import dataclasses
import functools

import jax
import jax.numpy as jnp
from jax import lax
from jax.experimental import pallas as pl
from jax.experimental.pallas import tpu as pltpu
from jax.experimental.pallas import tpu_sc as plsc

F32 = jnp.float32
BF16 = jnp.bfloat16

EPS = 1e-6
GRID_W = 64
ROPE_THETA = 10000.0
ATT_KV_HEADS = 2
TOP_K = 6
ROUTED_SCALE = 2.5
N_MOD = 6
CHUNK = 128
LOG2_E = 1.4426950408889634

LANES = 128
MXU_EDGE = 256
VMEM_LIMIT_BYTES = 56 * 1024 * 1024
SUBLANES = 8
CHOICE_ROWS = SUBLANES
SC_WINDOW = 64
LEAD_PART_SHARE = 0.75
EXPERT_IN_SLOTS = 4
EXPERT_OUT_SLOTS = 2


def _cparams(*sem):
    return pltpu.CompilerParams(dimension_semantics=sem, vmem_limit_bytes=VMEM_LIMIT_BYTES)


def _dot(a, b):
    return jnp.dot(a, b, preferred_element_type=F32)


def _dot_nt(a, b):
    return lax.dot_general(a, b, (((1,), (1,)), ((), ())), preferred_element_type=F32)


def _rms(x, g):
    return x * lax.rsqrt(jnp.mean(x * x, axis=-1, keepdims=True) + EPS) * g


def _group_mean_sq(x, bd_ref):
    x2 = (x * x).astype(BF16)
    parts = [_dot(x2[:, i:i + MXU_EDGE], bd_ref[...]) for i in range(0, x.shape[1], MXU_EDGE)]
    return parts[0] if len(parts) == 1 else jnp.concatenate(parts, axis=-1)


def _dup_halves(x, lane):
    xr = pltpu.roll(x, LANES // 2, axis=1)
    lo = lane < LANES // 2
    return jnp.where(lo, x, xr), jnp.where(lo, xr, x)


def _with_ones(x, lane):
    xr = pltpu.roll(x, LANES // 2, axis=1)
    lo = lane < LANES // 2
    return jnp.concatenate([jnp.where(lo, x, 1.0), jnp.where(lo, xr, 1.0)], axis=-1)


_HI16 = 0xFFFF0000


def _pack_rows(x):
    half = x.shape[1] // 2
    rounded = x.astype(BF16).astype(F32)
    bits = lax.bitcast_convert_type(rounded, jnp.uint32)
    word = (bits[:, :half] >> 16) | (bits[:, half:] & jnp.uint32(_HI16))
    return lax.bitcast_convert_type(word, jnp.int32)


def _unpack_rows(w):
    bits = lax.bitcast_convert_type(w, jnp.uint32)
    lo = lax.bitcast_convert_type(bits << 16, F32)
    hi = lax.bitcast_convert_type(bits & jnp.uint32(_HI16), F32)
    return lo, hi


def _ada_kernel(c_ref, w_ref, b_ref, o_ref):
    s = jax.nn.silu(c_ref[...])
    o_ref[...] = jnp.dot(s, w_ref[...], precision=lax.Precision.HIGHEST,
                         preferred_element_type=F32) + b_ref[...]


def _ada(c_all, w_ada, b_ada):
    rows, d = c_all.shape
    n = w_ada.shape[1]
    bn = 2048 if n % 2048 == 0 else 512
    return pl.pallas_call(
        _ada_kernel,
        out_shape=jax.ShapeDtypeStruct((rows, n), F32),
        grid=(n // bn,),
        in_specs=[pl.BlockSpec((rows, d), lambda j: (0, 0)),
                  pl.BlockSpec((d, bn), lambda j: (0, j)),
                  pl.BlockSpec((1, bn), lambda j: (0, j))],
        out_specs=pl.BlockSpec((rows, bn), lambda j: (0, j)),
        compiler_params=_cparams("arbitrary"),
        name="ada",
    )(c_all, w_ada, b_ada)


def _ctx_kv_kernel(x_ref, sh_ref, sc_ref, nmix_ref, w_ref, kg_ref, bd_ref, k_ref, v_ref):
    h = _rms(x_ref[...], nmix_ref[...]) * (1.0 + sc_ref[...]) + sh_ref[...]
    p = _dot(h.astype(BF16), w_ref[...])
    ms = _group_mean_sq(p, bd_ref)
    k = p[:, :LANES] * lax.rsqrt(ms[:, :LANES] + EPS) * kg_ref[...]
    v = p[:, LANES:]
    lane = lax.broadcasted_iota(jnp.int32, k.shape, 1)
    ka, kb = _dup_halves(k, lane)
    k_ref[...] = jnp.concatenate([ka, kb], axis=-1).astype(BF16)
    v_ref[...] = _with_ones(v, lane).astype(BF16)


def _ctx_kv(ctx, mod4, ctx_row, nmix, w_kv, kg, bd):
    b, n_ctx, d = ctx.shape
    kv_w = 2 * LANES
    return pl.pallas_call(
        _ctx_kv_kernel,
        out_shape=(jax.ShapeDtypeStruct((b, n_ctx, kv_w), BF16),) * 2,
        grid=(b,),
        in_specs=[pl.BlockSpec((None, n_ctx, d), lambda i: (i, 0, 0)),
                  pl.BlockSpec((None, None, 1, d), lambda i: (ctx_row, 0, 0, 0)),
                  pl.BlockSpec((None, None, 1, d), lambda i: (ctx_row, 1, 0, 0)),
                  pl.BlockSpec((1, d), lambda i: (0, 0)),
                  pl.BlockSpec(w_kv.shape, lambda i: (0, 0)),
                  pl.BlockSpec((1, LANES), lambda i: (0, 0)),
                  pl.BlockSpec(bd.shape, lambda i: (0, 0))],
        out_specs=(pl.BlockSpec((None, n_ctx, kv_w), lambda i: (i, 0, 0)),) * 2,
        compiler_params=_cparams("arbitrary"),
        name="ctx_kv",
    )(ctx, mod4, mod4, nmix, w_kv, kg, bd)


def _inproj_kernel(x_ref, sh_ref, sc_ref, nmix_ref, w_ref, cos_ref, sin_ref, qg_ref, kg_ref, gmg_ref,
                   bd_ref, ws_ref, bs_ref, og_ref, q_ref, k_ref, v_ref, gm_ref, *, att_w, gm_w):
    tm = x_ref.shape[0]
    h = _rms(x_ref[...], nmix_ref[...]) * (1.0 + sc_ref[...]) + sh_ref[...]
    p = _dot(h.astype(BF16), w_ref[...])

    cos1, sin1 = cos_ref[...], sin_ref[...]
    lane1 = lax.broadcasted_iota(jnp.int32, (tm, LANES), 1)

    def rope(xn, cosw, sinw, lanew):
        w = xn.shape[1]
        fwd = pltpu.roll(xn, w - 32, axis=1)
        bwd = pltpu.roll(xn, 32, axis=1)
        swapped = jnp.where((lanew & 63) < 32, fwd, bwd)
        return xn * cosw + swapped * sinw

    q = p[:, :att_w]
    qn = q * lax.rsqrt(_group_mean_sq(q, bd_ref) + EPS) * qg_ref[...]
    reps = att_w // LANES
    cosq = jnp.concatenate([cos1] * reps, axis=-1)
    sinq = jnp.concatenate([sin1] * reps, axis=-1)
    laneq = lax.broadcasted_iota(jnp.int32, (tm, att_w), 1)
    q_ref[...] = rope(qn, cosq, sinq, laneq).astype(BF16)

    kv = p[:, att_w:att_w + 2 * LANES]
    ms = _group_mean_sq(kv, bd_ref)
    kn = kv[:, :LANES] * lax.rsqrt(ms[:, :LANES] + EPS) * kg_ref[...]
    k = rope(kn, cos1, sin1, lane1)
    v = kv[:, LANES:]
    ka, kb = _dup_halves(k, lane1)
    k_ref[...] = jnp.concatenate([ka, kb], axis=-1).astype(BF16)
    v_ref[...] = _with_ones(v, lane1).astype(BF16)

    u0 = att_w + 2 * LANES
    u = jax.nn.gelu(p[:, u0:u0 + gm_w])
    gt = jax.nn.gelu(p[:, u0 + gm_w:u0 + 2 * gm_w])
    gt = (gt * lax.rsqrt(_group_mean_sq(gt, bd_ref) + EPS) * gmg_ref[...]).astype(BF16)
    lo_half = lax.broadcasted_iota(jnp.int32, (CHUNK, LANES), 1) < LANES // 2
    rows = []
    for c in range(tm // CHUNK):
        blocks = []
        for j in range(gm_w // LANES):
            g = gt[c * CHUNK:(c + 1) * CHUNK, j * LANES:(j + 1) * LANES]
            r = _dot(ws_ref[j], g)
            blocks.append(jnp.where(lo_half, r[:CHUNK], r[CHUNK:]))
        rows.append(jnp.concatenate(blocks, axis=-1) + bs_ref[...])
    mixed = jnp.concatenate(rows, axis=0)
    gm_ref[...] = _rms(u * mixed, og_ref[...]).astype(BF16)


def _inproj(x, mod4, nmix, w_in, cos_t, sin_t, qg, kg, gmg, bd, ws, bs, og, *, att_w, gm_w, tm):
    b, length, d = x.shape
    kv_w = 2 * LANES
    n_in = w_in.shape[1]
    const = lambda shape: pl.BlockSpec(shape, lambda i, j: (0,) * len(shape))
    return pl.pallas_call(
        functools.partial(_inproj_kernel, att_w=att_w, gm_w=gm_w),
        out_shape=(jax.ShapeDtypeStruct((b, length, att_w), BF16),
                   jax.ShapeDtypeStruct((b, length, kv_w), BF16),
                   jax.ShapeDtypeStruct((b, length, kv_w), BF16),
                   jax.ShapeDtypeStruct((b, length, gm_w), BF16)),
        grid=(b, length // tm),
        in_specs=[pl.BlockSpec((None, tm, d), lambda i, j: (i, j, 0)),
                  pl.BlockSpec((None, None, 1, d), lambda i, j: (i, 0, 0, 0)),
                  pl.BlockSpec((None, None, 1, d), lambda i, j: (i, 1, 0, 0)),
                  const((1, d)),
                  const((d, n_in)),
                  pl.BlockSpec((tm, LANES), lambda i, j: (j, 0)),
                  pl.BlockSpec((tm, LANES), lambda i, j: (j, 0)),
                  const((1, att_w)), const((1, LANES)), const((1, gm_w)),
                  const(bd.shape), const(ws.shape), const(bs.shape), const((1, gm_w))],
        out_specs=(pl.BlockSpec((None, tm, att_w), lambda i, j: (i, j, 0)),
                   pl.BlockSpec((None, tm, kv_w), lambda i, j: (i, j, 0)),
                   pl.BlockSpec((None, tm, kv_w), lambda i, j: (i, j, 0)),
                   pl.BlockSpec((None, tm, gm_w), lambda i, j: (i, j, 0))),
        compiler_params=_cparams("arbitrary", "arbitrary"),
        name="inproj",
    )(x, mod4, mod4, nmix, w_in, cos_t, sin_t, qg, kg, gmg, bd, ws, bs, og)


def _attn_kernel(q_ref, kc_ref, kl_ref, vc_ref, vl_ref, o_ref, s_buf, p_buf, *, tq):
    n_ctx = kc_ref.shape[0]
    heads = q_ref.shape[1] // (LANES // 2)
    nj = q_ref.shape[0] // tq
    half = LANES // 2
    lo = lax.broadcasted_iota(jnp.int32, (tq, LANES), 1) < half

    @pl.when(jnp.logical_and(pl.program_id(0) == 0, pl.program_id(1) == 0))
    def _():
        s_buf[...] = jnp.zeros_like(s_buf)
        p_buf[...] = jnp.ones_like(p_buf)

    def scores(j, g):
        q = q_ref[pl.ds(j * tq, tq), (g // 2) * LANES:(g // 2 + 1) * LANES]
        zero = jnp.zeros_like(q)
        qh = jnp.where(lo, q, zero) if g % 2 == 0 else jnp.where(lo, zero, q)
        s_buf[g % 2, :, :n_ctx] = _dot_nt(qh, kc_ref[...])
        s_buf[g % 2, :, n_ctx:] = _dot_nt(qh, kl_ref[...])

    def probs(g):
        s = s_buf[g % 2]
        p_buf[g % 2] = jnp.exp2(s - jnp.max(s, axis=-1, keepdims=True)).astype(BF16)

    def output(j, g):
        p = p_buf[g % 2]
        w = _dot(p[:, :n_ctx], vc_ref[...]) + _dot(p[:, n_ctx:], vl_ref[...])
        t = jnp.where(pl.program_id(1) == 0, w[:, :LANES], w[:, LANES:])
        r = pltpu.roll(t, half, axis=1)
        rows = pl.ds(j * tq, tq)
        if g % 2 == 0:
            o_ref[rows, g * half:(g + 1) * half] = (t / r)[:, :half].astype(o_ref.dtype)
        else:
            o_ref[rows, g * half:(g + 1) * half] = (r / t)[:, half:].astype(o_ref.dtype)

    def tile_stages(j):
        j_prev = jnp.maximum(j - 1, 0)
        for g in range(heads):
            scores(j, g)
            probs((g - 1) % heads)
            output(j if g >= 2 else j_prev, (g - 2) % heads)

    def trip(jj, carry):
        for u in range(tiles_per_trip):
            tile_stages(jj * tiles_per_trip + u)
        return carry

    tiles_per_trip = 2 if nj % 2 == 0 else 1
    lax.fori_loop(0, nj // tiles_per_trip, trip, 0)
    probs(heads - 1)
    output(nj - 1, heads - 2)
    output(nj - 1, heads - 1)


def _unread_operand(in_specs, operands, array):
    in_specs.append(pl.BlockSpec(memory_space=pl.ANY))
    operands.append(array)
    return len(operands) - 1


def _attention(q, k_c, k_l, v_c, v_l, att_so_far, after, *, tq, b0, nb, slot0, slots):
    _, length, att_w = q.shape
    n_ctx = k_c.shape[1]
    group_w = att_w // ATT_KV_HEADS
    assert (group_w // (LANES // 2)) % 2 == 0 and length % tq == 0
    k_blk = lambda n: pl.BlockSpec((None, n, LANES), lambda i, h: (i + b0, 0, h))
    v_blk = lambda n: pl.BlockSpec((None, n, ATT_KV_HEADS * LANES), lambda i, h: (i + b0, 0, 0))
    in_specs = [pl.BlockSpec((None, length, group_w), lambda i, h: (i + b0, 0, h)),
                k_blk(n_ctx), k_blk(length), v_blk(n_ctx), v_blk(length)]
    operands = [q, k_c, k_l, v_c, v_l]
    n_read = len(operands)
    aliases = {}
    if att_so_far is not None:
        aliases = {_unread_operand(in_specs, operands, att_so_far): 0}
    if after is not None:
        _unread_operand(in_specs, operands, after)
    n_in = len(operands)
    return pl.pallas_call(
        lambda *refs: _attn_kernel(*refs[:n_read], *refs[n_in:], tq=tq),
        out_shape=jax.ShapeDtypeStruct((slots, length, att_w), BF16),
        grid=(nb, ATT_KV_HEADS),
        in_specs=in_specs,
        out_specs=pl.BlockSpec((None, length, group_w), lambda i, h: (i + slot0, 0, h)),
        scratch_shapes=[pltpu.VMEM((2, tq, n_ctx + length), F32), pltpu.VMEM((2, tq, n_ctx + length), BF16)],
        input_output_aliases=aliases,
        compiler_params=_cparams("arbitrary", "arbitrary"),
        name="attn",
    )(*operands)


def _post_kernel(att_ref, gm_ref, x_ref, g1_ref, sh2_ref, sc2_ref, g2_ref, oa_ref, wo_ref, nffn_ref,
                 wrt_ref, rb_ref, wsg_ref, wsd_ref,
                 base_ref, h2p_ref, idx_ref, gate_ref, rank_ref, cnt_ref, carry_ref):
    i = pl.program_id(0)
    tm = x_ref.shape[0]
    n_exp = wrt_ref.shape[0]

    @pl.when(i == 0)
    def _():
        carry_ref[...] = jnp.zeros_like(carry_ref)

    att_n = _rms(att_ref[...].astype(F32), oa_ref[...]).astype(BF16)
    y = _dot(jnp.concatenate([att_n, gm_ref[...]], axis=-1), wo_ref[...])
    x_new = x_ref[...] + g1_ref[...] * y
    h2 = _rms(x_new, nffn_ref[...]) * (1.0 + sc2_ref[...]) + sh2_ref[...]
    h2p_ref[...] = _pack_rows(h2)

    h_hi = h2.astype(BF16)
    scores_t = jax.nn.sigmoid(_dot_nt(wrt_ref[...], h_hi))
    reps = tm // LANES
    sel = scores_t + jnp.concatenate([rb_ref[...]] * reps, axis=-1)
    expert = lax.broadcasted_iota(jnp.int32, scores_t.shape, 0).astype(F32)
    neg = jnp.float32(-jnp.inf)

    onehot = jnp.zeros(scores_t.shape, F32)
    hits, idxs, svals = [], [], []
    for _k in range(TOP_K):
        m = jnp.max(sel, axis=0, keepdims=True)
        idx = jnp.min(jnp.where(sel == m, expert, float(n_exp)), axis=0, keepdims=True)
        hit = expert == idx
        hits.append(hit)
        idxs.append(idx)
        svals.append(jnp.sum(jnp.where(hit, scores_t, 0.0), axis=0, keepdims=True))
        onehot = onehot + jnp.where(hit, 1.0, 0.0)
        sel = jnp.where(hit, neg, sel)
    ssum = functools.reduce(lambda a, b: a + b, svals)

    r_io = lax.broadcasted_iota(jnp.int32, (tm, tm), 0)
    c_io = lax.broadcasted_iota(jnp.int32, (tm, tm), 1)
    tri = jnp.where(r_io < c_io, 1.0, 0.0).astype(BF16)
    carry = carry_ref[...]
    before = _dot(onehot.astype(BF16), tri) + jnp.concatenate([carry] * reps, axis=-1)
    carry = carry + jnp.sum(onehot, axis=1, keepdims=True)
    carry_ref[...] = carry
    cnt_ref[...] = carry

    row = lax.broadcasted_iota(jnp.int32, (idx_ref.shape[0], tm), 0)
    idx_o = jnp.zeros(row.shape, F32)
    rank_o = jnp.zeros(row.shape, F32)
    gate_o = jnp.zeros(row.shape, F32)
    for k in range(TOP_K):
        rank = jnp.sum(jnp.where(hits[k], before, 0.0), axis=0, keepdims=True)
        idx_o = jnp.where(row == k, idxs[k], idx_o)
        rank_o = jnp.where(row == k, rank, rank_o)
        gate_o = jnp.where(row == k, svals[k] / ssum * ROUTED_SCALE, gate_o)
    idx_ref[...] = idx_o.astype(jnp.int32)
    rank_ref[...] = rank_o.astype(jnp.int32)
    gate_pad = jnp.concatenate([gate_o, jnp.zeros((LANES - gate_o.shape[0], tm), F32)], axis=0)
    gate_ref[...] = gate_pad.T

    gu = _dot(h_hi, wsg_ref[...])
    hs = gu.shape[1] // 2
    act = (jax.nn.silu(gu[:, :hs]) * gu[:, hs:]).astype(BF16)
    base_ref[...] = x_new + g2_ref[...] * _dot(act, wsd_ref[...])


def _post(att, gmn, x2, mod4, oa, w_out, nffn, wr_t, rb, w_sg, w_sd, *, tm, tokens_per_batch, tok0):
    t, att_w = att.shape
    d = x2.shape[1]
    n_exp = wr_t.shape[0]
    gm_w = gmn.shape[1]
    tpb = tokens_per_batch // tm
    blk0 = tok0 // tm
    const = lambda shape: pl.BlockSpec(shape, lambda i: (0,) * len(shape))
    mrow = lambda j: pl.BlockSpec((None, None, 1, d), lambda i: ((i + blk0) // tpb, j, 0, 0))
    by_choice = jax.ShapeDtypeStruct((CHOICE_ROWS, t), jnp.int32)
    return pl.pallas_call(
        _post_kernel,
        out_shape=(jax.ShapeDtypeStruct((t, d), F32), jax.ShapeDtypeStruct((t, d // 2), jnp.int32),
                   by_choice, jax.ShapeDtypeStruct((t, LANES), F32), by_choice,
                   jax.ShapeDtypeStruct((n_exp, LANES), F32)),
        grid=(t // tm,),
        in_specs=[pl.BlockSpec((tm, att_w), lambda i: (i, 0)),
                  pl.BlockSpec((tm, gm_w), lambda i: (i + blk0, 0)),
                  pl.BlockSpec((tm, d), lambda i: (i + blk0, 0)),
                  mrow(2), mrow(3), mrow(4), mrow(5),
                  const((1, att_w)), const(w_out.shape), const((1, d)),
                  const(wr_t.shape), const(rb.shape),
                  const(w_sg.shape), const(w_sd.shape)],
        out_specs=(pl.BlockSpec((tm, d), lambda i: (i, 0)), pl.BlockSpec((tm, d // 2), lambda i: (i, 0)),
                   pl.BlockSpec((CHOICE_ROWS, tm), lambda i: (0, i)), pl.BlockSpec((tm, LANES), lambda i: (i, 0)),
                   pl.BlockSpec((CHOICE_ROWS, tm), lambda i: (0, i)), pl.BlockSpec((n_exp, LANES), lambda i: (0, 0))),
        scratch_shapes=[pltpu.VMEM((n_exp, LANES), F32)],
        compiler_params=_cparams("arbitrary"),
        name="post",
    )(att, gmn, x2, mod4, mod4, mod4, mod4, oa, w_out, nffn, wr_t, rb, w_sg, w_sd)


def _sc_mesh_info():
    info = plsc.get_sparse_core_info()
    mesh = plsc.VectorSubcoreMesh(core_axis_name="c", subcore_axis_name="s")
    return mesh, info.num_cores, info.num_subcores, info.num_lanes


def _sc_params():
    cp = pltpu.CompilerParams()
    if "needs_layout_passes" in pltpu.CompilerParams.__dataclass_fields__:
        cp = dataclasses.replace(cp, needs_layout_passes=False)
    return cp


def _dispatch(h2p, idx_t, rank_t, offs, *, n_slots):
    t, dw = h2p.shape
    mesh, n_cores, n_sub, n_lanes = _sc_mesh_info()
    per_worker = t // (n_cores * n_sub)
    w = SC_WINDOW
    assert per_worker % w == 0 and w % n_lanes == 0

    @functools.partial(
        pl.kernel, mesh=mesh,
        out_type=(jax.ShapeDtypeStruct((n_slots, dw), jnp.int32), jax.ShapeDtypeStruct((CHOICE_ROWS, t), jnp.int32)),
        scratch_types=[pltpu.VMEM(offs.shape, jnp.int32)] + [pltpu.VMEM((TOP_K, w), jnp.int32)] * 3
        + [pltpu.VMEM((w, dw), jnp.int32), pltpu.SemaphoreType.DMA],
        compiler_params=_sc_params(),
    )
    def run(h_hbm, idx_hbm, rank_hbm, offs_hbm, xs_hbm, pos_hbm, offs_v, idx_v, rank_v, pos_v, rows_v, sem):
        base = (lax.axis_index("s") * n_cores + lax.axis_index("c")) * per_worker
        pltpu.sync_copy(offs_hbm, offs_v)

        @pl.loop(0, per_worker // w)
        def _(i):
            t0 = base + i * w
            pltpu.sync_copy(h_hbm.at[pl.ds(t0, w)], rows_v)
            for k in range(TOP_K):
                pltpu.sync_copy(idx_hbm.at[k, pl.ds(t0, w)], idx_v.at[k])
                pltpu.sync_copy(rank_hbm.at[k, pl.ds(t0, w)], rank_v.at[k])
            for k in range(TOP_K):
                for j in range(0, w, n_lanes):
                    group_start = plsc.load_gather(offs_v, [idx_v[k, pl.ds(j, n_lanes)]])
                    pos_v[k, pl.ds(j, n_lanes)] = group_start + rank_v[k, pl.ds(j, n_lanes)]
            copies = [pltpu.async_copy(rows_v, xs_hbm.at[pos_v.at[k]], sem) for k in range(TOP_K)]
            for k in range(TOP_K):
                pltpu.sync_copy(pos_v.at[k], pos_hbm.at[k, pl.ds(t0, w)])
            for cp in copies:
                cp.wait()

    return run(h2p, idx_t, rank_t, offs)


def _collect(ys, pos_t, *, tok_lo, n_tok):
    dw = ys.shape[1]
    mesh, n_cores, n_sub, _ = _sc_mesh_info()
    per_worker = n_tok // (n_cores * n_sub)
    w = SC_WINDOW
    assert per_worker % w == 0 and TOP_K % 2 == 0

    @functools.partial(
        pl.kernel, mesh=mesh,
        out_type=jax.ShapeDtypeStruct((TOP_K * n_tok, dw), jnp.int32),
        scratch_types=[pltpu.VMEM((TOP_K, w), jnp.int32), pltpu.VMEM((w, dw), jnp.int32),
                       pltpu.VMEM((w, dw), jnp.int32), pltpu.SemaphoreType.DMA, pltpu.SemaphoreType.DMA],
        compiler_params=_sc_params(),
    )
    def run(ys_hbm, pos_hbm, r_hbm, pos_v, rows_a, rows_b, sem_g, sem_w):
        base = (lax.axis_index("s") * n_cores + lax.axis_index("c")) * per_worker
        bufs = (rows_a, rows_b)

        @pl.loop(0, per_worker // w)
        def _(i):
            t0 = base + i * w
            for k in range(TOP_K):
                pltpu.sync_copy(pos_hbm.at[k, pl.ds(tok_lo + t0, w)], pos_v.at[k])
            gathers = [None] * TOP_K
            writes = [None] * TOP_K
            gathers[0] = pltpu.async_copy(ys_hbm.at[pos_v.at[0]], bufs[0], sem_g)
            for k in range(TOP_K):
                gathers[k].wait()
                if k + 1 < TOP_K:
                    if k >= 1:
                        writes[k - 1].wait()
                    gathers[k + 1] = pltpu.async_copy(ys_hbm.at[pos_v.at[k + 1]], bufs[(k + 1) % 2], sem_g)
                writes[k] = pltpu.async_copy(bufs[k % 2], r_hbm.at[pl.ds(k * n_tok + t0, w)], sem_w)
            writes[TOP_K - 2].wait()
            writes[TOP_K - 1].wait()

    return run(ys, pos_t)


def _experts_kernel(t0_ref, t1_ref, xs_hbm, wgu_ref, wdn_ref, ys_hbm, xbuf, ybuf, wgu_b, wdn_b, sem_in, sem_out):
    e = pl.program_id(0)
    n_exp = pl.num_programs(0)
    n_in, tr, half = xbuf.shape
    n_out = ybuf.shape[0]
    n_tiles = t1_ref[n_exp - 1]

    def fetch(t):
        slot = t & (n_in - 1)
        return pltpu.make_async_copy(xs_hbm.at[pl.ds(t * tr, tr)], xbuf.at[slot], sem_in.at[slot])

    def flush(t):
        slot = t & (n_out - 1)
        return pltpu.make_async_copy(ybuf.at[slot], ys_hbm.at[pl.ds(t * tr, tr)], sem_out.at[slot])

    @pl.when(e == 0)
    def _():
        for t in range(n_in - 1):
            @pl.when(t < n_tiles)
            def _():
                fetch(t).start()

    wgu_b[...] = wgu_ref[...].astype(BF16)
    wdn_b[...] = wdn_ref[...].astype(BF16)

    def tile(t, carry):
        fetch(t).wait()

        @pl.when(t + n_in - 1 < n_tiles)
        def _():
            fetch(t + n_in - 1).start()

        lo, hi = _unpack_rows(xbuf[t & (n_in - 1)])
        gu = _dot(lo.astype(BF16), wgu_b[:half, :]) + _dot(hi.astype(BF16), wgu_b[half:, :])
        hh = gu.shape[1] // 2
        act = (jax.nn.silu(gu[:, :hh]) * gu[:, hh:]).astype(BF16)
        y = _pack_rows(_dot(act, wdn_b[...]))

        @pl.when(t >= n_out)
        def _():
            flush(t - n_out).wait()

        ybuf[t & (n_out - 1)] = y
        flush(t).start()
        return carry

    lax.fori_loop(t0_ref[e], t1_ref[e], tile, 0)

    @pl.when(e == n_exp - 1)
    def _():
        for back in range(n_out, 0, -1):
            @pl.when(n_tiles >= back)
            def _():
                flush(n_tiles - back).wait()


def _experts(tile_lo, tile_hi, xs, w_gu, w_dn, after, *, tr):
    n_exp, d, h2w = w_gu.shape
    hh = w_dn.shape[1]
    in_specs = [pl.BlockSpec(memory_space=pl.ANY),
                pl.BlockSpec((None, d, h2w), lambda e, lo, hi: (e, 0, 0)),
                pl.BlockSpec((None, hh, d), lambda e, lo, hi: (e, 0, 0))]
    operands = [xs, w_gu, w_dn]
    n_read = 2 + len(operands)
    if after is not None:
        _unread_operand(in_specs, operands, after)
    n_in = 2 + len(operands)
    return pl.pallas_call(
        lambda *refs: _experts_kernel(*refs[:n_read], *refs[n_in:]),
        out_shape=jax.ShapeDtypeStruct(xs.shape, jnp.int32),
        grid_spec=pltpu.PrefetchScalarGridSpec(
            num_scalar_prefetch=2,
            grid=(n_exp,),
            in_specs=in_specs,
            out_specs=pl.BlockSpec(memory_space=pl.ANY),
            scratch_shapes=[pltpu.VMEM((EXPERT_IN_SLOTS, tr, d // 2), jnp.int32),
                            pltpu.VMEM((EXPERT_OUT_SLOTS, tr, d // 2), jnp.int32),
                            pltpu.VMEM((d, h2w), BF16), pltpu.VMEM((hh, d), BF16),
                            pltpu.SemaphoreType.DMA((EXPERT_IN_SLOTS,)), pltpu.SemaphoreType.DMA((EXPERT_OUT_SLOTS,))]),
        compiler_params=_cparams("arbitrary"),
        name="experts",
    )(tile_lo, tile_hi, *operands)


def _combine_kernel(base_ref, g2_ref, gate_ref, *refs):
    r_refs, o_ref = refs[:TOP_K], refs[TOP_K]
    gate = gate_ref[...]
    half = base_ref.shape[1] // 2
    acc_lo = jnp.zeros((base_ref.shape[0], half), F32)
    acc_hi = jnp.zeros((base_ref.shape[0], half), F32)
    for k in range(TOP_K):
        lo, hi = _unpack_rows(r_refs[k][...])
        g = gate[:, k:k + 1]
        acc_lo = acc_lo + g * lo
        acc_hi = acc_hi + g * hi
    g2 = g2_ref[...]
    o_ref[:, :half] = base_ref[:, :half] + g2[:, :half] * acc_lo
    o_ref[:, half:] = base_ref[:, half:] + g2[:, half:] * acc_hi


def _combine(base, mod4, gate, r, out_so_far, after, *, tm, tokens_per_batch, tok0, part_lo, total_tokens):
    d = base.shape[1]
    t = r.shape[0] // TOP_K
    tpb = tokens_per_batch // tm
    nblk = t // tm
    pblk0 = part_lo // tm
    blk0 = (tok0 + part_lo) // tm
    r_spec = lambda k: pl.BlockSpec((tm, d // 2), lambda i: (k * nblk + i, 0))
    in_specs = [pl.BlockSpec((tm, d), lambda i: (i + pblk0, 0)),
                pl.BlockSpec((None, None, 1, d), lambda i: ((i + blk0) // tpb, 5, 0, 0)),
                pl.BlockSpec((tm, LANES), lambda i: (i + pblk0, 0))] + [r_spec(k) for k in range(TOP_K)]
    operands = [base, mod4, gate] + [r] * TOP_K
    n_read = len(operands)
    aliases = {}
    if out_so_far is not None:
        aliases = {_unread_operand(in_specs, operands, out_so_far): 0}
    if after is not None:
        _unread_operand(in_specs, operands, after)
    n_in = len(operands)
    return pl.pallas_call(
        lambda *refs: _combine_kernel(*refs[:n_read], *refs[n_in:]),
        out_shape=jax.ShapeDtypeStruct((total_tokens, d), F32),
        grid=(nblk,),
        in_specs=in_specs,
        out_specs=pl.BlockSpec((tm, d), lambda i: (i + blk0, 0)),
        input_output_aliases=aliases,
        compiler_params=_cparams("arbitrary"),
        name="combine",
    )(*operands)


def _rope_tables(length, head_dim):
    rows = length // GRID_W
    pairs_axis = head_dim // 4
    r = jnp.repeat(jnp.arange(rows, dtype=F32), GRID_W)
    col = jnp.tile(jnp.arange(GRID_W, dtype=F32), rows)
    inv = ROPE_THETA ** (-jnp.arange(pairs_axis, dtype=F32) / pairs_axis)
    ang = jnp.concatenate([r[:, None] * inv, col[:, None] * inv], axis=-1)
    cos, sin = jnp.cos(ang), jnp.sin(ang)
    cos_h = jnp.concatenate([cos, cos], axis=-1)
    sin_h = jnp.concatenate([-sin, sin], axis=-1)
    reps = LANES // head_dim
    return jnp.tile(cos_h, (1, reps)), jnp.tile(sin_h, (1, reps))


def _part_batches(b):
    if b < 2:
        return [b]
    lead = max(1, min(b - 1, round(LEAD_PART_SHARE * b)))
    return [lead, b - lead]


def kernel(x, c, ctx, c_ctx, w_ada, b_ada, norm_mix, w_in, q_norm, k_norm, gm_norm, w_spatial, b_spatial,
           out_norm_attn, out_norm_gm, w_out, norm_ffn, w_router, router_bias, w_exp_gu, w_exp_down,
           w_sh_gu, w_sh_down):
    assert w_ada.shape[0] == 1, "single-layer kernel"
    b, length, d = x.shape
    head_dim = q_norm.shape[-1]
    att_w = out_norm_attn.shape[-1]
    gm_w = out_norm_gm.shape[-1]
    gm_groups = w_spatial.shape[1]
    gm_hd = gm_w // gm_groups
    n_exp = w_router.shape[-1]
    t = b * length
    assert head_dim == LANES // 2 and gm_hd == LANES // 2 and w_spatial.shape[-1] == CHUNK
    assert n_exp <= LANES and att_w % MXU_EDGE == 0 and gm_w % MXU_EDGE == 0

    mod_rows = 16
    c_all = jnp.zeros((mod_rows, d), F32).at[:b].set(c).at[b].set(c_ctx)
    mod = _ada(c_all, w_ada[0], b_ada)
    mod4 = mod.reshape(mod_rows, N_MOD, 1, d)

    w_in_b = w_in[0].astype(BF16)
    bd = (jnp.arange(MXU_EDGE)[:, None] // head_dim == jnp.arange(MXU_EDGE)[None, :] // head_dim)
    bd = (bd.astype(F32) / head_dim).astype(BF16)
    qg = jnp.tile(q_norm[0] * (head_dim ** -0.5 * LOG2_E), att_w // head_dim)[None, :]
    kg = jnp.tile(k_norm[0], LANES // head_dim)[None, :]
    gmg = jnp.tile(gm_norm[0], gm_groups)[None, :]
    ws = w_spatial[0].astype(BF16).reshape(gm_groups // 2, 2 * CHUNK, CHUNK)
    bs = jnp.repeat(b_spatial[0].T, gm_hd, axis=1)
    cos_t, sin_t = _rope_tables(length, head_dim)

    k_c, v_c = _ctx_kv(ctx, mod4, b, norm_mix, w_in_b[:, att_w:att_w + 2 * LANES], kg, bd)
    q, k_l, v_l, gmn = _inproj(x, mod4, norm_mix, w_in_b, cos_t, sin_t, qg, kg, gmg, bd, ws, bs,
                               out_norm_gm, att_w=att_w, gm_w=gm_w, tm=min(1024, length))
    wr_t = w_router[0].T.astype(BF16)
    rb = jnp.tile(router_bias[0][:, None], (1, LANES))
    w_out_b, w_sg_b, w_sd_b = w_out[0].astype(BF16), w_sh_gu[0].astype(BF16), w_sh_down[0].astype(BF16)
    gmn2, x2 = gmn.reshape(t, gm_w), x.reshape(t, d)
    tm_post = min(1024, length)
    tr = 512

    part_nb = _part_batches(b)
    n_parts = len(part_nb)
    part_b0 = [sum(part_nb[:p]) for p in range(n_parts)]
    state = [dict() for _ in range(n_parts)]

    last_tc = [None]

    def halves(part):
        nb = part_nb[part]
        return (nb // 2, nb - nb // 2) if nb >= 2 else (nb,)

    def attend(part, half):
        if half >= len(halves(part)):
            return
        st = state[part]
        slot0 = sum(halves(part)[:half])
        after = None if last_tc[0] is st.get("att") else last_tc[0]
        st["att"] = _attention(q, k_c, k_l, v_c, v_l, st.get("att"), after, tq=min(256, length),
                               b0=part_b0[part] + slot0, nb=halves(part)[half], slot0=slot0, slots=part_nb[part])
        last_tc[0] = st["att"]

    def route(part):
        st = state[part]
        tp = part_nb[part] * length
        nt = tp * TOP_K // tr + n_exp
        base, h2p, idx_t, gate, rank_t, cnt = _post(
            st["att"].reshape(tp, att_w), gmn2, x2, mod4, out_norm_attn, w_out_b, norm_ffn, wr_t, rb, w_sg_b, w_sd_b,
            tm=tm_post, tokens_per_batch=length, tok0=part_b0[part] * length)
        counts = cnt[:, 0].astype(jnp.int32)
        padded = (counts + tr - 1) // tr * tr
        ends = jnp.cumsum(padded)
        offs = ends - padded
        offs_pad = jnp.zeros((LANES,), jnp.int32).at[:n_exp].set(offs)
        xs, pos_t = _dispatch(h2p, idx_t, rank_t, offs_pad, n_slots=nt * tr)
        st.update(base=base, gate=gate, xs=xs, pos_t=pos_t, tile_lo=offs // tr, tile_hi=ends // tr)
        last_tc[0] = base

    def run_experts(part):
        st = state[part]
        ys = _experts(st["tile_lo"], st["tile_hi"], st["xs"], w_exp_gu[0], w_exp_down[0], last_tc[0], tr=tr)
        n_chunks = part_nb[part] if part == n_parts - 1 else 1
        chunk = part_nb[part] * length // n_chunks
        st["r"] = [(c * chunk, _collect(ys, st["pos_t"], tok_lo=c * chunk, n_tok=chunk)) for c in range(n_chunks)]
        last_tc[0] = ys

    result = [None]

    def finish(part):
        st = state[part]
        for part_lo, r in st["r"]:
            after = None if last_tc[0] is result[0] or last_tc[0] is st["base"] else last_tc[0]
            result[0] = _combine(st["base"], mod4, st["gate"], r, result[0], after, tm=min(512, length),
                                 tokens_per_batch=length, tok0=part_b0[part] * length, part_lo=part_lo,
                                 total_tokens=t)
            last_tc[0] = result[0]

    stages = [(0, lambda p: attend(p, 0)), (1, lambda p: attend(p, 1)), (2, route), (4, run_experts), (7, finish)]
    plan = sorted((when + 3 * part, part, n, fn) for part in range(n_parts) for n, (when, fn) in enumerate(stages))
    for _, part, _, fn in plan:
        fn(part)
    return result[0].reshape(b, length, d)
```

```python
import dataclasses
import functools

import jax
import jax.numpy as jnp
from jax import lax
from jax.experimental import pallas as pl
from jax.experimental.pallas import tpu as pltpu
from jax.experimental.pallas import tpu_sc as plsc

F32 = jnp.float32
BF16 = jnp.bfloat16

EPS = 1e-6
GRID_W = 64
ROPE_THETA = 10000.0
ATT_KV_HEADS = 2
TOP_K = 6
ROUTED_SCALE = 2.5
N_MOD = 6
CHUNK = 128
LOG2_E = 1.4426950408889634

LANES = 128
MXU_EDGE = 256
VMEM_LIMIT_BYTES = 56 * 1024 * 1024
SUBLANES = 8
CHOICE_ROWS = SUBLANES
SC_WINDOW = 64
LEAD_PART_SHARE = 0.75
EXPERT_IN_SLOTS = 4
EXPERT_OUT_SLOTS = 2


def _cparams(*sem):
    return pltpu.CompilerParams(dimension_semantics=sem, vmem_limit_bytes=VMEM_LIMIT_BYTES)


def _dot(a, b):
    return jnp.dot(a, b, preferred_element_type=F32)


def _dot_nt(a, b):
    return lax.dot_general(a, b, (((1,), (1,)), ((), ())), preferred_element_type=F32)


def _rms(x, g):
    return x * lax.rsqrt(jnp.mean(x * x, axis=-1, keepdims=True) + EPS) * g


def _group_mean_sq(x, bd_ref):
    x2 = (x * x).astype(BF16)
    parts = [_dot(x2[:, i:i + MXU_EDGE], bd_ref[...]) for i in range(0, x.shape[1], MXU_EDGE)]
    return parts[0] if len(parts) == 1 else jnp.concatenate(parts, axis=-1)


def _dup_halves(x, lane):
    xr = pltpu.roll(x, LANES // 2, axis=1)
    lo = lane < LANES // 2
    return jnp.where(lo, x, xr), jnp.where(lo, xr, x)


def _with_ones(x, lane):
    xr = pltpu.roll(x, LANES // 2, axis=1)
    lo = lane < LANES // 2
    return jnp.concatenate([jnp.where(lo, x, 1.0), jnp.where(lo, xr, 1.0)], axis=-1)


_HI16 = 0xFFFF0000


def _pack_rows(x):
    half = x.shape[1] // 2
    rounded = x.astype(BF16).astype(F32)
    bits = lax.bitcast_convert_type(rounded, jnp.uint32)
    word = (bits[:, :half] >> 16) | (bits[:, half:] & jnp.uint32(_HI16))
    return lax.bitcast_convert_type(word, jnp.int32)


def _unpack_rows(w):
    bits = lax.bitcast_convert_type(w, jnp.uint32)
    lo = lax.bitcast_convert_type(bits << 16, F32)
    hi = lax.bitcast_convert_type(bits & jnp.uint32(_HI16), F32)
    return lo, hi


def _ada_kernel(c_ref, w_ref, b_ref, o_ref):
    s = jax.nn.silu(c_ref[...])
    o_ref[...] = jnp.dot(s, w_ref[...], precision=lax.Precision.HIGHEST,
                         preferred_element_type=F32) + b_ref[...]


def _ada(c_all, w_ada, b_ada):
    rows, d = c_all.shape
    n = w_ada.shape[1]
    bn = 2048 if n % 2048 == 0 else 512
    return pl.pallas_call(
        _ada_kernel,
        out_shape=jax.ShapeDtypeStruct((rows, n), F32),
        grid=(n // bn,),
        in_specs=[pl.BlockSpec((rows, d), lambda j: (0, 0)),
                  pl.BlockSpec((d, bn), lambda j: (0, j)),
                  pl.BlockSpec((1, bn), lambda j: (0, j))],
        out_specs=pl.BlockSpec((rows, bn), lambda j: (0, j)),
        compiler_params=_cparams("arbitrary"),
        name="ada",
    )(c_all, w_ada, b_ada)


def _ctx_kv_kernel(x_ref, sh_ref, sc_ref, nmix_ref, w_ref, kg_ref, bd_ref, k_ref, v_ref):
    h = _rms(x_ref[...], nmix_ref[...]) * (1.0 + sc_ref[...]) + sh_ref[...]
    p = _dot(h.astype(BF16), w_ref[...])
    ms = _group_mean_sq(p, bd_ref)
    k = p[:, :LANES] * lax.rsqrt(ms[:, :LANES] + EPS) * kg_ref[...]
    v = p[:, LANES:]
    lane = lax.broadcasted_iota(jnp.int32, k.shape, 1)
    ka, kb = _dup_halves(k, lane)
    k_ref[...] = jnp.concatenate([ka, kb], axis=-1).astype(BF16)
    v_ref[...] = _with_ones(v, lane).astype(BF16)


def _ctx_kv(ctx, mod4, ctx_row, nmix, w_kv, kg, bd):
    b, n_ctx, d = ctx.shape
    kv_w = 2 * LANES
    return pl.pallas_call(
        _ctx_kv_kernel,
        out_shape=(jax.ShapeDtypeStruct((b, n_ctx, kv_w), BF16),) * 2,
        grid=(b,),
        in_specs=[pl.BlockSpec((None, n_ctx, d), lambda i: (i, 0, 0)),
                  pl.BlockSpec((None, None, 1, d), lambda i: (ctx_row, 0, 0, 0)),
                  pl.BlockSpec((None, None, 1, d), lambda i: (ctx_row, 1, 0, 0)),
                  pl.BlockSpec((1, d), lambda i: (0, 0)),
                  pl.BlockSpec(w_kv.shape, lambda i: (0, 0)),
                  pl.BlockSpec((1, LANES), lambda i: (0, 0)),
                  pl.BlockSpec(bd.shape, lambda i: (0, 0))],
        out_specs=(pl.BlockSpec((None, n_ctx, kv_w), lambda i: (i, 0, 0)),) * 2,
        compiler_params=_cparams("arbitrary"),
        name="ctx_kv",
    )(ctx, mod4, mod4, nmix, w_kv, kg, bd)


def _inproj_kernel(x_ref, sh_ref, sc_ref, nmix_ref, w_ref, cos_ref, sin_ref, qg_ref, kg_ref, gmg_ref,
                   bd_ref, ws_ref, bs_ref, og_ref, q_ref, k_ref, v_ref, gm_ref, *, att_w, gm_w):
    tm = x_ref.shape[0]
    h = _rms(x_ref[...], nmix_ref[...]) * (1.0 + sc_ref[...]) + sh_ref[...]
    p = _dot(h.astype(BF16), w_ref[...])

    cos1, sin1 = cos_ref[...], sin_ref[...]
    lane1 = lax.broadcasted_iota(jnp.int32, (tm, LANES), 1)

    def rope(xn, cosw, sinw, lanew):
        w = xn.shape[1]
        fwd = pltpu.roll(xn, w - 32, axis=1)
        bwd = pltpu.roll(xn, 32, axis=1)
        swapped = jnp.where((lanew & 63) < 32, fwd, bwd)
        return xn * cosw + swapped * sinw

    q = p[:, :att_w]
    qn = q * lax.rsqrt(_group_mean_sq(q, bd_ref) + EPS) * qg_ref[...]
    reps = att_w // LANES
    cosq = jnp.concatenate([cos1] * reps, axis=-1)
    sinq = jnp.concatenate([sin1] * reps, axis=-1)
    laneq = lax.broadcasted_iota(jnp.int32, (tm, att_w), 1)
    q_ref[...] = rope(qn, cosq, sinq, laneq).astype(BF16)

    kv = p[:, att_w:att_w + 2 * LANES]
    ms = _group_mean_sq(kv, bd_ref)
    kn = kv[:, :LANES] * lax.rsqrt(ms[:, :LANES] + EPS) * kg_ref[...]
    k = rope(kn, cos1, sin1, lane1)
    v = kv[:, LANES:]
    ka, kb = _dup_halves(k, lane1)
    k_ref[...] = jnp.concatenate([ka, kb], axis=-1).astype(BF16)
    v_ref[...] = _with_ones(v, lane1).astype(BF16)

    u0 = att_w + 2 * LANES
    u = jax.nn.gelu(p[:, u0:u0 + gm_w])
    gt = jax.nn.gelu(p[:, u0 + gm_w:u0 + 2 * gm_w])
    gt = (gt * lax.rsqrt(_group_mean_sq(gt, bd_ref) + EPS) * gmg_ref[...]).astype(BF16)
    lo_half = lax.broadcasted_iota(jnp.int32, (CHUNK, LANES), 1) < LANES // 2
    rows = []
    for c in range(tm // CHUNK):
        blocks = []
        for j in range(gm_w // LANES):
            g = gt[c * CHUNK:(c + 1) * CHUNK, j * LANES:(j + 1) * LANES]
            r = _dot(ws_ref[j], g)
            blocks.append(jnp.where(lo_half, r[:CHUNK], r[CHUNK:]))
        rows.append(jnp.concatenate(blocks, axis=-1) + bs_ref[...])
    mixed = jnp.concatenate(rows, axis=0)
    gm_ref[...] = _rms(u * mixed, og_ref[...]).astype(BF16)


def _inproj(x, mod4, nmix, w_in, cos_t, sin_t, qg, kg, gmg, bd, ws, bs, og, *, att_w, gm_w, tm):
    b, length, d = x.shape
    kv_w = 2 * LANES
    n_in = w_in.shape[1]
    const = lambda shape: pl.BlockSpec(shape, lambda i, j: (0,) * len(shape))
    return pl.pallas_call(
        functools.partial(_inproj_kernel, att_w=att_w, gm_w=gm_w),
        out_shape=(jax.ShapeDtypeStruct((b, length, att_w), BF16),
                   jax.ShapeDtypeStruct((b, length, kv_w), BF16),
                   jax.ShapeDtypeStruct((b, length, kv_w), BF16),
                   jax.ShapeDtypeStruct((b, length, gm_w), BF16)),
        grid=(b, length // tm),
        in_specs=[pl.BlockSpec((None, tm, d), lambda i, j: (i, j, 0)),
                  pl.BlockSpec((None, None, 1, d), lambda i, j: (i, 0, 0, 0)),
                  pl.BlockSpec((None, None, 1, d), lambda i, j: (i, 1, 0, 0)),
                  const((1, d)),
                  const((d, n_in)),
                  pl.BlockSpec((tm, LANES), lambda i, j: (j, 0)),
                  pl.BlockSpec((tm, LANES), lambda i, j: (j, 0)),
                  const((1, att_w)), const((1, LANES)), const((1, gm_w)),
                  const(bd.shape), const(ws.shape), const(bs.shape), const((1, gm_w))],
        out_specs=(pl.BlockSpec((None, tm, att_w), lambda i, j: (i, j, 0)),
                   pl.BlockSpec((None, tm, kv_w), lambda i, j: (i, j, 0)),
                   pl.BlockSpec((None, tm, kv_w), lambda i, j: (i, j, 0)),
                   pl.BlockSpec((None, tm, gm_w), lambda i, j: (i, j, 0))),
        compiler_params=_cparams("arbitrary", "arbitrary"),
        name="inproj",
    )(x, mod4, mod4, nmix, w_in, cos_t, sin_t, qg, kg, gmg, bd, ws, bs, og)


def _attn_kernel(q_ref, kc_ref, kl_ref, vc_ref, vl_ref, o_ref, s_buf, p_buf, *, tq):
    n_ctx = kc_ref.shape[0]
    heads = q_ref.shape[1] // (LANES // 2)
    nj = q_ref.shape[0] // tq
    half = LANES // 2
    lo = lax.broadcasted_iota(jnp.int32, (tq, LANES), 1) < half

    @pl.when(jnp.logical_and(pl.program_id(0) == 0, pl.program_id(1) == 0))
    def _():
        s_buf[...] = jnp.zeros_like(s_buf)
        p_buf[...] = jnp.ones_like(p_buf)

    def scores(j, g):
        q = q_ref[pl.ds(j * tq, tq), (g // 2) * LANES:(g // 2 + 1) * LANES]
        zero = jnp.zeros_like(q)
        qh = jnp.where(lo, q, zero) if g % 2 == 0 else jnp.where(lo, zero, q)
        s_buf[g % 2, :, :n_ctx] = _dot_nt(qh, kc_ref[...])
        s_buf[g % 2, :, n_ctx:] = _dot_nt(qh, kl_ref[...])

    def probs(g):
        s = s_buf[g % 2]
        p_buf[g % 2] = jnp.exp2(s - jnp.max(s, axis=-1, keepdims=True)).astype(BF16)

    def output(j, g):
        p = p_buf[g % 2]
        w = _dot(p[:, :n_ctx], vc_ref[...]) + _dot(p[:, n_ctx:], vl_ref[...])
        t = jnp.where(pl.program_id(1) == 0, w[:, :LANES], w[:, LANES:])
        r = pltpu.roll(t, half, axis=1)
        rows = pl.ds(j * tq, tq)
        if g % 2 == 0:
            o_ref[rows, g * half:(g + 1) * half] = (t / r)[:, :half].astype(o_ref.dtype)
        else:
            o_ref[rows, g * half:(g + 1) * half] = (r / t)[:, half:].astype(o_ref.dtype)

    def tile_stages(j):
        j_prev = jnp.maximum(j - 1, 0)
        for g in range(heads):
            scores(j, g)
            probs((g - 1) % heads)
            output(j if g >= 2 else j_prev, (g - 2) % heads)

    def trip(jj, carry):
        for u in range(tiles_per_trip):
            tile_stages(jj * tiles_per_trip + u)
        return carry

    tiles_per_trip = 2 if nj % 2 == 0 else 1
    lax.fori_loop(0, nj // tiles_per_trip, trip, 0)
    probs(heads - 1)
    output(nj - 1, heads - 2)
    output(nj - 1, heads - 1)


def _unread_operand(in_specs, operands, array):
    in_specs.append(pl.BlockSpec(memory_space=pl.ANY))
    operands.append(array)
    return len(operands) - 1


def _attention(q, k_c, k_l, v_c, v_l, att_so_far, after, *, tq, b0, nb, slot0, slots):
    _, length, att_w = q.shape
    n_ctx = k_c.shape[1]
    group_w = att_w // ATT_KV_HEADS
    assert (group_w // (LANES // 2)) % 2 == 0 and length % tq == 0
    k_blk = lambda n: pl.BlockSpec((None, n, LANES), lambda i, h: (i + b0, 0, h))
    v_blk = lambda n: pl.BlockSpec((None, n, ATT_KV_HEADS * LANES), lambda i, h: (i + b0, 0, 0))
    in_specs = [pl.BlockSpec((None, length, group_w), lambda i, h: (i + b0, 0, h)),
                k_blk(n_ctx), k_blk(length), v_blk(n_ctx), v_blk(length)]
    operands = [q, k_c, k_l, v_c, v_l]
    n_read = len(operands)
    aliases = {}
    if att_so_far is not None:
        aliases = {_unread_operand(in_specs, operands, att_so_far): 0}
    if after is not None:
        _unread_operand(in_specs, operands, after)
    n_in = len(operands)
    return pl.pallas_call(
        lambda *refs: _attn_kernel(*refs[:n_read], *refs[n_in:], tq=tq),
        out_shape=jax.ShapeDtypeStruct((slots, length, att_w), BF16),
        grid=(nb, ATT_KV_HEADS),
        in_specs=in_specs,
        out_specs=pl.BlockSpec((None, length, group_w), lambda i, h: (i + slot0, 0, h)),
        scratch_shapes=[pltpu.VMEM((2, tq, n_ctx + length), F32), pltpu.VMEM((2, tq, n_ctx + length), BF16)],
        input_output_aliases=aliases,
        compiler_params=_cparams("arbitrary", "arbitrary"),
        name="attn",
    )(*operands)


def _post_kernel(att_ref, gm_ref, x_ref, g1_ref, sh2_ref, sc2_ref, g2_ref, oa_ref, wo_ref, nffn_ref,
                 wrt_ref, rb_ref, wsg_ref, wsd_ref,
                 base_ref, h2p_ref, idx_ref, gate_ref, rank_ref, cnt_ref, carry_ref):
    i = pl.program_id(0)
    tm = x_ref.shape[0]
    n_exp = wrt_ref.shape[0]

    @pl.when(i == 0)
    def _():
        carry_ref[...] = jnp.zeros_like(carry_ref)

    att_n = _rms(att_ref[...].astype(F32), oa_ref[...]).astype(BF16)
    y = _dot(jnp.concatenate([att_n, gm_ref[...]], axis=-1), wo_ref[...])
    x_new = x_ref[...] + g1_ref[...] * y
    h2 = _rms(x_new, nffn_ref[...]) * (1.0 + sc2_ref[...]) + sh2_ref[...]
    h2p_ref[...] = _pack_rows(h2)

    h_hi = h2.astype(BF16)
    scores_t = jax.nn.sigmoid(_dot_nt(wrt_ref[...], h_hi))
    reps = tm // LANES
    sel = scores_t + jnp.concatenate([rb_ref[...]] * reps, axis=-1)
    expert = lax.broadcasted_iota(jnp.int32, scores_t.shape, 0).astype(F32)
    neg = jnp.float32(-jnp.inf)

    onehot = jnp.zeros(scores_t.shape, F32)
    hits, idxs, svals = [], [], []
    for _k in range(TOP_K):
        m = jnp.max(sel, axis=0, keepdims=True)
        idx = jnp.min(jnp.where(sel == m, expert, float(n_exp)), axis=0, keepdims=True)
        hit = expert == idx
        hits.append(hit)
        idxs.append(idx)
        svals.append(jnp.sum(jnp.where(hit, scores_t, 0.0), axis=0, keepdims=True))
        onehot = onehot + jnp.where(hit, 1.0, 0.0)
        sel = jnp.where(hit, neg, sel)
    ssum = functools.reduce(lambda a, b: a + b, svals)

    r_io = lax.broadcasted_iota(jnp.int32, (tm, tm), 0)
    c_io = lax.broadcasted_iota(jnp.int32, (tm, tm), 1)
    tri = jnp.where(r_io < c_io, 1.0, 0.0).astype(BF16)
    carry = carry_ref[...]
    before = _dot(onehot.astype(BF16), tri) + jnp.concatenate([carry] * reps, axis=-1)
    carry = carry + jnp.sum(onehot, axis=1, keepdims=True)
    carry_ref[...] = carry
    cnt_ref[...] = carry

    row = lax.broadcasted_iota(jnp.int32, (idx_ref.shape[0], tm), 0)
    idx_o = jnp.zeros(row.shape, F32)
    rank_o = jnp.zeros(row.shape, F32)
    gate_o = jnp.zeros(row.shape, F32)
    for k in range(TOP_K):
        rank = jnp.sum(jnp.where(hits[k], before, 0.0), axis=0, keepdims=True)
        idx_o = jnp.where(row == k, idxs[k], idx_o)
        rank_o = jnp.where(row == k, rank, rank_o)
        gate_o = jnp.where(row == k, svals[k] / ssum * ROUTED_SCALE, gate_o)
    idx_ref[...] = idx_o.astype(jnp.int32)
    rank_ref[...] = rank_o.astype(jnp.int32)
    gate_pad = jnp.concatenate([gate_o, jnp.zeros((LANES - gate_o.shape[0], tm), F32)], axis=0)
    gate_ref[...] = gate_pad.T

    gu = _dot(h_hi, wsg_ref[...])
    hs = gu.shape[1] // 2
    act = (jax.nn.silu(gu[:, :hs]) * gu[:, hs:]).astype(BF16)
    base_ref[...] = x_new + g2_ref[...] * _dot(act, wsd_ref[...])


def _post(att, gmn, x2, mod4, oa, w_out, nffn, wr_t, rb, w_sg, w_sd, *, tm, tokens_per_batch, tok0):
    t, att_w = att.shape
    d = x2.shape[1]
    n_exp = wr_t.shape[0]
    gm_w = gmn.shape[1]
    tpb = tokens_per_batch // tm
    blk0 = tok0 // tm
    const = lambda shape: pl.BlockSpec(shape, lambda i: (0,) * len(shape))
    mrow = lambda j: pl.BlockSpec((None, None, 1, d), lambda i: ((i + blk0) // tpb, j, 0, 0))
    by_choice = jax.ShapeDtypeStruct((CHOICE_ROWS, t), jnp.int32)
    return pl.pallas_call(
        _post_kernel,
        out_shape=(jax.ShapeDtypeStruct((t, d), F32), jax.ShapeDtypeStruct((t, d // 2), jnp.int32),
                   by_choice, jax.ShapeDtypeStruct((t, LANES), F32), by_choice,
                   jax.ShapeDtypeStruct((n_exp, LANES), F32)),
        grid=(t // tm,),
        in_specs=[pl.BlockSpec((tm, att_w), lambda i: (i, 0)),
                  pl.BlockSpec((tm, gm_w), lambda i: (i + blk0, 0)),
                  pl.BlockSpec((tm, d), lambda i: (i + blk0, 0)),
                  mrow(2), mrow(3), mrow(4), mrow(5),
                  const((1, att_w)), const(w_out.shape), const((1, d)),
                  const(wr_t.shape), const(rb.shape),
                  const(w_sg.shape), const(w_sd.shape)],
        out_specs=(pl.BlockSpec((tm, d), lambda i: (i, 0)), pl.BlockSpec((tm, d // 2), lambda i: (i, 0)),
                   pl.BlockSpec((CHOICE_ROWS, tm), lambda i: (0, i)), pl.BlockSpec((tm, LANES), lambda i: (i, 0)),
                   pl.BlockSpec((CHOICE_ROWS, tm), lambda i: (0, i)), pl.BlockSpec((n_exp, LANES), lambda i: (0, 0))),
        scratch_shapes=[pltpu.VMEM((n_exp, LANES), F32)],
        compiler_params=_cparams("arbitrary"),
        name="post",
    )(att, gmn, x2, mod4, mod4, mod4, mod4, oa, w_out, nffn, wr_t, rb, w_sg, w_sd)


def _sc_mesh_info():
    info = plsc.get_sparse_core_info()
    mesh = plsc.VectorSubcoreMesh(core_axis_name="c", subcore_axis_name="s")
    return mesh, info.num_cores, info.num_subcores, info.num_lanes


def _sc_params():
    cp = pltpu.CompilerParams()
    if "needs_layout_passes" in pltpu.CompilerParams.__dataclass_fields__:
        cp = dataclasses.replace(cp, needs_layout_passes=False)
    return cp


def _dispatch(h2p, idx_t, rank_t, offs, *, n_slots):
    t, dw = h2p.shape
    mesh, n_cores, n_sub, n_lanes = _sc_mesh_info()
    per_worker = t // (n_cores * n_sub)
    w = SC_WINDOW
    assert per_worker % w == 0 and w % n_lanes == 0

    @functools.partial(
        pl.kernel, mesh=mesh,
        out_type=(jax.ShapeDtypeStruct((n_slots, dw), jnp.int32), jax.ShapeDtypeStruct((CHOICE_ROWS, t), jnp.int32)),
        scratch_types=[pltpu.VMEM(offs.shape, jnp.int32)] + [pltpu.VMEM((TOP_K, w), jnp.int32)] * 3
        + [pltpu.VMEM((w, dw), jnp.int32), pltpu.SemaphoreType.DMA],
        compiler_params=_sc_params(),
    )
    def run(h_hbm, idx_hbm, rank_hbm, offs_hbm, xs_hbm, pos_hbm, offs_v, idx_v, rank_v, pos_v, rows_v, sem):
        base = (lax.axis_index("s") * n_cores + lax.axis_index("c")) * per_worker
        pltpu.sync_copy(offs_hbm, offs_v)

        @pl.loop(0, per_worker // w)
        def _(i):
            t0 = base + i * w
            pltpu.sync_copy(h_hbm.at[pl.ds(t0, w)], rows_v)
            for k in range(TOP_K):
                pltpu.sync_copy(idx_hbm.at[k, pl.ds(t0, w)], idx_v.at[k])
                pltpu.sync_copy(rank_hbm.at[k, pl.ds(t0, w)], rank_v.at[k])
            for k in range(TOP_K):
                for j in range(0, w, n_lanes):
                    group_start = plsc.load_gather(offs_v, [idx_v[k, pl.ds(j, n_lanes)]])
                    pos_v[k, pl.ds(j, n_lanes)] = group_start + rank_v[k, pl.ds(j, n_lanes)]
            copies = [pltpu.async_copy(rows_v, xs_hbm.at[pos_v.at[k]], sem) for k in range(TOP_K)]
            for k in range(TOP_K):
                pltpu.sync_copy(pos_v.at[k], pos_hbm.at[k, pl.ds(t0, w)])
            for cp in copies:
                cp.wait()

    return run(h2p, idx_t, rank_t, offs)


def _collect(ys, pos_t, *, tok_lo, n_tok):
    dw = ys.shape[1]
    mesh, n_cores, n_sub, _ = _sc_mesh_info()
    per_worker = n_tok // (n_cores * n_sub)
    w = SC_WINDOW
    assert per_worker % w == 0 and TOP_K % 2 == 0

    @functools.partial(
        pl.kernel, mesh=mesh,
        out_type=jax.ShapeDtypeStruct((TOP_K * n_tok, dw), jnp.int32),
        scratch_types=[pltpu.VMEM((TOP_K, w), jnp.int32), pltpu.VMEM((w, dw), jnp.int32),
                       pltpu.VMEM((w, dw), jnp.int32), pltpu.SemaphoreType.DMA, pltpu.SemaphoreType.DMA],
        compiler_params=_sc_params(),
    )
    def run(ys_hbm, pos_hbm, r_hbm, pos_v, rows_a, rows_b, sem_g, sem_w):
        base = (lax.axis_index("s") * n_cores + lax.axis_index("c")) * per_worker
        bufs = (rows_a, rows_b)

        @pl.loop(0, per_worker // w)
        def _(i):
            t0 = base + i * w
            for k in range(TOP_K):
                pltpu.sync_copy(pos_hbm.at[k, pl.ds(tok_lo + t0, w)], pos_v.at[k])
            gathers = [None] * TOP_K
            writes = [None] * TOP_K
            gathers[0] = pltpu.async_copy(ys_hbm.at[pos_v.at[0]], bufs[0], sem_g)
            for k in range(TOP_K):
                gathers[k].wait()
                if k + 1 < TOP_K:
                    if k >= 1:
                        writes[k - 1].wait()
                    gathers[k + 1] = pltpu.async_copy(ys_hbm.at[pos_v.at[k + 1]], bufs[(k + 1) % 2], sem_g)
                writes[k] = pltpu.async_copy(bufs[k % 2], r_hbm.at[pl.ds(k * n_tok + t0, w)], sem_w)
            writes[TOP_K - 2].wait()
            writes[TOP_K - 1].wait()

    return run(ys, pos_t)


def _experts_kernel(t0_ref, t1_ref, xs_hbm, wgu_ref, wdn_ref, ys_hbm, xbuf, ybuf, wgu_b, wdn_b, sem_in, sem_out):
    e = pl.program_id(0)
    n_exp = pl.num_programs(0)
    n_in, tr, half = xbuf.shape
    n_out = ybuf.shape[0]
    n_tiles = t1_ref[n_exp - 1]

    def fetch(t):
        slot = t & (n_in - 1)
        return pltpu.make_async_copy(xs_hbm.at[pl.ds(t * tr, tr)], xbuf.at[slot], sem_in.at[slot])

    def flush(t):
        slot = t & (n_out - 1)
        return pltpu.make_async_copy(ybuf.at[slot], ys_hbm.at[pl.ds(t * tr, tr)], sem_out.at[slot])

    @pl.when(e == 0)
    def _():
        for t in range(n_in - 1):
            @pl.when(t < n_tiles)
            def _():
                fetch(t).start()

    wgu_b[...] = wgu_ref[...].astype(BF16)
    wdn_b[...] = wdn_ref[...].astype(BF16)

    assert n_in == 4 and n_out == 2

    def fetch_ahead(t):
        @pl.when(t < n_tiles)
        def _():
            fetch(t).start()

    def drain(t):
        @pl.when(t >= 0)
        def _():
            flush(t).wait()

    def compute(t):
        lo, hi = _unpack_rows(xbuf[t & (n_in - 1)])
        gu = _dot(lo.astype(BF16), wgu_b[:half, :]) + _dot(hi.astype(BF16), wgu_b[half:, :])
        hh = gu.shape[1] // 2
        act = (jax.nn.silu(gu[:, :hh]) * gu[:, hh:]).astype(BF16)
        ybuf[t & (n_out - 1)] = _pack_rows(_dot(act, wdn_b[...]))

    t_lo, t_hi = t0_ref[e], t1_ref[e]
    n_pairs = (t_hi - t_lo) >> 1

    def pair(p, carry):
        t = t_lo + 2 * p
        fetch(t).wait()
        fetch(t + 1).wait()
        fetch_ahead(t + 3)
        drain(t - 2)
        drain(t - 1)
        compute(t)
        compute(t + 1)
        flush(t).start()
        flush(t + 1).start()
        fetch_ahead(t + 4)
        return carry

    lax.fori_loop(0, n_pairs, pair, 0)

    @pl.when(((t_hi - t_lo) & 1) == 1)
    def _():
        t = t_hi - 1
        fetch(t).wait()
        fetch_ahead(t + 3)
        drain(t - 2)
        compute(t)
        flush(t).start()

    @pl.when(e == n_exp - 1)
    def _():
        for back in range(n_out, 0, -1):
            @pl.when(n_tiles >= back)
            def _():
                flush(n_tiles - back).wait()


def _experts(tile_lo, tile_hi, xs, w_gu, w_dn, after, *, tr):
    n_exp, d, h2w = w_gu.shape
    hh = w_dn.shape[1]
    in_specs = [pl.BlockSpec(memory_space=pl.ANY),
                pl.BlockSpec((None, d, h2w), lambda e, lo, hi: (e, 0, 0)),
                pl.BlockSpec((None, hh, d), lambda e, lo, hi: (e, 0, 0))]
    operands = [xs, w_gu, w_dn]
    n_read = 2 + len(operands)
    if after is not None:
        _unread_operand(in_specs, operands, after)
    n_in = 2 + len(operands)
    return pl.pallas_call(
        lambda *refs: _experts_kernel(*refs[:n_read], *refs[n_in:]),
        out_shape=jax.ShapeDtypeStruct(xs.shape, jnp.int32),
        grid_spec=pltpu.PrefetchScalarGridSpec(
            num_scalar_prefetch=2,
            grid=(n_exp,),
            in_specs=in_specs,
            out_specs=pl.BlockSpec(memory_space=pl.ANY),
            scratch_shapes=[pltpu.VMEM((EXPERT_IN_SLOTS, tr, d // 2), jnp.int32),
                            pltpu.VMEM((EXPERT_OUT_SLOTS, tr, d // 2), jnp.int32),
                            pltpu.VMEM((d, h2w), BF16), pltpu.VMEM((hh, d), BF16),
                            pltpu.SemaphoreType.DMA((EXPERT_IN_SLOTS,)), pltpu.SemaphoreType.DMA((EXPERT_OUT_SLOTS,))]),
        compiler_params=_cparams("arbitrary"),
        name="experts",
    )(tile_lo, tile_hi, *operands)


def _combine_kernel(base_ref, g2_ref, gate_ref, *refs):
    r_refs, o_ref = refs[:TOP_K], refs[TOP_K]
    gate = gate_ref[...]
    half = base_ref.shape[1] // 2
    acc_lo = jnp.zeros((base_ref.shape[0], half), F32)
    acc_hi = jnp.zeros((base_ref.shape[0], half), F32)
    for k in range(TOP_K):
        lo, hi = _unpack_rows(r_refs[k][...])
        g = gate[:, k:k + 1]
        acc_lo = acc_lo + g * lo
        acc_hi = acc_hi + g * hi
    g2 = g2_ref[...]
    o_ref[:, :half] = base_ref[:, :half] + g2[:, :half] * acc_lo
    o_ref[:, half:] = base_ref[:, half:] + g2[:, half:] * acc_hi


def _combine(base, mod4, gate, r, out_so_far, after, *, tm, tokens_per_batch, tok0, part_lo, total_tokens):
    d = base.shape[1]
    t = r.shape[0] // TOP_K
    tpb = tokens_per_batch // tm
    nblk = t // tm
    pblk0 = part_lo // tm
    blk0 = (tok0 + part_lo) // tm
    r_spec = lambda k: pl.BlockSpec((tm, d // 2), lambda i: (k * nblk + i, 0))
    in_specs = [pl.BlockSpec((tm, d), lambda i: (i + pblk0, 0)),
                pl.BlockSpec((None, None, 1, d), lambda i: ((i + blk0) // tpb, 5, 0, 0)),
                pl.BlockSpec((tm, LANES), lambda i: (i + pblk0, 0))] + [r_spec(k) for k in range(TOP_K)]
    operands = [base, mod4, gate] + [r] * TOP_K
    n_read = len(operands)
    aliases = {}
    if out_so_far is not None:
        aliases = {_unread_operand(in_specs, operands, out_so_far): 0}
    if after is not None:
        _unread_operand(in_specs, operands, after)
    n_in = len(operands)
    return pl.pallas_call(
        lambda *refs: _combine_kernel(*refs[:n_read], *refs[n_in:]),
        out_shape=jax.ShapeDtypeStruct((total_tokens, d), F32),
        grid=(nblk,),
        in_specs=in_specs,
        out_specs=pl.BlockSpec((tm, d), lambda i: (i + blk0, 0)),
        input_output_aliases=aliases,
        compiler_params=_cparams("arbitrary"),
        name="combine",
    )(*operands)


def _rope_tables(length, head_dim):
    rows = length // GRID_W
    pairs_axis = head_dim // 4
    r = jnp.repeat(jnp.arange(rows, dtype=F32), GRID_W)
    col = jnp.tile(jnp.arange(GRID_W, dtype=F32), rows)
    inv = ROPE_THETA ** (-jnp.arange(pairs_axis, dtype=F32) / pairs_axis)
    ang = jnp.concatenate([r[:, None] * inv, col[:, None] * inv], axis=-1)
    cos, sin = jnp.cos(ang), jnp.sin(ang)
    cos_h = jnp.concatenate([cos, cos], axis=-1)
    sin_h = jnp.concatenate([-sin, sin], axis=-1)
    reps = LANES // head_dim
    return jnp.tile(cos_h, (1, reps)), jnp.tile(sin_h, (1, reps))


def _part_batches(b):
    if b < 2:
        return [b]
    lead = max(1, min(b - 1, round(LEAD_PART_SHARE * b)))
    return [lead, b - lead]


def kernel(x, c, ctx, c_ctx, w_ada, b_ada, norm_mix, w_in, q_norm, k_norm, gm_norm, w_spatial, b_spatial,
           out_norm_attn, out_norm_gm, w_out, norm_ffn, w_router, router_bias, w_exp_gu, w_exp_down,
           w_sh_gu, w_sh_down):
    assert w_ada.shape[0] == 1, "single-layer kernel"
    b, length, d = x.shape
    head_dim = q_norm.shape[-1]
    att_w = out_norm_attn.shape[-1]
    gm_w = out_norm_gm.shape[-1]
    gm_groups = w_spatial.shape[1]
    gm_hd = gm_w // gm_groups
    n_exp = w_router.shape[-1]
    t = b * length
    assert head_dim == LANES // 2 and gm_hd == LANES // 2 and w_spatial.shape[-1] == CHUNK
    assert n_exp <= LANES and att_w % MXU_EDGE == 0 and gm_w % MXU_EDGE == 0

    mod_rows = 16
    c_all = jnp.zeros((mod_rows, d), F32).at[:b].set(c).at[b].set(c_ctx)
    mod = _ada(c_all, w_ada[0], b_ada)
    mod4 = mod.reshape(mod_rows, N_MOD, 1, d)

    w_in_b = w_in[0].astype(BF16)
    bd = (jnp.arange(MXU_EDGE)[:, None] // head_dim == jnp.arange(MXU_EDGE)[None, :] // head_dim)
    bd = (bd.astype(F32) / head_dim).astype(BF16)
    qg = jnp.tile(q_norm[0] * (head_dim ** -0.5 * LOG2_E), att_w // head_dim)[None, :]
    kg = jnp.tile(k_norm[0], LANES // head_dim)[None, :]
    gmg = jnp.tile(gm_norm[0], gm_groups)[None, :]
    ws = w_spatial[0].astype(BF16).reshape(gm_groups // 2, 2 * CHUNK, CHUNK)
    bs = jnp.repeat(b_spatial[0].T, gm_hd, axis=1)
    cos_t, sin_t = _rope_tables(length, head_dim)

    k_c, v_c = _ctx_kv(ctx, mod4, b, norm_mix, w_in_b[:, att_w:att_w + 2 * LANES], kg, bd)
    q, k_l, v_l, gmn = _inproj(x, mod4, norm_mix, w_in_b, cos_t, sin_t, qg, kg, gmg, bd, ws, bs,
                               out_norm_gm, att_w=att_w, gm_w=gm_w, tm=min(1024, length))
    wr_t = w_router[0].T.astype(BF16)
    rb = jnp.tile(router_bias[0][:, None], (1, LANES))
    w_out_b, w_sg_b, w_sd_b = w_out[0].astype(BF16), w_sh_gu[0].astype(BF16), w_sh_down[0].astype(BF16)
    gmn2, x2 = gmn.reshape(t, gm_w), x.reshape(t, d)
    tm_post = min(1024, length)
    tr = 512

    part_nb = _part_batches(b)
    n_parts = len(part_nb)
    part_b0 = [sum(part_nb[:p]) for p in range(n_parts)]
    state = [dict() for _ in range(n_parts)]

    last_tc = [None]

    def halves(part):
        nb = part_nb[part]
        return (nb // 2, nb - nb // 2) if nb >= 2 else (nb,)

    def attend(part, half):
        if half >= len(halves(part)):
            return
        st = state[part]
        slot0 = sum(halves(part)[:half])
        after = None if last_tc[0] is st.get("att") else last_tc[0]
        st["att"] = _attention(q, k_c, k_l, v_c, v_l, st.get("att"), after, tq=min(256, length),
                               b0=part_b0[part] + slot0, nb=halves(part)[half], slot0=slot0, slots=part_nb[part])
        last_tc[0] = st["att"]

    def route(part):
        st = state[part]
        tp = part_nb[part] * length
        nt = tp * TOP_K // tr + n_exp
        base, h2p, idx_t, gate, rank_t, cnt = _post(
            st["att"].reshape(tp, att_w), gmn2, x2, mod4, out_norm_attn, w_out_b, norm_ffn, wr_t, rb, w_sg_b, w_sd_b,
            tm=tm_post, tokens_per_batch=length, tok0=part_b0[part] * length)
        counts = cnt[:, 0].astype(jnp.int32)
        padded = (counts + tr - 1) // tr * tr
        ends = jnp.cumsum(padded)
        offs = ends - padded
        offs_pad = jnp.zeros((LANES,), jnp.int32).at[:n_exp].set(offs)
        xs, pos_t = _dispatch(h2p, idx_t, rank_t, offs_pad, n_slots=nt * tr)
        st.update(base=base, gate=gate, xs=xs, pos_t=pos_t, tile_lo=offs // tr, tile_hi=ends // tr)
        last_tc[0] = base

    def run_experts(part):
        st = state[part]
        ys = _experts(st["tile_lo"], st["tile_hi"], st["xs"], w_exp_gu[0], w_exp_down[0], last_tc[0], tr=tr)
        n_chunks = part_nb[part] if part == n_parts - 1 else 1
        chunk = part_nb[part] * length // n_chunks
        st["r"] = [(c * chunk, _collect(ys, st["pos_t"], tok_lo=c * chunk, n_tok=chunk)) for c in range(n_chunks)]
        last_tc[0] = ys

    result = [None]

    def finish(part):
        st = state[part]
        for part_lo, r in st["r"]:
            after = None if last_tc[0] is result[0] or last_tc[0] is st["base"] else last_tc[0]
            result[0] = _combine(st["base"], mod4, st["gate"], r, result[0], after, tm=min(512, length),
                                 tokens_per_batch=length, tok0=part_b0[part] * length, part_lo=part_lo,
                                 total_tokens=t)
            last_tc[0] = result[0]

    stages = [(0, lambda p: attend(p, 0)), (1, lambda p: attend(p, 1)), (2, route), (4, run_experts), (7, finish)]
    plan = sorted((when + 3 * part, part, n, fn) for part in range(n_parts) for n, (when, fn) in enumerate(stages))
    for _, part, _, fn in plan:
        fn(part)
    return result[0].reshape(b, length, d)
```

```python
import dataclasses
import functools

import jax
import jax.numpy as jnp
from jax import lax
from jax.experimental import pallas as pl
from jax.experimental.pallas import tpu as pltpu
from jax.experimental.pallas import tpu_sc as plsc

F32 = jnp.float32
BF16 = jnp.bfloat16

EPS = 1e-6
GRID_W = 64
ROPE_THETA = 10000.0
ATT_KV_HEADS = 2
TOP_K = 6
ROUTED_SCALE = 2.5
N_MOD = 6
CHUNK = 128
LOG2_E = 1.4426950408889634

LANES = 128
MXU_EDGE = 256
VMEM_LIMIT_BYTES = 56 * 1024 * 1024
SUBLANES = 8
CHOICE_ROWS = SUBLANES
SC_WINDOW = 64
LEAD_PART_SHARE = 0.75
EXPERT_IN_SLOTS = 4
EXPERT_OUT_SLOTS = 4


def _cparams(*sem):
    return pltpu.CompilerParams(dimension_semantics=sem, vmem_limit_bytes=VMEM_LIMIT_BYTES)


def _dot(a, b):
    return jnp.dot(a, b, preferred_element_type=F32)


def _dot_nt(a, b):
    return lax.dot_general(a, b, (((1,), (1,)), ((), ())), preferred_element_type=F32)


def _rms(x, g):
    return x * lax.rsqrt(jnp.mean(x * x, axis=-1, keepdims=True) + EPS) * g


def _group_mean_sq(x, bd_ref):
    x2 = (x * x).astype(BF16)
    parts = [_dot(x2[:, i:i + MXU_EDGE], bd_ref[...]) for i in range(0, x.shape[1], MXU_EDGE)]
    return parts[0] if len(parts) == 1 else jnp.concatenate(parts, axis=-1)


def _dup_halves(x, lane):
    xr = pltpu.roll(x, LANES // 2, axis=1)
    lo = lane < LANES // 2
    return jnp.where(lo, x, xr), jnp.where(lo, xr, x)


def _with_ones(x, lane):
    xr = pltpu.roll(x, LANES // 2, axis=1)
    lo = lane < LANES // 2
    return jnp.concatenate([jnp.where(lo, x, 1.0), jnp.where(lo, xr, 1.0)], axis=-1)


_HI16 = 0xFFFF0000


def _pack_rows(x):
    half = x.shape[1] // 2
    rounded = x.astype(BF16).astype(F32)
    bits = lax.bitcast_convert_type(rounded, jnp.uint32)
    word = (bits[:, :half] >> 16) | (bits[:, half:] & jnp.uint32(_HI16))
    return lax.bitcast_convert_type(word, jnp.int32)


def _unpack_rows(w):
    bits = lax.bitcast_convert_type(w, jnp.uint32)
    lo = lax.bitcast_convert_type(bits << 16, F32)
    hi = lax.bitcast_convert_type(bits & jnp.uint32(_HI16), F32)
    return lo, hi


def _ada_kernel(c_ref, w_ref, b_ref, o_ref):
    s = jax.nn.silu(c_ref[...])
    o_ref[...] = jnp.dot(s, w_ref[...], precision=lax.Precision.HIGHEST,
                         preferred_element_type=F32) + b_ref[...]


def _ada(c_all, w_ada, b_ada):
    rows, d = c_all.shape
    n = w_ada.shape[1]
    bn = 2048 if n % 2048 == 0 else 512
    return pl.pallas_call(
        _ada_kernel,
        out_shape=jax.ShapeDtypeStruct((rows, n), F32),
        grid=(n // bn,),
        in_specs=[pl.BlockSpec((rows, d), lambda j: (0, 0)),
                  pl.BlockSpec((d, bn), lambda j: (0, j)),
                  pl.BlockSpec((1, bn), lambda j: (0, j))],
        out_specs=pl.BlockSpec((rows, bn), lambda j: (0, j)),
        compiler_params=_cparams("arbitrary"),
        name="ada",
    )(c_all, w_ada, b_ada)


def _ctx_kv_kernel(x_ref, sh_ref, sc_ref, nmix_ref, w_ref, kg_ref, bd_ref, k_ref, v_ref):
    h = _rms(x_ref[...], nmix_ref[...]) * (1.0 + sc_ref[...]) + sh_ref[...]
    p = _dot(h.astype(BF16), w_ref[...])
    ms = _group_mean_sq(p, bd_ref)
    k = p[:, :LANES] * lax.rsqrt(ms[:, :LANES] + EPS) * kg_ref[...]
    v = p[:, LANES:]
    lane = lax.broadcasted_iota(jnp.int32, k.shape, 1)
    ka, kb = _dup_halves(k, lane)
    k_ref[...] = jnp.concatenate([ka, kb], axis=-1).astype(BF16)
    v_ref[...] = _with_ones(v, lane).astype(BF16)


def _ctx_kv(ctx, mod4, ctx_row, nmix, w_kv, kg, bd):
    b, n_ctx, d = ctx.shape
    kv_w = 2 * LANES
    return pl.pallas_call(
        _ctx_kv_kernel,
        out_shape=(jax.ShapeDtypeStruct((b, n_ctx, kv_w), BF16),) * 2,
        grid=(b,),
        in_specs=[pl.BlockSpec((None, n_ctx, d), lambda i: (i, 0, 0)),
                  pl.BlockSpec((None, None, 1, d), lambda i: (ctx_row, 0, 0, 0)),
                  pl.BlockSpec((None, None, 1, d), lambda i: (ctx_row, 1, 0, 0)),
                  pl.BlockSpec((1, d), lambda i: (0, 0)),
                  pl.BlockSpec(w_kv.shape, lambda i: (0, 0)),
                  pl.BlockSpec((1, LANES), lambda i: (0, 0)),
                  pl.BlockSpec(bd.shape, lambda i: (0, 0))],
        out_specs=(pl.BlockSpec((None, n_ctx, kv_w), lambda i: (i, 0, 0)),) * 2,
        compiler_params=_cparams("arbitrary"),
        name="ctx_kv",
    )(ctx, mod4, mod4, nmix, w_kv, kg, bd)


def _inproj_kernel(x_ref, sh_ref, sc_ref, nmix_ref, w_ref, cos_ref, sin_ref, qg_ref, kg_ref, gmg_ref,
                   bd_ref, ws_ref, bs_ref, og_ref, q_ref, k_ref, v_ref, gm_ref, *, att_w, gm_w):
    tm = x_ref.shape[0]
    h = _rms(x_ref[...], nmix_ref[...]) * (1.0 + sc_ref[...]) + sh_ref[...]
    p = _dot(h.astype(BF16), w_ref[...])

    cos1, sin1 = cos_ref[...], sin_ref[...]
    lane1 = lax.broadcasted_iota(jnp.int32, (tm, LANES), 1)

    def rope(xn, cosw, sinw, lanew):
        w = xn.shape[1]
        fwd = pltpu.roll(xn, w - 32, axis=1)
        bwd = pltpu.roll(xn, 32, axis=1)
        swapped = jnp.where((lanew & 63) < 32, fwd, bwd)
        return xn * cosw + swapped * sinw

    q = p[:, :att_w]
    qn = q * lax.rsqrt(_group_mean_sq(q, bd_ref) + EPS) * qg_ref[...]
    reps = att_w // LANES
    cosq = jnp.concatenate([cos1] * reps, axis=-1)
    sinq = jnp.concatenate([sin1] * reps, axis=-1)
    laneq = lax.broadcasted_iota(jnp.int32, (tm, att_w), 1)
    q_ref[...] = rope(qn, cosq, sinq, laneq).astype(BF16)

    kv = p[:, att_w:att_w + 2 * LANES]
    ms = _group_mean_sq(kv, bd_ref)
    kn = kv[:, :LANES] * lax.rsqrt(ms[:, :LANES] + EPS) * kg_ref[...]
    k = rope(kn, cos1, sin1, lane1)
    v = kv[:, LANES:]
    ka, kb = _dup_halves(k, lane1)
    k_ref[...] = jnp.concatenate([ka, kb], axis=-1).astype(BF16)
    v_ref[...] = _with_ones(v, lane1).astype(BF16)

    u0 = att_w + 2 * LANES
    u = jax.nn.gelu(p[:, u0:u0 + gm_w])
    gt = jax.nn.gelu(p[:, u0 + gm_w:u0 + 2 * gm_w])
    gt = (gt * lax.rsqrt(_group_mean_sq(gt, bd_ref) + EPS) * gmg_ref[...]).astype(BF16)
    lo_half = lax.broadcasted_iota(jnp.int32, (CHUNK, LANES), 1) < LANES // 2
    rows = []
    for c in range(tm // CHUNK):
        blocks = []
        for j in range(gm_w // LANES):
            g = gt[c * CHUNK:(c + 1) * CHUNK, j * LANES:(j + 1) * LANES]
            r = _dot(ws_ref[j], g)
            blocks.append(jnp.where(lo_half, r[:CHUNK], r[CHUNK:]))
        rows.append(jnp.concatenate(blocks, axis=-1) + bs_ref[...])
    mixed = jnp.concatenate(rows, axis=0)
    gm_ref[...] = _rms(u * mixed, og_ref[...]).astype(BF16)


def _inproj(x, mod4, nmix, w_in, cos_t, sin_t, qg, kg, gmg, bd, ws, bs, og, *, att_w, gm_w, tm):
    b, length, d = x.shape
    kv_w = 2 * LANES
    n_in = w_in.shape[1]
    const = lambda shape: pl.BlockSpec(shape, lambda i, j: (0,) * len(shape))
    return pl.pallas_call(
        functools.partial(_inproj_kernel, att_w=att_w, gm_w=gm_w),
        out_shape=(jax.ShapeDtypeStruct((b, length, att_w), BF16),
                   jax.ShapeDtypeStruct((b, length, kv_w), BF16),
                   jax.ShapeDtypeStruct((b, length, kv_w), BF16),
                   jax.ShapeDtypeStruct((b, length, gm_w), BF16)),
        grid=(b, length // tm),
        in_specs=[pl.BlockSpec((None, tm, d), lambda i, j: (i, j, 0)),
                  pl.BlockSpec((None, None, 1, d), lambda i, j: (i, 0, 0, 0)),
                  pl.BlockSpec((None, None, 1, d), lambda i, j: (i, 1, 0, 0)),
                  const((1, d)),
                  const((d, n_in)),
                  pl.BlockSpec((tm, LANES), lambda i, j: (j, 0)),
                  pl.BlockSpec((tm, LANES), lambda i, j: (j, 0)),
                  const((1, att_w)), const((1, LANES)), const((1, gm_w)),
                  const(bd.shape), const(ws.shape), const(bs.shape), const((1, gm_w))],
        out_specs=(pl.BlockSpec((None, tm, att_w), lambda i, j: (i, j, 0)),
                   pl.BlockSpec((None, tm, kv_w), lambda i, j: (i, j, 0)),
                   pl.BlockSpec((None, tm, kv_w), lambda i, j: (i, j, 0)),
                   pl.BlockSpec((None, tm, gm_w), lambda i, j: (i, j, 0))),
        compiler_params=_cparams("arbitrary", "arbitrary"),
        name="inproj",
    )(x, mod4, mod4, nmix, w_in, cos_t, sin_t, qg, kg, gmg, bd, ws, bs, og)


def _attn_kernel(q_ref, kc_ref, kl_ref, vc_ref, vl_ref, o_ref, s_buf, p_buf, *, tq):
    n_ctx = kc_ref.shape[0]
    heads = q_ref.shape[1] // (LANES // 2)
    nj = q_ref.shape[0] // tq
    half = LANES // 2
    lo = lax.broadcasted_iota(jnp.int32, (tq, LANES), 1) < half

    @pl.when(jnp.logical_and(pl.program_id(0) == 0, pl.program_id(1) == 0))
    def _():
        s_buf[...] = jnp.zeros_like(s_buf)
        p_buf[...] = jnp.ones_like(p_buf)

    def scores(j, g):
        q = q_ref[pl.ds(j * tq, tq), (g // 2) * LANES:(g // 2 + 1) * LANES]
        zero = jnp.zeros_like(q)
        qh = jnp.where(lo, q, zero) if g % 2 == 0 else jnp.where(lo, zero, q)
        s_buf[g % 2, :, :n_ctx] = _dot_nt(qh, kc_ref[...])
        s_buf[g % 2, :, n_ctx:] = _dot_nt(qh, kl_ref[...])

    def probs(g):
        s = s_buf[g % 2]
        p_buf[g % 2] = jnp.exp2(s - jnp.max(s, axis=-1, keepdims=True)).astype(BF16)

    def output(j, g):
        p = p_buf[g % 2]
        w = _dot(p[:, :n_ctx], vc_ref[...]) + _dot(p[:, n_ctx:], vl_ref[...])
        t = jnp.where(pl.program_id(1) == 0, w[:, :LANES], w[:, LANES:])
        r = pltpu.roll(t, half, axis=1)
        rows = pl.ds(j * tq, tq)
        if g % 2 == 0:
            o_ref[rows, g * half:(g + 1) * half] = (t / r)[:, :half].astype(o_ref.dtype)
        else:
            o_ref[rows, g * half:(g + 1) * half] = (r / t)[:, half:].astype(o_ref.dtype)

    def tile_stages(j):
        j_prev = jnp.maximum(j - 1, 0)
        for g in range(heads):
            scores(j, g)
            probs((g - 1) % heads)
            output(j if g >= 2 else j_prev, (g - 2) % heads)

    def trip(jj, carry):
        for u in range(tiles_per_trip):
            tile_stages(jj * tiles_per_trip + u)
        return carry

    tiles_per_trip = 2 if nj % 2 == 0 else 1
    lax.fori_loop(0, nj // tiles_per_trip, trip, 0)
    probs(heads - 1)
    output(nj - 1, heads - 2)
    output(nj - 1, heads - 1)


def _unread_operand(in_specs, operands, array):
    in_specs.append(pl.BlockSpec(memory_space=pl.ANY))
    operands.append(array)
    return len(operands) - 1


def _attention(q, k_c, k_l, v_c, v_l, att_so_far, after, *, tq, b0, nb, slot0, slots):
    _, length, att_w = q.shape
    n_ctx = k_c.shape[1]
    group_w = att_w // ATT_KV_HEADS
    assert (group_w // (LANES // 2)) % 2 == 0 and length % tq == 0
    k_blk = lambda n: pl.BlockSpec((None, n, LANES), lambda i, h: (i + b0, 0, h))
    v_blk = lambda n: pl.BlockSpec((None, n, ATT_KV_HEADS * LANES), lambda i, h: (i + b0, 0, 0))
    in_specs = [pl.BlockSpec((None, length, group_w), lambda i, h: (i + b0, 0, h)),
                k_blk(n_ctx), k_blk(length), v_blk(n_ctx), v_blk(length)]
    operands = [q, k_c, k_l, v_c, v_l]
    n_read = len(operands)
    aliases = {}
    if att_so_far is not None:
        aliases = {_unread_operand(in_specs, operands, att_so_far): 0}
    if after is not None:
        _unread_operand(in_specs, operands, after)
    n_in = len(operands)
    return pl.pallas_call(
        lambda *refs: _attn_kernel(*refs[:n_read], *refs[n_in:], tq=tq),
        out_shape=jax.ShapeDtypeStruct((slots, length, att_w), BF16),
        grid=(nb, ATT_KV_HEADS),
        in_specs=in_specs,
        out_specs=pl.BlockSpec((None, length, group_w), lambda i, h: (i + slot0, 0, h)),
        scratch_shapes=[pltpu.VMEM((2, tq, n_ctx + length), F32), pltpu.VMEM((2, tq, n_ctx + length), BF16)],
        input_output_aliases=aliases,
        compiler_params=_cparams("arbitrary", "arbitrary"),
        name="attn",
    )(*operands)


def _post_kernel(att_ref, gm_ref, x_ref, g1_ref, sh2_ref, sc2_ref, g2_ref, oa_ref, wo_ref, nffn_ref,
                 wrt_ref, rb_ref, wsg_ref, wsd_ref,
                 base_ref, h2p_ref, idx_ref, gate_ref, rank_ref, cnt_ref, carry_ref):
    i = pl.program_id(0)
    tm = x_ref.shape[0]
    n_exp = wrt_ref.shape[0]

    @pl.when(i == 0)
    def _():
        carry_ref[...] = jnp.zeros_like(carry_ref)

    att_n = _rms(att_ref[...].astype(F32), oa_ref[...]).astype(BF16)
    y = _dot(jnp.concatenate([att_n, gm_ref[...]], axis=-1), wo_ref[...])
    x_new = x_ref[...] + g1_ref[...] * y
    h2 = _rms(x_new, nffn_ref[...]) * (1.0 + sc2_ref[...]) + sh2_ref[...]
    h2p_ref[...] = _pack_rows(h2)

    h_hi = h2.astype(BF16)
    scores_t = jax.nn.sigmoid(_dot_nt(wrt_ref[...], h_hi))
    reps = tm // LANES
    sel = scores_t + jnp.concatenate([rb_ref[...]] * reps, axis=-1)
    expert = lax.broadcasted_iota(jnp.int32, scores_t.shape, 0).astype(F32)
    neg = jnp.float32(-jnp.inf)

    onehot = jnp.zeros(scores_t.shape, F32)
    hits, idxs, svals = [], [], []
    for _k in range(TOP_K):
        m = jnp.max(sel, axis=0, keepdims=True)
        idx = jnp.min(jnp.where(sel == m, expert, float(n_exp)), axis=0, keepdims=True)
        hit = expert == idx
        hits.append(hit)
        idxs.append(idx)
        svals.append(jnp.sum(jnp.where(hit, scores_t, 0.0), axis=0, keepdims=True))
        onehot = onehot + jnp.where(hit, 1.0, 0.0)
        sel = jnp.where(hit, neg, sel)
    ssum = functools.reduce(lambda a, b: a + b, svals)

    r_io = lax.broadcasted_iota(jnp.int32, (tm, tm), 0)
    c_io = lax.broadcasted_iota(jnp.int32, (tm, tm), 1)
    tri = jnp.where(r_io < c_io, 1.0, 0.0).astype(BF16)
    carry = carry_ref[...]
    before = _dot(onehot.astype(BF16), tri) + jnp.concatenate([carry] * reps, axis=-1)
    carry = carry + jnp.sum(onehot, axis=1, keepdims=True)
    carry_ref[...] = carry
    cnt_ref[...] = carry

    row = lax.broadcasted_iota(jnp.int32, (idx_ref.shape[0], tm), 0)
    idx_o = jnp.zeros(row.shape, F32)
    rank_o = jnp.zeros(row.shape, F32)
    gate_o = jnp.zeros(row.shape, F32)
    for k in range(TOP_K):
        rank = jnp.sum(jnp.where(hits[k], before, 0.0), axis=0, keepdims=True)
        idx_o = jnp.where(row == k, idxs[k], idx_o)
        rank_o = jnp.where(row == k, rank, rank_o)
        gate_o = jnp.where(row == k, svals[k] / ssum * ROUTED_SCALE, gate_o)
    idx_ref[...] = idx_o.astype(jnp.int32)
    rank_ref[...] = rank_o.astype(jnp.int32)
    gate_pad = jnp.concatenate([gate_o, jnp.zeros((LANES - gate_o.shape[0], tm), F32)], axis=0)
    gate_ref[...] = gate_pad.T

    gu = _dot(h_hi, wsg_ref[...])
    hs = gu.shape[1] // 2
    act = (jax.nn.silu(gu[:, :hs]) * gu[:, hs:]).astype(BF16)
    base_ref[...] = x_new + g2_ref[...] * _dot(act, wsd_ref[...])


def _post(att, gmn, x2, mod4, oa, w_out, nffn, wr_t, rb, w_sg, w_sd, *, tm, tokens_per_batch, tok0):
    t, att_w = att.shape
    d = x2.shape[1]
    n_exp = wr_t.shape[0]
    gm_w = gmn.shape[1]
    tpb = tokens_per_batch // tm
    blk0 = tok0 // tm
    const = lambda shape: pl.BlockSpec(shape, lambda i: (0,) * len(shape))
    mrow = lambda j: pl.BlockSpec((None, None, 1, d), lambda i: ((i + blk0) // tpb, j, 0, 0))
    by_choice = jax.ShapeDtypeStruct((CHOICE_ROWS, t), jnp.int32)
    return pl.pallas_call(
        _post_kernel,
        out_shape=(jax.ShapeDtypeStruct((t, d), F32), jax.ShapeDtypeStruct((t, d // 2), jnp.int32),
                   by_choice, jax.ShapeDtypeStruct((t, LANES), F32), by_choice,
                   jax.ShapeDtypeStruct((n_exp, LANES), F32)),
        grid=(t // tm,),
        in_specs=[pl.BlockSpec((tm, att_w), lambda i: (i, 0)),
                  pl.BlockSpec((tm, gm_w), lambda i: (i + blk0, 0)),
                  pl.BlockSpec((tm, d), lambda i: (i + blk0, 0)),
                  mrow(2), mrow(3), mrow(4), mrow(5),
                  const((1, att_w)), const(w_out.shape), const((1, d)),
                  const(wr_t.shape), const(rb.shape),
                  const(w_sg.shape), const(w_sd.shape)],
        out_specs=(pl.BlockSpec((tm, d), lambda i: (i, 0)), pl.BlockSpec((tm, d // 2), lambda i: (i, 0)),
                   pl.BlockSpec((CHOICE_ROWS, tm), lambda i: (0, i)), pl.BlockSpec((tm, LANES), lambda i: (i, 0)),
                   pl.BlockSpec((CHOICE_ROWS, tm), lambda i: (0, i)), pl.BlockSpec((n_exp, LANES), lambda i: (0, 0))),
        scratch_shapes=[pltpu.VMEM((n_exp, LANES), F32)],
        compiler_params=_cparams("arbitrary"),
        name="post",
    )(att, gmn, x2, mod4, mod4, mod4, mod4, oa, w_out, nffn, wr_t, rb, w_sg, w_sd)


def _sc_mesh_info():
    info = plsc.get_sparse_core_info()
    mesh = plsc.VectorSubcoreMesh(core_axis_name="c", subcore_axis_name="s")
    return mesh, info.num_cores, info.num_subcores, info.num_lanes


def _sc_params():
    cp = pltpu.CompilerParams()
    if "needs_layout_passes" in pltpu.CompilerParams.__dataclass_fields__:
        cp = dataclasses.replace(cp, needs_layout_passes=False)
    return cp


def _dispatch(h2p, idx_t, rank_t, offs, *, n_slots):
    t, dw = h2p.shape
    mesh, n_cores, n_sub, n_lanes = _sc_mesh_info()
    per_worker = t // (n_cores * n_sub)
    w = SC_WINDOW
    assert per_worker % w == 0 and w % n_lanes == 0

    @functools.partial(
        pl.kernel, mesh=mesh,
        out_type=(jax.ShapeDtypeStruct((n_slots, dw), jnp.int32), jax.ShapeDtypeStruct((CHOICE_ROWS, t), jnp.int32)),
        scratch_types=[pltpu.VMEM(offs.shape, jnp.int32)] + [pltpu.VMEM((TOP_K, w), jnp.int32)] * 3
        + [pltpu.VMEM((w, dw), jnp.int32), pltpu.SemaphoreType.DMA],
        compiler_params=_sc_params(),
    )
    def run(h_hbm, idx_hbm, rank_hbm, offs_hbm, xs_hbm, pos_hbm, offs_v, idx_v, rank_v, pos_v, rows_v, sem):
        base = (lax.axis_index("s") * n_cores + lax.axis_index("c")) * per_worker
        pltpu.sync_copy(offs_hbm, offs_v)

        @pl.loop(0, per_worker // w)
        def _(i):
            t0 = base + i * w
            pltpu.sync_copy(h_hbm.at[pl.ds(t0, w)], rows_v)
            for k in range(TOP_K):
                pltpu.sync_copy(idx_hbm.at[k, pl.ds(t0, w)], idx_v.at[k])
                pltpu.sync_copy(rank_hbm.at[k, pl.ds(t0, w)], rank_v.at[k])
            for k in range(TOP_K):
                for j in range(0, w, n_lanes):
                    group_start = plsc.load_gather(offs_v, [idx_v[k, pl.ds(j, n_lanes)]])
                    pos_v[k, pl.ds(j, n_lanes)] = group_start + rank_v[k, pl.ds(j, n_lanes)]
            copies = [pltpu.async_copy(rows_v, xs_hbm.at[pos_v.at[k]], sem) for k in range(TOP_K)]
            for k in range(TOP_K):
                pltpu.sync_copy(pos_v.at[k], pos_hbm.at[k, pl.ds(t0, w)])
            for cp in copies:
                cp.wait()

    return run(h2p, idx_t, rank_t, offs)


def _collect(ys, pos_t, *, tok_lo, n_tok):
    dw = ys.shape[1]
    mesh, n_cores, n_sub, _ = _sc_mesh_info()
    per_worker = n_tok // (n_cores * n_sub)
    w = SC_WINDOW
    assert per_worker % w == 0 and TOP_K % 2 == 0

    @functools.partial(
        pl.kernel, mesh=mesh,
        out_type=jax.ShapeDtypeStruct((TOP_K * n_tok, dw), jnp.int32),
        scratch_types=[pltpu.VMEM((TOP_K, w), jnp.int32), pltpu.VMEM((w, dw), jnp.int32),
                       pltpu.VMEM((w, dw), jnp.int32), pltpu.SemaphoreType.DMA, pltpu.SemaphoreType.DMA],
        compiler_params=_sc_params(),
    )
    def run(ys_hbm, pos_hbm, r_hbm, pos_v, rows_a, rows_b, sem_g, sem_w):
        base = (lax.axis_index("s") * n_cores + lax.axis_index("c")) * per_worker
        bufs = (rows_a, rows_b)

        @pl.loop(0, per_worker // w)
        def _(i):
            t0 = base + i * w
            for k in range(TOP_K):
                pltpu.sync_copy(pos_hbm.at[k, pl.ds(tok_lo + t0, w)], pos_v.at[k])
            gathers = [None] * TOP_K
            writes = [None] * TOP_K
            gathers[0] = pltpu.async_copy(ys_hbm.at[pos_v.at[0]], bufs[0], sem_g)
            for k in range(TOP_K):
                gathers[k].wait()
                if k + 1 < TOP_K:
                    if k >= 1:
                        writes[k - 1].wait()
                    gathers[k + 1] = pltpu.async_copy(ys_hbm.at[pos_v.at[k + 1]], bufs[(k + 1) % 2], sem_g)
                writes[k] = pltpu.async_copy(bufs[k % 2], r_hbm.at[pl.ds(k * n_tok + t0, w)], sem_w)
            writes[TOP_K - 2].wait()
            writes[TOP_K - 1].wait()

    return run(ys, pos_t)


def _experts_kernel(t0_ref, t1_ref, xs_hbm, wgu_ref, wdn_ref, ys_hbm, xbuf, ybuf, wgu_b, wdn_b, sem_in, sem_out):
    e = pl.program_id(0)
    n_exp = pl.num_programs(0)
    n_in, tr, half = xbuf.shape
    n_out = ybuf.shape[0]
    n_tiles = t1_ref[n_exp - 1]

    def fetch(t):
        slot = t & (n_in - 1)
        return pltpu.make_async_copy(xs_hbm.at[pl.ds(t * tr, tr)], xbuf.at[slot], sem_in.at[slot])

    def flush(t):
        slot = t & (n_out - 1)
        return pltpu.make_async_copy(ybuf.at[slot], ys_hbm.at[pl.ds(t * tr, tr)], sem_out.at[slot])

    @pl.when(e == 0)
    def _():
        for t in range(n_in - 1):
            @pl.when(t < n_tiles)
            def _():
                fetch(t).start()

    wgu_b[...] = wgu_ref[...].astype(BF16)
    wdn_b[...] = wdn_ref[...].astype(BF16)

    assert n_in == 4 and n_out == 4

    def fetch_ahead(t):
        @pl.when(t < n_tiles)
        def _():
            fetch(t).start()

    def drain(t):
        @pl.when(t >= 0)
        def _():
            flush(t).wait()

    def compute(t):
        lo, hi = _unpack_rows(xbuf[t & (n_in - 1)])
        gu = _dot(lo.astype(BF16), wgu_b[:half, :]) + _dot(hi.astype(BF16), wgu_b[half:, :])
        hh = gu.shape[1] // 2
        act = (jax.nn.silu(gu[:, :hh]) * gu[:, hh:]).astype(BF16)
        ybuf[t & (n_out - 1)] = _pack_rows(_dot(act, wdn_b[...]))

    t_lo, t_hi = t0_ref[e], t1_ref[e]
    n_pairs = (t_hi - t_lo) >> 1

    def pair(p, carry):
        t = t_lo + 2 * p
        fetch(t).wait()
        fetch(t + 1).wait()
        fetch_ahead(t + 3)
        drain(t - 4)
        drain(t - 3)
        compute(t)
        compute(t + 1)
        flush(t).start()
        flush(t + 1).start()
        fetch_ahead(t + 4)
        return carry

    lax.fori_loop(0, n_pairs, pair, 0)

    @pl.when(((t_hi - t_lo) & 1) == 1)
    def _():
        t = t_hi - 1
        fetch(t).wait()
        fetch_ahead(t + 3)
        drain(t - 4)
        compute(t)
        flush(t).start()

    @pl.when(e == n_exp - 1)
    def _():
        for back in range(n_out, 0, -1):
            @pl.when(n_tiles >= back)
            def _():
                flush(n_tiles - back).wait()


def _experts(tile_lo, tile_hi, xs, w_gu, w_dn, after, *, tr):
    n_exp, d, h2w = w_gu.shape
    hh = w_dn.shape[1]
    in_specs = [pl.BlockSpec(memory_space=pl.ANY),
                pl.BlockSpec((None, d, h2w), lambda e, lo, hi: (e, 0, 0)),
                pl.BlockSpec((None, hh, d), lambda e, lo, hi: (e, 0, 0))]
    operands = [xs, w_gu, w_dn]
    n_read = 2 + len(operands)
    if after is not None:
        _unread_operand(in_specs, operands, after)
    n_in = 2 + len(operands)
    return pl.pallas_call(
        lambda *refs: _experts_kernel(*refs[:n_read], *refs[n_in:]),
        out_shape=jax.ShapeDtypeStruct(xs.shape, jnp.int32),
        grid_spec=pltpu.PrefetchScalarGridSpec(
            num_scalar_prefetch=2,
            grid=(n_exp,),
            in_specs=in_specs,
            out_specs=pl.BlockSpec(memory_space=pl.ANY),
            scratch_shapes=[pltpu.VMEM((EXPERT_IN_SLOTS, tr, d // 2), jnp.int32),
                            pltpu.VMEM((EXPERT_OUT_SLOTS, tr, d // 2), jnp.int32),
                            pltpu.VMEM((d, h2w), BF16), pltpu.VMEM((hh, d), BF16),
                            pltpu.SemaphoreType.DMA((EXPERT_IN_SLOTS,)), pltpu.SemaphoreType.DMA((EXPERT_OUT_SLOTS,))]),
        compiler_params=_cparams("arbitrary"),
        name="experts",
    )(tile_lo, tile_hi, *operands)


def _combine_kernel(base_ref, g2_ref, gate_ref, *refs):
    r_refs, o_ref = refs[:TOP_K], refs[TOP_K]
    gate = gate_ref[...]
    half = base_ref.shape[1] // 2
    acc_lo = jnp.zeros((base_ref.shape[0], half), F32)
    acc_hi = jnp.zeros((base_ref.shape[0], half), F32)
    for k in range(TOP_K):
        lo, hi = _unpack_rows(r_refs[k][...])
        g = gate[:, k:k + 1]
        acc_lo = acc_lo + g * lo
        acc_hi = acc_hi + g * hi
    g2 = g2_ref[...]
    o_ref[:, :half] = base_ref[:, :half] + g2[:, :half] * acc_lo
    o_ref[:, half:] = base_ref[:, half:] + g2[:, half:] * acc_hi


def _combine(base, mod4, gate, r, out_so_far, after, *, tm, tokens_per_batch, tok0, part_lo, total_tokens):
    d = base.shape[1]
    t = r.shape[0] // TOP_K
    tpb = tokens_per_batch // tm
    nblk = t // tm
    pblk0 = part_lo // tm
    blk0 = (tok0 + part_lo) // tm
    r_spec = lambda k: pl.BlockSpec((tm, d // 2), lambda i: (k * nblk + i, 0))
    in_specs = [pl.BlockSpec((tm, d), lambda i: (i + pblk0, 0)),
                pl.BlockSpec((None, None, 1, d), lambda i: ((i + blk0) // tpb, 5, 0, 0)),
                pl.BlockSpec((tm, LANES), lambda i: (i + pblk0, 0))] + [r_spec(k) for k in range(TOP_K)]
    operands = [base, mod4, gate] + [r] * TOP_K
    n_read = len(operands)
    aliases = {}
    if out_so_far is not None:
        aliases = {_unread_operand(in_specs, operands, out_so_far): 0}
    if after is not None:
        _unread_operand(in_specs, operands, after)
    n_in = len(operands)
    return pl.pallas_call(
        lambda *refs: _combine_kernel(*refs[:n_read], *refs[n_in:]),
        out_shape=jax.ShapeDtypeStruct((total_tokens, d), F32),
        grid=(nblk,),
        in_specs=in_specs,
        out_specs=pl.BlockSpec((tm, d), lambda i: (i + blk0, 0)),
        input_output_aliases=aliases,
        compiler_params=_cparams("arbitrary"),
        name="combine",
    )(*operands)


def _rope_tables(length, head_dim):
    rows = length // GRID_W
    pairs_axis = head_dim // 4
    r = jnp.repeat(jnp.arange(rows, dtype=F32), GRID_W)
    col = jnp.tile(jnp.arange(GRID_W, dtype=F32), rows)
    inv = ROPE_THETA ** (-jnp.arange(pairs_axis, dtype=F32) / pairs_axis)
    ang = jnp.concatenate([r[:, None] * inv, col[:, None] * inv], axis=-1)
    cos, sin = jnp.cos(ang), jnp.sin(ang)
    cos_h = jnp.concatenate([cos, cos], axis=-1)
    sin_h = jnp.concatenate([-sin, sin], axis=-1)
    reps = LANES // head_dim
    return jnp.tile(cos_h, (1, reps)), jnp.tile(sin_h, (1, reps))


def _part_batches(b):
    if b < 2:
        return [b]
    lead = max(1, min(b - 1, round(LEAD_PART_SHARE * b)))
    return [lead, b - lead]


def kernel(x, c, ctx, c_ctx, w_ada, b_ada, norm_mix, w_in, q_norm, k_norm, gm_norm, w_spatial, b_spatial,
           out_norm_attn, out_norm_gm, w_out, norm_ffn, w_router, router_bias, w_exp_gu, w_exp_down,
           w_sh_gu, w_sh_down):
    assert w_ada.shape[0] == 1, "single-layer kernel"
    b, length, d = x.shape
    head_dim = q_norm.shape[-1]
    att_w = out_norm_attn.shape[-1]
    gm_w = out_norm_gm.shape[-1]
    gm_groups = w_spatial.shape[1]
    gm_hd = gm_w // gm_groups
    n_exp = w_router.shape[-1]
    t = b * length
    assert head_dim == LANES // 2 and gm_hd == LANES // 2 and w_spatial.shape[-1] == CHUNK
    assert n_exp <= LANES and att_w % MXU_EDGE == 0 and gm_w % MXU_EDGE == 0

    mod_rows = 16
    c_all = jnp.zeros((mod_rows, d), F32).at[:b].set(c).at[b].set(c_ctx)
    mod = _ada(c_all, w_ada[0], b_ada)
    mod4 = mod.reshape(mod_rows, N_MOD, 1, d)

    w_in_b = w_in[0].astype(BF16)
    bd = (jnp.arange(MXU_EDGE)[:, None] // head_dim == jnp.arange(MXU_EDGE)[None, :] // head_dim)
    bd = (bd.astype(F32) / head_dim).astype(BF16)
    qg = jnp.tile(q_norm[0] * (head_dim ** -0.5 * LOG2_E), att_w // head_dim)[None, :]
    kg = jnp.tile(k_norm[0], LANES // head_dim)[None, :]
    gmg = jnp.tile(gm_norm[0], gm_groups)[None, :]
    ws = w_spatial[0].astype(BF16).reshape(gm_groups // 2, 2 * CHUNK, CHUNK)
    bs = jnp.repeat(b_spatial[0].T, gm_hd, axis=1)
    cos_t, sin_t = _rope_tables(length, head_dim)

    k_c, v_c = _ctx_kv(ctx, mod4, b, norm_mix, w_in_b[:, att_w:att_w + 2 * LANES], kg, bd)
    q, k_l, v_l, gmn = _inproj(x, mod4, norm_mix, w_in_b, cos_t, sin_t, qg, kg, gmg, bd, ws, bs,
                               out_norm_gm, att_w=att_w, gm_w=gm_w, tm=min(1024, length))
    wr_t = w_router[0].T.astype(BF16)
    rb = jnp.tile(router_bias[0][:, None], (1, LANES))
    w_out_b, w_sg_b, w_sd_b = w_out[0].astype(BF16), w_sh_gu[0].astype(BF16), w_sh_down[0].astype(BF16)
    gmn2, x2 = gmn.reshape(t, gm_w), x.reshape(t, d)
    tm_post = min(1024, length)
    tr = 512

    part_nb = _part_batches(b)
    n_parts = len(part_nb)
    part_b0 = [sum(part_nb[:p]) for p in range(n_parts)]
    state = [dict() for _ in range(n_parts)]

    last_tc = [None]

    def halves(part):
        nb = part_nb[part]
        return (nb // 2, nb - nb // 2) if nb >= 2 else (nb,)

    def attend(part, half):
        if half >= len(halves(part)):
            return
        st = state[part]
        slot0 = sum(halves(part)[:half])
        after = None if last_tc[0] is st.get("att") else last_tc[0]
        st["att"] = _attention(q, k_c, k_l, v_c, v_l, st.get("att"), after, tq=min(256, length),
                               b0=part_b0[part] + slot0, nb=halves(part)[half], slot0=slot0, slots=part_nb[part])
        last_tc[0] = st["att"]

    def route(part):
        st = state[part]
        tp = part_nb[part] * length
        nt = tp * TOP_K // tr + n_exp
        base, h2p, idx_t, gate, rank_t, cnt = _post(
            st["att"].reshape(tp, att_w), gmn2, x2, mod4, out_norm_attn, w_out_b, norm_ffn, wr_t, rb, w_sg_b, w_sd_b,
            tm=tm_post, tokens_per_batch=length, tok0=part_b0[part] * length)
        counts = cnt[:, 0].astype(jnp.int32)
        padded = (counts + tr - 1) // tr * tr
        ends = jnp.cumsum(padded)
        offs = ends - padded
        offs_pad = jnp.zeros((LANES,), jnp.int32).at[:n_exp].set(offs)
        xs, pos_t = _dispatch(h2p, idx_t, rank_t, offs_pad, n_slots=nt * tr)
        st.update(base=base, gate=gate, xs=xs, pos_t=pos_t, tile_lo=offs // tr, tile_hi=ends // tr)
        last_tc[0] = base

    def run_experts(part):
        st = state[part]
        ys = _experts(st["tile_lo"], st["tile_hi"], st["xs"], w_exp_gu[0], w_exp_down[0], last_tc[0], tr=tr)
        n_chunks = part_nb[part] if part == n_parts - 1 else 1
        chunk = part_nb[part] * length // n_chunks
        st["r"] = [(c * chunk, _collect(ys, st["pos_t"], tok_lo=c * chunk, n_tok=chunk)) for c in range(n_chunks)]
        last_tc[0] = ys

    result = [None]

    def finish(part):
        st = state[part]
        for part_lo, r in st["r"]:
            after = None if last_tc[0] is result[0] or last_tc[0] is st["base"] else last_tc[0]
            result[0] = _combine(st["base"], mod4, st["gate"], r, result[0], after, tm=min(512, length),
                                 tokens_per_batch=length, tok0=part_b0[part] * length, part_lo=part_lo,
                                 total_tokens=t)
            last_tc[0] = result[0]

    stages = [(0, lambda p: attend(p, 0)), (1, lambda p: attend(p, 1)), (2, route), (4, run_experts), (7, finish)]
    plan = sorted((when + 3 * part, part, n, fn) for part in range(n_parts) for n, (when, fn) in enumerate(stages))
    for _, part, _, fn in plan:
        fn(part)
    return result[0].reshape(b, length, d)
```

```python
import dataclasses
import functools

import jax
import jax.numpy as jnp
from jax import lax
from jax.experimental import pallas as pl
from jax.experimental.pallas import tpu as pltpu
from jax.experimental.pallas import tpu_sc as plsc

F32 = jnp.float32
BF16 = jnp.bfloat16

EPS = 1e-6
GRID_W = 64
ROPE_THETA = 10000.0
ATT_KV_HEADS = 2
TOP_K = 6
ROUTED_SCALE = 2.5
N_MOD = 6
CHUNK = 128
LOG2_E = 1.4426950408889634

LANES = 128
MXU_EDGE = 256
VMEM_LIMIT_BYTES = 56 * 1024 * 1024
SUBLANES = 8
CHOICE_ROWS = SUBLANES
SC_WINDOW = 64
SC_CHUNK_ALIGN = 2048
LEAD_PART_SHARE = 0.75
EXPERT_IN_SLOTS = 4
EXPERT_OUT_SLOTS = 2


def _cparams(*sem):
    return pltpu.CompilerParams(dimension_semantics=sem, vmem_limit_bytes=VMEM_LIMIT_BYTES)


def _dot(a, b):
    return jnp.dot(a, b, preferred_element_type=F32)


def _dot_nt(a, b):
    return lax.dot_general(a, b, (((1,), (1,)), ((), ())), preferred_element_type=F32)


def _rms(x, g):
    return x * lax.rsqrt(jnp.mean(x * x, axis=-1, keepdims=True) + EPS) * g


def _group_mean_sq(x, bd_ref):
    x2 = (x * x).astype(BF16)
    parts = [_dot(x2[:, i:i + MXU_EDGE], bd_ref[...]) for i in range(0, x.shape[1], MXU_EDGE)]
    return parts[0] if len(parts) == 1 else jnp.concatenate(parts, axis=-1)


def _dup_halves(x, lane):
    xr = pltpu.roll(x, LANES // 2, axis=1)
    lo = lane < LANES // 2
    return jnp.where(lo, x, xr), jnp.where(lo, xr, x)


def _with_ones(x, lane):
    xr = pltpu.roll(x, LANES // 2, axis=1)
    lo = lane < LANES // 2
    return jnp.concatenate([jnp.where(lo, x, 1.0), jnp.where(lo, xr, 1.0)], axis=-1)


_HI16 = 0xFFFF0000


def _pack_rows(x):
    half = x.shape[1] // 2
    rounded = x.astype(BF16).astype(F32)
    bits = lax.bitcast_convert_type(rounded, jnp.uint32)
    word = (bits[:, :half] >> 16) | (bits[:, half:] & jnp.uint32(_HI16))
    return lax.bitcast_convert_type(word, jnp.int32)


def _unpack_rows(w):
    bits = lax.bitcast_convert_type(w, jnp.uint32)
    lo = lax.bitcast_convert_type(bits << 16, F32)
    hi = lax.bitcast_convert_type(bits & jnp.uint32(_HI16), F32)
    return lo, hi


def _ada_kernel(c_ref, w_ref, b_ref, o_ref):
    s = jax.nn.silu(c_ref[...])
    o_ref[...] = jnp.dot(s, w_ref[...], precision=lax.Precision.HIGHEST,
                         preferred_element_type=F32) + b_ref[...]


def _ada(c_all, w_ada, b_ada):
    rows, d = c_all.shape
    n = w_ada.shape[1]
    bn = 2048 if n % 2048 == 0 else 512
    return pl.pallas_call(
        _ada_kernel,
        out_shape=jax.ShapeDtypeStruct((rows, n), F32),
        grid=(n // bn,),
        in_specs=[pl.BlockSpec((rows, d), lambda j: (0, 0)),
                  pl.BlockSpec((d, bn), lambda j: (0, j)),
                  pl.BlockSpec((1, bn), lambda j: (0, j))],
        out_specs=pl.BlockSpec((rows, bn), lambda j: (0, j)),
        compiler_params=_cparams("arbitrary"),
        name="ada",
    )(c_all, w_ada, b_ada)


def _ctx_kv_kernel(x_ref, sh_ref, sc_ref, nmix_ref, w_ref, kg_ref, bd_ref, k_ref, v_ref):
    h = _rms(x_ref[...], nmix_ref[...]) * (1.0 + sc_ref[...]) + sh_ref[...]
    p = _dot(h.astype(BF16), w_ref[...])
    ms = _group_mean_sq(p, bd_ref)
    k = p[:, :LANES] * lax.rsqrt(ms[:, :LANES] + EPS) * kg_ref[...]
    v = p[:, LANES:]
    lane = lax.broadcasted_iota(jnp.int32, k.shape, 1)
    ka, kb = _dup_halves(k, lane)
    k_ref[...] = jnp.concatenate([ka, kb], axis=-1).astype(BF16)
    v_ref[...] = _with_ones(v, lane).astype(BF16)


def _ctx_kv(ctx, mod4, ctx_row, nmix, w_kv, kg, bd):
    b, n_ctx, d = ctx.shape
    kv_w = 2 * LANES
    return pl.pallas_call(
        _ctx_kv_kernel,
        out_shape=(jax.ShapeDtypeStruct((b, n_ctx, kv_w), BF16),) * 2,
        grid=(b,),
        in_specs=[pl.BlockSpec((None, n_ctx, d), lambda i: (i, 0, 0)),
                  pl.BlockSpec((None, None, 1, d), lambda i: (ctx_row, 0, 0, 0)),
                  pl.BlockSpec((None, None, 1, d), lambda i: (ctx_row, 1, 0, 0)),
                  pl.BlockSpec((1, d), lambda i: (0, 0)),
                  pl.BlockSpec(w_kv.shape, lambda i: (0, 0)),
                  pl.BlockSpec((1, LANES), lambda i: (0, 0)),
                  pl.BlockSpec(bd.shape, lambda i: (0, 0))],
        out_specs=(pl.BlockSpec((None, n_ctx, kv_w), lambda i: (i, 0, 0)),) * 2,
        compiler_params=_cparams("arbitrary"),
        name="ctx_kv",
    )(ctx, mod4, mod4, nmix, w_kv, kg, bd)


def _inproj_kernel(x_ref, sh_ref, sc_ref, nmix_ref, w_ref, cos_ref, sin_ref, qg_ref, kg_ref, gmg_ref,
                   bd_ref, ws_ref, bs_ref, og_ref, q_ref, k_ref, v_ref, gm_ref, *, att_w, gm_w):
    tm = x_ref.shape[0]
    h = _rms(x_ref[...], nmix_ref[...]) * (1.0 + sc_ref[...]) + sh_ref[...]
    p = _dot(h.astype(BF16), w_ref[...])

    cos1, sin1 = cos_ref[...], sin_ref[...]
    lane1 = lax.broadcasted_iota(jnp.int32, (tm, LANES), 1)

    def rope(xn, cosw, sinw, lanew):
        w = xn.shape[1]
        fwd = pltpu.roll(xn, w - 32, axis=1)
        bwd = pltpu.roll(xn, 32, axis=1)
        swapped = jnp.where((lanew & 63) < 32, fwd, bwd)
        return xn * cosw + swapped * sinw

    q = p[:, :att_w]
    qn = q * lax.rsqrt(_group_mean_sq(q, bd_ref) + EPS) * qg_ref[...]
    reps = att_w // LANES
    cosq = jnp.concatenate([cos1] * reps, axis=-1)
    sinq = jnp.concatenate([sin1] * reps, axis=-1)
    laneq = lax.broadcasted_iota(jnp.int32, (tm, att_w), 1)
    q_ref[...] = rope(qn, cosq, sinq, laneq).astype(BF16)

    kv = p[:, att_w:att_w + 2 * LANES]
    ms = _group_mean_sq(kv, bd_ref)
    kn = kv[:, :LANES] * lax.rsqrt(ms[:, :LANES] + EPS) * kg_ref[...]
    k = rope(kn, cos1, sin1, lane1)
    v = kv[:, LANES:]
    ka, kb = _dup_halves(k, lane1)
    k_ref[...] = jnp.concatenate([ka, kb], axis=-1).astype(BF16)
    v_ref[...] = _with_ones(v, lane1).astype(BF16)

    u0 = att_w + 2 * LANES
    u = jax.nn.gelu(p[:, u0:u0 + gm_w])
    gt = jax.nn.gelu(p[:, u0 + gm_w:u0 + 2 * gm_w])
    gt = (gt * lax.rsqrt(_group_mean_sq(gt, bd_ref) + EPS) * gmg_ref[...]).astype(BF16)
    lo_half = lax.broadcasted_iota(jnp.int32, (CHUNK, LANES), 1) < LANES // 2
    rows = []
    for c in range(tm // CHUNK):
        blocks = []
        for j in range(gm_w // LANES):
            g = gt[c * CHUNK:(c + 1) * CHUNK, j * LANES:(j + 1) * LANES]
            r = _dot(ws_ref[j], g)
            blocks.append(jnp.where(lo_half, r[:CHUNK], r[CHUNK:]))
        rows.append(jnp.concatenate(blocks, axis=-1) + bs_ref[...])
    mixed = jnp.concatenate(rows, axis=0)
    gm_ref[...] = _rms(u * mixed, og_ref[...]).astype(BF16)


def _inproj(x, mod4, nmix, w_in, cos_t, sin_t, qg, kg, gmg, bd, ws, bs, og, *, att_w, gm_w, tm):
    b, length, d = x.shape
    kv_w = 2 * LANES
    n_in = w_in.shape[1]
    const = lambda shape: pl.BlockSpec(shape, lambda i, j: (0,) * len(shape))
    return pl.pallas_call(
        functools.partial(_inproj_kernel, att_w=att_w, gm_w=gm_w),
        out_shape=(jax.ShapeDtypeStruct((b, length, att_w), BF16),
                   jax.ShapeDtypeStruct((b, length, kv_w), BF16),
                   jax.ShapeDtypeStruct((b, length, kv_w), BF16),
                   jax.ShapeDtypeStruct((b, length, gm_w), BF16)),
        grid=(b, length // tm),
        in_specs=[pl.BlockSpec((None, tm, d), lambda i, j: (i, j, 0)),
                  pl.BlockSpec((None, None, 1, d), lambda i, j: (i, 0, 0, 0)),
                  pl.BlockSpec((None, None, 1, d), lambda i, j: (i, 1, 0, 0)),
                  const((1, d)),
                  const((d, n_in)),
                  pl.BlockSpec((tm, LANES), lambda i, j: (j, 0)),
                  pl.BlockSpec((tm, LANES), lambda i, j: (j, 0)),
                  const((1, att_w)), const((1, LANES)), const((1, gm_w)),
                  const(bd.shape), const(ws.shape), const(bs.shape), const((1, gm_w))],
        out_specs=(pl.BlockSpec((None, tm, att_w), lambda i, j: (i, j, 0)),
                   pl.BlockSpec((None, tm, kv_w), lambda i, j: (i, j, 0)),
                   pl.BlockSpec((None, tm, kv_w), lambda i, j: (i, j, 0)),
                   pl.BlockSpec((None, tm, gm_w), lambda i, j: (i, j, 0))),
        compiler_params=_cparams("arbitrary", "arbitrary"),
        name="inproj",
    )(x, mod4, mod4, nmix, w_in, cos_t, sin_t, qg, kg, gmg, bd, ws, bs, og)


def _attn_kernel(q_ref, kc_ref, kl_ref, vc_ref, vl_ref, o_ref, s_buf, p_buf, *, tq):
    n_ctx = kc_ref.shape[0]
    heads = q_ref.shape[1] // (LANES // 2)
    nj = q_ref.shape[0] // tq
    half = LANES // 2
    lo = lax.broadcasted_iota(jnp.int32, (tq, LANES), 1) < half

    @pl.when(jnp.logical_and(pl.program_id(0) == 0, pl.program_id(1) == 0))
    def _():
        s_buf[...] = jnp.zeros_like(s_buf)
        p_buf[...] = jnp.ones_like(p_buf)

    def scores(j, g):
        q = q_ref[pl.ds(j * tq, tq), (g // 2) * LANES:(g // 2 + 1) * LANES]
        zero = jnp.zeros_like(q)
        qh = jnp.where(lo, q, zero) if g % 2 == 0 else jnp.where(lo, zero, q)
        s_buf[g % 2, :, :n_ctx] = _dot_nt(qh, kc_ref[...])
        s_buf[g % 2, :, n_ctx:] = _dot_nt(qh, kl_ref[...])

    def probs(g):
        s = s_buf[g % 2]
        p_buf[g % 2] = jnp.exp2(s - jnp.max(s, axis=-1, keepdims=True)).astype(BF16)

    def output(j, g):
        p = p_buf[g % 2]
        w = _dot(p[:, :n_ctx], vc_ref[...]) + _dot(p[:, n_ctx:], vl_ref[...])
        t = jnp.where(pl.program_id(1) == 0, w[:, :LANES], w[:, LANES:])
        r = pltpu.roll(t, half, axis=1)
        rows = pl.ds(j * tq, tq)
        if g % 2 == 0:
            o_ref[rows, g * half:(g + 1) * half] = (t / r)[:, :half].astype(o_ref.dtype)
        else:
            o_ref[rows, g * half:(g + 1) * half] = (r / t)[:, half:].astype(o_ref.dtype)

    def tile_stages(j):
        j_prev = jnp.maximum(j - 1, 0)
        for g in range(heads):
            scores(j, g)
            probs((g - 1) % heads)
            output(j if g >= 2 else j_prev, (g - 2) % heads)

    def trip(jj, carry):
        for u in range(tiles_per_trip):
            tile_stages(jj * tiles_per_trip + u)
        return carry

    tiles_per_trip = 2 if nj % 2 == 0 else 1
    lax.fori_loop(0, nj // tiles_per_trip, trip, 0)
    probs(heads - 1)
    output(nj - 1, heads - 2)
    output(nj - 1, heads - 1)


def _unread_operand(in_specs, operands, array):
    in_specs.append(pl.BlockSpec(memory_space=pl.ANY))
    operands.append(array)
    return len(operands) - 1


def _attention(q, k_c, k_l, v_c, v_l, att_so_far, after, *, tq, b0, nb, slot0, slots):
    _, length, att_w = q.shape
    n_ctx = k_c.shape[1]
    group_w = att_w // ATT_KV_HEADS
    assert (group_w // (LANES // 2)) % 2 == 0 and length % tq == 0
    k_blk = lambda n: pl.BlockSpec((None, n, LANES), lambda i, h: (i + b0, 0, h))
    v_blk = lambda n: pl.BlockSpec((None, n, ATT_KV_HEADS * LANES), lambda i, h: (i + b0, 0, 0))
    in_specs = [pl.BlockSpec((None, length, group_w), lambda i, h: (i + b0, 0, h)),
                k_blk(n_ctx), k_blk(length), v_blk(n_ctx), v_blk(length)]
    operands = [q, k_c, k_l, v_c, v_l]
    n_read = len(operands)
    aliases = {}
    if att_so_far is not None:
        aliases = {_unread_operand(in_specs, operands, att_so_far): 0}
    if after is not None:
        _unread_operand(in_specs, operands, after)
    n_in = len(operands)
    return pl.pallas_call(
        lambda *refs: _attn_kernel(*refs[:n_read], *refs[n_in:], tq=tq),
        out_shape=jax.ShapeDtypeStruct((slots, length, att_w), BF16),
        grid=(nb, ATT_KV_HEADS),
        in_specs=in_specs,
        out_specs=pl.BlockSpec((None, length, group_w), lambda i, h: (i + slot0, 0, h)),
        scratch_shapes=[pltpu.VMEM((2, tq, n_ctx + length), F32), pltpu.VMEM((2, tq, n_ctx + length), BF16)],
        input_output_aliases=aliases,
        compiler_params=_cparams("arbitrary", "arbitrary"),
        name="attn",
    )(*operands)


def _post_kernel(att_ref, gm_ref, x_ref, g1_ref, sh2_ref, sc2_ref, g2_ref, oa_ref, wo_ref, nffn_ref,
                 wrt_ref, rb_ref, wsg_ref, wsd_ref,
                 base_ref, h2p_ref, idx_ref, gate_ref, rank_ref, cnt_ref, carry_ref):
    i = pl.program_id(0)
    tm = x_ref.shape[0]
    n_exp = wrt_ref.shape[0]

    @pl.when(i == 0)
    def _():
        carry_ref[...] = jnp.zeros_like(carry_ref)

    att_n = _rms(att_ref[...].astype(F32), oa_ref[...]).astype(BF16)
    y = _dot(jnp.concatenate([att_n, gm_ref[...]], axis=-1), wo_ref[...])
    x_new = x_ref[...] + g1_ref[...] * y
    h2 = _rms(x_new, nffn_ref[...]) * (1.0 + sc2_ref[...]) + sh2_ref[...]
    h2p_ref[...] = _pack_rows(h2)

    h_hi = h2.astype(BF16)
    scores_t = jax.nn.sigmoid(_dot_nt(wrt_ref[...], h_hi))
    reps = tm // LANES
    sel = scores_t + jnp.concatenate([rb_ref[...]] * reps, axis=-1)
    expert = lax.broadcasted_iota(jnp.int32, scores_t.shape, 0).astype(F32)
    neg = jnp.float32(-jnp.inf)

    onehot = jnp.zeros(scores_t.shape, F32)
    hits, idxs, svals = [], [], []
    for _k in range(TOP_K):
        m = jnp.max(sel, axis=0, keepdims=True)
        idx = jnp.min(jnp.where(sel == m, expert, float(n_exp)), axis=0, keepdims=True)
        hit = expert == idx
        hits.append(hit)
        idxs.append(idx)
        svals.append(jnp.sum(jnp.where(hit, scores_t, 0.0), axis=0, keepdims=True))
        onehot = onehot + jnp.where(hit, 1.0, 0.0)
        sel = jnp.where(hit, neg, sel)
    ssum = functools.reduce(lambda a, b: a + b, svals)

    r_io = lax.broadcasted_iota(jnp.int32, (tm, tm), 0)
    c_io = lax.broadcasted_iota(jnp.int32, (tm, tm), 1)
    tri = jnp.where(r_io < c_io, 1.0, 0.0).astype(BF16)
    carry = carry_ref[...]
    before = _dot(onehot.astype(BF16), tri) + jnp.concatenate([carry] * reps, axis=-1)
    carry = carry + jnp.sum(onehot, axis=1, keepdims=True)
    carry_ref[...] = carry
    cnt_ref[...] = carry

    row = lax.broadcasted_iota(jnp.int32, (idx_ref.shape[0], tm), 0)
    idx_o = jnp.zeros(row.shape, F32)
    rank_o = jnp.zeros(row.shape, F32)
    gate_o = jnp.zeros(row.shape, F32)
    for k in range(TOP_K):
        rank = jnp.sum(jnp.where(hits[k], before, 0.0), axis=0, keepdims=True)
        idx_o = jnp.where(row == k, idxs[k], idx_o)
        rank_o = jnp.where(row == k, rank, rank_o)
        gate_o = jnp.where(row == k, svals[k] / ssum * ROUTED_SCALE, gate_o)
    idx_ref[...] = idx_o.astype(jnp.int32)
    rank_ref[...] = rank_o.astype(jnp.int32)
    gate_pad = jnp.concatenate([gate_o, jnp.zeros((LANES - gate_o.shape[0], tm), F32)], axis=0)
    gate_ref[...] = gate_pad.T

    gu = _dot(h_hi, wsg_ref[...])
    hs = gu.shape[1] // 2
    act = (jax.nn.silu(gu[:, :hs]) * gu[:, hs:]).astype(BF16)
    base_ref[...] = x_new + g2_ref[...] * _dot(act, wsd_ref[...])


def _post(att, gmn, x2, mod4, oa, w_out, nffn, wr_t, rb, w_sg, w_sd, *, tm, tokens_per_batch, tok0):
    t, att_w = att.shape
    d = x2.shape[1]
    n_exp = wr_t.shape[0]
    gm_w = gmn.shape[1]
    tpb = tokens_per_batch // tm
    blk0 = tok0 // tm
    const = lambda shape: pl.BlockSpec(shape, lambda i: (0,) * len(shape))
    mrow = lambda j: pl.BlockSpec((None, None, 1, d), lambda i: ((i + blk0) // tpb, j, 0, 0))
    by_choice = jax.ShapeDtypeStruct((CHOICE_ROWS, t), jnp.int32)
    return pl.pallas_call(
        _post_kernel,
        out_shape=(jax.ShapeDtypeStruct((t, d), F32), jax.ShapeDtypeStruct((t, d // 2), jnp.int32),
                   by_choice, jax.ShapeDtypeStruct((t, LANES), F32), by_choice,
                   jax.ShapeDtypeStruct((n_exp, LANES), F32)),
        grid=(t // tm,),
        in_specs=[pl.BlockSpec((tm, att_w), lambda i: (i, 0)),
                  pl.BlockSpec((tm, gm_w), lambda i: (i + blk0, 0)),
                  pl.BlockSpec((tm, d), lambda i: (i + blk0, 0)),
                  mrow(2), mrow(3), mrow(4), mrow(5),
                  const((1, att_w)), const(w_out.shape), const((1, d)),
                  const(wr_t.shape), const(rb.shape),
                  const(w_sg.shape), const(w_sd.shape)],
        out_specs=(pl.BlockSpec((tm, d), lambda i: (i, 0)), pl.BlockSpec((tm, d // 2), lambda i: (i, 0)),
                   pl.BlockSpec((CHOICE_ROWS, tm), lambda i: (0, i)), pl.BlockSpec((tm, LANES), lambda i: (i, 0)),
                   pl.BlockSpec((CHOICE_ROWS, tm), lambda i: (0, i)), pl.BlockSpec((n_exp, LANES), lambda i: (0, 0))),
        scratch_shapes=[pltpu.VMEM((n_exp, LANES), F32)],
        compiler_params=_cparams("arbitrary"),
        name="post",
    )(att, gmn, x2, mod4, mod4, mod4, mod4, oa, w_out, nffn, wr_t, rb, w_sg, w_sd)


def _sc_mesh_info():
    info = plsc.get_sparse_core_info()
    mesh = plsc.VectorSubcoreMesh(core_axis_name="c", subcore_axis_name="s")
    return mesh, info.num_cores, info.num_subcores, info.num_lanes


def _sc_params():
    cp = pltpu.CompilerParams()
    if "needs_layout_passes" in pltpu.CompilerParams.__dataclass_fields__:
        cp = dataclasses.replace(cp, needs_layout_passes=False)
    return cp


def _dispatch(h2p, idx_t, rank_t, offs, *, n_slots):
    t, dw = h2p.shape
    mesh, n_cores, n_sub, n_lanes = _sc_mesh_info()
    per_worker = t // (n_cores * n_sub)
    w = SC_WINDOW
    assert per_worker % w == 0 and w % n_lanes == 0

    @functools.partial(
        pl.kernel, mesh=mesh,
        out_type=(jax.ShapeDtypeStruct((n_slots, dw), jnp.int32), jax.ShapeDtypeStruct((CHOICE_ROWS, t), jnp.int32)),
        scratch_types=[pltpu.VMEM(offs.shape, jnp.int32)] + [pltpu.VMEM((TOP_K, w), jnp.int32)] * 3
        + [pltpu.VMEM((w, dw), jnp.int32), pltpu.SemaphoreType.DMA],
        compiler_params=_sc_params(),
    )
    def run(h_hbm, idx_hbm, rank_hbm, offs_hbm, xs_hbm, pos_hbm, offs_v, idx_v, rank_v, pos_v, rows_v, sem):
        base = (lax.axis_index("s") * n_cores + lax.axis_index("c")) * per_worker
        pltpu.sync_copy(offs_hbm, offs_v)

        @pl.loop(0, per_worker // w)
        def _(i):
            t0 = base + i * w
            pltpu.sync_copy(h_hbm.at[pl.ds(t0, w)], rows_v)
            for k in range(TOP_K):
                pltpu.sync_copy(idx_hbm.at[k, pl.ds(t0, w)], idx_v.at[k])
                pltpu.sync_copy(rank_hbm.at[k, pl.ds(t0, w)], rank_v.at[k])
            for k in range(TOP_K):
                for j in range(0, w, n_lanes):
                    group_start = plsc.load_gather(offs_v, [idx_v[k, pl.ds(j, n_lanes)]])
                    pos_v[k, pl.ds(j, n_lanes)] = group_start + rank_v[k, pl.ds(j, n_lanes)]
            copies = [pltpu.async_copy(rows_v, xs_hbm.at[pos_v.at[k]], sem) for k in range(TOP_K)]
            for k in range(TOP_K):
                pltpu.sync_copy(pos_v.at[k], pos_hbm.at[k, pl.ds(t0, w)])
            for cp in copies:
                cp.wait()

    return run(h2p, idx_t, rank_t, offs)


def _collect(ys, pos_t, *, tok_lo, n_tok):
    dw = ys.shape[1]
    mesh, n_cores, n_sub, _ = _sc_mesh_info()
    per_worker = n_tok // (n_cores * n_sub)
    w = SC_WINDOW
    assert per_worker % w == 0 and TOP_K % 2 == 0

    @functools.partial(
        pl.kernel, mesh=mesh,
        out_type=jax.ShapeDtypeStruct((TOP_K * n_tok, dw), jnp.int32),
        scratch_types=[pltpu.VMEM((TOP_K, w), jnp.int32), pltpu.VMEM((w, dw), jnp.int32),
                       pltpu.VMEM((w, dw), jnp.int32), pltpu.SemaphoreType.DMA, pltpu.SemaphoreType.DMA],
        compiler_params=_sc_params(),
    )
    def run(ys_hbm, pos_hbm, r_hbm, pos_v, rows_a, rows_b, sem_g, sem_w):
        base = (lax.axis_index("s") * n_cores + lax.axis_index("c")) * per_worker
        bufs = (rows_a, rows_b)

        @pl.loop(0, per_worker // w)
        def _(i):
            t0 = base + i * w
            for k in range(TOP_K):
                pltpu.sync_copy(pos_hbm.at[k, pl.ds(tok_lo + t0, w)], pos_v.at[k])
            gathers = [None] * TOP_K
            writes = [None] * TOP_K
            gathers[0] = pltpu.async_copy(ys_hbm.at[pos_v.at[0]], bufs[0], sem_g)
            for k in range(TOP_K):
                gathers[k].wait()
                if k + 1 < TOP_K:
                    if k >= 1:
                        writes[k - 1].wait()
                    gathers[k + 1] = pltpu.async_copy(ys_hbm.at[pos_v.at[k + 1]], bufs[(k + 1) % 2], sem_g)
                writes[k] = pltpu.async_copy(bufs[k % 2], r_hbm.at[pl.ds(k * n_tok + t0, w)], sem_w)
            writes[TOP_K - 2].wait()
            writes[TOP_K - 1].wait()

    return run(ys, pos_t)


def _experts_kernel(t0_ref, t1_ref, xs_hbm, wgu_ref, wdn_ref, ys_hbm, xbuf, ybuf, wgu_b, wdn_b, sem_in, sem_out):
    e = pl.program_id(0)
    n_exp = pl.num_programs(0)
    n_in, tr, half = xbuf.shape
    n_out = ybuf.shape[0]
    n_tiles = t1_ref[n_exp - 1]

    def fetch(t):
        slot = t & (n_in - 1)
        return pltpu.make_async_copy(xs_hbm.at[pl.ds(t * tr, tr)], xbuf.at[slot], sem_in.at[slot])

    def flush(t):
        slot = t & (n_out - 1)
        return pltpu.make_async_copy(ybuf.at[slot], ys_hbm.at[pl.ds(t * tr, tr)], sem_out.at[slot])

    @pl.when(e == 0)
    def _():
        for t in range(n_in - 1):
            @pl.when(t < n_tiles)
            def _():
                fetch(t).start()

    wgu_b[...] = wgu_ref[...].astype(BF16)
    wdn_b[...] = wdn_ref[...].astype(BF16)

    def tile(t, carry):
        fetch(t).wait()

        @pl.when(t + n_in - 1 < n_tiles)
        def _():
            fetch(t + n_in - 1).start()

        lo, hi = _unpack_rows(xbuf[t & (n_in - 1)])
        gu = _dot(lo.astype(BF16), wgu_b[:half, :]) + _dot(hi.astype(BF16), wgu_b[half:, :])
        hh = gu.shape[1] // 2
        act = (jax.nn.silu(gu[:, :hh]) * gu[:, hh:]).astype(BF16)
        y = _pack_rows(_dot(act, wdn_b[...]))

        @pl.when(t >= n_out)
        def _():
            flush(t - n_out).wait()

        ybuf[t & (n_out - 1)] = y
        flush(t).start()
        return carry

    lax.fori_loop(t0_ref[e], t1_ref[e], tile, 0)

    @pl.when(e == n_exp - 1)
    def _():
        for back in range(n_out, 0, -1):
            @pl.when(n_tiles >= back)
            def _():
                flush(n_tiles - back).wait()


def _experts(tile_lo, tile_hi, xs, w_gu, w_dn, after, *, tr):
    n_exp, d, h2w = w_gu.shape
    hh = w_dn.shape[1]
    in_specs = [pl.BlockSpec(memory_space=pl.ANY),
                pl.BlockSpec((None, d, h2w), lambda e, lo, hi: (e, 0, 0)),
                pl.BlockSpec((None, hh, d), lambda e, lo, hi: (e, 0, 0))]
    operands = [xs, w_gu, w_dn]
    n_read = 2 + len(operands)
    if after is not None:
        _unread_operand(in_specs, operands, after)
    n_in = 2 + len(operands)
    return pl.pallas_call(
        lambda *refs: _experts_kernel(*refs[:n_read], *refs[n_in:]),
        out_shape=jax.ShapeDtypeStruct(xs.shape, jnp.int32),
        grid_spec=pltpu.PrefetchScalarGridSpec(
            num_scalar_prefetch=2,
            grid=(n_exp,),
            in_specs=in_specs,
            out_specs=pl.BlockSpec(memory_space=pl.ANY),
            scratch_shapes=[pltpu.VMEM((EXPERT_IN_SLOTS, tr, d // 2), jnp.int32),
                            pltpu.VMEM((EXPERT_OUT_SLOTS, tr, d // 2), jnp.int32),
                            pltpu.VMEM((d, h2w), BF16), pltpu.VMEM((hh, d), BF16),
                            pltpu.SemaphoreType.DMA((EXPERT_IN_SLOTS,)), pltpu.SemaphoreType.DMA((EXPERT_OUT_SLOTS,))]),
        compiler_params=_cparams("arbitrary"),
        name="experts",
    )(tile_lo, tile_hi, *operands)


def _combine_kernel(base_ref, g2_ref, gate_ref, *refs):
    r_refs, o_ref = refs[:TOP_K], refs[TOP_K]
    gate = gate_ref[...]
    half = base_ref.shape[1] // 2
    acc_lo = jnp.zeros((base_ref.shape[0], half), F32)
    acc_hi = jnp.zeros((base_ref.shape[0], half), F32)
    for k in range(TOP_K):
        lo, hi = _unpack_rows(r_refs[k][...])
        g = gate[:, k:k + 1]
        acc_lo = acc_lo + g * lo
        acc_hi = acc_hi + g * hi
    g2 = g2_ref[...]
    o_ref[:, :half] = base_ref[:, :half] + g2[:, :half] * acc_lo
    o_ref[:, half:] = base_ref[:, half:] + g2[:, half:] * acc_hi


def _combine(base, mod4, gate, r, out_so_far, after, *, tm, tokens_per_batch, tok0, part_lo, total_tokens):
    d = base.shape[1]
    t = r.shape[0] // TOP_K
    tpb = tokens_per_batch // tm
    nblk = t // tm
    pblk0 = part_lo // tm
    blk0 = (tok0 + part_lo) // tm
    r_spec = lambda k: pl.BlockSpec((tm, d // 2), lambda i: (k * nblk + i, 0))
    in_specs = [pl.BlockSpec((tm, d), lambda i: (i + pblk0, 0)),
                pl.BlockSpec((None, None, 1, d), lambda i: ((i + blk0) // tpb, 5, 0, 0)),
                pl.BlockSpec((tm, LANES), lambda i: (i + pblk0, 0))] + [r_spec(k) for k in range(TOP_K)]
    operands = [base, mod4, gate] + [r] * TOP_K
    n_read = len(operands)
    aliases = {}
    if out_so_far is not None:
        aliases = {_unread_operand(in_specs, operands, out_so_far): 0}
    if after is not None:
        _unread_operand(in_specs, operands, after)
    n_in = len(operands)
    return pl.pallas_call(
        lambda *refs: _combine_kernel(*refs[:n_read], *refs[n_in:]),
        out_shape=jax.ShapeDtypeStruct((total_tokens, d), F32),
        grid=(nblk,),
        in_specs=in_specs,
        out_specs=pl.BlockSpec((tm, d), lambda i: (i + blk0, 0)),
        input_output_aliases=aliases,
        compiler_params=_cparams("arbitrary"),
        name="combine",
    )(*operands)


def _rope_tables(length, head_dim):
    rows = length // GRID_W
    pairs_axis = head_dim // 4
    r = jnp.repeat(jnp.arange(rows, dtype=F32), GRID_W)
    col = jnp.tile(jnp.arange(GRID_W, dtype=F32), rows)
    inv = ROPE_THETA ** (-jnp.arange(pairs_axis, dtype=F32) / pairs_axis)
    ang = jnp.concatenate([r[:, None] * inv, col[:, None] * inv], axis=-1)
    cos, sin = jnp.cos(ang), jnp.sin(ang)
    cos_h = jnp.concatenate([cos, cos], axis=-1)
    sin_h = jnp.concatenate([-sin, sin], axis=-1)
    reps = LANES // head_dim
    return jnp.tile(cos_h, (1, reps)), jnp.tile(sin_h, (1, reps))


def _part_batches(b):
    if b < 2:
        return [b]
    lead = max(1, min(b - 1, round(LEAD_PART_SHARE * b)))
    return [lead, b - lead]


def kernel(x, c, ctx, c_ctx, w_ada, b_ada, norm_mix, w_in, q_norm, k_norm, gm_norm, w_spatial, b_spatial,
           out_norm_attn, out_norm_gm, w_out, norm_ffn, w_router, router_bias, w_exp_gu, w_exp_down,
           w_sh_gu, w_sh_down):
    assert w_ada.shape[0] == 1, "single-layer kernel"
    b, length, d = x.shape
    head_dim = q_norm.shape[-1]
    att_w = out_norm_attn.shape[-1]
    gm_w = out_norm_gm.shape[-1]
    gm_groups = w_spatial.shape[1]
    gm_hd = gm_w // gm_groups
    n_exp = w_router.shape[-1]
    t = b * length
    assert head_dim == LANES // 2 and gm_hd == LANES // 2 and w_spatial.shape[-1] == CHUNK
    assert n_exp <= LANES and att_w % MXU_EDGE == 0 and gm_w % MXU_EDGE == 0

    mod_rows = 16
    c_all = jnp.zeros((mod_rows, d), F32).at[:b].set(c).at[b].set(c_ctx)
    mod = _ada(c_all, w_ada[0], b_ada)
    mod4 = mod.reshape(mod_rows, N_MOD, 1, d)

    w_in_b = w_in[0].astype(BF16)
    bd = (jnp.arange(MXU_EDGE)[:, None] // head_dim == jnp.arange(MXU_EDGE)[None, :] // head_dim)
    bd = (bd.astype(F32) / head_dim).astype(BF16)
    qg = jnp.tile(q_norm[0] * (head_dim ** -0.5 * LOG2_E), att_w // head_dim)[None, :]
    kg = jnp.tile(k_norm[0], LANES // head_dim)[None, :]
    gmg = jnp.tile(gm_norm[0], gm_groups)[None, :]
    ws = w_spatial[0].astype(BF16).reshape(gm_groups // 2, 2 * CHUNK, CHUNK)
    bs = jnp.repeat(b_spatial[0].T, gm_hd, axis=1)
    cos_t, sin_t = _rope_tables(length, head_dim)

    k_c, v_c = _ctx_kv(ctx, mod4, b, norm_mix, w_in_b[:, att_w:att_w + 2 * LANES], kg, bd)
    q, k_l, v_l, gmn = _inproj(x, mod4, norm_mix, w_in_b, cos_t, sin_t, qg, kg, gmg, bd, ws, bs,
                               out_norm_gm, att_w=att_w, gm_w=gm_w, tm=min(1024, length))
    wr_t = w_router[0].T.astype(BF16)
    rb = jnp.tile(router_bias[0][:, None], (1, LANES))
    w_out_b, w_sg_b, w_sd_b = w_out[0].astype(BF16), w_sh_gu[0].astype(BF16), w_sh_down[0].astype(BF16)
    gmn2, x2 = gmn.reshape(t, gm_w), x.reshape(t, d)
    tm_post = min(1024, length)
    tr = 512

    part_nb = _part_batches(b)
    n_parts = len(part_nb)
    part_b0 = [sum(part_nb[:p]) for p in range(n_parts)]
    state = [dict() for _ in range(n_parts)]

    last_tc = [None]

    def halves(part):
        nb = part_nb[part]
        return (nb // 2, nb - nb // 2) if nb >= 2 else (nb,)

    def attend(part, half):
        if half >= len(halves(part)):
            return
        st = state[part]
        slot0 = sum(halves(part)[:half])
        after = None if last_tc[0] is st.get("att") else last_tc[0]
        st["att"] = _attention(q, k_c, k_l, v_c, v_l, st.get("att"), after, tq=min(256, length),
                               b0=part_b0[part] + slot0, nb=halves(part)[half], slot0=slot0, slots=part_nb[part])
        last_tc[0] = st["att"]

    def route(part):
        st = state[part]
        tp = part_nb[part] * length
        nt = tp * TOP_K // tr + n_exp
        base, h2p, idx_t, gate, rank_t, cnt = _post(
            st["att"].reshape(tp, att_w), gmn2, x2, mod4, out_norm_attn, w_out_b, norm_ffn, wr_t, rb, w_sg_b, w_sd_b,
            tm=tm_post, tokens_per_batch=length, tok0=part_b0[part] * length)
        counts = cnt[:, 0].astype(jnp.int32)
        padded = (counts + tr - 1) // tr * tr
        ends = jnp.cumsum(padded)
        offs = ends - padded
        offs_pad = jnp.zeros((LANES,), jnp.int32).at[:n_exp].set(offs)
        xs, pos_t = _dispatch(h2p, idx_t, rank_t, offs_pad, n_slots=nt * tr)
        st.update(base=base, gate=gate, xs=xs, pos_t=pos_t, tile_lo=offs // tr, tile_hi=ends // tr)
        last_tc[0] = base

    def run_experts(part):
        st = state[part]
        ys = _experts(st["tile_lo"], st["tile_hi"], st["xs"], w_exp_gu[0], w_exp_down[0], last_tc[0], tr=tr)
        n_chunks = 2 * part_nb[part] if part == n_parts - 1 and length % (2 * SC_CHUNK_ALIGN) == 0 else 1
        chunk = part_nb[part] * length // n_chunks
        st["r"] = [(c * chunk, _collect(ys, st["pos_t"], tok_lo=c * chunk, n_tok=chunk)) for c in range(n_chunks)]
        last_tc[0] = ys

    result = [None]

    def finish(part):
        st = state[part]
        for part_lo, r in st["r"]:
            after = None if last_tc[0] is result[0] or last_tc[0] is st["base"] else last_tc[0]
            result[0] = _combine(st["base"], mod4, st["gate"], r, result[0], after, tm=min(512, length),
                                 tokens_per_batch=length, tok0=part_b0[part] * length, part_lo=part_lo,
                                 total_tokens=t)
            last_tc[0] = result[0]

    stages = [(0, lambda p: attend(p, 0)), (1, lambda p: attend(p, 1)), (2, route), (4, run_experts), (7, finish)]
    plan = sorted((when + 3 * part, part, n, fn) for part in range(n_parts) for n, (when, fn) in enumerate(stages))
    for _, part, _, fn in plan:
        fn(part)
    return result[0].reshape(b, length, d)
```

```python
import dataclasses
import functools

import jax
import jax.numpy as jnp
from jax import lax
from jax.experimental import pallas as pl
from jax.experimental.pallas import tpu as pltpu
from jax.experimental.pallas import tpu_sc as plsc

F32 = jnp.float32
BF16 = jnp.bfloat16

EPS = 1e-6
GRID_W = 64
ROPE_THETA = 10000.0
ATT_KV_HEADS = 2
TOP_K = 6
ROUTED_SCALE = 2.5
N_MOD = 6
CHUNK = 128
LOG2_E = 1.4426950408889634

LANES = 128
MXU_EDGE = 256
VMEM_LIMIT_BYTES = 56 * 1024 * 1024
SUBLANES = 8
CHOICE_ROWS = SUBLANES
SC_WINDOW = 64
LEAD_PART_SHARE = 0.75
EXPERT_IN_SLOTS = 4
EXPERT_OUT_SLOTS = 2


def _cparams(*sem):
    return pltpu.CompilerParams(dimension_semantics=sem, vmem_limit_bytes=VMEM_LIMIT_BYTES)


def _dot(a, b):
    return jnp.dot(a, b, preferred_element_type=F32)


def _dot_nt(a, b):
    return lax.dot_general(a, b, (((1,), (1,)), ((), ())), preferred_element_type=F32)


def _rms(x, g):
    return x * lax.rsqrt(jnp.mean(x * x, axis=-1, keepdims=True) + EPS) * g


def _group_mean_sq(x, bd_ref):
    x2 = (x * x).astype(BF16)
    parts = [_dot(x2[:, i:i + MXU_EDGE], bd_ref[...]) for i in range(0, x.shape[1], MXU_EDGE)]
    return parts[0] if len(parts) == 1 else jnp.concatenate(parts, axis=-1)


def _dup_halves(x, lane):
    xr = pltpu.roll(x, LANES // 2, axis=1)
    lo = lane < LANES // 2
    return jnp.where(lo, x, xr), jnp.where(lo, xr, x)


def _with_ones(x, lane):
    xr = pltpu.roll(x, LANES // 2, axis=1)
    lo = lane < LANES // 2
    return jnp.concatenate([jnp.where(lo, x, 1.0), jnp.where(lo, xr, 1.0)], axis=-1)


_HI16 = 0xFFFF0000


def _pack_rows(x):
    half = x.shape[1] // 2
    rounded = x.astype(BF16).astype(F32)
    bits = lax.bitcast_convert_type(rounded, jnp.uint32)
    word = (bits[:, :half] >> 16) | (bits[:, half:] & jnp.uint32(_HI16))
    return lax.bitcast_convert_type(word, jnp.int32)


def _unpack_rows(w):
    bits = lax.bitcast_convert_type(w, jnp.uint32)
    lo = lax.bitcast_convert_type(bits << 16, F32)
    hi = lax.bitcast_convert_type(bits & jnp.uint32(_HI16), F32)
    return lo, hi


def _ada_kernel(c_ref, w_ref, b_ref, o_ref):
    s = jax.nn.silu(c_ref[...])
    o_ref[...] = jnp.dot(s, w_ref[...], precision=lax.Precision.HIGHEST,
                         preferred_element_type=F32) + b_ref[...]


def _ada(c_all, w_ada, b_ada):
    rows, d = c_all.shape
    n = w_ada.shape[1]
    bn = 2048 if n % 2048 == 0 else 512
    return pl.pallas_call(
        _ada_kernel,
        out_shape=jax.ShapeDtypeStruct((rows, n), F32),
        grid=(n // bn,),
        in_specs=[pl.BlockSpec((rows, d), lambda j: (0, 0)),
                  pl.BlockSpec((d, bn), lambda j: (0, j)),
                  pl.BlockSpec((1, bn), lambda j: (0, j))],
        out_specs=pl.BlockSpec((rows, bn), lambda j: (0, j)),
        compiler_params=_cparams("arbitrary"),
        name="ada",
    )(c_all, w_ada, b_ada)


def _ctx_kv_kernel(x_ref, sh_ref, sc_ref, nmix_ref, w_ref, kg_ref, bd_ref, k_ref, v_ref):
    h = _rms(x_ref[...], nmix_ref[...]) * (1.0 + sc_ref[...]) + sh_ref[...]
    p = _dot(h.astype(BF16), w_ref[...])
    ms = _group_mean_sq(p, bd_ref)
    k = p[:, :LANES] * lax.rsqrt(ms[:, :LANES] + EPS) * kg_ref[...]
    v = p[:, LANES:]
    lane = lax.broadcasted_iota(jnp.int32, k.shape, 1)
    ka, kb = _dup_halves(k, lane)
    k_ref[...] = jnp.concatenate([ka, kb], axis=-1).astype(BF16)
    v_ref[...] = _with_ones(v, lane).astype(BF16)


def _ctx_kv(ctx, mod4, ctx_row, nmix, w_kv, kg, bd):
    b, n_ctx, d = ctx.shape
    kv_w = 2 * LANES
    return pl.pallas_call(
        _ctx_kv_kernel,
        out_shape=(jax.ShapeDtypeStruct((b, n_ctx, kv_w), BF16),) * 2,
        grid=(b,),
        in_specs=[pl.BlockSpec((None, n_ctx, d), lambda i: (i, 0, 0)),
                  pl.BlockSpec((None, None, 1, d), lambda i: (ctx_row, 0, 0, 0)),
                  pl.BlockSpec((None, None, 1, d), lambda i: (ctx_row, 1, 0, 0)),
                  pl.BlockSpec((1, d), lambda i: (0, 0)),
                  pl.BlockSpec(w_kv.shape, lambda i: (0, 0)),
                  pl.BlockSpec((1, LANES), lambda i: (0, 0)),
                  pl.BlockSpec(bd.shape, lambda i: (0, 0))],
        out_specs=(pl.BlockSpec((None, n_ctx, kv_w), lambda i: (i, 0, 0)),) * 2,
        compiler_params=_cparams("arbitrary"),
        name="ctx_kv",
    )(ctx, mod4, mod4, nmix, w_kv, kg, bd)


def _inproj_kernel(x_ref, sh_ref, sc_ref, nmix_ref, w_ref, cos_ref, sin_ref, qg_ref, kg_ref, gmg_ref,
                   bd_ref, ws_ref, bs_ref, og_ref, q_ref, k_ref, v_ref, gm_ref, *, att_w, gm_w):
    tm = x_ref.shape[0]
    h = _rms(x_ref[...], nmix_ref[...]) * (1.0 + sc_ref[...]) + sh_ref[...]
    p = _dot(h.astype(BF16), w_ref[...])

    cos1, sin1 = cos_ref[...], sin_ref[...]
    lane1 = lax.broadcasted_iota(jnp.int32, (tm, LANES), 1)

    def rope(xn, cosw, sinw, lanew):
        w = xn.shape[1]
        fwd = pltpu.roll(xn, w - 32, axis=1)
        bwd = pltpu.roll(xn, 32, axis=1)
        swapped = jnp.where((lanew & 63) < 32, fwd, bwd)
        return xn * cosw + swapped * sinw

    q = p[:, :att_w]
    qn = q * lax.rsqrt(_group_mean_sq(q, bd_ref) + EPS) * qg_ref[...]
    reps = att_w // LANES
    cosq = jnp.concatenate([cos1] * reps, axis=-1)
    sinq = jnp.concatenate([sin1] * reps, axis=-1)
    laneq = lax.broadcasted_iota(jnp.int32, (tm, att_w), 1)
    q_ref[...] = rope(qn, cosq, sinq, laneq).astype(BF16)

    kv = p[:, att_w:att_w + 2 * LANES]
    ms = _group_mean_sq(kv, bd_ref)
    kn = kv[:, :LANES] * lax.rsqrt(ms[:, :LANES] + EPS) * kg_ref[...]
    k = rope(kn, cos1, sin1, lane1)
    v = kv[:, LANES:]
    ka, kb = _dup_halves(k, lane1)
    k_ref[...] = jnp.concatenate([ka, kb], axis=-1).astype(BF16)
    v_ref[...] = _with_ones(v, lane1).astype(BF16)

    u0 = att_w + 2 * LANES
    u = jax.nn.gelu(p[:, u0:u0 + gm_w])
    gt = jax.nn.gelu(p[:, u0 + gm_w:u0 + 2 * gm_w])
    gt = (gt * lax.rsqrt(_group_mean_sq(gt, bd_ref) + EPS) * gmg_ref[...]).astype(BF16)
    lo_half = lax.broadcasted_iota(jnp.int32, (CHUNK, LANES), 1) < LANES // 2
    rows = []
    for c in range(tm // CHUNK):
        blocks = []
        for j in range(gm_w // LANES):
            g = gt[c * CHUNK:(c + 1) * CHUNK, j * LANES:(j + 1) * LANES]
            r = _dot(ws_ref[j], g)
            blocks.append(jnp.where(lo_half, r[:CHUNK], r[CHUNK:]))
        rows.append(jnp.concatenate(blocks, axis=-1) + bs_ref[...])
    mixed = jnp.concatenate(rows, axis=0)
    gm_ref[...] = _rms(u * mixed, og_ref[...]).astype(BF16)


def _inproj(x, mod4, nmix, w_in, cos_t, sin_t, qg, kg, gmg, bd, ws, bs, og, *, att_w, gm_w, tm):
    b, length, d = x.shape
    kv_w = 2 * LANES
    n_in = w_in.shape[1]
    const = lambda shape: pl.BlockSpec(shape, lambda i, j: (0,) * len(shape))
    return pl.pallas_call(
        functools.partial(_inproj_kernel, att_w=att_w, gm_w=gm_w),
        out_shape=(jax.ShapeDtypeStruct((b, length, att_w), BF16),
                   jax.ShapeDtypeStruct((b, length, kv_w), BF16),
                   jax.ShapeDtypeStruct((b, length, kv_w), BF16),
                   jax.ShapeDtypeStruct((b, length, gm_w), BF16)),
        grid=(b, length // tm),
        in_specs=[pl.BlockSpec((None, tm, d), lambda i, j: (i, j, 0)),
                  pl.BlockSpec((None, None, 1, d), lambda i, j: (i, 0, 0, 0)),
                  pl.BlockSpec((None, None, 1, d), lambda i, j: (i, 1, 0, 0)),
                  const((1, d)),
                  const((d, n_in)),
                  pl.BlockSpec((tm, LANES), lambda i, j: (j, 0)),
                  pl.BlockSpec((tm, LANES), lambda i, j: (j, 0)),
                  const((1, att_w)), const((1, LANES)), const((1, gm_w)),
                  const(bd.shape), const(ws.shape), const(bs.shape), const((1, gm_w))],
        out_specs=(pl.BlockSpec((None, tm, att_w), lambda i, j: (i, j, 0)),
                   pl.BlockSpec((None, tm, kv_w), lambda i, j: (i, j, 0)),
                   pl.BlockSpec((None, tm, kv_w), lambda i, j: (i, j, 0)),
                   pl.BlockSpec((None, tm, gm_w), lambda i, j: (i, j, 0))),
        compiler_params=_cparams("arbitrary", "arbitrary"),
        name="inproj",
    )(x, mod4, mod4, nmix, w_in, cos_t, sin_t, qg, kg, gmg, bd, ws, bs, og)


def _attn_kernel(q_ref, kc_ref, kl_ref, vc_ref, vl_ref, o_ref, s_buf, p_buf, *, tq):
    n_ctx = kc_ref.shape[0]
    heads = q_ref.shape[1] // (LANES // 2)
    nj = q_ref.shape[0] // tq
    half = LANES // 2
    lo = lax.broadcasted_iota(jnp.int32, (tq, LANES), 1) < half

    @pl.when(jnp.logical_and(pl.program_id(0) == 0, pl.program_id(1) == 0))
    def _():
        s_buf[...] = jnp.zeros_like(s_buf)
        p_buf[...] = jnp.ones_like(p_buf)

    def scores(j, g):
        q = q_ref[pl.ds(j * tq, tq), (g // 2) * LANES:(g // 2 + 1) * LANES]
        zero = jnp.zeros_like(q)
        qh = jnp.where(lo, q, zero) if g % 2 == 0 else jnp.where(lo, zero, q)
        s_buf[g % 2, :, :n_ctx] = _dot_nt(qh, kc_ref[...])
        s_buf[g % 2, :, n_ctx:] = _dot_nt(qh, kl_ref[...])

    def probs(g):
        s = s_buf[g % 2]
        p_buf[g % 2] = jnp.exp2(s - jnp.max(s, axis=-1, keepdims=True)).astype(BF16)

    def output(j, g):
        p = p_buf[g % 2]
        w = _dot(p[:, :n_ctx], vc_ref[...]) + _dot(p[:, n_ctx:], vl_ref[...])
        t = jnp.where(pl.program_id(1) == 0, w[:, :LANES], w[:, LANES:])
        r = pltpu.roll(t, half, axis=1)
        rows = pl.ds(j * tq, tq)
        if g % 2 == 0:
            o_ref[rows, g * half:(g + 1) * half] = (t / r)[:, :half].astype(o_ref.dtype)
        else:
            o_ref[rows, g * half:(g + 1) * half] = (r / t)[:, half:].astype(o_ref.dtype)

    def tile_stages(j):
        j_prev = jnp.maximum(j - 1, 0)
        for g in range(heads):
            scores(j, g)
            probs((g - 1) % heads)
            output(j if g >= 2 else j_prev, (g - 2) % heads)

    def trip(jj, carry):
        for u in range(tiles_per_trip):
            tile_stages(jj * tiles_per_trip + u)
        return carry

    tiles_per_trip = 2 if nj % 2 == 0 else 1
    lax.fori_loop(0, nj // tiles_per_trip, trip, 0)
    probs(heads - 1)
    output(nj - 1, heads - 2)
    output(nj - 1, heads - 1)


def _unread_operand(in_specs, operands, array):
    in_specs.append(pl.BlockSpec(memory_space=pl.ANY))
    operands.append(array)
    return len(operands) - 1


def _attention(q, k_c, k_l, v_c, v_l, att_so_far, after, *, tq, b0, nb, slot0, slots):
    _, length, att_w = q.shape
    n_ctx = k_c.shape[1]
    group_w = att_w // ATT_KV_HEADS
    assert (group_w // (LANES // 2)) % 2 == 0 and length % tq == 0
    k_blk = lambda n: pl.BlockSpec((None, n, LANES), lambda i, h: (i + b0, 0, h))
    v_blk = lambda n: pl.BlockSpec((None, n, ATT_KV_HEADS * LANES), lambda i, h: (i + b0, 0, 0))
    in_specs = [pl.BlockSpec((None, length, group_w), lambda i, h: (i + b0, 0, h)),
                k_blk(n_ctx), k_blk(length), v_blk(n_ctx), v_blk(length)]
    operands = [q, k_c, k_l, v_c, v_l]
    n_read = len(operands)
    aliases = {}
    if att_so_far is not None:
        aliases = {_unread_operand(in_specs, operands, att_so_far): 0}
    if after is not None:
        _unread_operand(in_specs, operands, after)
    n_in = len(operands)
    return pl.pallas_call(
        lambda *refs: _attn_kernel(*refs[:n_read], *refs[n_in:], tq=tq),
        out_shape=jax.ShapeDtypeStruct((slots, length, att_w), BF16),
        grid=(nb, ATT_KV_HEADS),
        in_specs=in_specs,
        out_specs=pl.BlockSpec((None, length, group_w), lambda i, h: (i + slot0, 0, h)),
        scratch_shapes=[pltpu.VMEM((2, tq, n_ctx + length), F32), pltpu.VMEM((2, tq, n_ctx + length), BF16)],
        input_output_aliases=aliases,
        compiler_params=_cparams("arbitrary", "arbitrary"),
        name="attn",
    )(*operands)


def _post_kernel(att_ref, gm_ref, x_ref, g1_ref, sh2_ref, sc2_ref, g2_ref, oa_ref, wo_ref, nffn_ref,
                 wrt_ref, rb_ref, wsg_ref, wsd_ref,
                 base_ref, h2p_ref, idx_ref, gate_ref, rank_ref, cnt_ref, carry_ref):
    i = pl.program_id(0)
    tm = x_ref.shape[0]
    n_exp = wrt_ref.shape[0]

    @pl.when(i == 0)
    def _():
        carry_ref[...] = jnp.zeros_like(carry_ref)

    att_n = _rms(att_ref[...].astype(F32), oa_ref[...]).astype(BF16)
    y = _dot(jnp.concatenate([att_n, gm_ref[...]], axis=-1), wo_ref[...])
    x_new = x_ref[...] + g1_ref[...] * y
    h2 = _rms(x_new, nffn_ref[...]) * (1.0 + sc2_ref[...]) + sh2_ref[...]
    h2p_ref[...] = _pack_rows(h2)

    h_hi = h2.astype(BF16)
    scores_t = jax.nn.sigmoid(_dot_nt(wrt_ref[...], h_hi))
    reps = tm // LANES
    sel = scores_t + jnp.concatenate([rb_ref[...]] * reps, axis=-1)
    expert = lax.broadcasted_iota(jnp.int32, scores_t.shape, 0).astype(F32)
    neg = jnp.float32(-jnp.inf)

    onehot = jnp.zeros(scores_t.shape, F32)
    hits, idxs, svals = [], [], []
    for _k in range(TOP_K):
        m = jnp.max(sel, axis=0, keepdims=True)
        idx = jnp.min(jnp.where(sel == m, expert, float(n_exp)), axis=0, keepdims=True)
        hit = expert == idx
        hits.append(hit)
        idxs.append(idx)
        svals.append(jnp.sum(jnp.where(hit, scores_t, 0.0), axis=0, keepdims=True))
        onehot = onehot + jnp.where(hit, 1.0, 0.0)
        sel = jnp.where(hit, neg, sel)
    ssum = functools.reduce(lambda a, b: a + b, svals)

    r_io = lax.broadcasted_iota(jnp.int32, (tm, tm), 0)
    c_io = lax.broadcasted_iota(jnp.int32, (tm, tm), 1)
    tri = jnp.where(r_io < c_io, 1.0, 0.0).astype(BF16)
    carry = carry_ref[...]
    before = _dot(onehot.astype(BF16), tri) + jnp.concatenate([carry] * reps, axis=-1)
    carry = carry + jnp.sum(onehot, axis=1, keepdims=True)
    carry_ref[...] = carry
    cnt_ref[...] = carry

    row = lax.broadcasted_iota(jnp.int32, (idx_ref.shape[0], tm), 0)
    idx_o = jnp.zeros(row.shape, F32)
    rank_o = jnp.zeros(row.shape, F32)
    gate_o = jnp.zeros(row.shape, F32)
    for k in range(TOP_K):
        rank = jnp.sum(jnp.where(hits[k], before, 0.0), axis=0, keepdims=True)
        idx_o = jnp.where(row == k, idxs[k], idx_o)
        rank_o = jnp.where(row == k, rank, rank_o)
        gate_o = jnp.where(row == k, svals[k] / ssum * ROUTED_SCALE, gate_o)
    idx_ref[...] = idx_o.astype(jnp.int32)
    rank_ref[...] = rank_o.astype(jnp.int32)
    gate_pad = jnp.concatenate([gate_o, jnp.zeros((LANES - gate_o.shape[0], tm), F32)], axis=0)
    gate_ref[...] = gate_pad.T

    gu = _dot(h_hi, wsg_ref[...])
    hs = gu.shape[1] // 2
    act = (jax.nn.silu(gu[:, :hs]) * gu[:, hs:]).astype(BF16)
    base_ref[...] = x_new + g2_ref[...] * _dot(act, wsd_ref[...])


def _post(att, gmn, x2, mod4, oa, w_out, nffn, wr_t, rb, w_sg, w_sd, *, tm, tokens_per_batch, tok0):
    t, att_w = att.shape
    d = x2.shape[1]
    n_exp = wr_t.shape[0]
    gm_w = gmn.shape[1]
    tpb = tokens_per_batch // tm
    blk0 = tok0 // tm
    const = lambda shape: pl.BlockSpec(shape, lambda i: (0,) * len(shape))
    mrow = lambda j: pl.BlockSpec((None, None, 1, d), lambda i: ((i + blk0) // tpb, j, 0, 0))
    by_choice = jax.ShapeDtypeStruct((CHOICE_ROWS, t), jnp.int32)
    return pl.pallas_call(
        _post_kernel,
        out_shape=(jax.ShapeDtypeStruct((t, d), F32), jax.ShapeDtypeStruct((t, d // 2), jnp.int32),
                   by_choice, jax.ShapeDtypeStruct((t, LANES), F32), by_choice,
                   jax.ShapeDtypeStruct((n_exp, LANES), F32)),
        grid=(t // tm,),
        in_specs=[pl.BlockSpec((tm, att_w), lambda i: (i, 0)),
                  pl.BlockSpec((tm, gm_w), lambda i: (i + blk0, 0)),
                  pl.BlockSpec((tm, d), lambda i: (i + blk0, 0)),
                  mrow(2), mrow(3), mrow(4), mrow(5),
                  const((1, att_w)), const(w_out.shape), const((1, d)),
                  const(wr_t.shape), const(rb.shape),
                  const(w_sg.shape), const(w_sd.shape)],
        out_specs=(pl.BlockSpec((tm, d), lambda i: (i, 0)), pl.BlockSpec((tm, d // 2), lambda i: (i, 0)),
                   pl.BlockSpec((CHOICE_ROWS, tm), lambda i: (0, i)), pl.BlockSpec((tm, LANES), lambda i: (i, 0)),
                   pl.BlockSpec((CHOICE_ROWS, tm), lambda i: (0, i)), pl.BlockSpec((n_exp, LANES), lambda i: (0, 0))),
        scratch_shapes=[pltpu.VMEM((n_exp, LANES), F32)],
        compiler_params=_cparams("arbitrary"),
        name="post",
    )(att, gmn, x2, mod4, mod4, mod4, mod4, oa, w_out, nffn, wr_t, rb, w_sg, w_sd)


def _sc_mesh_info():
    info = plsc.get_sparse_core_info()
    mesh = plsc.VectorSubcoreMesh(core_axis_name="c", subcore_axis_name="s")
    return mesh, info.num_cores, info.num_subcores, info.num_lanes


def _sc_params():
    cp = pltpu.CompilerParams()
    if "needs_layout_passes" in pltpu.CompilerParams.__dataclass_fields__:
        cp = dataclasses.replace(cp, needs_layout_passes=False)
    return cp


def _pack_weights(w2, *, rows_per_step):
    n_rows, width = w2.shape
    half = width // 2
    mesh, n_cores, n_sub, n_lanes = _sc_mesh_info()
    per_worker = n_rows // (n_cores * n_sub)
    rs = rows_per_step
    assert n_rows % (n_cores * n_sub) == 0 and per_worker % rs == 0 and half % n_lanes == 0

    @functools.partial(
        pl.kernel, mesh=mesh,
        out_type=jax.ShapeDtypeStruct((n_rows, half), jnp.int32),
        scratch_types=[pltpu.VMEM((rs, width), F32), pltpu.VMEM((rs, half), jnp.int32)],
        compiler_params=_sc_params(),
    )
    def run(w_hbm, o_hbm, in_v, out_v):
        base = (lax.axis_index("s") * n_cores + lax.axis_index("c")) * per_worker

        def bf16_bits(x):
            u = lax.bitcast_convert_type(x, jnp.uint32)
            return (u + jnp.uint32(0x7FFF) + ((u >> 16) & jnp.uint32(1))) >> 16

        @pl.loop(0, per_worker // rs)
        def _(i):
            r0 = base + i * rs
            pltpu.sync_copy(w_hbm.at[pl.ds(r0, rs)], in_v)

            @pl.loop(0, rs)
            def _(r):
                for j in range(0, half, n_lanes):
                    lo = bf16_bits(in_v[r, pl.ds(j, n_lanes)])
                    hi = bf16_bits(in_v[r, pl.ds(half + j, n_lanes)])
                    out_v[r, pl.ds(j, n_lanes)] = lax.bitcast_convert_type(lo | (hi << 16), jnp.int32)

            pltpu.sync_copy(out_v, o_hbm.at[pl.ds(r0, rs)])

    return run(w2)


def _dispatch(h2p, idx_t, rank_t, offs, *, n_slots):
    t, dw = h2p.shape
    mesh, n_cores, n_sub, n_lanes = _sc_mesh_info()
    per_worker = t // (n_cores * n_sub)
    w = SC_WINDOW
    assert per_worker % w == 0 and w % n_lanes == 0

    @functools.partial(
        pl.kernel, mesh=mesh,
        out_type=(jax.ShapeDtypeStruct((n_slots, dw), jnp.int32), jax.ShapeDtypeStruct((CHOICE_ROWS, t), jnp.int32)),
        scratch_types=[pltpu.VMEM(offs.shape, jnp.int32)] + [pltpu.VMEM((TOP_K, w), jnp.int32)] * 3
        + [pltpu.VMEM((w, dw), jnp.int32), pltpu.SemaphoreType.DMA],
        compiler_params=_sc_params(),
    )
    def run(h_hbm, idx_hbm, rank_hbm, offs_hbm, xs_hbm, pos_hbm, offs_v, idx_v, rank_v, pos_v, rows_v, sem):
        base = (lax.axis_index("s") * n_cores + lax.axis_index("c")) * per_worker
        pltpu.sync_copy(offs_hbm, offs_v)

        @pl.loop(0, per_worker // w)
        def _(i):
            t0 = base + i * w
            pltpu.sync_copy(h_hbm.at[pl.ds(t0, w)], rows_v)
            for k in range(TOP_K):
                pltpu.sync_copy(idx_hbm.at[k, pl.ds(t0, w)], idx_v.at[k])
                pltpu.sync_copy(rank_hbm.at[k, pl.ds(t0, w)], rank_v.at[k])
            for k in range(TOP_K):
                for j in range(0, w, n_lanes):
                    group_start = plsc.load_gather(offs_v, [idx_v[k, pl.ds(j, n_lanes)]])
                    pos_v[k, pl.ds(j, n_lanes)] = group_start + rank_v[k, pl.ds(j, n_lanes)]
            copies = [pltpu.async_copy(rows_v, xs_hbm.at[pos_v.at[k]], sem) for k in range(TOP_K)]
            for k in range(TOP_K):
                pltpu.sync_copy(pos_v.at[k], pos_hbm.at[k, pl.ds(t0, w)])
            for cp in copies:
                cp.wait()

    return run(h2p, idx_t, rank_t, offs)


def _collect(ys, pos_t, *, tok_lo, n_tok):
    dw = ys.shape[1]
    mesh, n_cores, n_sub, _ = _sc_mesh_info()
    per_worker = n_tok // (n_cores * n_sub)
    w = SC_WINDOW
    assert per_worker % w == 0 and TOP_K % 2 == 0

    @functools.partial(
        pl.kernel, mesh=mesh,
        out_type=jax.ShapeDtypeStruct((TOP_K * n_tok, dw), jnp.int32),
        scratch_types=[pltpu.VMEM((TOP_K, w), jnp.int32), pltpu.VMEM((w, dw), jnp.int32),
                       pltpu.VMEM((w, dw), jnp.int32), pltpu.SemaphoreType.DMA, pltpu.SemaphoreType.DMA],
        compiler_params=_sc_params(),
    )
    def run(ys_hbm, pos_hbm, r_hbm, pos_v, rows_a, rows_b, sem_g, sem_w):
        base = (lax.axis_index("s") * n_cores + lax.axis_index("c")) * per_worker
        bufs = (rows_a, rows_b)

        @pl.loop(0, per_worker // w)
        def _(i):
            t0 = base + i * w
            for k in range(TOP_K):
                pltpu.sync_copy(pos_hbm.at[k, pl.ds(tok_lo + t0, w)], pos_v.at[k])
            gathers = [None] * TOP_K
            writes = [None] * TOP_K
            gathers[0] = pltpu.async_copy(ys_hbm.at[pos_v.at[0]], bufs[0], sem_g)
            for k in range(TOP_K):
                gathers[k].wait()
                if k + 1 < TOP_K:
                    if k >= 1:
                        writes[k - 1].wait()
                    gathers[k + 1] = pltpu.async_copy(ys_hbm.at[pos_v.at[k + 1]], bufs[(k + 1) % 2], sem_g)
                writes[k] = pltpu.async_copy(bufs[k % 2], r_hbm.at[pl.ds(k * n_tok + t0, w)], sem_w)
            writes[TOP_K - 2].wait()
            writes[TOP_K - 1].wait()

    return run(ys, pos_t)


def _experts_kernel(t0_ref, t1_ref, xs_hbm, wgu_ref, wdn_ref, ys_hbm, xbuf, ybuf, wgu_b, wdn_b, sem_in, sem_out):
    e = pl.program_id(0)
    n_exp = pl.num_programs(0)
    n_in, tr, half = xbuf.shape
    n_out = ybuf.shape[0]
    n_tiles = t1_ref[n_exp - 1]

    def fetch(t):
        slot = t & (n_in - 1)
        return pltpu.make_async_copy(xs_hbm.at[pl.ds(t * tr, tr)], xbuf.at[slot], sem_in.at[slot])

    def flush(t):
        slot = t & (n_out - 1)
        return pltpu.make_async_copy(ybuf.at[slot], ys_hbm.at[pl.ds(t * tr, tr)], sem_out.at[slot])

    @pl.when(e == 0)
    def _():
        for t in range(n_in - 1):
            @pl.when(t < n_tiles)
            def _():
                fetch(t).start()

    for packed, dense in ((wgu_ref, wgu_b), (wdn_ref, wdn_b)):
        w_half = packed.shape[1]
        w_lo, w_hi = _unpack_rows(packed[...])
        dense[:, :w_half] = w_lo.astype(BF16)
        dense[:, w_half:] = w_hi.astype(BF16)

    def tile(t, carry):
        fetch(t).wait()

        @pl.when(t + n_in - 1 < n_tiles)
        def _():
            fetch(t + n_in - 1).start()

        lo, hi = _unpack_rows(xbuf[t & (n_in - 1)])
        gu = _dot(lo.astype(BF16), wgu_b[:half, :]) + _dot(hi.astype(BF16), wgu_b[half:, :])
        hh = gu.shape[1] // 2
        act = (jax.nn.silu(gu[:, :hh]) * gu[:, hh:]).astype(BF16)
        y = _pack_rows(_dot(act, wdn_b[...]))

        @pl.when(t >= n_out)
        def _():
            flush(t - n_out).wait()

        ybuf[t & (n_out - 1)] = y
        flush(t).start()
        return carry

    lax.fori_loop(t0_ref[e], t1_ref[e], tile, 0)

    @pl.when(e == n_exp - 1)
    def _():
        for back in range(n_out, 0, -1):
            @pl.when(n_tiles >= back)
            def _():
                flush(n_tiles - back).wait()


def _experts(tile_lo, tile_hi, xs, w_gu, w_dn, after, *, tr):
    n_exp, d, h2w = w_gu.shape[0], w_gu.shape[1], 2 * w_gu.shape[2]
    hh = w_dn.shape[1]
    in_specs = [pl.BlockSpec(memory_space=pl.ANY),
                pl.BlockSpec((None, d, h2w // 2), lambda e, lo, hi: (e, 0, 0)),
                pl.BlockSpec((None, hh, d // 2), lambda e, lo, hi: (e, 0, 0))]
    operands = [xs, w_gu, w_dn]
    n_read = 2 + len(operands)
    if after is not None:
        _unread_operand(in_specs, operands, after)
    n_in = 2 + len(operands)
    return pl.pallas_call(
        lambda *refs: _experts_kernel(*refs[:n_read], *refs[n_in:]),
        out_shape=jax.ShapeDtypeStruct(xs.shape, jnp.int32),
        grid_spec=pltpu.PrefetchScalarGridSpec(
            num_scalar_prefetch=2,
            grid=(n_exp,),
            in_specs=in_specs,
            out_specs=pl.BlockSpec(memory_space=pl.ANY),
            scratch_shapes=[pltpu.VMEM((EXPERT_IN_SLOTS, tr, d // 2), jnp.int32),
                            pltpu.VMEM((EXPERT_OUT_SLOTS, tr, d // 2), jnp.int32),
                            pltpu.VMEM((d, h2w), BF16), pltpu.VMEM((hh, d), BF16),
                            pltpu.SemaphoreType.DMA((EXPERT_IN_SLOTS,)), pltpu.SemaphoreType.DMA((EXPERT_OUT_SLOTS,))]),
        compiler_params=_cparams("arbitrary"),
        name="experts",
    )(tile_lo, tile_hi, *operands)


def _combine_kernel(base_ref, g2_ref, gate_ref, *refs):
    r_refs, o_ref = refs[:TOP_K], refs[TOP_K]
    gate = gate_ref[...]
    half = base_ref.shape[1] // 2
    acc_lo = jnp.zeros((base_ref.shape[0], half), F32)
    acc_hi = jnp.zeros((base_ref.shape[0], half), F32)
    for k in range(TOP_K):
        lo, hi = _unpack_rows(r_refs[k][...])
        g = gate[:, k:k + 1]
        acc_lo = acc_lo + g * lo
        acc_hi = acc_hi + g * hi
    g2 = g2_ref[...]
    o_ref[:, :half] = base_ref[:, :half] + g2[:, :half] * acc_lo
    o_ref[:, half:] = base_ref[:, half:] + g2[:, half:] * acc_hi


def _combine(base, mod4, gate, r, out_so_far, after, *, tm, tokens_per_batch, tok0, part_lo, total_tokens):
    d = base.shape[1]
    t = r.shape[0] // TOP_K
    tpb = tokens_per_batch // tm
    nblk = t // tm
    pblk0 = part_lo // tm
    blk0 = (tok0 + part_lo) // tm
    r_spec = lambda k: pl.BlockSpec((tm, d // 2), lambda i: (k * nblk + i, 0))
    in_specs = [pl.BlockSpec((tm, d), lambda i: (i + pblk0, 0)),
                pl.BlockSpec((None, None, 1, d), lambda i: ((i + blk0) // tpb, 5, 0, 0)),
                pl.BlockSpec((tm, LANES), lambda i: (i + pblk0, 0))] + [r_spec(k) for k in range(TOP_K)]
    operands = [base, mod4, gate] + [r] * TOP_K
    n_read = len(operands)
    aliases = {}
    if out_so_far is not None:
        aliases = {_unread_operand(in_specs, operands, out_so_far): 0}
    if after is not None:
        _unread_operand(in_specs, operands, after)
    n_in = len(operands)
    return pl.pallas_call(
        lambda *refs: _combine_kernel(*refs[:n_read], *refs[n_in:]),
        out_shape=jax.ShapeDtypeStruct((total_tokens, d), F32),
        grid=(nblk,),
        in_specs=in_specs,
        out_specs=pl.BlockSpec((tm, d), lambda i: (i + blk0, 0)),
        input_output_aliases=aliases,
        compiler_params=_cparams("arbitrary"),
        name="combine",
    )(*operands)


def _rope_tables(length, head_dim):
    rows = length // GRID_W
    pairs_axis = head_dim // 4
    r = jnp.repeat(jnp.arange(rows, dtype=F32), GRID_W)
    col = jnp.tile(jnp.arange(GRID_W, dtype=F32), rows)
    inv = ROPE_THETA ** (-jnp.arange(pairs_axis, dtype=F32) / pairs_axis)
    ang = jnp.concatenate([r[:, None] * inv, col[:, None] * inv], axis=-1)
    cos, sin = jnp.cos(ang), jnp.sin(ang)
    cos_h = jnp.concatenate([cos, cos], axis=-1)
    sin_h = jnp.concatenate([-sin, sin], axis=-1)
    reps = LANES // head_dim
    return jnp.tile(cos_h, (1, reps)), jnp.tile(sin_h, (1, reps))


def _part_batches(b):
    if b < 2:
        return [b]
    lead = max(1, min(b - 1, round(LEAD_PART_SHARE * b)))
    return [lead, b - lead]


def kernel(x, c, ctx, c_ctx, w_ada, b_ada, norm_mix, w_in, q_norm, k_norm, gm_norm, w_spatial, b_spatial,
           out_norm_attn, out_norm_gm, w_out, norm_ffn, w_router, router_bias, w_exp_gu, w_exp_down,
           w_sh_gu, w_sh_down):
    assert w_ada.shape[0] == 1, "single-layer kernel"
    b, length, d = x.shape
    head_dim = q_norm.shape[-1]
    att_w = out_norm_attn.shape[-1]
    gm_w = out_norm_gm.shape[-1]
    gm_groups = w_spatial.shape[1]
    gm_hd = gm_w // gm_groups
    n_exp = w_router.shape[-1]
    t = b * length
    assert head_dim == LANES // 2 and gm_hd == LANES // 2 and w_spatial.shape[-1] == CHUNK
    assert n_exp <= LANES and att_w % MXU_EDGE == 0 and gm_w % MXU_EDGE == 0

    mod_rows = 16
    c_all = jnp.zeros((mod_rows, d), F32).at[:b].set(c).at[b].set(c_ctx)
    mod = _ada(c_all, w_ada[0], b_ada)
    mod4 = mod.reshape(mod_rows, N_MOD, 1, d)

    w_in_b = w_in[0].astype(BF16)
    bd = (jnp.arange(MXU_EDGE)[:, None] // head_dim == jnp.arange(MXU_EDGE)[None, :] // head_dim)
    bd = (bd.astype(F32) / head_dim).astype(BF16)
    qg = jnp.tile(q_norm[0] * (head_dim ** -0.5 * LOG2_E), att_w // head_dim)[None, :]
    kg = jnp.tile(k_norm[0], LANES // head_dim)[None, :]
    gmg = jnp.tile(gm_norm[0], gm_groups)[None, :]
    ws = w_spatial[0].astype(BF16).reshape(gm_groups // 2, 2 * CHUNK, CHUNK)
    bs = jnp.repeat(b_spatial[0].T, gm_hd, axis=1)
    cos_t, sin_t = _rope_tables(length, head_dim)

    k_c, v_c = _ctx_kv(ctx, mod4, b, norm_mix, w_in_b[:, att_w:att_w + 2 * LANES], kg, bd)
    q, k_l, v_l, gmn = _inproj(x, mod4, norm_mix, w_in_b, cos_t, sin_t, qg, kg, gmg, bd, ws, bs,
                               out_norm_gm, att_w=att_w, gm_w=gm_w, tm=min(1024, length))
    wr_t = w_router[0].T.astype(BF16)
    rb = jnp.tile(router_bias[0][:, None], (1, LANES))
    w_out_b, w_sg_b, w_sd_b = w_out[0].astype(BF16), w_sh_gu[0].astype(BF16), w_sh_down[0].astype(BF16)
    gmn2, x2 = gmn.reshape(t, gm_w), x.reshape(t, d)
    exp_h = w_exp_down.shape[2]
    wgu_p = _pack_weights(w_exp_gu[0].reshape(n_exp * d, 2 * exp_h), rows_per_step=32).reshape(n_exp, d, exp_h)
    wdn_p = _pack_weights(w_exp_down[0].reshape(n_exp * exp_h, d), rows_per_step=16).reshape(n_exp, exp_h, d // 2)
    tm_post = min(1024, length)
    tr = 512

    part_nb = _part_batches(b)
    n_parts = len(part_nb)
    part_b0 = [sum(part_nb[:p]) for p in range(n_parts)]
    state = [dict() for _ in range(n_parts)]

    last_tc = [None]

    def halves(part):
        nb = part_nb[part]
        return (nb // 2, nb - nb // 2) if nb >= 2 else (nb,)

    def attend(part, half):
        if half >= len(halves(part)):
            return
        st = state[part]
        slot0 = sum(halves(part)[:half])
        after = None if last_tc[0] is st.get("att") else last_tc[0]
        st["att"] = _attention(q, k_c, k_l, v_c, v_l, st.get("att"), after, tq=min(256, length),
                               b0=part_b0[part] + slot0, nb=halves(part)[half], slot0=slot0, slots=part_nb[part])
        last_tc[0] = st["att"]

    def route(part):
        st = state[part]
        tp = part_nb[part] * length
        nt = tp * TOP_K // tr + n_exp
        base, h2p, idx_t, gate, rank_t, cnt = _post(
            st["att"].reshape(tp, att_w), gmn2, x2, mod4, out_norm_attn, w_out_b, norm_ffn, wr_t, rb, w_sg_b, w_sd_b,
            tm=tm_post, tokens_per_batch=length, tok0=part_b0[part] * length)
        counts = cnt[:, 0].astype(jnp.int32)
        padded = (counts + tr - 1) // tr * tr
        ends = jnp.cumsum(padded)
        offs = ends - padded
        offs_pad = jnp.zeros((LANES,), jnp.int32).at[:n_exp].set(offs)
        xs, pos_t = _dispatch(h2p, idx_t, rank_t, offs_pad, n_slots=nt * tr)
        st.update(base=base, gate=gate, xs=xs, pos_t=pos_t, tile_lo=offs // tr, tile_hi=ends // tr)
        last_tc[0] = base

    def run_experts(part):
        st = state[part]
        ys = _experts(st["tile_lo"], st["tile_hi"], st["xs"], wgu_p, wdn_p, last_tc[0], tr=tr)
        n_chunks = part_nb[part] if part == n_parts - 1 else 1
        chunk = part_nb[part] * length // n_chunks
        st["r"] = [(c * chunk, _collect(ys, st["pos_t"], tok_lo=c * chunk, n_tok=chunk)) for c in range(n_chunks)]
        last_tc[0] = ys

    result = [None]

    def finish(part):
        st = state[part]
        for part_lo, r in st["r"]:
            after = None if last_tc[0] is result[0] or last_tc[0] is st["base"] else last_tc[0]
            result[0] = _combine(st["base"], mod4, st["gate"], r, result[0], after, tm=min(512, length),
                                 tokens_per_batch=length, tok0=part_b0[part] * length, part_lo=part_lo,
                                 total_tokens=t)
            last_tc[0] = result[0]

    stages = [(0, lambda p: attend(p, 0)), (1, lambda p: attend(p, 1)), (2, route), (4, run_experts), (7, finish)]
    plan = sorted((when + 3 * part, part, n, fn) for part in range(n_parts) for n, (when, fn) in enumerate(stages))
    for _, part, _, fn in plan:
        fn(part)
    return result[0].reshape(b, length, d)
```

```python
import dataclasses
import functools

import jax
import jax.numpy as jnp
from jax import lax
from jax.experimental import pallas as pl
from jax.experimental.pallas import tpu as pltpu
from jax.experimental.pallas import tpu_sc as plsc

F32 = jnp.float32
BF16 = jnp.bfloat16

EPS = 1e-6
GRID_W = 64
ROPE_THETA = 10000.0
ATT_KV_HEADS = 2
TOP_K = 6
ROUTED_SCALE = 2.5
N_MOD = 6
CHUNK = 128
LOG2_E = 1.4426950408889634

LANES = 128
MXU_EDGE = 256
VMEM_LIMIT_BYTES = 56 * 1024 * 1024
SUBLANES = 8
CHOICE_ROWS = SUBLANES
SC_WINDOW = 64
LEAD_PART_SHARE = 0.75
EXPERT_IN_SLOTS = 8
EXPERT_OUT_SLOTS = 4


def _cparams(*sem):
    return pltpu.CompilerParams(dimension_semantics=sem, vmem_limit_bytes=VMEM_LIMIT_BYTES)


def _dot(a, b):
    return jnp.dot(a, b, preferred_element_type=F32)


def _dot_nt(a, b):
    return lax.dot_general(a, b, (((1,), (1,)), ((), ())), preferred_element_type=F32)


def _rms(x, g):
    return x * lax.rsqrt(jnp.mean(x * x, axis=-1, keepdims=True) + EPS) * g


def _group_mean_sq(x, bd_ref):
    x2 = (x * x).astype(BF16)
    parts = [_dot(x2[:, i:i + MXU_EDGE], bd_ref[...]) for i in range(0, x.shape[1], MXU_EDGE)]
    return parts[0] if len(parts) == 1 else jnp.concatenate(parts, axis=-1)


def _dup_halves(x, lane):
    xr = pltpu.roll(x, LANES // 2, axis=1)
    lo = lane < LANES // 2
    return jnp.where(lo, x, xr), jnp.where(lo, xr, x)


def _with_ones(x, lane):
    xr = pltpu.roll(x, LANES // 2, axis=1)
    lo = lane < LANES // 2
    return jnp.concatenate([jnp.where(lo, x, 1.0), jnp.where(lo, xr, 1.0)], axis=-1)


_HI16 = 0xFFFF0000


def _pack_rows(x):
    half = x.shape[1] // 2
    rounded = x.astype(BF16).astype(F32)
    bits = lax.bitcast_convert_type(rounded, jnp.uint32)
    word = (bits[:, :half] >> 16) | (bits[:, half:] & jnp.uint32(_HI16))
    return lax.bitcast_convert_type(word, jnp.int32)


def _unpack_rows(w):
    bits = lax.bitcast_convert_type(w, jnp.uint32)
    lo = lax.bitcast_convert_type(bits << 16, F32)
    hi = lax.bitcast_convert_type(bits & jnp.uint32(_HI16), F32)
    return lo, hi


def _ada_kernel(c_ref, w_ref, b_ref, o_ref):
    s = jax.nn.silu(c_ref[...])
    o_ref[...] = jnp.dot(s, w_ref[...], precision=lax.Precision.HIGHEST,
                         preferred_element_type=F32) + b_ref[...]


def _ada(c_all, w_ada, b_ada):
    rows, d = c_all.shape
    n = w_ada.shape[1]
    bn = 2048 if n % 2048 == 0 else 512
    return pl.pallas_call(
        _ada_kernel,
        out_shape=jax.ShapeDtypeStruct((rows, n), F32),
        grid=(n // bn,),
        in_specs=[pl.BlockSpec((rows, d), lambda j: (0, 0)),
                  pl.BlockSpec((d, bn), lambda j: (0, j)),
                  pl.BlockSpec((1, bn), lambda j: (0, j))],
        out_specs=pl.BlockSpec((rows, bn), lambda j: (0, j)),
        compiler_params=_cparams("arbitrary"),
        name="ada",
    )(c_all, w_ada, b_ada)


def _ctx_kv_kernel(x_ref, sh_ref, sc_ref, nmix_ref, w_ref, kg_ref, bd_ref, k_ref, v_ref):
    h = _rms(x_ref[...], nmix_ref[...]) * (1.0 + sc_ref[...]) + sh_ref[...]
    p = _dot(h.astype(BF16), w_ref[...])
    ms = _group_mean_sq(p, bd_ref)
    k = p[:, :LANES] * lax.rsqrt(ms[:, :LANES] + EPS) * kg_ref[...]
    v = p[:, LANES:]
    lane = lax.broadcasted_iota(jnp.int32, k.shape, 1)
    ka, kb = _dup_halves(k, lane)
    k_ref[...] = jnp.concatenate([ka, kb], axis=-1).astype(BF16)
    v_ref[...] = _with_ones(v, lane).astype(BF16)


def _ctx_kv(ctx, mod4, ctx_row, nmix, w_kv, kg, bd):
    b, n_ctx, d = ctx.shape
    kv_w = 2 * LANES
    return pl.pallas_call(
        _ctx_kv_kernel,
        out_shape=(jax.ShapeDtypeStruct((b, n_ctx, kv_w), BF16),) * 2,
        grid=(b,),
        in_specs=[pl.BlockSpec((None, n_ctx, d), lambda i: (i, 0, 0)),
                  pl.BlockSpec((None, None, 1, d), lambda i: (ctx_row, 0, 0, 0)),
                  pl.BlockSpec((None, None, 1, d), lambda i: (ctx_row, 1, 0, 0)),
                  pl.BlockSpec((1, d), lambda i: (0, 0)),
                  pl.BlockSpec(w_kv.shape, lambda i: (0, 0)),
                  pl.BlockSpec((1, LANES), lambda i: (0, 0)),
                  pl.BlockSpec(bd.shape, lambda i: (0, 0))],
        out_specs=(pl.BlockSpec((None, n_ctx, kv_w), lambda i: (i, 0, 0)),) * 2,
        compiler_params=_cparams("arbitrary"),
        name="ctx_kv",
    )(ctx, mod4, mod4, nmix, w_kv, kg, bd)


def _inproj_kernel(x_ref, sh_ref, sc_ref, nmix_ref, w_ref, cos_ref, sin_ref, qg_ref, kg_ref, gmg_ref,
                   bd_ref, ws_ref, bs_ref, og_ref, q_ref, k_ref, v_ref, gm_ref, *, att_w, gm_w):
    tm = x_ref.shape[0]
    h = _rms(x_ref[...], nmix_ref[...]) * (1.0 + sc_ref[...]) + sh_ref[...]
    p = _dot(h.astype(BF16), w_ref[...])

    cos1, sin1 = cos_ref[...], sin_ref[...]
    lane1 = lax.broadcasted_iota(jnp.int32, (tm, LANES), 1)

    def rope(xn, cosw, sinw, lanew):
        w = xn.shape[1]
        fwd = pltpu.roll(xn, w - 32, axis=1)
        bwd = pltpu.roll(xn, 32, axis=1)
        swapped = jnp.where((lanew & 63) < 32, fwd, bwd)
        return xn * cosw + swapped * sinw

    q = p[:, :att_w]
    qn = q * lax.rsqrt(_group_mean_sq(q, bd_ref) + EPS) * qg_ref[...]
    reps = att_w // LANES
    cosq = jnp.concatenate([cos1] * reps, axis=-1)
    sinq = jnp.concatenate([sin1] * reps, axis=-1)
    laneq = lax.broadcasted_iota(jnp.int32, (tm, att_w), 1)
    q_ref[...] = rope(qn, cosq, sinq, laneq).astype(BF16)

    kv = p[:, att_w:att_w + 2 * LANES]
    ms = _group_mean_sq(kv, bd_ref)
    kn = kv[:, :LANES] * lax.rsqrt(ms[:, :LANES] + EPS) * kg_ref[...]
    k = rope(kn, cos1, sin1, lane1)
    v = kv[:, LANES:]
    ka, kb = _dup_halves(k, lane1)
    k_ref[...] = jnp.concatenate([ka, kb], axis=-1).astype(BF16)
    v_ref[...] = _with_ones(v, lane1).astype(BF16)

    u0 = att_w + 2 * LANES
    u = jax.nn.gelu(p[:, u0:u0 + gm_w])
    gt = jax.nn.gelu(p[:, u0 + gm_w:u0 + 2 * gm_w])
    gt = (gt * lax.rsqrt(_group_mean_sq(gt, bd_ref) + EPS) * gmg_ref[...]).astype(BF16)
    lo_half = lax.broadcasted_iota(jnp.int32, (CHUNK, LANES), 1) < LANES // 2
    rows = []
    for c in range(tm // CHUNK):
        blocks = []
        for j in range(gm_w // LANES):
            g = gt[c * CHUNK:(c + 1) * CHUNK, j * LANES:(j + 1) * LANES]
            r = _dot(ws_ref[j], g)
            blocks.append(jnp.where(lo_half, r[:CHUNK], r[CHUNK:]))
        rows.append(jnp.concatenate(blocks, axis=-1) + bs_ref[...])
    mixed = jnp.concatenate(rows, axis=0)
    gm_ref[...] = _rms(u * mixed, og_ref[...]).astype(BF16)


def _inproj(x, mod4, nmix, w_in, cos_t, sin_t, qg, kg, gmg, bd, ws, bs, og, *, att_w, gm_w, tm):
    b, length, d = x.shape
    kv_w = 2 * LANES
    n_in = w_in.shape[1]
    const = lambda shape: pl.BlockSpec(shape, lambda i, j: (0,) * len(shape))
    return pl.pallas_call(
        functools.partial(_inproj_kernel, att_w=att_w, gm_w=gm_w),
        out_shape=(jax.ShapeDtypeStruct((b, length, att_w), BF16),
                   jax.ShapeDtypeStruct((b, length, kv_w), BF16),
                   jax.ShapeDtypeStruct((b, length, kv_w), BF16),
                   jax.ShapeDtypeStruct((b, length, gm_w), BF16)),
        grid=(b, length // tm),
        in_specs=[pl.BlockSpec((None, tm, d), lambda i, j: (i, j, 0)),
                  pl.BlockSpec((None, None, 1, d), lambda i, j: (i, 0, 0, 0)),
                  pl.BlockSpec((None, None, 1, d), lambda i, j: (i, 1, 0, 0)),
                  const((1, d)),
                  const((d, n_in)),
                  pl.BlockSpec((tm, LANES), lambda i, j: (j, 0)),
                  pl.BlockSpec((tm, LANES), lambda i, j: (j, 0)),
                  const((1, att_w)), const((1, LANES)), const((1, gm_w)),
                  const(bd.shape), const(ws.shape), const(bs.shape), const((1, gm_w))],
        out_specs=(pl.BlockSpec((None, tm, att_w), lambda i, j: (i, j, 0)),
                   pl.BlockSpec((None, tm, kv_w), lambda i, j: (i, j, 0)),
                   pl.BlockSpec((None, tm, kv_w), lambda i, j: (i, j, 0)),
                   pl.BlockSpec((None, tm, gm_w), lambda i, j: (i, j, 0))),
        compiler_params=_cparams("arbitrary", "arbitrary"),
        name="inproj",
    )(x, mod4, mod4, nmix, w_in, cos_t, sin_t, qg, kg, gmg, bd, ws, bs, og)


def _attn_kernel(q_ref, kc_ref, kl_ref, vc_ref, vl_ref, o_ref, s_buf, p_buf, *, tq):
    n_ctx = kc_ref.shape[0]
    heads = q_ref.shape[1] // (LANES // 2)
    nj = q_ref.shape[0] // tq
    half = LANES // 2
    lo = lax.broadcasted_iota(jnp.int32, (tq, LANES), 1) < half

    @pl.when(jnp.logical_and(pl.program_id(0) == 0, pl.program_id(1) == 0))
    def _():
        s_buf[...] = jnp.zeros_like(s_buf)
        p_buf[...] = jnp.ones_like(p_buf)

    def scores(j, g):
        q = q_ref[pl.ds(j * tq, tq), (g // 2) * LANES:(g // 2 + 1) * LANES]
        zero = jnp.zeros_like(q)
        qh = jnp.where(lo, q, zero) if g % 2 == 0 else jnp.where(lo, zero, q)
        s_buf[g % 2, :, :n_ctx] = _dot_nt(qh, kc_ref[...])
        s_buf[g % 2, :, n_ctx:] = _dot_nt(qh, kl_ref[...])

    def probs(g):
        s = s_buf[g % 2]
        p_buf[g % 2] = jnp.exp2(s - jnp.max(s, axis=-1, keepdims=True)).astype(BF16)

    def output(j, g):
        p = p_buf[g % 2]
        w = _dot(p[:, :n_ctx], vc_ref[...]) + _dot(p[:, n_ctx:], vl_ref[...])
        t = jnp.where(pl.program_id(1) == 0, w[:, :LANES], w[:, LANES:])
        r = pltpu.roll(t, half, axis=1)
        rows = pl.ds(j * tq, tq)
        if g % 2 == 0:
            o_ref[rows, g * half:(g + 1) * half] = (t / r)[:, :half].astype(o_ref.dtype)
        else:
            o_ref[rows, g * half:(g + 1) * half] = (r / t)[:, half:].astype(o_ref.dtype)

    def tile_stages(j):
        j_prev = jnp.maximum(j - 1, 0)
        for g in range(heads):
            scores(j, g)
            probs((g - 1) % heads)
            output(j if g >= 2 else j_prev, (g - 2) % heads)

    def trip(jj, carry):
        for u in range(tiles_per_trip):
            tile_stages(jj * tiles_per_trip + u)
        return carry

    tiles_per_trip = 2 if nj % 2 == 0 else 1
    lax.fori_loop(0, nj // tiles_per_trip, trip, 0)
    probs(heads - 1)
    output(nj - 1, heads - 2)
    output(nj - 1, heads - 1)


def _unread_operand(in_specs, operands, array):
    in_specs.append(pl.BlockSpec(memory_space=pl.ANY))
    operands.append(array)
    return len(operands) - 1


def _attention(q, k_c, k_l, v_c, v_l, att_so_far, after, *, tq, b0, nb, slot0, slots):
    _, length, att_w = q.shape
    n_ctx = k_c.shape[1]
    group_w = att_w // ATT_KV_HEADS
    assert (group_w // (LANES // 2)) % 2 == 0 and length % tq == 0
    k_blk = lambda n: pl.BlockSpec((None, n, LANES), lambda i, h: (i + b0, 0, h))
    v_blk = lambda n: pl.BlockSpec((None, n, ATT_KV_HEADS * LANES), lambda i, h: (i + b0, 0, 0))
    in_specs = [pl.BlockSpec((None, length, group_w), lambda i, h: (i + b0, 0, h)),
                k_blk(n_ctx), k_blk(length), v_blk(n_ctx), v_blk(length)]
    operands = [q, k_c, k_l, v_c, v_l]
    n_read = len(operands)
    aliases = {}
    if att_so_far is not None:
        aliases = {_unread_operand(in_specs, operands, att_so_far): 0}
    if after is not None:
        _unread_operand(in_specs, operands, after)
    n_in = len(operands)
    return pl.pallas_call(
        lambda *refs: _attn_kernel(*refs[:n_read], *refs[n_in:], tq=tq),
        out_shape=jax.ShapeDtypeStruct((slots, length, att_w), BF16),
        grid=(nb, ATT_KV_HEADS),
        in_specs=in_specs,
        out_specs=pl.BlockSpec((None, length, group_w), lambda i, h: (i + slot0, 0, h)),
        scratch_shapes=[pltpu.VMEM((2, tq, n_ctx + length), F32), pltpu.VMEM((2, tq, n_ctx + length), BF16)],
        input_output_aliases=aliases,
        compiler_params=_cparams("arbitrary", "arbitrary"),
        name="attn",
    )(*operands)


def _post_kernel(att_ref, gm_ref, x_ref, g1_ref, sh2_ref, sc2_ref, g2_ref, oa_ref, wo_ref, nffn_ref,
                 wrt_ref, rb_ref, wsg_ref, wsd_ref,
                 base_ref, h2p_ref, idx_ref, gate_ref, rank_ref, cnt_ref, carry_ref):
    i = pl.program_id(0)
    tm = x_ref.shape[0]
    n_exp = wrt_ref.shape[0]

    @pl.when(i == 0)
    def _():
        carry_ref[...] = jnp.zeros_like(carry_ref)

    att_n = _rms(att_ref[...].astype(F32), oa_ref[...]).astype(BF16)
    y = _dot(jnp.concatenate([att_n, gm_ref[...]], axis=-1), wo_ref[...])
    x_new = x_ref[...] + g1_ref[...] * y
    h2 = _rms(x_new, nffn_ref[...]) * (1.0 + sc2_ref[...]) + sh2_ref[...]
    h2p_ref[...] = _pack_rows(h2)

    h_hi = h2.astype(BF16)
    scores_t = jax.nn.sigmoid(_dot_nt(wrt_ref[...], h_hi))
    reps = tm // LANES
    sel = scores_t + jnp.concatenate([rb_ref[...]] * reps, axis=-1)
    expert = lax.broadcasted_iota(jnp.int32, scores_t.shape, 0).astype(F32)
    neg = jnp.float32(-jnp.inf)

    onehot = jnp.zeros(scores_t.shape, F32)
    hits, idxs, svals = [], [], []
    for _k in range(TOP_K):
        m = jnp.max(sel, axis=0, keepdims=True)
        idx = jnp.min(jnp.where(sel == m, expert, float(n_exp)), axis=0, keepdims=True)
        hit = expert == idx
        hits.append(hit)
        idxs.append(idx)
        svals.append(jnp.sum(jnp.where(hit, scores_t, 0.0), axis=0, keepdims=True))
        onehot = onehot + jnp.where(hit, 1.0, 0.0)
        sel = jnp.where(hit, neg, sel)
    ssum = functools.reduce(lambda a, b: a + b, svals)

    r_io = lax.broadcasted_iota(jnp.int32, (tm, tm), 0)
    c_io = lax.broadcasted_iota(jnp.int32, (tm, tm), 1)
    tri = jnp.where(r_io < c_io, 1.0, 0.0).astype(BF16)
    carry = carry_ref[...]
    before = _dot(onehot.astype(BF16), tri) + jnp.concatenate([carry] * reps, axis=-1)
    carry = carry + jnp.sum(onehot, axis=1, keepdims=True)
    carry_ref[...] = carry
    cnt_ref[...] = carry

    row = lax.broadcasted_iota(jnp.int32, (idx_ref.shape[0], tm), 0)
    idx_o = jnp.zeros(row.shape, F32)
    rank_o = jnp.zeros(row.shape, F32)
    gate_o = jnp.zeros(row.shape, F32)
    for k in range(TOP_K):
        rank = jnp.sum(jnp.where(hits[k], before, 0.0), axis=0, keepdims=True)
        idx_o = jnp.where(row == k, idxs[k], idx_o)
        rank_o = jnp.where(row == k, rank, rank_o)
        gate_o = jnp.where(row == k, svals[k] / ssum * ROUTED_SCALE, gate_o)
    idx_ref[...] = idx_o.astype(jnp.int32)
    rank_ref[...] = rank_o.astype(jnp.int32)
    gate_pad = jnp.concatenate([gate_o, jnp.zeros((LANES - gate_o.shape[0], tm), F32)], axis=0)
    gate_ref[...] = gate_pad.T

    gu = _dot(h_hi, wsg_ref[...])
    hs = gu.shape[1] // 2
    act = (jax.nn.silu(gu[:, :hs]) * gu[:, hs:]).astype(BF16)
    base_ref[...] = x_new + g2_ref[...] * _dot(act, wsd_ref[...])


def _post(att, gmn, x2, mod4, oa, w_out, nffn, wr_t, rb, w_sg, w_sd, *, tm, tokens_per_batch, tok0):
    t, att_w = att.shape
    d = x2.shape[1]
    n_exp = wr_t.shape[0]
    gm_w = gmn.shape[1]
    tpb = tokens_per_batch // tm
    blk0 = tok0 // tm
    const = lambda shape: pl.BlockSpec(shape, lambda i: (0,) * len(shape))
    mrow = lambda j: pl.BlockSpec((None, None, 1, d), lambda i: ((i + blk0) // tpb, j, 0, 0))
    by_choice = jax.ShapeDtypeStruct((CHOICE_ROWS, t), jnp.int32)
    return pl.pallas_call(
        _post_kernel,
        out_shape=(jax.ShapeDtypeStruct((t, d), F32), jax.ShapeDtypeStruct((t, d // 2), jnp.int32),
                   by_choice, jax.ShapeDtypeStruct((t, LANES), F32), by_choice,
                   jax.ShapeDtypeStruct((n_exp, LANES), F32)),
        grid=(t // tm,),
        in_specs=[pl.BlockSpec((tm, att_w), lambda i: (i, 0)),
                  pl.BlockSpec((tm, gm_w), lambda i: (i + blk0, 0)),
                  pl.BlockSpec((tm, d), lambda i: (i + blk0, 0)),
                  mrow(2), mrow(3), mrow(4), mrow(5),
                  const((1, att_w)), const(w_out.shape), const((1, d)),
                  const(wr_t.shape), const(rb.shape),
                  const(w_sg.shape), const(w_sd.shape)],
        out_specs=(pl.BlockSpec((tm, d), lambda i: (i, 0)), pl.BlockSpec((tm, d // 2), lambda i: (i, 0)),
                   pl.BlockSpec((CHOICE_ROWS, tm), lambda i: (0, i)), pl.BlockSpec((tm, LANES), lambda i: (i, 0)),
                   pl.BlockSpec((CHOICE_ROWS, tm), lambda i: (0, i)), pl.BlockSpec((n_exp, LANES), lambda i: (0, 0))),
        scratch_shapes=[pltpu.VMEM((n_exp, LANES), F32)],
        compiler_params=_cparams("arbitrary"),
        name="post",
    )(att, gmn, x2, mod4, mod4, mod4, mod4, oa, w_out, nffn, wr_t, rb, w_sg, w_sd)


def _sc_mesh_info():
    info = plsc.get_sparse_core_info()
    mesh = plsc.VectorSubcoreMesh(core_axis_name="c", subcore_axis_name="s")
    return mesh, info.num_cores, info.num_subcores, info.num_lanes


def _sc_params():
    cp = pltpu.CompilerParams()
    if "needs_layout_passes" in pltpu.CompilerParams.__dataclass_fields__:
        cp = dataclasses.replace(cp, needs_layout_passes=False)
    return cp


def _dispatch(h2p, idx_t, rank_t, offs, *, n_slots):
    t, dw = h2p.shape
    mesh, n_cores, n_sub, n_lanes = _sc_mesh_info()
    per_worker = t // (n_cores * n_sub)
    w = SC_WINDOW
    assert per_worker % w == 0 and w % n_lanes == 0

    @functools.partial(
        pl.kernel, mesh=mesh,
        out_type=(jax.ShapeDtypeStruct((n_slots, dw), jnp.int32), jax.ShapeDtypeStruct((CHOICE_ROWS, t), jnp.int32)),
        scratch_types=[pltpu.VMEM(offs.shape, jnp.int32)] + [pltpu.VMEM((TOP_K, w), jnp.int32)] * 3
        + [pltpu.VMEM((w, dw), jnp.int32), pltpu.SemaphoreType.DMA],
        compiler_params=_sc_params(),
    )
    def run(h_hbm, idx_hbm, rank_hbm, offs_hbm, xs_hbm, pos_hbm, offs_v, idx_v, rank_v, pos_v, rows_v, sem):
        base = (lax.axis_index("s") * n_cores + lax.axis_index("c")) * per_worker
        pltpu.sync_copy(offs_hbm, offs_v)

        @pl.loop(0, per_worker // w)
        def _(i):
            t0 = base + i * w
            pltpu.sync_copy(h_hbm.at[pl.ds(t0, w)], rows_v)
            for k in range(TOP_K):
                pltpu.sync_copy(idx_hbm.at[k, pl.ds(t0, w)], idx_v.at[k])
                pltpu.sync_copy(rank_hbm.at[k, pl.ds(t0, w)], rank_v.at[k])
            for k in range(TOP_K):
                for j in range(0, w, n_lanes):
                    group_start = plsc.load_gather(offs_v, [idx_v[k, pl.ds(j, n_lanes)]])
                    pos_v[k, pl.ds(j, n_lanes)] = group_start + rank_v[k, pl.ds(j, n_lanes)]
            copies = [pltpu.async_copy(rows_v, xs_hbm.at[pos_v.at[k]], sem) for k in range(TOP_K)]
            for k in range(TOP_K):
                pltpu.sync_copy(pos_v.at[k], pos_hbm.at[k, pl.ds(t0, w)])
            for cp in copies:
                cp.wait()

    return run(h2p, idx_t, rank_t, offs)


def _collect(ys, pos_t, *, tok_lo, n_tok):
    dw = ys.shape[1]
    mesh, n_cores, n_sub, _ = _sc_mesh_info()
    per_worker = n_tok // (n_cores * n_sub)
    w = SC_WINDOW
    assert per_worker % w == 0 and TOP_K % 2 == 0

    @functools.partial(
        pl.kernel, mesh=mesh,
        out_type=jax.ShapeDtypeStruct((TOP_K * n_tok, dw), jnp.int32),
        scratch_types=[pltpu.VMEM((TOP_K, w), jnp.int32), pltpu.VMEM((w, dw), jnp.int32),
                       pltpu.VMEM((w, dw), jnp.int32), pltpu.SemaphoreType.DMA, pltpu.SemaphoreType.DMA],
        compiler_params=_sc_params(),
    )
    def run(ys_hbm, pos_hbm, r_hbm, pos_v, rows_a, rows_b, sem_g, sem_w):
        base = (lax.axis_index("s") * n_cores + lax.axis_index("c")) * per_worker
        bufs = (rows_a, rows_b)

        @pl.loop(0, per_worker // w)
        def _(i):
            t0 = base + i * w
            for k in range(TOP_K):
                pltpu.sync_copy(pos_hbm.at[k, pl.ds(tok_lo + t0, w)], pos_v.at[k])
            gathers = [None] * TOP_K
            writes = [None] * TOP_K
            gathers[0] = pltpu.async_copy(ys_hbm.at[pos_v.at[0]], bufs[0], sem_g)
            for k in range(TOP_K):
                gathers[k].wait()
                if k + 1 < TOP_K:
                    if k >= 1:
                        writes[k - 1].wait()
                    gathers[k + 1] = pltpu.async_copy(ys_hbm.at[pos_v.at[k + 1]], bufs[(k + 1) % 2], sem_g)
                writes[k] = pltpu.async_copy(bufs[k % 2], r_hbm.at[pl.ds(k * n_tok + t0, w)], sem_w)
            writes[TOP_K - 2].wait()
            writes[TOP_K - 1].wait()

    return run(ys, pos_t)


def _experts_kernel(t0_ref, t1_ref, xs_hbm, wgu_ref, wdn_ref, ys_hbm, xbuf, ybuf, wgu_b, wdn_b, sem_in, sem_out):
    e = pl.program_id(0)
    n_exp = pl.num_programs(0)
    n_in, tr, half = xbuf.shape
    n_out = ybuf.shape[0]
    n_tiles = t1_ref[n_exp - 1]

    def fetch(t):
        slot = t & (n_in - 1)
        return pltpu.make_async_copy(xs_hbm.at[pl.ds(t * tr, tr)], xbuf.at[slot], sem_in.at[slot])

    def flush(t):
        slot = t & (n_out - 1)
        return pltpu.make_async_copy(ybuf.at[slot], ys_hbm.at[pl.ds(t * tr, tr)], sem_out.at[slot])

    @pl.when(e == 0)
    def _():
        for t in range(n_in - 1):
            @pl.when(t < n_tiles)
            def _():
                fetch(t).start()

    wgu_b[...] = wgu_ref[...].astype(BF16)
    wdn_b[...] = wdn_ref[...].astype(BF16)

    def tile(t, carry):
        fetch(t).wait()

        @pl.when(t + n_in - 1 < n_tiles)
        def _():
            fetch(t + n_in - 1).start()

        lo, hi = _unpack_rows(xbuf[t & (n_in - 1)])
        gu = _dot(lo.astype(BF16), wgu_b[:half, :]) + _dot(hi.astype(BF16), wgu_b[half:, :])
        hh = gu.shape[1] // 2
        act = (jax.nn.silu(gu[:, :hh]) * gu[:, hh:]).astype(BF16)
        y = _pack_rows(_dot(act, wdn_b[...]))

        @pl.when(t >= n_out)
        def _():
            flush(t - n_out).wait()

        ybuf[t & (n_out - 1)] = y
        flush(t).start()
        return carry

    lax.fori_loop(t0_ref[e], t1_ref[e], tile, 0)

    @pl.when(e == n_exp - 1)
    def _():
        for back in range(n_out, 0, -1):
            @pl.when(n_tiles >= back)
            def _():
                flush(n_tiles - back).wait()


def _experts(tile_lo, tile_hi, xs, w_gu, w_dn, after, *, tr):
    n_exp, d, h2w = w_gu.shape
    hh = w_dn.shape[1]
    in_specs = [pl.BlockSpec(memory_space=pl.ANY),
                pl.BlockSpec((None, d, h2w), lambda e, lo, hi: (e, 0, 0)),
                pl.BlockSpec((None, hh, d), lambda e, lo, hi: (e, 0, 0))]
    operands = [xs, w_gu, w_dn]
    n_read = 2 + len(operands)
    if after is not None:
        _unread_operand(in_specs, operands, after)
    n_in = 2 + len(operands)
    return pl.pallas_call(
        lambda *refs: _experts_kernel(*refs[:n_read], *refs[n_in:]),
        out_shape=jax.ShapeDtypeStruct(xs.shape, jnp.int32),
        grid_spec=pltpu.PrefetchScalarGridSpec(
            num_scalar_prefetch=2,
            grid=(n_exp,),
            in_specs=in_specs,
            out_specs=pl.BlockSpec(memory_space=pl.ANY),
            scratch_shapes=[pltpu.VMEM((EXPERT_IN_SLOTS, tr, d // 2), jnp.int32),
                            pltpu.VMEM((EXPERT_OUT_SLOTS, tr, d // 2), jnp.int32),
                            pltpu.VMEM((d, h2w), BF16), pltpu.VMEM((hh, d), BF16),
                            pltpu.SemaphoreType.DMA((EXPERT_IN_SLOTS,)), pltpu.SemaphoreType.DMA((EXPERT_OUT_SLOTS,))]),
        compiler_params=_cparams("arbitrary"),
        name="experts",
    )(tile_lo, tile_hi, *operands)


def _combine_kernel(base_ref, g2_ref, gate_ref, *refs):
    r_refs, o_ref = refs[:TOP_K], refs[TOP_K]
    gate = gate_ref[...]
    half = base_ref.shape[1] // 2
    acc_lo = jnp.zeros((base_ref.shape[0], half), F32)
    acc_hi = jnp.zeros((base_ref.shape[0], half), F32)
    for k in range(TOP_K):
        lo, hi = _unpack_rows(r_refs[k][...])
        g = gate[:, k:k + 1]
        acc_lo = acc_lo + g * lo
        acc_hi = acc_hi + g * hi
    g2 = g2_ref[...]
    o_ref[:, :half] = base_ref[:, :half] + g2[:, :half] * acc_lo
    o_ref[:, half:] = base_ref[:, half:] + g2[:, half:] * acc_hi


def _combine(base, mod4, gate, r, out_so_far, after, *, tm, tokens_per_batch, tok0, part_lo, total_tokens):
    d = base.shape[1]
    t = r.shape[0] // TOP_K
    tpb = tokens_per_batch // tm
    nblk = t // tm
    pblk0 = part_lo // tm
    blk0 = (tok0 + part_lo) // tm
    r_spec = lambda k: pl.BlockSpec((tm, d // 2), lambda i: (k * nblk + i, 0))
    in_specs = [pl.BlockSpec((tm, d), lambda i: (i + pblk0, 0)),
                pl.BlockSpec((None, None, 1, d), lambda i: ((i + blk0) // tpb, 5, 0, 0)),
                pl.BlockSpec((tm, LANES), lambda i: (i + pblk0, 0))] + [r_spec(k) for k in range(TOP_K)]
    operands = [base, mod4, gate] + [r] * TOP_K
    n_read = len(operands)
    aliases = {}
    if out_so_far is not None:
        aliases = {_unread_operand(in_specs, operands, out_so_far): 0}
    if after is not None:
        _unread_operand(in_specs, operands, after)
    n_in = len(operands)
    return pl.pallas_call(
        lambda *refs: _combine_kernel(*refs[:n_read], *refs[n_in:]),
        out_shape=jax.ShapeDtypeStruct((total_tokens, d), F32),
        grid=(nblk,),
        in_specs=in_specs,
        out_specs=pl.BlockSpec((tm, d), lambda i: (i + blk0, 0)),
        input_output_aliases=aliases,
        compiler_params=_cparams("arbitrary"),
        name="combine",
    )(*operands)


def _rope_tables(length, head_dim):
    rows = length // GRID_W
    pairs_axis = head_dim // 4
    r = jnp.repeat(jnp.arange(rows, dtype=F32), GRID_W)
    col = jnp.tile(jnp.arange(GRID_W, dtype=F32), rows)
    inv = ROPE_THETA ** (-jnp.arange(pairs_axis, dtype=F32) / pairs_axis)
    ang = jnp.concatenate([r[:, None] * inv, col[:, None] * inv], axis=-1)
    cos, sin = jnp.cos(ang), jnp.sin(ang)
    cos_h = jnp.concatenate([cos, cos], axis=-1)
    sin_h = jnp.concatenate([-sin, sin], axis=-1)
    reps = LANES // head_dim
    return jnp.tile(cos_h, (1, reps)), jnp.tile(sin_h, (1, reps))


def _part_batches(b):
    if b < 2:
        return [b]
    lead = max(1, min(b - 1, round(LEAD_PART_SHARE * b)))
    return [lead, b - lead]


def kernel(x, c, ctx, c_ctx, w_ada, b_ada, norm_mix, w_in, q_norm, k_norm, gm_norm, w_spatial, b_spatial,
           out_norm_attn, out_norm_gm, w_out, norm_ffn, w_router, router_bias, w_exp_gu, w_exp_down,
           w_sh_gu, w_sh_down):
    assert w_ada.shape[0] == 1, "single-layer kernel"
    b, length, d = x.shape
    head_dim = q_norm.shape[-1]
    att_w = out_norm_attn.shape[-1]
    gm_w = out_norm_gm.shape[-1]
    gm_groups = w_spatial.shape[1]
    gm_hd = gm_w // gm_groups
    n_exp = w_router.shape[-1]
    t = b * length
    assert head_dim == LANES // 2 and gm_hd == LANES // 2 and w_spatial.shape[-1] == CHUNK
    assert n_exp <= LANES and att_w % MXU_EDGE == 0 and gm_w % MXU_EDGE == 0

    mod_rows = 16
    c_all = jnp.zeros((mod_rows, d), F32).at[:b].set(c).at[b].set(c_ctx)
    mod = _ada(c_all, w_ada[0], b_ada)
    mod4 = mod.reshape(mod_rows, N_MOD, 1, d)

    w_in_b = w_in[0].astype(BF16)
    bd = (jnp.arange(MXU_EDGE)[:, None] // head_dim == jnp.arange(MXU_EDGE)[None, :] // head_dim)
    bd = (bd.astype(F32) / head_dim).astype(BF16)
    qg = jnp.tile(q_norm[0] * (head_dim ** -0.5 * LOG2_E), att_w // head_dim)[None, :]
    kg = jnp.tile(k_norm[0], LANES // head_dim)[None, :]
    gmg = jnp.tile(gm_norm[0], gm_groups)[None, :]
    ws = w_spatial[0].astype(BF16).reshape(gm_groups // 2, 2 * CHUNK, CHUNK)
    bs = jnp.repeat(b_spatial[0].T, gm_hd, axis=1)
    cos_t, sin_t = _rope_tables(length, head_dim)

    k_c, v_c = _ctx_kv(ctx, mod4, b, norm_mix, w_in_b[:, att_w:att_w + 2 * LANES], kg, bd)
    q, k_l, v_l, gmn = _inproj(x, mod4, norm_mix, w_in_b, cos_t, sin_t, qg, kg, gmg, bd, ws, bs,
                               out_norm_gm, att_w=att_w, gm_w=gm_w, tm=min(1024, length))
    wr_t = w_router[0].T.astype(BF16)
    rb = jnp.tile(router_bias[0][:, None], (1, LANES))
    w_out_b, w_sg_b, w_sd_b = w_out[0].astype(BF16), w_sh_gu[0].astype(BF16), w_sh_down[0].astype(BF16)
    gmn2, x2 = gmn.reshape(t, gm_w), x.reshape(t, d)
    tm_post = min(1024, length)
    tr = 512

    part_nb = _part_batches(b)
    n_parts = len(part_nb)
    part_b0 = [sum(part_nb[:p]) for p in range(n_parts)]
    state = [dict() for _ in range(n_parts)]

    last_tc = [None]

    def halves(part):
        nb = part_nb[part]
        return (nb // 2, nb - nb // 2) if nb >= 2 else (nb,)

    def attend(part, half):
        if half >= len(halves(part)):
            return
        st = state[part]
        slot0 = sum(halves(part)[:half])
        after = None if last_tc[0] is st.get("att") else last_tc[0]
        st["att"] = _attention(q, k_c, k_l, v_c, v_l, st.get("att"), after, tq=min(256, length),
                               b0=part_b0[part] + slot0, nb=halves(part)[half], slot0=slot0, slots=part_nb[part])
        last_tc[0] = st["att"]

    def route(part):
        st = state[part]
        tp = part_nb[part] * length
        nt = tp * TOP_K // tr + n_exp
        base, h2p, idx_t, gate, rank_t, cnt = _post(
            st["att"].reshape(tp, att_w), gmn2, x2, mod4, out_norm_attn, w_out_b, norm_ffn, wr_t, rb, w_sg_b, w_sd_b,
            tm=tm_post, tokens_per_batch=length, tok0=part_b0[part] * length)
        counts = cnt[:, 0].astype(jnp.int32)
        padded = (counts + tr - 1) // tr * tr
        ends = jnp.cumsum(padded)
        offs = ends - padded
        offs_pad = jnp.zeros((LANES,), jnp.int32).at[:n_exp].set(offs)
        xs, pos_t = _dispatch(h2p, idx_t, rank_t, offs_pad, n_slots=nt * tr)
        st.update(base=base, gate=gate, xs=xs, pos_t=pos_t, tile_lo=offs // tr, tile_hi=ends // tr)
        last_tc[0] = base

    def run_experts(part):
        st = state[part]
        ys = _experts(st["tile_lo"], st["tile_hi"], st["xs"], w_exp_gu[0], w_exp_down[0], last_tc[0], tr=tr)
        n_chunks = part_nb[part] if part == n_parts - 1 else 1
        chunk = part_nb[part] * length // n_chunks
        st["r"] = [(c * chunk, _collect(ys, st["pos_t"], tok_lo=c * chunk, n_tok=chunk)) for c in range(n_chunks)]
        last_tc[0] = ys

    result = [None]

    def finish(part):
        st = state[part]
        for part_lo, r in st["r"]:
            after = None if last_tc[0] is result[0] or last_tc[0] is st["base"] else last_tc[0]
            result[0] = _combine(st["base"], mod4, st["gate"], r, result[0], after, tm=min(512, length),
                                 tokens_per_batch=length, tok0=part_b0[part] * length, part_lo=part_lo,
                                 total_tokens=t)
            last_tc[0] = result[0]

    stages = [(0, lambda p: attend(p, 0)), (1, lambda p: attend(p, 1)), (2, route), (4, run_experts), (7, finish)]
    plan = sorted((when + 3 * part, part, n, fn) for part in range(n_parts) for n, (when, fn) in enumerate(stages))
    for _, part, _, fn in plan:
        fn(part)
    return result[0].reshape(b, length, d)
```

```python
import dataclasses
import functools

import jax
import jax.numpy as jnp
from jax import lax
from jax.experimental import pallas as pl
from jax.experimental.pallas import tpu as pltpu
from jax.experimental.pallas import tpu_sc as plsc

F32 = jnp.float32
BF16 = jnp.bfloat16

EPS = 1e-6
GRID_W = 64
ROPE_THETA = 10000.0
ATT_KV_HEADS = 2
TOP_K = 6
ROUTED_SCALE = 2.5
N_MOD = 6
CHUNK = 128
LOG2_E = 1.4426950408889634

LANES = 128
MXU_EDGE = 256
VMEM_LIMIT_BYTES = 56 * 1024 * 1024
SUBLANES = 8
CHOICE_ROWS = SUBLANES
SC_WINDOW = 64
LEAD_PART_SHARE = 0.75
ROW_BLOCK = 512
EXPERT_IN_SLOTS = 4
EXPERT_OUT_SLOTS = 2


def _cparams(*sem):
    return pltpu.CompilerParams(dimension_semantics=sem, vmem_limit_bytes=VMEM_LIMIT_BYTES)


def _dot(a, b):
    return jnp.dot(a, b, preferred_element_type=F32)


def _dot_nt(a, b):
    return lax.dot_general(a, b, (((1,), (1,)), ((), ())), preferred_element_type=F32)


def _rms(x, g):
    return x * lax.rsqrt(jnp.mean(x * x, axis=-1, keepdims=True) + EPS) * g


def _group_mean_sq(x, bd_ref):
    x2 = (x * x).astype(BF16)
    parts = [_dot(x2[:, i:i + MXU_EDGE], bd_ref[...]) for i in range(0, x.shape[1], MXU_EDGE)]
    return parts[0] if len(parts) == 1 else jnp.concatenate(parts, axis=-1)


def _dup_halves(x, lane):
    xr = pltpu.roll(x, LANES // 2, axis=1)
    lo = lane < LANES // 2
    return jnp.where(lo, x, xr), jnp.where(lo, xr, x)


def _with_ones(x, lane):
    xr = pltpu.roll(x, LANES // 2, axis=1)
    lo = lane < LANES // 2
    return jnp.concatenate([jnp.where(lo, x, 1.0), jnp.where(lo, xr, 1.0)], axis=-1)


def _pipelined_blocks(tm, sub, head, tail):
    blocks = [pl.ds(r0, sub) for r0 in range(0, tm, sub)]
    pending = head(blocks[0])
    for n, rows in enumerate(blocks):
        ahead = head(blocks[n + 1]) if n + 1 < len(blocks) else None
        tail(rows, pending)
        pending = ahead


_HI16 = 0xFFFF0000


def _pack_rows(x):
    half = x.shape[1] // 2
    rounded = x.astype(BF16).astype(F32)
    bits = lax.bitcast_convert_type(rounded, jnp.uint32)
    word = (bits[:, :half] >> 16) | (bits[:, half:] & jnp.uint32(_HI16))
    return lax.bitcast_convert_type(word, jnp.int32)


def _unpack_rows(w):
    bits = lax.bitcast_convert_type(w, jnp.uint32)
    lo = lax.bitcast_convert_type(bits << 16, F32)
    hi = lax.bitcast_convert_type(bits & jnp.uint32(_HI16), F32)
    return lo, hi


def _ada_kernel(c_ref, w_ref, b_ref, o_ref):
    s = jax.nn.silu(c_ref[...])
    o_ref[...] = jnp.dot(s, w_ref[...], precision=lax.Precision.HIGHEST,
                         preferred_element_type=F32) + b_ref[...]


def _ada(c_all, w_ada, b_ada):
    rows, d = c_all.shape
    n = w_ada.shape[1]
    bn = 2048 if n % 2048 == 0 else 512
    return pl.pallas_call(
        _ada_kernel,
        out_shape=jax.ShapeDtypeStruct((rows, n), F32),
        grid=(n // bn,),
        in_specs=[pl.BlockSpec((rows, d), lambda j: (0, 0)),
                  pl.BlockSpec((d, bn), lambda j: (0, j)),
                  pl.BlockSpec((1, bn), lambda j: (0, j))],
        out_specs=pl.BlockSpec((rows, bn), lambda j: (0, j)),
        compiler_params=_cparams("arbitrary"),
        name="ada",
    )(c_all, w_ada, b_ada)


def _ctx_kv_kernel(x_ref, sh_ref, sc_ref, nmix_ref, w_ref, kg_ref, bd_ref, k_ref, v_ref):
    h = _rms(x_ref[...], nmix_ref[...]) * (1.0 + sc_ref[...]) + sh_ref[...]
    p = _dot(h.astype(BF16), w_ref[...])
    ms = _group_mean_sq(p, bd_ref)
    k = p[:, :LANES] * lax.rsqrt(ms[:, :LANES] + EPS) * kg_ref[...]
    v = p[:, LANES:]
    lane = lax.broadcasted_iota(jnp.int32, k.shape, 1)
    ka, kb = _dup_halves(k, lane)
    k_ref[...] = jnp.concatenate([ka, kb], axis=-1).astype(BF16)
    v_ref[...] = _with_ones(v, lane).astype(BF16)


def _ctx_kv(ctx, mod4, ctx_row, nmix, w_kv, kg, bd):
    b, n_ctx, d = ctx.shape
    kv_w = 2 * LANES
    return pl.pallas_call(
        _ctx_kv_kernel,
        out_shape=(jax.ShapeDtypeStruct((b, n_ctx, kv_w), BF16),) * 2,
        grid=(b,),
        in_specs=[pl.BlockSpec((None, n_ctx, d), lambda i: (i, 0, 0)),
                  pl.BlockSpec((None, None, 1, d), lambda i: (ctx_row, 0, 0, 0)),
                  pl.BlockSpec((None, None, 1, d), lambda i: (ctx_row, 1, 0, 0)),
                  pl.BlockSpec((1, d), lambda i: (0, 0)),
                  pl.BlockSpec(w_kv.shape, lambda i: (0, 0)),
                  pl.BlockSpec((1, LANES), lambda i: (0, 0)),
                  pl.BlockSpec(bd.shape, lambda i: (0, 0))],
        out_specs=(pl.BlockSpec((None, n_ctx, kv_w), lambda i: (i, 0, 0)),) * 2,
        compiler_params=_cparams("arbitrary"),
        name="ctx_kv",
    )(ctx, mod4, mod4, nmix, w_kv, kg, bd)


def _inproj_kernel(x_ref, sh_ref, sc_ref, nmix_ref, w_ref, cos_ref, sin_ref, qg_ref, kg_ref, gmg_ref,
                   bd_ref, ws_ref, bs_ref, og_ref, q_ref, k_ref, v_ref, gm_ref, *, att_w, gm_w):
    tm = x_ref.shape[0]
    sub = min(tm, ROW_BLOCK)

    def project(rows):
        h = _rms(x_ref[rows, :], nmix_ref[...]) * (1.0 + sc_ref[...]) + sh_ref[...]
        return _dot(h.astype(BF16), w_ref[...])

    def rope(xn, cosw, sinw, lanew):
        w = xn.shape[1]
        fwd = pltpu.roll(xn, w - 32, axis=1)
        bwd = pltpu.roll(xn, 32, axis=1)
        swapped = jnp.where((lanew & 63) < 32, fwd, bwd)
        return xn * cosw + swapped * sinw

    def finish(rows, p):
        cos1, sin1 = cos_ref[rows, :], sin_ref[rows, :]
        lane1 = lax.broadcasted_iota(jnp.int32, (sub, LANES), 1)

        q = p[:, :att_w]
        qn = q * lax.rsqrt(_group_mean_sq(q, bd_ref) + EPS) * qg_ref[...]
        reps = att_w // LANES
        cosq = jnp.concatenate([cos1] * reps, axis=-1)
        sinq = jnp.concatenate([sin1] * reps, axis=-1)
        laneq = lax.broadcasted_iota(jnp.int32, (sub, att_w), 1)
        q_ref[rows, :] = rope(qn, cosq, sinq, laneq).astype(BF16)

        kv = p[:, att_w:att_w + 2 * LANES]
        ms = _group_mean_sq(kv, bd_ref)
        kn = kv[:, :LANES] * lax.rsqrt(ms[:, :LANES] + EPS) * kg_ref[...]
        k = rope(kn, cos1, sin1, lane1)
        v = kv[:, LANES:]
        ka, kb = _dup_halves(k, lane1)
        k_ref[rows, :] = jnp.concatenate([ka, kb], axis=-1).astype(BF16)
        v_ref[rows, :] = _with_ones(v, lane1).astype(BF16)

        u0 = att_w + 2 * LANES
        u = jax.nn.gelu(p[:, u0:u0 + gm_w])
        gt = jax.nn.gelu(p[:, u0 + gm_w:u0 + 2 * gm_w])
        gt = (gt * lax.rsqrt(_group_mean_sq(gt, bd_ref) + EPS) * gmg_ref[...]).astype(BF16)
        lo_half = lax.broadcasted_iota(jnp.int32, (CHUNK, LANES), 1) < LANES // 2
        mixed_rows = []
        for c in range(sub // CHUNK):
            blocks = []
            for j in range(gm_w // LANES):
                g = gt[c * CHUNK:(c + 1) * CHUNK, j * LANES:(j + 1) * LANES]
                r = _dot(ws_ref[j], g)
                blocks.append(jnp.where(lo_half, r[:CHUNK], r[CHUNK:]))
            mixed_rows.append(jnp.concatenate(blocks, axis=-1) + bs_ref[...])
        mixed = jnp.concatenate(mixed_rows, axis=0)
        gm_ref[rows, :] = _rms(u * mixed, og_ref[...]).astype(BF16)

    _pipelined_blocks(tm, sub, project, finish)


def _inproj(x, mod4, nmix, w_in, cos_t, sin_t, qg, kg, gmg, bd, ws, bs, og, *, att_w, gm_w, tm):
    b, length, d = x.shape
    kv_w = 2 * LANES
    n_in = w_in.shape[1]
    const = lambda shape: pl.BlockSpec(shape, lambda i, j: (0,) * len(shape))
    return pl.pallas_call(
        functools.partial(_inproj_kernel, att_w=att_w, gm_w=gm_w),
        out_shape=(jax.ShapeDtypeStruct((b, length, att_w), BF16),
                   jax.ShapeDtypeStruct((b, length, kv_w), BF16),
                   jax.ShapeDtypeStruct((b, length, kv_w), BF16),
                   jax.ShapeDtypeStruct((b, length, gm_w), BF16)),
        grid=(b, length // tm),
        in_specs=[pl.BlockSpec((None, tm, d), lambda i, j: (i, j, 0)),
                  pl.BlockSpec((None, None, 1, d), lambda i, j: (i, 0, 0, 0)),
                  pl.BlockSpec((None, None, 1, d), lambda i, j: (i, 1, 0, 0)),
                  const((1, d)),
                  const((d, n_in)),
                  pl.BlockSpec((tm, LANES), lambda i, j: (j, 0)),
                  pl.BlockSpec((tm, LANES), lambda i, j: (j, 0)),
                  const((1, att_w)), const((1, LANES)), const((1, gm_w)),
                  const(bd.shape), const(ws.shape), const(bs.shape), const((1, gm_w))],
        out_specs=(pl.BlockSpec((None, tm, att_w), lambda i, j: (i, j, 0)),
                   pl.BlockSpec((None, tm, kv_w), lambda i, j: (i, j, 0)),
                   pl.BlockSpec((None, tm, kv_w), lambda i, j: (i, j, 0)),
                   pl.BlockSpec((None, tm, gm_w), lambda i, j: (i, j, 0))),
        compiler_params=_cparams("arbitrary", "arbitrary"),
        name="inproj",
    )(x, mod4, mod4, nmix, w_in, cos_t, sin_t, qg, kg, gmg, bd, ws, bs, og)


def _attn_kernel(q_ref, kc_ref, kl_ref, vc_ref, vl_ref, o_ref, s_buf, p_buf, *, tq):
    n_ctx = kc_ref.shape[0]
    heads = q_ref.shape[1] // (LANES // 2)
    nj = q_ref.shape[0] // tq
    half = LANES // 2
    lo = lax.broadcasted_iota(jnp.int32, (tq, LANES), 1) < half

    @pl.when(jnp.logical_and(pl.program_id(0) == 0, pl.program_id(1) == 0))
    def _():
        s_buf[...] = jnp.zeros_like(s_buf)
        p_buf[...] = jnp.ones_like(p_buf)

    def scores(j, g):
        q = q_ref[pl.ds(j * tq, tq), (g // 2) * LANES:(g // 2 + 1) * LANES]
        zero = jnp.zeros_like(q)
        qh = jnp.where(lo, q, zero) if g % 2 == 0 else jnp.where(lo, zero, q)
        s_buf[g % 2, :, :n_ctx] = _dot_nt(qh, kc_ref[...])
        s_buf[g % 2, :, n_ctx:] = _dot_nt(qh, kl_ref[...])

    def probs(g):
        s = s_buf[g % 2]
        p_buf[g % 2] = jnp.exp2(s - jnp.max(s, axis=-1, keepdims=True)).astype(BF16)

    def output(j, g):
        p = p_buf[g % 2]
        w = _dot(p[:, :n_ctx], vc_ref[...]) + _dot(p[:, n_ctx:], vl_ref[...])
        t = jnp.where(pl.program_id(1) == 0, w[:, :LANES], w[:, LANES:])
        r = pltpu.roll(t, half, axis=1)
        rows = pl.ds(j * tq, tq)
        if g % 2 == 0:
            o_ref[rows, g * half:(g + 1) * half] = (t / r)[:, :half].astype(o_ref.dtype)
        else:
            o_ref[rows, g * half:(g + 1) * half] = (r / t)[:, half:].astype(o_ref.dtype)

    def tile_stages(j):
        j_prev = jnp.maximum(j - 1, 0)
        for g in range(heads):
            scores(j, g)
            probs((g - 1) % heads)
            output(j if g >= 2 else j_prev, (g - 2) % heads)

    def trip(jj, carry):
        for u in range(tiles_per_trip):
            tile_stages(jj * tiles_per_trip + u)
        return carry

    tiles_per_trip = 2 if nj % 2 == 0 else 1
    lax.fori_loop(0, nj // tiles_per_trip, trip, 0)
    probs(heads - 1)
    output(nj - 1, heads - 2)
    output(nj - 1, heads - 1)


def _unread_operand(in_specs, operands, array):
    in_specs.append(pl.BlockSpec(memory_space=pl.ANY))
    operands.append(array)
    return len(operands) - 1


def _attention(q, k_c, k_l, v_c, v_l, att_so_far, after, *, tq, b0, nb, slot0, slots):
    _, length, att_w = q.shape
    n_ctx = k_c.shape[1]
    group_w = att_w // ATT_KV_HEADS
    assert (group_w // (LANES // 2)) % 2 == 0 and length % tq == 0
    k_blk = lambda n: pl.BlockSpec((None, n, LANES), lambda i, h: (i + b0, 0, h))
    v_blk = lambda n: pl.BlockSpec((None, n, ATT_KV_HEADS * LANES), lambda i, h: (i + b0, 0, 0))
    in_specs = [pl.BlockSpec((None, length, group_w), lambda i, h: (i + b0, 0, h)),
                k_blk(n_ctx), k_blk(length), v_blk(n_ctx), v_blk(length)]
    operands = [q, k_c, k_l, v_c, v_l]
    n_read = len(operands)
    aliases = {}
    if att_so_far is not None:
        aliases = {_unread_operand(in_specs, operands, att_so_far): 0}
    if after is not None:
        _unread_operand(in_specs, operands, after)
    n_in = len(operands)
    return pl.pallas_call(
        lambda *refs: _attn_kernel(*refs[:n_read], *refs[n_in:], tq=tq),
        out_shape=jax.ShapeDtypeStruct((slots, length, att_w), BF16),
        grid=(nb, ATT_KV_HEADS),
        in_specs=in_specs,
        out_specs=pl.BlockSpec((None, length, group_w), lambda i, h: (i + slot0, 0, h)),
        scratch_shapes=[pltpu.VMEM((2, tq, n_ctx + length), F32), pltpu.VMEM((2, tq, n_ctx + length), BF16)],
        input_output_aliases=aliases,
        compiler_params=_cparams("arbitrary", "arbitrary"),
        name="attn",
    )(*operands)


def _post_kernel(att_ref, gm_ref, x_ref, g1_ref, sh2_ref, sc2_ref, g2_ref, oa_ref, wo_ref, nffn_ref,
                 wrt_ref, rb_ref, wsg_ref, wsd_ref,
                 base_ref, h2p_ref, idx_ref, gate_ref, rank_ref, cnt_ref, carry_ref):
    i = pl.program_id(0)
    tm = x_ref.shape[0]
    n_exp = wrt_ref.shape[0]

    @pl.when(i == 0)
    def _():
        carry_ref[...] = jnp.zeros_like(carry_ref)

    sub = min(tm, ROW_BLOCK)
    reps = sub // LANES
    neg = jnp.float32(-jnp.inf)
    r_io = lax.broadcasted_iota(jnp.int32, (sub, sub), 0)
    c_io = lax.broadcasted_iota(jnp.int32, (sub, sub), 1)
    tri = jnp.where(r_io < c_io, 1.0, 0.0).astype(BF16)

    def residual(rows):
        att_n = _rms(att_ref[rows, :].astype(F32), oa_ref[...]).astype(BF16)
        y = _dot(jnp.concatenate([att_n, gm_ref[rows, :]], axis=-1), wo_ref[...])
        x_new = x_ref[rows, :] + g1_ref[...] * y
        h2 = _rms(x_new, nffn_ref[...]) * (1.0 + sc2_ref[...]) + sh2_ref[...]
        h2p_ref[rows, :] = _pack_rows(h2)
        return x_new, h2

    def route_and_share(rows, carried):
        x_new, h2 = carried
        h_hi = h2.astype(BF16)
        scores_t = jax.nn.sigmoid(_dot_nt(wrt_ref[...], h_hi))
        sel = scores_t + jnp.concatenate([rb_ref[...]] * reps, axis=-1)
        expert = lax.broadcasted_iota(jnp.int32, scores_t.shape, 0).astype(F32)

        onehot = jnp.zeros(scores_t.shape, F32)
        hits, idxs, svals = [], [], []
        for _k in range(TOP_K):
            m = jnp.max(sel, axis=0, keepdims=True)
            idx = jnp.min(jnp.where(sel == m, expert, float(n_exp)), axis=0, keepdims=True)
            hit = expert == idx
            hits.append(hit)
            idxs.append(idx)
            svals.append(jnp.sum(jnp.where(hit, scores_t, 0.0), axis=0, keepdims=True))
            onehot = onehot + jnp.where(hit, 1.0, 0.0)
            sel = jnp.where(hit, neg, sel)
        ssum = functools.reduce(lambda a, b: a + b, svals)

        carry = carry_ref[...]
        before = _dot(onehot.astype(BF16), tri) + jnp.concatenate([carry] * reps, axis=-1)
        carry = carry + jnp.sum(onehot, axis=1, keepdims=True)
        carry_ref[...] = carry
        cnt_ref[...] = carry

        row = lax.broadcasted_iota(jnp.int32, (idx_ref.shape[0], sub), 0)
        idx_o = jnp.zeros(row.shape, F32)
        rank_o = jnp.zeros(row.shape, F32)
        gate_o = jnp.zeros(row.shape, F32)
        for k in range(TOP_K):
            rank = jnp.sum(jnp.where(hits[k], before, 0.0), axis=0, keepdims=True)
            idx_o = jnp.where(row == k, idxs[k], idx_o)
            rank_o = jnp.where(row == k, rank, rank_o)
            gate_o = jnp.where(row == k, svals[k] / ssum * ROUTED_SCALE, gate_o)
        idx_ref[:, rows] = idx_o.astype(jnp.int32)
        rank_ref[:, rows] = rank_o.astype(jnp.int32)
        gate_pad = jnp.concatenate([gate_o, jnp.zeros((LANES - gate_o.shape[0], sub), F32)], axis=0)
        gate_ref[rows, :] = gate_pad.T

        gu = _dot(h_hi, wsg_ref[...])
        hs = gu.shape[1] // 2
        act = (jax.nn.silu(gu[:, :hs]) * gu[:, hs:]).astype(BF16)
        base_ref[rows, :] = x_new + g2_ref[...] * _dot(act, wsd_ref[...])

    _pipelined_blocks(tm, sub, residual, route_and_share)


def _post(att, gmn, x2, mod4, oa, w_out, nffn, wr_t, rb, w_sg, w_sd, *, tm, tokens_per_batch, tok0):
    t, att_w = att.shape
    d = x2.shape[1]
    n_exp = wr_t.shape[0]
    gm_w = gmn.shape[1]
    tpb = tokens_per_batch // tm
    blk0 = tok0 // tm
    const = lambda shape: pl.BlockSpec(shape, lambda i: (0,) * len(shape))
    mrow = lambda j: pl.BlockSpec((None, None, 1, d), lambda i: ((i + blk0) // tpb, j, 0, 0))
    by_choice = jax.ShapeDtypeStruct((CHOICE_ROWS, t), jnp.int32)
    return pl.pallas_call(
        _post_kernel,
        out_shape=(jax.ShapeDtypeStruct((t, d), F32), jax.ShapeDtypeStruct((t, d // 2), jnp.int32),
                   by_choice, jax.ShapeDtypeStruct((t, LANES), F32), by_choice,
                   jax.ShapeDtypeStruct((n_exp, LANES), F32)),
        grid=(t // tm,),
        in_specs=[pl.BlockSpec((tm, att_w), lambda i: (i, 0)),
                  pl.BlockSpec((tm, gm_w), lambda i: (i + blk0, 0)),
                  pl.BlockSpec((tm, d), lambda i: (i + blk0, 0)),
                  mrow(2), mrow(3), mrow(4), mrow(5),
                  const((1, att_w)), const(w_out.shape), const((1, d)),
                  const(wr_t.shape), const(rb.shape),
                  const(w_sg.shape), const(w_sd.shape)],
        out_specs=(pl.BlockSpec((tm, d), lambda i: (i, 0)), pl.BlockSpec((tm, d // 2), lambda i: (i, 0)),
                   pl.BlockSpec((CHOICE_ROWS, tm), lambda i: (0, i)), pl.BlockSpec((tm, LANES), lambda i: (i, 0)),
                   pl.BlockSpec((CHOICE_ROWS, tm), lambda i: (0, i)), pl.BlockSpec((n_exp, LANES), lambda i: (0, 0))),
        scratch_shapes=[pltpu.VMEM((n_exp, LANES), F32)],
        compiler_params=_cparams("arbitrary"),
        name="post",
    )(att, gmn, x2, mod4, mod4, mod4, mod4, oa, w_out, nffn, wr_t, rb, w_sg, w_sd)


def _sc_mesh_info():
    info = plsc.get_sparse_core_info()
    mesh = plsc.VectorSubcoreMesh(core_axis_name="c", subcore_axis_name="s")
    return mesh, info.num_cores, info.num_subcores, info.num_lanes


def _sc_params():
    cp = pltpu.CompilerParams()
    if "needs_layout_passes" in pltpu.CompilerParams.__dataclass_fields__:
        cp = dataclasses.replace(cp, needs_layout_passes=False)
    return cp


def _dispatch(h2p, idx_t, rank_t, offs, *, n_slots):
    t, dw = h2p.shape
    mesh, n_cores, n_sub, n_lanes = _sc_mesh_info()
    per_worker = t // (n_cores * n_sub)
    w = SC_WINDOW
    assert per_worker % w == 0 and w % n_lanes == 0

    @functools.partial(
        pl.kernel, mesh=mesh,
        out_type=(jax.ShapeDtypeStruct((n_slots, dw), jnp.int32), jax.ShapeDtypeStruct((CHOICE_ROWS, t), jnp.int32)),
        scratch_types=[pltpu.VMEM(offs.shape, jnp.int32)] + [pltpu.VMEM((TOP_K, w), jnp.int32)] * 3
        + [pltpu.VMEM((w, dw), jnp.int32), pltpu.SemaphoreType.DMA],
        compiler_params=_sc_params(),
    )
    def run(h_hbm, idx_hbm, rank_hbm, offs_hbm, xs_hbm, pos_hbm, offs_v, idx_v, rank_v, pos_v, rows_v, sem):
        base = (lax.axis_index("s") * n_cores + lax.axis_index("c")) * per_worker
        pltpu.sync_copy(offs_hbm, offs_v)

        @pl.loop(0, per_worker // w)
        def _(i):
            t0 = base + i * w
            pltpu.sync_copy(h_hbm.at[pl.ds(t0, w)], rows_v)
            for k in range(TOP_K):
                pltpu.sync_copy(idx_hbm.at[k, pl.ds(t0, w)], idx_v.at[k])
                pltpu.sync_copy(rank_hbm.at[k, pl.ds(t0, w)], rank_v.at[k])
            for k in range(TOP_K):
                for j in range(0, w, n_lanes):
                    group_start = plsc.load_gather(offs_v, [idx_v[k, pl.ds(j, n_lanes)]])
                    pos_v[k, pl.ds(j, n_lanes)] = group_start + rank_v[k, pl.ds(j, n_lanes)]
            copies = [pltpu.async_copy(rows_v, xs_hbm.at[pos_v.at[k]], sem) for k in range(TOP_K)]
            for k in range(TOP_K):
                pltpu.sync_copy(pos_v.at[k], pos_hbm.at[k, pl.ds(t0, w)])
            for cp in copies:
                cp.wait()

    return run(h2p, idx_t, rank_t, offs)


def _collect(ys, pos_t, *, tok_lo, n_tok):
    dw = ys.shape[1]
    mesh, n_cores, n_sub, _ = _sc_mesh_info()
    per_worker = n_tok // (n_cores * n_sub)
    w = SC_WINDOW
    assert per_worker % w == 0 and TOP_K % 2 == 0

    @functools.partial(
        pl.kernel, mesh=mesh,
        out_type=jax.ShapeDtypeStruct((TOP_K * n_tok, dw), jnp.int32),
        scratch_types=[pltpu.VMEM((TOP_K, w), jnp.int32), pltpu.VMEM((w, dw), jnp.int32),
                       pltpu.VMEM((w, dw), jnp.int32), pltpu.SemaphoreType.DMA, pltpu.SemaphoreType.DMA],
        compiler_params=_sc_params(),
    )
    def run(ys_hbm, pos_hbm, r_hbm, pos_v, rows_a, rows_b, sem_g, sem_w):
        base = (lax.axis_index("s") * n_cores + lax.axis_index("c")) * per_worker
        bufs = (rows_a, rows_b)

        @pl.loop(0, per_worker // w)
        def _(i):
            t0 = base + i * w
            for k in range(TOP_K):
                pltpu.sync_copy(pos_hbm.at[k, pl.ds(tok_lo + t0, w)], pos_v.at[k])
            gathers = [None] * TOP_K
            writes = [None] * TOP_K
            gathers[0] = pltpu.async_copy(ys_hbm.at[pos_v.at[0]], bufs[0], sem_g)
            for k in range(TOP_K):
                gathers[k].wait()
                if k + 1 < TOP_K:
                    if k >= 1:
                        writes[k - 1].wait()
                    gathers[k + 1] = pltpu.async_copy(ys_hbm.at[pos_v.at[k + 1]], bufs[(k + 1) % 2], sem_g)
                writes[k] = pltpu.async_copy(bufs[k % 2], r_hbm.at[pl.ds(k * n_tok + t0, w)], sem_w)
            writes[TOP_K - 2].wait()
            writes[TOP_K - 1].wait()

    return run(ys, pos_t)


def _experts_kernel(t0_ref, t1_ref, xs_hbm, wgu_ref, wdn_ref, ys_hbm, xbuf, ybuf, wgu_b, wdn_b, sem_in, sem_out):
    e = pl.program_id(0)
    n_exp = pl.num_programs(0)
    n_in, tr, half = xbuf.shape
    n_out = ybuf.shape[0]
    n_tiles = t1_ref[n_exp - 1]

    def fetch(t):
        slot = t & (n_in - 1)
        return pltpu.make_async_copy(xs_hbm.at[pl.ds(t * tr, tr)], xbuf.at[slot], sem_in.at[slot])

    def flush(t):
        slot = t & (n_out - 1)
        return pltpu.make_async_copy(ybuf.at[slot], ys_hbm.at[pl.ds(t * tr, tr)], sem_out.at[slot])

    @pl.when(e == 0)
    def _():
        for t in range(n_in - 1):
            @pl.when(t < n_tiles)
            def _():
                fetch(t).start()

    wgu_b[...] = wgu_ref[...].astype(BF16)
    wdn_b[...] = wdn_ref[...].astype(BF16)

    def tile(t, carry):
        fetch(t).wait()

        @pl.when(t + n_in - 1 < n_tiles)
        def _():
            fetch(t + n_in - 1).start()

        lo, hi = _unpack_rows(xbuf[t & (n_in - 1)])
        gu = _dot(lo.astype(BF16), wgu_b[:half, :]) + _dot(hi.astype(BF16), wgu_b[half:, :])
        hh = gu.shape[1] // 2
        act = (jax.nn.silu(gu[:, :hh]) * gu[:, hh:]).astype(BF16)
        y = _pack_rows(_dot(act, wdn_b[...]))

        @pl.when(t >= n_out)
        def _():
            flush(t - n_out).wait()

        ybuf[t & (n_out - 1)] = y
        flush(t).start()
        return carry

    lax.fori_loop(t0_ref[e], t1_ref[e], tile, 0)

    @pl.when(e == n_exp - 1)
    def _():
        for back in range(n_out, 0, -1):
            @pl.when(n_tiles >= back)
            def _():
                flush(n_tiles - back).wait()


def _experts(tile_lo, tile_hi, xs, w_gu, w_dn, after, *, tr):
    n_exp, d, h2w = w_gu.shape
    hh = w_dn.shape[1]
    in_specs = [pl.BlockSpec(memory_space=pl.ANY),
                pl.BlockSpec((None, d, h2w), lambda e, lo, hi: (e, 0, 0)),
                pl.BlockSpec((None, hh, d), lambda e, lo, hi: (e, 0, 0))]
    operands = [xs, w_gu, w_dn]
    n_read = 2 + len(operands)
    if after is not None:
        _unread_operand(in_specs, operands, after)
    n_in = 2 + len(operands)
    return pl.pallas_call(
        lambda *refs: _experts_kernel(*refs[:n_read], *refs[n_in:]),
        out_shape=jax.ShapeDtypeStruct(xs.shape, jnp.int32),
        grid_spec=pltpu.PrefetchScalarGridSpec(
            num_scalar_prefetch=2,
            grid=(n_exp,),
            in_specs=in_specs,
            out_specs=pl.BlockSpec(memory_space=pl.ANY),
            scratch_shapes=[pltpu.VMEM((EXPERT_IN_SLOTS, tr, d // 2), jnp.int32),
                            pltpu.VMEM((EXPERT_OUT_SLOTS, tr, d // 2), jnp.int32),
                            pltpu.VMEM((d, h2w), BF16), pltpu.VMEM((hh, d), BF16),
                            pltpu.SemaphoreType.DMA((EXPERT_IN_SLOTS,)), pltpu.SemaphoreType.DMA((EXPERT_OUT_SLOTS,))]),
        compiler_params=_cparams("arbitrary"),
        name="experts",
    )(tile_lo, tile_hi, *operands)


def _combine_kernel(base_ref, g2_ref, gate_ref, *refs):
    r_refs, o_ref = refs[:TOP_K], refs[TOP_K]
    gate = gate_ref[...]
    half = base_ref.shape[1] // 2
    acc_lo = jnp.zeros((base_ref.shape[0], half), F32)
    acc_hi = jnp.zeros((base_ref.shape[0], half), F32)
    for k in range(TOP_K):
        lo, hi = _unpack_rows(r_refs[k][...])
        g = gate[:, k:k + 1]
        acc_lo = acc_lo + g * lo
        acc_hi = acc_hi + g * hi
    g2 = g2_ref[...]
    o_ref[:, :half] = base_ref[:, :half] + g2[:, :half] * acc_lo
    o_ref[:, half:] = base_ref[:, half:] + g2[:, half:] * acc_hi


def _combine(base, mod4, gate, r, out_so_far, after, *, tm, tokens_per_batch, tok0, part_lo, total_tokens):
    d = base.shape[1]
    t = r.shape[0] // TOP_K
    tpb = tokens_per_batch // tm
    nblk = t // tm
    pblk0 = part_lo // tm
    blk0 = (tok0 + part_lo) // tm
    r_spec = lambda k: pl.BlockSpec((tm, d // 2), lambda i: (k * nblk + i, 0))
    in_specs = [pl.BlockSpec((tm, d), lambda i: (i + pblk0, 0)),
                pl.BlockSpec((None, None, 1, d), lambda i: ((i + blk0) // tpb, 5, 0, 0)),
                pl.BlockSpec((tm, LANES), lambda i: (i + pblk0, 0))] + [r_spec(k) for k in range(TOP_K)]
    operands = [base, mod4, gate] + [r] * TOP_K
    n_read = len(operands)
    aliases = {}
    if out_so_far is not None:
        aliases = {_unread_operand(in_specs, operands, out_so_far): 0}
    if after is not None:
        _unread_operand(in_specs, operands, after)
    n_in = len(operands)
    return pl.pallas_call(
        lambda *refs: _combine_kernel(*refs[:n_read], *refs[n_in:]),
        out_shape=jax.ShapeDtypeStruct((total_tokens, d), F32),
        grid=(nblk,),
        in_specs=in_specs,
        out_specs=pl.BlockSpec((tm, d), lambda i: (i + blk0, 0)),
        input_output_aliases=aliases,
        compiler_params=_cparams("arbitrary"),
        name="combine",
    )(*operands)


def _rope_tables(length, head_dim):
    rows = length // GRID_W
    pairs_axis = head_dim // 4
    r = jnp.repeat(jnp.arange(rows, dtype=F32), GRID_W)
    col = jnp.tile(jnp.arange(GRID_W, dtype=F32), rows)
    inv = ROPE_THETA ** (-jnp.arange(pairs_axis, dtype=F32) / pairs_axis)
    ang = jnp.concatenate([r[:, None] * inv, col[:, None] * inv], axis=-1)
    cos, sin = jnp.cos(ang), jnp.sin(ang)
    cos_h = jnp.concatenate([cos, cos], axis=-1)
    sin_h = jnp.concatenate([-sin, sin], axis=-1)
    reps = LANES // head_dim
    return jnp.tile(cos_h, (1, reps)), jnp.tile(sin_h, (1, reps))


def _part_batches(b):
    if b < 2:
        return [b]
    lead = max(1, min(b - 1, round(LEAD_PART_SHARE * b)))
    return [lead, b - lead]


def kernel(x, c, ctx, c_ctx, w_ada, b_ada, norm_mix, w_in, q_norm, k_norm, gm_norm, w_spatial, b_spatial,
           out_norm_attn, out_norm_gm, w_out, norm_ffn, w_router, router_bias, w_exp_gu, w_exp_down,
           w_sh_gu, w_sh_down):
    assert w_ada.shape[0] == 1, "single-layer kernel"
    b, length, d = x.shape
    head_dim = q_norm.shape[-1]
    att_w = out_norm_attn.shape[-1]
    gm_w = out_norm_gm.shape[-1]
    gm_groups = w_spatial.shape[1]
    gm_hd = gm_w // gm_groups
    n_exp = w_router.shape[-1]
    t = b * length
    assert head_dim == LANES // 2 and gm_hd == LANES // 2 and w_spatial.shape[-1] == CHUNK
    assert n_exp <= LANES and att_w % MXU_EDGE == 0 and gm_w % MXU_EDGE == 0

    mod_rows = 16
    c_all = jnp.zeros((mod_rows, d), F32).at[:b].set(c).at[b].set(c_ctx)
    mod = _ada(c_all, w_ada[0], b_ada)
    mod4 = mod.reshape(mod_rows, N_MOD, 1, d)

    w_in_b = w_in[0].astype(BF16)
    bd = (jnp.arange(MXU_EDGE)[:, None] // head_dim == jnp.arange(MXU_EDGE)[None, :] // head_dim)
    bd = (bd.astype(F32) / head_dim).astype(BF16)
    qg = jnp.tile(q_norm[0] * (head_dim ** -0.5 * LOG2_E), att_w // head_dim)[None, :]
    kg = jnp.tile(k_norm[0], LANES // head_dim)[None, :]
    gmg = jnp.tile(gm_norm[0], gm_groups)[None, :]
    ws = w_spatial[0].astype(BF16).reshape(gm_groups // 2, 2 * CHUNK, CHUNK)
    bs = jnp.repeat(b_spatial[0].T, gm_hd, axis=1)
    cos_t, sin_t = _rope_tables(length, head_dim)

    k_c, v_c = _ctx_kv(ctx, mod4, b, norm_mix, w_in_b[:, att_w:att_w + 2 * LANES], kg, bd)
    q, k_l, v_l, gmn = _inproj(x, mod4, norm_mix, w_in_b, cos_t, sin_t, qg, kg, gmg, bd, ws, bs,
                               out_norm_gm, att_w=att_w, gm_w=gm_w, tm=min(1024, length))
    wr_t = w_router[0].T.astype(BF16)
    rb = jnp.tile(router_bias[0][:, None], (1, LANES))
    w_out_b, w_sg_b, w_sd_b = w_out[0].astype(BF16), w_sh_gu[0].astype(BF16), w_sh_down[0].astype(BF16)
    gmn2, x2 = gmn.reshape(t, gm_w), x.reshape(t, d)
    tm_post = min(1024, length)
    tr = 512

    part_nb = _part_batches(b)
    n_parts = len(part_nb)
    part_b0 = [sum(part_nb[:p]) for p in range(n_parts)]
    state = [dict() for _ in range(n_parts)]

    last_tc = [None]

    def halves(part):
        nb = part_nb[part]
        return (nb // 2, nb - nb // 2) if nb >= 2 else (nb,)

    def attend(part, half):
        if half >= len(halves(part)):
            return
        st = state[part]
        slot0 = sum(halves(part)[:half])
        after = None if last_tc[0] is st.get("att") else last_tc[0]
        st["att"] = _attention(q, k_c, k_l, v_c, v_l, st.get("att"), after, tq=min(256, length),
                               b0=part_b0[part] + slot0, nb=halves(part)[half], slot0=slot0, slots=part_nb[part])
        last_tc[0] = st["att"]

    def route(part):
        st = state[part]
        tp = part_nb[part] * length
        nt = tp * TOP_K // tr + n_exp
        base, h2p, idx_t, gate, rank_t, cnt = _post(
            st["att"].reshape(tp, att_w), gmn2, x2, mod4, out_norm_attn, w_out_b, norm_ffn, wr_t, rb, w_sg_b, w_sd_b,
            tm=tm_post, tokens_per_batch=length, tok0=part_b0[part] * length)
        counts = cnt[:, 0].astype(jnp.int32)
        padded = (counts + tr - 1) // tr * tr
        ends = jnp.cumsum(padded)
        offs = ends - padded
        offs_pad = jnp.zeros((LANES,), jnp.int32).at[:n_exp].set(offs)
        xs, pos_t = _dispatch(h2p, idx_t, rank_t, offs_pad, n_slots=nt * tr)
        st.update(base=base, gate=gate, xs=xs, pos_t=pos_t, tile_lo=offs // tr, tile_hi=ends // tr)
        last_tc[0] = base

    def run_experts(part):
        st = state[part]
        ys = _experts(st["tile_lo"], st["tile_hi"], st["xs"], w_exp_gu[0], w_exp_down[0], last_tc[0], tr=tr)
        n_chunks = part_nb[part] if part == n_parts - 1 else 1
        chunk = part_nb[part] * length // n_chunks
        st["r"] = [(c * chunk, _collect(ys, st["pos_t"], tok_lo=c * chunk, n_tok=chunk)) for c in range(n_chunks)]
        last_tc[0] = ys

    result = [None]

    def finish(part):
        st = state[part]
        for part_lo, r in st["r"]:
            after = None if last_tc[0] is result[0] or last_tc[0] is st["base"] else last_tc[0]
            result[0] = _combine(st["base"], mod4, st["gate"], r, result[0], after, tm=min(512, length),
                                 tokens_per_batch=length, tok0=part_b0[part] * length, part_lo=part_lo,
                                 total_tokens=t)
            last_tc[0] = result[0]

    stages = [(0, lambda p: attend(p, 0)), (1, lambda p: attend(p, 1)), (2, route), (4, run_experts), (7, finish)]
    plan = sorted((when + 3 * part, part, n, fn) for part in range(n_parts) for n, (when, fn) in enumerate(stages))
    for _, part, _, fn in plan:
        fn(part)
    return result[0].reshape(b, length, d)
```

```python
import dataclasses
import functools

import jax
import jax.numpy as jnp
from jax import lax
from jax.experimental import pallas as pl
from jax.experimental.pallas import tpu as pltpu
from jax.experimental.pallas import tpu_sc as plsc

F32 = jnp.float32
BF16 = jnp.bfloat16

EPS = 1e-6
GRID_W = 64
ROPE_THETA = 10000.0
ATT_KV_HEADS = 2
TOP_K = 6
ROUTED_SCALE = 2.5
N_MOD = 6
CHUNK = 128
LOG2_E = 1.4426950408889634

LANES = 128
MXU_EDGE = 256
VMEM_LIMIT_BYTES = 56 * 1024 * 1024
SUBLANES = 8
CHOICE_ROWS = SUBLANES
SC_WINDOW = 64
LEAD_PART_SHARE = 0.75
ROW_BLOCK = 512
EXPERT_IN_SLOTS = 4
EXPERT_OUT_SLOTS = 2


def _cparams(*sem):
    return pltpu.CompilerParams(dimension_semantics=sem, vmem_limit_bytes=VMEM_LIMIT_BYTES)


def _dot(a, b):
    return jnp.dot(a, b, preferred_element_type=F32)


def _dot_nt(a, b):
    return lax.dot_general(a, b, (((1,), (1,)), ((), ())), preferred_element_type=F32)


def _rms(x, g):
    return x * lax.rsqrt(jnp.mean(x * x, axis=-1, keepdims=True) + EPS) * g


def _group_mean_sq(x, bd_ref):
    x2 = (x * x).astype(BF16)
    parts = [_dot(x2[:, i:i + MXU_EDGE], bd_ref[...]) for i in range(0, x.shape[1], MXU_EDGE)]
    return parts[0] if len(parts) == 1 else jnp.concatenate(parts, axis=-1)


def _dup_halves(x, lane):
    xr = pltpu.roll(x, LANES // 2, axis=1)
    lo = lane < LANES // 2
    return jnp.where(lo, x, xr), jnp.where(lo, xr, x)


def _with_ones(x, lane):
    xr = pltpu.roll(x, LANES // 2, axis=1)
    lo = lane < LANES // 2
    return jnp.concatenate([jnp.where(lo, x, 1.0), jnp.where(lo, xr, 1.0)], axis=-1)


def _pipelined_blocks(tm, sub, head, tail):
    blocks = [pl.ds(r0, sub) for r0 in range(0, tm, sub)]
    pending = head(blocks[0])
    for n, rows in enumerate(blocks):
        ahead = head(blocks[n + 1]) if n + 1 < len(blocks) else None
        tail(rows, pending)
        pending = ahead


_HI16 = 0xFFFF0000


def _pack_rows(x):
    half = x.shape[1] // 2
    rounded = x.astype(BF16).astype(F32)
    bits = lax.bitcast_convert_type(rounded, jnp.uint32)
    word = (bits[:, :half] >> 16) | (bits[:, half:] & jnp.uint32(_HI16))
    return lax.bitcast_convert_type(word, jnp.int32)


def _unpack_rows(w):
    bits = lax.bitcast_convert_type(w, jnp.uint32)
    lo = lax.bitcast_convert_type(bits << 16, F32)
    hi = lax.bitcast_convert_type(bits & jnp.uint32(_HI16), F32)
    return lo, hi


def _ada_kernel(c_ref, w_ref, b_ref, o_ref):
    s = jax.nn.silu(c_ref[...])
    o_ref[...] = jnp.dot(s, w_ref[...], precision=lax.Precision.HIGHEST,
                         preferred_element_type=F32) + b_ref[...]


def _ada(c_all, w_ada, b_ada):
    rows, d = c_all.shape
    n = w_ada.shape[1]
    bn = 2048 if n % 2048 == 0 else 512
    return pl.pallas_call(
        _ada_kernel,
        out_shape=jax.ShapeDtypeStruct((rows, n), F32),
        grid=(n // bn,),
        in_specs=[pl.BlockSpec((rows, d), lambda j: (0, 0)),
                  pl.BlockSpec((d, bn), lambda j: (0, j)),
                  pl.BlockSpec((1, bn), lambda j: (0, j))],
        out_specs=pl.BlockSpec((rows, bn), lambda j: (0, j)),
        compiler_params=_cparams("arbitrary"),
        name="ada",
    )(c_all, w_ada, b_ada)


def _ctx_kv_kernel(x_ref, sh_ref, sc_ref, nmix_ref, w_ref, kg_ref, bd_ref, k_ref, v_ref):
    h = _rms(x_ref[...], nmix_ref[...]) * (1.0 + sc_ref[...]) + sh_ref[...]
    p = _dot(h.astype(BF16), w_ref[...])
    ms = _group_mean_sq(p, bd_ref)
    k = p[:, :LANES] * lax.rsqrt(ms[:, :LANES] + EPS) * kg_ref[...]
    v = p[:, LANES:]
    lane = lax.broadcasted_iota(jnp.int32, k.shape, 1)
    ka, kb = _dup_halves(k, lane)
    k_ref[...] = jnp.concatenate([ka, kb], axis=-1).astype(BF16)
    v_ref[...] = _with_ones(v, lane).astype(BF16)


def _ctx_kv(ctx, mod4, ctx_row, nmix, w_kv, kg, bd):
    b, n_ctx, d = ctx.shape
    kv_w = 2 * LANES
    return pl.pallas_call(
        _ctx_kv_kernel,
        out_shape=(jax.ShapeDtypeStruct((b, n_ctx, kv_w), BF16),) * 2,
        grid=(b,),
        in_specs=[pl.BlockSpec((None, n_ctx, d), lambda i: (i, 0, 0)),
                  pl.BlockSpec((None, None, 1, d), lambda i: (ctx_row, 0, 0, 0)),
                  pl.BlockSpec((None, None, 1, d), lambda i: (ctx_row, 1, 0, 0)),
                  pl.BlockSpec((1, d), lambda i: (0, 0)),
                  pl.BlockSpec(w_kv.shape, lambda i: (0, 0)),
                  pl.BlockSpec((1, LANES), lambda i: (0, 0)),
                  pl.BlockSpec(bd.shape, lambda i: (0, 0))],
        out_specs=(pl.BlockSpec((None, n_ctx, kv_w), lambda i: (i, 0, 0)),) * 2,
        compiler_params=_cparams("arbitrary"),
        name="ctx_kv",
    )(ctx, mod4, mod4, nmix, w_kv, kg, bd)


def _inproj_kernel(x_ref, sh_ref, sc_ref, nmix_ref, w_ref, cos_ref, sin_ref, qg_ref, kg_ref, gmg_ref,
                   bd_ref, ws_ref, bs_ref, og_ref, q_ref, k_ref, v_ref, gm_ref, *, att_w, gm_w):
    tm = x_ref.shape[0]
    sub = min(tm, ROW_BLOCK)

    def project(rows):
        h = _rms(x_ref[rows, :], nmix_ref[...]) * (1.0 + sc_ref[...]) + sh_ref[...]
        return _dot(h.astype(BF16), w_ref[...])

    def rope(xn, cosw, sinw, lanew):
        w = xn.shape[1]
        fwd = pltpu.roll(xn, w - 32, axis=1)
        bwd = pltpu.roll(xn, 32, axis=1)
        swapped = jnp.where((lanew & 63) < 32, fwd, bwd)
        return xn * cosw + swapped * sinw

    def finish(rows, p):
        cos1, sin1 = cos_ref[rows, :], sin_ref[rows, :]
        lane1 = lax.broadcasted_iota(jnp.int32, (sub, LANES), 1)

        q = p[:, :att_w]
        qn = q * lax.rsqrt(_group_mean_sq(q, bd_ref) + EPS) * qg_ref[...]
        reps = att_w // LANES
        cosq = jnp.concatenate([cos1] * reps, axis=-1)
        sinq = jnp.concatenate([sin1] * reps, axis=-1)
        laneq = lax.broadcasted_iota(jnp.int32, (sub, att_w), 1)
        q_ref[rows, :] = rope(qn, cosq, sinq, laneq).astype(BF16)

        kv = p[:, att_w:att_w + 2 * LANES]
        ms = _group_mean_sq(kv, bd_ref)
        kn = kv[:, :LANES] * lax.rsqrt(ms[:, :LANES] + EPS) * kg_ref[...]
        k = rope(kn, cos1, sin1, lane1)
        v = kv[:, LANES:]
        ka, kb = _dup_halves(k, lane1)
        k_ref[rows, :] = jnp.concatenate([ka, kb], axis=-1).astype(BF16)
        v_ref[rows, :] = _with_ones(v, lane1).astype(BF16)

        u0 = att_w + 2 * LANES
        u = jax.nn.gelu(p[:, u0:u0 + gm_w])
        gt = jax.nn.gelu(p[:, u0 + gm_w:u0 + 2 * gm_w])
        gt = (gt * lax.rsqrt(_group_mean_sq(gt, bd_ref) + EPS) * gmg_ref[...]).astype(BF16)
        lo_half = lax.broadcasted_iota(jnp.int32, (CHUNK, LANES), 1) < LANES // 2
        mixed_rows = []
        for c in range(sub // CHUNK):
            blocks = []
            for j in range(gm_w // LANES):
                g = gt[c * CHUNK:(c + 1) * CHUNK, j * LANES:(j + 1) * LANES]
                r = _dot(ws_ref[j], g)
                blocks.append(jnp.where(lo_half, r[:CHUNK], r[CHUNK:]))
            mixed_rows.append(jnp.concatenate(blocks, axis=-1) + bs_ref[...])
        mixed = jnp.concatenate(mixed_rows, axis=0)
        gm_ref[rows, :] = _rms(u * mixed, og_ref[...]).astype(BF16)

    _pipelined_blocks(tm, sub, project, finish)


def _inproj(x, mod4, nmix, w_in, cos_t, sin_t, qg, kg, gmg, bd, ws, bs, og, *, att_w, gm_w, tm):
    b, length, d = x.shape
    kv_w = 2 * LANES
    n_in = w_in.shape[1]
    const = lambda shape: pl.BlockSpec(shape, lambda i, j: (0,) * len(shape))
    return pl.pallas_call(
        functools.partial(_inproj_kernel, att_w=att_w, gm_w=gm_w),
        out_shape=(jax.ShapeDtypeStruct((b, length, att_w), BF16),
                   jax.ShapeDtypeStruct((b, length, kv_w), BF16),
                   jax.ShapeDtypeStruct((b, length, kv_w), BF16),
                   jax.ShapeDtypeStruct((b, length, gm_w), BF16)),
        grid=(b, length // tm),
        in_specs=[pl.BlockSpec((None, tm, d), lambda i, j: (i, j, 0)),
                  pl.BlockSpec((None, None, 1, d), lambda i, j: (i, 0, 0, 0)),
                  pl.BlockSpec((None, None, 1, d), lambda i, j: (i, 1, 0, 0)),
                  const((1, d)),
                  const((d, n_in)),
                  pl.BlockSpec((tm, LANES), lambda i, j: (j, 0)),
                  pl.BlockSpec((tm, LANES), lambda i, j: (j, 0)),
                  const((1, att_w)), const((1, LANES)), const((1, gm_w)),
                  const(bd.shape), const(ws.shape), const(bs.shape), const((1, gm_w))],
        out_specs=(pl.BlockSpec((None, tm, att_w), lambda i, j: (i, j, 0)),
                   pl.BlockSpec((None, tm, kv_w), lambda i, j: (i, j, 0)),
                   pl.BlockSpec((None, tm, kv_w), lambda i, j: (i, j, 0)),
                   pl.BlockSpec((None, tm, gm_w), lambda i, j: (i, j, 0))),
        compiler_params=_cparams("arbitrary", "arbitrary"),
        name="inproj",
    )(x, mod4, mod4, nmix, w_in, cos_t, sin_t, qg, kg, gmg, bd, ws, bs, og)


def _attn_kernel(q_ref, kc_ref, kl_ref, vc_ref, vl_ref, o_ref, s_buf, p_buf, *, tq):
    n_ctx = kc_ref.shape[0]
    heads = q_ref.shape[1] // (LANES // 2)
    nj = q_ref.shape[0] // tq
    half = LANES // 2
    lo = lax.broadcasted_iota(jnp.int32, (tq, LANES), 1) < half

    @pl.when(jnp.logical_and(pl.program_id(0) == 0, pl.program_id(1) == 0))
    def _():
        s_buf[...] = jnp.zeros_like(s_buf)
        p_buf[...] = jnp.ones_like(p_buf)

    def scores(j, g):
        q = q_ref[pl.ds(j * tq, tq), (g // 2) * LANES:(g // 2 + 1) * LANES]
        zero = jnp.zeros_like(q)
        qh = jnp.where(lo, q, zero) if g % 2 == 0 else jnp.where(lo, zero, q)
        s_buf[g % 2, :, :n_ctx] = _dot_nt(qh, kc_ref[...])
        s_buf[g % 2, :, n_ctx:] = _dot_nt(qh, kl_ref[...])

    def probs(g):
        s = s_buf[g % 2]
        p_buf[g % 2] = jnp.exp2(s - jnp.max(s, axis=-1, keepdims=True)).astype(BF16)

    def output(j, g):
        p = p_buf[g % 2]
        w = _dot(p[:, :n_ctx], vc_ref[...]) + _dot(p[:, n_ctx:], vl_ref[...])
        t = jnp.where(pl.program_id(1) == 0, w[:, :LANES], w[:, LANES:])
        r = pltpu.roll(t, half, axis=1)
        rows = pl.ds(j * tq, tq)
        if g % 2 == 0:
            o_ref[rows, g * half:(g + 1) * half] = (t / r)[:, :half].astype(o_ref.dtype)
        else:
            o_ref[rows, g * half:(g + 1) * half] = (r / t)[:, half:].astype(o_ref.dtype)

    def tile_stages(j):
        j_prev = jnp.maximum(j - 1, 0)
        for g in range(heads):
            scores(j, g)
            probs((g - 1) % heads)
            output(j if g >= 2 else j_prev, (g - 2) % heads)

    def trip(jj, carry):
        for u in range(tiles_per_trip):
            tile_stages(jj * tiles_per_trip + u)
        return carry

    tiles_per_trip = 2 if nj % 2 == 0 else 1
    lax.fori_loop(0, nj // tiles_per_trip, trip, 0)
    probs(heads - 1)
    output(nj - 1, heads - 2)
    output(nj - 1, heads - 1)


def _unread_operand(in_specs, operands, array):
    in_specs.append(pl.BlockSpec(memory_space=pl.ANY))
    operands.append(array)
    return len(operands) - 1


def _attention(q, k_c, k_l, v_c, v_l, att_so_far, after, *, tq, b0, nb, slot0, slots):
    _, length, att_w = q.shape
    n_ctx = k_c.shape[1]
    group_w = att_w // ATT_KV_HEADS
    assert (group_w // (LANES // 2)) % 2 == 0 and length % tq == 0
    k_blk = lambda n: pl.BlockSpec((None, n, LANES), lambda i, h: (i + b0, 0, h))
    v_blk = lambda n: pl.BlockSpec((None, n, ATT_KV_HEADS * LANES), lambda i, h: (i + b0, 0, 0))
    in_specs = [pl.BlockSpec((None, length, group_w), lambda i, h: (i + b0, 0, h)),
                k_blk(n_ctx), k_blk(length), v_blk(n_ctx), v_blk(length)]
    operands = [q, k_c, k_l, v_c, v_l]
    n_read = len(operands)
    aliases = {}
    if att_so_far is not None:
        aliases = {_unread_operand(in_specs, operands, att_so_far): 0}
    if after is not None:
        _unread_operand(in_specs, operands, after)
    n_in = len(operands)
    return pl.pallas_call(
        lambda *refs: _attn_kernel(*refs[:n_read], *refs[n_in:], tq=tq),
        out_shape=jax.ShapeDtypeStruct((slots, length, att_w), BF16),
        grid=(nb, ATT_KV_HEADS),
        in_specs=in_specs,
        out_specs=pl.BlockSpec((None, length, group_w), lambda i, h: (i + slot0, 0, h)),
        scratch_shapes=[pltpu.VMEM((2, tq, n_ctx + length), F32), pltpu.VMEM((2, tq, n_ctx + length), BF16)],
        input_output_aliases=aliases,
        compiler_params=_cparams("arbitrary", "arbitrary"),
        name="attn",
    )(*operands)


def _post_kernel(att_ref, gm_ref, x_ref, g1_ref, sh2_ref, sc2_ref, g2_ref, oa_ref, wo_ref, nffn_ref,
                 wrt_ref, rb_ref, wsg_ref, wsd_ref,
                 base_ref, h2p_ref, idx_ref, gate_ref, rank_ref, cnt_ref, carry_ref):
    i = pl.program_id(0)
    tm = x_ref.shape[0]
    n_exp = wrt_ref.shape[0]

    @pl.when(i == 0)
    def _():
        carry_ref[...] = jnp.zeros_like(carry_ref)

    sub = min(tm, ROW_BLOCK)
    reps = sub // LANES
    neg = jnp.float32(-jnp.inf)
    r_io = lax.broadcasted_iota(jnp.int32, (sub, sub), 0)
    c_io = lax.broadcasted_iota(jnp.int32, (sub, sub), 1)
    tri = jnp.where(r_io < c_io, 1.0, 0.0).astype(BF16)

    def residual(rows):
        att_n = _rms(att_ref[rows, :].astype(F32), oa_ref[...]).astype(BF16)
        y = _dot(jnp.concatenate([att_n, gm_ref[rows, :]], axis=-1), wo_ref[...])
        x_new = x_ref[rows, :] + g1_ref[...] * y
        h2 = _rms(x_new, nffn_ref[...]) * (1.0 + sc2_ref[...]) + sh2_ref[...]
        h2p_ref[rows, :] = _pack_rows(h2)
        return x_new, h2

    def route_and_share(rows, carried):
        x_new, h2 = carried
        h_hi = h2.astype(BF16)
        scores_t = jax.nn.sigmoid(_dot_nt(wrt_ref[...], h_hi))
        sel = scores_t + jnp.concatenate([rb_ref[...]] * reps, axis=-1)
        expert = lax.broadcasted_iota(jnp.int32, scores_t.shape, 0).astype(F32)

        onehot = jnp.zeros(scores_t.shape, F32)
        hits, idxs, svals = [], [], []
        for _k in range(TOP_K):
            m = jnp.max(sel, axis=0, keepdims=True)
            idx = jnp.min(jnp.where(sel == m, expert, float(n_exp)), axis=0, keepdims=True)
            hit = expert == idx
            hits.append(hit)
            idxs.append(idx)
            svals.append(jnp.sum(jnp.where(hit, scores_t, 0.0), axis=0, keepdims=True))
            onehot = onehot + jnp.where(hit, 1.0, 0.0)
            sel = jnp.where(hit, neg, sel)
        ssum = functools.reduce(lambda a, b: a + b, svals)

        carry = carry_ref[...]
        before = _dot(onehot.astype(BF16), tri) + jnp.concatenate([carry] * reps, axis=-1)
        carry = carry + jnp.sum(onehot, axis=1, keepdims=True)
        carry_ref[...] = carry
        cnt_ref[...] = carry

        row = lax.broadcasted_iota(jnp.int32, (idx_ref.shape[0], sub), 0)
        idx_o = jnp.zeros(row.shape, F32)
        rank_o = jnp.zeros(row.shape, F32)
        gate_o = jnp.zeros(row.shape, F32)
        for k in range(TOP_K):
            rank = jnp.sum(jnp.where(hits[k], before, 0.0), axis=0, keepdims=True)
            idx_o = jnp.where(row == k, idxs[k], idx_o)
            rank_o = jnp.where(row == k, rank, rank_o)
            gate_o = jnp.where(row == k, svals[k] / ssum * ROUTED_SCALE, gate_o)
        idx_ref[:, rows] = idx_o.astype(jnp.int32)
        rank_ref[:, rows] = rank_o.astype(jnp.int32)
        gate_pad = jnp.concatenate([gate_o, jnp.zeros((LANES - gate_o.shape[0], sub), F32)], axis=0)
        gate_ref[rows, :] = gate_pad.T

        gu = _dot(h_hi, wsg_ref[...])
        hs = gu.shape[1] // 2
        act = (jax.nn.silu(gu[:, :hs]) * gu[:, hs:]).astype(BF16)
        base_ref[rows, :] = x_new + g2_ref[...] * _dot(act, wsd_ref[...])

    _pipelined_blocks(tm, sub, residual, route_and_share)


def _post(att, gmn, x2, mod4, oa, w_out, nffn, wr_t, rb, w_sg, w_sd, *, tm, tokens_per_batch, tok0):
    t, att_w = att.shape
    d = x2.shape[1]
    n_exp = wr_t.shape[0]
    gm_w = gmn.shape[1]
    tpb = tokens_per_batch // tm
    blk0 = tok0 // tm
    const = lambda shape: pl.BlockSpec(shape, lambda i: (0,) * len(shape))
    mrow = lambda j: pl.BlockSpec((None, None, 1, d), lambda i: ((i + blk0) // tpb, j, 0, 0))
    by_choice = jax.ShapeDtypeStruct((CHOICE_ROWS, t), jnp.int32)
    return pl.pallas_call(
        _post_kernel,
        out_shape=(jax.ShapeDtypeStruct((t, d), F32), jax.ShapeDtypeStruct((t, d // 2), jnp.int32),
                   by_choice, jax.ShapeDtypeStruct((t, LANES), F32), by_choice,
                   jax.ShapeDtypeStruct((n_exp, LANES), F32)),
        grid=(t // tm,),
        in_specs=[pl.BlockSpec((tm, att_w), lambda i: (i, 0)),
                  pl.BlockSpec((tm, gm_w), lambda i: (i + blk0, 0)),
                  pl.BlockSpec((tm, d), lambda i: (i + blk0, 0)),
                  mrow(2), mrow(3), mrow(4), mrow(5),
                  const((1, att_w)), const(w_out.shape), const((1, d)),
                  const(wr_t.shape), const(rb.shape),
                  const(w_sg.shape), const(w_sd.shape)],
        out_specs=(pl.BlockSpec((tm, d), lambda i: (i, 0)), pl.BlockSpec((tm, d // 2), lambda i: (i, 0)),
                   pl.BlockSpec((CHOICE_ROWS, tm), lambda i: (0, i)), pl.BlockSpec((tm, LANES), lambda i: (i, 0)),
                   pl.BlockSpec((CHOICE_ROWS, tm), lambda i: (0, i)), pl.BlockSpec((n_exp, LANES), lambda i: (0, 0))),
        scratch_shapes=[pltpu.VMEM((n_exp, LANES), F32)],
        compiler_params=_cparams("arbitrary"),
        name="post",
    )(att, gmn, x2, mod4, mod4, mod4, mod4, oa, w_out, nffn, wr_t, rb, w_sg, w_sd)


def _sc_mesh_info():
    info = plsc.get_sparse_core_info()
    mesh = plsc.VectorSubcoreMesh(core_axis_name="c", subcore_axis_name="s")
    return mesh, info.num_cores, info.num_subcores, info.num_lanes


def _sc_params():
    cp = pltpu.CompilerParams()
    if "needs_layout_passes" in pltpu.CompilerParams.__dataclass_fields__:
        cp = dataclasses.replace(cp, needs_layout_passes=False)
    return cp


def _dispatch(h2p, idx_t, rank_t, offs, *, n_slots):
    t, dw = h2p.shape
    mesh, n_cores, n_sub, n_lanes = _sc_mesh_info()
    per_worker = t // (n_cores * n_sub)
    w = SC_WINDOW
    assert per_worker % w == 0 and w % n_lanes == 0

    @functools.partial(
        pl.kernel, mesh=mesh,
        out_type=(jax.ShapeDtypeStruct((n_slots, dw), jnp.int32), jax.ShapeDtypeStruct((CHOICE_ROWS, t), jnp.int32)),
        scratch_types=[pltpu.VMEM(offs.shape, jnp.int32)] + [pltpu.VMEM((TOP_K, w), jnp.int32)] * 3
        + [pltpu.VMEM((w, dw), jnp.int32), pltpu.SemaphoreType.DMA],
        compiler_params=_sc_params(),
    )
    def run(h_hbm, idx_hbm, rank_hbm, offs_hbm, xs_hbm, pos_hbm, offs_v, idx_v, rank_v, pos_v, rows_v, sem):
        base = (lax.axis_index("s") * n_cores + lax.axis_index("c")) * per_worker
        pltpu.sync_copy(offs_hbm, offs_v)

        @pl.loop(0, per_worker // w)
        def _(i):
            t0 = base + i * w
            pltpu.sync_copy(h_hbm.at[pl.ds(t0, w)], rows_v)
            for k in range(TOP_K):
                pltpu.sync_copy(idx_hbm.at[k, pl.ds(t0, w)], idx_v.at[k])
                pltpu.sync_copy(rank_hbm.at[k, pl.ds(t0, w)], rank_v.at[k])
            for k in range(TOP_K):
                for j in range(0, w, n_lanes):
                    group_start = plsc.load_gather(offs_v, [idx_v[k, pl.ds(j, n_lanes)]])
                    pos_v[k, pl.ds(j, n_lanes)] = group_start + rank_v[k, pl.ds(j, n_lanes)]
            copies = [pltpu.async_copy(rows_v, xs_hbm.at[pos_v.at[k]], sem) for k in range(TOP_K)]
            for k in range(TOP_K):
                pltpu.sync_copy(pos_v.at[k], pos_hbm.at[k, pl.ds(t0, w)])
            for cp in copies:
                cp.wait()

    return run(h2p, idx_t, rank_t, offs)


def _collect(ys, pos_t, *, tok_lo, n_tok):
    dw = ys.shape[1]
    mesh, n_cores, n_sub, _ = _sc_mesh_info()
    per_worker = n_tok // (n_cores * n_sub)
    w = SC_WINDOW
    assert per_worker % w == 0 and TOP_K % 2 == 0

    @functools.partial(
        pl.kernel, mesh=mesh,
        out_type=jax.ShapeDtypeStruct((TOP_K * n_tok, dw), jnp.int32),
        scratch_types=[pltpu.VMEM((TOP_K, w), jnp.int32), pltpu.VMEM((w, dw), jnp.int32),
                       pltpu.VMEM((w, dw), jnp.int32), pltpu.SemaphoreType.DMA, pltpu.SemaphoreType.DMA],
        compiler_params=_sc_params(),
    )
    def run(ys_hbm, pos_hbm, r_hbm, pos_v, rows_a, rows_b, sem_g, sem_w):
        base = (lax.axis_index("s") * n_cores + lax.axis_index("c")) * per_worker
        bufs = (rows_a, rows_b)

        @pl.loop(0, per_worker // w)
        def _(i):
            t0 = base + i * w
            for k in range(TOP_K):
                pltpu.sync_copy(pos_hbm.at[k, pl.ds(tok_lo + t0, w)], pos_v.at[k])
            gathers = [None] * TOP_K
            writes = [None] * TOP_K
            gathers[0] = pltpu.async_copy(ys_hbm.at[pos_v.at[0]], bufs[0], sem_g)
            for k in range(TOP_K):
                gathers[k].wait()
                if k + 1 < TOP_K:
                    if k >= 1:
                        writes[k - 1].wait()
                    gathers[k + 1] = pltpu.async_copy(ys_hbm.at[pos_v.at[k + 1]], bufs[(k + 1) % 2], sem_g)
                writes[k] = pltpu.async_copy(bufs[k % 2], r_hbm.at[pl.ds(k * n_tok + t0, w)], sem_w)
            writes[TOP_K - 2].wait()
            writes[TOP_K - 1].wait()

    return run(ys, pos_t)


def _experts_kernel(t0_ref, t1_ref, xs_hbm, wgu_ref, wdn_ref, ys_hbm, xbuf, ybuf, wgu_b, wdn_b, sem_in, sem_out):
    e = pl.program_id(0)
    n_exp = pl.num_programs(0)
    n_in, tr, half = xbuf.shape
    n_out = ybuf.shape[0]
    n_tiles = t1_ref[n_exp - 1]

    def fetch(t):
        slot = t & (n_in - 1)
        return pltpu.make_async_copy(xs_hbm.at[pl.ds(t * tr, tr)], xbuf.at[slot], sem_in.at[slot])

    def flush(t):
        slot = t & (n_out - 1)
        return pltpu.make_async_copy(ybuf.at[slot], ys_hbm.at[pl.ds(t * tr, tr)], sem_out.at[slot])

    @pl.when(e == 0)
    def _():
        for t in range(n_in - 1):
            @pl.when(t < n_tiles)
            def _():
                fetch(t).start()

    wgu_b[...] = wgu_ref[...].astype(BF16)
    wdn_b[...] = wdn_ref[...].astype(BF16)

    def tile(t, carry):
        fetch(t).wait()

        @pl.when(t + n_in - 1 < n_tiles)
        def _():
            fetch(t + n_in - 1).start()

        @pl.when(t >= n_out)
        def _():
            flush(t - n_out).wait()

        x_slot = xbuf.at[t & (n_in - 1)]
        y_slot = ybuf.at[t & (n_out - 1)]

        def up(rows):
            lo, hi = _unpack_rows(x_slot[rows, :])
            gu = _dot(lo.astype(BF16), wgu_b[:half, :]) + _dot(hi.astype(BF16), wgu_b[half:, :])
            hh = gu.shape[1] // 2
            return (jax.nn.silu(gu[:, :hh]) * gu[:, hh:]).astype(BF16)

        def down(rows, act):
            y_slot[rows, :] = _pack_rows(_dot(act, wdn_b[...]))

        _pipelined_blocks(tr, min(tr, ROW_BLOCK // 2), up, down)
        flush(t).start()
        return carry

    lax.fori_loop(t0_ref[e], t1_ref[e], tile, 0)

    @pl.when(e == n_exp - 1)
    def _():
        for back in range(n_out, 0, -1):
            @pl.when(n_tiles >= back)
            def _():
                flush(n_tiles - back).wait()


def _experts(tile_lo, tile_hi, xs, w_gu, w_dn, after, *, tr):
    n_exp, d, h2w = w_gu.shape
    hh = w_dn.shape[1]
    in_specs = [pl.BlockSpec(memory_space=pl.ANY),
                pl.BlockSpec((None, d, h2w), lambda e, lo, hi: (e, 0, 0)),
                pl.BlockSpec((None, hh, d), lambda e, lo, hi: (e, 0, 0))]
    operands = [xs, w_gu, w_dn]
    n_read = 2 + len(operands)
    if after is not None:
        _unread_operand(in_specs, operands, after)
    n_in = 2 + len(operands)
    return pl.pallas_call(
        lambda *refs: _experts_kernel(*refs[:n_read], *refs[n_in:]),
        out_shape=jax.ShapeDtypeStruct(xs.shape, jnp.int32),
        grid_spec=pltpu.PrefetchScalarGridSpec(
            num_scalar_prefetch=2,
            grid=(n_exp,),
            in_specs=in_specs,
            out_specs=pl.BlockSpec(memory_space=pl.ANY),
            scratch_shapes=[pltpu.VMEM((EXPERT_IN_SLOTS, tr, d // 2), jnp.int32),
                            pltpu.VMEM((EXPERT_OUT_SLOTS, tr, d // 2), jnp.int32),
                            pltpu.VMEM((d, h2w), BF16), pltpu.VMEM((hh, d), BF16),
                            pltpu.SemaphoreType.DMA((EXPERT_IN_SLOTS,)), pltpu.SemaphoreType.DMA((EXPERT_OUT_SLOTS,))]),
        compiler_params=_cparams("arbitrary"),
        name="experts",
    )(tile_lo, tile_hi, *operands)


def _combine_kernel(base_ref, g2_ref, gate_ref, *refs):
    r_refs, o_ref = refs[:TOP_K], refs[TOP_K]
    gate = gate_ref[...]
    half = base_ref.shape[1] // 2
    acc_lo = jnp.zeros((base_ref.shape[0], half), F32)
    acc_hi = jnp.zeros((base_ref.shape[0], half), F32)
    for k in range(TOP_K):
        lo, hi = _unpack_rows(r_refs[k][...])
        g = gate[:, k:k + 1]
        acc_lo = acc_lo + g * lo
        acc_hi = acc_hi + g * hi
    g2 = g2_ref[...]
    o_ref[:, :half] = base_ref[:, :half] + g2[:, :half] * acc_lo
    o_ref[:, half:] = base_ref[:, half:] + g2[:, half:] * acc_hi


def _combine(base, mod4, gate, r, out_so_far, after, *, tm, tokens_per_batch, tok0, part_lo, total_tokens):
    d = base.shape[1]
    t = r.shape[0] // TOP_K
    tpb = tokens_per_batch // tm
    nblk = t // tm
    pblk0 = part_lo // tm
    blk0 = (tok0 + part_lo) // tm
    r_spec = lambda k: pl.BlockSpec((tm, d // 2), lambda i: (k * nblk + i, 0))
    in_specs = [pl.BlockSpec((tm, d), lambda i: (i + pblk0, 0)),
                pl.BlockSpec((None, None, 1, d), lambda i: ((i + blk0) // tpb, 5, 0, 0)),
                pl.BlockSpec((tm, LANES), lambda i: (i + pblk0, 0))] + [r_spec(k) for k in range(TOP_K)]
    operands = [base, mod4, gate] + [r] * TOP_K
    n_read = len(operands)
    aliases = {}
    if out_so_far is not None:
        aliases = {_unread_operand(in_specs, operands, out_so_far): 0}
    if after is not None:
        _unread_operand(in_specs, operands, after)
    n_in = len(operands)
    return pl.pallas_call(
        lambda *refs: _combine_kernel(*refs[:n_read], *refs[n_in:]),
        out_shape=jax.ShapeDtypeStruct((total_tokens, d), F32),
        grid=(nblk,),
        in_specs=in_specs,
        out_specs=pl.BlockSpec((tm, d), lambda i: (i + blk0, 0)),
        input_output_aliases=aliases,
        compiler_params=_cparams("arbitrary"),
        name="combine",
    )(*operands)


def _rope_tables(length, head_dim):
    rows = length // GRID_W
    pairs_axis = head_dim // 4
    r = jnp.repeat(jnp.arange(rows, dtype=F32), GRID_W)
    col = jnp.tile(jnp.arange(GRID_W, dtype=F32), rows)
    inv = ROPE_THETA ** (-jnp.arange(pairs_axis, dtype=F32) / pairs_axis)
    ang = jnp.concatenate([r[:, None] * inv, col[:, None] * inv], axis=-1)
    cos, sin = jnp.cos(ang), jnp.sin(ang)
    cos_h = jnp.concatenate([cos, cos], axis=-1)
    sin_h = jnp.concatenate([-sin, sin], axis=-1)
    reps = LANES // head_dim
    return jnp.tile(cos_h, (1, reps)), jnp.tile(sin_h, (1, reps))


def _part_batches(b):
    if b < 2:
        return [b]
    lead = max(1, min(b - 1, round(LEAD_PART_SHARE * b)))
    return [lead, b - lead]


def kernel(x, c, ctx, c_ctx, w_ada, b_ada, norm_mix, w_in, q_norm, k_norm, gm_norm, w_spatial, b_spatial,
           out_norm_attn, out_norm_gm, w_out, norm_ffn, w_router, router_bias, w_exp_gu, w_exp_down,
           w_sh_gu, w_sh_down):
    assert w_ada.shape[0] == 1, "single-layer kernel"
    b, length, d = x.shape
    head_dim = q_norm.shape[-1]
    att_w = out_norm_attn.shape[-1]
    gm_w = out_norm_gm.shape[-1]
    gm_groups = w_spatial.shape[1]
    gm_hd = gm_w // gm_groups
    n_exp = w_router.shape[-1]
    t = b * length
    assert head_dim == LANES // 2 and gm_hd == LANES // 2 and w_spatial.shape[-1] == CHUNK
    assert n_exp <= LANES and att_w % MXU_EDGE == 0 and gm_w % MXU_EDGE == 0

    mod_rows = 16
    c_all = jnp.zeros((mod_rows, d), F32).at[:b].set(c).at[b].set(c_ctx)
    mod = _ada(c_all, w_ada[0], b_ada)
    mod4 = mod.reshape(mod_rows, N_MOD, 1, d)

    w_in_b = w_in[0].astype(BF16)
    bd = (jnp.arange(MXU_EDGE)[:, None] // head_dim == jnp.arange(MXU_EDGE)[None, :] // head_dim)
    bd = (bd.astype(F32) / head_dim).astype(BF16)
    qg = jnp.tile(q_norm[0] * (head_dim ** -0.5 * LOG2_E), att_w // head_dim)[None, :]
    kg = jnp.tile(k_norm[0], LANES // head_dim)[None, :]
    gmg = jnp.tile(gm_norm[0], gm_groups)[None, :]
    ws = w_spatial[0].astype(BF16).reshape(gm_groups // 2, 2 * CHUNK, CHUNK)
    bs = jnp.repeat(b_spatial[0].T, gm_hd, axis=1)
    cos_t, sin_t = _rope_tables(length, head_dim)

    k_c, v_c = _ctx_kv(ctx, mod4, b, norm_mix, w_in_b[:, att_w:att_w + 2 * LANES], kg, bd)
    q, k_l, v_l, gmn = _inproj(x, mod4, norm_mix, w_in_b, cos_t, sin_t, qg, kg, gmg, bd, ws, bs,
                               out_norm_gm, att_w=att_w, gm_w=gm_w, tm=min(1024, length))
    wr_t = w_router[0].T.astype(BF16)
    rb = jnp.tile(router_bias[0][:, None], (1, LANES))
    w_out_b, w_sg_b, w_sd_b = w_out[0].astype(BF16), w_sh_gu[0].astype(BF16), w_sh_down[0].astype(BF16)
    gmn2, x2 = gmn.reshape(t, gm_w), x.reshape(t, d)
    tm_post = min(1024, length)
    tr = 512

    part_nb = _part_batches(b)
    n_parts = len(part_nb)
    part_b0 = [sum(part_nb[:p]) for p in range(n_parts)]
    state = [dict() for _ in range(n_parts)]

    last_tc = [None]

    def halves(part):
        nb = part_nb[part]
        return (nb // 2, nb - nb // 2) if nb >= 2 else (nb,)

    def attend(part, half):
        if half >= len(halves(part)):
            return
        st = state[part]
        slot0 = sum(halves(part)[:half])
        after = None if last_tc[0] is st.get("att") else last_tc[0]
        st["att"] = _attention(q, k_c, k_l, v_c, v_l, st.get("att"), after, tq=min(256, length),
                               b0=part_b0[part] + slot0, nb=halves(part)[half], slot0=slot0, slots=part_nb[part])
        last_tc[0] = st["att"]

    def route(part):
        st = state[part]
        tp = part_nb[part] * length
        nt = tp * TOP_K // tr + n_exp
        base, h2p, idx_t, gate, rank_t, cnt = _post(
            st["att"].reshape(tp, att_w), gmn2, x2, mod4, out_norm_attn, w_out_b, norm_ffn, wr_t, rb, w_sg_b, w_sd_b,
            tm=tm_post, tokens_per_batch=length, tok0=part_b0[part] * length)
        counts = cnt[:, 0].astype(jnp.int32)
        padded = (counts + tr - 1) // tr * tr
        ends = jnp.cumsum(padded)
        offs = ends - padded
        offs_pad = jnp.zeros((LANES,), jnp.int32).at[:n_exp].set(offs)
        xs, pos_t = _dispatch(h2p, idx_t, rank_t, offs_pad, n_slots=nt * tr)
        st.update(base=base, gate=gate, xs=xs, pos_t=pos_t, tile_lo=offs // tr, tile_hi=ends // tr)
        last_tc[0] = base

    def run_experts(part):
        st = state[part]
        ys = _experts(st["tile_lo"], st["tile_hi"], st["xs"], w_exp_gu[0], w_exp_down[0], last_tc[0], tr=tr)
        n_chunks = part_nb[part] if part == n_parts - 1 else 1
        chunk = part_nb[part] * length // n_chunks
        st["r"] = [(c * chunk, _collect(ys, st["pos_t"], tok_lo=c * chunk, n_tok=chunk)) for c in range(n_chunks)]
        last_tc[0] = ys

    result = [None]

    def finish(part):
        st = state[part]
        for part_lo, r in st["r"]:
            after = None if last_tc[0] is result[0] or last_tc[0] is st["base"] else last_tc[0]
            result[0] = _combine(st["base"], mod4, st["gate"], r, result[0], after, tm=min(512, length),
                                 tokens_per_batch=length, tok0=part_b0[part] * length, part_lo=part_lo,
                                 total_tokens=t)
            last_tc[0] = result[0]

    stages = [(0, lambda p: attend(p, 0)), (1, lambda p: attend(p, 1)), (2, route), (4, run_experts), (7, finish)]
    plan = sorted((when + 3 * part, part, n, fn) for part in range(n_parts) for n, (when, fn) in enumerate(stages))
    for _, part, _, fn in plan:
        fn(part)
    return result[0].reshape(b, length, d)
```

```python
import dataclasses
import functools

import jax
import jax.numpy as jnp
from jax import lax
from jax.experimental import pallas as pl
from jax.experimental.pallas import tpu as pltpu
from jax.experimental.pallas import tpu_sc as plsc

F32 = jnp.float32
BF16 = jnp.bfloat16

EPS = 1e-6
GRID_W = 64
ROPE_THETA = 10000.0
ATT_KV_HEADS = 2
TOP_K = 6
ROUTED_SCALE = 2.5
N_MOD = 6
CHUNK = 128
LOG2_E = 1.4426950408889634

LANES = 128
MXU_EDGE = 256
VMEM_LIMIT_BYTES = 56 * 1024 * 1024
SUBLANES = 8
CHOICE_ROWS = SUBLANES
SC_WINDOW = 64
LEAD_PART_SHARE = 0.75
ROW_BLOCK = 512
EXPERT_IN_SLOTS = 4
EXPERT_OUT_SLOTS = 2


def _cparams(*sem):
    return pltpu.CompilerParams(dimension_semantics=sem, vmem_limit_bytes=VMEM_LIMIT_BYTES)


def _dot(a, b):
    return jnp.dot(a, b, preferred_element_type=F32)


def _dot_nt(a, b):
    return lax.dot_general(a, b, (((1,), (1,)), ((), ())), preferred_element_type=F32)


def _rms(x, g):
    return x * lax.rsqrt(jnp.mean(x * x, axis=-1, keepdims=True) + EPS) * g


def _group_mean_sq(x, bd_ref):
    x2 = (x * x).astype(BF16)
    parts = [_dot(x2[:, i:i + MXU_EDGE], bd_ref[...]) for i in range(0, x.shape[1], MXU_EDGE)]
    return parts[0] if len(parts) == 1 else jnp.concatenate(parts, axis=-1)


def _dup_halves(x, lane):
    xr = pltpu.roll(x, LANES // 2, axis=1)
    lo = lane < LANES // 2
    return jnp.where(lo, x, xr), jnp.where(lo, xr, x)


def _with_ones(x, lane):
    xr = pltpu.roll(x, LANES // 2, axis=1)
    lo = lane < LANES // 2
    return jnp.concatenate([jnp.where(lo, x, 1.0), jnp.where(lo, xr, 1.0)], axis=-1)


def _pipelined_blocks(tm, sub, head, tail):
    blocks = [pl.ds(r0, sub) for r0 in range(0, tm, sub)]
    pending = head(blocks[0])
    for n, rows in enumerate(blocks):
        ahead = head(blocks[n + 1]) if n + 1 < len(blocks) else None
        tail(rows, pending)
        pending = ahead


_HI16 = 0xFFFF0000


def _pack_rows(x):
    half = x.shape[1] // 2
    rounded = x.astype(BF16).astype(F32)
    bits = lax.bitcast_convert_type(rounded, jnp.uint32)
    word = (bits[:, :half] >> 16) | (bits[:, half:] & jnp.uint32(_HI16))
    return lax.bitcast_convert_type(word, jnp.int32)


def _unpack_rows(w):
    bits = lax.bitcast_convert_type(w, jnp.uint32)
    lo = lax.bitcast_convert_type(bits << 16, F32)
    hi = lax.bitcast_convert_type(bits & jnp.uint32(_HI16), F32)
    return lo, hi


def _ada_kernel(c_ref, w_ref, b_ref, o_ref):
    s = jax.nn.silu(c_ref[...])
    o_ref[...] = jnp.dot(s, w_ref[...], precision=lax.Precision.HIGHEST,
                         preferred_element_type=F32) + b_ref[...]


def _ada(c_all, w_ada, b_ada):
    rows, d = c_all.shape
    n = w_ada.shape[1]
    bn = 2048 if n % 2048 == 0 else 512
    return pl.pallas_call(
        _ada_kernel,
        out_shape=jax.ShapeDtypeStruct((rows, n), F32),
        grid=(n // bn,),
        in_specs=[pl.BlockSpec((rows, d), lambda j: (0, 0)),
                  pl.BlockSpec((d, bn), lambda j: (0, j)),
                  pl.BlockSpec((1, bn), lambda j: (0, j))],
        out_specs=pl.BlockSpec((rows, bn), lambda j: (0, j)),
        compiler_params=_cparams("arbitrary"),
        name="ada",
    )(c_all, w_ada, b_ada)


def _ctx_kv_kernel(x_ref, sh_ref, sc_ref, nmix_ref, w_ref, kg_ref, bd_ref, k_ref, v_ref):
    h = _rms(x_ref[...], nmix_ref[...]) * (1.0 + sc_ref[...]) + sh_ref[...]
    p = _dot(h.astype(BF16), w_ref[...])
    ms = _group_mean_sq(p, bd_ref)
    k = p[:, :LANES] * lax.rsqrt(ms[:, :LANES] + EPS) * kg_ref[...]
    v = p[:, LANES:]
    lane = lax.broadcasted_iota(jnp.int32, k.shape, 1)
    ka, kb = _dup_halves(k, lane)
    k_ref[...] = jnp.concatenate([ka, kb], axis=-1).astype(BF16)
    v_ref[...] = _with_ones(v, lane).astype(BF16)


def _ctx_kv(ctx, mod4, ctx_row, nmix, w_kv, kg, bd):
    b, n_ctx, d = ctx.shape
    kv_w = 2 * LANES
    return pl.pallas_call(
        _ctx_kv_kernel,
        out_shape=(jax.ShapeDtypeStruct((b, n_ctx, kv_w), BF16),) * 2,
        grid=(b,),
        in_specs=[pl.BlockSpec((None, n_ctx, d), lambda i: (i, 0, 0)),
                  pl.BlockSpec((None, None, 1, d), lambda i: (ctx_row, 0, 0, 0)),
                  pl.BlockSpec((None, None, 1, d), lambda i: (ctx_row, 1, 0, 0)),
                  pl.BlockSpec((1, d), lambda i: (0, 0)),
                  pl.BlockSpec(w_kv.shape, lambda i: (0, 0)),
                  pl.BlockSpec((1, LANES), lambda i: (0, 0)),
                  pl.BlockSpec(bd.shape, lambda i: (0, 0))],
        out_specs=(pl.BlockSpec((None, n_ctx, kv_w), lambda i: (i, 0, 0)),) * 2,
        compiler_params=_cparams("arbitrary"),
        name="ctx_kv",
    )(ctx, mod4, mod4, nmix, w_kv, kg, bd)


def _inproj_kernel(x_ref, sh_ref, sc_ref, nmix_ref, w_ref, cos_ref, sin_ref, qg_ref, kg_ref, gmg_ref,
                   bd_ref, ws_ref, bs_ref, og_ref, q_ref, k_ref, v_ref, gm_ref, *, att_w, gm_w):
    tm = x_ref.shape[0]
    sub = min(tm, ROW_BLOCK)

    def project(rows):
        h = _rms(x_ref[rows, :], nmix_ref[...]) * (1.0 + sc_ref[...]) + sh_ref[...]
        return _dot(h.astype(BF16), w_ref[...])

    def rope(xn, cosw, sinw, lanew):
        w = xn.shape[1]
        fwd = pltpu.roll(xn, w - 32, axis=1)
        bwd = pltpu.roll(xn, 32, axis=1)
        swapped = jnp.where((lanew & 63) < 32, fwd, bwd)
        return xn * cosw + swapped * sinw

    def finish(rows, p):
        cos1, sin1 = cos_ref[rows, :], sin_ref[rows, :]
        lane1 = lax.broadcasted_iota(jnp.int32, (sub, LANES), 1)

        q = p[:, :att_w]
        qn = q * lax.rsqrt(_group_mean_sq(q, bd_ref) + EPS) * qg_ref[...]
        reps = att_w // LANES
        cosq = jnp.concatenate([cos1] * reps, axis=-1)
        sinq = jnp.concatenate([sin1] * reps, axis=-1)
        laneq = lax.broadcasted_iota(jnp.int32, (sub, att_w), 1)
        q_ref[rows, :] = rope(qn, cosq, sinq, laneq).astype(BF16)

        kv = p[:, att_w:att_w + 2 * LANES]
        ms = _group_mean_sq(kv, bd_ref)
        kn = kv[:, :LANES] * lax.rsqrt(ms[:, :LANES] + EPS) * kg_ref[...]
        k = rope(kn, cos1, sin1, lane1)
        v = kv[:, LANES:]
        ka, kb = _dup_halves(k, lane1)
        k_ref[rows, :] = jnp.concatenate([ka, kb], axis=-1).astype(BF16)
        v_ref[rows, :] = _with_ones(v, lane1).astype(BF16)

        u0 = att_w + 2 * LANES
        u = jax.nn.gelu(p[:, u0:u0 + gm_w])
        gt = jax.nn.gelu(p[:, u0 + gm_w:u0 + 2 * gm_w])
        gt = (gt * lax.rsqrt(_group_mean_sq(gt, bd_ref) + EPS) * gmg_ref[...]).astype(BF16)
        lo_half = lax.broadcasted_iota(jnp.int32, (CHUNK, LANES), 1) < LANES // 2
        mixed_rows = []
        for c in range(sub // CHUNK):
            blocks = []
            for j in range(gm_w // LANES):
                g = gt[c * CHUNK:(c + 1) * CHUNK, j * LANES:(j + 1) * LANES]
                r = _dot(ws_ref[j], g)
                blocks.append(jnp.where(lo_half, r[:CHUNK], r[CHUNK:]))
            mixed_rows.append(jnp.concatenate(blocks, axis=-1) + bs_ref[...])
        mixed = jnp.concatenate(mixed_rows, axis=0)
        gm_ref[rows, :] = _rms(u * mixed, og_ref[...]).astype(BF16)

    _pipelined_blocks(tm, sub, project, finish)


def _inproj(x, mod4, nmix, w_in, cos_t, sin_t, qg, kg, gmg, bd, ws, bs, og, *, att_w, gm_w, tm):
    b, length, d = x.shape
    kv_w = 2 * LANES
    n_in = w_in.shape[1]
    const = lambda shape: pl.BlockSpec(shape, lambda i, j: (0,) * len(shape))
    return pl.pallas_call(
        functools.partial(_inproj_kernel, att_w=att_w, gm_w=gm_w),
        out_shape=(jax.ShapeDtypeStruct((b, length, att_w), BF16),
                   jax.ShapeDtypeStruct((b, length, kv_w), BF16),
                   jax.ShapeDtypeStruct((b, length, kv_w), BF16),
                   jax.ShapeDtypeStruct((b, length, gm_w), BF16)),
        grid=(b, length // tm),
        in_specs=[pl.BlockSpec((None, tm, d), lambda i, j: (i, j, 0)),
                  pl.BlockSpec((None, None, 1, d), lambda i, j: (i, 0, 0, 0)),
                  pl.BlockSpec((None, None, 1, d), lambda i, j: (i, 1, 0, 0)),
                  const((1, d)),
                  const((d, n_in)),
                  pl.BlockSpec((tm, LANES), lambda i, j: (j, 0)),
                  pl.BlockSpec((tm, LANES), lambda i, j: (j, 0)),
                  const((1, att_w)), const((1, LANES)), const((1, gm_w)),
                  const(bd.shape), const(ws.shape), const(bs.shape), const((1, gm_w))],
        out_specs=(pl.BlockSpec((None, tm, att_w), lambda i, j: (i, j, 0)),
                   pl.BlockSpec((None, tm, kv_w), lambda i, j: (i, j, 0)),
                   pl.BlockSpec((None, tm, kv_w), lambda i, j: (i, j, 0)),
                   pl.BlockSpec((None, tm, gm_w), lambda i, j: (i, j, 0))),
        compiler_params=_cparams("arbitrary", "arbitrary"),
        name="inproj",
    )(x, mod4, mod4, nmix, w_in, cos_t, sin_t, qg, kg, gmg, bd, ws, bs, og)


def _attn_kernel(q_ref, kc_ref, kl_ref, vc_ref, vl_ref, o_ref, s_buf, p_buf, *, tq):
    n_ctx = kc_ref.shape[0]
    heads = q_ref.shape[1] // (LANES // 2)
    nj = q_ref.shape[0] // tq
    half = LANES // 2
    lo = lax.broadcasted_iota(jnp.int32, (tq, LANES), 1) < half

    @pl.when(jnp.logical_and(pl.program_id(0) == 0, pl.program_id(1) == 0))
    def _():
        s_buf[...] = jnp.zeros_like(s_buf)
        p_buf[...] = jnp.ones_like(p_buf)

    def scores(j, g):
        q = q_ref[pl.ds(j * tq, tq), (g // 2) * LANES:(g // 2 + 1) * LANES]
        zero = jnp.zeros_like(q)
        qh = jnp.where(lo, q, zero) if g % 2 == 0 else jnp.where(lo, zero, q)
        s_buf[g % 2, :, :n_ctx] = _dot_nt(qh, kc_ref[...])
        s_buf[g % 2, :, n_ctx:] = _dot_nt(qh, kl_ref[...])

    def probs(g):
        s = s_buf[g % 2]
        p_buf[g % 2] = jnp.exp2(s - jnp.max(s, axis=-1, keepdims=True)).astype(BF16)

    def output(j, g):
        p = p_buf[g % 2]
        w = _dot(p[:, :n_ctx], vc_ref[...]) + _dot(p[:, n_ctx:], vl_ref[...])
        t = jnp.where(pl.program_id(1) == 0, w[:, :LANES], w[:, LANES:])
        r = pltpu.roll(t, half, axis=1)
        rows = pl.ds(j * tq, tq)
        if g % 2 == 0:
            o_ref[rows, g * half:(g + 1) * half] = (t / r)[:, :half].astype(o_ref.dtype)
        else:
            o_ref[rows, g * half:(g + 1) * half] = (r / t)[:, half:].astype(o_ref.dtype)

    def tile_stages(j):
        j_prev = jnp.maximum(j - 1, 0)
        for g in range(heads):
            scores(j, g)
            probs((g - 1) % heads)
            output(j if g >= 2 else j_prev, (g - 2) % heads)

    def trip(jj, carry):
        for u in range(tiles_per_trip):
            tile_stages(jj * tiles_per_trip + u)
        return carry

    tiles_per_trip = 2 if nj % 2 == 0 else 1
    lax.fori_loop(0, nj // tiles_per_trip, trip, 0)
    probs(heads - 1)
    output(nj - 1, heads - 2)
    output(nj - 1, heads - 1)


def _unread_operand(in_specs, operands, array):
    in_specs.append(pl.BlockSpec(memory_space=pl.ANY))
    operands.append(array)
    return len(operands) - 1


def _attention(q, k_c, k_l, v_c, v_l, att_so_far, after, *, tq, b0, nb, slot0, slots):
    _, length, att_w = q.shape
    n_ctx = k_c.shape[1]
    group_w = att_w // ATT_KV_HEADS
    assert (group_w // (LANES // 2)) % 2 == 0 and length % tq == 0
    k_blk = lambda n: pl.BlockSpec((None, n, LANES), lambda i, h: (i + b0, 0, h))
    v_blk = lambda n: pl.BlockSpec((None, n, ATT_KV_HEADS * LANES), lambda i, h: (i + b0, 0, 0))
    in_specs = [pl.BlockSpec((None, length, group_w), lambda i, h: (i + b0, 0, h)),
                k_blk(n_ctx), k_blk(length), v_blk(n_ctx), v_blk(length)]
    operands = [q, k_c, k_l, v_c, v_l]
    n_read = len(operands)
    aliases = {}
    if att_so_far is not None:
        aliases = {_unread_operand(in_specs, operands, att_so_far): 0}
    if after is not None:
        _unread_operand(in_specs, operands, after)
    n_in = len(operands)
    return pl.pallas_call(
        lambda *refs: _attn_kernel(*refs[:n_read], *refs[n_in:], tq=tq),
        out_shape=jax.ShapeDtypeStruct((slots, length, att_w), BF16),
        grid=(nb, ATT_KV_HEADS),
        in_specs=in_specs,
        out_specs=pl.BlockSpec((None, length, group_w), lambda i, h: (i + slot0, 0, h)),
        scratch_shapes=[pltpu.VMEM((2, tq, n_ctx + length), F32), pltpu.VMEM((2, tq, n_ctx + length), BF16)],
        input_output_aliases=aliases,
        compiler_params=_cparams("arbitrary", "arbitrary"),
        name="attn",
    )(*operands)


def _post_kernel(att_ref, gm_ref, x_ref, g1_ref, sh2_ref, sc2_ref, g2_ref, oa_ref, wo_ref, nffn_ref,
                 wrt_ref, rb_ref, wsg_ref, wsd_ref,
                 base_ref, h2p_ref, idx_ref, gate_ref, rank_ref, cnt_ref, carry_ref):
    i = pl.program_id(0)
    tm = x_ref.shape[0]
    n_exp = wrt_ref.shape[0]

    @pl.when(i == 0)
    def _():
        carry_ref[...] = jnp.zeros_like(carry_ref)

    sub = min(tm, ROW_BLOCK)
    reps = sub // LANES
    neg = jnp.float32(-jnp.inf)
    r_io = lax.broadcasted_iota(jnp.int32, (sub, sub), 0)
    c_io = lax.broadcasted_iota(jnp.int32, (sub, sub), 1)
    tri = jnp.where(r_io < c_io, 1.0, 0.0).astype(BF16)

    def residual(rows):
        att_n = _rms(att_ref[rows, :].astype(F32), oa_ref[...]).astype(BF16)
        y = _dot(jnp.concatenate([att_n, gm_ref[rows, :]], axis=-1), wo_ref[...])
        x_new = x_ref[rows, :] + g1_ref[...] * y
        h2 = _rms(x_new, nffn_ref[...]) * (1.0 + sc2_ref[...]) + sh2_ref[...]
        h2p_ref[rows, :] = _pack_rows(h2)
        return x_new, h2

    def route_and_share(rows, carried):
        x_new, h2 = carried
        h_hi = h2.astype(BF16)
        scores_t = jax.nn.sigmoid(_dot_nt(wrt_ref[...], h_hi))
        sel = scores_t + jnp.concatenate([rb_ref[...]] * reps, axis=-1)
        expert = lax.broadcasted_iota(jnp.int32, scores_t.shape, 0).astype(F32)

        onehot = jnp.zeros(scores_t.shape, F32)
        hits, idxs, svals = [], [], []
        for _k in range(TOP_K):
            m = jnp.max(sel, axis=0, keepdims=True)
            idx = jnp.min(jnp.where(sel == m, expert, float(n_exp)), axis=0, keepdims=True)
            hit = expert == idx
            hits.append(hit)
            idxs.append(idx)
            svals.append(jnp.sum(jnp.where(hit, scores_t, 0.0), axis=0, keepdims=True))
            onehot = onehot + jnp.where(hit, 1.0, 0.0)
            sel = jnp.where(hit, neg, sel)
        ssum = functools.reduce(lambda a, b: a + b, svals)

        carry = carry_ref[...]
        before = _dot(onehot.astype(BF16), tri) + jnp.concatenate([carry] * reps, axis=-1)
        carry = carry + jnp.sum(onehot, axis=1, keepdims=True)
        carry_ref[...] = carry
        cnt_ref[...] = carry

        row = lax.broadcasted_iota(jnp.int32, (idx_ref.shape[0], sub), 0)
        idx_o = jnp.zeros(row.shape, F32)
        rank_o = jnp.zeros(row.shape, F32)
        gate_o = jnp.zeros(row.shape, F32)
        for k in range(TOP_K):
            rank = jnp.sum(jnp.where(hits[k], before, 0.0), axis=0, keepdims=True)
            idx_o = jnp.where(row == k, idxs[k], idx_o)
            rank_o = jnp.where(row == k, rank, rank_o)
            gate_o = jnp.where(row == k, svals[k] / ssum * ROUTED_SCALE, gate_o)
        idx_ref[:, rows] = idx_o.astype(jnp.int32)
        rank_ref[:, rows] = rank_o.astype(jnp.int32)
        gate_pad = jnp.concatenate([gate_o, jnp.zeros((LANES - gate_o.shape[0], sub), F32)], axis=0)
        gate_ref[rows, :] = gate_pad.T

        gu = _dot(h_hi, wsg_ref[...])
        hs = gu.shape[1] // 2
        act = (jax.nn.silu(gu[:, :hs]) * gu[:, hs:]).astype(BF16)
        base_ref[rows, :] = x_new + g2_ref[...] * _dot(act, wsd_ref[...])

    _pipelined_blocks(tm, sub, residual, route_and_share)


def _post(att, gmn, x2, mod4, oa, w_out, nffn, wr_t, rb, w_sg, w_sd, *, tm, tokens_per_batch, tok0):
    t, att_w = att.shape
    d = x2.shape[1]
    n_exp = wr_t.shape[0]
    gm_w = gmn.shape[1]
    tpb = tokens_per_batch // tm
    blk0 = tok0 // tm
    const = lambda shape: pl.BlockSpec(shape, lambda i: (0,) * len(shape))
    mrow = lambda j: pl.BlockSpec((None, None, 1, d), lambda i: ((i + blk0) // tpb, j, 0, 0))
    by_choice = jax.ShapeDtypeStruct((CHOICE_ROWS, t), jnp.int32)
    return pl.pallas_call(
        _post_kernel,
        out_shape=(jax.ShapeDtypeStruct((t, d), F32), jax.ShapeDtypeStruct((t, d // 2), jnp.int32),
                   by_choice, jax.ShapeDtypeStruct((t, LANES), F32), by_choice,
                   jax.ShapeDtypeStruct((n_exp, LANES), F32)),
        grid=(t // tm,),
        in_specs=[pl.BlockSpec((tm, att_w), lambda i: (i, 0)),
                  pl.BlockSpec((tm, gm_w), lambda i: (i + blk0, 0)),
                  pl.BlockSpec((tm, d), lambda i: (i + blk0, 0)),
                  mrow(2), mrow(3), mrow(4), mrow(5),
                  const((1, att_w)), const(w_out.shape), const((1, d)),
                  const(wr_t.shape), const(rb.shape),
                  const(w_sg.shape), const(w_sd.shape)],
        out_specs=(pl.BlockSpec((tm, d), lambda i: (i, 0)), pl.BlockSpec((tm, d // 2), lambda i: (i, 0)),
                   pl.BlockSpec((CHOICE_ROWS, tm), lambda i: (0, i)), pl.BlockSpec((tm, LANES), lambda i: (i, 0)),
                   pl.BlockSpec((CHOICE_ROWS, tm), lambda i: (0, i)), pl.BlockSpec((n_exp, LANES), lambda i: (0, 0))),
        scratch_shapes=[pltpu.VMEM((n_exp, LANES), F32)],
        compiler_params=_cparams("arbitrary"),
        name="post",
    )(att, gmn, x2, mod4, mod4, mod4, mod4, oa, w_out, nffn, wr_t, rb, w_sg, w_sd)


def _sc_mesh_info():
    info = plsc.get_sparse_core_info()
    mesh = plsc.VectorSubcoreMesh(core_axis_name="c", subcore_axis_name="s")
    return mesh, info.num_cores, info.num_subcores, info.num_lanes


def _sc_params():
    cp = pltpu.CompilerParams()
    if "needs_layout_passes" in pltpu.CompilerParams.__dataclass_fields__:
        cp = dataclasses.replace(cp, needs_layout_passes=False)
    return cp


def _dispatch(h2p, idx_t, rank_t, offs, *, n_slots):
    t, dw = h2p.shape
    mesh, n_cores, n_sub, n_lanes = _sc_mesh_info()
    per_worker = t // (n_cores * n_sub)
    w = SC_WINDOW
    n_win = per_worker // w
    assert per_worker % w == 0 and w % n_lanes == 0 and n_win % 2 == 0

    @functools.partial(
        pl.kernel, mesh=mesh,
        out_type=(jax.ShapeDtypeStruct((n_slots, dw), jnp.int32), jax.ShapeDtypeStruct((CHOICE_ROWS, t), jnp.int32)),
        scratch_types=[pltpu.VMEM(offs.shape, jnp.int32)] + [pltpu.VMEM((TOP_K, w), jnp.int32)] * 5
        + [pltpu.VMEM((w, dw), jnp.int32)] * 2 + [pltpu.SemaphoreType.DMA] * 2,
        compiler_params=_sc_params(),
    )
    def run(h_hbm, idx_hbm, rank_hbm, offs_hbm, xs_hbm, pos_hbm, offs_v, idx_a, idx_b, rank_a, rank_b, pos_v,
            rows_a, rows_b, sem_in, sem_out):
        base = (lax.axis_index("s") * n_cores + lax.axis_index("c")) * per_worker
        last = base + per_worker - w
        sets = ((rows_a, idx_a, rank_a), (rows_b, idx_b, rank_b))
        pltpu.sync_copy(offs_hbm, offs_v)

        def loads(t0, slot):
            rows_v, idx_v, rank_v = sets[slot]
            cps = [pltpu.make_async_copy(h_hbm.at[pl.ds(t0, w)], rows_v, sem_in)]
            for k in range(TOP_K):
                cps.append(pltpu.make_async_copy(idx_hbm.at[k, pl.ds(t0, w)], idx_v.at[k], sem_in))
                cps.append(pltpu.make_async_copy(rank_hbm.at[k, pl.ds(t0, w)], rank_v.at[k], sem_in))
            return cps

        for cp in loads(base, 0):
            cp.start()

        @pl.loop(0, n_win // 2)
        def _(i):
            for slot in range(2):
                t0 = base + (2 * i + slot) * w
                rows_v, idx_v, rank_v = sets[slot]
                for cp in loads(t0, slot):
                    cp.wait()
                for cp in loads(jnp.minimum(t0 + w, last), 1 - slot):
                    cp.start()
                for k in range(TOP_K):
                    for j in range(0, w, n_lanes):
                        group_start = plsc.load_gather(offs_v, [idx_v[k, pl.ds(j, n_lanes)]])
                        pos_v[k, pl.ds(j, n_lanes)] = group_start + rank_v[k, pl.ds(j, n_lanes)]
                copies = [pltpu.async_copy(rows_v, xs_hbm.at[pos_v.at[k]], sem_out) for k in range(TOP_K)]
                copies += [pltpu.async_copy(pos_v.at[k], pos_hbm.at[k, pl.ds(t0, w)], sem_out) for k in range(TOP_K)]
                for cp in copies:
                    cp.wait()

        for cp in loads(last, 0):
            cp.wait()

    return run(h2p, idx_t, rank_t, offs)


def _collect(ys, pos_t, *, tok_lo, n_tok):
    dw = ys.shape[1]
    mesh, n_cores, n_sub, _ = _sc_mesh_info()
    per_worker = n_tok // (n_cores * n_sub)
    w = SC_WINDOW
    assert per_worker % w == 0 and TOP_K % 2 == 0

    @functools.partial(
        pl.kernel, mesh=mesh,
        out_type=jax.ShapeDtypeStruct((TOP_K * n_tok, dw), jnp.int32),
        scratch_types=[pltpu.VMEM((TOP_K, w), jnp.int32), pltpu.VMEM((w, dw), jnp.int32),
                       pltpu.VMEM((w, dw), jnp.int32), pltpu.SemaphoreType.DMA, pltpu.SemaphoreType.DMA],
        compiler_params=_sc_params(),
    )
    def run(ys_hbm, pos_hbm, r_hbm, pos_v, rows_a, rows_b, sem_g, sem_w):
        base = (lax.axis_index("s") * n_cores + lax.axis_index("c")) * per_worker
        bufs = (rows_a, rows_b)

        @pl.loop(0, per_worker // w)
        def _(i):
            t0 = base + i * w
            for k in range(TOP_K):
                pltpu.sync_copy(pos_hbm.at[k, pl.ds(tok_lo + t0, w)], pos_v.at[k])
            gathers = [None] * TOP_K
            writes = [None] * TOP_K
            gathers[0] = pltpu.async_copy(ys_hbm.at[pos_v.at[0]], bufs[0], sem_g)
            for k in range(TOP_K):
                gathers[k].wait()
                if k + 1 < TOP_K:
                    if k >= 1:
                        writes[k - 1].wait()
                    gathers[k + 1] = pltpu.async_copy(ys_hbm.at[pos_v.at[k + 1]], bufs[(k + 1) % 2], sem_g)
                writes[k] = pltpu.async_copy(bufs[k % 2], r_hbm.at[pl.ds(k * n_tok + t0, w)], sem_w)
            writes[TOP_K - 2].wait()
            writes[TOP_K - 1].wait()

    return run(ys, pos_t)


def _experts_kernel(t0_ref, t1_ref, xs_hbm, wgu_ref, wdn_ref, ys_hbm, xbuf, ybuf, wgu_b, wdn_b, sem_in, sem_out):
    e = pl.program_id(0)
    n_exp = pl.num_programs(0)
    n_in, tr, half = xbuf.shape
    n_out = ybuf.shape[0]
    n_tiles = t1_ref[n_exp - 1]

    def fetch(t):
        slot = t & (n_in - 1)
        return pltpu.make_async_copy(xs_hbm.at[pl.ds(t * tr, tr)], xbuf.at[slot], sem_in.at[slot])

    def flush(t):
        slot = t & (n_out - 1)
        return pltpu.make_async_copy(ybuf.at[slot], ys_hbm.at[pl.ds(t * tr, tr)], sem_out.at[slot])

    @pl.when(e == 0)
    def _():
        for t in range(n_in - 1):
            @pl.when(t < n_tiles)
            def _():
                fetch(t).start()

    wgu_b[...] = wgu_ref[...].astype(BF16)
    wdn_b[...] = wdn_ref[...].astype(BF16)

    def tile(t, carry):
        fetch(t).wait()

        @pl.when(t + n_in - 1 < n_tiles)
        def _():
            fetch(t + n_in - 1).start()

        @pl.when(t >= n_out)
        def _():
            flush(t - n_out).wait()

        x_slot = xbuf.at[t & (n_in - 1)]
        y_slot = ybuf.at[t & (n_out - 1)]

        def up(rows):
            lo, hi = _unpack_rows(x_slot[rows, :])
            gu = _dot(lo.astype(BF16), wgu_b[:half, :]) + _dot(hi.astype(BF16), wgu_b[half:, :])
            hh = gu.shape[1] // 2
            return (jax.nn.silu(gu[:, :hh]) * gu[:, hh:]).astype(BF16)

        def down(rows, act):
            y_slot[rows, :] = _pack_rows(_dot(act, wdn_b[...]))

        _pipelined_blocks(tr, min(tr, ROW_BLOCK // 2), up, down)
        flush(t).start()
        return carry

    lax.fori_loop(t0_ref[e], t1_ref[e], tile, 0)

    @pl.when(e == n_exp - 1)
    def _():
        for back in range(n_out, 0, -1):
            @pl.when(n_tiles >= back)
            def _():
                flush(n_tiles - back).wait()


def _experts(tile_lo, tile_hi, xs, w_gu, w_dn, after, *, tr):
    n_exp, d, h2w = w_gu.shape
    hh = w_dn.shape[1]
    in_specs = [pl.BlockSpec(memory_space=pl.ANY),
                pl.BlockSpec((None, d, h2w), lambda e, lo, hi: (e, 0, 0)),
                pl.BlockSpec((None, hh, d), lambda e, lo, hi: (e, 0, 0))]
    operands = [xs, w_gu, w_dn]
    n_read = 2 + len(operands)
    if after is not None:
        _unread_operand(in_specs, operands, after)
    n_in = 2 + len(operands)
    return pl.pallas_call(
        lambda *refs: _experts_kernel(*refs[:n_read], *refs[n_in:]),
        out_shape=jax.ShapeDtypeStruct(xs.shape, jnp.int32),
        grid_spec=pltpu.PrefetchScalarGridSpec(
            num_scalar_prefetch=2,
            grid=(n_exp,),
            in_specs=in_specs,
            out_specs=pl.BlockSpec(memory_space=pl.ANY),
            scratch_shapes=[pltpu.VMEM((EXPERT_IN_SLOTS, tr, d // 2), jnp.int32),
                            pltpu.VMEM((EXPERT_OUT_SLOTS, tr, d // 2), jnp.int32),
                            pltpu.VMEM((d, h2w), BF16), pltpu.VMEM((hh, d), BF16),
                            pltpu.SemaphoreType.DMA((EXPERT_IN_SLOTS,)), pltpu.SemaphoreType.DMA((EXPERT_OUT_SLOTS,))]),
        compiler_params=_cparams("arbitrary"),
        name="experts",
    )(tile_lo, tile_hi, *operands)


def _combine_kernel(base_ref, g2_ref, gate_ref, *refs):
    r_refs, o_ref = refs[:TOP_K], refs[TOP_K]
    gate = gate_ref[...]
    half = base_ref.shape[1] // 2
    acc_lo = jnp.zeros((base_ref.shape[0], half), F32)
    acc_hi = jnp.zeros((base_ref.shape[0], half), F32)
    for k in range(TOP_K):
        lo, hi = _unpack_rows(r_refs[k][...])
        g = gate[:, k:k + 1]
        acc_lo = acc_lo + g * lo
        acc_hi = acc_hi + g * hi
    g2 = g2_ref[...]
    o_ref[:, :half] = base_ref[:, :half] + g2[:, :half] * acc_lo
    o_ref[:, half:] = base_ref[:, half:] + g2[:, half:] * acc_hi


def _combine(base, mod4, gate, r, out_so_far, after, *, tm, tokens_per_batch, tok0, part_lo, total_tokens):
    d = base.shape[1]
    t = r.shape[0] // TOP_K
    tpb = tokens_per_batch // tm
    nblk = t // tm
    pblk0 = part_lo // tm
    blk0 = (tok0 + part_lo) // tm
    r_spec = lambda k: pl.BlockSpec((tm, d // 2), lambda i: (k * nblk + i, 0))
    in_specs = [pl.BlockSpec((tm, d), lambda i: (i + pblk0, 0)),
                pl.BlockSpec((None, None, 1, d), lambda i: ((i + blk0) // tpb, 5, 0, 0)),
                pl.BlockSpec((tm, LANES), lambda i: (i + pblk0, 0))] + [r_spec(k) for k in range(TOP_K)]
    operands = [base, mod4, gate] + [r] * TOP_K
    n_read = len(operands)
    aliases = {}
    if out_so_far is not None:
        aliases = {_unread_operand(in_specs, operands, out_so_far): 0}
    if after is not None:
        _unread_operand(in_specs, operands, after)
    n_in = len(operands)
    return pl.pallas_call(
        lambda *refs: _combine_kernel(*refs[:n_read], *refs[n_in:]),
        out_shape=jax.ShapeDtypeStruct((total_tokens, d), F32),
        grid=(nblk,),
        in_specs=in_specs,
        out_specs=pl.BlockSpec((tm, d), lambda i: (i + blk0, 0)),
        input_output_aliases=aliases,
        compiler_params=_cparams("arbitrary"),
        name="combine",
    )(*operands)


def _rope_tables(length, head_dim):
    rows = length // GRID_W
    pairs_axis = head_dim // 4
    r = jnp.repeat(jnp.arange(rows, dtype=F32), GRID_W)
    col = jnp.tile(jnp.arange(GRID_W, dtype=F32), rows)
    inv = ROPE_THETA ** (-jnp.arange(pairs_axis, dtype=F32) / pairs_axis)
    ang = jnp.concatenate([r[:, None] * inv, col[:, None] * inv], axis=-1)
    cos, sin = jnp.cos(ang), jnp.sin(ang)
    cos_h = jnp.concatenate([cos, cos], axis=-1)
    sin_h = jnp.concatenate([-sin, sin], axis=-1)
    reps = LANES // head_dim
    return jnp.tile(cos_h, (1, reps)), jnp.tile(sin_h, (1, reps))


def _part_batches(b):
    if b < 2:
        return [b]
    lead = max(1, min(b - 1, round(LEAD_PART_SHARE * b)))
    return [lead, b - lead]


def kernel(x, c, ctx, c_ctx, w_ada, b_ada, norm_mix, w_in, q_norm, k_norm, gm_norm, w_spatial, b_spatial,
           out_norm_attn, out_norm_gm, w_out, norm_ffn, w_router, router_bias, w_exp_gu, w_exp_down,
           w_sh_gu, w_sh_down):
    assert w_ada.shape[0] == 1, "single-layer kernel"
    b, length, d = x.shape
    head_dim = q_norm.shape[-1]
    att_w = out_norm_attn.shape[-1]
    gm_w = out_norm_gm.shape[-1]
    gm_groups = w_spatial.shape[1]
    gm_hd = gm_w // gm_groups
    n_exp = w_router.shape[-1]
    t = b * length
    assert head_dim == LANES // 2 and gm_hd == LANES // 2 and w_spatial.shape[-1] == CHUNK
    assert n_exp <= LANES and att_w % MXU_EDGE == 0 and gm_w % MXU_EDGE == 0

    mod_rows = 16
    c_all = jnp.zeros((mod_rows, d), F32).at[:b].set(c).at[b].set(c_ctx)
    mod = _ada(c_all, w_ada[0], b_ada)
    mod4 = mod.reshape(mod_rows, N_MOD, 1, d)

    w_in_b = w_in[0].astype(BF16)
    bd = (jnp.arange(MXU_EDGE)[:, None] // head_dim == jnp.arange(MXU_EDGE)[None, :] // head_dim)
    bd = (bd.astype(F32) / head_dim).astype(BF16)
    qg = jnp.tile(q_norm[0] * (head_dim ** -0.5 * LOG2_E), att_w // head_dim)[None, :]
    kg = jnp.tile(k_norm[0], LANES // head_dim)[None, :]
    gmg = jnp.tile(gm_norm[0], gm_groups)[None, :]
    ws = w_spatial[0].astype(BF16).reshape(gm_groups // 2, 2 * CHUNK, CHUNK)
    bs = jnp.repeat(b_spatial[0].T, gm_hd, axis=1)
    cos_t, sin_t = _rope_tables(length, head_dim)

    k_c, v_c = _ctx_kv(ctx, mod4, b, norm_mix, w_in_b[:, att_w:att_w + 2 * LANES], kg, bd)
    q, k_l, v_l, gmn = _inproj(x, mod4, norm_mix, w_in_b, cos_t, sin_t, qg, kg, gmg, bd, ws, bs,
                               out_norm_gm, att_w=att_w, gm_w=gm_w, tm=min(1024, length))
    wr_t = w_router[0].T.astype(BF16)
    rb = jnp.tile(router_bias[0][:, None], (1, LANES))
    w_out_b, w_sg_b, w_sd_b = w_out[0].astype(BF16), w_sh_gu[0].astype(BF16), w_sh_down[0].astype(BF16)
    gmn2, x2 = gmn.reshape(t, gm_w), x.reshape(t, d)
    tm_post = min(1024, length)
    tr = 512

    part_nb = _part_batches(b)
    n_parts = len(part_nb)
    part_b0 = [sum(part_nb[:p]) for p in range(n_parts)]
    state = [dict() for _ in range(n_parts)]

    last_tc = [None]

    def halves(part):
        nb = part_nb[part]
        return (nb // 2, nb - nb // 2) if nb >= 2 else (nb,)

    def attend(part, half):
        if half >= len(halves(part)):
            return
        st = state[part]
        slot0 = sum(halves(part)[:half])
        after = None if last_tc[0] is st.get("att") else last_tc[0]
        st["att"] = _attention(q, k_c, k_l, v_c, v_l, st.get("att"), after, tq=min(256, length),
                               b0=part_b0[part] + slot0, nb=halves(part)[half], slot0=slot0, slots=part_nb[part])
        last_tc[0] = st["att"]

    def route(part):
        st = state[part]
        tp = part_nb[part] * length
        nt = tp * TOP_K // tr + n_exp
        base, h2p, idx_t, gate, rank_t, cnt = _post(
            st["att"].reshape(tp, att_w), gmn2, x2, mod4, out_norm_attn, w_out_b, norm_ffn, wr_t, rb, w_sg_b, w_sd_b,
            tm=tm_post, tokens_per_batch=length, tok0=part_b0[part] * length)
        counts = cnt[:, 0].astype(jnp.int32)
        padded = (counts + tr - 1) // tr * tr
        ends = jnp.cumsum(padded)
        offs = ends - padded
        offs_pad = jnp.zeros((LANES,), jnp.int32).at[:n_exp].set(offs)
        xs, pos_t = _dispatch(h2p, idx_t, rank_t, offs_pad, n_slots=nt * tr)
        st.update(base=base, gate=gate, xs=xs, pos_t=pos_t, tile_lo=offs // tr, tile_hi=ends // tr)
        last_tc[0] = base

    def run_experts(part):
        st = state[part]
        ys = _experts(st["tile_lo"], st["tile_hi"], st["xs"], w_exp_gu[0], w_exp_down[0], last_tc[0], tr=tr)
        n_chunks = part_nb[part] if part == n_parts - 1 else 1
        chunk = part_nb[part] * length // n_chunks
        st["r"] = [(c * chunk, _collect(ys, st["pos_t"], tok_lo=c * chunk, n_tok=chunk)) for c in range(n_chunks)]
        last_tc[0] = ys

    result = [None]

    def finish(part):
        st = state[part]
        for part_lo, r in st["r"]:
            after = None if last_tc[0] is result[0] or last_tc[0] is st["base"] else last_tc[0]
            result[0] = _combine(st["base"], mod4, st["gate"], r, result[0], after, tm=min(512, length),
                                 tokens_per_batch=length, tok0=part_b0[part] * length, part_lo=part_lo,
                                 total_tokens=t)
            last_tc[0] = result[0]

    stages = [(0, lambda p: attend(p, 0)), (1, lambda p: attend(p, 1)), (2, route), (4, run_experts), (7, finish)]
    plan = sorted((when + 3 * part, part, n, fn) for part in range(n_parts) for n, (when, fn) in enumerate(stages))
    for _, part, _, fn in plan:
        fn(part)
    return result[0].reshape(b, length, d)
```

```python
import dataclasses
import functools

import jax
import jax.numpy as jnp
from jax import lax
from jax.experimental import pallas as pl
from jax.experimental.pallas import tpu as pltpu
from jax.experimental.pallas import tpu_sc as plsc

F32 = jnp.float32
BF16 = jnp.bfloat16

EPS = 1e-6
GRID_W = 64
ROPE_THETA = 10000.0
ATT_KV_HEADS = 2
TOP_K = 6
ROUTED_SCALE = 2.5
N_MOD = 6
CHUNK = 128
LOG2_E = 1.4426950408889634

LANES = 128
MXU_EDGE = 256
VMEM_LIMIT_BYTES = 56 * 1024 * 1024
SUBLANES = 8
CHOICE_ROWS = SUBLANES
SC_WINDOW = 64
LEAD_PART_SHARE = 0.75
ROW_BLOCK = 512
EXPERT_IN_SLOTS = 4
EXPERT_OUT_SLOTS = 2


def _cparams(*sem):
    return pltpu.CompilerParams(dimension_semantics=sem, vmem_limit_bytes=VMEM_LIMIT_BYTES)


def _dot(a, b):
    return jnp.dot(a, b, preferred_element_type=F32)


def _dot_nt(a, b):
    return lax.dot_general(a, b, (((1,), (1,)), ((), ())), preferred_element_type=F32)


def _rms(x, g):
    return x * lax.rsqrt(jnp.mean(x * x, axis=-1, keepdims=True) + EPS) * g


def _group_mean_sq(x, bd_ref):
    x2 = (x * x).astype(BF16)
    parts = [_dot(x2[:, i:i + MXU_EDGE], bd_ref[...]) for i in range(0, x.shape[1], MXU_EDGE)]
    return parts[0] if len(parts) == 1 else jnp.concatenate(parts, axis=-1)


def _dup_halves(x, lane):
    xr = pltpu.roll(x, LANES // 2, axis=1)
    lo = lane < LANES // 2
    return jnp.where(lo, x, xr), jnp.where(lo, xr, x)


def _with_ones(x, lane):
    xr = pltpu.roll(x, LANES // 2, axis=1)
    lo = lane < LANES // 2
    return jnp.concatenate([jnp.where(lo, x, 1.0), jnp.where(lo, xr, 1.0)], axis=-1)


def _pipelined_blocks(tm, sub, head, tail):
    blocks = [pl.ds(r0, sub) for r0 in range(0, tm, sub)]
    pending = head(blocks[0])
    for n, rows in enumerate(blocks):
        ahead = head(blocks[n + 1]) if n + 1 < len(blocks) else None
        tail(rows, pending)
        pending = ahead


_HI16 = 0xFFFF0000


def _pack_rows(x):
    half = x.shape[1] // 2
    rounded = x.astype(BF16).astype(F32)
    bits = lax.bitcast_convert_type(rounded, jnp.uint32)
    word = (bits[:, :half] >> 16) | (bits[:, half:] & jnp.uint32(_HI16))
    return lax.bitcast_convert_type(word, jnp.int32)


def _unpack_rows(w):
    bits = lax.bitcast_convert_type(w, jnp.uint32)
    lo = lax.bitcast_convert_type(bits << 16, F32)
    hi = lax.bitcast_convert_type(bits & jnp.uint32(_HI16), F32)
    return lo, hi


def _ada_kernel(c_ref, w_ref, b_ref, o_ref):
    s = jax.nn.silu(c_ref[...])
    o_ref[...] = jnp.dot(s, w_ref[...], precision=lax.Precision.HIGHEST,
                         preferred_element_type=F32) + b_ref[...]


def _ada(c_all, w_ada, b_ada):
    rows, d = c_all.shape
    n = w_ada.shape[1]
    bn = 2048 if n % 2048 == 0 else 512
    return pl.pallas_call(
        _ada_kernel,
        out_shape=jax.ShapeDtypeStruct((rows, n), F32),
        grid=(n // bn,),
        in_specs=[pl.BlockSpec((rows, d), lambda j: (0, 0)),
                  pl.BlockSpec((d, bn), lambda j: (0, j)),
                  pl.BlockSpec((1, bn), lambda j: (0, j))],
        out_specs=pl.BlockSpec((rows, bn), lambda j: (0, j)),
        compiler_params=_cparams("arbitrary"),
        name="ada",
    )(c_all, w_ada, b_ada)


def _ctx_kv_kernel(x_ref, sh_ref, sc_ref, nmix_ref, w_ref, kg_ref, bd_ref, k_ref, v_ref):
    h = _rms(x_ref[...], nmix_ref[...]) * (1.0 + sc_ref[...]) + sh_ref[...]
    p = _dot(h.astype(BF16), w_ref[...])
    ms = _group_mean_sq(p, bd_ref)
    k = p[:, :LANES] * lax.rsqrt(ms[:, :LANES] + EPS) * kg_ref[...]
    v = p[:, LANES:]
    lane = lax.broadcasted_iota(jnp.int32, k.shape, 1)
    ka, kb = _dup_halves(k, lane)
    k_ref[...] = jnp.concatenate([ka, kb], axis=-1).astype(BF16)
    v_ref[...] = _with_ones(v, lane).astype(BF16)


def _ctx_kv(ctx, mod4, ctx_row, nmix, w_kv, kg, bd):
    b, n_ctx, d = ctx.shape
    kv_w = 2 * LANES
    return pl.pallas_call(
        _ctx_kv_kernel,
        out_shape=(jax.ShapeDtypeStruct((b, n_ctx, kv_w), BF16),) * 2,
        grid=(b,),
        in_specs=[pl.BlockSpec((None, n_ctx, d), lambda i: (i, 0, 0)),
                  pl.BlockSpec((None, None, 1, d), lambda i: (ctx_row, 0, 0, 0)),
                  pl.BlockSpec((None, None, 1, d), lambda i: (ctx_row, 1, 0, 0)),
                  pl.BlockSpec((1, d), lambda i: (0, 0)),
                  pl.BlockSpec(w_kv.shape, lambda i: (0, 0)),
                  pl.BlockSpec((1, LANES), lambda i: (0, 0)),
                  pl.BlockSpec(bd.shape, lambda i: (0, 0))],
        out_specs=(pl.BlockSpec((None, n_ctx, kv_w), lambda i: (i, 0, 0)),) * 2,
        compiler_params=_cparams("arbitrary"),
        name="ctx_kv",
    )(ctx, mod4, mod4, nmix, w_kv, kg, bd)


def _inproj_kernel(x_ref, sh_ref, sc_ref, nmix_ref, w_ref, cos_ref, sin_ref, qg_ref, kg_ref, gmg_ref,
                   bd_ref, ws_ref, bs_ref, og_ref, q_ref, k_ref, v_ref, gm_ref, *, att_w, gm_w):
    tm = x_ref.shape[0]
    sub = min(tm, ROW_BLOCK)

    def project(rows):
        h = _rms(x_ref[rows, :], nmix_ref[...]) * (1.0 + sc_ref[...]) + sh_ref[...]
        return _dot(h.astype(BF16), w_ref[...])

    def rope(xn, cosw, sinw, lanew):
        w = xn.shape[1]
        fwd = pltpu.roll(xn, w - 32, axis=1)
        bwd = pltpu.roll(xn, 32, axis=1)
        swapped = jnp.where((lanew & 63) < 32, fwd, bwd)
        return xn * cosw + swapped * sinw

    def finish(rows, p):
        cos1, sin1 = cos_ref[rows, :], sin_ref[rows, :]
        lane1 = lax.broadcasted_iota(jnp.int32, (sub, LANES), 1)

        q = p[:, :att_w]
        qn = q * lax.rsqrt(_group_mean_sq(q, bd_ref) + EPS) * qg_ref[...]
        reps = att_w // LANES
        cosq = jnp.concatenate([cos1] * reps, axis=-1)
        sinq = jnp.concatenate([sin1] * reps, axis=-1)
        laneq = lax.broadcasted_iota(jnp.int32, (sub, att_w), 1)
        q_ref[rows, :] = rope(qn, cosq, sinq, laneq).astype(BF16)

        kv = p[:, att_w:att_w + 2 * LANES]
        ms = _group_mean_sq(kv, bd_ref)
        kn = kv[:, :LANES] * lax.rsqrt(ms[:, :LANES] + EPS) * kg_ref[...]
        k = rope(kn, cos1, sin1, lane1)
        v = kv[:, LANES:]
        ka, kb = _dup_halves(k, lane1)
        k_ref[rows, :] = jnp.concatenate([ka, kb], axis=-1).astype(BF16)
        v_ref[rows, :] = _with_ones(v, lane1).astype(BF16)

        u0 = att_w + 2 * LANES
        u = jax.nn.gelu(p[:, u0:u0 + gm_w])
        gt = jax.nn.gelu(p[:, u0 + gm_w:u0 + 2 * gm_w])
        gt = (gt * lax.rsqrt(_group_mean_sq(gt, bd_ref) + EPS) * gmg_ref[...]).astype(BF16)
        lo_half = lax.broadcasted_iota(jnp.int32, (CHUNK, LANES), 1) < LANES // 2
        mixed_rows = []
        for c in range(sub // CHUNK):
            blocks = []
            for j in range(gm_w // LANES):
                g = gt[c * CHUNK:(c + 1) * CHUNK, j * LANES:(j + 1) * LANES]
                r = _dot(ws_ref[j], g)
                blocks.append(jnp.where(lo_half, r[:CHUNK], r[CHUNK:]))
            mixed_rows.append(jnp.concatenate(blocks, axis=-1) + bs_ref[...])
        mixed = jnp.concatenate(mixed_rows, axis=0)
        gm_ref[rows, :] = _rms(u * mixed, og_ref[...]).astype(BF16)

    _pipelined_blocks(tm, sub, project, finish)


def _inproj(x, mod4, nmix, w_in, cos_t, sin_t, qg, kg, gmg, bd, ws, bs, og, *, att_w, gm_w, tm):
    b, length, d = x.shape
    kv_w = 2 * LANES
    n_in = w_in.shape[1]
    const = lambda shape: pl.BlockSpec(shape, lambda i, j: (0,) * len(shape))
    return pl.pallas_call(
        functools.partial(_inproj_kernel, att_w=att_w, gm_w=gm_w),
        out_shape=(jax.ShapeDtypeStruct((b, length, att_w), BF16),
                   jax.ShapeDtypeStruct((b, length, kv_w), BF16),
                   jax.ShapeDtypeStruct((b, length, kv_w), BF16),
                   jax.ShapeDtypeStruct((b, length, gm_w), BF16)),
        grid=(b, length // tm),
        in_specs=[pl.BlockSpec((None, tm, d), lambda i, j: (i, j, 0)),
                  pl.BlockSpec((None, None, 1, d), lambda i, j: (i, 0, 0, 0)),
                  pl.BlockSpec((None, None, 1, d), lambda i, j: (i, 1, 0, 0)),
                  const((1, d)),
                  const((d, n_in)),
                  pl.BlockSpec((tm, LANES), lambda i, j: (j, 0)),
                  pl.BlockSpec((tm, LANES), lambda i, j: (j, 0)),
                  const((1, att_w)), const((1, LANES)), const((1, gm_w)),
                  const(bd.shape), const(ws.shape), const(bs.shape), const((1, gm_w))],
        out_specs=(pl.BlockSpec((None, tm, att_w), lambda i, j: (i, j, 0)),
                   pl.BlockSpec((None, tm, kv_w), lambda i, j: (i, j, 0)),
                   pl.BlockSpec((None, tm, kv_w), lambda i, j: (i, j, 0)),
                   pl.BlockSpec((None, tm, gm_w), lambda i, j: (i, j, 0))),
        compiler_params=_cparams("arbitrary", "arbitrary"),
        name="inproj",
    )(x, mod4, mod4, nmix, w_in, cos_t, sin_t, qg, kg, gmg, bd, ws, bs, og)


def _attn_kernel(q_ref, kc_ref, kl_ref, vc_ref, vl_ref, o_ref, s_buf, p_buf, *, tq):
    n_ctx = kc_ref.shape[0]
    heads = q_ref.shape[1] // (LANES // 2)
    nj = q_ref.shape[0] // tq
    half = LANES // 2
    lo = lax.broadcasted_iota(jnp.int32, (tq, LANES), 1) < half

    @pl.when(jnp.logical_and(pl.program_id(0) == 0, pl.program_id(1) == 0))
    def _():
        s_buf[...] = jnp.zeros_like(s_buf)
        p_buf[...] = jnp.ones_like(p_buf)

    def scores(j, g):
        q = q_ref[pl.ds(j * tq, tq), (g // 2) * LANES:(g // 2 + 1) * LANES]
        zero = jnp.zeros_like(q)
        qh = jnp.where(lo, q, zero) if g % 2 == 0 else jnp.where(lo, zero, q)
        s_buf[g % 2, :, :n_ctx] = _dot_nt(qh, kc_ref[...])
        s_buf[g % 2, :, n_ctx:] = _dot_nt(qh, kl_ref[...])

    def probs(g):
        s = s_buf[g % 2]
        p_buf[g % 2] = jnp.exp2(s - jnp.max(s, axis=-1, keepdims=True)).astype(BF16)

    def output(j, g):
        p = p_buf[g % 2]
        w = _dot(p[:, :n_ctx], vc_ref[...]) + _dot(p[:, n_ctx:], vl_ref[...])
        t = jnp.where(pl.program_id(1) == 0, w[:, :LANES], w[:, LANES:])
        r = pltpu.roll(t, half, axis=1)
        rows = pl.ds(j * tq, tq)
        if g % 2 == 0:
            o_ref[rows, g * half:(g + 1) * half] = (t / r)[:, :half].astype(o_ref.dtype)
        else:
            o_ref[rows, g * half:(g + 1) * half] = (r / t)[:, half:].astype(o_ref.dtype)

    def tile_stages(j):
        j_prev = jnp.maximum(j - 1, 0)
        for g in range(heads):
            scores(j, g)
            probs((g - 1) % heads)
            output(j if g >= 2 else j_prev, (g - 2) % heads)

    def trip(jj, carry):
        for u in range(tiles_per_trip):
            tile_stages(jj * tiles_per_trip + u)
        return carry

    tiles_per_trip = 2 if nj % 2 == 0 else 1
    lax.fori_loop(0, nj // tiles_per_trip, trip, 0)
    probs(heads - 1)
    output(nj - 1, heads - 2)
    output(nj - 1, heads - 1)


def _unread_operand(in_specs, operands, array):
    in_specs.append(pl.BlockSpec(memory_space=pl.ANY))
    operands.append(array)
    return len(operands) - 1


def _attention(q, k_c, k_l, v_c, v_l, att_so_far, after, *, tq, b0, nb, slot0, slots):
    _, length, att_w = q.shape
    n_ctx = k_c.shape[1]
    group_w = att_w // ATT_KV_HEADS
    assert (group_w // (LANES // 2)) % 2 == 0 and length % tq == 0
    k_blk = lambda n: pl.BlockSpec((None, n, LANES), lambda i, h: (i + b0, 0, h))
    v_blk = lambda n: pl.BlockSpec((None, n, ATT_KV_HEADS * LANES), lambda i, h: (i + b0, 0, 0))
    in_specs = [pl.BlockSpec((None, length, group_w), lambda i, h: (i + b0, 0, h)),
                k_blk(n_ctx), k_blk(length), v_blk(n_ctx), v_blk(length)]
    operands = [q, k_c, k_l, v_c, v_l]
    n_read = len(operands)
    aliases = {}
    if att_so_far is not None:
        aliases = {_unread_operand(in_specs, operands, att_so_far): 0}
    if after is not None:
        _unread_operand(in_specs, operands, after)
    n_in = len(operands)
    return pl.pallas_call(
        lambda *refs: _attn_kernel(*refs[:n_read], *refs[n_in:], tq=tq),
        out_shape=jax.ShapeDtypeStruct((slots, length, att_w), BF16),
        grid=(nb, ATT_KV_HEADS),
        in_specs=in_specs,
        out_specs=pl.BlockSpec((None, length, group_w), lambda i, h: (i + slot0, 0, h)),
        scratch_shapes=[pltpu.VMEM((2, tq, n_ctx + length), F32), pltpu.VMEM((2, tq, n_ctx + length), BF16)],
        input_output_aliases=aliases,
        compiler_params=_cparams("arbitrary", "arbitrary"),
        name="attn",
    )(*operands)


def _post_kernel(att_ref, gm_ref, x_ref, g1_ref, sh2_ref, sc2_ref, g2_ref, oa_ref, wo_ref, nffn_ref,
                 wrt_ref, rb_ref, wsg_ref, wsd_ref,
                 base_ref, h2p_ref, idx_ref, gate_ref, rank_ref, cnt_ref, carry_ref):
    i = pl.program_id(0)
    tm = x_ref.shape[0]
    n_exp = wrt_ref.shape[0]

    @pl.when(i == 0)
    def _():
        carry_ref[...] = jnp.zeros_like(carry_ref)

    sub = min(tm, ROW_BLOCK)
    reps = sub // LANES
    neg = jnp.float32(-jnp.inf)
    r_io = lax.broadcasted_iota(jnp.int32, (sub, sub), 0)
    c_io = lax.broadcasted_iota(jnp.int32, (sub, sub), 1)
    tri = jnp.where(r_io < c_io, 1.0, 0.0).astype(BF16)

    def residual(rows):
        att_n = _rms(att_ref[rows, :].astype(F32), oa_ref[...]).astype(BF16)
        y = _dot(jnp.concatenate([att_n, gm_ref[rows, :]], axis=-1), wo_ref[...])
        x_new = x_ref[rows, :] + g1_ref[...] * y
        h2 = _rms(x_new, nffn_ref[...]) * (1.0 + sc2_ref[...]) + sh2_ref[...]
        h2p_ref[rows, :] = _pack_rows(h2)
        return x_new, h2

    def route_and_share(rows, carried):
        x_new, h2 = carried
        h_hi = h2.astype(BF16)
        scores_t = jax.nn.sigmoid(_dot_nt(wrt_ref[...], h_hi))
        sel = scores_t + jnp.concatenate([rb_ref[...]] * reps, axis=-1)
        expert = lax.broadcasted_iota(jnp.int32, scores_t.shape, 0).astype(F32)

        onehot = jnp.zeros(scores_t.shape, F32)
        hits, idxs, svals = [], [], []
        for _k in range(TOP_K):
            m = jnp.max(sel, axis=0, keepdims=True)
            idx = jnp.min(jnp.where(sel == m, expert, float(n_exp)), axis=0, keepdims=True)
            hit = expert == idx
            hits.append(hit)
            idxs.append(idx)
            svals.append(jnp.sum(jnp.where(hit, scores_t, 0.0), axis=0, keepdims=True))
            onehot = onehot + jnp.where(hit, 1.0, 0.0)
            sel = jnp.where(hit, neg, sel)
        ssum = functools.reduce(lambda a, b: a + b, svals)

        carry = carry_ref[...]
        before = _dot(onehot.astype(BF16), tri) + jnp.concatenate([carry] * reps, axis=-1)
        carry = carry + jnp.sum(onehot, axis=1, keepdims=True)
        carry_ref[...] = carry
        cnt_ref[...] = carry

        row = lax.broadcasted_iota(jnp.int32, (idx_ref.shape[0], sub), 0)
        idx_o = jnp.zeros(row.shape, F32)
        rank_o = jnp.zeros(row.shape, F32)
        gate_o = jnp.zeros(row.shape, F32)
        for k in range(TOP_K):
            rank = jnp.sum(jnp.where(hits[k], before, 0.0), axis=0, keepdims=True)
            idx_o = jnp.where(row == k, idxs[k], idx_o)
            rank_o = jnp.where(row == k, rank, rank_o)
            gate_o = jnp.where(row == k, svals[k] / ssum * ROUTED_SCALE, gate_o)
        idx_ref[:, rows] = idx_o.astype(jnp.int32)
        rank_ref[:, rows] = rank_o.astype(jnp.int32)
        gate_pad = jnp.concatenate([gate_o, jnp.zeros((LANES - gate_o.shape[0], sub), F32)], axis=0)
        gate_ref[rows, :] = gate_pad.T

        gu = _dot(h_hi, wsg_ref[...])
        hs = gu.shape[1] // 2
        act = (jax.nn.silu(gu[:, :hs]) * gu[:, hs:]).astype(BF16)
        base_ref[rows, :] = x_new + g2_ref[...] * _dot(act, wsd_ref[...])

    _pipelined_blocks(tm, sub, residual, route_and_share)


def _post(att, gmn, x2, mod4, oa, w_out, nffn, wr_t, rb, w_sg, w_sd, *, tm, tokens_per_batch, tok0):
    t, att_w = att.shape
    d = x2.shape[1]
    n_exp = wr_t.shape[0]
    gm_w = gmn.shape[1]
    tpb = tokens_per_batch // tm
    blk0 = tok0 // tm
    const = lambda shape: pl.BlockSpec(shape, lambda i: (0,) * len(shape))
    mrow = lambda j: pl.BlockSpec((None, None, 1, d), lambda i: ((i + blk0) // tpb, j, 0, 0))
    by_choice = jax.ShapeDtypeStruct((CHOICE_ROWS, t), jnp.int32)
    return pl.pallas_call(
        _post_kernel,
        out_shape=(jax.ShapeDtypeStruct((t, d), F32), jax.ShapeDtypeStruct((t, d // 2), jnp.int32),
                   by_choice, jax.ShapeDtypeStruct((t, LANES), F32), by_choice,
                   jax.ShapeDtypeStruct((n_exp, LANES), F32)),
        grid=(t // tm,),
        in_specs=[pl.BlockSpec((tm, att_w), lambda i: (i, 0)),
                  pl.BlockSpec((tm, gm_w), lambda i: (i + blk0, 0)),
                  pl.BlockSpec((tm, d), lambda i: (i + blk0, 0)),
                  mrow(2), mrow(3), mrow(4), mrow(5),
                  const((1, att_w)), const(w_out.shape), const((1, d)),
                  const(wr_t.shape), const(rb.shape),
                  const(w_sg.shape), const(w_sd.shape)],
        out_specs=(pl.BlockSpec((tm, d), lambda i: (i, 0)), pl.BlockSpec((tm, d // 2), lambda i: (i, 0)),
                   pl.BlockSpec((CHOICE_ROWS, tm), lambda i: (0, i)), pl.BlockSpec((tm, LANES), lambda i: (i, 0)),
                   pl.BlockSpec((CHOICE_ROWS, tm), lambda i: (0, i)), pl.BlockSpec((n_exp, LANES), lambda i: (0, 0))),
        scratch_shapes=[pltpu.VMEM((n_exp, LANES), F32)],
        compiler_params=_cparams("arbitrary"),
        name="post",
    )(att, gmn, x2, mod4, mod4, mod4, mod4, oa, w_out, nffn, wr_t, rb, w_sg, w_sd)


def _sc_mesh_info():
    info = plsc.get_sparse_core_info()
    mesh = plsc.VectorSubcoreMesh(core_axis_name="c", subcore_axis_name="s")
    return mesh, info.num_cores, info.num_subcores, info.num_lanes


def _sc_params():
    cp = pltpu.CompilerParams()
    if "needs_layout_passes" in pltpu.CompilerParams.__dataclass_fields__:
        cp = dataclasses.replace(cp, needs_layout_passes=False)
    return cp


def _dispatch(h2p, idx_t, rank_t, offs, *, n_slots):
    t, dw = h2p.shape
    mesh, n_cores, n_sub, n_lanes = _sc_mesh_info()
    per_worker = t // (n_cores * n_sub)
    w = SC_WINDOW
    n_win = per_worker // w
    assert per_worker % w == 0 and w % n_lanes == 0 and n_win % 2 == 0

    @functools.partial(
        pl.kernel, mesh=mesh,
        out_type=(jax.ShapeDtypeStruct((n_slots, dw), jnp.int32), jax.ShapeDtypeStruct((CHOICE_ROWS, t), jnp.int32)),
        scratch_types=[pltpu.VMEM(offs.shape, jnp.int32)] + [pltpu.VMEM((TOP_K, w), jnp.int32)] * 5
        + [pltpu.VMEM((w, dw), jnp.int32)] * 2 + [pltpu.SemaphoreType.DMA] * 2,
        compiler_params=_sc_params(),
    )
    def run(h_hbm, idx_hbm, rank_hbm, offs_hbm, xs_hbm, pos_hbm, offs_v, idx_a, idx_b, rank_a, rank_b, pos_v,
            rows_a, rows_b, sem_in, sem_out):
        base = (lax.axis_index("s") * n_cores + lax.axis_index("c")) * per_worker
        last = base + per_worker - w
        sets = ((rows_a, idx_a, rank_a), (rows_b, idx_b, rank_b))
        pltpu.sync_copy(offs_hbm, offs_v)

        def loads(t0, slot):
            rows_v, idx_v, rank_v = sets[slot]
            cps = [pltpu.make_async_copy(h_hbm.at[pl.ds(t0, w)], rows_v, sem_in)]
            for k in range(TOP_K):
                cps.append(pltpu.make_async_copy(idx_hbm.at[k, pl.ds(t0, w)], idx_v.at[k], sem_in))
                cps.append(pltpu.make_async_copy(rank_hbm.at[k, pl.ds(t0, w)], rank_v.at[k], sem_in))
            return cps

        for cp in loads(base, 0):
            cp.start()

        @pl.loop(0, n_win // 2)
        def _(i):
            for slot in range(2):
                t0 = base + (2 * i + slot) * w
                rows_v, idx_v, rank_v = sets[slot]
                for cp in loads(t0, slot):
                    cp.wait()
                for cp in loads(jnp.minimum(t0 + w, last), 1 - slot):
                    cp.start()
                for k in range(TOP_K):
                    for j in range(0, w, n_lanes):
                        group_start = plsc.load_gather(offs_v, [idx_v[k, pl.ds(j, n_lanes)]])
                        pos_v[k, pl.ds(j, n_lanes)] = group_start + rank_v[k, pl.ds(j, n_lanes)]
                copies = [pltpu.async_copy(rows_v, xs_hbm.at[pos_v.at[k]], sem_out) for k in range(TOP_K)]
                copies += [pltpu.async_copy(pos_v.at[k], pos_hbm.at[k, pl.ds(t0, w)], sem_out) for k in range(TOP_K)]
                for cp in copies:
                    cp.wait()

        for cp in loads(last, 0):
            cp.wait()

    return run(h2p, idx_t, rank_t, offs)


def _collect(ys, pos_t, *, tok_lo, n_tok):
    dw = ys.shape[1]
    mesh, n_cores, n_sub, _ = _sc_mesh_info()
    per_worker = n_tok // (n_cores * n_sub)
    w = SC_WINDOW
    n_win = per_worker // w
    assert per_worker % w == 0 and n_win % 2 == 0

    @functools.partial(
        pl.kernel, mesh=mesh,
        out_type=jax.ShapeDtypeStruct((TOP_K * n_tok, dw), jnp.int32),
        scratch_types=[pltpu.VMEM((TOP_K, w), jnp.int32)] * 2 + [pltpu.VMEM((w, dw), jnp.int32)] * 2
        + [pltpu.SemaphoreType.DMA] * 3,
        compiler_params=_sc_params(),
    )
    def run(ys_hbm, pos_hbm, r_hbm, pos_a, pos_b, rows_a, rows_b, sem_p, sem_g, sem_w):
        base = (lax.axis_index("s") * n_cores + lax.axis_index("c")) * per_worker
        bufs = (rows_a, rows_b)
        pos = (pos_a, pos_b)
        steps = [(s, k) for s in range(2) for k in range(TOP_K)]

        @pl.loop(0, n_win // 2)
        def _(i):
            t0 = base + 2 * i * w
            slots = [pltpu.async_copy(pos_hbm.at[k, pl.ds(tok_lo + t0 + s * w, w)], pos[s].at[k], sem_p)
                     for s, k in steps]
            for cp in slots:
                cp.wait()

            def gather(n):
                s, k = steps[n]
                return pltpu.async_copy(ys_hbm.at[pos[s].at[k]], bufs[n % 2], sem_g)

            gathers = [None] * len(steps)
            writes = [None] * len(steps)
            gathers[0] = gather(0)
            for n, (s, k) in enumerate(steps):
                gathers[n].wait()
                if n + 1 < len(steps):
                    if n >= 1:
                        writes[n - 1].wait()
                    gathers[n + 1] = gather(n + 1)
                writes[n] = pltpu.async_copy(bufs[n % 2], r_hbm.at[pl.ds(k * n_tok + t0 + s * w, w)], sem_w)
            writes[-2].wait()
            writes[-1].wait()

    return run(ys, pos_t)


def _experts_kernel(t0_ref, t1_ref, xs_hbm, wgu_ref, wdn_ref, ys_hbm, xbuf, ybuf, wgu_b, wdn_b, sem_in, sem_out):
    e = pl.program_id(0)
    n_exp = pl.num_programs(0)
    n_in, tr, half = xbuf.shape
    n_out = ybuf.shape[0]
    n_tiles = t1_ref[n_exp - 1]

    def fetch(t):
        slot = t & (n_in - 1)
        return pltpu.make_async_copy(xs_hbm.at[pl.ds(t * tr, tr)], xbuf.at[slot], sem_in.at[slot])

    def flush(t):
        slot = t & (n_out - 1)
        return pltpu.make_async_copy(ybuf.at[slot], ys_hbm.at[pl.ds(t * tr, tr)], sem_out.at[slot])

    @pl.when(e == 0)
    def _():
        for t in range(n_in - 1):
            @pl.when(t < n_tiles)
            def _():
                fetch(t).start()

    wgu_b[...] = wgu_ref[...].astype(BF16)
    wdn_b[...] = wdn_ref[...].astype(BF16)

    def tile(t, carry):
        fetch(t).wait()

        @pl.when(t + n_in - 1 < n_tiles)
        def _():
            fetch(t + n_in - 1).start()

        @pl.when(t >= n_out)
        def _():
            flush(t - n_out).wait()

        x_slot = xbuf.at[t & (n_in - 1)]
        y_slot = ybuf.at[t & (n_out - 1)]

        def up(rows):
            lo, hi = _unpack_rows(x_slot[rows, :])
            gu = _dot(lo.astype(BF16), wgu_b[:half, :]) + _dot(hi.astype(BF16), wgu_b[half:, :])
            hh = gu.shape[1] // 2
            return (jax.nn.silu(gu[:, :hh]) * gu[:, hh:]).astype(BF16)

        def down(rows, act):
            y_slot[rows, :] = _pack_rows(_dot(act, wdn_b[...]))

        _pipelined_blocks(tr, min(tr, ROW_BLOCK // 2), up, down)
        flush(t).start()
        return carry

    lax.fori_loop(t0_ref[e], t1_ref[e], tile, 0)

    @pl.when(e == n_exp - 1)
    def _():
        for back in range(n_out, 0, -1):
            @pl.when(n_tiles >= back)
            def _():
                flush(n_tiles - back).wait()


def _experts(tile_lo, tile_hi, xs, w_gu, w_dn, after, *, tr):
    n_exp, d, h2w = w_gu.shape
    hh = w_dn.shape[1]
    in_specs = [pl.BlockSpec(memory_space=pl.ANY),
                pl.BlockSpec((None, d, h2w), lambda e, lo, hi: (e, 0, 0)),
                pl.BlockSpec((None, hh, d), lambda e, lo, hi: (e, 0, 0))]
    operands = [xs, w_gu, w_dn]
    n_read = 2 + len(operands)
    if after is not None:
        _unread_operand(in_specs, operands, after)
    n_in = 2 + len(operands)
    return pl.pallas_call(
        lambda *refs: _experts_kernel(*refs[:n_read], *refs[n_in:]),
        out_shape=jax.ShapeDtypeStruct(xs.shape, jnp.int32),
        grid_spec=pltpu.PrefetchScalarGridSpec(
            num_scalar_prefetch=2,
            grid=(n_exp,),
            in_specs=in_specs,
            out_specs=pl.BlockSpec(memory_space=pl.ANY),
            scratch_shapes=[pltpu.VMEM((EXPERT_IN_SLOTS, tr, d // 2), jnp.int32),
                            pltpu.VMEM((EXPERT_OUT_SLOTS, tr, d // 2), jnp.int32),
                            pltpu.VMEM((d, h2w), BF16), pltpu.VMEM((hh, d), BF16),
                            pltpu.SemaphoreType.DMA((EXPERT_IN_SLOTS,)), pltpu.SemaphoreType.DMA((EXPERT_OUT_SLOTS,))]),
        compiler_params=_cparams("arbitrary"),
        name="experts",
    )(tile_lo, tile_hi, *operands)


def _combine_kernel(base_ref, g2_ref, gate_ref, *refs):
    r_refs, o_ref = refs[:TOP_K], refs[TOP_K]
    gate = gate_ref[...]
    half = base_ref.shape[1] // 2
    acc_lo = jnp.zeros((base_ref.shape[0], half), F32)
    acc_hi = jnp.zeros((base_ref.shape[0], half), F32)
    for k in range(TOP_K):
        lo, hi = _unpack_rows(r_refs[k][...])
        g = gate[:, k:k + 1]
        acc_lo = acc_lo + g * lo
        acc_hi = acc_hi + g * hi
    g2 = g2_ref[...]
    o_ref[:, :half] = base_ref[:, :half] + g2[:, :half] * acc_lo
    o_ref[:, half:] = base_ref[:, half:] + g2[:, half:] * acc_hi


def _combine(base, mod4, gate, r, out_so_far, after, *, tm, tokens_per_batch, tok0, part_lo, total_tokens):
    d = base.shape[1]
    t = r.shape[0] // TOP_K
    tpb = tokens_per_batch // tm
    nblk = t // tm
    pblk0 = part_lo // tm
    blk0 = (tok0 + part_lo) // tm
    r_spec = lambda k: pl.BlockSpec((tm, d // 2), lambda i: (k * nblk + i, 0))
    in_specs = [pl.BlockSpec((tm, d), lambda i: (i + pblk0, 0)),
                pl.BlockSpec((None, None, 1, d), lambda i: ((i + blk0) // tpb, 5, 0, 0)),
                pl.BlockSpec((tm, LANES), lambda i: (i + pblk0, 0))] + [r_spec(k) for k in range(TOP_K)]
    operands = [base, mod4, gate] + [r] * TOP_K
    n_read = len(operands)
    aliases = {}
    if out_so_far is not None:
        aliases = {_unread_operand(in_specs, operands, out_so_far): 0}
    if after is not None:
        _unread_operand(in_specs, operands, after)
    n_in = len(operands)
    return pl.pallas_call(
        lambda *refs: _combine_kernel(*refs[:n_read], *refs[n_in:]),
        out_shape=jax.ShapeDtypeStruct((total_tokens, d), F32),
        grid=(nblk,),
        in_specs=in_specs,
        out_specs=pl.BlockSpec((tm, d), lambda i: (i + blk0, 0)),
        input_output_aliases=aliases,
        compiler_params=_cparams("arbitrary"),
        name="combine",
    )(*operands)


def _rope_tables(length, head_dim):
    rows = length // GRID_W
    pairs_axis = head_dim // 4
    r = jnp.repeat(jnp.arange(rows, dtype=F32), GRID_W)
    col = jnp.tile(jnp.arange(GRID_W, dtype=F32), rows)
    inv = ROPE_THETA ** (-jnp.arange(pairs_axis, dtype=F32) / pairs_axis)
    ang = jnp.concatenate([r[:, None] * inv, col[:, None] * inv], axis=-1)
    cos, sin = jnp.cos(ang), jnp.sin(ang)
    cos_h = jnp.concatenate([cos, cos], axis=-1)
    sin_h = jnp.concatenate([-sin, sin], axis=-1)
    reps = LANES // head_dim
    return jnp.tile(cos_h, (1, reps)), jnp.tile(sin_h, (1, reps))


def _part_batches(b):
    if b < 2:
        return [b]
    lead = max(1, min(b - 1, round(LEAD_PART_SHARE * b)))
    return [lead, b - lead]


def kernel(x, c, ctx, c_ctx, w_ada, b_ada, norm_mix, w_in, q_norm, k_norm, gm_norm, w_spatial, b_spatial,
           out_norm_attn, out_norm_gm, w_out, norm_ffn, w_router, router_bias, w_exp_gu, w_exp_down,
           w_sh_gu, w_sh_down):
    assert w_ada.shape[0] == 1, "single-layer kernel"
    b, length, d = x.shape
    head_dim = q_norm.shape[-1]
    att_w = out_norm_attn.shape[-1]
    gm_w = out_norm_gm.shape[-1]
    gm_groups = w_spatial.shape[1]
    gm_hd = gm_w // gm_groups
    n_exp = w_router.shape[-1]
    t = b * length
    assert head_dim == LANES // 2 and gm_hd == LANES // 2 and w_spatial.shape[-1] == CHUNK
    assert n_exp <= LANES and att_w % MXU_EDGE == 0 and gm_w % MXU_EDGE == 0

    mod_rows = 16
    c_all = jnp.zeros((mod_rows, d), F32).at[:b].set(c).at[b].set(c_ctx)
    mod = _ada(c_all, w_ada[0], b_ada)
    mod4 = mod.reshape(mod_rows, N_MOD, 1, d)

    w_in_b = w_in[0].astype(BF16)
    bd = (jnp.arange(MXU_EDGE)[:, None] // head_dim == jnp.arange(MXU_EDGE)[None, :] // head_dim)
    bd = (bd.astype(F32) / head_dim).astype(BF16)
    qg = jnp.tile(q_norm[0] * (head_dim ** -0.5 * LOG2_E), att_w // head_dim)[None, :]
    kg = jnp.tile(k_norm[0], LANES // head_dim)[None, :]
    gmg = jnp.tile(gm_norm[0], gm_groups)[None, :]
    ws = w_spatial[0].astype(BF16).reshape(gm_groups // 2, 2 * CHUNK, CHUNK)
    bs = jnp.repeat(b_spatial[0].T, gm_hd, axis=1)
    cos_t, sin_t = _rope_tables(length, head_dim)

    k_c, v_c = _ctx_kv(ctx, mod4, b, norm_mix, w_in_b[:, att_w:att_w + 2 * LANES], kg, bd)
    q, k_l, v_l, gmn = _inproj(x, mod4, norm_mix, w_in_b, cos_t, sin_t, qg, kg, gmg, bd, ws, bs,
                               out_norm_gm, att_w=att_w, gm_w=gm_w, tm=min(1024, length))
    wr_t = w_router[0].T.astype(BF16)
    rb = jnp.tile(router_bias[0][:, None], (1, LANES))
    w_out_b, w_sg_b, w_sd_b = w_out[0].astype(BF16), w_sh_gu[0].astype(BF16), w_sh_down[0].astype(BF16)
    gmn2, x2 = gmn.reshape(t, gm_w), x.reshape(t, d)
    tm_post = min(1024, length)
    tr = 512

    part_nb = _part_batches(b)
    n_parts = len(part_nb)
    part_b0 = [sum(part_nb[:p]) for p in range(n_parts)]
    state = [dict() for _ in range(n_parts)]

    last_tc = [None]

    def halves(part):
        nb = part_nb[part]
        return (nb // 2, nb - nb // 2) if nb >= 2 else (nb,)

    def attend(part, half):
        if half >= len(halves(part)):
            return
        st = state[part]
        slot0 = sum(halves(part)[:half])
        after = None if last_tc[0] is st.get("att") else last_tc[0]
        st["att"] = _attention(q, k_c, k_l, v_c, v_l, st.get("att"), after, tq=min(256, length),
                               b0=part_b0[part] + slot0, nb=halves(part)[half], slot0=slot0, slots=part_nb[part])
        last_tc[0] = st["att"]

    def route(part):
        st = state[part]
        tp = part_nb[part] * length
        nt = tp * TOP_K // tr + n_exp
        base, h2p, idx_t, gate, rank_t, cnt = _post(
            st["att"].reshape(tp, att_w), gmn2, x2, mod4, out_norm_attn, w_out_b, norm_ffn, wr_t, rb, w_sg_b, w_sd_b,
            tm=tm_post, tokens_per_batch=length, tok0=part_b0[part] * length)
        counts = cnt[:, 0].astype(jnp.int32)
        padded = (counts + tr - 1) // tr * tr
        ends = jnp.cumsum(padded)
        offs = ends - padded
        offs_pad = jnp.zeros((LANES,), jnp.int32).at[:n_exp].set(offs)
        xs, pos_t = _dispatch(h2p, idx_t, rank_t, offs_pad, n_slots=nt * tr)
        st.update(base=base, gate=gate, xs=xs, pos_t=pos_t, tile_lo=offs // tr, tile_hi=ends // tr)
        last_tc[0] = base

    def run_experts(part):
        st = state[part]
        ys = _experts(st["tile_lo"], st["tile_hi"], st["xs"], w_exp_gu[0], w_exp_down[0], last_tc[0], tr=tr)
        n_chunks = part_nb[part] if part == n_parts - 1 else 1
        chunk = part_nb[part] * length // n_chunks
        st["r"] = [(c * chunk, _collect(ys, st["pos_t"], tok_lo=c * chunk, n_tok=chunk)) for c in range(n_chunks)]
        last_tc[0] = ys

    result = [None]

    def finish(part):
        st = state[part]
        for part_lo, r in st["r"]:
            after = None if last_tc[0] is result[0] or last_tc[0] is st["base"] else last_tc[0]
            result[0] = _combine(st["base"], mod4, st["gate"], r, result[0], after, tm=min(512, length),
                                 tokens_per_batch=length, tok0=part_b0[part] * length, part_lo=part_lo,
                                 total_tokens=t)
            last_tc[0] = result[0]

    stages = [(0, lambda p: attend(p, 0)), (1, lambda p: attend(p, 1)), (2, route), (4, run_experts), (7, finish)]
    plan = sorted((when + 3 * part, part, n, fn) for part in range(n_parts) for n, (when, fn) in enumerate(stages))
    for _, part, _, fn in plan:
        fn(part)
    return result[0].reshape(b, length, d)
```

```python
import dataclasses
import functools

import jax
import jax.numpy as jnp
from jax import lax
from jax.experimental import pallas as pl
from jax.experimental.pallas import tpu as pltpu
from jax.experimental.pallas import tpu_sc as plsc

F32 = jnp.float32
BF16 = jnp.bfloat16

EPS = 1e-6
GRID_W = 64
ROPE_THETA = 10000.0
ATT_KV_HEADS = 2
TOP_K = 6
ROUTED_SCALE = 2.5
N_MOD = 6
CHUNK = 128
LOG2_E = 1.4426950408889634

LANES = 128
MXU_EDGE = 256
VMEM_LIMIT_BYTES = 56 * 1024 * 1024
SUBLANES = 8
CHOICE_ROWS = SUBLANES
SC_WINDOW = 64
LEAD_PART_SHARE = 0.75
ROW_BLOCK = 512
EXPERT_IN_SLOTS = 4
EXPERT_OUT_SLOTS = 2


def _cparams(*sem):
    return pltpu.CompilerParams(dimension_semantics=sem, vmem_limit_bytes=VMEM_LIMIT_BYTES)


def _dot(a, b):
    return jnp.dot(a, b, preferred_element_type=F32)


def _dot_nt(a, b):
    return lax.dot_general(a, b, (((1,), (1,)), ((), ())), preferred_element_type=F32)


def _rms(x, g):
    return x * lax.rsqrt(jnp.mean(x * x, axis=-1, keepdims=True) + EPS) * g


def _group_mean_sq(x, bd_ref):
    x2 = (x * x).astype(BF16)
    parts = [_dot(x2[:, i:i + MXU_EDGE], bd_ref[...]) for i in range(0, x.shape[1], MXU_EDGE)]
    return parts[0] if len(parts) == 1 else jnp.concatenate(parts, axis=-1)


def _dup_halves(x, lane):
    xr = pltpu.roll(x, LANES // 2, axis=1)
    lo = lane < LANES // 2
    return jnp.where(lo, x, xr), jnp.where(lo, xr, x)


def _with_ones(x, lane):
    xr = pltpu.roll(x, LANES // 2, axis=1)
    lo = lane < LANES // 2
    return jnp.concatenate([jnp.where(lo, x, 1.0), jnp.where(lo, xr, 1.0)], axis=-1)


def _pipelined_blocks(tm, sub, head, tail):
    blocks = [pl.ds(r0, sub) for r0 in range(0, tm, sub)]
    pending = head(blocks[0])
    for n, rows in enumerate(blocks):
        ahead = head(blocks[n + 1]) if n + 1 < len(blocks) else None
        tail(rows, pending)
        pending = ahead


_HI16 = 0xFFFF0000


def _pack_rows(x):
    half = x.shape[1] // 2
    rounded = x.astype(BF16).astype(F32)
    bits = lax.bitcast_convert_type(rounded, jnp.uint32)
    word = (bits[:, :half] >> 16) | (bits[:, half:] & jnp.uint32(_HI16))
    return lax.bitcast_convert_type(word, jnp.int32)


def _unpack_rows(w):
    bits = lax.bitcast_convert_type(w, jnp.uint32)
    lo = lax.bitcast_convert_type(bits << 16, F32)
    hi = lax.bitcast_convert_type(bits & jnp.uint32(_HI16), F32)
    return lo, hi


def _ada_kernel(c_ref, w_ref, b_ref, o_ref):
    s = jax.nn.silu(c_ref[...])
    o_ref[...] = jnp.dot(s, w_ref[...], precision=lax.Precision.HIGHEST,
                         preferred_element_type=F32) + b_ref[...]


def _ada(c_all, w_ada, b_ada):
    rows, d = c_all.shape
    n = w_ada.shape[1]
    bn = 2048 if n % 2048 == 0 else 512
    return pl.pallas_call(
        _ada_kernel,
        out_shape=jax.ShapeDtypeStruct((rows, n), F32),
        grid=(n // bn,),
        in_specs=[pl.BlockSpec((rows, d), lambda j: (0, 0)),
                  pl.BlockSpec((d, bn), lambda j: (0, j)),
                  pl.BlockSpec((1, bn), lambda j: (0, j))],
        out_specs=pl.BlockSpec((rows, bn), lambda j: (0, j)),
        compiler_params=_cparams("arbitrary"),
        name="ada",
    )(c_all, w_ada, b_ada)


def _ctx_kv_kernel(x_ref, sh_ref, sc_ref, nmix_ref, w_ref, kg_ref, bd_ref, k_ref, v_ref):
    h = _rms(x_ref[...], nmix_ref[...]) * (1.0 + sc_ref[...]) + sh_ref[...]
    p = _dot(h.astype(BF16), w_ref[...])
    ms = _group_mean_sq(p, bd_ref)
    k = p[:, :LANES] * lax.rsqrt(ms[:, :LANES] + EPS) * kg_ref[...]
    v = p[:, LANES:]
    lane = lax.broadcasted_iota(jnp.int32, k.shape, 1)
    ka, kb = _dup_halves(k, lane)
    k_ref[...] = jnp.concatenate([ka, kb], axis=-1).astype(BF16)
    v_ref[...] = _with_ones(v, lane).astype(BF16)


def _ctx_kv(ctx, mod4, ctx_row, nmix, w_kv, kg, bd):
    b, n_ctx, d = ctx.shape
    kv_w = 2 * LANES
    return pl.pallas_call(
        _ctx_kv_kernel,
        out_shape=(jax.ShapeDtypeStruct((b, n_ctx, kv_w), BF16),) * 2,
        grid=(b,),
        in_specs=[pl.BlockSpec((None, n_ctx, d), lambda i: (i, 0, 0)),
                  pl.BlockSpec((None, None, 1, d), lambda i: (ctx_row, 0, 0, 0)),
                  pl.BlockSpec((None, None, 1, d), lambda i: (ctx_row, 1, 0, 0)),
                  pl.BlockSpec((1, d), lambda i: (0, 0)),
                  pl.BlockSpec(w_kv.shape, lambda i: (0, 0)),
                  pl.BlockSpec((1, LANES), lambda i: (0, 0)),
                  pl.BlockSpec(bd.shape, lambda i: (0, 0))],
        out_specs=(pl.BlockSpec((None, n_ctx, kv_w), lambda i: (i, 0, 0)),) * 2,
        compiler_params=_cparams("arbitrary"),
        name="ctx_kv",
    )(ctx, mod4, mod4, nmix, w_kv, kg, bd)


def _inproj_kernel(x_ref, sh_ref, sc_ref, nmix_ref, w_ref, cos_ref, sin_ref, qg_ref, kg_ref, gmg_ref,
                   bd_ref, ws_ref, bs_ref, og_ref, q_ref, k_ref, v_ref, gm_ref, *, att_w, gm_w):
    tm = x_ref.shape[0]
    sub = min(tm, ROW_BLOCK)

    def project(rows):
        h = _rms(x_ref[rows, :], nmix_ref[...]) * (1.0 + sc_ref[...]) + sh_ref[...]
        return _dot(h.astype(BF16), w_ref[...])

    def rope(xn, cosw, sinw, lanew):
        w = xn.shape[1]
        fwd = pltpu.roll(xn, w - 32, axis=1)
        bwd = pltpu.roll(xn, 32, axis=1)
        swapped = jnp.where((lanew & 63) < 32, fwd, bwd)
        return xn * cosw + swapped * sinw

    def finish(rows, p):
        cos1, sin1 = cos_ref[rows, :], sin_ref[rows, :]
        lane1 = lax.broadcasted_iota(jnp.int32, (sub, LANES), 1)

        q = p[:, :att_w]
        qn = q * lax.rsqrt(_group_mean_sq(q, bd_ref) + EPS) * qg_ref[...]
        reps = att_w // LANES
        cosq = jnp.concatenate([cos1] * reps, axis=-1)
        sinq = jnp.concatenate([sin1] * reps, axis=-1)
        laneq = lax.broadcasted_iota(jnp.int32, (sub, att_w), 1)
        q_ref[rows, :] = rope(qn, cosq, sinq, laneq).astype(BF16)

        kv = p[:, att_w:att_w + 2 * LANES]
        ms = _group_mean_sq(kv, bd_ref)
        kn = kv[:, :LANES] * lax.rsqrt(ms[:, :LANES] + EPS) * kg_ref[...]
        k = rope(kn, cos1, sin1, lane1)
        v = kv[:, LANES:]
        ka, kb = _dup_halves(k, lane1)
        k_ref[rows, :] = jnp.concatenate([ka, kb], axis=-1).astype(BF16)
        v_ref[rows, :] = _with_ones(v, lane1).astype(BF16)

        u0 = att_w + 2 * LANES
        u = jax.nn.gelu(p[:, u0:u0 + gm_w])
        gt = jax.nn.gelu(p[:, u0 + gm_w:u0 + 2 * gm_w])
        gt = (gt * lax.rsqrt(_group_mean_sq(gt, bd_ref) + EPS) * gmg_ref[...]).astype(BF16)
        lo_half = lax.broadcasted_iota(jnp.int32, (CHUNK, LANES), 1) < LANES // 2
        mixed_rows = []
        for c in range(sub // CHUNK):
            blocks = []
            for j in range(gm_w // LANES):
                g = gt[c * CHUNK:(c + 1) * CHUNK, j * LANES:(j + 1) * LANES]
                r = _dot(ws_ref[j], g)
                blocks.append(jnp.where(lo_half, r[:CHUNK], r[CHUNK:]))
            mixed_rows.append(jnp.concatenate(blocks, axis=-1) + bs_ref[...])
        mixed = jnp.concatenate(mixed_rows, axis=0)
        gm_ref[rows, :] = _rms(u * mixed, og_ref[...]).astype(BF16)

    _pipelined_blocks(tm, sub, project, finish)


def _inproj(x, mod4, nmix, w_in, cos_t, sin_t, qg, kg, gmg, bd, ws, bs, og, *, att_w, gm_w, tm):
    b, length, d = x.shape
    kv_w = 2 * LANES
    n_in = w_in.shape[1]
    const = lambda shape: pl.BlockSpec(shape, lambda i, j: (0,) * len(shape))
    return pl.pallas_call(
        functools.partial(_inproj_kernel, att_w=att_w, gm_w=gm_w),
        out_shape=(jax.ShapeDtypeStruct((b, length, att_w), BF16),
                   jax.ShapeDtypeStruct((b, length, kv_w), BF16),
                   jax.ShapeDtypeStruct((b, length, kv_w), BF16),
                   jax.ShapeDtypeStruct((b, length, gm_w), BF16)),
        grid=(b, length // tm),
        in_specs=[pl.BlockSpec((None, tm, d), lambda i, j: (i, j, 0)),
                  pl.BlockSpec((None, None, 1, d), lambda i, j: (i, 0, 0, 0)),
                  pl.BlockSpec((None, None, 1, d), lambda i, j: (i, 1, 0, 0)),
                  const((1, d)),
                  const((d, n_in)),
                  pl.BlockSpec((tm, LANES), lambda i, j: (j, 0)),
                  pl.BlockSpec((tm, LANES), lambda i, j: (j, 0)),
                  const((1, att_w)), const((1, LANES)), const((1, gm_w)),
                  const(bd.shape), const(ws.shape), const(bs.shape), const((1, gm_w))],
        out_specs=(pl.BlockSpec((None, tm, att_w), lambda i, j: (i, j, 0)),
                   pl.BlockSpec((None, tm, kv_w), lambda i, j: (i, j, 0)),
                   pl.BlockSpec((None, tm, kv_w), lambda i, j: (i, j, 0)),
                   pl.BlockSpec((None, tm, gm_w), lambda i, j: (i, j, 0))),
        compiler_params=_cparams("arbitrary", "arbitrary"),
        name="inproj",
    )(x, mod4, mod4, nmix, w_in, cos_t, sin_t, qg, kg, gmg, bd, ws, bs, og)


def _attn_kernel(q_ref, kc_ref, kl_ref, vc_ref, vl_ref, o_ref, s_buf, p_buf, *, tq):
    n_ctx = kc_ref.shape[0]
    heads = q_ref.shape[1] // (LANES // 2)
    nj = q_ref.shape[0] // tq
    half = LANES // 2
    lo = lax.broadcasted_iota(jnp.int32, (tq, LANES), 1) < half

    @pl.when(jnp.logical_and(pl.program_id(0) == 0, pl.program_id(1) == 0))
    def _():
        s_buf[...] = jnp.zeros_like(s_buf)
        p_buf[...] = jnp.ones_like(p_buf)

    def scores(j, g):
        q = q_ref[pl.ds(j * tq, tq), (g // 2) * LANES:(g // 2 + 1) * LANES]
        zero = jnp.zeros_like(q)
        qh = jnp.where(lo, q, zero) if g % 2 == 0 else jnp.where(lo, zero, q)
        s_buf[g % 2, :, :n_ctx] = _dot_nt(qh, kc_ref[...])
        s_buf[g % 2, :, n_ctx:] = _dot_nt(qh, kl_ref[...])

    def probs(g):
        s = s_buf[g % 2]
        p_buf[g % 2] = jnp.exp2(s - jnp.max(s, axis=-1, keepdims=True)).astype(BF16)

    def output(j, g):
        p = p_buf[g % 2]
        w = _dot(p[:, :n_ctx], vc_ref[...]) + _dot(p[:, n_ctx:], vl_ref[...])
        t = jnp.where(pl.program_id(1) == 0, w[:, :LANES], w[:, LANES:])
        r = pltpu.roll(t, half, axis=1)
        rows = pl.ds(j * tq, tq)
        if g % 2 == 0:
            o_ref[rows, g * half:(g + 1) * half] = (t / r)[:, :half].astype(o_ref.dtype)
        else:
            o_ref[rows, g * half:(g + 1) * half] = (r / t)[:, half:].astype(o_ref.dtype)

    def tile_stages(j):
        j_prev = jnp.maximum(j - 1, 0)
        for g in range(heads):
            scores(j, g)
            probs((g - 1) % heads)
            output(j if g >= 2 else j_prev, (g - 2) % heads)

    def trip(jj, carry):
        for u in range(tiles_per_trip):
            tile_stages(jj * tiles_per_trip + u)
        return carry

    tiles_per_trip = 2 if nj % 2 == 0 else 1
    lax.fori_loop(0, nj // tiles_per_trip, trip, 0)
    probs(heads - 1)
    output(nj - 1, heads - 2)
    output(nj - 1, heads - 1)


def _unread_operand(in_specs, operands, array):
    in_specs.append(pl.BlockSpec(memory_space=pl.ANY))
    operands.append(array)
    return len(operands) - 1


def _attention(q, k_c, k_l, v_c, v_l, att_so_far, after, *, tq, b0, nb, slot0, slots):
    _, length, att_w = q.shape
    n_ctx = k_c.shape[1]
    group_w = att_w // ATT_KV_HEADS
    assert (group_w // (LANES // 2)) % 2 == 0 and length % tq == 0
    k_blk = lambda n: pl.BlockSpec((None, n, LANES), lambda i, h: (i + b0, 0, h))
    v_blk = lambda n: pl.BlockSpec((None, n, ATT_KV_HEADS * LANES), lambda i, h: (i + b0, 0, 0))
    in_specs = [pl.BlockSpec((None, length, group_w), lambda i, h: (i + b0, 0, h)),
                k_blk(n_ctx), k_blk(length), v_blk(n_ctx), v_blk(length)]
    operands = [q, k_c, k_l, v_c, v_l]
    n_read = len(operands)
    aliases = {}
    if att_so_far is not None:
        aliases = {_unread_operand(in_specs, operands, att_so_far): 0}
    if after is not None:
        _unread_operand(in_specs, operands, after)
    n_in = len(operands)
    return pl.pallas_call(
        lambda *refs: _attn_kernel(*refs[:n_read], *refs[n_in:], tq=tq),
        out_shape=jax.ShapeDtypeStruct((slots, length, att_w), BF16),
        grid=(nb, ATT_KV_HEADS),
        in_specs=in_specs,
        out_specs=pl.BlockSpec((None, length, group_w), lambda i, h: (i + slot0, 0, h)),
        scratch_shapes=[pltpu.VMEM((2, tq, n_ctx + length), F32), pltpu.VMEM((2, tq, n_ctx + length), BF16)],
        input_output_aliases=aliases,
        compiler_params=_cparams("arbitrary", "arbitrary"),
        name="attn",
    )(*operands)


def _post_kernel(att_ref, gm_ref, x_ref, g1_ref, sh2_ref, sc2_ref, g2_ref, oa_ref, wo_ref, nffn_ref,
                 wrt_ref, rb_ref, wsg_ref, wsd_ref,
                 base_ref, h2p_ref, idx_ref, gate_ref, rank_ref, cnt_ref, carry_ref):
    i = pl.program_id(0)
    tm = x_ref.shape[0]
    n_exp = wrt_ref.shape[0]

    @pl.when(i == 0)
    def _():
        carry_ref[...] = jnp.zeros_like(carry_ref)

    sub = min(tm, ROW_BLOCK)
    reps = sub // LANES
    neg = jnp.float32(-jnp.inf)
    r_io = lax.broadcasted_iota(jnp.int32, (sub, sub), 0)
    c_io = lax.broadcasted_iota(jnp.int32, (sub, sub), 1)
    tri = jnp.where(r_io < c_io, 1.0, 0.0).astype(BF16)

    def residual(rows):
        att_n = _rms(att_ref[rows, :].astype(F32), oa_ref[...]).astype(BF16)
        y = _dot(jnp.concatenate([att_n, gm_ref[rows, :]], axis=-1), wo_ref[...])
        x_new = x_ref[rows, :] + g1_ref[...] * y
        h2 = _rms(x_new, nffn_ref[...]) * (1.0 + sc2_ref[...]) + sh2_ref[...]
        h2p_ref[rows, :] = _pack_rows(h2)
        return x_new, h2

    def route_and_share(rows, carried):
        x_new, h2 = carried
        h_hi = h2.astype(BF16)
        scores_t = jax.nn.sigmoid(_dot_nt(wrt_ref[...], h_hi))
        sel = scores_t + jnp.concatenate([rb_ref[...]] * reps, axis=-1)
        expert = lax.broadcasted_iota(jnp.int32, scores_t.shape, 0).astype(F32)

        onehot = jnp.zeros(scores_t.shape, F32)
        hits, idxs, svals = [], [], []
        for _k in range(TOP_K):
            m = jnp.max(sel, axis=0, keepdims=True)
            idx = jnp.min(jnp.where(sel == m, expert, float(n_exp)), axis=0, keepdims=True)
            hit = expert == idx
            hits.append(hit)
            idxs.append(idx)
            svals.append(jnp.sum(jnp.where(hit, scores_t, 0.0), axis=0, keepdims=True))
            onehot = onehot + jnp.where(hit, 1.0, 0.0)
            sel = jnp.where(hit, neg, sel)
        ssum = functools.reduce(lambda a, b: a + b, svals)

        carry = carry_ref[...]
        before = _dot(onehot.astype(BF16), tri) + jnp.concatenate([carry] * reps, axis=-1)
        carry = carry + jnp.sum(onehot, axis=1, keepdims=True)
        carry_ref[...] = carry
        cnt_ref[...] = carry

        row = lax.broadcasted_iota(jnp.int32, (idx_ref.shape[0], sub), 0)
        idx_o = jnp.zeros(row.shape, F32)
        rank_o = jnp.zeros(row.shape, F32)
        gate_o = jnp.zeros(row.shape, F32)
        for k in range(TOP_K):
            rank = jnp.sum(jnp.where(hits[k], before, 0.0), axis=0, keepdims=True)
            idx_o = jnp.where(row == k, idxs[k], idx_o)
            rank_o = jnp.where(row == k, rank, rank_o)
            gate_o = jnp.where(row == k, svals[k] / ssum * ROUTED_SCALE, gate_o)
        idx_ref[:, rows] = idx_o.astype(jnp.int32)
        rank_ref[:, rows] = rank_o.astype(jnp.int32)
        gate_pad = jnp.concatenate([gate_o, jnp.zeros((LANES - gate_o.shape[0], sub), F32)], axis=0)
        gate_ref[rows, :] = gate_pad.T

        gu = _dot(h_hi, wsg_ref[...])
        hs = gu.shape[1] // 2
        act = (jax.nn.silu(gu[:, :hs]) * gu[:, hs:]).astype(BF16)
        base_ref[rows, :] = x_new + g2_ref[...] * _dot(act, wsd_ref[...])

    _pipelined_blocks(tm, sub, residual, route_and_share)


def _post(att, gmn, x2, mod4, oa, w_out, nffn, wr_t, rb, w_sg, w_sd, *, tm, tokens_per_batch, tok0):
    t, att_w = att.shape
    d = x2.shape[1]
    n_exp = wr_t.shape[0]
    gm_w = gmn.shape[1]
    tpb = tokens_per_batch // tm
    blk0 = tok0 // tm
    const = lambda shape: pl.BlockSpec(shape, lambda i: (0,) * len(shape))
    mrow = lambda j: pl.BlockSpec((None, None, 1, d), lambda i: ((i + blk0) // tpb, j, 0, 0))
    by_choice = jax.ShapeDtypeStruct((CHOICE_ROWS, t), jnp.int32)
    return pl.pallas_call(
        _post_kernel,
        out_shape=(jax.ShapeDtypeStruct((t, d), F32), jax.ShapeDtypeStruct((t, d // 2), jnp.int32),
                   by_choice, jax.ShapeDtypeStruct((t, LANES), F32), by_choice,
                   jax.ShapeDtypeStruct((n_exp, LANES), F32)),
        grid=(t // tm,),
        in_specs=[pl.BlockSpec((tm, att_w), lambda i: (i, 0)),
                  pl.BlockSpec((tm, gm_w), lambda i: (i + blk0, 0)),
                  pl.BlockSpec((tm, d), lambda i: (i + blk0, 0)),
                  mrow(2), mrow(3), mrow(4), mrow(5),
                  const((1, att_w)), const(w_out.shape), const((1, d)),
                  const(wr_t.shape), const(rb.shape),
                  const(w_sg.shape), const(w_sd.shape)],
        out_specs=(pl.BlockSpec((tm, d), lambda i: (i, 0)), pl.BlockSpec((tm, d // 2), lambda i: (i, 0)),
                   pl.BlockSpec((CHOICE_ROWS, tm), lambda i: (0, i)), pl.BlockSpec((tm, LANES), lambda i: (i, 0)),
                   pl.BlockSpec((CHOICE_ROWS, tm), lambda i: (0, i)), pl.BlockSpec((n_exp, LANES), lambda i: (0, 0))),
        scratch_shapes=[pltpu.VMEM((n_exp, LANES), F32)],
        compiler_params=_cparams("arbitrary"),
        name="post",
    )(att, gmn, x2, mod4, mod4, mod4, mod4, oa, w_out, nffn, wr_t, rb, w_sg, w_sd)


def _sc_mesh_info():
    info = plsc.get_sparse_core_info()
    mesh = plsc.VectorSubcoreMesh(core_axis_name="c", subcore_axis_name="s")
    return mesh, info.num_cores, info.num_subcores, info.num_lanes


def _sc_params():
    cp = pltpu.CompilerParams()
    if "needs_layout_passes" in pltpu.CompilerParams.__dataclass_fields__:
        cp = dataclasses.replace(cp, needs_layout_passes=False)
    return cp


def _dispatch(h2p, idx_t, rank_t, offs, *, n_slots):
    t, dw = h2p.shape
    mesh, n_cores, n_sub, n_lanes = _sc_mesh_info()
    per_worker = t // (n_cores * n_sub)
    w = SC_WINDOW
    n_win = per_worker // w
    assert per_worker % w == 0 and w % n_lanes == 0 and n_win % 2 == 0

    @functools.partial(
        pl.kernel, mesh=mesh,
        out_type=(jax.ShapeDtypeStruct((n_slots, dw), jnp.int32), jax.ShapeDtypeStruct((CHOICE_ROWS, t), jnp.int32)),
        scratch_types=[pltpu.VMEM(offs.shape, jnp.int32)] + [pltpu.VMEM((TOP_K, w), jnp.int32)] * 5
        + [pltpu.VMEM((w, dw), jnp.int32)] * 2 + [pltpu.SemaphoreType.DMA] * 2,
        compiler_params=_sc_params(),
    )
    def run(h_hbm, idx_hbm, rank_hbm, offs_hbm, xs_hbm, pos_hbm, offs_v, idx_a, idx_b, rank_a, rank_b, pos_v,
            rows_a, rows_b, sem_in, sem_out):
        base = (lax.axis_index("s") * n_cores + lax.axis_index("c")) * per_worker
        last = base + per_worker - w
        sets = ((rows_a, idx_a, rank_a), (rows_b, idx_b, rank_b))
        pltpu.sync_copy(offs_hbm, offs_v)

        def loads(t0, slot):
            rows_v, idx_v, rank_v = sets[slot]
            cps = [pltpu.make_async_copy(h_hbm.at[pl.ds(t0, w)], rows_v, sem_in)]
            for k in range(TOP_K):
                cps.append(pltpu.make_async_copy(idx_hbm.at[k, pl.ds(t0, w)], idx_v.at[k], sem_in))
                cps.append(pltpu.make_async_copy(rank_hbm.at[k, pl.ds(t0, w)], rank_v.at[k], sem_in))
            return cps

        for cp in loads(base, 0):
            cp.start()

        @pl.loop(0, n_win // 2)
        def _(i):
            for slot in range(2):
                t0 = base + (2 * i + slot) * w
                rows_v, idx_v, rank_v = sets[slot]
                for cp in loads(t0, slot):
                    cp.wait()
                for cp in loads(jnp.minimum(t0 + w, last), 1 - slot):
                    cp.start()
                for k in range(TOP_K):
                    for j in range(0, w, n_lanes):
                        group_start = plsc.load_gather(offs_v, [idx_v[k, pl.ds(j, n_lanes)]])
                        pos_v[k, pl.ds(j, n_lanes)] = group_start + rank_v[k, pl.ds(j, n_lanes)]
                copies = [pltpu.async_copy(rows_v, xs_hbm.at[pos_v.at[k]], sem_out) for k in range(TOP_K)]
                copies += [pltpu.async_copy(pos_v.at[k], pos_hbm.at[k, pl.ds(t0, w)], sem_out) for k in range(TOP_K)]
                for cp in copies:
                    cp.wait()

        for cp in loads(last, 0):
            cp.wait()

    return run(h2p, idx_t, rank_t, offs)


def _collect(ys, pos_t, *, tok_lo, n_tok):
    dw = ys.shape[1]
    mesh, n_cores, n_sub, _ = _sc_mesh_info()
    per_worker = n_tok // (n_cores * n_sub)
    w = SC_WINDOW
    n_win = per_worker // w
    assert per_worker % w == 0 and n_win % 2 == 0

    @functools.partial(
        pl.kernel, mesh=mesh,
        out_type=jax.ShapeDtypeStruct((TOP_K * n_tok, dw), jnp.int32),
        scratch_types=[pltpu.VMEM((TOP_K, w), jnp.int32)] * 2 + [pltpu.VMEM((w, dw), jnp.int32)] * 2
        + [pltpu.SemaphoreType.DMA] * 3,
        compiler_params=_sc_params(),
    )
    def run(ys_hbm, pos_hbm, r_hbm, pos_a, pos_b, rows_a, rows_b, sem_p, sem_g, sem_w):
        base = (lax.axis_index("s") * n_cores + lax.axis_index("c")) * per_worker
        bufs = (rows_a, rows_b)
        pos = (pos_a, pos_b)
        steps = [(s, k) for s in range(2) for k in range(TOP_K)]

        @pl.loop(0, n_win // 2)
        def _(i):
            t0 = base + 2 * i * w
            slots = [pltpu.async_copy(pos_hbm.at[k, pl.ds(tok_lo + t0 + s * w, w)], pos[s].at[k], sem_p)
                     for s, k in steps]
            for cp in slots:
                cp.wait()

            def gather(n):
                s, k = steps[n]
                return pltpu.async_copy(ys_hbm.at[pos[s].at[k]], bufs[n % 2], sem_g)

            gathers = [None] * len(steps)
            writes = [None] * len(steps)
            gathers[0] = gather(0)
            for n, (s, k) in enumerate(steps):
                gathers[n].wait()
                if n + 1 < len(steps):
                    if n >= 1:
                        writes[n - 1].wait()
                    gathers[n + 1] = gather(n + 1)
                writes[n] = pltpu.async_copy(bufs[n % 2], r_hbm.at[pl.ds(k * n_tok + t0 + s * w, w)], sem_w)
            writes[-2].wait()
            writes[-1].wait()

    return run(ys, pos_t)


def _experts_kernel(t0_ref, t1_ref, xs_hbm, wgu_ref, wdn_ref, ys_hbm, xbuf, ybuf, wgu_b, wdn_b, sem_in, sem_out):
    e = pl.program_id(0)
    n_exp = pl.num_programs(0)
    n_in, tr, half = xbuf.shape
    n_out = ybuf.shape[0]
    n_tiles = t1_ref[n_exp - 1]

    def fetch(t):
        slot = t & (n_in - 1)
        return pltpu.make_async_copy(xs_hbm.at[pl.ds(t * tr, tr)], xbuf.at[slot], sem_in.at[slot])

    def flush(t):
        slot = t & (n_out - 1)
        return pltpu.make_async_copy(ybuf.at[slot], ys_hbm.at[pl.ds(t * tr, tr)], sem_out.at[slot])

    @pl.when(e == 0)
    def _():
        for t in range(n_in - 1):
            @pl.when(t < n_tiles)
            def _():
                fetch(t).start()

    wgu_b[...] = wgu_ref[...].astype(BF16)
    wdn_b[...] = wdn_ref[...].astype(BF16)

    def tile(t, carry):
        fetch(t).wait()

        @pl.when(t + n_in - 1 < n_tiles)
        def _():
            fetch(t + n_in - 1).start()

        @pl.when(t >= n_out)
        def _():
            flush(t - n_out).wait()

        x_slot = xbuf.at[t & (n_in - 1)]
        y_slot = ybuf.at[t & (n_out - 1)]

        def up(rows):
            lo, hi = _unpack_rows(x_slot[rows, :])
            gu = _dot(lo.astype(BF16), wgu_b[:half, :]) + _dot(hi.astype(BF16), wgu_b[half:, :])
            hh = gu.shape[1] // 2
            return (jax.nn.silu(gu[:, :hh]) * gu[:, hh:]).astype(BF16)

        def down(rows, act):
            y_slot[rows, :] = _pack_rows(_dot(act, wdn_b[...]))

        _pipelined_blocks(tr, min(tr, ROW_BLOCK // 2), up, down)
        flush(t).start()
        return carry

    lax.fori_loop(t0_ref[e], t1_ref[e], tile, 0)

    @pl.when(e == n_exp - 1)
    def _():
        for back in range(n_out, 0, -1):
            @pl.when(n_tiles >= back)
            def _():
                flush(n_tiles - back).wait()


def _experts(tile_lo, tile_hi, xs, w_gu, w_dn, after, *, tr):
    n_exp, d, h2w = w_gu.shape
    hh = w_dn.shape[1]
    in_specs = [pl.BlockSpec(memory_space=pl.ANY),
                pl.BlockSpec((None, d, h2w), lambda e, lo, hi: (e, 0, 0)),
                pl.BlockSpec((None, hh, d), lambda e, lo, hi: (e, 0, 0))]
    operands = [xs, w_gu, w_dn]
    n_read = 2 + len(operands)
    if after is not None:
        _unread_operand(in_specs, operands, after)
    n_in = 2 + len(operands)
    return pl.pallas_call(
        lambda *refs: _experts_kernel(*refs[:n_read], *refs[n_in:]),
        out_shape=jax.ShapeDtypeStruct(xs.shape, jnp.int32),
        grid_spec=pltpu.PrefetchScalarGridSpec(
            num_scalar_prefetch=2,
            grid=(n_exp,),
            in_specs=in_specs,
            out_specs=pl.BlockSpec(memory_space=pl.ANY),
            scratch_shapes=[pltpu.VMEM((EXPERT_IN_SLOTS, tr, d // 2), jnp.int32),
                            pltpu.VMEM((EXPERT_OUT_SLOTS, tr, d // 2), jnp.int32),
                            pltpu.VMEM((d, h2w), BF16), pltpu.VMEM((hh, d), BF16),
                            pltpu.SemaphoreType.DMA((EXPERT_IN_SLOTS,)), pltpu.SemaphoreType.DMA((EXPERT_OUT_SLOTS,))]),
        compiler_params=_cparams("arbitrary"),
        name="experts",
    )(tile_lo, tile_hi, *operands)


def _combine_kernel(base_ref, g2_ref, gate_ref, *refs):
    r_refs, o_ref = refs[:TOP_K], refs[TOP_K]
    gate = gate_ref[...]
    half = base_ref.shape[1] // 2
    acc_lo = jnp.zeros((base_ref.shape[0], half), F32)
    acc_hi = jnp.zeros((base_ref.shape[0], half), F32)
    for k in range(TOP_K):
        lo, hi = _unpack_rows(r_refs[k][...])
        g = gate[:, k:k + 1]
        acc_lo = acc_lo + g * lo
        acc_hi = acc_hi + g * hi
    g2 = g2_ref[...]
    o_ref[:, :half] = base_ref[:, :half] + g2[:, :half] * acc_lo
    o_ref[:, half:] = base_ref[:, half:] + g2[:, half:] * acc_hi


def _combine(base, mod4, gate, r, out_so_far, after, *, tm, tokens_per_batch, tok0, part_lo, total_tokens):
    d = base.shape[1]
    t = r.shape[0] // TOP_K
    tpb = tokens_per_batch // tm
    nblk = t // tm
    pblk0 = part_lo // tm
    blk0 = (tok0 + part_lo) // tm
    r_spec = lambda k: pl.BlockSpec((tm, d // 2), lambda i: (k * nblk + i, 0))
    in_specs = [pl.BlockSpec((tm, d), lambda i: (i + pblk0, 0)),
                pl.BlockSpec((None, None, 1, d), lambda i: ((i + blk0) // tpb, 5, 0, 0)),
                pl.BlockSpec((tm, LANES), lambda i: (i + pblk0, 0))] + [r_spec(k) for k in range(TOP_K)]
    operands = [base, mod4, gate] + [r] * TOP_K
    n_read = len(operands)
    aliases = {}
    if out_so_far is not None:
        aliases = {_unread_operand(in_specs, operands, out_so_far): 0}
    if after is not None:
        _unread_operand(in_specs, operands, after)
    n_in = len(operands)
    return pl.pallas_call(
        lambda *refs: _combine_kernel(*refs[:n_read], *refs[n_in:]),
        out_shape=jax.ShapeDtypeStruct((total_tokens, d), F32),
        grid=(nblk,),
        in_specs=in_specs,
        out_specs=pl.BlockSpec((tm, d), lambda i: (i + blk0, 0)),
        input_output_aliases=aliases,
        compiler_params=_cparams("arbitrary"),
        name="combine",
    )(*operands)


def _rope_tables(length, head_dim):
    rows = length // GRID_W
    pairs_axis = head_dim // 4
    r = jnp.repeat(jnp.arange(rows, dtype=F32), GRID_W)
    col = jnp.tile(jnp.arange(GRID_W, dtype=F32), rows)
    inv = ROPE_THETA ** (-jnp.arange(pairs_axis, dtype=F32) / pairs_axis)
    ang = jnp.concatenate([r[:, None] * inv, col[:, None] * inv], axis=-1)
    cos, sin = jnp.cos(ang), jnp.sin(ang)
    cos_h = jnp.concatenate([cos, cos], axis=-1)
    sin_h = jnp.concatenate([-sin, sin], axis=-1)
    reps = LANES // head_dim
    return jnp.tile(cos_h, (1, reps)), jnp.tile(sin_h, (1, reps))


def _part_batches(b):
    if b < 2:
        return [b]
    lead = max(1, min(b - 1, round(LEAD_PART_SHARE * b)))
    return [lead, b - lead]


def kernel(x, c, ctx, c_ctx, w_ada, b_ada, norm_mix, w_in, q_norm, k_norm, gm_norm, w_spatial, b_spatial,
           out_norm_attn, out_norm_gm, w_out, norm_ffn, w_router, router_bias, w_exp_gu, w_exp_down,
           w_sh_gu, w_sh_down):
    assert w_ada.shape[0] == 1, "single-layer kernel"
    b, length, d = x.shape
    head_dim = q_norm.shape[-1]
    att_w = out_norm_attn.shape[-1]
    gm_w = out_norm_gm.shape[-1]
    gm_groups = w_spatial.shape[1]
    gm_hd = gm_w // gm_groups
    n_exp = w_router.shape[-1]
    t = b * length
    assert head_dim == LANES // 2 and gm_hd == LANES // 2 and w_spatial.shape[-1] == CHUNK
    assert n_exp <= LANES and att_w % MXU_EDGE == 0 and gm_w % MXU_EDGE == 0

    mod_rows = 16
    c_all = jnp.zeros((mod_rows, d), F32).at[:b].set(c).at[b].set(c_ctx)
    mod = _ada(c_all, w_ada[0], b_ada)
    mod4 = mod.reshape(mod_rows, N_MOD, 1, d)

    w_in_b = w_in[0].astype(BF16)
    bd = (jnp.arange(MXU_EDGE)[:, None] // head_dim == jnp.arange(MXU_EDGE)[None, :] // head_dim)
    bd = (bd.astype(F32) / head_dim).astype(BF16)
    qg = jnp.tile(q_norm[0] * (head_dim ** -0.5 * LOG2_E), att_w // head_dim)[None, :]
    kg = jnp.tile(k_norm[0], LANES // head_dim)[None, :]
    gmg = jnp.tile(gm_norm[0], gm_groups)[None, :]
    ws = w_spatial[0].astype(BF16).reshape(gm_groups // 2, 2 * CHUNK, CHUNK)
    bs = jnp.repeat(b_spatial[0].T, gm_hd, axis=1)
    cos_t, sin_t = _rope_tables(length, head_dim)

    k_c, v_c = _ctx_kv(ctx, mod4, b, norm_mix, w_in_b[:, att_w:att_w + 2 * LANES], kg, bd)
    q, k_l, v_l, gmn = _inproj(x, mod4, norm_mix, w_in_b, cos_t, sin_t, qg, kg, gmg, bd, ws, bs,
                               out_norm_gm, att_w=att_w, gm_w=gm_w, tm=min(1024, length))
    wr_t = w_router[0].T.astype(BF16)
    rb = jnp.tile(router_bias[0][:, None], (1, LANES))
    w_out_b, w_sg_b, w_sd_b = w_out[0].astype(BF16), w_sh_gu[0].astype(BF16), w_sh_down[0].astype(BF16)
    gmn2, x2 = gmn.reshape(t, gm_w), x.reshape(t, d)
    tm_post = min(1024, length)
    tr = 512

    part_nb = _part_batches(b)
    n_parts = len(part_nb)
    part_b0 = [sum(part_nb[:p]) for p in range(n_parts)]
    state = [dict() for _ in range(n_parts)]

    last_tc = [None]

    def halves(part):
        nb = part_nb[part]
        return (nb // 2, nb - nb // 2) if nb >= 2 else (nb,)

    def attend(part, half):
        if half >= len(halves(part)):
            return
        st = state[part]
        slot0 = sum(halves(part)[:half])
        after = None if last_tc[0] is st.get("att") else last_tc[0]
        st["att"] = _attention(q, k_c, k_l, v_c, v_l, st.get("att"), after, tq=min(256, length),
                               b0=part_b0[part] + slot0, nb=halves(part)[half], slot0=slot0, slots=part_nb[part])
        last_tc[0] = st["att"]

    def route(part):
        st = state[part]
        tp = part_nb[part] * length
        nt = tp * TOP_K // tr + n_exp
        base, h2p, idx_t, gate, rank_t, cnt = _post(
            st["att"].reshape(tp, att_w), gmn2, x2, mod4, out_norm_attn, w_out_b, norm_ffn, wr_t, rb, w_sg_b, w_sd_b,
            tm=tm_post, tokens_per_batch=length, tok0=part_b0[part] * length)
        counts = cnt[:, 0].astype(jnp.int32)
        padded = (counts + tr - 1) // tr * tr
        ends = jnp.cumsum(padded)
        offs = ends - padded
        offs_pad = jnp.zeros((LANES,), jnp.int32).at[:n_exp].set(offs)
        xs, pos_t = _dispatch(h2p, idx_t, rank_t, offs_pad, n_slots=nt * tr)
        st.update(base=base, gate=gate, xs=xs, pos_t=pos_t, tile_lo=offs // tr, tile_hi=ends // tr)
        last_tc[0] = base

    def run_experts(part):
        st = state[part]
        ys = _experts(st["tile_lo"], st["tile_hi"], st["xs"], w_exp_gu[0], w_exp_down[0], last_tc[0], tr=tr)
        _, n_cores, n_sub, _ = _sc_mesh_info()
        pair_tokens = 2 * SC_WINDOW * n_cores * n_sub
        n_chunks = part_nb[part] if part == n_parts - 1 else (2 if part_nb[part] * length % (2 * pair_tokens) == 0 else 1)
        chunk = part_nb[part] * length // n_chunks
        st["r"] = [(c * chunk, _collect(ys, st["pos_t"], tok_lo=c * chunk, n_tok=chunk)) for c in range(n_chunks)]
        last_tc[0] = ys

    result = [None]

    def finish(part, chunks):
        st = state[part]
        for part_lo, r in st["r"][chunks]:
            after = None if last_tc[0] is result[0] or last_tc[0] is st["base"] else last_tc[0]
            result[0] = _combine(st["base"], mod4, st["gate"], r, result[0], after, tm=min(512, length),
                                 tokens_per_batch=length, tok0=part_b0[part] * length, part_lo=part_lo,
                                 total_tokens=t)
            last_tc[0] = result[0]

    stages = [(0, lambda p: attend(p, 0)), (1, lambda p: attend(p, 1)), (2, route), (4, run_experts),
              (7, lambda p: finish(p, slice(0, 1))), (7.5, lambda p: finish(p, slice(1, None)))]
    plan = sorted((when + 3 * part, part, n, fn) for part in range(n_parts) for n, (when, fn) in enumerate(stages))
    for _, part, _, fn in plan:
        fn(part)
    return result[0].reshape(b, length, d)
```
